```python
import functools
import jax, jax.numpy as jnp
from jax import lax
import numpy as np

D_MODEL = 1024
BATCH = 16
SEQ = 2048
DEPTH = 4

GRID_W = 64
CTX_LEN = 256
N_MIXERS = 2
N_A = (DEPTH + 1) // 2
N_B = DEPTH // 2
N_MOD = 9
D_FF = 2816
CONV_DIM = D_MODEL
CONV_W = 3
MLA_HEADS = 8
QK_NOPE = 128
QK_ROPE = 64
QK_HEAD = QK_NOPE + QK_ROPE
V_HEAD = 128
Q_LORA = 256
KV_LORA = 128
ROPE_BASE = 10000.0
QK_SCALE = QK_HEAD ** -0.5
Q_BLOCK = 128
EPS = 1e-6

kernel_name = "hybrid_shortconv_mla_macaron_dit"


def rms_norm(x, g):
    xf = x.astype(jnp.float32)
    y = xf * lax.rsqrt(jnp.mean(xf * xf, axis=-1, keepdims=True) + EPS)
    return (y * g.astype(jnp.float32)).astype(x.dtype)


def adaln_chunks(cond, w_mod, b_mod):
    m = jax.nn.silu(cond) @ w_mod + b_mod
    return jnp.split(m[:, None, :], N_MOD, axis=-1)


def pre(h, g, shift, scale):
    return rms_norm(h, g) * (1 + scale) + shift


def swiglu(h, w1, w3, w2):
    return (jax.nn.silu(h @ w1) * (h @ w3)) @ w2


def conv3_centred(u, w):
    return lax.conv_general_dilated(
        u, w[:, None, :].astype(u.dtype), window_strides=(1,), padding=((1, 1),),
        dimension_numbers=("NWC", "WIO", "NWC"), feature_group_count=u.shape[-1])


def short_conv_mixer(h, w_in, conv_w, w_out):
    b_gate, c_gate, u = jnp.split(h @ w_in, 3, axis=-1)
    return (b_gate * conv3_centred(c_gate * u, conv_w)) @ w_out


def axial_rope_tables(n):
    rows = n // GRID_W
    r = jnp.broadcast_to(jnp.arange(rows)[:, None], (rows, GRID_W)).reshape(n).astype(jnp.float32)
    col = jnp.broadcast_to(jnp.arange(GRID_W)[None, :], (rows, GRID_W)).reshape(n).astype(jnp.float32)
    n_freq = QK_ROPE // 4
    inv = ROPE_BASE ** (-jnp.arange(n_freq, dtype=jnp.float32) / n_freq)
    ang = jnp.stack([r[:, None] * inv, col[:, None] * inv], axis=1)
    return jnp.cos(ang), jnp.sin(ang)


def apply_axial_rope(t, cos, sin):
    ts = t.reshape(t.shape[:-1] + (2, 2, QK_ROPE // 4))
    x1, x2 = ts[..., 0, :], ts[..., 1, :]
    cos = cos.astype(t.dtype)
    sin = sin.astype(t.dtype)
    y = jnp.stack([x1 * cos - x2 * sin, x1 * sin + x2 * cos], axis=-2)
    return y.reshape(t.shape)


def rope_tail(t, cos, sin):
    return jnp.concatenate([t[..., :QK_NOPE], apply_axial_rope(t[..., QK_NOPE:], cos, sin)], axis=-1)


def mla_down(h, w_a):
    return jnp.split(h @ w_a, [Q_LORA, Q_LORA + KV_LORA], axis=-1)


def mla_queries(cq, g_qa, w_uq, g_q):
    b, n, _ = cq.shape
    q = (rms_norm(cq, g_qa) @ w_uq).reshape(b, n, MLA_HEADS, QK_HEAD)
    return rms_norm(q, g_q).transpose(0, 2, 1, 3)


def mla_keys_values(ckv, k_rope, g_kva, w_ukv, g_k):
    b, n, _ = ckv.shape
    kv = (rms_norm(ckv, g_kva) @ w_ukv).reshape(b, n, MLA_HEADS, QK_NOPE + V_HEAD)
    k_nope, v = jnp.split(kv, [QK_NOPE], axis=-1)
    k_r = jnp.broadcast_to(k_rope[:, :, None, :], (b, n, MLA_HEADS, QK_ROPE))
    k = rms_norm(jnp.concatenate([k_nope, k_r], axis=-1), g_k)
    return k.transpose(0, 2, 1, 3), v.transpose(0, 2, 1, 3)


def softmax_attend(q, k, v):
    s = jnp.einsum("bhqd,bhkd->bhqk", q, k).astype(jnp.float32) * QK_SCALE
    p = jax.nn.softmax(s, axis=-1).astype(v.dtype)
    return jnp.einsum("bhqk,bhkd->bhqd", p, v)


def merge_heads(o):
    b, h, n, d = o.shape
    return o.transpose(0, 2, 1, 3).reshape(b, n, h * d)


def latent_attention(q, k_all, v_all):
    b, h, n, dq = q.shape
    nb = n // Q_BLOCK
    qb = q.reshape(b, h, nb, Q_BLOCK, dq).transpose(2, 0, 1, 3, 4)
    o = lax.map(lambda qblk: softmax_attend(qblk, k_all, v_all), qb)
    return o.transpose(1, 0, 3, 2, 4).reshape(b, n, h * V_HEAD)


def _fwd_setup_inputs(seed: int = 0) -> dict:
    key = jax.random.key(seed)
    ks = jax.random.split(key, 24)
    f32 = jnp.float32

    def nrm(k, shape, scale):
        return jax.random.normal(k, shape, f32) * scale

    def gain(k, shape):
        return 1.0 + 0.1 * jax.random.normal(k, shape, f32)

    D = D_MODEL
    return {
        "x": nrm(ks[0], (BATCH, SEQ, D), 1.0),
        "c": nrm(ks[1], (BATCH, D), 1.0),
        "ctx": nrm(ks[2], (BATCH, CTX_LEN, D), 1.0),
        "c_ctx": nrm(ks[3], (D,), 1.0),
        "w_mod": nrm(ks[4], (DEPTH, D, N_MOD * D), 0.5 * D ** -0.5),
        "b_mod": nrm(ks[5], (DEPTH, N_MOD * D), 0.02),
        "g_norm": gain(ks[6], (DEPTH, 3, D)),
        "ffn_w1": nrm(ks[7], (DEPTH, 2, D, D_FF), D ** -0.5),
        "ffn_w3": nrm(ks[8], (DEPTH, 2, D, D_FF), D ** -0.5),
        "ffn_w2": nrm(ks[9], (DEPTH, 2, D_FF, D), D_FF ** -0.5),
        "sc_w_in": nrm(ks[10], (N_A, D, 3 * CONV_DIM), D ** -0.5),
        "sc_conv": nrm(ks[11], (N_A, CONV_W, CONV_DIM), CONV_W ** -0.5),
        "sc_w_out": nrm(ks[12], (N_A, CONV_DIM, D), CONV_DIM ** -0.5),
        "mla_w_a": nrm(ks[13], (N_B, D, Q_LORA + KV_LORA + QK_ROPE), D ** -0.5),
        "mla_g_qa": gain(ks[14], (N_B, Q_LORA)),
        "mla_w_uq": nrm(ks[15], (N_B, Q_LORA, MLA_HEADS * QK_HEAD), Q_LORA ** -0.5),
        "mla_g_kva": gain(ks[16], (N_B, KV_LORA)),
        "mla_w_ukv": nrm(ks[17], (N_B, KV_LORA, MLA_HEADS * (QK_NOPE + V_HEAD)), KV_LORA ** -0.5),
        "mla_g_q": gain(ks[18], (N_B, QK_HEAD)),
        "mla_g_k": gain(ks[19], (N_B, QK_HEAD)),
        "mla_w_o": nrm(ks[20], (N_B, MLA_HEADS * V_HEAD, D), (MLA_HEADS * V_HEAD) ** -0.5),
    }


def _fwd_reference(x, c, ctx, c_ctx, w_mod, b_mod, g_norm, ffn_w1, ffn_w3, ffn_w2,
              sc_w_in, sc_conv, sc_w_out, mla_w_a, mla_g_qa, mla_w_uq, mla_g_kva,
              mla_w_ukv, mla_g_q, mla_g_k, mla_w_o):
    n = x.shape[1]
    cos, sin = axial_rope_tables(n)
    h_x, h_c = x, ctx
    for i in range(DEPTH):
        kind, j = i % N_MIXERS, i // N_MIXERS
        last = i == DEPTH - 1
        run_ctx_in = (not last) or kind == 1
        run_ctx_out = not last

        mx = adaln_chunks(c, w_mod[i], b_mod[i])
        mc = adaln_chunks(c_ctx[None], w_mod[i], b_mod[i])
        ffn1 = functools.partial(swiglu, w1=ffn_w1[i, 0], w3=ffn_w3[i, 0], w2=ffn_w2[i, 0])
        ffn2 = functools.partial(swiglu, w1=ffn_w1[i, 1], w3=ffn_w3[i, 1], w2=ffn_w2[i, 1])

        h_x = h_x + 0.5 * mx[2] * ffn1(pre(h_x, g_norm[i, 0], mx[0], mx[1]))
        if run_ctx_in:
            h_c = h_c + 0.5 * mc[2] * ffn1(pre(h_c, g_norm[i, 0], mc[0], mc[1]))

        nx = pre(h_x, g_norm[i, 1], mx[3], mx[4])
        if kind == 0:
            ox = short_conv_mixer(nx, sc_w_in[j], sc_conv[j], sc_w_out[j])
            if run_ctx_out:
                nc = pre(h_c, g_norm[i, 1], mc[3], mc[4])
                oc = short_conv_mixer(nc, sc_w_in[j], sc_conv[j], sc_w_out[j])
        else:
            nc = pre(h_c, g_norm[i, 1], mc[3], mc[4])
            cq_c, ckv_c, kr_c = mla_down(nc, mla_w_a[j])
            k_c, v_c = mla_keys_values(ckv_c, kr_c, mla_g_kva[j], mla_w_ukv[j], mla_g_k[j])
            cq_x, ckv_x, kr_x = mla_down(nx, mla_w_a[j])
            k_x, v_x = mla_keys_values(ckv_x, kr_x, mla_g_kva[j], mla_w_ukv[j], mla_g_k[j])
            k_x = rope_tail(k_x, cos, sin)
            q_x = rope_tail(mla_queries(cq_x, mla_g_qa[j], mla_w_uq[j], mla_g_q[j]), cos, sin)
            k_all = jnp.concatenate([k_c, k_x], axis=2)
            v_all = jnp.concatenate([v_c, v_x], axis=2)
            ox = latent_attention(q_x, k_all, v_all) @ mla_w_o[j]
            if run_ctx_out:
                q_c = mla_queries(cq_c, mla_g_qa[j], mla_w_uq[j], mla_g_q[j])
                oc = merge_heads(softmax_attend(q_c, k_c, v_c)) @ mla_w_o[j]
        h_x = h_x + mx[5] * ox
        if run_ctx_out:
            h_c = h_c + mc[5] * oc

        h_x = h_x + 0.5 * mx[8] * ffn2(pre(h_x, g_norm[i, 2], mx[6], mx[7]))
        if run_ctx_out:
            h_c = h_c + 0.5 * mc[8] * ffn2(pre(h_c, g_norm[i, 2], mc[6], mc[7]))
    return h_x


import jax as _jax
import jax.numpy as _jnp

TWIN_FORMAT = 'train_step'
FWD_PARAMS = ['x', 'c', 'ctx', 'c_ctx', 'w_mod', 'b_mod', 'g_norm', 'ffn_w1', 'ffn_w3', 'ffn_w2', 'sc_w_in', 'sc_conv', 'sc_w_out', 'mla_w_a', 'mla_g_qa', 'mla_w_uq', 'mla_g_kva', 'mla_w_ukv', 'mla_g_q', 'mla_g_k', 'mla_w_o']
TWIN_WEIGHTS = ['c_ctx', 'w_mod', 'b_mod', 'g_norm', 'ffn_w1', 'ffn_w3', 'ffn_w2', 'sc_w_in', 'sc_conv', 'sc_w_out', 'mla_w_a', 'mla_g_qa', 'mla_w_uq', 'mla_g_kva', 'mla_w_ukv', 'mla_g_q', 'mla_g_k', 'mla_w_o']
TWIN_DIFF_INPUT = 'x'
TWIN_INPUTS = ['x', 'c', 'ctx', 'c_ctx', 'w_mod', 'b_mod', 'g_norm', 'ffn_w1', 'ffn_w3', 'ffn_w2', 'sc_w_in', 'sc_conv', 'sc_w_out', 'mla_w_a', 'mla_g_qa', 'mla_w_uq', 'mla_g_kva', 'mla_w_ukv', 'mla_g_q', 'mla_g_k', 'mla_w_o', 'loss_target', 'm_c_ctx', 'm_w_mod', 'm_b_mod', 'm_g_norm', 'm_ffn_w1', 'm_ffn_w3', 'm_ffn_w2', 'm_sc_w_in', 'm_sc_conv', 'm_sc_w_out', 'm_mla_w_a', 'm_mla_g_qa', 'm_mla_w_uq', 'm_mla_g_kva', 'm_mla_w_ukv', 'm_mla_g_q', 'm_mla_g_k', 'm_mla_w_o', 'v_c_ctx', 'v_w_mod', 'v_b_mod', 'v_g_norm', 'v_ffn_w1', 'v_ffn_w3', 'v_ffn_w2', 'v_sc_w_in', 'v_sc_conv', 'v_sc_w_out', 'v_mla_w_a', 'v_mla_g_qa', 'v_mla_w_uq', 'v_mla_g_kva', 'v_mla_w_ukv', 'v_mla_g_q', 'v_mla_g_k', 'v_mla_w_o']
TWIN_OUTPUTS = ['loss', 'grad_x', 'grad_c_ctx', 'grad_w_mod', 'grad_b_mod', 'grad_g_norm', 'grad_ffn_w1', 'grad_ffn_w3', 'grad_ffn_w2', 'grad_sc_w_in', 'grad_sc_conv', 'grad_sc_w_out', 'grad_mla_w_a', 'grad_mla_g_qa', 'grad_mla_w_uq', 'grad_mla_g_kva', 'grad_mla_w_ukv', 'grad_mla_g_q', 'grad_mla_g_k', 'grad_mla_w_o', 'delta_c_ctx', 'delta_w_mod', 'delta_b_mod', 'delta_g_norm', 'delta_ffn_w1', 'delta_ffn_w3', 'delta_ffn_w2', 'delta_sc_w_in', 'delta_sc_conv', 'delta_sc_w_out', 'delta_mla_w_a', 'delta_mla_g_qa', 'delta_mla_w_uq', 'delta_mla_g_kva', 'delta_mla_w_ukv', 'delta_mla_g_q', 'delta_mla_g_k', 'delta_mla_w_o', 'new_m_c_ctx', 'new_m_w_mod', 'new_m_b_mod', 'new_m_g_norm', 'new_m_ffn_w1', 'new_m_ffn_w3', 'new_m_ffn_w2', 'new_m_sc_w_in', 'new_m_sc_conv', 'new_m_sc_w_out', 'new_m_mla_w_a', 'new_m_mla_g_qa', 'new_m_mla_w_uq', 'new_m_mla_g_kva', 'new_m_mla_w_ukv', 'new_m_mla_g_q', 'new_m_mla_g_k', 'new_m_mla_w_o', 'new_v_c_ctx', 'new_v_w_mod', 'new_v_b_mod', 'new_v_g_norm', 'new_v_ffn_w1', 'new_v_ffn_w3', 'new_v_ffn_w2', 'new_v_sc_w_in', 'new_v_sc_conv', 'new_v_sc_w_out', 'new_v_mla_w_a', 'new_v_mla_g_qa', 'new_v_mla_w_uq', 'new_v_mla_g_kva', 'new_v_mla_w_ukv', 'new_v_mla_g_q', 'new_v_mla_g_k', 'new_v_mla_w_o']
TWIN_LEAF_KINDS = {'loss': 'loss', 'grad_x': 'grad_x', 'grad_c_ctx': 'grad_w', 'grad_w_mod': 'grad_w', 'grad_b_mod': 'grad_w', 'grad_g_norm': 'grad_w', 'grad_ffn_w1': 'grad_w', 'grad_ffn_w3': 'grad_w', 'grad_ffn_w2': 'grad_w', 'grad_sc_w_in': 'grad_w', 'grad_sc_conv': 'grad_w', 'grad_sc_w_out': 'grad_w', 'grad_mla_w_a': 'grad_w', 'grad_mla_g_qa': 'grad_w', 'grad_mla_w_uq': 'grad_w', 'grad_mla_g_kva': 'grad_w', 'grad_mla_w_ukv': 'grad_w', 'grad_mla_g_q': 'grad_w', 'grad_mla_g_k': 'grad_w', 'grad_mla_w_o': 'grad_w', 'delta_c_ctx': 'delta_w', 'delta_w_mod': 'delta_w', 'delta_b_mod': 'delta_w', 'delta_g_norm': 'delta_w', 'delta_ffn_w1': 'delta_w', 'delta_ffn_w3': 'delta_w', 'delta_ffn_w2': 'delta_w', 'delta_sc_w_in': 'delta_w', 'delta_sc_conv': 'delta_w', 'delta_sc_w_out': 'delta_w', 'delta_mla_w_a': 'delta_w', 'delta_mla_g_qa': 'delta_w', 'delta_mla_w_uq': 'delta_w', 'delta_mla_g_kva': 'delta_w', 'delta_mla_w_ukv': 'delta_w', 'delta_mla_g_q': 'delta_w', 'delta_mla_g_k': 'delta_w', 'delta_mla_w_o': 'delta_w', 'new_m_c_ctx': 'new_m', 'new_m_w_mod': 'new_m', 'new_m_b_mod': 'new_m', 'new_m_g_norm': 'new_m', 'new_m_ffn_w1': 'new_m', 'new_m_ffn_w3': 'new_m', 'new_m_ffn_w2': 'new_m', 'new_m_sc_w_in': 'new_m', 'new_m_sc_conv': 'new_m', 'new_m_sc_w_out': 'new_m', 'new_m_mla_w_a': 'new_m', 'new_m_mla_g_qa': 'new_m', 'new_m_mla_w_uq': 'new_m', 'new_m_mla_g_kva': 'new_m', 'new_m_mla_w_ukv': 'new_m', 'new_m_mla_g_q': 'new_m', 'new_m_mla_g_k': 'new_m', 'new_m_mla_w_o': 'new_m', 'new_v_c_ctx': 'new_v', 'new_v_w_mod': 'new_v', 'new_v_b_mod': 'new_v', 'new_v_g_norm': 'new_v', 'new_v_ffn_w1': 'new_v', 'new_v_ffn_w3': 'new_v', 'new_v_ffn_w2': 'new_v', 'new_v_sc_w_in': 'new_v', 'new_v_sc_conv': 'new_v', 'new_v_sc_w_out': 'new_v', 'new_v_mla_w_a': 'new_v', 'new_v_mla_g_qa': 'new_v', 'new_v_mla_w_uq': 'new_v', 'new_v_mla_g_kva': 'new_v', 'new_v_mla_w_ukv': 'new_v', 'new_v_mla_g_q': 'new_v', 'new_v_mla_g_k': 'new_v', 'new_v_mla_w_o': 'new_v'}


def _forward(args):
    return _fwd_reference(*[args[k] for k in FWD_PARAMS])


def _output_shape():
    out = _jax.eval_shape(lambda: _forward(_fwd_setup_inputs(0)))
    return out.shape, out.dtype

N_MICROBATCH = 1
ADAM_LR = 0.001
ADAM_B1 = 0.9
ADAM_B2 = 0.999
ADAM_EPS = 1e-08
ADAM_WD = 0.01
ADAM_STEP = 10
PER_EXAMPLE_BATCH_AXIS = {'x': 0, 'c': 0, 'ctx': 0, 'loss_target': 0}
SHARED_INPUTS = []
_WEIGHT_DTYPES = {'c_ctx': _jnp.float32, 'w_mod': _jnp.float32, 'b_mod': _jnp.float32, 'g_norm': _jnp.float32, 'ffn_w1': _jnp.float32, 'ffn_w3': _jnp.float32, 'ffn_w2': _jnp.float32, 'sc_w_in': _jnp.float32, 'sc_conv': _jnp.float32, 'sc_w_out': _jnp.float32, 'mla_w_a': _jnp.float32, 'mla_g_qa': _jnp.float32, 'mla_w_uq': _jnp.float32, 'mla_g_kva': _jnp.float32, 'mla_w_ukv': _jnp.float32, 'mla_g_q': _jnp.float32, 'mla_g_k': _jnp.float32, 'mla_w_o': _jnp.float32}
MOMENT_SCALE = {'c_ctx': 8.169941e-02, 'w_mod': 1.846127e+00, 'b_mod': 4.260772e+00, 'g_norm': 6.505888e+00, 'ffn_w1': 3.134856e-02, 'ffn_w3': 3.255310e-02, 'ffn_w2': 5.369360e-02, 'sc_w_in': 2.953571e-01, 'sc_conv': 3.087238e+00, 'sc_w_out': 2.202221e-01, 'mla_w_a': 5.585041e-01, 'mla_g_qa': 3.420540e-02, 'mla_w_uq': 1.608961e-02, 'mla_g_kva': 2.008465e+00, 'mla_w_ukv': 1.984740e-01, 'mla_g_q': 7.338992e-02, 'mla_g_k': 7.054944e-02, 'mla_w_o': 2.617224e-01}


def _to_microbatches(a, axis):
    t = _jnp.moveaxis(a, axis, 0)
    t = t.reshape((N_MICROBATCH, t.shape[0] // N_MICROBATCH) + t.shape[1:])
    return _jnp.moveaxis(t, 1, axis + 1)


def setup_inputs(seed: int = 0) -> dict:
    inp = _fwd_setup_inputs(seed)
    key = _jax.random.fold_in(_jax.random.key(seed), 7919)
    shape, _ = _output_shape()
    out = dict(inp)
    out["loss_target"] = _jax.random.normal(_jax.random.fold_in(key, 0), shape, _jnp.float32)
    for i, name in enumerate(TWIN_WEIGHTS):
        w = inp[name].astype(_jnp.float32)
        if MOMENT_SCALE is None:
            s = _jnp.sqrt(_jnp.mean(_jnp.square(w)) + 1e-30)
        else:
            s = MOMENT_SCALE[name]
        km, kv = _jax.random.split(_jax.random.fold_in(key, i + 1))
        out[name] = w
        out["m_" + name] = s * _jax.random.normal(km, w.shape, _jnp.float32)
        out["v_" + name] = (s * s) * _jax.random.uniform(kv, w.shape, _jnp.float32, 0.5, 1.5)
    if N_MICROBATCH > 1:
        for name, axis in PER_EXAMPLE_BATCH_AXIS.items():
            out[name] = _to_microbatches(out[name], axis)
    return {'x': out['x'], 'c': out['c'], 'ctx': out['ctx'], 'c_ctx': out['c_ctx'], 'w_mod': out['w_mod'], 'b_mod': out['b_mod'], 'g_norm': out['g_norm'], 'ffn_w1': out['ffn_w1'], 'ffn_w3': out['ffn_w3'], 'ffn_w2': out['ffn_w2'], 'sc_w_in': out['sc_w_in'], 'sc_conv': out['sc_conv'], 'sc_w_out': out['sc_w_out'], 'mla_w_a': out['mla_w_a'], 'mla_g_qa': out['mla_g_qa'], 'mla_w_uq': out['mla_w_uq'], 'mla_g_kva': out['mla_g_kva'], 'mla_w_ukv': out['mla_w_ukv'], 'mla_g_q': out['mla_g_q'], 'mla_g_k': out['mla_g_k'], 'mla_w_o': out['mla_w_o'], 'loss_target': out['loss_target'], 'm_c_ctx': out['m_c_ctx'], 'm_w_mod': out['m_w_mod'], 'm_b_mod': out['m_b_mod'], 'm_g_norm': out['m_g_norm'], 'm_ffn_w1': out['m_ffn_w1'], 'm_ffn_w3': out['m_ffn_w3'], 'm_ffn_w2': out['m_ffn_w2'], 'm_sc_w_in': out['m_sc_w_in'], 'm_sc_conv': out['m_sc_conv'], 'm_sc_w_out': out['m_sc_w_out'], 'm_mla_w_a': out['m_mla_w_a'], 'm_mla_g_qa': out['m_mla_g_qa'], 'm_mla_w_uq': out['m_mla_w_uq'], 'm_mla_g_kva': out['m_mla_g_kva'], 'm_mla_w_ukv': out['m_mla_w_ukv'], 'm_mla_g_q': out['m_mla_g_q'], 'm_mla_g_k': out['m_mla_g_k'], 'm_mla_w_o': out['m_mla_w_o'], 'v_c_ctx': out['v_c_ctx'], 'v_w_mod': out['v_w_mod'], 'v_b_mod': out['v_b_mod'], 'v_g_norm': out['v_g_norm'], 'v_ffn_w1': out['v_ffn_w1'], 'v_ffn_w3': out['v_ffn_w3'], 'v_ffn_w2': out['v_ffn_w2'], 'v_sc_w_in': out['v_sc_w_in'], 'v_sc_conv': out['v_sc_conv'], 'v_sc_w_out': out['v_sc_w_out'], 'v_mla_w_a': out['v_mla_w_a'], 'v_mla_g_qa': out['v_mla_g_qa'], 'v_mla_w_uq': out['v_mla_w_uq'], 'v_mla_g_kva': out['v_mla_g_kva'], 'v_mla_w_ukv': out['v_mla_w_ukv'], 'v_mla_g_q': out['v_mla_g_q'], 'v_mla_g_k': out['v_mla_g_k'], 'v_mla_w_o': out['v_mla_w_o']}


def _loss(weights, diff, rest, loss_target):
    with _jax.named_scope("forward"):
        args = {**rest, TWIN_DIFF_INPUT: diff, **{k: w.astype(_WEIGHT_DTYPES[k]) for k, w in weights.items()}}
        y = _forward(args)
    with _jax.named_scope("loss_head"):
        err = _jnp.square(y.astype(_jnp.float32) - loss_target)
        return 0.5 * _jnp.sum(_jnp.mean(err, axis=-1)) if err.ndim else 0.5 * err


def _adamw(w, g, m, v):
    m = ADAM_B1 * m + (1.0 - ADAM_B1) * g
    v = ADAM_B2 * v + (1.0 - ADAM_B2) * _jnp.square(g)
    m_hat = m / (1.0 - ADAM_B1 ** ADAM_STEP)
    v_hat = v / (1.0 - ADAM_B2 ** ADAM_STEP)
    delta = -ADAM_LR * (m_hat / (_jnp.sqrt(v_hat) + ADAM_EPS) + ADAM_WD * w)
    return delta, m, v


def reference(x, c, ctx, c_ctx, w_mod, b_mod, g_norm, ffn_w1, ffn_w3, ffn_w2, sc_w_in, sc_conv, sc_w_out, mla_w_a, mla_g_qa, mla_w_uq, mla_g_kva, mla_w_ukv, mla_g_q, mla_g_k, mla_w_o, loss_target, m_c_ctx, m_w_mod, m_b_mod, m_g_norm, m_ffn_w1, m_ffn_w3, m_ffn_w2, m_sc_w_in, m_sc_conv, m_sc_w_out, m_mla_w_a, m_mla_g_qa, m_mla_w_uq, m_mla_g_kva, m_mla_w_ukv, m_mla_g_q, m_mla_g_k, m_mla_w_o, v_c_ctx, v_w_mod, v_b_mod, v_g_norm, v_ffn_w1, v_ffn_w3, v_ffn_w2, v_sc_w_in, v_sc_conv, v_sc_w_out, v_mla_w_a, v_mla_g_qa, v_mla_w_uq, v_mla_g_kva, v_mla_w_ukv, v_mla_g_q, v_mla_g_k, v_mla_w_o):
    given = dict(x=x, c=c, ctx=ctx, c_ctx=c_ctx, w_mod=w_mod, b_mod=b_mod, g_norm=g_norm, ffn_w1=ffn_w1, ffn_w3=ffn_w3, ffn_w2=ffn_w2, sc_w_in=sc_w_in, sc_conv=sc_conv, sc_w_out=sc_w_out, mla_w_a=mla_w_a, mla_g_qa=mla_g_qa, mla_w_uq=mla_w_uq, mla_g_kva=mla_g_kva, mla_w_ukv=mla_w_ukv, mla_g_q=mla_g_q, mla_g_k=mla_g_k, mla_w_o=mla_w_o, loss_target=loss_target, m_c_ctx=m_c_ctx, m_w_mod=m_w_mod, m_b_mod=m_b_mod, m_g_norm=m_g_norm, m_ffn_w1=m_ffn_w1, m_ffn_w3=m_ffn_w3, m_ffn_w2=m_ffn_w2, m_sc_w_in=m_sc_w_in, m_sc_conv=m_sc_conv, m_sc_w_out=m_sc_w_out, m_mla_w_a=m_mla_w_a, m_mla_g_qa=m_mla_g_qa, m_mla_w_uq=m_mla_w_uq, m_mla_g_kva=m_mla_g_kva, m_mla_w_ukv=m_mla_w_ukv, m_mla_g_q=m_mla_g_q, m_mla_g_k=m_mla_g_k, m_mla_w_o=m_mla_w_o, v_c_ctx=v_c_ctx, v_w_mod=v_w_mod, v_b_mod=v_b_mod, v_g_norm=v_g_norm, v_ffn_w1=v_ffn_w1, v_ffn_w3=v_ffn_w3, v_ffn_w2=v_ffn_w2, v_sc_w_in=v_sc_w_in, v_sc_conv=v_sc_conv, v_sc_w_out=v_sc_w_out, v_mla_w_a=v_mla_w_a, v_mla_g_qa=v_mla_g_qa, v_mla_w_uq=v_mla_w_uq, v_mla_g_kva=v_mla_g_kva, v_mla_w_ukv=v_mla_w_ukv, v_mla_g_q=v_mla_g_q, v_mla_g_k=v_mla_g_k, v_mla_w_o=v_mla_w_o)
    weights = {n: given[n] for n in TWIN_WEIGHTS}
    shared = {n: given[n] for n in SHARED_INPUTS}
    per_example = {n: given[n] for n in ['x', 'c', 'ctx']}
    grad_fn = _jax.value_and_grad(_loss, argnums=(0, 1))

    def one_microbatch(ex, loss_target):
        ex = dict(ex)
        diff = ex.pop(TWIN_DIFF_INPUT)
        return grad_fn(weights, diff, {**shared, **ex}, loss_target)

    if N_MICROBATCH == 1:
        loss, (grad_w, grad_x) = one_microbatch(per_example, given["loss_target"])
    else:
        def body(carry, xs):
            loss_sum, grad_sum = carry
            l_k, (gw_k, gx_k) = one_microbatch(xs[0], xs[1])
            with _jax.named_scope("update"):
                return (loss_sum + l_k, _jax.tree.map(_jnp.add, grad_sum, gw_k)), gx_k

        init = (_jnp.zeros((), _jnp.float32), _jax.tree.map(_jnp.zeros_like, weights))
        (loss, grad_w), grad_x = _jax.lax.scan(body, init, (per_example, given["loss_target"]))
    with _jax.named_scope("update"):
        delta_w, new_m, new_v = {}, {}, {}
        for n in TWIN_WEIGHTS:
            delta_w[n], new_m[n], new_v[n] = _adamw(weights[n], grad_w[n], given["m_" + n], given["v_" + n])
    return (loss, grad_x, *[grad_w[n] for n in TWIN_WEIGHTS], *[delta_w[n] for n in TWIN_WEIGHTS],
            *[new_m[n] for n in TWIN_WEIGHTS], *[new_v[n] for n in TWIN_WEIGHTS])
```

```python
import functools
import math

import jax
import jax.numpy as jnp
import numpy as np
from jax import lax
from jax.experimental import pallas as pl
from jax.experimental.pallas import tpu as pltpu

F32 = jnp.float32
BF16 = jnp.bfloat16

N_MOD = 9
HEADS = 8
QK_NOPE = 128
QK_ROPE = 64
QK_HEAD = QK_NOPE + QK_ROPE
V_HEAD = 128
GRID_W = 64
ROPE_BASE = 10000.0
QK_SCALE = QK_HEAD ** -0.5
EPS = 1e-6
ADAM_LR, ADAM_B1, ADAM_B2, ADAM_EPS, ADAM_WD, ADAM_STEP = 0.001, 0.9, 0.999, 1e-08, 0.01, 10

N_DEV = 8
N_SEG = 3
LANE = 128
HEAD_PAD = 2 * LANE
PACK_COLS = 1024
PACK_ROWS = 16
VMEM_LIMIT_BYTES = 48 * 1024 * 1024
MESH = pl.DeviceIdType.MESH

SHARD_AXIS = {
    "w_mod": 2, "g_norm": 2, "ffn_w1": 3, "ffn_w3": 3, "ffn_w2": 2, "sc_w_in": 2, "sc_conv": 2, "sc_w_out": 1,
    "mla_w_a": 1, "mla_g_qa": 1, "mla_w_uq": 2, "mla_w_ukv": 2, "mla_w_o": 1,
}
WEIGHTS = ["c_ctx", "w_mod", "b_mod", "g_norm", "ffn_w1", "ffn_w3", "ffn_w2", "sc_w_in", "sc_conv", "sc_w_out",
           "mla_w_a", "mla_g_qa", "mla_w_uq", "mla_g_kva", "mla_w_ukv", "mla_g_q", "mla_g_k", "mla_w_o"]
GATHER_BF16 = ["w_mod", "ffn_w1", "ffn_w3", "ffn_w2", "sc_w_in", "sc_w_out", "mla_w_a", "mla_w_uq", "mla_w_ukv", "mla_w_o"]
GATHER_F32 = ["g_norm", "sc_conv", "mla_g_qa"]


def _pick(n, cands):
    for cand in cands:
        if n % cand == 0:
            return cand
    return n


def _params(*sem):
    return pltpu.CompilerParams(dimension_semantics=sem, vmem_limit_bytes=VMEM_LIMIT_BYTES)


class Geo:
    def __init__(self, n_lat, n_ctx):
        self.n_lat, self.n_ctx = n_lat, n_ctx
        self.rows = 2 * n_lat + 2 * n_ctx
        self.tile = n_ctx
        assert n_lat % n_ctx == 0 and n_ctx % 16 == 0
        self.mm_tile = _pick(n_lat, (512, 256, 128)) if self.rows % _pick(n_lat, (512, 256, 128)) == 0 else n_ctx

    def seg(self, i, tile):
        return jnp.minimum((i * tile) // self.n_lat, N_SEG - 1)

    def seg_start(self, i, tile):
        row = i * tile
        return jnp.logical_or(row % self.n_lat == 0, row == 2 * self.n_lat) & (row <= 2 * self.n_lat)


def _mm(a, b, *, ta=False, tb=False, out_dtype=F32, name, gate=None):
    (kdim, m) = a.shape if ta else a.shape[::-1]
    n = b.shape[0] if tb else b.shape[1]
    assert (b.shape[1] if tb else b.shape[0]) == kdim
    if gate is not None:
        tm = gate[4].mm_tile
    else:
        tm = _pick(m, (512, 256, 128))
    tn = _pick(n, (512, 256, 128))
    tk = _pick(kdim, (1024, 512, 256, 128))
    nk = kdim // tk
    dims = (((0 if ta else 1,), (1 if tb else 0,)), ((), ()))

    def body(*refs):
        if gate is not None:
            a_ref, b_ref, res_ref, gate_ref, o_ref, y_ref, acc_ref = refs
        else:
            a_ref, b_ref, o_ref, acc_ref = refs
        kk = pl.program_id(2)

        @pl.when(kk == 0)
        def _():
            acc_ref[...] = jnp.zeros_like(acc_ref)

        acc_ref[...] += lax.dot_general(a_ref[...].astype(BF16), b_ref[...].astype(BF16), dims,
                                        preferred_element_type=F32)

        @pl.when(kk == nk - 1)
        def _():
            acc = acc_ref[...]
            if gate is not None:
                y_ref[...] = acc.astype(y_ref.dtype)
                o_ref[...] = res_ref[...] + (gate[3] * gate_ref[...]) * acc
            else:
                o_ref[...] = acc.astype(o_ref.dtype)

    a_spec = pl.BlockSpec((tk, tm), lambda i, j, k: (k, i)) if ta else pl.BlockSpec((tm, tk), lambda i, j, k: (i, k))
    b_spec = pl.BlockSpec((tn, tk), lambda i, j, k: (j, k)) if tb else pl.BlockSpec((tk, tn), lambda i, j, k: (k, j))
    o_spec = pl.BlockSpec((tm, tn), lambda i, j, k: (i, j))
    in_specs, args = [a_spec, b_spec], [a, b]
    out_shape, out_specs = jax.ShapeDtypeStruct((m, n), out_dtype), o_spec
    if gate is not None:
        res, mod4, kmod, _, geo = gate
        in_specs += [o_spec, pl.BlockSpec((None, None, 1, tn), lambda i, j, k: (geo.seg(i, tm), kmod, 0, j))]
        args += [res, mod4]
        out_shape = (jax.ShapeDtypeStruct((m, n), F32), jax.ShapeDtypeStruct((m, n), BF16))
        out_specs = (o_spec, o_spec)
    return pl.pallas_call(
        body, name=name, grid=(m // tm, n // tn, nk), in_specs=in_specs, out_specs=out_specs, out_shape=out_shape,
        scratch_shapes=[pltpu.VMEM((tm, tn), F32)], compiler_params=_params("parallel", "parallel", "arbitrary"),
    )(*args)


def _mod_spec(geo, tile, kmod, d):
    return pl.BlockSpec((None, None, 1, d), lambda i: (geo.seg(i, tile), kmod, 0, 0))


def _pre_fwd(h, g, mod4, k_shift, geo, name):
    t, d = h.shape
    tile = geo.tile

    def body(h_ref, g_ref, sh_ref, sc_ref, o_ref):
        hv = h_ref[...]
        r = lax.rsqrt(jnp.mean(hv * hv, axis=-1, keepdims=True) + EPS)
        y = hv * r * g_ref[...]
        o_ref[...] = (y * (1.0 + sc_ref[...]) + sh_ref[...]).astype(o_ref.dtype)

    row = pl.BlockSpec((tile, d), lambda i: (i, 0))
    return pl.pallas_call(
        body, name=name, grid=(t // tile,),
        in_specs=[row, pl.BlockSpec((1, d), lambda i: (0, 0)), _mod_spec(geo, tile, k_shift, d),
                  _mod_spec(geo, tile, k_shift + 1, d)],
        out_specs=row, out_shape=jax.ShapeDtypeStruct((t, d), BF16), compiler_params=_params("parallel"),
    )(h, g, mod4, mod4)


def _pre_bwd(h, g, mod4, k_shift, dnx, dres, geo, name):
    t, d = h.shape
    tile = geo.tile

    def body(h_ref, g_ref, sc_ref, dnx_ref, dres_ref, dh_ref, dg_ref, dsh_ref, dsc_ref):
        i = pl.program_id(0)
        hv, gv, dout = h_ref[...], g_ref[...], dnx_ref[...].astype(F32)
        r = lax.rsqrt(jnp.mean(hv * hv, axis=-1, keepdims=True) + EPS)
        xhat = hv * r
        dy = dout * (1.0 + sc_ref[...])
        u = dy * gv
        dh_ref[...] = r * (u - xhat * jnp.mean(u * xhat, axis=-1, keepdims=True)) + dres_ref[...]

        @pl.when(i == 0)
        def _():
            dg_ref[...] = jnp.zeros_like(dg_ref)

        @pl.when(geo.seg_start(i, tile))
        def _():
            dsh_ref[...] = jnp.zeros_like(dsh_ref)
            dsc_ref[...] = jnp.zeros_like(dsc_ref)

        dg_ref[...] += jnp.sum(dy * xhat, axis=0, keepdims=True)
        dsh_ref[...] += jnp.sum(dout, axis=0, keepdims=True)
        dsc_ref[...] += jnp.sum(dout * (xhat * gv), axis=0, keepdims=True)

    row = pl.BlockSpec((tile, d), lambda i: (i, 0))
    vec = pl.BlockSpec((1, d), lambda i: (0, 0))
    segv = pl.BlockSpec((None, 1, d), lambda i: (geo.seg(i, tile), 0, 0))
    return pl.pallas_call(
        body, name=name, grid=(t // tile,),
        in_specs=[row, vec, _mod_spec(geo, tile, k_shift + 1, d), row, row],
        out_specs=(row, vec, segv, segv),
        out_shape=(jax.ShapeDtypeStruct((t, d), F32), jax.ShapeDtypeStruct((1, d), F32),
                   jax.ShapeDtypeStruct((N_SEG, 1, d), F32), jax.ShapeDtypeStruct((N_SEG, 1, d), F32)),
        compiler_params=_params("arbitrary"),
    )(h, g, mod4, dnx, dres)


def _gate_bwd(dh, y, mod4, k_gate, coef, geo, name):
    t, d = dh.shape
    tile = geo.tile

    def body(dh_ref, y_ref, gt_ref, dy_ref, dgt_ref):
        i = pl.program_id(0)
        dhv = dh_ref[...]
        dy_ref[...] = ((coef * gt_ref[...]) * dhv).astype(dy_ref.dtype)

        @pl.when(geo.seg_start(i, tile))
        def _():
            dgt_ref[...] = jnp.zeros_like(dgt_ref)

        dgt_ref[...] += coef * jnp.sum(dhv * y_ref[...].astype(F32), axis=0, keepdims=True)

    row = pl.BlockSpec((tile, d), lambda i: (i, 0))
    segv = pl.BlockSpec((None, 1, d), lambda i: (geo.seg(i, tile), 0, 0))
    return pl.pallas_call(
        body, name=name, grid=(t // tile,), in_specs=[row, row, _mod_spec(geo, tile, k_gate, d)],
        out_specs=(row, segv),
        out_shape=(jax.ShapeDtypeStruct((t, d), BF16), jax.ShapeDtypeStruct((N_SEG, 1, d), F32)),
        compiler_params=_params("arbitrary"),
    )(dh, y, mod4)


def _ff_tile(f):
    return _pick(f, (256, 128))


def _swiglu_fwd(ab, geo, name):
    t, f2 = ab.shape
    f = f2 // 2
    tf, tile = _ff_tile(f), geo.tile

    def body(ab_ref, o_ref):
        a = ab_ref[:, :tf].astype(F32)
        bv = ab_ref[:, tf:].astype(F32)
        o_ref[...] = (a * jax.nn.sigmoid(a) * bv).astype(o_ref.dtype)

    return pl.pallas_call(
        body, name=name, grid=(t // tile, f // tf), in_specs=[pl.BlockSpec((tile, 2 * tf), lambda i, j: (i, j))],
        out_specs=pl.BlockSpec((tile, tf), lambda i, j: (i, j)), out_shape=jax.ShapeDtypeStruct((t, f), BF16),
        compiler_params=_params("parallel", "parallel"),
    )(ab)


def _swiglu_bwd(ab, dact, geo, name):
    t, f2 = ab.shape
    f = f2 // 2
    tf, tile = _ff_tile(f), geo.tile

    def body(ab_ref, d_ref, o_ref):
        a = ab_ref[:, :tf].astype(F32)
        bv = ab_ref[:, tf:].astype(F32)
        dv = d_ref[...].astype(F32)
        sg = jax.nn.sigmoid(a)
        silu = a * sg
        o_ref[:, :tf] = (dv * bv * (sg * (1.0 + a * (1.0 - sg)))).astype(o_ref.dtype)
        o_ref[:, tf:] = (dv * silu).astype(o_ref.dtype)

    blk2 = pl.BlockSpec((tile, 2 * tf), lambda i, j: (i, j))
    return pl.pallas_call(
        body, name=name, grid=(t // tile, f // tf), in_specs=[blk2, pl.BlockSpec((tile, tf), lambda i, j: (i, j))],
        out_specs=blk2, out_shape=jax.ShapeDtypeStruct((t, f2), BF16), compiler_params=_params("parallel", "parallel"),
    )(ab, dact)


def _interleave(w, n_parts, tile):
    lead, cols = w.shape[:-1], w.shape[-1] // n_parts
    return w.reshape(*lead, n_parts, cols // tile, tile).swapaxes(-3, -2).reshape(*lead, n_parts * cols)


def _deinterleave(w, n_parts, tile):
    lead, cols = w.shape[:-1], w.shape[-1] // n_parts
    return w.reshape(*lead, cols // tile, n_parts, tile).swapaxes(-3, -2).reshape(*lead, n_parts * cols)


def _ffn_fwd(h, g, mod4, k0, w13, w2, geo, tag):
    nx = _pre_fwd(h, g, mod4, k0, geo, f"{tag}_pre")
    ab = _mm(nx, w13, out_dtype=BF16, name=f"{tag}_up")
    act = _swiglu_fwd(ab, geo, f"{tag}_act")
    h_out, y = _mm(act, w2, name=f"{tag}_down", gate=(h, mod4, k0 + 2, 0.5, geo))
    return h_out, (h, nx, ab, act, y)


def _ffn_bwd(dh_out, saved, g, mod4, k0, w13, w2, geo, tag):
    h, nx, ab, act, y = saved
    dy, dgate = _gate_bwd(dh_out, y, mod4, k0 + 2, 0.5, geo, f"{tag}_dgate")
    dact = _mm(dy, w2, tb=True, out_dtype=BF16, name=f"{tag}_dact")
    dw2 = _mm(act, dy, ta=True, name=f"{tag}_dw2")
    dab = _swiglu_bwd(ab, dact, geo, f"{tag}_dab")
    dnx = _mm(dab, w13, tb=True, name=f"{tag}_dnx")
    dw13 = _mm(nx, dab, ta=True, name=f"{tag}_dw13")
    dh, dg, dshift, dscale = _pre_bwd(h, g, mod4, k0, dnx, dh_out, geo, f"{tag}_dpre")
    return dh, dg, (dshift, dscale, dgate), dw13, dw2


HALO = 16


def _conv_tile(c):
    return _pick(c, (256, 128))


def _conv_specs(geo, tc, t):
    tile = geo.tile
    per = tile // HALO
    last = t // HALO - 1
    cur = pl.BlockSpec((tile, 3 * tc), lambda j, i: (i, j))
    prev = pl.BlockSpec((HALO, 3 * tc), lambda j, i: (jnp.maximum(i * per - 1, 0), j))
    nxt = pl.BlockSpec((HALO, 3 * tc), lambda j, i: (jnp.minimum((i + 1) * per, last), j))
    return cur, prev, nxt


def _conv_edges(geo, i):
    tile = geo.tile
    row = i * tile
    lat = row < 2 * geo.n_lat
    first = jnp.where(lat, row % geo.n_lat == 0, (row - 2 * geo.n_lat) % geo.n_ctx == 0)
    end = row + tile
    last = jnp.where(lat, end % geo.n_lat == 0, (end - 2 * geo.n_lat) % geo.n_ctx == 0)
    return first, last


def _shift_rows(v, before, after):
    n = v.shape[0]
    rows = lax.broadcasted_iota(jnp.int32, v.shape, 0)
    down = jnp.where(rows == 0, before, pltpu.roll(v, 1, 0))
    up = jnp.where(rows == n - 1, after, pltpu.roll(v, n - 1, 0))
    return down, up


def _conv_fwd(proj, conv_w, geo, name):
    t, c3 = proj.shape
    c = c3 // 3
    tc, tile = _conv_tile(c), geo.tile

    def body(cur_ref, prev_ref, next_ref, w_ref, o_ref):
        first, last = _conv_edges(geo, pl.program_id(1))
        bv = cur_ref[:, :tc].astype(F32)
        p = cur_ref[:, tc:2 * tc].astype(F32) * cur_ref[:, 2 * tc:].astype(F32)
        p_before = prev_ref[HALO - 1:HALO, tc:2 * tc].astype(F32) * prev_ref[HALO - 1:HALO, 2 * tc:].astype(F32)
        p_after = next_ref[0:1, tc:2 * tc].astype(F32) * next_ref[0:1, 2 * tc:].astype(F32)
        p_before = jnp.where(first, 0.0, p_before)
        p_after = jnp.where(last, 0.0, p_after)
        pm1, pp1 = _shift_rows(p, p_before, p_after)
        w = w_ref[...]
        q = w[0:1] * pm1 + w[1:2] * p + w[2:3] * pp1
        o_ref[...] = (bv * q).astype(o_ref.dtype)

    cur, prev, nxt = _conv_specs(geo, tc, t)
    return pl.pallas_call(
        body, name=name, grid=(c // tc, t // tile),
        in_specs=[cur, prev, nxt, pl.BlockSpec((3, tc), lambda j, i: (0, j))],
        out_specs=pl.BlockSpec((tile, tc), lambda j, i: (i, j)), out_shape=jax.ShapeDtypeStruct((t, c), BF16),
        compiler_params=_params("parallel", "parallel"),
    )(proj, proj, proj, conv_w)


def _conv_bwd(proj, dyc, conv_w, geo, name):
    t, c3 = proj.shape
    c = c3 // 3
    tc, tile = _conv_tile(c), geo.tile

    def body(cur_ref, prev_ref, next_ref, d_ref, dprev_ref, dnext_ref, w_ref, o_ref, dw_ref):
        i = pl.program_id(1)
        first, last = _conv_edges(geo, i)
        bv = cur_ref[:, :tc].astype(F32)
        cv = cur_ref[:, tc:2 * tc].astype(F32)
        uv = cur_ref[:, 2 * tc:].astype(F32)
        p = cv * uv
        p_before = prev_ref[HALO - 1:HALO, tc:2 * tc].astype(F32) * prev_ref[HALO - 1:HALO, 2 * tc:].astype(F32)
        p_after = next_ref[0:1, tc:2 * tc].astype(F32) * next_ref[0:1, 2 * tc:].astype(F32)
        p_before = jnp.where(first, 0.0, p_before)
        p_after = jnp.where(last, 0.0, p_after)
        pm1, pp1 = _shift_rows(p, p_before, p_after)
        w = w_ref[...]
        q = w[0:1] * pm1 + w[1:2] * p + w[2:3] * pp1
        dy = d_ref[...].astype(F32)
        dq = dy * bv
        dq_before = dprev_ref[HALO - 1:HALO, :].astype(F32) * prev_ref[HALO - 1:HALO, :tc].astype(F32)
        dq_after = dnext_ref[0:1, :].astype(F32) * next_ref[0:1, :tc].astype(F32)
        dq_before = jnp.where(first, 0.0, dq_before)
        dq_after = jnp.where(last, 0.0, dq_after)
        dqm1, dqp1 = _shift_rows(dq, dq_before, dq_after)
        dp = w[0:1] * dqp1 + w[1:2] * dq + w[2:3] * dqm1
        o_ref[:, :tc] = (dy * q).astype(o_ref.dtype)
        o_ref[:, tc:2 * tc] = (dp * uv).astype(o_ref.dtype)
        o_ref[:, 2 * tc:] = (dp * cv).astype(o_ref.dtype)

        @pl.when(i == 0)
        def _():
            dw_ref[...] = jnp.zeros_like(dw_ref)

        dw_ref[0:1, :] += jnp.sum(dq * pm1, axis=0, keepdims=True)
        dw_ref[1:2, :] += jnp.sum(dq * p, axis=0, keepdims=True)
        dw_ref[2:3, :] += jnp.sum(dq * pp1, axis=0, keepdims=True)

    cur, prev, nxt = _conv_specs(geo, tc, t)
    per, lastb = tile // HALO, t // HALO - 1
    dcur = pl.BlockSpec((tile, tc), lambda j, i: (i, j))
    dprev = pl.BlockSpec((HALO, tc), lambda j, i: (jnp.maximum(i * per - 1, 0), j))
    dnext = pl.BlockSpec((HALO, tc), lambda j, i: (jnp.minimum((i + 1) * per, lastb), j))
    wspec = pl.BlockSpec((3, tc), lambda j, i: (0, j))
    return pl.pallas_call(
        body, name=name, grid=(c // tc, t // tile), in_specs=[cur, prev, nxt, dcur, dprev, dnext, wspec],
        out_specs=(cur, wspec), out_shape=(jax.ShapeDtypeStruct((t, c3), BF16), jax.ShapeDtypeStruct((3, c), F32)),
        compiler_params=_params("parallel", "arbitrary"),
    )(proj, proj, proj, dyc, dyc, dyc, conv_w)


def _sconv_fwd(h, g, mod4, w_in, conv_w, w_out, geo, tag):
    nx = _pre_fwd(h, g, mod4, 3, geo, f"{tag}_pre")
    proj = _mm(nx, w_in, out_dtype=BF16, name=f"{tag}_in")
    yc = _conv_fwd(proj, conv_w, geo, f"{tag}_conv")
    h_out, y = _mm(yc, w_out, name=f"{tag}_out", gate=(h, mod4, 5, 1.0, geo))
    return h_out, (h, nx, proj, yc, y)


def _sconv_bwd(dh_out, saved, g, mod4, w_in, conv_w, w_out, geo, tag):
    h, nx, proj, yc, y = saved
    dy, dgate = _gate_bwd(dh_out, y, mod4, 5, 1.0, geo, f"{tag}_dgate")
    dyc = _mm(dy, w_out, tb=True, out_dtype=BF16, name=f"{tag}_dyc")
    dw_out = _mm(yc, dy, ta=True, name=f"{tag}_dwout")
    dproj, dconv = _conv_bwd(proj, dyc, conv_w, geo, f"{tag}_dconv")
    dnx = _mm(dproj, w_in, tb=True, name=f"{tag}_dnx")
    dw_in = _mm(nx, dproj, ta=True, name=f"{tag}_dwin")
    dh, dg, dshift, dscale = _pre_bwd(h, g, mod4, 3, dnx, dh_out, geo, f"{tag}_dpre")
    return dh, dg, (dshift, dscale, dgate), dw_in, dconv, dw_out


def _rope_swap(v):
    nf = QK_ROPE // 4
    return v.reshape(v.shape[:-1] + (2, 2, nf)).swapaxes(-3, -2).reshape(v.shape)


def _rope_tables(geo):
    n = geo.n_lat
    nf = QK_ROPE // 4
    pos = np.arange(n)
    inv = ROPE_BASE ** (-np.arange(nf, dtype=np.float32) / nf)
    ang = np.concatenate([(pos // GRID_W)[:, None] * inv, (pos % GRID_W)[:, None] * inv], axis=1).astype(np.float32)
    cos, sin = np.cos(ang), np.sin(ang)
    zeros = np.zeros((n, LANE - QK_ROPE), np.float32)
    c_lat = np.concatenate([cos, cos, zeros], axis=1)
    s_lat = np.concatenate([-sin, sin, zeros], axis=1)
    c_ctx = np.concatenate([np.ones((2 * geo.n_ctx, QK_ROPE), np.float32), np.zeros((2 * geo.n_ctx, LANE - QK_ROPE), np.float32)], 1)
    s_ctx = np.zeros((2 * geo.n_ctx, LANE), np.float32)
    return (jnp.asarray(np.concatenate([c_lat, c_lat, c_ctx], 0)), jnp.asarray(np.concatenate([s_lat, s_lat, s_ctx], 0)))


def _swap_halves(v):
    lanes = lax.broadcasted_iota(jnp.int32, v.shape, 1)
    return jnp.where(lanes < QK_ROPE // 2, pltpu.roll(v, LANE - QK_ROPE // 2, 1), pltpu.roll(v, QK_ROPE // 2, 1))


def _latent_norm_fwd(down, g_qa, g_kva, geo, name):
    t, wd = down.shape
    ql, kl = g_qa.shape[1], g_kva.shape[1]
    tile = geo.tile

    def body(d_ref, gq_ref, gk_ref, cq_ref, ckv_ref):
        for lo, n, g_ref, o_ref in ((0, ql, gq_ref, cq_ref), (ql, kl, gk_ref, ckv_ref)):
            x = d_ref[:, lo:lo + n]
            r = lax.rsqrt(jnp.mean(x * x, axis=-1, keepdims=True) + EPS)
            o_ref[...] = (x * r * g_ref[...]).astype(o_ref.dtype)

    return pl.pallas_call(
        body, name=name, grid=(t // tile,),
        in_specs=[pl.BlockSpec((tile, wd), lambda i: (i, 0)), pl.BlockSpec((1, ql), lambda i: (0, 0)),
                  pl.BlockSpec((1, kl), lambda i: (0, 0))],
        out_specs=(pl.BlockSpec((tile, ql), lambda i: (i, 0)), pl.BlockSpec((tile, kl), lambda i: (i, 0))),
        out_shape=(jax.ShapeDtypeStruct((t, ql), BF16), jax.ShapeDtypeStruct((t, kl), BF16)),
        compiler_params=_params("parallel"),
    )(down, g_qa, g_kva)


def _latent_norm_bwd(down, g_qa, g_kva, dcqn, dckvn, dkr, geo, name):
    t, wd = down.shape
    ql, kl = g_qa.shape[1], g_kva.shape[1]
    tile = geo.tile

    def body(d_ref, gq_ref, gk_ref, dq_ref, dk_ref, dkr_ref, o_ref, dgq_ref, dgk_ref):
        i = pl.program_id(0)

        @pl.when(i == 0)
        def _():
            dgq_ref[...] = jnp.zeros_like(dgq_ref)
            dgk_ref[...] = jnp.zeros_like(dgk_ref)

        for lo, n, g_ref, dy_ref, dg_ref in ((0, ql, gq_ref, dq_ref, dgq_ref), (ql, kl, gk_ref, dk_ref, dgk_ref)):
            x = d_ref[:, lo:lo + n]
            dy = dy_ref[...].astype(F32)
            r = lax.rsqrt(jnp.mean(x * x, axis=-1, keepdims=True) + EPS)
            xhat = x * r
            u = dy * g_ref[...]
            o_ref[:, lo:lo + n] = (r * (u - xhat * jnp.mean(u * xhat, axis=-1, keepdims=True))).astype(o_ref.dtype)
            dg_ref[...] += jnp.sum(dy * xhat, axis=0, keepdims=True)
        o_ref[:, ql + kl:] = dkr_ref[...].astype(o_ref.dtype)

    def row(n):
        return pl.BlockSpec((tile, n), lambda i: (i, 0))

    def vec(n):
        return pl.BlockSpec((1, n), lambda i: (0, 0))

    return pl.pallas_call(
        body, name=name, grid=(t // tile,),
        in_specs=[row(wd), vec(ql), vec(kl), row(ql), row(kl), row(wd - ql - kl)],
        out_specs=(row(wd), vec(ql), vec(kl)),
        out_shape=(jax.ShapeDtypeStruct((t, wd), BF16), jax.ShapeDtypeStruct((1, ql), F32),
                   jax.ShapeDtypeStruct((1, kl), F32)),
        compiler_params=_params("arbitrary"),
    )(down, g_qa, g_kva, dcqn, dckvn, dkr)


def _qk_specs(geo, xn_col0, xr, xr_col, shared_rope):
    tile = geo.tile
    xn_spec = pl.BlockSpec((tile, LANE), lambda i, hh: (i, xn_col0 + hh))
    if shared_rope:
        xr_spec = pl.BlockSpec((tile, LANE), lambda i, hh: (i, xr_col))
    else:
        xr_spec = pl.BlockSpec((tile, LANE), lambda i, hh: (i, xr_col + hh))
    vec = pl.BlockSpec((1, LANE), lambda i, hh: (0, 0))
    tab = pl.BlockSpec((tile, LANE), lambda i, hh: (i, 0))
    return xn_spec, xr_spec, vec, tab


def _qk_norm(xn, xr):
    ss = jnp.sum(xn * xn, axis=-1, keepdims=True) + jnp.sum(xr * xr, axis=-1, keepdims=True)
    return lax.rsqrt(ss * (1.0 / QK_HEAD) + EPS)


def _qk_fwd(xn_arr, xn_col0, xr_arr, xr_col, shared_rope, gn, gr, cos, sin, geo, name):
    t = xn_arr.shape[0]
    tile = geo.tile

    def body(xn_ref, xr_ref, gn_ref, gr_ref, c_ref, s_ref, o_ref):
        xn, xr = xn_ref[...].astype(F32), xr_ref[...].astype(F32)
        r = _qk_norm(xn, xr)
        yr = xr * r * gr_ref[...]
        o_ref[:, :LANE] = (xn * r * gn_ref[...]).astype(o_ref.dtype)
        o_ref[:, LANE:] = (yr * c_ref[...] + _swap_halves(yr) * s_ref[...]).astype(o_ref.dtype)

    xn_spec, xr_spec, vec, tab = _qk_specs(geo, xn_col0, xr_arr, xr_col, shared_rope)
    return pl.pallas_call(
        body, name=name, grid=(t // tile, HEADS), in_specs=[xn_spec, xr_spec, vec, vec, tab, tab],
        out_specs=pl.BlockSpec((tile, HEAD_PAD), lambda i, hh: (i, hh)),
        out_shape=jax.ShapeDtypeStruct((t, HEADS * HEAD_PAD), BF16), compiler_params=_params("parallel", "parallel"),
    )(xn_arr, xr_arr, gn, gr, cos, sin)


def _qk_bwd(xn_arr, xn_col0, xr_arr, xr_col, shared_rope, gn, gr, cos, sin, dout, geo, name):
    t = xn_arr.shape[0]
    tile = geo.tile

    def body(xn_ref, xr_ref, gn_ref, gr_ref, c_ref, s_ref, d_ref, dxn_ref, dxr_ref, dgn_ref, dgr_ref):
        i, hh = pl.program_id(0), pl.program_id(1)
        xn, xr = xn_ref[...].astype(F32), xr_ref[...].astype(F32)
        r = _qk_norm(xn, xr)
        xhn, xhr = xn * r, xr * r
        dyn = d_ref[:, :LANE].astype(F32)
        dro = d_ref[:, LANE:].astype(F32)
        dyr = dro * c_ref[...] + _swap_halves(dro * s_ref[...])
        un, ur = dyn * gn_ref[...], dyr * gr_ref[...]
        mean = (jnp.sum(un * xhn, axis=-1, keepdims=True) + jnp.sum(ur * xhr, axis=-1, keepdims=True)) * (1.0 / QK_HEAD)
        dxn_ref[...] = (r * (un - xhn * mean)).astype(dxn_ref.dtype)
        dxr = r * (ur - xhr * mean)

        @pl.when(jnp.logical_and(i == 0, hh == 0))
        def _():
            dgn_ref[...] = jnp.zeros_like(dgn_ref)
            dgr_ref[...] = jnp.zeros_like(dgr_ref)

        dgn_ref[...] += jnp.sum(dyn * xhn, axis=0, keepdims=True)
        dgr_ref[...] += jnp.sum(dyr * xhr, axis=0, keepdims=True)
        if shared_rope:
            @pl.when(hh == 0)
            def _():
                dxr_ref[...] = jnp.zeros_like(dxr_ref)

            dxr_ref[...] += dxr
        else:
            dxr_ref[...] = dxr.astype(dxr_ref.dtype)

    xn_spec, xr_spec, vec, tab = _qk_specs(geo, xn_col0, xr_arr, xr_col, shared_rope)
    head = pl.BlockSpec((tile, LANE), lambda i, hh: (i, hh))
    if shared_rope:
        dxr_spec, dxr_shape = pl.BlockSpec((tile, LANE), lambda i, hh: (i, 0)), jax.ShapeDtypeStruct((t, LANE), F32)
    else:
        dxr_spec, dxr_shape = head, jax.ShapeDtypeStruct((t, HEADS * LANE), BF16)
    return pl.pallas_call(
        body, name=name, grid=(t // tile, HEADS),
        in_specs=[xn_spec, xr_spec, vec, vec, tab, tab, pl.BlockSpec((tile, HEAD_PAD), lambda i, hh: (i, hh))],
        out_specs=(head, dxr_spec, vec, vec),
        out_shape=(jax.ShapeDtypeStruct((t, HEADS * LANE), BF16), dxr_shape, jax.ShapeDtypeStruct((1, LANE), F32),
                   jax.ShapeDtypeStruct((1, LANE), F32)),
        compiler_params=_params("arbitrary", "arbitrary"),
    )(xn_arr, xr_arr, gn, gr, cos, sin, dout)


def _attn_specs(geo):
    tq, nq = geo.n_ctx, geo.n_lat // geo.n_ctx

    def qrow(b, i):
        return jnp.where(i < nq, b * nq + i, 2 * nq + b)

    q_spec = pl.BlockSpec((tq, HEAD_PAD), lambda b, hh, i: (qrow(b, i), hh))
    kc_spec = pl.BlockSpec((geo.n_ctx, HEAD_PAD), lambda b, hh, i: (2 * nq + b, hh))
    kl_spec = pl.BlockSpec((geo.n_lat, HEAD_PAD), lambda b, hh, i: (b, hh))
    vc_spec = pl.BlockSpec((geo.n_ctx, V_HEAD), lambda b, hh, i: (2 * nq + b, HEADS + hh))
    vl_spec = pl.BlockSpec((geo.n_lat, V_HEAD), lambda b, hh, i: (b, HEADS + hh))
    o_spec = pl.BlockSpec((tq, V_HEAD), lambda b, hh, i: (qrow(b, i), hh))
    return tq, nq, q_spec, kc_spec, kl_spec, vc_spec, vl_spec, o_spec


_NT = (((1,), (1,)), ((), ()))
_NN = (((1,), (0,)), ((), ()))
_TN = (((0,), (0,)), ((), ()))


def _dot(a, b, dims):
    return lax.dot_general(a.astype(BF16), b.astype(BF16), dims, preferred_element_type=F32)


def _attn_fwd(q, k, kv, with_ctx_q, geo, name):
    t = q.shape[0]
    tq, nq, q_spec, kc_spec, kl_spec, vc_spec, vl_spec, o_spec = _attn_specs(geo)

    def body(q_ref, kc_ref, kl_ref, vc_ref, vl_ref, o_ref):
        i = pl.program_id(2)
        qv = q_ref[...]
        s_c = _dot(qv, kc_ref[...], _NT) * QK_SCALE

        @pl.when(i < nq)
        def _():
            s_l = _dot(qv, kl_ref[...], _NT) * QK_SCALE
            m = jnp.maximum(jnp.max(s_c, axis=-1, keepdims=True), jnp.max(s_l, axis=-1, keepdims=True))
            p_c, p_l = jnp.exp(s_c - m), jnp.exp(s_l - m)
            den = jnp.sum(p_c, axis=-1, keepdims=True) + jnp.sum(p_l, axis=-1, keepdims=True)
            o = _dot(p_c, vc_ref[...], _NN) + _dot(p_l, vl_ref[...], _NN)
            o_ref[...] = (o / den).astype(o_ref.dtype)

        @pl.when(i == nq)
        def _():
            if with_ctx_q:
                m = jnp.max(s_c, axis=-1, keepdims=True)
                p_c = jnp.exp(s_c - m)
                o = _dot(p_c, vc_ref[...], _NN) / jnp.sum(p_c, axis=-1, keepdims=True)
                o_ref[...] = o.astype(o_ref.dtype)
            else:
                o_ref[...] = jnp.zeros_like(o_ref)

    return pl.pallas_call(
        body, name=name, grid=(2, HEADS, nq + 1), in_specs=[q_spec, kc_spec, kl_spec, vc_spec, vl_spec],
        out_specs=o_spec, out_shape=jax.ShapeDtypeStruct((t, HEADS * V_HEAD), BF16),
        compiler_params=_params("parallel", "parallel", "arbitrary"),
    )(q, k, k, kv, kv)


def _attn_bwd(q, k, kv, do, with_ctx_q, geo, name):
    t = q.shape[0]
    tq, nq, q_spec, kc_spec, kl_spec, vc_spec, vl_spec, o_spec = _attn_specs(geo)

    def body(q_ref, kc_ref, kl_ref, vc_ref, vl_ref, do_ref, dq_ref, dkl_ref, dkc_ref, dvl_ref, dvc_ref,
             akl_ref, akc_ref, avl_ref, avc_ref):
        i = pl.program_id(2)

        @pl.when(i == 0)
        def _():
            for ref in (akl_ref, akc_ref, avl_ref, avc_ref):
                ref[...] = jnp.zeros_like(ref)

        qv, dov = q_ref[...], do_ref[...]
        s_c = _dot(qv, kc_ref[...], _NT) * QK_SCALE
        dp_c = _dot(dov, vc_ref[...], _NT)

        def ctx_part(p_c, delta):
            ds_c = (p_c * (dp_c - delta) * QK_SCALE).astype(BF16)
            akc_ref[...] += _dot(ds_c, qv, _TN)
            avc_ref[...] += _dot(p_c, dov, _TN)
            return _dot(ds_c, kc_ref[...], _NN)

        @pl.when(i < nq)
        def _():
            s_l = _dot(qv, kl_ref[...], _NT) * QK_SCALE
            m = jnp.maximum(jnp.max(s_c, axis=-1, keepdims=True), jnp.max(s_l, axis=-1, keepdims=True))
            p_c, p_l = jnp.exp(s_c - m), jnp.exp(s_l - m)
            inv = 1.0 / (jnp.sum(p_c, axis=-1, keepdims=True) + jnp.sum(p_l, axis=-1, keepdims=True))
            p_c, p_l = p_c * inv, p_l * inv
            dp_l = _dot(dov, vl_ref[...], _NT)
            delta = jnp.sum(p_c * dp_c, axis=-1, keepdims=True) + jnp.sum(p_l * dp_l, axis=-1, keepdims=True)
            ds_l = (p_l * (dp_l - delta) * QK_SCALE).astype(BF16)
            akl_ref[...] += _dot(ds_l, qv, _TN)
            avl_ref[...] += _dot(p_l, dov, _TN)
            dq_ref[...] = (ctx_part(p_c, delta) + _dot(ds_l, kl_ref[...], _NN)).astype(dq_ref.dtype)

        @pl.when(i == nq)
        def _():
            if with_ctx_q:
                m = jnp.max(s_c, axis=-1, keepdims=True)
                p_c = jnp.exp(s_c - m)
                p_c = p_c * (1.0 / jnp.sum(p_c, axis=-1, keepdims=True))
                delta = jnp.sum(p_c * dp_c, axis=-1, keepdims=True)
                dq_ref[...] = ctx_part(p_c, delta).astype(dq_ref.dtype)
            else:
                dq_ref[...] = jnp.zeros_like(dq_ref)
            dkl_ref[...] = akl_ref[...].astype(dkl_ref.dtype)
            dkc_ref[...] = akc_ref[...].astype(dkc_ref.dtype)
            dvl_ref[...] = avl_ref[...].astype(dvl_ref.dtype)
            dvc_ref[...] = avc_ref[...].astype(dvc_ref.dtype)

    def acc_spec(rows, width):
        return pl.BlockSpec((rows, width), lambda b, hh, i: (b, hh))

    return pl.pallas_call(
        body, name=name, grid=(2, HEADS, nq + 1), in_specs=[q_spec, kc_spec, kl_spec, vc_spec, vl_spec, o_spec],
        out_specs=(q_spec, acc_spec(geo.n_lat, HEAD_PAD), acc_spec(geo.n_ctx, HEAD_PAD), acc_spec(geo.n_lat, V_HEAD),
                   acc_spec(geo.n_ctx, V_HEAD)),
        out_shape=(jax.ShapeDtypeStruct((t, HEADS * HEAD_PAD), BF16),
                   jax.ShapeDtypeStruct((2 * geo.n_lat, HEADS * HEAD_PAD), BF16),
                   jax.ShapeDtypeStruct((2 * geo.n_ctx, HEADS * HEAD_PAD), BF16),
                   jax.ShapeDtypeStruct((2 * geo.n_lat, HEADS * V_HEAD), BF16),
                   jax.ShapeDtypeStruct((2 * geo.n_ctx, HEADS * V_HEAD), BF16)),
        scratch_shapes=[pltpu.VMEM((geo.n_lat, HEAD_PAD), F32), pltpu.VMEM((geo.n_ctx, HEAD_PAD), F32),
                        pltpu.VMEM((geo.n_lat, V_HEAD), F32), pltpu.VMEM((geo.n_ctx, V_HEAD), F32)],
        compiler_params=_params("parallel", "parallel", "arbitrary"),
    )(q, k, k, kv, kv, do)


def _mla_fwd(h, g, mod4, w, with_ctx_q, tabs, geo, tag):
    cos, sin = tabs
    ql, kl = w["g_qa"].shape[1], w["g_kva"].shape[1]
    kr_col = (ql + kl) // LANE
    nx = _pre_fwd(h, g, mod4, 3, geo, f"{tag}_pre")
    down = _mm(nx, w["w_a"], name=f"{tag}_down")
    cqn, ckvn = _latent_norm_fwd(down, w["g_qa"], w["g_kva"], geo, f"{tag}_lnorm")
    qraw = _mm(cqn, w["w_uq"], out_dtype=BF16, name=f"{tag}_uq")
    kvraw = _mm(ckvn, w["w_ukv"], out_dtype=BF16, name=f"{tag}_ukv")
    q = _qk_fwd(qraw, 0, qraw, HEADS, False, w["gq_n"], w["gq_r"], cos, sin, geo, f"{tag}_qnorm")
    k = _qk_fwd(kvraw, 0, down, kr_col, True, w["gk_n"], w["gk_r"], cos, sin, geo, f"{tag}_knorm")
    o = _attn_fwd(q, k, kvraw, with_ctx_q, geo, f"{tag}_attn")
    h_out, y = _mm(o, w["w_o"], name=f"{tag}_o", gate=(h, mod4, 5, 1.0, geo))
    return h_out, (h, nx, down, cqn, ckvn, qraw, kvraw, q, k, o, y)


def _mla_bwd(dh_out, saved, g, mod4, w, with_ctx_q, tabs, geo, tag):
    cos, sin = tabs
    h, nx, down, cqn, ckvn, qraw, kvraw, q, k, o, y = saved
    ql, kl = w["g_qa"].shape[1], w["g_kva"].shape[1]
    kr_col = (ql + kl) // LANE
    dy, dgate = _gate_bwd(dh_out, y, mod4, 5, 1.0, geo, f"{tag}_dgate")
    do = _mm(dy, w["w_o"], tb=True, out_dtype=BF16, name=f"{tag}_do")
    dw_o = _mm(o, dy, ta=True, name=f"{tag}_dwo")
    dq, dk_lat, dk_ctx, dv_lat, dv_ctx = _attn_bwd(q, k, kvraw, do, with_ctx_q, geo, f"{tag}_dattn")
    dk = jnp.concatenate([dk_lat, dk_ctx], axis=0)
    dqn, dqr, dgq_n, dgq_r = _qk_bwd(qraw, 0, qraw, HEADS, False, w["gq_n"], w["gq_r"], cos, sin, dq, geo, f"{tag}_dqnorm")
    dkn, dkr, dgk_n, dgk_r = _qk_bwd(kvraw, 0, down, kr_col, True, w["gk_n"], w["gk_r"], cos, sin, dk, geo, f"{tag}_dknorm")
    dqraw = jnp.concatenate([dqn, dqr], axis=1)
    dkvraw = jnp.concatenate([dkn, jnp.concatenate([dv_lat, dv_ctx], axis=0)], axis=1)
    dcqn = _mm(dqraw, w["w_uq"], tb=True, out_dtype=BF16, name=f"{tag}_dcqn")
    dw_uq = _mm(cqn, dqraw, ta=True, name=f"{tag}_dwuq")
    dckvn = _mm(dkvraw, w["w_ukv"], tb=True, out_dtype=BF16, name=f"{tag}_dckvn")
    dw_ukv = _mm(ckvn, dkvraw, ta=True, name=f"{tag}_dwukv")
    ddown, dg_qa, dg_kva = _latent_norm_bwd(down, w["g_qa"], w["g_kva"], dcqn, dckvn, dkr, geo, f"{tag}_dlnorm")
    dnx = _mm(ddown, w["w_a"], tb=True, name=f"{tag}_dnx")
    dw_a = _mm(nx, ddown, ta=True, name=f"{tag}_dwa")
    dh, dg, dshift, dscale = _pre_bwd(h, g, mod4, 3, dnx, dh_out, geo, f"{tag}_dpre")
    grads = dict(w_a=dw_a, g_qa=dg_qa, w_uq=dw_uq, g_kva=dg_kva, w_ukv=dw_ukv, gq_n=dgq_n, gq_r=dgq_r, gk_n=dgk_n,
                 gk_r=dgk_r, w_o=dw_o)
    return dh, dg, (dshift, dscale, dgate), grads


def _mla_prepare(w_a, g_qa, w_uq, g_kva, w_ukv, g_q, g_k, w_o):
    ql, kl = g_qa.shape[0], g_kva.shape[0]
    d = w_a.shape[0]
    w_a_pad = jnp.concatenate([w_a[:, :ql + kl], _rope_swap(w_a[:, ql + kl:]), jnp.zeros((d, LANE - QK_ROPE), w_a.dtype)], axis=1)
    uq = w_uq.reshape(ql, HEADS, QK_HEAD)
    uq_r = jnp.pad(_rope_swap(uq[:, :, QK_NOPE:]), ((0, 0), (0, 0), (0, LANE - QK_ROPE)))
    w_uq_pad = jnp.concatenate([uq[:, :, :QK_NOPE].reshape(ql, HEADS * LANE), uq_r.reshape(ql, HEADS * LANE)], axis=1)
    ukv = w_ukv.reshape(kl, HEADS, QK_NOPE + V_HEAD)
    w_ukv_p = jnp.concatenate([ukv[:, :, :QK_NOPE].reshape(kl, HEADS * LANE), ukv[:, :, QK_NOPE:].reshape(kl, HEADS * V_HEAD)], axis=1)

    def gains(gv):
        gv = gv.astype(F32)
        return gv[None, :QK_NOPE], jnp.pad(_rope_swap(gv[QK_NOPE:]), (0, LANE - QK_ROPE))[None]

    gq_n, gq_r = gains(g_q)
    gk_n, gk_r = gains(g_k)
    return dict(w_a=w_a_pad, g_qa=g_qa.astype(F32)[None], w_uq=w_uq_pad, g_kva=g_kva.astype(F32)[None], w_ukv=w_ukv_p,
                gq_n=gq_n, gq_r=gq_r, gk_n=gk_n, gk_r=gk_r, w_o=w_o)


def _mla_unprepare(gr):
    ql, kl = gr["g_qa"].shape[1], gr["g_kva"].shape[1]
    dw_a = jnp.concatenate([gr["w_a"][:, :ql + kl], _rope_swap(gr["w_a"][:, ql + kl:ql + kl + QK_ROPE])], axis=1)
    uqn = gr["w_uq"][:, :HEADS * LANE].reshape(ql, HEADS, LANE)
    uqr = _rope_swap(gr["w_uq"][:, HEADS * LANE:].reshape(ql, HEADS, LANE)[:, :, :QK_ROPE])
    dw_uq = jnp.concatenate([uqn, uqr], axis=2).reshape(ql, HEADS * QK_HEAD)
    ukn = gr["w_ukv"][:, :HEADS * LANE].reshape(kl, HEADS, LANE)
    ukv = gr["w_ukv"][:, HEADS * LANE:].reshape(kl, HEADS, V_HEAD)
    dw_ukv = jnp.concatenate([ukn, ukv], axis=2).reshape(kl, HEADS * (QK_NOPE + V_HEAD))

    def gains(gn, grr):
        return jnp.concatenate([gn[0], _rope_swap(grr[0, :QK_ROPE])])

    return dict(mla_w_a=dw_a, mla_g_qa=gr["g_qa"][0], mla_w_uq=dw_uq, mla_g_kva=gr["g_kva"][0], mla_w_ukv=dw_ukv,
                mla_g_q=gains(gr["gq_n"], gr["gq_r"]), mla_g_k=gains(gr["gk_n"], gr["gk_r"]), mla_w_o=gr["w_o"])


def _loss_head(h, target, geo, name):
    t, d = h.shape
    tile = geo.tile
    n_lat_tiles = 2 * geo.n_lat // tile

    def body(h_ref, t_ref, dh_ref, loss_ref):
        i = pl.program_id(0)

        @pl.when(i == 0)
        def _():
            loss_ref[...] = jnp.zeros_like(loss_ref)

        @pl.when(i < n_lat_tiles)
        def _():
            e = h_ref[...] - t_ref[...]
            dh_ref[...] = e * (1.0 / d)
            part = jnp.sum(e * e, axis=0, keepdims=True) * (0.5 / d)
            loss_ref[...] += sum(part[:, j * LANE:(j + 1) * LANE] for j in range(d // LANE))

        @pl.when(i >= n_lat_tiles)
        def _():
            dh_ref[...] = jnp.zeros_like(dh_ref)

    row = pl.BlockSpec((tile, d), lambda i: (i, 0))
    tgt = pl.BlockSpec((tile, d), lambda i: (jnp.minimum(i, n_lat_tiles - 1), 0))
    dh, loss = pl.pallas_call(
        body, name=name, grid=(t // tile,), in_specs=[row, tgt],
        out_specs=(row, pl.BlockSpec((1, LANE), lambda i: (0, 0))),
        out_shape=(jax.ShapeDtypeStruct((t, d), F32), jax.ShapeDtypeStruct((1, LANE), F32)),
        compiler_params=_params("arbitrary"),
    )(h, target)
    return jnp.sum(loss), dh


def _adamw(w, g, m, v, name):
    shape = w.shape
    cols = shape[-1]
    rows = int(np.prod(shape[:-1])) if len(shape) > 1 else 1
    w2, g2, m2, v2 = (a.reshape(rows, cols) for a in (w, g, m, v))
    tr = _pick(rows, (512, 256, 128, 64, 32, 16, 8))
    c1 = 1.0 / (1.0 - ADAM_B1 ** ADAM_STEP)
    c2 = 1.0 / (1.0 - ADAM_B2 ** ADAM_STEP)

    def body(w_ref, g_ref, m_ref, v_ref, d_ref, mo_ref, vo_ref):
        gv = g_ref[...]
        mn = ADAM_B1 * m_ref[...] + (1.0 - ADAM_B1) * gv
        vn = ADAM_B2 * v_ref[...] + (1.0 - ADAM_B2) * (gv * gv)
        d_ref[...] = -ADAM_LR * ((mn * c1) / (jnp.sqrt(vn * c2) + ADAM_EPS) + ADAM_WD * w_ref[...])
        mo_ref[...] = mn
        vo_ref[...] = vn

    blk = pl.BlockSpec((tr, cols), lambda i: (i, 0))
    sds = jax.ShapeDtypeStruct((rows, cols), F32)
    d, mo, vo = pl.pallas_call(
        body, name=name, grid=(rows // tr,), in_specs=[blk] * 4, out_specs=(blk,) * 3, out_shape=(sds,) * 3,
        compiler_params=_params("parallel"),
    )(w2, g2, m2, v2)
    return d.reshape(shape), mo.reshape(shape), vo.reshape(shape)


def _hbm():
    return pl.BlockSpec(memory_space=pl.ANY)


def _all_gather(x, name):
    def body(x_ref, out_ref, send_sems, recv_sems, local_sem):
        x, y, c = lax.axis_index("x"), lax.axis_index("y"), lax.axis_index("c")
        me, sibling = (x, y, c), (x, y, 1 - c)
        chips = [(1 - x, y), (x, 1 - y), (1 - x, 1 - y)]

        def slot(px, py, pc):
            return out_ref.at[4 * px + 2 * py + pc]

        def copy(k, block, to, src=None):
            return pltpu.make_async_remote_copy(
                src_ref=slot(*block) if src is None else src, dst_ref=slot(*block), send_sem=send_sems.at[k],
                recv_sem=recv_sems.at[k], device_id=to, device_id_type=MESH)

        mine = pltpu.make_async_copy(x_ref, slot(*me), local_sem)
        mine.start()
        first = [copy(0, me, sibling, src=x_ref)]
        first += [copy(1 + j, me, (*chip, c), src=x_ref) for j, chip in enumerate(chips)]
        for cp in first:
            cp.start()
        passed = [copy(4 + j, (*chip, c), sibling) for j, chip in enumerate(chips)]
        for j, chip in enumerate(chips):
            copy(1 + j, (*chip, c), me).wait_recv()
            passed[j].start()
        copy(0, sibling, me).wait_recv()
        for j, chip in enumerate(chips):
            copy(4 + j, (*chip, 1 - c), me).wait_recv()
        for cp in first + passed:
            cp.wait_send()
        mine.wait()

    return pl.pallas_call(
        body, name=name, out_shape=jax.ShapeDtypeStruct((N_DEV,) + x.shape, x.dtype), in_specs=[_hbm()],
        out_specs=_hbm(),
        scratch_shapes=[pltpu.SemaphoreType.DMA((7,)), pltpu.SemaphoreType.DMA((7,)), pltpu.SemaphoreType.DMA],
    )(x)


def _sibling_exchange(g, name):
    def body(g_ref, recv_ref, send_sems, recv_sems):
        x, y, c = lax.axis_index("x"), lax.axis_index("y"), lax.axis_index("c")
        copies = [
            pltpu.make_async_remote_copy(
                src_ref=g_ref.at[2 * chip + (1 - c)], dst_ref=recv_ref.at[chip], send_sem=send_sems.at[chip],
                recv_sem=recv_sems.at[chip], device_id=(x, y, 1 - c), device_id_type=MESH)
            for chip in range(4)
        ]
        for cp in copies:
            cp.start()
        for cp in copies:
            cp.wait_recv()
        for cp in copies:
            cp.wait_send()

    return pl.pallas_call(
        body, name=name, out_shape=jax.ShapeDtypeStruct((4,) + g.shape[1:], g.dtype), in_specs=[_hbm()],
        out_specs=_hbm(), scratch_shapes=[pltpu.SemaphoreType.DMA((4,)), pltpu.SemaphoreType.DMA((4,))],
    )(g)


def _chip_partials(g, recv, core, name):
    _, r, cols = g.shape
    tr = _pick(r, (512, 256, 128, 64, 32, 16))

    def body(core_ref, g_ref, r_ref, o_ref):
        o_ref[...] = (g_ref[...] + r_ref[...]).astype(o_ref.dtype)

    return pl.pallas_call(
        body, name=name,
        grid_spec=pltpu.PrefetchScalarGridSpec(
            num_scalar_prefetch=1, grid=(4, r // tr),
            in_specs=[pl.BlockSpec((None, tr, cols), lambda k, i, core_ref: (2 * k + core_ref[0], i, 0)),
                      pl.BlockSpec((None, tr, cols), lambda k, i, core_ref: (k, i, 0))],
            out_specs=pl.BlockSpec((None, tr, cols), lambda k, i, core_ref: (k, i, 0))),
        out_shape=jax.ShapeDtypeStruct((4, r, cols), BF16), compiler_params=_params("parallel", "parallel"),
    )(core, g, recv)


def _chip_exchange(p, name):
    def body(p_ref, recv_ref, send_sems, recv_sems):
        x, y, c = lax.axis_index("x"), lax.axis_index("y"), lax.axis_index("c")
        chips = [(1 - x, y), (x, 1 - y), (1 - x, 1 - y)]
        copies = [
            pltpu.make_async_remote_copy(
                src_ref=p_ref.at[2 * px + py], dst_ref=recv_ref.at[j], send_sem=send_sems.at[j],
                recv_sem=recv_sems.at[j], device_id=(px, py, c), device_id_type=MESH)
            for j, (px, py) in enumerate(chips)
        ]
        for cp in copies:
            cp.start()
        for cp in copies:
            cp.wait_recv()
        for cp in copies:
            cp.wait_send()

    return pl.pallas_call(
        body, name=name, out_shape=jax.ShapeDtypeStruct((3,) + p.shape[1:], p.dtype), in_specs=[_hbm()],
        out_specs=_hbm(), scratch_shapes=[pltpu.SemaphoreType.DMA((3,)), pltpu.SemaphoreType.DMA((3,))],
    )(p)


def _reduce_final(p, recv, chip, name):
    _, r, cols = p.shape
    tr = _pick(r, (512, 256, 128, 64, 32, 16))

    def body(chip_ref, p_ref, ry_ref, rx_ref, rxy_ref, o_ref):
        own_pair = p_ref[...].astype(F32) + ry_ref[...].astype(F32)
        o_ref[...] = own_pair + (rx_ref[...].astype(F32) + rxy_ref[...].astype(F32))

    def rel(j):
        return pl.BlockSpec((None, tr, cols), lambda i, chip_ref: (j, i, 0))

    return pl.pallas_call(
        body, name=name,
        grid_spec=pltpu.PrefetchScalarGridSpec(
            num_scalar_prefetch=1, grid=(r // tr,),
            in_specs=[pl.BlockSpec((None, tr, cols), lambda i, chip_ref: (chip_ref[0], i, 0)), rel(1), rel(0), rel(2)],
            out_specs=pl.BlockSpec((tr, cols), lambda i, chip_ref: (i, 0))),
        out_shape=jax.ShapeDtypeStruct((r, cols), F32), compiler_params=_params("parallel"),
    )(chip, p, recv, recv, recv)


def _padded(size):
    return -(-size // PACK_COLS) * PACK_COLS


def _pack_rows(total):
    rows = total // PACK_COLS
    return -(-rows // PACK_ROWS) * PACK_ROWS


def _pack_local(arrays, dtype):
    pieces = []
    for a in arrays:
        flat = a.reshape(-1).astype(dtype)
        pieces.append(jnp.pad(flat, (0, _padded(flat.size) - flat.size)))
    total = sum(p.size for p in pieces)
    rows = _pack_rows(total)
    pieces.append(jnp.zeros((rows * PACK_COLS - total,), dtype))
    return jnp.concatenate(pieces).reshape(rows, PACK_COLS)


def _unpack_gathered(buf, names, local_shapes):
    flat = buf.reshape(N_DEV, -1)
    out, off = {}, 0
    for name in names:
        shape = local_shapes[name]
        size = int(np.prod(shape))
        ax = SHARD_AXIS[name]
        blocks = flat[:, off:off + size].reshape((N_DEV,) + tuple(shape))
        full = jnp.moveaxis(blocks, 0, ax)
        out[name] = full.reshape(shape[:ax] + (N_DEV * shape[ax],) + shape[ax + 1:])
        off += _padded(size)
    return out


def _pack_grads(grads, local_shapes):
    pieces, total = [], 0
    for name in WEIGHTS:
        gfull = grads[name].astype(F32)
        if name in SHARD_AXIS:
            shape, ax = local_shapes[name], SHARD_AXIS[name]
            blocks = gfull.reshape(shape[:ax] + (N_DEV, shape[ax]) + shape[ax + 1:])
            blocks = jnp.moveaxis(blocks, ax, 0).reshape(N_DEV, -1)
        else:
            blocks = jnp.broadcast_to(gfull.reshape(1, -1), (N_DEV, gfull.size))
        size = blocks.shape[1]
        pieces.append(jnp.pad(blocks, ((0, 0), (0, _padded(size) - size))))
        total += _padded(size)
    rows = _pack_rows(total)
    pieces.append(jnp.zeros((N_DEV, rows * PACK_COLS - total), F32))
    return jnp.concatenate(pieces, axis=1).reshape(N_DEV, rows, PACK_COLS)


def _unpack_reduced(buf, local_shapes):
    flat = buf.reshape(-1)
    out, off = {}, 0
    for name in WEIGHTS:
        shape = local_shapes[name]
        size = int(np.prod(shape))
        out[name] = flat[off:off + size].reshape(shape)
        off += _padded(size)
    return out


def _silu(v):
    return v * jax.nn.sigmoid(v)


def _local_step(fw, x, c, ctx, target):
    bsz, n_lat, d = x.shape
    n_ctx = ctx.shape[1]
    assert bsz == 2
    geo = Geo(n_lat, n_ctx)
    depth = fw["w_mod"].shape[0]
    f = fw["ffn_w1"].shape[-1]
    tf, tc = _ff_tile(f), _conv_tile(d)
    tabs = _rope_tables(geo)

    h = jnp.concatenate([x.reshape(2 * n_lat, d), ctx.reshape(2 * n_ctx, d)], axis=0)
    tgt = target.reshape(2 * n_lat, d)
    cond = jnp.concatenate([c, fw["c_ctx"][None], jnp.zeros((8 - bsz - 1, d), F32)], axis=0)
    scond = _silu(cond)

    w13 = [[_interleave(jnp.concatenate([fw["ffn_w1"][i, s], fw["ffn_w3"][i, s]], axis=1), 2, tf) for s in range(2)]
           for i in range(depth)]
    w_in = [_interleave(fw["sc_w_in"][j], 3, tc) for j in range(fw["sc_w_in"].shape[0])]
    mla = [_mla_prepare(fw["mla_w_a"][j], fw["mla_g_qa"][j], fw["mla_w_uq"][j], fw["mla_g_kva"][j], fw["mla_w_ukv"][j],
                        fw["mla_g_q"][j], fw["mla_g_k"][j], fw["mla_w_o"][j]) for j in range(fw["mla_w_a"].shape[0])]
    gn = fw["g_norm"].astype(F32)

    saved, mods = [], []
    for i in range(depth):
        kind, j = i % 2, i // 2
        mod = _mm(scond, fw["w_mod"][i], name=f"l{i}_mod") + fw["b_mod"][i][None]
        mod4 = mod[:N_SEG].reshape(N_SEG, N_MOD, 1, d)
        mods.append(mod4)
        h, s1 = _ffn_fwd(h, gn[i, 0:1], mod4, 0, w13[i][0], fw["ffn_w2"][i, 0], geo, f"l{i}_f1")
        if kind == 0:
            h, s2 = _sconv_fwd(h, gn[i, 1:2], mod4, w_in[j], fw["sc_conv"][j].astype(F32), fw["sc_w_out"][j], geo, f"l{i}_sc")
        else:
            h, s2 = _mla_fwd(h, gn[i, 1:2], mod4, mla[j], i != depth - 1, tabs, geo, f"l{i}_mla")
        h, s3 = _ffn_fwd(h, gn[i, 2:3], mod4, 6, w13[i][1], fw["ffn_w2"][i, 1], geo, f"l{i}_f2")
        saved.append((s1, s2, s3))

    loss, dh = _loss_head(h, tgt, geo, "loss_head")

    g_w_mod, g_b_mod, g_g_norm = [None] * depth, [None] * depth, [None] * depth
    g_w1, g_w3, g_w2 = [None] * depth, [None] * depth, [None] * depth
    n_a, n_b = fw["sc_w_in"].shape[0], fw["mla_w_a"].shape[0]
    g_sc_in, g_sc_conv, g_sc_out, g_mla = [None] * n_a, [None] * n_a, [None] * n_a, [None] * n_b
    dscond = jnp.zeros_like(scond)
    for i in reversed(range(depth)):
        kind, j = i % 2, i // 2
        mod4 = mods[i]
        s1, s2, s3 = saved[i]
        dh, dg2, dm2, dw13_2, dw2_2 = _ffn_bwd(dh, s3, gn[i, 2:3], mod4, 6, w13[i][1], fw["ffn_w2"][i, 1], geo, f"l{i}_f2")
        if kind == 0:
            dh, dg1, dm1, dwin, dconv, dwout = _sconv_bwd(dh, s2, gn[i, 1:2], mod4, w_in[j], fw["sc_conv"][j].astype(F32),
                                                          fw["sc_w_out"][j], geo, f"l{i}_sc")
            g_sc_in[j], g_sc_conv[j], g_sc_out[j] = _deinterleave(dwin, 3, tc), dconv, dwout
        else:
            dh, dg1, dm1, gm = _mla_bwd(dh, s2, gn[i, 1:2], mod4, mla[j], i != depth - 1, tabs, geo, f"l{i}_mla")
            g_mla[j] = _mla_unprepare(gm)
        dh, dg0, dm0, dw13_1, dw2_1 = _ffn_bwd(dh, s1, gn[i, 0:1], mod4, 0, w13[i][0], fw["ffn_w2"][i, 0], geo, f"l{i}_f1")
        d13 = [_deinterleave(dw13_1, 2, tf), _deinterleave(dw13_2, 2, tf)]
        g_w1[i] = jnp.stack([d13[0][:, :f], d13[1][:, :f]])
        g_w3[i] = jnp.stack([d13[0][:, f:], d13[1][:, f:]])
        g_w2[i] = jnp.stack([dw2_1, dw2_2])
        g_g_norm[i] = jnp.concatenate([dg0, dg1, dg2], axis=0)
        dmod = jnp.concatenate(list(dm0) + list(dm1) + list(dm2), axis=1).reshape(N_SEG, N_MOD * d)
        dmod8 = jnp.concatenate([dmod, jnp.zeros((8 - N_SEG, N_MOD * d), F32)], axis=0)
        g_b_mod[i] = jnp.sum(dmod, axis=0)
        g_w_mod[i] = _mm(scond, dmod8, ta=True, name=f"l{i}_dwmod")
        dscond = dscond + _mm(dmod8, fw["w_mod"][i], tb=True, name=f"l{i}_dcond")

    sg = jax.nn.sigmoid(cond)
    dcond = dscond * (sg * (1.0 + cond * (1.0 - sg)))
    grads = {
        "c_ctx": dcond[bsz], "w_mod": jnp.stack(g_w_mod), "b_mod": jnp.stack(g_b_mod), "g_norm": jnp.stack(g_g_norm),
        "ffn_w1": jnp.stack(g_w1), "ffn_w3": jnp.stack(g_w3), "ffn_w2": jnp.stack(g_w2),
        "sc_w_in": jnp.stack(g_sc_in), "sc_conv": jnp.stack(g_sc_conv), "sc_w_out": jnp.stack(g_sc_out),
    }
    for name in ("mla_w_a", "mla_g_qa", "mla_w_uq", "mla_g_kva", "mla_w_ukv", "mla_g_q", "mla_g_k", "mla_w_o"):
        grads[name] = jnp.stack([g_mla[j][name] for j in range(n_b)])
    grad_x = dh[:2 * n_lat].reshape(x.shape)
    return loss, grad_x, grads


def kernel(x, c, ctx, c_ctx, w_mod, b_mod, g_norm, ffn_w1, ffn_w3, ffn_w2, sc_w_in, sc_conv, sc_w_out, mla_w_a, mla_g_qa, mla_w_uq, mla_g_kva, mla_w_ukv, mla_g_q, mla_g_k, mla_w_o, loss_target, m_c_ctx, m_w_mod, m_b_mod, m_g_norm, m_ffn_w1, m_ffn_w3, m_ffn_w2, m_sc_w_in, m_sc_conv, m_sc_w_out, m_mla_w_a, m_mla_g_qa, m_mla_w_uq, m_mla_g_kva, m_mla_w_ukv, m_mla_g_q, m_mla_g_k, m_mla_w_o, v_c_ctx, v_w_mod, v_b_mod, v_g_norm, v_ffn_w1, v_ffn_w3, v_ffn_w2, v_sc_w_in, v_sc_conv, v_sc_w_out, v_mla_w_a, v_mla_g_qa, v_mla_w_uq, v_mla_g_kva, v_mla_w_ukv, v_mla_g_q, v_mla_g_k, v_mla_w_o):
    w = dict(c_ctx=c_ctx, w_mod=w_mod, b_mod=b_mod, g_norm=g_norm, ffn_w1=ffn_w1, ffn_w3=ffn_w3, ffn_w2=ffn_w2,
             sc_w_in=sc_w_in, sc_conv=sc_conv, sc_w_out=sc_w_out, mla_w_a=mla_w_a, mla_g_qa=mla_g_qa, mla_w_uq=mla_w_uq,
             mla_g_kva=mla_g_kva, mla_w_ukv=mla_w_ukv, mla_g_q=mla_g_q, mla_g_k=mla_g_k, mla_w_o=mla_w_o)
    m = dict(c_ctx=m_c_ctx, w_mod=m_w_mod, b_mod=m_b_mod, g_norm=m_g_norm, ffn_w1=m_ffn_w1, ffn_w3=m_ffn_w3,
             ffn_w2=m_ffn_w2, sc_w_in=m_sc_w_in, sc_conv=m_sc_conv, sc_w_out=m_sc_w_out, mla_w_a=m_mla_w_a,
             mla_g_qa=m_mla_g_qa, mla_w_uq=m_mla_w_uq, mla_g_kva=m_mla_g_kva, mla_w_ukv=m_mla_w_ukv, mla_g_q=m_mla_g_q,
             mla_g_k=m_mla_g_k, mla_w_o=m_mla_w_o)
    v = dict(c_ctx=v_c_ctx, w_mod=v_w_mod, b_mod=v_b_mod, g_norm=v_g_norm, ffn_w1=v_ffn_w1, ffn_w3=v_ffn_w3,
             ffn_w2=v_ffn_w2, sc_w_in=v_sc_w_in, sc_conv=v_sc_conv, sc_w_out=v_sc_w_out, mla_w_a=v_mla_w_a,
             mla_g_qa=v_mla_g_qa, mla_w_uq=v_mla_w_uq, mla_g_kva=v_mla_g_kva, mla_w_ukv=v_mla_w_ukv, mla_g_q=v_mla_g_q,
             mla_g_k=v_mla_g_k, mla_w_o=v_mla_w_o)
    local_shapes = {name: tuple(w[name].shape) for name in WEIGHTS}

    big = _all_gather(_pack_local([w[n] for n in GATHER_BF16], BF16), "gather_matrices")
    small = _all_gather(_pack_local([w[n] for n in GATHER_F32], F32), "gather_vectors")
    fw = {**_unpack_gathered(big, GATHER_BF16, local_shapes), **_unpack_gathered(small, GATHER_F32, local_shapes)}
    for name in WEIGHTS:
        if name not in SHARD_AXIS:
            fw[name] = w[name]

    loss, grad_x, grads = _local_step(fw, x, c, ctx, loss_target)
    loss = lax.psum(loss, ("x", "y", "c"))

    core = lax.axis_index("c").astype(jnp.int32).reshape(1)
    chip = (2 * lax.axis_index("x") + lax.axis_index("y")).astype(jnp.int32).reshape(1)
    packed = _pack_grads(grads, local_shapes)
    from_sibling = _sibling_exchange(packed, "reduce_sibling")
    partial = _chip_partials(packed, from_sibling, core, "reduce_partials")
    from_chips = _chip_exchange(partial, "reduce_chips")
    reduced = _unpack_reduced(_reduce_final(partial, from_chips, chip, "reduce_final"), local_shapes)

    outs = [[], [], [], []]
    for name in WEIGHTS:
        delta, new_m, new_v = _adamw(w[name], reduced[name], m[name], v[name], f"adamw_{name}")
        for lst, val in zip(outs, (reduced[name], delta, new_m, new_v)):
            lst.append(val)
    return (loss, grad_x, *outs[0], *outs[1], *outs[2], *outs[3])
```

```python
import functools
import math

import jax
import jax.numpy as jnp
import numpy as np
from jax import lax
from jax.experimental import pallas as pl
from jax.experimental.pallas import tpu as pltpu

F32 = jnp.float32
BF16 = jnp.bfloat16

N_MOD = 9
HEADS = 8
QK_NOPE = 128
QK_ROPE = 64
QK_HEAD = QK_NOPE + QK_ROPE
V_HEAD = 128
GRID_W = 64
ROPE_BASE = 10000.0
QK_SCALE = QK_HEAD ** -0.5
EPS = 1e-6
ADAM_LR, ADAM_B1, ADAM_B2, ADAM_EPS, ADAM_WD, ADAM_STEP = 0.001, 0.9, 0.999, 1e-08, 0.01, 10

N_DEV = 8
N_CHIP = 4
N_SEG = 3
LANE = 128
HEAD_PAD = 2 * LANE
VMEM_LIMIT_BYTES = 48 * 1024 * 1024
MESH = pl.DeviceIdType.MESH

WEIGHTS = ["c_ctx", "w_mod", "b_mod", "g_norm", "ffn_w1", "ffn_w3", "ffn_w2", "sc_w_in", "sc_conv", "sc_w_out",
           "mla_w_a", "mla_g_qa", "mla_w_uq", "mla_g_kva", "mla_w_ukv", "mla_g_q", "mla_g_k", "mla_w_o"]
EXCHANGE = {
    "w_mod": ("last", False, True), "ffn_w1": ("mid", True, True), "ffn_w3": ("mid", True, True),
    "ffn_w2": ("mid", False, True), "sc_w_in": ("last", False, True), "sc_w_out": ("mid", False, True),
    "mla_w_a": ("mid", False, True), "mla_w_uq": ("mid", True, True), "mla_w_ukv": ("last", False, True),
    "mla_w_o": ("mid", False, True), "g_norm": ("last", False, False), "sc_conv": ("last", False, False),
    "mla_g_qa": ("mid", True, False),
}
REPLICATED = ["c_ctx", "b_mod", "mla_g_kva", "mla_g_q", "mla_g_k"]


def _pick(n, cands):
    for cand in cands:
        if n % cand == 0:
            return cand
    return n


def _params(*sem):
    return pltpu.CompilerParams(dimension_semantics=sem, vmem_limit_bytes=VMEM_LIMIT_BYTES)


class Geo:
    def __init__(self, n_lat, n_ctx):
        self.n_lat, self.n_ctx = n_lat, n_ctx
        self.rows = 2 * n_lat + 2 * n_ctx
        self.tile = n_ctx
        assert n_lat % n_ctx == 0 and n_ctx % 16 == 0
        self.mm_tile = _pick(n_lat, (512, 256, 128)) if self.rows % _pick(n_lat, (512, 256, 128)) == 0 else n_ctx
        self.big_tile = _pick(self.rows, (1536, 768, 512, 256))

    def seg(self, i, tile):
        return jnp.minimum((i * tile) // self.n_lat, N_SEG - 1)

    def seg_start(self, i, tile):
        row = i * tile
        return jnp.logical_or(row % self.n_lat == 0, row == 2 * self.n_lat) & (row <= 2 * self.n_lat)


_NT = (((1,), (1,)), ((), ()))
_NN = (((1,), (0,)), ((), ()))
_TN = (((0,), (0,)), ((), ()))


def _dot(a, b, dims):
    return lax.dot_general(a.astype(BF16), b.astype(BF16), dims, preferred_element_type=F32)


def _mm(a, b, *, ta=False, tb=False, out_dtype=F32, name, gate=None, into=None):
    (kdim, m) = a.shape if ta else a.shape[::-1]
    n = b.shape[0] if tb else b.shape[1]
    assert (b.shape[1] if tb else b.shape[0]) == kdim
    if gate is not None:
        tm = gate[4].mm_tile
    else:
        tm = _pick(m, (512, 256, 128))
    tn = _pick(n, (512, 256, 128))
    tk = _pick(kdim, (1024, 512, 256, 128))
    nk = kdim // tk
    dims = (((0 if ta else 1,), (1 if tb else 0,)), ((), ()))

    def body(*refs):
        if gate is not None:
            a_ref, b_ref, res_ref, gate_ref, o_ref, y_ref, acc_ref = refs
        elif into is not None:
            a_ref, b_ref, _, o_ref, acc_ref = refs
        else:
            a_ref, b_ref, o_ref, acc_ref = refs
        kk = pl.program_id(2)

        @pl.when(kk == 0)
        def _():
            acc_ref[...] = jnp.zeros_like(acc_ref)

        acc_ref[...] += lax.dot_general(a_ref[...].astype(BF16), b_ref[...].astype(BF16), dims,
                                        preferred_element_type=F32)

        @pl.when(kk == nk - 1)
        def _():
            acc = acc_ref[...]
            if gate is not None:
                y_ref[...] = acc.astype(y_ref.dtype)
                o_ref[...] = res_ref[...] + (gate[3] * gate_ref[...]) * acc
            else:
                o_ref[...] = acc.astype(o_ref.dtype)

    a_spec = pl.BlockSpec((tk, tm), lambda i, j, k: (k, i)) if ta else pl.BlockSpec((tm, tk), lambda i, j, k: (i, k))
    b_spec = pl.BlockSpec((tn, tk), lambda i, j, k: (j, k)) if tb else pl.BlockSpec((tk, tn), lambda i, j, k: (k, j))
    o_spec = pl.BlockSpec((tm, tn), lambda i, j, k: (i, j))
    in_specs, args, aliases = [a_spec, b_spec], [a, b], {}
    out_shape, out_specs = jax.ShapeDtypeStruct((m, n), out_dtype), o_spec
    if gate is not None:
        res, mod4, kmod, _, geo = gate
        in_specs += [o_spec, pl.BlockSpec((None, None, 1, tn), lambda i, j, k: (geo.seg(i, tm), kmod, 0, j))]
        args += [res, mod4]
        out_shape = (jax.ShapeDtypeStruct((m, n), F32), jax.ShapeDtypeStruct((m, n), BF16))
        out_specs = (o_spec, o_spec)
    if into is not None:
        buf, layer = into
        in_specs.append(pl.BlockSpec(memory_space=pl.ANY))
        args.append(buf)
        aliases = {2: 0}
        out_shape = jax.ShapeDtypeStruct(buf.shape, buf.dtype)
        out_specs = pl.BlockSpec((None, tm, tn), lambda i, j, k: (layer, i, j))
    return pl.pallas_call(
        body, name=name, grid=(m // tm, n // tn, nk), in_specs=in_specs, out_specs=out_specs, out_shape=out_shape,
        scratch_shapes=[pltpu.VMEM((tm, tn), F32)], input_output_aliases=aliases,
        compiler_params=_params("parallel", "parallel", "arbitrary"),
    )(*args)


def _tn_wide(lhs, rhs, into, idx, name):
    t, m = lhs.shape
    n = rhs.shape[1]
    tm = _pick(m, (1408, 1024, 512, 256, 128))
    tk = _pick(t, (768, 512, 256, 128))
    i0, s0 = idx

    def body(l_ref, r_ref, _, o_ref):
        kk = pl.program_id(1)
        part = lax.dot_general(l_ref[...], r_ref[...], _TN, preferred_element_type=F32)

        @pl.when(kk == 0)
        def _():
            o_ref[...] = part

        @pl.when(kk > 0)
        def _():
            o_ref[...] += part

    return pl.pallas_call(
        body, name=name, grid=(m // tm, t // tk),
        in_specs=[pl.BlockSpec((tk, tm), lambda i, k: (k, i)), pl.BlockSpec((tk, n), lambda i, k: (k, 0)),
                  pl.BlockSpec(memory_space=pl.ANY)],
        out_specs=pl.BlockSpec((None, None, tm, n), lambda i, k: (i0, s0, i, 0)),
        out_shape=jax.ShapeDtypeStruct(into.shape, into.dtype), input_output_aliases={2: 0},
        compiler_params=_params("parallel", "arbitrary"),
    )(lhs, rhs, into)


def _mod_spec(geo, tile, kmod, d):
    return pl.BlockSpec((None, None, 1, d), lambda i: (geo.seg(i, tile), kmod, 0, 0))


def _pre_fwd(h, g, mod4, k_shift, geo, name):
    t, d = h.shape
    tile = geo.tile

    def body(h_ref, g_ref, sh_ref, sc_ref, o_ref):
        hv = h_ref[...]
        r = lax.rsqrt(jnp.mean(hv * hv, axis=-1, keepdims=True) + EPS)
        y = hv * r * g_ref[...]
        o_ref[...] = (y * (1.0 + sc_ref[...]) + sh_ref[...]).astype(o_ref.dtype)

    row = pl.BlockSpec((tile, d), lambda i: (i, 0))
    return pl.pallas_call(
        body, name=name, grid=(t // tile,),
        in_specs=[row, pl.BlockSpec((1, d), lambda i: (0, 0)), _mod_spec(geo, tile, k_shift, d),
                  _mod_spec(geo, tile, k_shift + 1, d)],
        out_specs=row, out_shape=jax.ShapeDtypeStruct((t, d), BF16), compiler_params=_params("parallel"),
    )(h, g, mod4, mod4)


def _pre_bwd(h, g, mod4, k_shift, dnx, dres, geo, name):
    t, d = h.shape
    tile = geo.tile

    def body(h_ref, g_ref, sc_ref, dnx_ref, dres_ref, dh_ref, dg_ref, dsh_ref, dsc_ref):
        i = pl.program_id(0)
        hv, gv, dout = h_ref[...], g_ref[...], dnx_ref[...].astype(F32)
        r = lax.rsqrt(jnp.mean(hv * hv, axis=-1, keepdims=True) + EPS)
        xhat = hv * r
        dy = dout * (1.0 + sc_ref[...])
        u = dy * gv
        dh_ref[...] = r * (u - xhat * jnp.mean(u * xhat, axis=-1, keepdims=True)) + dres_ref[...]

        @pl.when(i == 0)
        def _():
            dg_ref[...] = jnp.zeros_like(dg_ref)

        @pl.when(geo.seg_start(i, tile))
        def _():
            dsh_ref[...] = jnp.zeros_like(dsh_ref)
            dsc_ref[...] = jnp.zeros_like(dsc_ref)

        dg_ref[...] += jnp.sum(dy * xhat, axis=0, keepdims=True)
        dsh_ref[...] += jnp.sum(dout, axis=0, keepdims=True)
        dsc_ref[...] += jnp.sum(dout * (xhat * gv), axis=0, keepdims=True)

    row = pl.BlockSpec((tile, d), lambda i: (i, 0))
    vec = pl.BlockSpec((1, d), lambda i: (0, 0))
    segv = pl.BlockSpec((None, 1, d), lambda i: (geo.seg(i, tile), 0, 0))
    return pl.pallas_call(
        body, name=name, grid=(t // tile,),
        in_specs=[row, vec, _mod_spec(geo, tile, k_shift + 1, d), row, row],
        out_specs=(row, vec, segv, segv),
        out_shape=(jax.ShapeDtypeStruct((t, d), F32), jax.ShapeDtypeStruct((1, d), F32),
                   jax.ShapeDtypeStruct((N_SEG, 1, d), F32), jax.ShapeDtypeStruct((N_SEG, 1, d), F32)),
        compiler_params=_params("arbitrary"),
    )(h, g, mod4, dnx, dres)


def _gate_bwd(dh, y, mod4, k_gate, coef, geo, name):
    t, d = dh.shape
    tile = geo.tile

    def body(dh_ref, y_ref, gt_ref, dy_ref, dgt_ref):
        i = pl.program_id(0)
        dhv = dh_ref[...]
        dy_ref[...] = ((coef * gt_ref[...]) * dhv).astype(dy_ref.dtype)

        @pl.when(geo.seg_start(i, tile))
        def _():
            dgt_ref[...] = jnp.zeros_like(dgt_ref)

        dgt_ref[...] += coef * jnp.sum(dhv * y_ref[...].astype(F32), axis=0, keepdims=True)

    row = pl.BlockSpec((tile, d), lambda i: (i, 0))
    segv = pl.BlockSpec((None, 1, d), lambda i: (geo.seg(i, tile), 0, 0))
    return pl.pallas_call(
        body, name=name, grid=(t // tile,), in_specs=[row, row, _mod_spec(geo, tile, k_gate, d)],
        out_specs=(row, segv),
        out_shape=(jax.ShapeDtypeStruct((t, d), BF16), jax.ShapeDtypeStruct((N_SEG, 1, d), F32)),
        compiler_params=_params("arbitrary"),
    )(dh, y, mod4)


def _ff_tile(f):
    return _pick(f, (256, 128))


def _ffn_up(nx, w1t, w3t, idx, geo, name):
    t, d = nx.shape
    f = w1t.shape[2]
    tm, tn = geo.big_tile, _ff_tile(f)
    i0, s0 = idx

    def body(x_ref, w1_ref, w3_ref, a_ref, b_ref, act_ref):
        xv = x_ref[...]
        a = lax.dot_general(xv, w1_ref[...], _NT, preferred_element_type=F32)
        bv = lax.dot_general(xv, w3_ref[...], _NT, preferred_element_type=F32)
        a_ref[...] = a.astype(a_ref.dtype)
        b_ref[...] = bv.astype(b_ref.dtype)
        act_ref[...] = (a * jax.nn.sigmoid(a) * bv).astype(act_ref.dtype)

    w_spec = pl.BlockSpec((None, None, tn, d), lambda i, j: (i0, s0, j, 0))
    o_spec = pl.BlockSpec((tm, tn), lambda i, j: (i, j))
    sds = jax.ShapeDtypeStruct((t, f), BF16)
    return pl.pallas_call(
        body, name=name, grid=(t // tm, f // tn), in_specs=[pl.BlockSpec((tm, d), lambda i, j: (i, 0)), w_spec, w_spec],
        out_specs=(o_spec,) * 3, out_shape=(sds,) * 3, compiler_params=_params("parallel", "parallel"),
    )(nx, w1t, w3t)


def _ffn_down(act, w2, idx, res, mod4, k_gate, geo, name):
    t, f = act.shape
    d = w2.shape[3]
    tm, tn = geo.mm_tile, _pick(d, (1024, 512, 256, 128))
    i0, s0 = idx

    def body(a_ref, w_ref, res_ref, gate_ref, o_ref, y_ref):
        acc = lax.dot_general(a_ref[...], w_ref[...], _NN, preferred_element_type=F32)
        y_ref[...] = acc.astype(y_ref.dtype)
        o_ref[...] = res_ref[...] + (0.5 * gate_ref[...]) * acc

    o_spec = pl.BlockSpec((tm, tn), lambda i, j: (i, j))
    return pl.pallas_call(
        body, name=name, grid=(t // tm, d // tn),
        in_specs=[pl.BlockSpec((tm, f), lambda i, j: (i, 0)), pl.BlockSpec((None, None, f, tn), lambda i, j: (i0, s0, 0, j)),
                  o_spec, pl.BlockSpec((None, None, 1, tn), lambda i, j: (geo.seg(i, tm), k_gate, 0, j))],
        out_specs=(o_spec, o_spec),
        out_shape=(jax.ShapeDtypeStruct((t, d), F32), jax.ShapeDtypeStruct((t, d), BF16)),
        compiler_params=_params("parallel", "parallel"),
    )(act, w2, res, mod4)


def _ffn_dact(dy, w2, a, b, idx, geo, name):
    t, d = dy.shape
    f = w2.shape[2]
    tm, tn = geo.big_tile, _ff_tile(f)
    i0, s0 = idx

    def body(dy_ref, w_ref, a_ref, b_ref, da_ref, db_ref):
        dact = lax.dot_general(dy_ref[...], w_ref[...], _NT, preferred_element_type=F32)
        av, bv = a_ref[...].astype(F32), b_ref[...].astype(F32)
        sg = jax.nn.sigmoid(av)
        da_ref[...] = (dact * bv * (sg * (1.0 + av * (1.0 - sg)))).astype(da_ref.dtype)
        db_ref[...] = (dact * (av * sg)).astype(db_ref.dtype)

    o_spec = pl.BlockSpec((tm, tn), lambda i, j: (i, j))
    sds = jax.ShapeDtypeStruct((t, f), BF16)
    return pl.pallas_call(
        body, name=name, grid=(t // tm, f // tn),
        in_specs=[pl.BlockSpec((tm, d), lambda i, j: (i, 0)), pl.BlockSpec((None, None, tn, d), lambda i, j: (i0, s0, j, 0)),
                  o_spec, o_spec],
        out_specs=(o_spec, o_spec), out_shape=(sds, sds), compiler_params=_params("parallel", "parallel"),
    )(dy, w2, a, b)


def _ffn_dnx(da, db, w1t, w3t, idx, geo, name):
    t, f = da.shape
    d = w1t.shape[3]
    tm, tn = geo.mm_tile, _pick(d, (512, 256, 128))
    i0, s0 = idx

    def body(da_ref, db_ref, w1_ref, w3_ref, o_ref):
        o_ref[...] = (lax.dot_general(da_ref[...], w1_ref[...], _NN, preferred_element_type=F32)
                      + lax.dot_general(db_ref[...], w3_ref[...], _NN, preferred_element_type=F32))

    x_spec = pl.BlockSpec((tm, f), lambda j, i: (i, 0))
    w_spec = pl.BlockSpec((None, None, f, tn), lambda j, i: (i0, s0, 0, j))
    return pl.pallas_call(
        body, name=name, grid=(d // tn, t // tm), in_specs=[x_spec, x_spec, w_spec, w_spec],
        out_specs=pl.BlockSpec((tm, tn), lambda j, i: (i, j)), out_shape=jax.ShapeDtypeStruct((t, d), F32),
        compiler_params=_params("parallel", "parallel"),
    )(da, db, w1t, w3t)


def _ffn_fwd(h, g, mod4, k0, w, idx, geo, tag):
    nx = _pre_fwd(h, g, mod4, k0, geo, f"{tag}_pre")
    a, b, act = _ffn_up(nx, w["ffn_w1"], w["ffn_w3"], idx, geo, f"{tag}_up")
    h_out, y = _ffn_down(act, w["ffn_w2"], idx, h, mod4, k0 + 2, geo, f"{tag}_down")
    return h_out, (h, nx, a, b, act, y)


def _ffn_bwd(dh_out, saved, g, mod4, k0, w, idx, gbuf, geo, tag):
    h, nx, a, b, act, y = saved
    dy, dgate = _gate_bwd(dh_out, y, mod4, k0 + 2, 0.5, geo, f"{tag}_dgate")
    da, db = _ffn_dact(dy, w["ffn_w2"], a, b, idx, geo, f"{tag}_dact")
    gbuf["ffn_w2"] = _tn_wide(act, dy, gbuf["ffn_w2"], idx, f"{tag}_dw2")
    dnx = _ffn_dnx(da, db, w["ffn_w1"], w["ffn_w3"], idx, geo, f"{tag}_dnx")
    gbuf["ffn_w1"] = _tn_wide(da, nx, gbuf["ffn_w1"], idx, f"{tag}_dw1")
    gbuf["ffn_w3"] = _tn_wide(db, nx, gbuf["ffn_w3"], idx, f"{tag}_dw3")
    dh, dg, dshift, dscale = _pre_bwd(h, g, mod4, k0, dnx, dh_out, geo, f"{tag}_dpre")
    return dh, dg, (dshift, dscale, dgate)


def _interleave(w, n_parts, tile):
    lead, cols = w.shape[:-1], w.shape[-1] // n_parts
    return w.reshape(*lead, n_parts, cols // tile, tile).swapaxes(-3, -2).reshape(*lead, n_parts * cols)


def _deinterleave(w, n_parts, tile):
    lead, cols = w.shape[:-1], w.shape[-1] // n_parts
    return w.reshape(*lead, cols // tile, n_parts, tile).swapaxes(-3, -2).reshape(*lead, n_parts * cols)


HALO = 16


def _conv_tile(c):
    return _pick(c, (256, 128))


def _conv_specs(geo, tc, t):
    tile = geo.tile
    per = tile // HALO
    last = t // HALO - 1
    cur = pl.BlockSpec((tile, 3 * tc), lambda j, i: (i, j))
    prev = pl.BlockSpec((HALO, 3 * tc), lambda j, i: (jnp.maximum(i * per - 1, 0), j))
    nxt = pl.BlockSpec((HALO, 3 * tc), lambda j, i: (jnp.minimum((i + 1) * per, last), j))
    return cur, prev, nxt


def _conv_edges(geo, i):
    tile = geo.tile
    row = i * tile
    lat = row < 2 * geo.n_lat
    first = jnp.where(lat, row % geo.n_lat == 0, (row - 2 * geo.n_lat) % geo.n_ctx == 0)
    end = row + tile
    last = jnp.where(lat, end % geo.n_lat == 0, (end - 2 * geo.n_lat) % geo.n_ctx == 0)
    return first, last


def _shift_rows(v, before, after):
    n = v.shape[0]
    rows = lax.broadcasted_iota(jnp.int32, v.shape, 0)
    down = jnp.where(rows == 0, before, pltpu.roll(v, 1, 0))
    up = jnp.where(rows == n - 1, after, pltpu.roll(v, n - 1, 0))
    return down, up


def _conv_fwd(proj, conv_w, geo, name):
    t, c3 = proj.shape
    c = c3 // 3
    tc, tile = _conv_tile(c), geo.tile

    def body(cur_ref, prev_ref, next_ref, w_ref, o_ref):
        first, last = _conv_edges(geo, pl.program_id(1))
        bv = cur_ref[:, :tc].astype(F32)
        p = cur_ref[:, tc:2 * tc].astype(F32) * cur_ref[:, 2 * tc:].astype(F32)
        p_before = prev_ref[HALO - 1:HALO, tc:2 * tc].astype(F32) * prev_ref[HALO - 1:HALO, 2 * tc:].astype(F32)
        p_after = next_ref[0:1, tc:2 * tc].astype(F32) * next_ref[0:1, 2 * tc:].astype(F32)
        p_before = jnp.where(first, 0.0, p_before)
        p_after = jnp.where(last, 0.0, p_after)
        pm1, pp1 = _shift_rows(p, p_before, p_after)
        w = w_ref[...]
        q = w[0:1] * pm1 + w[1:2] * p + w[2:3] * pp1
        o_ref[...] = (bv * q).astype(o_ref.dtype)

    cur, prev, nxt = _conv_specs(geo, tc, t)
    return pl.pallas_call(
        body, name=name, grid=(c // tc, t // tile),
        in_specs=[cur, prev, nxt, pl.BlockSpec((3, tc), lambda j, i: (0, j))],
        out_specs=pl.BlockSpec((tile, tc), lambda j, i: (i, j)), out_shape=jax.ShapeDtypeStruct((t, c), BF16),
        compiler_params=_params("parallel", "parallel"),
    )(proj, proj, proj, conv_w)


def _conv_bwd(proj, dyc, conv_w, geo, name):
    t, c3 = proj.shape
    c = c3 // 3
    tc, tile = _conv_tile(c), geo.tile

    def body(cur_ref, prev_ref, next_ref, d_ref, dprev_ref, dnext_ref, w_ref, o_ref, dw_ref):
        i = pl.program_id(1)
        first, last = _conv_edges(geo, i)
        bv = cur_ref[:, :tc].astype(F32)
        cv = cur_ref[:, tc:2 * tc].astype(F32)
        uv = cur_ref[:, 2 * tc:].astype(F32)
        p = cv * uv
        p_before = prev_ref[HALO - 1:HALO, tc:2 * tc].astype(F32) * prev_ref[HALO - 1:HALO, 2 * tc:].astype(F32)
        p_after = next_ref[0:1, tc:2 * tc].astype(F32) * next_ref[0:1, 2 * tc:].astype(F32)
        p_before = jnp.where(first, 0.0, p_before)
        p_after = jnp.where(last, 0.0, p_after)
        pm1, pp1 = _shift_rows(p, p_before, p_after)
        w = w_ref[...]
        q = w[0:1] * pm1 + w[1:2] * p + w[2:3] * pp1
        dy = d_ref[...].astype(F32)
        dq = dy * bv
        dq_before = dprev_ref[HALO - 1:HALO, :].astype(F32) * prev_ref[HALO - 1:HALO, :tc].astype(F32)
        dq_after = dnext_ref[0:1, :].astype(F32) * next_ref[0:1, :tc].astype(F32)
        dq_before = jnp.where(first, 0.0, dq_before)
        dq_after = jnp.where(last, 0.0, dq_after)
        dqm1, dqp1 = _shift_rows(dq, dq_before, dq_after)
        dp = w[0:1] * dqp1 + w[1:2] * dq + w[2:3] * dqm1
        o_ref[:, :tc] = (dy * q).astype(o_ref.dtype)
        o_ref[:, tc:2 * tc] = (dp * uv).astype(o_ref.dtype)
        o_ref[:, 2 * tc:] = (dp * cv).astype(o_ref.dtype)

        @pl.when(i == 0)
        def _():
            dw_ref[...] = jnp.zeros_like(dw_ref)

        dw_ref[0:1, :] += jnp.sum(dq * pm1, axis=0, keepdims=True)
        dw_ref[1:2, :] += jnp.sum(dq * p, axis=0, keepdims=True)
        dw_ref[2:3, :] += jnp.sum(dq * pp1, axis=0, keepdims=True)

    cur, prev, nxt = _conv_specs(geo, tc, t)
    per, lastb = tile // HALO, t // HALO - 1
    dcur = pl.BlockSpec((tile, tc), lambda j, i: (i, j))
    dprev = pl.BlockSpec((HALO, tc), lambda j, i: (jnp.maximum(i * per - 1, 0), j))
    dnext = pl.BlockSpec((HALO, tc), lambda j, i: (jnp.minimum((i + 1) * per, lastb), j))
    wspec = pl.BlockSpec((3, tc), lambda j, i: (0, j))
    return pl.pallas_call(
        body, name=name, grid=(c // tc, t // tile), in_specs=[cur, prev, nxt, dcur, dprev, dnext, wspec],
        out_specs=(cur, wspec), out_shape=(jax.ShapeDtypeStruct((t, c3), BF16), jax.ShapeDtypeStruct((3, c), F32)),
        compiler_params=_params("parallel", "arbitrary"),
    )(proj, proj, proj, dyc, dyc, dyc, conv_w)


def _sconv_fwd(h, g, mod4, w_in, conv_w, w_out, geo, tag):
    nx = _pre_fwd(h, g, mod4, 3, geo, f"{tag}_pre")
    proj = _mm(nx, w_in, out_dtype=BF16, name=f"{tag}_in")
    yc = _conv_fwd(proj, conv_w, geo, f"{tag}_conv")
    h_out, y = _mm(yc, w_out, name=f"{tag}_out", gate=(h, mod4, 5, 1.0, geo))
    return h_out, (h, nx, proj, yc, y)


def _sconv_bwd(dh_out, saved, g, mod4, w_in, conv_w, w_out, geo, tag):
    h, nx, proj, yc, y = saved
    dy, dgate = _gate_bwd(dh_out, y, mod4, 5, 1.0, geo, f"{tag}_dgate")
    dyc = _mm(dy, w_out, tb=True, out_dtype=BF16, name=f"{tag}_dyc")
    dw_out = _mm(yc, dy, ta=True, name=f"{tag}_dwout")
    dproj, dconv = _conv_bwd(proj, dyc, conv_w, geo, f"{tag}_dconv")
    dnx = _mm(dproj, w_in, tb=True, name=f"{tag}_dnx")
    dw_in = _mm(nx, dproj, ta=True, name=f"{tag}_dwin")
    dh, dg, dshift, dscale = _pre_bwd(h, g, mod4, 3, dnx, dh_out, geo, f"{tag}_dpre")
    return dh, dg, (dshift, dscale, dgate), dw_in, dconv, dw_out


def _rope_swap(v):
    nf = QK_ROPE // 4
    return v.reshape(v.shape[:-1] + (2, 2, nf)).swapaxes(-3, -2).reshape(v.shape)


def _rope_tables(geo):
    n = geo.n_lat
    nf = QK_ROPE // 4
    pos = np.arange(n)
    inv = ROPE_BASE ** (-np.arange(nf, dtype=np.float32) / nf)
    ang = np.concatenate([(pos // GRID_W)[:, None] * inv, (pos % GRID_W)[:, None] * inv], axis=1).astype(np.float32)
    cos, sin = np.cos(ang), np.sin(ang)
    zeros = np.zeros((n, LANE - QK_ROPE), np.float32)
    c_lat = np.concatenate([cos, cos, zeros], axis=1)
    s_lat = np.concatenate([-sin, sin, zeros], axis=1)
    c_ctx = np.concatenate([np.ones((2 * geo.n_ctx, QK_ROPE), np.float32), np.zeros((2 * geo.n_ctx, LANE - QK_ROPE), np.float32)], 1)
    s_ctx = np.zeros((2 * geo.n_ctx, LANE), np.float32)
    return (jnp.asarray(np.concatenate([c_lat, c_lat, c_ctx], 0)), jnp.asarray(np.concatenate([s_lat, s_lat, s_ctx], 0)))


def _swap_halves(v):
    lanes = lax.broadcasted_iota(jnp.int32, v.shape, 1)
    return jnp.where(lanes < QK_ROPE // 2, pltpu.roll(v, LANE - QK_ROPE // 2, 1), pltpu.roll(v, QK_ROPE // 2, 1))


def _latent_norm_fwd(down, g_qa, g_kva, geo, name):
    t, wd = down.shape
    ql, kl = g_qa.shape[1], g_kva.shape[1]
    tile = geo.tile

    def body(d_ref, gq_ref, gk_ref, cq_ref, ckv_ref):
        for lo, n, g_ref, o_ref in ((0, ql, gq_ref, cq_ref), (ql, kl, gk_ref, ckv_ref)):
            x = d_ref[:, lo:lo + n]
            r = lax.rsqrt(jnp.mean(x * x, axis=-1, keepdims=True) + EPS)
            o_ref[...] = (x * r * g_ref[...]).astype(o_ref.dtype)

    return pl.pallas_call(
        body, name=name, grid=(t // tile,),
        in_specs=[pl.BlockSpec((tile, wd), lambda i: (i, 0)), pl.BlockSpec((1, ql), lambda i: (0, 0)),
                  pl.BlockSpec((1, kl), lambda i: (0, 0))],
        out_specs=(pl.BlockSpec((tile, ql), lambda i: (i, 0)), pl.BlockSpec((tile, kl), lambda i: (i, 0))),
        out_shape=(jax.ShapeDtypeStruct((t, ql), BF16), jax.ShapeDtypeStruct((t, kl), BF16)),
        compiler_params=_params("parallel"),
    )(down, g_qa, g_kva)


def _latent_norm_bwd(down, g_qa, g_kva, dcqn, dckvn, dkr, geo, name):
    t, wd = down.shape
    ql, kl = g_qa.shape[1], g_kva.shape[1]
    tile = geo.tile

    def body(d_ref, gq_ref, gk_ref, dq_ref, dk_ref, dkr_ref, o_ref, dgq_ref, dgk_ref):
        i = pl.program_id(0)

        @pl.when(i == 0)
        def _():
            dgq_ref[...] = jnp.zeros_like(dgq_ref)
            dgk_ref[...] = jnp.zeros_like(dgk_ref)

        for lo, n, g_ref, dy_ref, dg_ref in ((0, ql, gq_ref, dq_ref, dgq_ref), (ql, kl, gk_ref, dk_ref, dgk_ref)):
            x = d_ref[:, lo:lo + n]
            dy = dy_ref[...].astype(F32)
            r = lax.rsqrt(jnp.mean(x * x, axis=-1, keepdims=True) + EPS)
            xhat = x * r
            u = dy * g_ref[...]
            o_ref[:, lo:lo + n] = (r * (u - xhat * jnp.mean(u * xhat, axis=-1, keepdims=True))).astype(o_ref.dtype)
            dg_ref[...] += jnp.sum(dy * xhat, axis=0, keepdims=True)
        o_ref[:, ql + kl:] = dkr_ref[...].astype(o_ref.dtype)

    def row(n):
        return pl.BlockSpec((tile, n), lambda i: (i, 0))

    def vec(n):
        return pl.BlockSpec((1, n), lambda i: (0, 0))

    return pl.pallas_call(
        body, name=name, grid=(t // tile,),
        in_specs=[row(wd), vec(ql), vec(kl), row(ql), row(kl), row(wd - ql - kl)],
        out_specs=(row(wd), vec(ql), vec(kl)),
        out_shape=(jax.ShapeDtypeStruct((t, wd), BF16), jax.ShapeDtypeStruct((1, ql), F32),
                   jax.ShapeDtypeStruct((1, kl), F32)),
        compiler_params=_params("arbitrary"),
    )(down, g_qa, g_kva, dcqn, dckvn, dkr)


def _qk_specs(geo, xn_col0, xr_col, shared_rope):
    tile = geo.tile
    xn_spec = pl.BlockSpec((tile, LANE), lambda i, hh: (i, xn_col0 + hh))
    if shared_rope:
        xr_spec = pl.BlockSpec((tile, LANE), lambda i, hh: (i, xr_col))
    else:
        xr_spec = pl.BlockSpec((tile, LANE), lambda i, hh: (i, xr_col + hh))
    vec = pl.BlockSpec((1, LANE), lambda i, hh: (0, 0))
    tab = pl.BlockSpec((tile, LANE), lambda i, hh: (i, 0))
    return xn_spec, xr_spec, vec, tab


def _qk_norm(xn, xr):
    ss = jnp.sum(xn * xn, axis=-1, keepdims=True) + jnp.sum(xr * xr, axis=-1, keepdims=True)
    return lax.rsqrt(ss * (1.0 / QK_HEAD) + EPS)


def _qk_fwd(xn_arr, xn_col0, xr_arr, xr_col, shared_rope, gn, gr, cos, sin, geo, name):
    t = xn_arr.shape[0]
    tile = geo.tile

    def body(xn_ref, xr_ref, gn_ref, gr_ref, c_ref, s_ref, o_ref):
        xn, xr = xn_ref[...].astype(F32), xr_ref[...].astype(F32)
        r = _qk_norm(xn, xr)
        yr = xr * r * gr_ref[...]
        o_ref[:, :LANE] = (xn * r * gn_ref[...]).astype(o_ref.dtype)
        o_ref[:, LANE:] = (yr * c_ref[...] + _swap_halves(yr) * s_ref[...]).astype(o_ref.dtype)

    xn_spec, xr_spec, vec, tab = _qk_specs(geo, xn_col0, xr_col, shared_rope)
    return pl.pallas_call(
        body, name=name, grid=(t // tile, HEADS), in_specs=[xn_spec, xr_spec, vec, vec, tab, tab],
        out_specs=pl.BlockSpec((tile, HEAD_PAD), lambda i, hh: (i, hh)),
        out_shape=jax.ShapeDtypeStruct((t, HEADS * HEAD_PAD), BF16), compiler_params=_params("parallel", "parallel"),
    )(xn_arr, xr_arr, gn, gr, cos, sin)


def _qk_bwd(xn_arr, xn_col0, xr_arr, xr_col, shared_rope, gn, gr, cos, sin, dout, geo, name):
    t = xn_arr.shape[0]
    tile = geo.tile

    def body(xn_ref, xr_ref, gn_ref, gr_ref, c_ref, s_ref, d_ref, dxn_ref, dxr_ref, dgn_ref, dgr_ref):
        i, hh = pl.program_id(0), pl.program_id(1)
        xn, xr = xn_ref[...].astype(F32), xr_ref[...].astype(F32)
        r = _qk_norm(xn, xr)
        xhn, xhr = xn * r, xr * r
        dyn = d_ref[:, :LANE].astype(F32)
        dro = d_ref[:, LANE:].astype(F32)
        dyr = dro * c_ref[...] + _swap_halves(dro * s_ref[...])
        un, ur = dyn * gn_ref[...], dyr * gr_ref[...]
        mean = (jnp.sum(un * xhn, axis=-1, keepdims=True) + jnp.sum(ur * xhr, axis=-1, keepdims=True)) * (1.0 / QK_HEAD)
        dxn_ref[...] = (r * (un - xhn * mean)).astype(dxn_ref.dtype)
        dxr = r * (ur - xhr * mean)

        @pl.when(jnp.logical_and(i == 0, hh == 0))
        def _():
            dgn_ref[...] = jnp.zeros_like(dgn_ref)
            dgr_ref[...] = jnp.zeros_like(dgr_ref)

        dgn_ref[...] += jnp.sum(dyn * xhn, axis=0, keepdims=True)
        dgr_ref[...] += jnp.sum(dyr * xhr, axis=0, keepdims=True)
        if shared_rope:
            @pl.when(hh == 0)
            def _():
                dxr_ref[...] = jnp.zeros_like(dxr_ref)

            dxr_ref[...] += dxr
        else:
            dxr_ref[...] = dxr.astype(dxr_ref.dtype)

    xn_spec, xr_spec, vec, tab = _qk_specs(geo, xn_col0, xr_col, shared_rope)
    head = pl.BlockSpec((tile, LANE), lambda i, hh: (i, hh))
    if shared_rope:
        dxr_spec, dxr_shape = pl.BlockSpec((tile, LANE), lambda i, hh: (i, 0)), jax.ShapeDtypeStruct((t, LANE), F32)
    else:
        dxr_spec, dxr_shape = head, jax.ShapeDtypeStruct((t, HEADS * LANE), BF16)
    return pl.pallas_call(
        body, name=name, grid=(t // tile, HEADS),
        in_specs=[xn_spec, xr_spec, vec, vec, tab, tab, pl.BlockSpec((tile, HEAD_PAD), lambda i, hh: (i, hh))],
        out_specs=(head, dxr_spec, vec, vec),
        out_shape=(jax.ShapeDtypeStruct((t, HEADS * LANE), BF16), dxr_shape, jax.ShapeDtypeStruct((1, LANE), F32),
                   jax.ShapeDtypeStruct((1, LANE), F32)),
        compiler_params=_params("arbitrary", "arbitrary"),
    )(xn_arr, xr_arr, gn, gr, cos, sin, dout)


def _attn_specs(geo):
    tq, nq = geo.n_ctx, geo.n_lat // geo.n_ctx

    def qrow(b, i):
        return jnp.where(i < nq, b * nq + i, 2 * nq + b)

    q_spec = pl.BlockSpec((tq, HEAD_PAD), lambda b, hh, i: (qrow(b, i), hh))
    kc_spec = pl.BlockSpec((geo.n_ctx, HEAD_PAD), lambda b, hh, i: (2 * nq + b, hh))
    kl_spec = pl.BlockSpec((geo.n_lat, HEAD_PAD), lambda b, hh, i: (b, hh))
    vc_spec = pl.BlockSpec((geo.n_ctx, V_HEAD), lambda b, hh, i: (2 * nq + b, HEADS + hh))
    vl_spec = pl.BlockSpec((geo.n_lat, V_HEAD), lambda b, hh, i: (b, HEADS + hh))
    o_spec = pl.BlockSpec((tq, V_HEAD), lambda b, hh, i: (qrow(b, i), hh))
    return tq, nq, q_spec, kc_spec, kl_spec, vc_spec, vl_spec, o_spec


def _attn_fwd(q, k, kv, with_ctx_q, geo, name):
    t = q.shape[0]
    tq, nq, q_spec, kc_spec, kl_spec, vc_spec, vl_spec, o_spec = _attn_specs(geo)

    def body(q_ref, kc_ref, kl_ref, vc_ref, vl_ref, o_ref):
        i = pl.program_id(2)
        qv = q_ref[...]
        s_c = _dot(qv, kc_ref[...], _NT) * QK_SCALE

        @pl.when(i < nq)
        def _():
            s_l = _dot(qv, kl_ref[...], _NT) * QK_SCALE
            m = jnp.maximum(jnp.max(s_c, axis=-1, keepdims=True), jnp.max(s_l, axis=-1, keepdims=True))
            p_c, p_l = jnp.exp(s_c - m), jnp.exp(s_l - m)
            den = jnp.sum(p_c, axis=-1, keepdims=True) + jnp.sum(p_l, axis=-1, keepdims=True)
            o = _dot(p_c, vc_ref[...], _NN) + _dot(p_l, vl_ref[...], _NN)
            o_ref[...] = (o / den).astype(o_ref.dtype)

        @pl.when(i == nq)
        def _():
            if with_ctx_q:
                m = jnp.max(s_c, axis=-1, keepdims=True)
                p_c = jnp.exp(s_c - m)
                o = _dot(p_c, vc_ref[...], _NN) / jnp.sum(p_c, axis=-1, keepdims=True)
                o_ref[...] = o.astype(o_ref.dtype)
            else:
                o_ref[...] = jnp.zeros_like(o_ref)

    return pl.pallas_call(
        body, name=name, grid=(2, HEADS, nq + 1), in_specs=[q_spec, kc_spec, kl_spec, vc_spec, vl_spec],
        out_specs=o_spec, out_shape=jax.ShapeDtypeStruct((t, HEADS * V_HEAD), BF16),
        compiler_params=_params("parallel", "parallel", "arbitrary"),
    )(q, k, k, kv, kv)


def _attn_bwd(q, k, kv, do, with_ctx_q, geo, name):
    t = q.shape[0]
    tq, nq, q_spec, kc_spec, kl_spec, vc_spec, vl_spec, o_spec = _attn_specs(geo)

    def body(q_ref, kc_ref, kl_ref, vc_ref, vl_ref, do_ref, dq_ref, dkl_ref, dkc_ref, dvl_ref, dvc_ref,
             akl_ref, akc_ref, avl_ref, avc_ref):
        i = pl.program_id(2)

        @pl.when(i == 0)
        def _():
            for ref in (akl_ref, akc_ref, avl_ref, avc_ref):
                ref[...] = jnp.zeros_like(ref)

        qv, dov = q_ref[...], do_ref[...]
        s_c = _dot(qv, kc_ref[...], _NT) * QK_SCALE
        dp_c = _dot(dov, vc_ref[...], _NT)

        def ctx_part(p_c, delta):
            ds_c = (p_c * (dp_c - delta) * QK_SCALE).astype(BF16)
            akc_ref[...] += _dot(ds_c, qv, _TN)
            avc_ref[...] += _dot(p_c, dov, _TN)
            return _dot(ds_c, kc_ref[...], _NN)

        @pl.when(i < nq)
        def _():
            s_l = _dot(qv, kl_ref[...], _NT) * QK_SCALE
            m = jnp.maximum(jnp.max(s_c, axis=-1, keepdims=True), jnp.max(s_l, axis=-1, keepdims=True))
            p_c, p_l = jnp.exp(s_c - m), jnp.exp(s_l - m)
            inv = 1.0 / (jnp.sum(p_c, axis=-1, keepdims=True) + jnp.sum(p_l, axis=-1, keepdims=True))
            p_c, p_l = p_c * inv, p_l * inv
            dp_l = _dot(dov, vl_ref[...], _NT)
            delta = jnp.sum(p_c * dp_c, axis=-1, keepdims=True) + jnp.sum(p_l * dp_l, axis=-1, keepdims=True)
            ds_l = (p_l * (dp_l - delta) * QK_SCALE).astype(BF16)
            akl_ref[...] += _dot(ds_l, qv, _TN)
            avl_ref[...] += _dot(p_l, dov, _TN)
            dq_ref[...] = (ctx_part(p_c, delta) + _dot(ds_l, kl_ref[...], _NN)).astype(dq_ref.dtype)

        @pl.when(i == nq)
        def _():
            if with_ctx_q:
                m = jnp.max(s_c, axis=-1, keepdims=True)
                p_c = jnp.exp(s_c - m)
                p_c = p_c * (1.0 / jnp.sum(p_c, axis=-1, keepdims=True))
                delta = jnp.sum(p_c * dp_c, axis=-1, keepdims=True)
                dq_ref[...] = ctx_part(p_c, delta).astype(dq_ref.dtype)
            else:
                dq_ref[...] = jnp.zeros_like(dq_ref)
            dkl_ref[...] = akl_ref[...].astype(dkl_ref.dtype)
            dkc_ref[...] = akc_ref[...].astype(dkc_ref.dtype)
            dvl_ref[...] = avl_ref[...].astype(dvl_ref.dtype)
            dvc_ref[...] = avc_ref[...].astype(dvc_ref.dtype)

    def acc_spec(rows, width):
        return pl.BlockSpec((rows, width), lambda b, hh, i: (b, hh))

    return pl.pallas_call(
        body, name=name, grid=(2, HEADS, nq + 1), in_specs=[q_spec, kc_spec, kl_spec, vc_spec, vl_spec, o_spec],
        out_specs=(q_spec, acc_spec(geo.n_lat, HEAD_PAD), acc_spec(geo.n_ctx, HEAD_PAD), acc_spec(geo.n_lat, V_HEAD),
                   acc_spec(geo.n_ctx, V_HEAD)),
        out_shape=(jax.ShapeDtypeStruct((t, HEADS * HEAD_PAD), BF16),
                   jax.ShapeDtypeStruct((2 * geo.n_lat, HEADS * HEAD_PAD), BF16),
                   jax.ShapeDtypeStruct((2 * geo.n_ctx, HEADS * HEAD_PAD), BF16),
                   jax.ShapeDtypeStruct((2 * geo.n_lat, HEADS * V_HEAD), BF16),
                   jax.ShapeDtypeStruct((2 * geo.n_ctx, HEADS * V_HEAD), BF16)),
        scratch_shapes=[pltpu.VMEM((geo.n_lat, HEAD_PAD), F32), pltpu.VMEM((geo.n_ctx, HEAD_PAD), F32),
                        pltpu.VMEM((geo.n_lat, V_HEAD), F32), pltpu.VMEM((geo.n_ctx, V_HEAD), F32)],
        compiler_params=_params("parallel", "parallel", "arbitrary"),
    )(q, k, k, kv, kv, do)


def _mla_fwd(h, g, mod4, w, with_ctx_q, tabs, geo, tag):
    cos, sin = tabs
    ql, kl = w["g_qa"].shape[1], w["g_kva"].shape[1]
    kr_col = (ql + kl) // LANE
    nx = _pre_fwd(h, g, mod4, 3, geo, f"{tag}_pre")
    down = _mm(nx, w["w_a"], name=f"{tag}_down")
    cqn, ckvn = _latent_norm_fwd(down, w["g_qa"], w["g_kva"], geo, f"{tag}_lnorm")
    qraw = _mm(cqn, w["w_uq"], out_dtype=BF16, name=f"{tag}_uq")
    kvraw = _mm(ckvn, w["w_ukv"], out_dtype=BF16, name=f"{tag}_ukv")
    q = _qk_fwd(qraw, 0, qraw, HEADS, False, w["gq_n"], w["gq_r"], cos, sin, geo, f"{tag}_qnorm")
    k = _qk_fwd(kvraw, 0, down, kr_col, True, w["gk_n"], w["gk_r"], cos, sin, geo, f"{tag}_knorm")
    o = _attn_fwd(q, k, kvraw, with_ctx_q, geo, f"{tag}_attn")
    h_out, y = _mm(o, w["w_o"], name=f"{tag}_o", gate=(h, mod4, 5, 1.0, geo))
    return h_out, (h, nx, down, cqn, ckvn, qraw, kvraw, q, k, o, y)


def _mla_bwd(dh_out, saved, g, mod4, w, with_ctx_q, tabs, geo, tag):
    cos, sin = tabs
    h, nx, down, cqn, ckvn, qraw, kvraw, q, k, o, y = saved
    ql, kl = w["g_qa"].shape[1], w["g_kva"].shape[1]
    kr_col = (ql + kl) // LANE
    dy, dgate = _gate_bwd(dh_out, y, mod4, 5, 1.0, geo, f"{tag}_dgate")
    do = _mm(dy, w["w_o"], tb=True, out_dtype=BF16, name=f"{tag}_do")
    dw_o = _mm(o, dy, ta=True, name=f"{tag}_dwo")
    dq, dk_lat, dk_ctx, dv_lat, dv_ctx = _attn_bwd(q, k, kvraw, do, with_ctx_q, geo, f"{tag}_dattn")
    dk = jnp.concatenate([dk_lat, dk_ctx], axis=0)
    dqn, dqr, dgq_n, dgq_r = _qk_bwd(qraw, 0, qraw, HEADS, False, w["gq_n"], w["gq_r"], cos, sin, dq, geo, f"{tag}_dqnorm")
    dkn, dkr, dgk_n, dgk_r = _qk_bwd(kvraw, 0, down, kr_col, True, w["gk_n"], w["gk_r"], cos, sin, dk, geo, f"{tag}_dknorm")
    dqraw = jnp.concatenate([dqn, dqr], axis=1)
    dkvraw = jnp.concatenate([dkn, jnp.concatenate([dv_lat, dv_ctx], axis=0)], axis=1)
    dcqn = _mm(dqraw, w["w_uq"], tb=True, out_dtype=BF16, name=f"{tag}_dcqn")
    dw_uq = _mm(cqn, dqraw, ta=True, name=f"{tag}_dwuq")
    dckvn = _mm(dkvraw, w["w_ukv"], tb=True, out_dtype=BF16, name=f"{tag}_dckvn")
    dw_ukv = _mm(ckvn, dkvraw, ta=True, name=f"{tag}_dwukv")
    ddown, dg_qa, dg_kva = _latent_norm_bwd(down, w["g_qa"], w["g_kva"], dcqn, dckvn, dkr, geo, f"{tag}_dlnorm")
    dnx = _mm(ddown, w["w_a"], tb=True, name=f"{tag}_dnx")
    dw_a = _mm(nx, ddown, ta=True, name=f"{tag}_dwa")
    dh, dg, dshift, dscale = _pre_bwd(h, g, mod4, 3, dnx, dh_out, geo, f"{tag}_dpre")
    grads = dict(w_a=dw_a, g_qa=dg_qa, w_uq=dw_uq, g_kva=dg_kva, w_ukv=dw_ukv, gq_n=dgq_n, gq_r=dgq_r, gk_n=dgk_n,
                 gk_r=dgk_r, w_o=dw_o)
    return dh, dg, (dshift, dscale, dgate), grads


def _mla_prepare(w_a, g_qa, w_uq, g_kva, w_ukv, g_q, g_k, w_o):
    ql, kl = g_qa.shape[0], g_kva.shape[0]
    d = w_a.shape[0]
    w_a_pad = jnp.concatenate([w_a[:, :ql + kl], _rope_swap(w_a[:, ql + kl:]), jnp.zeros((d, LANE - QK_ROPE), w_a.dtype)], axis=1)
    uq = w_uq.reshape(ql, HEADS, QK_HEAD)
    uq_r = jnp.pad(_rope_swap(uq[:, :, QK_NOPE:]), ((0, 0), (0, 0), (0, LANE - QK_ROPE)))
    w_uq_pad = jnp.concatenate([uq[:, :, :QK_NOPE].reshape(ql, HEADS * LANE), uq_r.reshape(ql, HEADS * LANE)], axis=1)
    ukv = w_ukv.reshape(kl, HEADS, QK_NOPE + V_HEAD)
    w_ukv_p = jnp.concatenate([ukv[:, :, :QK_NOPE].reshape(kl, HEADS * LANE), ukv[:, :, QK_NOPE:].reshape(kl, HEADS * V_HEAD)], axis=1)

    def gains(gv):
        gv = gv.astype(F32)
        return gv[None, :QK_NOPE], jnp.pad(_rope_swap(gv[QK_NOPE:]), (0, LANE - QK_ROPE))[None]

    gq_n, gq_r = gains(g_q)
    gk_n, gk_r = gains(g_k)
    return dict(w_a=w_a_pad, g_qa=g_qa.astype(F32)[None], w_uq=w_uq_pad, g_kva=g_kva.astype(F32)[None], w_ukv=w_ukv_p,
                gq_n=gq_n, gq_r=gq_r, gk_n=gk_n, gk_r=gk_r, w_o=w_o)


def _mla_unprepare(gr):
    ql, kl = gr["g_qa"].shape[1], gr["g_kva"].shape[1]
    dw_a = jnp.concatenate([gr["w_a"][:, :ql + kl], _rope_swap(gr["w_a"][:, ql + kl:ql + kl + QK_ROPE])], axis=1)
    uqn = gr["w_uq"][:, :HEADS * LANE].reshape(ql, HEADS, LANE)
    uqr = _rope_swap(gr["w_uq"][:, HEADS * LANE:].reshape(ql, HEADS, LANE)[:, :, :QK_ROPE])
    dw_uq = jnp.concatenate([uqn, uqr], axis=2).reshape(ql, HEADS * QK_HEAD)
    ukn = gr["w_ukv"][:, :HEADS * LANE].reshape(kl, HEADS, LANE)
    ukv = gr["w_ukv"][:, HEADS * LANE:].reshape(kl, HEADS, V_HEAD)
    dw_ukv = jnp.concatenate([ukn, ukv], axis=2).reshape(kl, HEADS * (QK_NOPE + V_HEAD))

    def gains(gn, grr):
        return jnp.concatenate([gn[0], _rope_swap(grr[0, :QK_ROPE])])

    return dict(mla_w_a=dw_a, mla_g_qa=gr["g_qa"][0], mla_w_uq=dw_uq, mla_g_kva=gr["g_kva"][0], mla_w_ukv=dw_ukv,
                mla_g_q=gains(gr["gq_n"], gr["gq_r"]), mla_g_k=gains(gr["gk_n"], gr["gk_r"]), mla_w_o=gr["w_o"])


def _loss_head(h, target, geo, name):
    t, d = h.shape
    tile = geo.tile
    n_lat_tiles = 2 * geo.n_lat // tile

    def body(h_ref, t_ref, dh_ref, loss_ref):
        i = pl.program_id(0)

        @pl.when(i == 0)
        def _():
            loss_ref[...] = jnp.zeros_like(loss_ref)

        @pl.when(i < n_lat_tiles)
        def _():
            e = h_ref[...] - t_ref[...]
            dh_ref[...] = e * (1.0 / d)
            part = jnp.sum(e * e, axis=0, keepdims=True) * (0.5 / d)
            loss_ref[...] += sum(part[:, j * LANE:(j + 1) * LANE] for j in range(d // LANE))

        @pl.when(i >= n_lat_tiles)
        def _():
            dh_ref[...] = jnp.zeros_like(dh_ref)

    row = pl.BlockSpec((tile, d), lambda i: (i, 0))
    tgt = pl.BlockSpec((tile, d), lambda i: (jnp.minimum(i, n_lat_tiles - 1), 0))
    dh, loss = pl.pallas_call(
        body, name=name, grid=(t // tile,), in_specs=[row, tgt],
        out_specs=(row, pl.BlockSpec((1, LANE), lambda i: (0, 0))),
        out_shape=(jax.ShapeDtypeStruct((t, d), F32), jax.ShapeDtypeStruct((1, LANE), F32)),
        compiler_params=_params("arbitrary"),
    )(h, target)
    return jnp.sum(loss), dh


def _adamw(w, g, m, v, name):
    shape = w.shape
    cols = shape[-1]
    rows = int(np.prod(shape[:-1])) if len(shape) > 1 else 1
    w2, g2, m2, v2 = (a.reshape(rows, cols) for a in (w, g, m, v))
    tr = _pick(rows, (512, 256, 128, 64, 32, 16, 8))
    c1 = 1.0 / (1.0 - ADAM_B1 ** ADAM_STEP)
    c2 = 1.0 / (1.0 - ADAM_B2 ** ADAM_STEP)

    def body(w_ref, g_ref, m_ref, v_ref, d_ref, mo_ref, vo_ref):
        gv = g_ref[...]
        mn = ADAM_B1 * m_ref[...] + (1.0 - ADAM_B1) * gv
        vn = ADAM_B2 * v_ref[...] + (1.0 - ADAM_B2) * (gv * gv)
        d_ref[...] = -ADAM_LR * ((mn * c1) / (jnp.sqrt(vn * c2) + ADAM_EPS) + ADAM_WD * w_ref[...])
        mo_ref[...] = mn
        vo_ref[...] = vn

    blk = pl.BlockSpec((tr, cols), lambda i: (i, 0))
    sds = jax.ShapeDtypeStruct((rows, cols), F32)
    d, mo, vo = pl.pallas_call(
        body, name=name, grid=(rows // tr,), in_specs=[blk] * 4, out_specs=(blk,) * 3, out_shape=(sds,) * 3,
        compiler_params=_params("parallel"),
    )(w2, g2, m2, v2)
    return d.reshape(shape), mo.reshape(shape), vo.reshape(shape)


SHARD_AXIS = {
    "w_mod": 2, "g_norm": 2, "ffn_w1": 3, "ffn_w3": 3, "ffn_w2": 2, "sc_w_in": 2, "sc_conv": 2, "sc_w_out": 1,
    "mla_w_a": 1, "mla_g_qa": 1, "mla_w_uq": 2, "mla_w_ukv": 2, "mla_w_o": 1,
}
HIDDEN_MAJOR = ("ffn_w1", "ffn_w3")


def _view(name, arr, swapped=False):
    form, swap, _ = EXCHANGE[name]
    if swap and not swapped:
        arr = jnp.swapaxes(arr, -1, -2)
    if form == "mid":
        arr = arr.reshape((-1,) + arr.shape[-2:])
        return jnp.pad(arr, ((0, 0), (0, 0), (0, -arr.shape[-1] % LANE)))
    arr = arr.reshape(-1, arr.shape[-1])
    return jnp.pad(arr, ((0, -arr.shape[0] % 16), (0, 0)))


def _unview(name, view, shape, keep_swapped=False):
    form, swap, _ = EXCHANGE[name]
    shape = shape[:-2] + (shape[-1], shape[-2]) if swap else shape
    if form == "mid":
        view = view[:, :, :shape[-1]]
    else:
        view = view[:int(np.prod(shape[:-1]))]
    arr = view.reshape(shape)
    return arr if (not swap or keep_swapped) else jnp.swapaxes(arr, -1, -2)


def _full_shape(name, local_shape):
    ax = SHARD_AXIS[name]
    return local_shape[:ax] + (N_DEV * local_shape[ax],) + local_shape[ax + 1:]


def _win(ref, form, n, j):
    start = j * n
    if not isinstance(start, int):
        start = pl.multiple_of(start, LANE if form == "last" else math.gcd(n, 16))
    if form == "mid":
        return ref.at[:, pl.ds(start, n), :]
    return ref.at[:, pl.ds(start, n)]


def _hbm():
    return pl.BlockSpec(memory_space=pl.ANY)


def _windows(view, count, of):
    return view.shape[:1] + (view.shape[1] * count // of,) + view.shape[2:]


def _all_gather(views, forms, name):
    na = len(views)

    def body(*refs):
        x_refs, out_refs = refs[:na], refs[na:2 * na]
        send_sems, recv_sems, local_sems = refs[2 * na:]
        x, y, c = lax.axis_index("x"), lax.axis_index("y"), lax.axis_index("c")
        me, sibling = (x, y, c), (x, y, 1 - c)
        chips = [(1 - x, y), (x, 1 - y), (1 - x, 1 - y)]

        def copy(a, k, block, to, from_input):
            dst = _win(out_refs[a], forms[a], views[a].shape[1], 4 * block[0] + 2 * block[1] + block[2])
            return pltpu.make_async_remote_copy(
                src_ref=x_refs[a] if from_input else dst, dst_ref=dst, send_sem=send_sems.at[a, k],
                recv_sem=recv_sems.at[a, k], device_id=to, device_id_type=MESH)

        mine = [pltpu.make_async_copy(x_refs[a], _win(out_refs[a], forms[a], views[a].shape[1], 4 * x + 2 * y + c),
                                      local_sems.at[a]) for a in range(na)]
        for cp in mine:
            cp.start()
        first = []
        for a in range(na):
            first.append(copy(a, 0, me, sibling, True))
            first += [copy(a, 1 + j, me, (*chip, c), True) for j, chip in enumerate(chips)]
        for cp in first:
            cp.start()
        passed = []
        for j, chip in enumerate(chips):
            for a in range(na):
                copy(a, 1 + j, (*chip, c), me, False).wait_recv()
                fwd = copy(a, 4 + j, (*chip, c), sibling, False)
                fwd.start()
                passed.append(fwd)
        for a in range(na):
            copy(a, 0, sibling, me, False).wait_recv()
            for j, chip in enumerate(chips):
                copy(a, 4 + j, (*chip, 1 - c), me, False).wait_recv()
        for cp in first + passed:
            cp.wait_send()
        for cp in mine:
            cp.wait()

    return pl.pallas_call(
        body, name=name, in_specs=[_hbm()] * na, out_specs=tuple([_hbm()] * na),
        out_shape=tuple(jax.ShapeDtypeStruct(_windows(v, N_DEV, 1), v.dtype) for v in views),
        scratch_shapes=[pltpu.SemaphoreType.DMA((na, 7)), pltpu.SemaphoreType.DMA((na, 7)), pltpu.SemaphoreType.DMA((na,))],
    )(*views)


def _sibling_exchange(fulls, forms, name):
    na = len(fulls)
    widths = [f.shape[1] // N_DEV for f in fulls]

    def body(*refs):
        g_refs, r_refs, send_sems, recv_sems = refs[:na], refs[na:2 * na], refs[2 * na], refs[2 * na + 1]
        x, y, c = lax.axis_index("x"), lax.axis_index("y"), lax.axis_index("c")
        copies = [
            pltpu.make_async_remote_copy(
                src_ref=_win(g_refs[a], forms[a], widths[a], 2 * chip + (1 - c)),
                dst_ref=_win(r_refs[a], forms[a], widths[a], chip), send_sem=send_sems.at[a, chip],
                recv_sem=recv_sems.at[a, chip], device_id=(x, y, 1 - c), device_id_type=MESH)
            for a in range(na) for chip in range(N_CHIP)
        ]
        for cp in copies:
            cp.start()
        for cp in copies:
            cp.wait_recv()
        for cp in copies:
            cp.wait_send()

    return pl.pallas_call(
        body, name=name, in_specs=[_hbm()] * na, out_specs=tuple([_hbm()] * na),
        out_shape=tuple(jax.ShapeDtypeStruct(_windows(f, N_CHIP, N_DEV), f.dtype) for f in fulls),
        scratch_shapes=[pltpu.SemaphoreType.DMA((na, N_CHIP)), pltpu.SemaphoreType.DMA((na, N_CHIP))],
    )(*fulls)


def _chip_exchange(parts, forms, name):
    na = len(parts)
    widths = [p.shape[1] // N_CHIP for p in parts]

    def body(*refs):
        p_refs, r_refs, send_sems, recv_sems = refs[:na], refs[na:2 * na], refs[2 * na], refs[2 * na + 1]
        x, y, c = lax.axis_index("x"), lax.axis_index("y"), lax.axis_index("c")
        chips = [(1 - x, y), (x, 1 - y), (1 - x, 1 - y)]
        copies = [
            pltpu.make_async_remote_copy(
                src_ref=_win(p_refs[a], forms[a], widths[a], 2 * px + py), dst_ref=_win(r_refs[a], forms[a], widths[a], j),
                send_sem=send_sems.at[a, j], recv_sem=recv_sems.at[a, j], device_id=(px, py, c), device_id_type=MESH)
            for a in range(na) for j, (px, py) in enumerate(chips)
        ]
        for cp in copies:
            cp.start()
        for cp in copies:
            cp.wait_recv()
        for cp in copies:
            cp.wait_send()

    return pl.pallas_call(
        body, name=name, in_specs=[_hbm()] * na, out_specs=tuple([_hbm()] * na),
        out_shape=tuple(jax.ShapeDtypeStruct(_windows(p, 3, N_CHIP), p.dtype) for p in parts),
        scratch_shapes=[pltpu.SemaphoreType.DMA((na, 3)), pltpu.SemaphoreType.DMA((na, 3))],
    )(*parts)


def _sum_tiles(view, form, n):
    if form == "mid":
        tr = n
        while tr * view.shape[2] * 4 > 2 * 1024 * 1024 and tr % 32 == 0:
            tr //= 2
        return 1, tr
    return _pick(view.shape[0], (512, 256, 128, 64, 32, 16)), n


def _window_spec(form, tl, tr, rest, window_of):
    if form == "mid":
        return lambda per: pl.BlockSpec((None, tr) + rest, lambda l, k, i, s: (l, window_of(k, s) * per + i, 0))
    return lambda per: pl.BlockSpec((tl, tr), lambda l, k, i, s: (l, window_of(k, s)))


def _chip_partials(g, recv, core, form, name):
    n = g.shape[1] // N_DEV
    tl, tr = _sum_tiles(g, form, n)
    per = n // tr
    rest = tuple(g.shape[2:])

    def body(core_ref, g_ref, r_ref, o_ref):
        o_ref[...] = (g_ref[...] + r_ref[...]).astype(o_ref.dtype)

    own = _window_spec(form, tl, tr, rest, lambda k, s: 2 * k + s[0])(per)
    by_chip = _window_spec(form, tl, tr, rest, lambda k, s: k)(per)
    return pl.pallas_call(
        body, name=name,
        grid_spec=pltpu.PrefetchScalarGridSpec(
            num_scalar_prefetch=1, grid=(g.shape[0] // tl, N_CHIP, per), in_specs=[own, by_chip], out_specs=by_chip),
        out_shape=jax.ShapeDtypeStruct(recv.shape, BF16), compiler_params=_params("parallel", "parallel", "parallel"),
    )(core, g, recv)


def _reduce_final(p, recv, chip, form, name):
    n = p.shape[1] // N_CHIP
    tl, tr = _sum_tiles(p, form, n)
    per = n // tr
    rest = tuple(p.shape[2:])

    def body(chip_ref, p_ref, ry_ref, rx_ref, rxy_ref, o_ref):
        own_pair = p_ref[...].astype(F32) + ry_ref[...].astype(F32)
        o_ref[...] = own_pair + (rx_ref[...].astype(F32) + rxy_ref[...].astype(F32))

    def rel(j):
        return _window_spec(form, tl, tr, rest, lambda k, s: j)(per)

    own = _window_spec(form, tl, tr, rest, lambda k, s: s[0])(per)
    return pl.pallas_call(
        body, name=name,
        grid_spec=pltpu.PrefetchScalarGridSpec(
            num_scalar_prefetch=1, grid=(p.shape[0] // tl, 1, per), in_specs=[own, rel(1), rel(0), rel(2)],
            out_specs=rel(0)),
        out_shape=jax.ShapeDtypeStruct(p.shape[:1] + (n,) + p.shape[2:], F32),
        compiler_params=_params("parallel", "parallel", "parallel"),
    )(chip, p, recv, recv, recv)


def _pack_replicated(arrays):
    pieces = []
    for a in arrays:
        flat = a.reshape(-1).astype(F32)
        pieces.append(jnp.pad(flat, (0, -flat.size % LANE)))
    total = sum(p.size for p in pieces)
    pieces.append(jnp.zeros((-total % (16 * LANE),), F32))
    return jnp.concatenate(pieces).reshape(-1, LANE)


def _unpack_replicated(buf, shapes):
    flat, out, off = buf.reshape(-1), [], 0
    for shape in shapes:
        size = int(np.prod(shape))
        out.append(flat[off:off + size].reshape(shape))
        off += size + (-size % LANE)
    return out


def _silu(v):
    return v * jax.nn.sigmoid(v)


def _local_step(fw, x, c, ctx, target):
    bsz, n_lat, d = x.shape
    n_ctx = ctx.shape[1]
    assert bsz == 2
    geo = Geo(n_lat, n_ctx)
    depth = fw["w_mod"].shape[0]
    tc = _conv_tile(d)
    tabs = _rope_tables(geo)

    h = jnp.concatenate([x.reshape(2 * n_lat, d), ctx.reshape(2 * n_ctx, d)], axis=0)
    tgt = target.reshape(2 * n_lat, d)
    cond = jnp.concatenate([c, fw["c_ctx"][None], jnp.zeros((8 - bsz - 1, d), F32)], axis=0)
    scond = _silu(cond)

    w_in = [_interleave(fw["sc_w_in"][j], 3, tc) for j in range(fw["sc_w_in"].shape[0])]
    mla = [_mla_prepare(fw["mla_w_a"][j], fw["mla_g_qa"][j], fw["mla_w_uq"][j], fw["mla_g_kva"][j], fw["mla_w_ukv"][j],
                        fw["mla_g_q"][j], fw["mla_g_k"][j], fw["mla_w_o"][j]) for j in range(fw["mla_w_a"].shape[0])]
    gn = fw["g_norm"].astype(F32)

    saved, mods = [], []
    for i in range(depth):
        kind, j = i % 2, i // 2
        mod = _mm(scond, fw["w_mod"][i], name=f"l{i}_mod") + fw["b_mod"][i][None]
        mod4 = mod[:N_SEG].reshape(N_SEG, N_MOD, 1, d)
        mods.append(mod4)
        h, s1 = _ffn_fwd(h, gn[i, 0:1], mod4, 0, fw, (i, 0), geo, f"l{i}_f1")
        if kind == 0:
            h, s2 = _sconv_fwd(h, gn[i, 1:2], mod4, w_in[j], fw["sc_conv"][j].astype(F32), fw["sc_w_out"][j], geo, f"l{i}_sc")
        else:
            h, s2 = _mla_fwd(h, gn[i, 1:2], mod4, mla[j], i != depth - 1, tabs, geo, f"l{i}_mla")
        h, s3 = _ffn_fwd(h, gn[i, 2:3], mod4, 6, fw, (i, 1), geo, f"l{i}_f2")
        saved.append((s1, s2, s3))

    loss, dh = _loss_head(h, tgt, geo, "loss_head")

    gbuf = {name: lax.empty(fw[name].shape, F32) for name in ("ffn_w1", "ffn_w3", "ffn_w2", "w_mod")}
    g_b_mod, g_g_norm = [None] * depth, [None] * depth
    n_a, n_b = fw["sc_w_in"].shape[0], fw["mla_w_a"].shape[0]
    g_sc_in, g_sc_conv, g_sc_out, g_mla = [None] * n_a, [None] * n_a, [None] * n_a, [None] * n_b
    dscond = jnp.zeros_like(scond)
    for i in reversed(range(depth)):
        kind, j = i % 2, i // 2
        mod4 = mods[i]
        s1, s2, s3 = saved[i]
        dh, dg2, dm2 = _ffn_bwd(dh, s3, gn[i, 2:3], mod4, 6, fw, (i, 1), gbuf, geo, f"l{i}_f2")
        if kind == 0:
            dh, dg1, dm1, dwin, dconv, dwout = _sconv_bwd(dh, s2, gn[i, 1:2], mod4, w_in[j], fw["sc_conv"][j].astype(F32),
                                                          fw["sc_w_out"][j], geo, f"l{i}_sc")
            g_sc_in[j], g_sc_conv[j], g_sc_out[j] = _deinterleave(dwin, 3, tc), dconv, dwout
        else:
            dh, dg1, dm1, gm = _mla_bwd(dh, s2, gn[i, 1:2], mod4, mla[j], i != depth - 1, tabs, geo, f"l{i}_mla")
            g_mla[j] = _mla_unprepare(gm)
        dh, dg0, dm0 = _ffn_bwd(dh, s1, gn[i, 0:1], mod4, 0, fw, (i, 0), gbuf, geo, f"l{i}_f1")
        g_g_norm[i] = jnp.concatenate([dg0, dg1, dg2], axis=0)
        dmod = jnp.concatenate(list(dm0) + list(dm1) + list(dm2), axis=1).reshape(N_SEG, N_MOD * d)
        dmod8 = jnp.concatenate([dmod, jnp.zeros((8 - N_SEG, N_MOD * d), F32)], axis=0)
        g_b_mod[i] = jnp.sum(dmod, axis=0)
        gbuf["w_mod"] = _mm(scond, dmod8, ta=True, name=f"l{i}_dwmod", into=(gbuf["w_mod"], i))
        dscond = dscond + _mm(dmod8, fw["w_mod"][i], tb=True, name=f"l{i}_dcond")

    sg = jax.nn.sigmoid(cond)
    dcond = dscond * (sg * (1.0 + cond * (1.0 - sg)))
    grads = {
        "c_ctx": dcond[bsz], "w_mod": gbuf["w_mod"], "b_mod": jnp.stack(g_b_mod), "g_norm": jnp.stack(g_g_norm),
        "ffn_w1": gbuf["ffn_w1"], "ffn_w3": gbuf["ffn_w3"], "ffn_w2": gbuf["ffn_w2"],
        "sc_w_in": jnp.stack(g_sc_in), "sc_conv": jnp.stack(g_sc_conv), "sc_w_out": jnp.stack(g_sc_out),
    }
    for name in ("mla_w_a", "mla_g_qa", "mla_w_uq", "mla_g_kva", "mla_w_ukv", "mla_g_q", "mla_g_k", "mla_w_o"):
        grads[name] = jnp.stack([g_mla[j][name] for j in range(n_b)])
    grad_x = dh[:2 * n_lat].reshape(x.shape)
    return loss, grad_x, grads


def kernel(x, c, ctx, c_ctx, w_mod, b_mod, g_norm, ffn_w1, ffn_w3, ffn_w2, sc_w_in, sc_conv, sc_w_out, mla_w_a, mla_g_qa, mla_w_uq, mla_g_kva, mla_w_ukv, mla_g_q, mla_g_k, mla_w_o, loss_target, m_c_ctx, m_w_mod, m_b_mod, m_g_norm, m_ffn_w1, m_ffn_w3, m_ffn_w2, m_sc_w_in, m_sc_conv, m_sc_w_out, m_mla_w_a, m_mla_g_qa, m_mla_w_uq, m_mla_g_kva, m_mla_w_ukv, m_mla_g_q, m_mla_g_k, m_mla_w_o, v_c_ctx, v_w_mod, v_b_mod, v_g_norm, v_ffn_w1, v_ffn_w3, v_ffn_w2, v_sc_w_in, v_sc_conv, v_sc_w_out, v_mla_w_a, v_mla_g_qa, v_mla_w_uq, v_mla_g_kva, v_mla_w_ukv, v_mla_g_q, v_mla_g_k, v_mla_w_o):
    w = dict(c_ctx=c_ctx, w_mod=w_mod, b_mod=b_mod, g_norm=g_norm, ffn_w1=ffn_w1, ffn_w3=ffn_w3, ffn_w2=ffn_w2,
             sc_w_in=sc_w_in, sc_conv=sc_conv, sc_w_out=sc_w_out, mla_w_a=mla_w_a, mla_g_qa=mla_g_qa, mla_w_uq=mla_w_uq,
             mla_g_kva=mla_g_kva, mla_w_ukv=mla_w_ukv, mla_g_q=mla_g_q, mla_g_k=mla_g_k, mla_w_o=mla_w_o)
    m = dict(c_ctx=m_c_ctx, w_mod=m_w_mod, b_mod=m_b_mod, g_norm=m_g_norm, ffn_w1=m_ffn_w1, ffn_w3=m_ffn_w3,
             ffn_w2=m_ffn_w2, sc_w_in=m_sc_w_in, sc_conv=m_sc_conv, sc_w_out=m_sc_w_out, mla_w_a=m_mla_w_a,
             mla_g_qa=m_mla_g_qa, mla_w_uq=m_mla_w_uq, mla_g_kva=m_mla_g_kva, mla_w_ukv=m_mla_w_ukv, mla_g_q=m_mla_g_q,
             mla_g_k=m_mla_g_k, mla_w_o=m_mla_w_o)
    v = dict(c_ctx=v_c_ctx, w_mod=v_w_mod, b_mod=v_b_mod, g_norm=v_g_norm, ffn_w1=v_ffn_w1, ffn_w3=v_ffn_w3,
             ffn_w2=v_ffn_w2, sc_w_in=v_sc_w_in, sc_conv=v_sc_conv, sc_w_out=v_sc_w_out, mla_w_a=v_mla_w_a,
             mla_g_qa=v_mla_g_qa, mla_w_uq=v_mla_w_uq, mla_g_kva=v_mla_g_kva, mla_w_ukv=v_mla_w_ukv, mla_g_q=v_mla_g_q,
             mla_g_k=v_mla_g_k, mla_w_o=v_mla_w_o)
    sharded = list(EXCHANGE)
    forms = [EXCHANGE[name][0] for name in sharded]

    local_views = [_view(name, w[name].astype(BF16 if EXCHANGE[name][2] else F32)) for name in sharded]
    full_views = _all_gather(local_views, forms, "gather_weights")
    fw = {name: _unview(name, fv, _full_shape(name, tuple(w[name].shape)), keep_swapped=name in HIDDEN_MAJOR)
          for name, fv in zip(sharded, full_views)}
    for name in REPLICATED:
        fw[name] = w[name]

    loss, grad_x, grads = _local_step(fw, x, c, ctx, loss_target)
    loss = lax.psum(loss, ("x", "y", "c"))

    core = lax.axis_index("c").astype(jnp.int32).reshape(1)
    chip = (2 * lax.axis_index("x") + lax.axis_index("y")).astype(jnp.int32).reshape(1)
    g_views = [_view(name, grads[name], swapped=name in HIDDEN_MAJOR) for name in sharded]
    rep = _pack_replicated([grads[name] for name in REPLICATED])
    g_views.append(jnp.tile(rep[None], (1, N_DEV, 1)))
    names, forms = sharded + ["replicated"], forms + ["mid"]
    from_sibling = _sibling_exchange(g_views, forms, "reduce_sibling")
    partials = [_chip_partials(g, r, core, f, f"partial_{n}") for g, r, f, n in zip(g_views, from_sibling, forms, names)]
    from_chips = _chip_exchange(partials, forms, "reduce_chips")
    reduced_views = [_reduce_final(p, r, chip, f, f"final_{n}") for p, r, f, n in zip(partials, from_chips, forms, names)]
    reduced = {name: _unview(name, rv, tuple(w[name].shape)) for name, rv in zip(sharded, reduced_views)}
    reduced.update(zip(REPLICATED, _unpack_replicated(reduced_views[-1], [w[name].shape for name in REPLICATED])))

    outs = [[], [], [], []]
    for name in WEIGHTS:
        delta, new_m, new_v = _adamw(w[name], reduced[name], m[name], v[name], f"adamw_{name}")
        for lst, val in zip(outs, (reduced[name], delta, new_m, new_v)):
            lst.append(val)
    return (loss, grad_x, *outs[0], *outs[1], *outs[2], *outs[3])
```

```python
import functools
import math

import jax
import jax.numpy as jnp
import numpy as np
from jax import lax
from jax.experimental import pallas as pl
from jax.experimental.pallas import tpu as pltpu

F32 = jnp.float32
BF16 = jnp.bfloat16

N_MOD = 9
HEADS = 8
QK_NOPE = 128
QK_ROPE = 64
QK_HEAD = QK_NOPE + QK_ROPE
V_HEAD = 128
GRID_W = 64
ROPE_BASE = 10000.0
QK_SCALE = QK_HEAD ** -0.5
EPS = 1e-6
ADAM_LR, ADAM_B1, ADAM_B2, ADAM_EPS, ADAM_WD, ADAM_STEP = 0.001, 0.9, 0.999, 1e-08, 0.01, 10

N_DEV = 8
N_CHIP = 4
N_SEG = 3
LANE = 128
HEAD_PAD = 2 * LANE
VMEM_LIMIT_BYTES = 48 * 1024 * 1024
MESH = pl.DeviceIdType.MESH

WEIGHTS = ["c_ctx", "w_mod", "b_mod", "g_norm", "ffn_w1", "ffn_w3", "ffn_w2", "sc_w_in", "sc_conv", "sc_w_out",
           "mla_w_a", "mla_g_qa", "mla_w_uq", "mla_g_kva", "mla_w_ukv", "mla_g_q", "mla_g_k", "mla_w_o"]
EXCHANGE = {
    "w_mod": ("last", False, True), "ffn_w1": ("mid", True, True), "ffn_w3": ("mid", True, True),
    "ffn_w2": ("mid", False, True), "sc_w_in": ("last", False, True), "sc_w_out": ("mid", False, True),
    "mla_w_a": ("mid", False, True), "mla_w_uq": ("mid", True, True), "mla_w_ukv": ("last", False, True),
    "mla_w_o": ("mid", False, True), "g_norm": ("last", False, False), "sc_conv": ("last", False, False),
    "mla_g_qa": ("mid", False, False),
}
REPLICATED = ["c_ctx", "b_mod", "mla_g_kva", "mla_g_q", "mla_g_k"]


def _pick(n, cands):
    for cand in cands:
        if n % cand == 0:
            return cand
    return n


def _params(*sem):
    return pltpu.CompilerParams(dimension_semantics=sem, vmem_limit_bytes=VMEM_LIMIT_BYTES)


def _hbm():
    return pl.BlockSpec(memory_space=pl.ANY)


class Hosted:
    def __init__(self, inputs, out_shapes, scratch, start, finish):
        self.inputs, self.out_shapes, self.scratch, self.start, self.finish = inputs, out_shapes, scratch, start, finish


def _call(body, hosted, **kw):
    if hosted is None:
        return pl.pallas_call(body, **kw)
    single = not isinstance(kw["out_shape"], (tuple, list))
    out_shape = [kw["out_shape"]] if single else list(kw["out_shape"])
    out_specs = [kw["out_specs"]] if single else list(kw["out_specs"])
    in_specs, scratch, grid = list(kw["in_specs"]), list(kw.get("scratch_shapes", ())), kw["grid"]
    n_in, n_out, n_scr = len(in_specs), len(out_shape), len(scratch)
    h_in, h_out = len(hosted.inputs), len(hosted.out_shapes)

    def wrapped(*refs):
        ins, hins = refs[:n_in], refs[n_in:n_in + h_in]
        o0 = n_in + h_in
        outs, houts = refs[o0:o0 + n_out], refs[o0 + n_out:o0 + n_out + h_out]
        s0 = o0 + n_out + h_out
        scr, hscr = refs[s0:s0 + n_scr], refs[s0 + n_scr:]
        first = functools.reduce(jnp.logical_and, [pl.program_id(a) == 0 for a in range(len(grid))])
        last = functools.reduce(jnp.logical_and, [pl.program_id(a) == g - 1 for a, g in enumerate(grid)])

        @pl.when(first)
        def _():
            hosted.start(hins, houts, hscr)

        body(*ins, *outs, *scr)

        @pl.when(last)
        def _():
            hosted.finish(hins, houts, hscr)

    call = pl.pallas_call(
        wrapped, name=kw["name"], grid=grid, in_specs=in_specs + [_hbm()] * h_in,
        out_specs=tuple(out_specs + [_hbm()] * h_out), out_shape=tuple(out_shape + list(hosted.out_shapes)),
        scratch_shapes=scratch + list(hosted.scratch), input_output_aliases=kw.get("input_output_aliases", {}),
        compiler_params=_params(*["arbitrary"] * len(grid)))

    def run(*args):
        res = call(*args, *hosted.inputs)
        comp = res[:n_out]
        return (comp[0] if single else tuple(comp)), list(res[n_out:])

    return run


def _run_hosted(hosted, name):
    def body(*refs):
        h_in, h_out = len(hosted.inputs), len(hosted.out_shapes)
        hins, houts, hscr = refs[:h_in], refs[h_in:h_in + h_out], refs[h_in + h_out:]
        hosted.start(hins, houts, hscr)
        hosted.finish(hins, houts, hscr)

    return list(pl.pallas_call(
        body, name=name, in_specs=[_hbm()] * len(hosted.inputs), out_specs=tuple([_hbm()] * len(hosted.out_shapes)),
        out_shape=tuple(hosted.out_shapes), scratch_shapes=list(hosted.scratch))(*hosted.inputs))


class Geo:
    def __init__(self, n_lat, n_ctx):
        self.n_lat, self.n_ctx = n_lat, n_ctx
        self.rows = 2 * n_lat + 2 * n_ctx
        self.tile = n_ctx
        assert n_lat % n_ctx == 0 and n_ctx % 16 == 0
        self.mm_tile = _pick(n_lat, (512, 256, 128)) if self.rows % _pick(n_lat, (512, 256, 128)) == 0 else n_ctx
        self.big_tile = _pick(self.rows, (1536, 768, 512, 256))

    def seg(self, i, tile):
        return jnp.minimum((i * tile) // self.n_lat, N_SEG - 1)

    def seg_start(self, i, tile):
        row = i * tile
        return jnp.logical_or(row % self.n_lat == 0, row == 2 * self.n_lat) & (row <= 2 * self.n_lat)


_NT = (((1,), (1,)), ((), ()))
_NN = (((1,), (0,)), ((), ()))
_TN = (((0,), (0,)), ((), ()))


def _dot(a, b, dims):
    return lax.dot_general(a.astype(BF16), b.astype(BF16), dims, preferred_element_type=F32)


def _mm(a, b, *, ta=False, tb=False, out_dtype=F32, name, gate=None, hosted=None):
    (kdim, m) = a.shape if ta else a.shape[::-1]
    n = b.shape[0] if tb else b.shape[1]
    assert (b.shape[1] if tb else b.shape[0]) == kdim
    if gate is not None:
        tm = gate[4].mm_tile
    else:
        tm = _pick(m, (512, 256, 128))
    tn = _pick(n, (512, 256, 128))
    tk = _pick(kdim, (1024, 512, 256, 128))
    nk = kdim // tk
    dims = (((0 if ta else 1,), (1 if tb else 0,)), ((), ()))

    def body(*refs):
        if gate is not None:
            a_ref, b_ref, res_ref, gate_ref, o_ref, y_ref, acc_ref = refs
        else:
            a_ref, b_ref, o_ref, acc_ref = refs
        kk = pl.program_id(2)

        @pl.when(kk == 0)
        def _():
            acc_ref[...] = jnp.zeros_like(acc_ref)

        acc_ref[...] += lax.dot_general(a_ref[...].astype(BF16), b_ref[...].astype(BF16), dims,
                                        preferred_element_type=F32)

        @pl.when(kk == nk - 1)
        def _():
            acc = acc_ref[...]
            if gate is not None:
                y_ref[...] = acc.astype(y_ref.dtype)
                o_ref[...] = res_ref[...] + (gate[3] * gate_ref[...]) * acc
            else:
                o_ref[...] = acc.astype(o_ref.dtype)

    a_spec = pl.BlockSpec((tk, tm), lambda i, j, k: (k, i)) if ta else pl.BlockSpec((tm, tk), lambda i, j, k: (i, k))
    b_spec = pl.BlockSpec((tn, tk), lambda i, j, k: (j, k)) if tb else pl.BlockSpec((tk, tn), lambda i, j, k: (k, j))
    o_spec = pl.BlockSpec((tm, tn), lambda i, j, k: (i, j))
    in_specs, args = [a_spec, b_spec], [a, b]
    out_shape, out_specs = jax.ShapeDtypeStruct((m, n), out_dtype), o_spec
    if gate is not None:
        res, mod4, kmod, _, geo = gate
        in_specs += [o_spec, pl.BlockSpec((None, None, 1, tn), lambda i, j, k: (geo.seg(i, tm), kmod, 0, j))]
        args += [res, mod4]
        out_shape = (jax.ShapeDtypeStruct((m, n), F32), jax.ShapeDtypeStruct((m, n), BF16))
        out_specs = (o_spec, o_spec)
    return _call(
        body, hosted, name=name, grid=(m // tm, n // tn, nk), in_specs=in_specs, out_specs=out_specs,
        out_shape=out_shape, scratch_shapes=[pltpu.VMEM((tm, tn), F32)],
        compiler_params=_params("parallel", "parallel", "arbitrary"),
    )(*args)


def _tn_wide(lhs, rhs, into, s0, name, hosted=None):
    t, m = lhs.shape
    n = rhs.shape[1]
    tm = _pick(m, (1408, 1024, 512, 256, 128))
    tk = _pick(t, (768, 512, 256, 128))

    def body(l_ref, r_ref, _, o_ref):
        kk = pl.program_id(1)
        part = lax.dot_general(l_ref[...], r_ref[...], _TN, preferred_element_type=F32)

        @pl.when(kk == 0)
        def _():
            o_ref[...] = part

        @pl.when(kk > 0)
        def _():
            o_ref[...] += part

    return _call(
        body, hosted, name=name, grid=(m // tm, t // tk),
        in_specs=[pl.BlockSpec((tk, tm), lambda i, k: (k, i)), pl.BlockSpec((tk, n), lambda i, k: (k, 0)),
                  pl.BlockSpec(memory_space=pl.ANY)],
        out_specs=pl.BlockSpec((None, tm, n), lambda i, k: (s0, i, 0)),
        out_shape=jax.ShapeDtypeStruct(into.shape, into.dtype), input_output_aliases={2: 0},
        compiler_params=_params("parallel", "arbitrary"),
    )(lhs, rhs, into)


def _mod_spec(geo, tile, kmod, d):
    return pl.BlockSpec((None, None, 1, d), lambda i: (geo.seg(i, tile), kmod, 0, 0))


def _pre_fwd(h, g, mod4, k_shift, geo, name):
    t, d = h.shape
    tile = geo.tile

    def body(h_ref, g_ref, sh_ref, sc_ref, o_ref):
        hv = h_ref[...]
        r = lax.rsqrt(jnp.mean(hv * hv, axis=-1, keepdims=True) + EPS)
        y = hv * r * g_ref[...]
        o_ref[...] = (y * (1.0 + sc_ref[...]) + sh_ref[...]).astype(o_ref.dtype)

    row = pl.BlockSpec((tile, d), lambda i: (i, 0))
    return pl.pallas_call(
        body, name=name, grid=(t // tile,),
        in_specs=[row, pl.BlockSpec((1, d), lambda i: (0, 0)), _mod_spec(geo, tile, k_shift, d),
                  _mod_spec(geo, tile, k_shift + 1, d)],
        out_specs=row, out_shape=jax.ShapeDtypeStruct((t, d), BF16), compiler_params=_params("parallel"),
    )(h, g, mod4, mod4)


def _pre_bwd(h, g, mod4, k_shift, dnx, dres, geo, name):
    t, d = h.shape
    tile = geo.tile

    def body(h_ref, g_ref, sc_ref, dnx_ref, dres_ref, dh_ref, dg_ref, dsh_ref, dsc_ref):
        i = pl.program_id(0)
        hv, gv, dout = h_ref[...], g_ref[...], dnx_ref[...].astype(F32)
        r = lax.rsqrt(jnp.mean(hv * hv, axis=-1, keepdims=True) + EPS)
        xhat = hv * r
        dy = dout * (1.0 + sc_ref[...])
        u = dy * gv
        dh_ref[...] = r * (u - xhat * jnp.mean(u * xhat, axis=-1, keepdims=True)) + dres_ref[...]

        @pl.when(i == 0)
        def _():
            dg_ref[...] = jnp.zeros_like(dg_ref)

        @pl.when(geo.seg_start(i, tile))
        def _():
            dsh_ref[...] = jnp.zeros_like(dsh_ref)
            dsc_ref[...] = jnp.zeros_like(dsc_ref)

        dg_ref[...] += jnp.sum(dy * xhat, axis=0, keepdims=True)
        dsh_ref[...] += jnp.sum(dout, axis=0, keepdims=True)
        dsc_ref[...] += jnp.sum(dout * (xhat * gv), axis=0, keepdims=True)

    row = pl.BlockSpec((tile, d), lambda i: (i, 0))
    vec = pl.BlockSpec((1, d), lambda i: (0, 0))
    segv = pl.BlockSpec((None, 1, d), lambda i: (geo.seg(i, tile), 0, 0))
    return pl.pallas_call(
        body, name=name, grid=(t // tile,),
        in_specs=[row, vec, _mod_spec(geo, tile, k_shift + 1, d), row, row],
        out_specs=(row, vec, segv, segv),
        out_shape=(jax.ShapeDtypeStruct((t, d), F32), jax.ShapeDtypeStruct((1, d), F32),
                   jax.ShapeDtypeStruct((N_SEG, 1, d), F32), jax.ShapeDtypeStruct((N_SEG, 1, d), F32)),
        compiler_params=_params("arbitrary"),
    )(h, g, mod4, dnx, dres)


def _gate_bwd(dh, y, mod4, k_gate, coef, geo, name):
    t, d = dh.shape
    tile = geo.tile

    def body(dh_ref, y_ref, gt_ref, dy_ref, dgt_ref):
        i = pl.program_id(0)
        dhv = dh_ref[...]
        dy_ref[...] = ((coef * gt_ref[...]) * dhv).astype(dy_ref.dtype)

        @pl.when(geo.seg_start(i, tile))
        def _():
            dgt_ref[...] = jnp.zeros_like(dgt_ref)

        dgt_ref[...] += coef * jnp.sum(dhv * y_ref[...].astype(F32), axis=0, keepdims=True)

    row = pl.BlockSpec((tile, d), lambda i: (i, 0))
    segv = pl.BlockSpec((None, 1, d), lambda i: (geo.seg(i, tile), 0, 0))
    return pl.pallas_call(
        body, name=name, grid=(t // tile,), in_specs=[row, row, _mod_spec(geo, tile, k_gate, d)],
        out_specs=(row, segv),
        out_shape=(jax.ShapeDtypeStruct((t, d), BF16), jax.ShapeDtypeStruct((N_SEG, 1, d), F32)),
        compiler_params=_params("arbitrary"),
    )(dh, y, mod4)


def _ff_tile(f):
    return _pick(f, (256, 128))


def _ffn_up(nx, w1t, w3t, s0, geo, name, hosted=None):
    t, d = nx.shape
    f = w1t.shape[1]
    tm, tn = geo.big_tile, _ff_tile(f)

    def body(x_ref, w1_ref, w3_ref, a_ref, b_ref, act_ref):
        xv = x_ref[...]
        a = lax.dot_general(xv, w1_ref[...], _NT, preferred_element_type=F32)
        bv = lax.dot_general(xv, w3_ref[...], _NT, preferred_element_type=F32)
        a_ref[...] = a.astype(a_ref.dtype)
        b_ref[...] = bv.astype(b_ref.dtype)
        act_ref[...] = (a * jax.nn.sigmoid(a) * bv).astype(act_ref.dtype)

    w_spec = pl.BlockSpec((None, tn, d), lambda i, j: (s0, j, 0))
    o_spec = pl.BlockSpec((tm, tn), lambda i, j: (i, j))
    sds = jax.ShapeDtypeStruct((t, f), BF16)
    return _call(
        body, hosted, name=name, grid=(t // tm, f // tn),
        in_specs=[pl.BlockSpec((tm, d), lambda i, j: (i, 0)), w_spec, w_spec],
        out_specs=(o_spec,) * 3, out_shape=(sds,) * 3, compiler_params=_params("parallel", "parallel"),
    )(nx, w1t, w3t)


def _ffn_down(act, w2, s0, res, mod4, k_gate, geo, name, hosted=None):
    t, f = act.shape
    d = w2.shape[2]
    tm, tn = geo.mm_tile, _pick(d, (1024, 512, 256, 128))

    def body(a_ref, w_ref, res_ref, gate_ref, o_ref, y_ref):
        acc = lax.dot_general(a_ref[...], w_ref[...], _NN, preferred_element_type=F32)
        y_ref[...] = acc.astype(y_ref.dtype)
        o_ref[...] = res_ref[...] + (0.5 * gate_ref[...]) * acc

    o_spec = pl.BlockSpec((tm, tn), lambda i, j: (i, j))
    return _call(
        body, hosted, name=name, grid=(t // tm, d // tn),
        in_specs=[pl.BlockSpec((tm, f), lambda i, j: (i, 0)), pl.BlockSpec((None, f, tn), lambda i, j: (s0, 0, j)),
                  o_spec, pl.BlockSpec((None, None, 1, tn), lambda i, j: (geo.seg(i, tm), k_gate, 0, j))],
        out_specs=(o_spec, o_spec),
        out_shape=(jax.ShapeDtypeStruct((t, d), F32), jax.ShapeDtypeStruct((t, d), BF16)),
        compiler_params=_params("parallel", "parallel"),
    )(act, w2, res, mod4)


def _ffn_dact(dy, w2, a, b, s0, geo, name, hosted=None):
    t, d = dy.shape
    f = w2.shape[1]
    tm, tn = geo.big_tile, _ff_tile(f)

    def body(dy_ref, w_ref, a_ref, b_ref, da_ref, db_ref):
        dact = lax.dot_general(dy_ref[...], w_ref[...], _NT, preferred_element_type=F32)
        av, bv = a_ref[...].astype(F32), b_ref[...].astype(F32)
        sg = jax.nn.sigmoid(av)
        da_ref[...] = (dact * bv * (sg * (1.0 + av * (1.0 - sg)))).astype(da_ref.dtype)
        db_ref[...] = (dact * (av * sg)).astype(db_ref.dtype)

    o_spec = pl.BlockSpec((tm, tn), lambda i, j: (i, j))
    sds = jax.ShapeDtypeStruct((t, f), BF16)
    return _call(
        body, hosted, name=name, grid=(t // tm, f // tn),
        in_specs=[pl.BlockSpec((tm, d), lambda i, j: (i, 0)), pl.BlockSpec((None, tn, d), lambda i, j: (s0, j, 0)),
                  o_spec, o_spec],
        out_specs=(o_spec, o_spec), out_shape=(sds, sds), compiler_params=_params("parallel", "parallel"),
    )(dy, w2, a, b)


def _ffn_dnx(da, db, w1t, w3t, s0, geo, name, hosted=None):
    t, f = da.shape
    d = w1t.shape[2]
    tm, tn = geo.mm_tile, _pick(d, (512, 256, 128))

    def body(da_ref, db_ref, w1_ref, w3_ref, o_ref):
        o_ref[...] = (lax.dot_general(da_ref[...], w1_ref[...], _NN, preferred_element_type=F32)
                      + lax.dot_general(db_ref[...], w3_ref[...], _NN, preferred_element_type=F32))

    x_spec = pl.BlockSpec((tm, f), lambda j, i: (i, 0))
    w_spec = pl.BlockSpec((None, f, tn), lambda j, i: (s0, 0, j))
    return _call(
        body, hosted, name=name, grid=(d // tn, t // tm), in_specs=[x_spec, x_spec, w_spec, w_spec],
        out_specs=pl.BlockSpec((tm, tn), lambda j, i: (i, j)), out_shape=jax.ShapeDtypeStruct((t, d), F32),
        compiler_params=_params("parallel", "parallel"),
    )(da, db, w1t, w3t)


def _with_host(fn, hosts, got, slot, *args, **kw):
    hosted = hosts.get(slot)
    if hosted is None:
        return fn(*args, **kw)
    out, got[slot] = fn(*args, hosted=hosted, **kw)
    return out


def _ffn_fwd(h, g, mod4, k0, w, s0, geo, tag, sub, hosts, got):
    nx = _pre_fwd(h, g, mod4, k0, geo, f"{tag}_pre")
    a, b, act = _with_host(_ffn_up, hosts, got, f"{sub}_up", nx, w["ffn_w1"], w["ffn_w3"], s0, geo, f"{tag}_up")
    h_out, y = _with_host(_ffn_down, hosts, got, f"{sub}_down", act, w["ffn_w2"], s0, h, mod4, k0 + 2, geo, f"{tag}_down")
    return h_out, (h, nx, a, b, act, y)


def _ffn_bwd(dh_out, saved, g, mod4, k0, w, s0, gbuf, geo, tag, sub, hosts, got):
    h, nx, a, b, act, y = saved
    dy, dgate = _gate_bwd(dh_out, y, mod4, k0 + 2, 0.5, geo, f"{tag}_dgate")
    da, db = _with_host(_ffn_dact, hosts, got, f"{sub}_dact", dy, w["ffn_w2"], a, b, s0, geo, f"{tag}_dact")
    gbuf["ffn_w2"] = _tn_wide(act, dy, gbuf["ffn_w2"], s0, f"{tag}_dw2")
    dnx = _with_host(_ffn_dnx, hosts, got, f"{sub}_dnx", da, db, w["ffn_w1"], w["ffn_w3"], s0, geo, f"{tag}_dnx")
    gbuf["ffn_w1"] = _tn_wide(da, nx, gbuf["ffn_w1"], s0, f"{tag}_dw1")
    gbuf["ffn_w3"] = _tn_wide(db, nx, gbuf["ffn_w3"], s0, f"{tag}_dw3")
    dh, dg, dshift, dscale = _pre_bwd(h, g, mod4, k0, dnx, dh_out, geo, f"{tag}_dpre")
    return dh, dg, (dshift, dscale, dgate)


def _interleave(w, n_parts, tile):
    lead, cols = w.shape[:-1], w.shape[-1] // n_parts
    return w.reshape(*lead, n_parts, cols // tile, tile).swapaxes(-3, -2).reshape(*lead, n_parts * cols)


def _deinterleave(w, n_parts, tile):
    lead, cols = w.shape[:-1], w.shape[-1] // n_parts
    return w.reshape(*lead, cols // tile, n_parts, tile).swapaxes(-3, -2).reshape(*lead, n_parts * cols)


HALO = 16


def _conv_tile(c):
    return _pick(c, (256, 128))


def _conv_specs(geo, tc, t):
    tile = geo.tile
    per = tile // HALO
    last = t // HALO - 1
    cur = pl.BlockSpec((tile, 3 * tc), lambda j, i: (i, j))
    prev = pl.BlockSpec((HALO, 3 * tc), lambda j, i: (jnp.maximum(i * per - 1, 0), j))
    nxt = pl.BlockSpec((HALO, 3 * tc), lambda j, i: (jnp.minimum((i + 1) * per, last), j))
    return cur, prev, nxt


def _conv_edges(geo, i):
    tile = geo.tile
    row = i * tile
    lat = row < 2 * geo.n_lat
    first = jnp.where(lat, row % geo.n_lat == 0, (row - 2 * geo.n_lat) % geo.n_ctx == 0)
    end = row + tile
    last = jnp.where(lat, end % geo.n_lat == 0, (end - 2 * geo.n_lat) % geo.n_ctx == 0)
    return first, last


def _shift_rows(v, before, after):
    n = v.shape[0]
    rows = lax.broadcasted_iota(jnp.int32, v.shape, 0)
    down = jnp.where(rows == 0, before, pltpu.roll(v, 1, 0))
    up = jnp.where(rows == n - 1, after, pltpu.roll(v, n - 1, 0))
    return down, up


def _conv_fwd(proj, conv_w, geo, name, hosted=None):
    t, c3 = proj.shape
    c = c3 // 3
    tc, tile = _conv_tile(c), geo.tile

    def body(cur_ref, prev_ref, next_ref, w_ref, o_ref):
        first, last = _conv_edges(geo, pl.program_id(1))
        bv = cur_ref[:, :tc].astype(F32)
        p = cur_ref[:, tc:2 * tc].astype(F32) * cur_ref[:, 2 * tc:].astype(F32)
        p_before = prev_ref[HALO - 1:HALO, tc:2 * tc].astype(F32) * prev_ref[HALO - 1:HALO, 2 * tc:].astype(F32)
        p_after = next_ref[0:1, tc:2 * tc].astype(F32) * next_ref[0:1, 2 * tc:].astype(F32)
        p_before = jnp.where(first, 0.0, p_before)
        p_after = jnp.where(last, 0.0, p_after)
        pm1, pp1 = _shift_rows(p, p_before, p_after)
        w = w_ref[...]
        q = w[0:1] * pm1 + w[1:2] * p + w[2:3] * pp1
        o_ref[...] = (bv * q).astype(o_ref.dtype)

    cur, prev, nxt = _conv_specs(geo, tc, t)
    return _call(
        body, hosted, name=name, grid=(c // tc, t // tile),
        in_specs=[cur, prev, nxt, pl.BlockSpec((3, tc), lambda j, i: (0, j))],
        out_specs=pl.BlockSpec((tile, tc), lambda j, i: (i, j)), out_shape=jax.ShapeDtypeStruct((t, c), BF16),
        compiler_params=_params("parallel", "parallel"),
    )(proj, proj, proj, conv_w)


def _conv_bwd(proj, dyc, conv_w, geo, name):
    t, c3 = proj.shape
    c = c3 // 3
    tc, tile = _conv_tile(c), geo.tile

    def body(cur_ref, prev_ref, next_ref, d_ref, dprev_ref, dnext_ref, w_ref, o_ref, dw_ref):
        i = pl.program_id(1)
        first, last = _conv_edges(geo, i)
        bv = cur_ref[:, :tc].astype(F32)
        cv = cur_ref[:, tc:2 * tc].astype(F32)
        uv = cur_ref[:, 2 * tc:].astype(F32)
        p = cv * uv
        p_before = prev_ref[HALO - 1:HALO, tc:2 * tc].astype(F32) * prev_ref[HALO - 1:HALO, 2 * tc:].astype(F32)
        p_after = next_ref[0:1, tc:2 * tc].astype(F32) * next_ref[0:1, 2 * tc:].astype(F32)
        p_before = jnp.where(first, 0.0, p_before)
        p_after = jnp.where(last, 0.0, p_after)
        pm1, pp1 = _shift_rows(p, p_before, p_after)
        w = w_ref[...]
        q = w[0:1] * pm1 + w[1:2] * p + w[2:3] * pp1
        dy = d_ref[...].astype(F32)
        dq = dy * bv
        dq_before = dprev_ref[HALO - 1:HALO, :].astype(F32) * prev_ref[HALO - 1:HALO, :tc].astype(F32)
        dq_after = dnext_ref[0:1, :].astype(F32) * next_ref[0:1, :tc].astype(F32)
        dq_before = jnp.where(first, 0.0, dq_before)
        dq_after = jnp.where(last, 0.0, dq_after)
        dqm1, dqp1 = _shift_rows(dq, dq_before, dq_after)
        dp = w[0:1] * dqp1 + w[1:2] * dq + w[2:3] * dqm1
        o_ref[:, :tc] = (dy * q).astype(o_ref.dtype)
        o_ref[:, tc:2 * tc] = (dp * uv).astype(o_ref.dtype)
        o_ref[:, 2 * tc:] = (dp * cv).astype(o_ref.dtype)

        @pl.when(i == 0)
        def _():
            dw_ref[...] = jnp.zeros_like(dw_ref)

        dw_ref[0:1, :] += jnp.sum(dq * pm1, axis=0, keepdims=True)
        dw_ref[1:2, :] += jnp.sum(dq * p, axis=0, keepdims=True)
        dw_ref[2:3, :] += jnp.sum(dq * pp1, axis=0, keepdims=True)

    cur, prev, nxt = _conv_specs(geo, tc, t)
    per, lastb = tile // HALO, t // HALO - 1
    dcur = pl.BlockSpec((tile, tc), lambda j, i: (i, j))
    dprev = pl.BlockSpec((HALO, tc), lambda j, i: (jnp.maximum(i * per - 1, 0), j))
    dnext = pl.BlockSpec((HALO, tc), lambda j, i: (jnp.minimum((i + 1) * per, lastb), j))
    wspec = pl.BlockSpec((3, tc), lambda j, i: (0, j))
    return pl.pallas_call(
        body, name=name, grid=(c // tc, t // tile), in_specs=[cur, prev, nxt, dcur, dprev, dnext, wspec],
        out_specs=(cur, wspec), out_shape=(jax.ShapeDtypeStruct((t, c3), BF16), jax.ShapeDtypeStruct((3, c), F32)),
        compiler_params=_params("parallel", "arbitrary"),
    )(proj, proj, proj, dyc, dyc, dyc, conv_w)


def _sconv_fwd(h, g, mod4, w_in, conv_w, w_out, geo, tag, hosts, got):
    nx = _pre_fwd(h, g, mod4, 3, geo, f"{tag}_pre")
    proj = _with_host(_mm, hosts, got, "mix_a", nx, w_in, out_dtype=BF16, name=f"{tag}_in")
    yc = _with_host(_conv_fwd, hosts, got, "mix_b", proj, conv_w, geo, f"{tag}_conv")
    h_out, y = _mm(yc, w_out, name=f"{tag}_out", gate=(h, mod4, 5, 1.0, geo))
    return h_out, (h, nx, proj, yc, y)


def _sconv_bwd(dh_out, saved, g, mod4, w_in, conv_w, w_out, geo, tag, hosts, got):
    h, nx, proj, yc, y = saved
    dy, dgate = _gate_bwd(dh_out, y, mod4, 5, 1.0, geo, f"{tag}_dgate")
    dyc = _mm(dy, w_out, tb=True, out_dtype=BF16, name=f"{tag}_dyc")
    dw_out = _mm(yc, dy, ta=True, name=f"{tag}_dwout")
    dproj, dconv = _conv_bwd(proj, dyc, conv_w, geo, f"{tag}_dconv")
    dnx = _with_host(_mm, hosts, got, "mix_b", dproj, w_in, tb=True, name=f"{tag}_dnx")
    dw_in = _with_host(_mm, hosts, got, "mix_a", nx, dproj, ta=True, name=f"{tag}_dwin")
    dh, dg, dshift, dscale = _pre_bwd(h, g, mod4, 3, dnx, dh_out, geo, f"{tag}_dpre")
    return dh, dg, (dshift, dscale, dgate), dw_in, dconv, dw_out


def _rope_swap(v):
    nf = QK_ROPE // 4
    return v.reshape(v.shape[:-1] + (2, 2, nf)).swapaxes(-3, -2).reshape(v.shape)


def _rope_tables(geo):
    n = geo.n_lat
    nf = QK_ROPE // 4
    pos = np.arange(n)
    inv = ROPE_BASE ** (-np.arange(nf, dtype=np.float32) / nf)
    ang = np.concatenate([(pos // GRID_W)[:, None] * inv, (pos % GRID_W)[:, None] * inv], axis=1).astype(np.float32)
    cos, sin = np.cos(ang), np.sin(ang)
    zeros = np.zeros((n, LANE - QK_ROPE), np.float32)
    c_lat = np.concatenate([cos, cos, zeros], axis=1)
    s_lat = np.concatenate([-sin, sin, zeros], axis=1)
    c_ctx = np.concatenate([np.ones((2 * geo.n_ctx, QK_ROPE), np.float32), np.zeros((2 * geo.n_ctx, LANE - QK_ROPE), np.float32)], 1)
    s_ctx = np.zeros((2 * geo.n_ctx, LANE), np.float32)
    return (jnp.asarray(np.concatenate([c_lat, c_lat, c_ctx], 0)), jnp.asarray(np.concatenate([s_lat, s_lat, s_ctx], 0)))


def _swap_halves(v):
    lanes = lax.broadcasted_iota(jnp.int32, v.shape, 1)
    return jnp.where(lanes < QK_ROPE // 2, pltpu.roll(v, LANE - QK_ROPE // 2, 1), pltpu.roll(v, QK_ROPE // 2, 1))


def _latent_norm_fwd(down, g_qa, g_kva, geo, name):
    t, wd = down.shape
    ql, kl = g_qa.shape[1], g_kva.shape[1]
    tile = geo.tile

    def body(d_ref, gq_ref, gk_ref, cq_ref, ckv_ref):
        for lo, n, g_ref, o_ref in ((0, ql, gq_ref, cq_ref), (ql, kl, gk_ref, ckv_ref)):
            x = d_ref[:, lo:lo + n]
            r = lax.rsqrt(jnp.mean(x * x, axis=-1, keepdims=True) + EPS)
            o_ref[...] = (x * r * g_ref[...]).astype(o_ref.dtype)

    return pl.pallas_call(
        body, name=name, grid=(t // tile,),
        in_specs=[pl.BlockSpec((tile, wd), lambda i: (i, 0)), pl.BlockSpec((1, ql), lambda i: (0, 0)),
                  pl.BlockSpec((1, kl), lambda i: (0, 0))],
        out_specs=(pl.BlockSpec((tile, ql), lambda i: (i, 0)), pl.BlockSpec((tile, kl), lambda i: (i, 0))),
        out_shape=(jax.ShapeDtypeStruct((t, ql), BF16), jax.ShapeDtypeStruct((t, kl), BF16)),
        compiler_params=_params("parallel"),
    )(down, g_qa, g_kva)


def _latent_norm_bwd(down, g_qa, g_kva, dcqn, dckvn, dkr, geo, name):
    t, wd = down.shape
    ql, kl = g_qa.shape[1], g_kva.shape[1]
    tile = geo.tile

    def body(d_ref, gq_ref, gk_ref, dq_ref, dk_ref, dkr_ref, o_ref, dgq_ref, dgk_ref):
        i = pl.program_id(0)

        @pl.when(i == 0)
        def _():
            dgq_ref[...] = jnp.zeros_like(dgq_ref)
            dgk_ref[...] = jnp.zeros_like(dgk_ref)

        for lo, n, g_ref, dy_ref, dg_ref in ((0, ql, gq_ref, dq_ref, dgq_ref), (ql, kl, gk_ref, dk_ref, dgk_ref)):
            x = d_ref[:, lo:lo + n]
            dy = dy_ref[...].astype(F32)
            r = lax.rsqrt(jnp.mean(x * x, axis=-1, keepdims=True) + EPS)
            xhat = x * r
            u = dy * g_ref[...]
            o_ref[:, lo:lo + n] = (r * (u - xhat * jnp.mean(u * xhat, axis=-1, keepdims=True))).astype(o_ref.dtype)
            dg_ref[...] += jnp.sum(dy * xhat, axis=0, keepdims=True)
        o_ref[:, ql + kl:] = dkr_ref[...].astype(o_ref.dtype)

    def row(n):
        return pl.BlockSpec((tile, n), lambda i: (i, 0))

    def vec(n):
        return pl.BlockSpec((1, n), lambda i: (0, 0))

    return pl.pallas_call(
        body, name=name, grid=(t // tile,),
        in_specs=[row(wd), vec(ql), vec(kl), row(ql), row(kl), row(wd - ql - kl)],
        out_specs=(row(wd), vec(ql), vec(kl)),
        out_shape=(jax.ShapeDtypeStruct((t, wd), BF16), jax.ShapeDtypeStruct((1, ql), F32),
                   jax.ShapeDtypeStruct((1, kl), F32)),
        compiler_params=_params("arbitrary"),
    )(down, g_qa, g_kva, dcqn, dckvn, dkr)


def _qk_specs(geo, xn_col0, xr_col, shared_rope):
    tile = geo.tile
    xn_spec = pl.BlockSpec((tile, LANE), lambda i, hh: (i, xn_col0 + hh))
    if shared_rope:
        xr_spec = pl.BlockSpec((tile, LANE), lambda i, hh: (i, xr_col))
    else:
        xr_spec = pl.BlockSpec((tile, LANE), lambda i, hh: (i, xr_col + hh))
    vec = pl.BlockSpec((1, LANE), lambda i, hh: (0, 0))
    tab = pl.BlockSpec((tile, LANE), lambda i, hh: (i, 0))
    return xn_spec, xr_spec, vec, tab


def _qk_norm(xn, xr):
    ss = jnp.sum(xn * xn, axis=-1, keepdims=True) + jnp.sum(xr * xr, axis=-1, keepdims=True)
    return lax.rsqrt(ss * (1.0 / QK_HEAD) + EPS)


def _qk_fwd(xn_arr, xn_col0, xr_arr, xr_col, shared_rope, gn, gr, cos, sin, geo, name):
    t = xn_arr.shape[0]
    tile = geo.tile

    def body(xn_ref, xr_ref, gn_ref, gr_ref, c_ref, s_ref, o_ref):
        xn, xr = xn_ref[...].astype(F32), xr_ref[...].astype(F32)
        r = _qk_norm(xn, xr)
        yr = xr * r * gr_ref[...]
        o_ref[:, :LANE] = (xn * r * gn_ref[...]).astype(o_ref.dtype)
        o_ref[:, LANE:] = (yr * c_ref[...] + _swap_halves(yr) * s_ref[...]).astype(o_ref.dtype)

    xn_spec, xr_spec, vec, tab = _qk_specs(geo, xn_col0, xr_col, shared_rope)
    return pl.pallas_call(
        body, name=name, grid=(t // tile, HEADS), in_specs=[xn_spec, xr_spec, vec, vec, tab, tab],
        out_specs=pl.BlockSpec((tile, HEAD_PAD), lambda i, hh: (i, hh)),
        out_shape=jax.ShapeDtypeStruct((t, HEADS * HEAD_PAD), BF16), compiler_params=_params("parallel", "parallel"),
    )(xn_arr, xr_arr, gn, gr, cos, sin)


def _qk_bwd(xn_arr, xn_col0, xr_arr, xr_col, shared_rope, gn, gr, cos, sin, dout, geo, name):
    t = xn_arr.shape[0]
    tile = geo.tile

    def body(xn_ref, xr_ref, gn_ref, gr_ref, c_ref, s_ref, d_ref, dxn_ref, dxr_ref, dgn_ref, dgr_ref):
        i, hh = pl.program_id(0), pl.program_id(1)
        xn, xr = xn_ref[...].astype(F32), xr_ref[...].astype(F32)
        r = _qk_norm(xn, xr)
        xhn, xhr = xn * r, xr * r
        dyn = d_ref[:, :LANE].astype(F32)
        dro = d_ref[:, LANE:].astype(F32)
        dyr = dro * c_ref[...] + _swap_halves(dro * s_ref[...])
        un, ur = dyn * gn_ref[...], dyr * gr_ref[...]
        mean = (jnp.sum(un * xhn, axis=-1, keepdims=True) + jnp.sum(ur * xhr, axis=-1, keepdims=True)) * (1.0 / QK_HEAD)
        dxn_ref[...] = (r * (un - xhn * mean)).astype(dxn_ref.dtype)
        dxr = r * (ur - xhr * mean)

        @pl.when(jnp.logical_and(i == 0, hh == 0))
        def _():
            dgn_ref[...] = jnp.zeros_like(dgn_ref)
            dgr_ref[...] = jnp.zeros_like(dgr_ref)

        dgn_ref[...] += jnp.sum(dyn * xhn, axis=0, keepdims=True)
        dgr_ref[...] += jnp.sum(dyr * xhr, axis=0, keepdims=True)
        if shared_rope:
            @pl.when(hh == 0)
            def _():
                dxr_ref[...] = jnp.zeros_like(dxr_ref)

            dxr_ref[...] += dxr
        else:
            dxr_ref[...] = dxr.astype(dxr_ref.dtype)

    xn_spec, xr_spec, vec, tab = _qk_specs(geo, xn_col0, xr_col, shared_rope)
    head = pl.BlockSpec((tile, LANE), lambda i, hh: (i, hh))
    if shared_rope:
        dxr_spec, dxr_shape = pl.BlockSpec((tile, LANE), lambda i, hh: (i, 0)), jax.ShapeDtypeStruct((t, LANE), F32)
    else:
        dxr_spec, dxr_shape = head, jax.ShapeDtypeStruct((t, HEADS * LANE), BF16)
    return pl.pallas_call(
        body, name=name, grid=(t // tile, HEADS),
        in_specs=[xn_spec, xr_spec, vec, vec, tab, tab, pl.BlockSpec((tile, HEAD_PAD), lambda i, hh: (i, hh))],
        out_specs=(head, dxr_spec, vec, vec),
        out_shape=(jax.ShapeDtypeStruct((t, HEADS * LANE), BF16), dxr_shape, jax.ShapeDtypeStruct((1, LANE), F32),
                   jax.ShapeDtypeStruct((1, LANE), F32)),
        compiler_params=_params("arbitrary", "arbitrary"),
    )(xn_arr, xr_arr, gn, gr, cos, sin, dout)


def _attn_specs(geo):
    tq, nq = geo.n_ctx, geo.n_lat // geo.n_ctx

    def qrow(b, i):
        return jnp.where(i < nq, b * nq + i, 2 * nq + b)

    q_spec = pl.BlockSpec((tq, HEAD_PAD), lambda b, hh, i: (qrow(b, i), hh))
    kc_spec = pl.BlockSpec((geo.n_ctx, HEAD_PAD), lambda b, hh, i: (2 * nq + b, hh))
    kl_spec = pl.BlockSpec((geo.n_lat, HEAD_PAD), lambda b, hh, i: (b, hh))
    vc_spec = pl.BlockSpec((geo.n_ctx, V_HEAD), lambda b, hh, i: (2 * nq + b, HEADS + hh))
    vl_spec = pl.BlockSpec((geo.n_lat, V_HEAD), lambda b, hh, i: (b, HEADS + hh))
    o_spec = pl.BlockSpec((tq, V_HEAD), lambda b, hh, i: (qrow(b, i), hh))
    return tq, nq, q_spec, kc_spec, kl_spec, vc_spec, vl_spec, o_spec


def _attn_fwd(q, k, kv, with_ctx_q, geo, name, hosted=None):
    t = q.shape[0]
    tq, nq, q_spec, kc_spec, kl_spec, vc_spec, vl_spec, o_spec = _attn_specs(geo)

    def body(q_ref, kc_ref, kl_ref, vc_ref, vl_ref, o_ref):
        i = pl.program_id(2)
        qv = q_ref[...]
        s_c = _dot(qv, kc_ref[...], _NT) * QK_SCALE

        @pl.when(i < nq)
        def _():
            s_l = _dot(qv, kl_ref[...], _NT) * QK_SCALE
            m = jnp.maximum(jnp.max(s_c, axis=-1, keepdims=True), jnp.max(s_l, axis=-1, keepdims=True))
            p_c, p_l = jnp.exp(s_c - m), jnp.exp(s_l - m)
            den = jnp.sum(p_c, axis=-1, keepdims=True) + jnp.sum(p_l, axis=-1, keepdims=True)
            o = _dot(p_c, vc_ref[...], _NN) + _dot(p_l, vl_ref[...], _NN)
            o_ref[...] = (o / den).astype(o_ref.dtype)

        @pl.when(i == nq)
        def _():
            if with_ctx_q:
                m = jnp.max(s_c, axis=-1, keepdims=True)
                p_c = jnp.exp(s_c - m)
                o = _dot(p_c, vc_ref[...], _NN) / jnp.sum(p_c, axis=-1, keepdims=True)
                o_ref[...] = o.astype(o_ref.dtype)
            else:
                o_ref[...] = jnp.zeros_like(o_ref)

    return _call(
        body, hosted, name=name, grid=(2, HEADS, nq + 1), in_specs=[q_spec, kc_spec, kl_spec, vc_spec, vl_spec],
        out_specs=o_spec, out_shape=jax.ShapeDtypeStruct((t, HEADS * V_HEAD), BF16),
        compiler_params=_params("parallel", "parallel", "arbitrary"),
    )(q, k, k, kv, kv)


def _attn_bwd(q, k, kv, do, with_ctx_q, geo, name, hosted=None):
    t = q.shape[0]
    tq, nq, q_spec, kc_spec, kl_spec, vc_spec, vl_spec, o_spec = _attn_specs(geo)

    def body(q_ref, kc_ref, kl_ref, vc_ref, vl_ref, do_ref, dq_ref, dkl_ref, dkc_ref, dvl_ref, dvc_ref,
             akl_ref, akc_ref, avl_ref, avc_ref):
        i = pl.program_id(2)

        @pl.when(i == 0)
        def _():
            for ref in (akl_ref, akc_ref, avl_ref, avc_ref):
                ref[...] = jnp.zeros_like(ref)

        qv, dov = q_ref[...], do_ref[...]
        s_c = _dot(qv, kc_ref[...], _NT) * QK_SCALE
        dp_c = _dot(dov, vc_ref[...], _NT)

        def ctx_part(p_c, delta):
            ds_c = (p_c * (dp_c - delta) * QK_SCALE).astype(BF16)
            akc_ref[...] += _dot(ds_c, qv, _TN)
            avc_ref[...] += _dot(p_c, dov, _TN)
            return _dot(ds_c, kc_ref[...], _NN)

        @pl.when(i < nq)
        def _():
            s_l = _dot(qv, kl_ref[...], _NT) * QK_SCALE
            m = jnp.maximum(jnp.max(s_c, axis=-1, keepdims=True), jnp.max(s_l, axis=-1, keepdims=True))
            p_c, p_l = jnp.exp(s_c - m), jnp.exp(s_l - m)
            inv = 1.0 / (jnp.sum(p_c, axis=-1, keepdims=True) + jnp.sum(p_l, axis=-1, keepdims=True))
            p_c, p_l = p_c * inv, p_l * inv
            dp_l = _dot(dov, vl_ref[...], _NT)
            delta = jnp.sum(p_c * dp_c, axis=-1, keepdims=True) + jnp.sum(p_l * dp_l, axis=-1, keepdims=True)
            ds_l = (p_l * (dp_l - delta) * QK_SCALE).astype(BF16)
            akl_ref[...] += _dot(ds_l, qv, _TN)
            avl_ref[...] += _dot(p_l, dov, _TN)
            dq_ref[...] = (ctx_part(p_c, delta) + _dot(ds_l, kl_ref[...], _NN)).astype(dq_ref.dtype)

        @pl.when(i == nq)
        def _():
            if with_ctx_q:
                m = jnp.max(s_c, axis=-1, keepdims=True)
                p_c = jnp.exp(s_c - m)
                p_c = p_c * (1.0 / jnp.sum(p_c, axis=-1, keepdims=True))
                delta = jnp.sum(p_c * dp_c, axis=-1, keepdims=True)
                dq_ref[...] = ctx_part(p_c, delta).astype(dq_ref.dtype)
            else:
                dq_ref[...] = jnp.zeros_like(dq_ref)
            dkl_ref[...] = akl_ref[...].astype(dkl_ref.dtype)
            dkc_ref[...] = akc_ref[...].astype(dkc_ref.dtype)
            dvl_ref[...] = avl_ref[...].astype(dvl_ref.dtype)
            dvc_ref[...] = avc_ref[...].astype(dvc_ref.dtype)

    def acc_spec(rows, width):
        return pl.BlockSpec((rows, width), lambda b, hh, i: (b, hh))

    return _call(
        body, hosted, name=name, grid=(2, HEADS, nq + 1), in_specs=[q_spec, kc_spec, kl_spec, vc_spec, vl_spec, o_spec],
        out_specs=(q_spec, acc_spec(geo.n_lat, HEAD_PAD), acc_spec(geo.n_ctx, HEAD_PAD), acc_spec(geo.n_lat, V_HEAD),
                   acc_spec(geo.n_ctx, V_HEAD)),
        out_shape=(jax.ShapeDtypeStruct((t, HEADS * HEAD_PAD), BF16),
                   jax.ShapeDtypeStruct((2 * geo.n_lat, HEADS * HEAD_PAD), BF16),
                   jax.ShapeDtypeStruct((2 * geo.n_ctx, HEADS * HEAD_PAD), BF16),
                   jax.ShapeDtypeStruct((2 * geo.n_lat, HEADS * V_HEAD), BF16),
                   jax.ShapeDtypeStruct((2 * geo.n_ctx, HEADS * V_HEAD), BF16)),
        scratch_shapes=[pltpu.VMEM((geo.n_lat, HEAD_PAD), F32), pltpu.VMEM((geo.n_ctx, HEAD_PAD), F32),
                        pltpu.VMEM((geo.n_lat, V_HEAD), F32), pltpu.VMEM((geo.n_ctx, V_HEAD), F32)],
        compiler_params=_params("parallel", "parallel", "arbitrary"),
    )(q, k, k, kv, kv, do)


def _mla_fwd(h, g, mod4, w, with_ctx_q, tabs, geo, tag, hosts, got):
    cos, sin = tabs
    ql, kl = w["g_qa"].shape[1], w["g_kva"].shape[1]
    kr_col = (ql + kl) // LANE
    nx = _pre_fwd(h, g, mod4, 3, geo, f"{tag}_pre")
    down = _mm(nx, w["w_a"], name=f"{tag}_down")
    cqn, ckvn = _latent_norm_fwd(down, w["g_qa"], w["g_kva"], geo, f"{tag}_lnorm")
    qraw = _mm(cqn, w["w_uq"], out_dtype=BF16, name=f"{tag}_uq")
    kvraw = _mm(ckvn, w["w_ukv"], out_dtype=BF16, name=f"{tag}_ukv")
    q = _qk_fwd(qraw, 0, qraw, HEADS, False, w["gq_n"], w["gq_r"], cos, sin, geo, f"{tag}_qnorm")
    k = _qk_fwd(kvraw, 0, down, kr_col, True, w["gk_n"], w["gk_r"], cos, sin, geo, f"{tag}_knorm")
    o = _with_host(_attn_fwd, hosts, got, "mix_a", q, k, kvraw, with_ctx_q, geo, f"{tag}_attn")
    h_out, y = _mm(o, w["w_o"], name=f"{tag}_o", gate=(h, mod4, 5, 1.0, geo))
    return h_out, (h, nx, down, cqn, ckvn, qraw, kvraw, q, k, o, y)


def _mla_bwd(dh_out, saved, g, mod4, w, with_ctx_q, tabs, geo, tag, hosts, got):
    cos, sin = tabs
    h, nx, down, cqn, ckvn, qraw, kvraw, q, k, o, y = saved
    ql, kl = w["g_qa"].shape[1], w["g_kva"].shape[1]
    kr_col = (ql + kl) // LANE
    dy, dgate = _gate_bwd(dh_out, y, mod4, 5, 1.0, geo, f"{tag}_dgate")
    do = _mm(dy, w["w_o"], tb=True, out_dtype=BF16, name=f"{tag}_do")
    dw_o = _mm(o, dy, ta=True, name=f"{tag}_dwo")
    dq, dk_lat, dk_ctx, dv_lat, dv_ctx = _with_host(_attn_bwd, hosts, got, "mix_a", q, k, kvraw, do, with_ctx_q, geo,
                                                    f"{tag}_dattn")
    dk = jnp.concatenate([dk_lat, dk_ctx], axis=0)
    dqn, dqr, dgq_n, dgq_r = _qk_bwd(qraw, 0, qraw, HEADS, False, w["gq_n"], w["gq_r"], cos, sin, dq, geo, f"{tag}_dqnorm")
    dkn, dkr, dgk_n, dgk_r = _qk_bwd(kvraw, 0, down, kr_col, True, w["gk_n"], w["gk_r"], cos, sin, dk, geo, f"{tag}_dknorm")
    dqraw = jnp.concatenate([dqn, dqr], axis=1)
    dkvraw = jnp.concatenate([dkn, jnp.concatenate([dv_lat, dv_ctx], axis=0)], axis=1)
    dcqn = _mm(dqraw, w["w_uq"], tb=True, out_dtype=BF16, name=f"{tag}_dcqn")
    dw_uq = _mm(cqn, dqraw, ta=True, name=f"{tag}_dwuq")
    dckvn = _mm(dkvraw, w["w_ukv"], tb=True, out_dtype=BF16, name=f"{tag}_dckvn")
    dw_ukv = _mm(ckvn, dkvraw, ta=True, name=f"{tag}_dwukv")
    ddown, dg_qa, dg_kva = _latent_norm_bwd(down, w["g_qa"], w["g_kva"], dcqn, dckvn, dkr, geo, f"{tag}_dlnorm")
    dnx = _mm(ddown, w["w_a"], tb=True, name=f"{tag}_dnx")
    dw_a = _mm(nx, ddown, ta=True, name=f"{tag}_dwa")
    dh, dg, dshift, dscale = _pre_bwd(h, g, mod4, 3, dnx, dh_out, geo, f"{tag}_dpre")
    grads = dict(w_a=dw_a, g_qa=dg_qa, w_uq=dw_uq, g_kva=dg_kva, w_ukv=dw_ukv, gq_n=dgq_n, gq_r=dgq_r, gk_n=dgk_n,
                 gk_r=dgk_r, w_o=dw_o)
    return dh, dg, (dshift, dscale, dgate), grads


def _mla_prepare(w_a, g_qa, w_uq, g_kva, w_ukv, g_q, g_k, w_o):
    ql, kl = g_qa.shape[0], g_kva.shape[0]
    d = w_a.shape[0]
    w_a_pad = jnp.concatenate([w_a[:, :ql + kl], _rope_swap(w_a[:, ql + kl:]), jnp.zeros((d, LANE - QK_ROPE), w_a.dtype)], axis=1)
    uq = w_uq.reshape(ql, HEADS, QK_HEAD)
    uq_r = jnp.pad(_rope_swap(uq[:, :, QK_NOPE:]), ((0, 0), (0, 0), (0, LANE - QK_ROPE)))
    w_uq_pad = jnp.concatenate([uq[:, :, :QK_NOPE].reshape(ql, HEADS * LANE), uq_r.reshape(ql, HEADS * LANE)], axis=1)
    ukv = w_ukv.reshape(kl, HEADS, QK_NOPE + V_HEAD)
    w_ukv_p = jnp.concatenate([ukv[:, :, :QK_NOPE].reshape(kl, HEADS * LANE), ukv[:, :, QK_NOPE:].reshape(kl, HEADS * V_HEAD)], axis=1)

    def gains(gv):
        gv = gv.astype(F32)
        return gv[None, :QK_NOPE], jnp.pad(_rope_swap(gv[QK_NOPE:]), (0, LANE - QK_ROPE))[None]

    gq_n, gq_r = gains(g_q)
    gk_n, gk_r = gains(g_k)
    return dict(w_a=w_a_pad, g_qa=g_qa.astype(F32)[None], w_uq=w_uq_pad, g_kva=g_kva.astype(F32)[None], w_ukv=w_ukv_p,
                gq_n=gq_n, gq_r=gq_r, gk_n=gk_n, gk_r=gk_r, w_o=w_o)


def _mla_unprepare(gr):
    ql, kl = gr["g_qa"].shape[1], gr["g_kva"].shape[1]
    dw_a = jnp.concatenate([gr["w_a"][:, :ql + kl], _rope_swap(gr["w_a"][:, ql + kl:ql + kl + QK_ROPE])], axis=1)
    uqn = gr["w_uq"][:, :HEADS * LANE].reshape(ql, HEADS, LANE)
    uqr = _rope_swap(gr["w_uq"][:, HEADS * LANE:].reshape(ql, HEADS, LANE)[:, :, :QK_ROPE])
    dw_uq = jnp.concatenate([uqn, uqr], axis=2).reshape(ql, HEADS * QK_HEAD)
    ukn = gr["w_ukv"][:, :HEADS * LANE].reshape(kl, HEADS, LANE)
    ukv = gr["w_ukv"][:, HEADS * LANE:].reshape(kl, HEADS, V_HEAD)
    dw_ukv = jnp.concatenate([ukn, ukv], axis=2).reshape(kl, HEADS * (QK_NOPE + V_HEAD))

    def gains(gn, grr):
        return jnp.concatenate([gn[0], _rope_swap(grr[0, :QK_ROPE])])

    return dict(mla_w_a=dw_a, mla_g_qa=gr["g_qa"][0], mla_w_uq=dw_uq, mla_g_kva=gr["g_kva"][0], mla_w_ukv=dw_ukv,
                mla_g_q=gains(gr["gq_n"], gr["gq_r"]), mla_g_k=gains(gr["gk_n"], gr["gk_r"]), mla_w_o=gr["w_o"])


def _loss_head(h, target, geo, name):
    t, d = h.shape
    tile = geo.tile
    n_lat_tiles = 2 * geo.n_lat // tile

    def body(h_ref, t_ref, dh_ref, loss_ref):
        i = pl.program_id(0)

        @pl.when(i == 0)
        def _():
            loss_ref[...] = jnp.zeros_like(loss_ref)

        @pl.when(i < n_lat_tiles)
        def _():
            e = h_ref[...] - t_ref[...]
            dh_ref[...] = e * (1.0 / d)
            part = jnp.sum(e * e, axis=0, keepdims=True) * (0.5 / d)
            loss_ref[...] += sum(part[:, j * LANE:(j + 1) * LANE] for j in range(d // LANE))

        @pl.when(i >= n_lat_tiles)
        def _():
            dh_ref[...] = jnp.zeros_like(dh_ref)

    row = pl.BlockSpec((tile, d), lambda i: (i, 0))
    tgt = pl.BlockSpec((tile, d), lambda i: (jnp.minimum(i, n_lat_tiles - 1), 0))
    dh, loss = pl.pallas_call(
        body, name=name, grid=(t // tile,), in_specs=[row, tgt],
        out_specs=(row, pl.BlockSpec((1, LANE), lambda i: (0, 0))),
        out_shape=(jax.ShapeDtypeStruct((t, d), F32), jax.ShapeDtypeStruct((1, LANE), F32)),
        compiler_params=_params("arbitrary"),
    )(h, target)
    return jnp.sum(loss), dh


def _adamw(w, g, m, v, name):
    shape = w.shape
    cols = shape[-1]
    rows = int(np.prod(shape[:-1])) if len(shape) > 1 else 1
    w2, g2, m2, v2 = (a.reshape(rows, cols) for a in (w, g, m, v))
    tr = _pick(rows, (512, 256, 128, 64, 32, 16, 8))
    c1 = 1.0 / (1.0 - ADAM_B1 ** ADAM_STEP)
    c2 = 1.0 / (1.0 - ADAM_B2 ** ADAM_STEP)

    def body(w_ref, g_ref, m_ref, v_ref, d_ref, mo_ref, vo_ref):
        gv = g_ref[...]
        mn = ADAM_B1 * m_ref[...] + (1.0 - ADAM_B1) * gv
        vn = ADAM_B2 * v_ref[...] + (1.0 - ADAM_B2) * (gv * gv)
        d_ref[...] = -ADAM_LR * ((mn * c1) / (jnp.sqrt(vn * c2) + ADAM_EPS) + ADAM_WD * w_ref[...])
        mo_ref[...] = mn
        vo_ref[...] = vn

    blk = pl.BlockSpec((tr, cols), lambda i: (i, 0))
    sds = jax.ShapeDtypeStruct((rows, cols), F32)
    d, mo, vo = pl.pallas_call(
        body, name=name, grid=(rows // tr,), in_specs=[blk] * 4, out_specs=(blk,) * 3, out_shape=(sds,) * 3,
        compiler_params=_params("parallel"),
    )(w2, g2, m2, v2)
    return d.reshape(shape), mo.reshape(shape), vo.reshape(shape)


SHARD_AXIS = {
    "w_mod": 2, "g_norm": 2, "ffn_w1": 3, "ffn_w3": 3, "ffn_w2": 2, "sc_w_in": 2, "sc_conv": 2, "sc_w_out": 1,
    "mla_w_a": 1, "mla_g_qa": 1, "mla_w_uq": 2, "mla_w_ukv": 2, "mla_w_o": 1,
}
HIDDEN_MAJOR = ("ffn_w1", "ffn_w3")


def _view(name, arr, swapped=False):
    form, swap, _ = EXCHANGE[name]
    if swap and not swapped:
        arr = jnp.swapaxes(arr, -1, -2)
    if form == "mid":
        arr = arr.reshape((-1,) + arr.shape[-2:])
        return jnp.pad(arr, ((0, 0), (0, 0), (0, -arr.shape[-1] % LANE)))
    arr = arr.reshape(-1, arr.shape[-1])
    return jnp.pad(arr, ((0, -arr.shape[0] % 16), (0, 0)))


def _unview(name, view, shape, keep_swapped=False):
    form, swap, _ = EXCHANGE[name]
    shape = shape[:-2] + (shape[-1], shape[-2]) if swap else shape
    if form == "mid":
        view = view[:, :, :shape[-1]]
    else:
        view = view[:int(np.prod(shape[:-1]))]
    arr = view.reshape(shape)
    return arr if (not swap or keep_swapped) else jnp.swapaxes(arr, -1, -2)


def _full_shape(name, local_shape):
    ax = SHARD_AXIS[name]
    return local_shape[:ax] + (N_DEV * local_shape[ax],) + local_shape[ax + 1:]


def _win(ref, form, n, j):
    start = j * n
    if not isinstance(start, int):
        start = pl.multiple_of(start, LANE if form == "last" else math.gcd(n, 16))
    if form == "mid":
        return ref.at[:, pl.ds(start, n), :]
    return ref.at[:, pl.ds(start, n)]


def _windows(view, count, of):
    return view.shape[:1] + (view.shape[1] * count // of,) + view.shape[2:]


def _gather_work(views, forms):
    na = len(views)

    def plan(x_refs, out_refs, sems):
        send_sems, recv_sems, local_sems = sems
        x, y, c = lax.axis_index("x"), lax.axis_index("y"), lax.axis_index("c")
        me, sibling = (x, y, c), (x, y, 1 - c)
        chips = [(1 - x, y), (x, 1 - y), (1 - x, 1 - y)]

        def copy(a, k, block, to, from_input):
            dst = _win(out_refs[a], forms[a], views[a].shape[1], 4 * block[0] + 2 * block[1] + block[2])
            return pltpu.make_async_remote_copy(
                src_ref=x_refs[a] if from_input else dst, dst_ref=dst, send_sem=send_sems.at[a, k],
                recv_sem=recv_sems.at[a, k], device_id=to, device_id_type=MESH)

        mine = [pltpu.make_async_copy(x_refs[a], _win(out_refs[a], forms[a], views[a].shape[1], 4 * x + 2 * y + c),
                                      local_sems.at[a]) for a in range(na)]
        first = []
        for a in range(na):
            first.append(copy(a, 0, me, sibling, True))
            first += [copy(a, 1 + j, me, (*chip, c), True) for j, chip in enumerate(chips)]
        return copy, mine, first, me, sibling, chips, c

    def start(x_refs, out_refs, sems):
        _, mine, first, *_ = plan(x_refs, out_refs, sems)
        for cp in mine + first:
            cp.start()

    def finish(x_refs, out_refs, sems):
        copy, mine, first, me, sibling, chips, c = plan(x_refs, out_refs, sems)
        passed = []
        for j, chip in enumerate(chips):
            for a in range(na):
                copy(a, 1 + j, (*chip, c), me, False).wait_recv()
                fwd = copy(a, 4 + j, (*chip, c), sibling, False)
                fwd.start()
                passed.append(fwd)
        for a in range(na):
            copy(a, 0, sibling, me, False).wait_recv()
            for j, chip in enumerate(chips):
                copy(a, 4 + j, (*chip, 1 - c), me, False).wait_recv()
        for cp in first + passed:
            cp.wait_send()
        for cp in mine:
            cp.wait()

    return Hosted(
        list(views), [jax.ShapeDtypeStruct(_windows(v, N_DEV, 1), v.dtype) for v in views],
        [pltpu.SemaphoreType.DMA((na, 7)), pltpu.SemaphoreType.DMA((na, 7)), pltpu.SemaphoreType.DMA((na,))], start, finish)


def _push_work(srcs, out_shapes, n_copies, make_copies):
    na = len(srcs)

    def start(s_refs, r_refs, sems):
        for cp in make_copies(s_refs, r_refs, sems[0], sems[1]):
            cp.start()

    def finish(s_refs, r_refs, sems):
        copies = make_copies(s_refs, r_refs, sems[0], sems[1])
        for cp in copies:
            cp.wait_recv()
        for cp in copies:
            cp.wait_send()

    return Hosted(list(srcs), out_shapes, [pltpu.SemaphoreType.DMA((na, n_copies)), pltpu.SemaphoreType.DMA((na, n_copies))],
                  start, finish)


def _sibling_work(fulls, forms):
    na = len(fulls)
    widths = [f.shape[1] // N_DEV for f in fulls]

    def make_copies(g_refs, r_refs, send_sems, recv_sems):
        x, y, c = lax.axis_index("x"), lax.axis_index("y"), lax.axis_index("c")
        return [
            pltpu.make_async_remote_copy(
                src_ref=_win(g_refs[a], forms[a], widths[a], 2 * chip + (1 - c)),
                dst_ref=_win(r_refs[a], forms[a], widths[a], chip), send_sem=send_sems.at[a, chip],
                recv_sem=recv_sems.at[a, chip], device_id=(x, y, 1 - c), device_id_type=MESH)
            for a in range(na) for chip in range(N_CHIP)
        ]

    return _push_work(fulls, [jax.ShapeDtypeStruct(_windows(f, N_CHIP, N_DEV), f.dtype) for f in fulls], N_CHIP, make_copies)


def _chip_work(parts, forms):
    na = len(parts)
    widths = [p.shape[1] // N_CHIP for p in parts]

    def make_copies(p_refs, r_refs, send_sems, recv_sems):
        x, y, c = lax.axis_index("x"), lax.axis_index("y"), lax.axis_index("c")
        chips = [(1 - x, y), (x, 1 - y), (1 - x, 1 - y)]
        return [
            pltpu.make_async_remote_copy(
                src_ref=_win(p_refs[a], forms[a], widths[a], 2 * px + py), dst_ref=_win(r_refs[a], forms[a], widths[a], j),
                send_sem=send_sems.at[a, j], recv_sem=recv_sems.at[a, j], device_id=(px, py, c), device_id_type=MESH)
            for a in range(na) for j, (px, py) in enumerate(chips)
        ]

    return _push_work(parts, [jax.ShapeDtypeStruct(_windows(p, 3, N_CHIP), p.dtype) for p in parts], 3, make_copies)


def _sum_tiles(view, form, n):
    if form == "mid":
        tr = n
        while tr * view.shape[2] * 4 > 2 * 1024 * 1024 and tr % 32 == 0:
            tr //= 2
        return 1, tr
    return _pick(view.shape[0], (512, 256, 128, 64, 32, 16)), n


def _window_spec(form, tl, tr, rest, window_of):
    if form == "mid":
        return lambda per: pl.BlockSpec((None, tr) + rest, lambda l, k, i, s: (l, window_of(k, s) * per + i, 0))
    return lambda per: pl.BlockSpec((tl, tr), lambda l, k, i, s: (l, window_of(k, s)))


def _chip_partials(g, recv, core, form, name):
    n = g.shape[1] // N_DEV
    tl, tr = _sum_tiles(g, form, n)
    per = n // tr
    rest = tuple(g.shape[2:])

    def body(core_ref, g_ref, r_ref, o_ref):
        o_ref[...] = (g_ref[...] + r_ref[...]).astype(o_ref.dtype)

    own = _window_spec(form, tl, tr, rest, lambda k, s: 2 * k + s[0])(per)
    by_chip = _window_spec(form, tl, tr, rest, lambda k, s: k)(per)
    return pl.pallas_call(
        body, name=name,
        grid_spec=pltpu.PrefetchScalarGridSpec(
            num_scalar_prefetch=1, grid=(g.shape[0] // tl, N_CHIP, per), in_specs=[own, by_chip], out_specs=by_chip),
        out_shape=jax.ShapeDtypeStruct(recv.shape, BF16), compiler_params=_params("parallel", "parallel", "parallel"),
    )(core, g, recv)


def _reduce_final(p, recv, chip, form, name):
    n = p.shape[1] // N_CHIP
    tl, tr = _sum_tiles(p, form, n)
    per = n // tr
    rest = tuple(p.shape[2:])

    def body(chip_ref, p_ref, ry_ref, rx_ref, rxy_ref, o_ref):
        own_pair = p_ref[...].astype(F32) + ry_ref[...].astype(F32)
        o_ref[...] = own_pair + (rx_ref[...].astype(F32) + rxy_ref[...].astype(F32))

    def rel(j):
        return _window_spec(form, tl, tr, rest, lambda k, s: j)(per)

    own = _window_spec(form, tl, tr, rest, lambda k, s: s[0])(per)
    return pl.pallas_call(
        body, name=name,
        grid_spec=pltpu.PrefetchScalarGridSpec(
            num_scalar_prefetch=1, grid=(p.shape[0] // tl, 1, per), in_specs=[own, rel(1), rel(0), rel(2)],
            out_specs=rel(0)),
        out_shape=jax.ShapeDtypeStruct(p.shape[:1] + (n,) + p.shape[2:], F32),
        compiler_params=_params("parallel", "parallel", "parallel"),
    )(chip, p, recv, recv, recv)


def _pack_replicated(arrays):
    pieces = []
    for a in arrays:
        flat = a.reshape(-1).astype(F32)
        pieces.append(jnp.pad(flat, (0, -flat.size % LANE)))
    total = sum(p.size for p in pieces)
    pieces.append(jnp.zeros((-total % (16 * LANE),), F32))
    return jnp.concatenate(pieces).reshape(-1, LANE)


def _unpack_replicated(buf, shapes):
    flat, out, off = buf.reshape(-1), [], 0
    for shape in shapes:
        size = int(np.prod(shape))
        out.append(flat[off:off + size].reshape(shape))
        off += size + (-size % LANE)
    return out


def _silu(v):
    return v * jax.nn.sigmoid(v)


SC_NAMES = ("sc_w_in", "sc_conv", "sc_w_out")
MLA_SHARDED = ("mla_w_a", "mla_g_qa", "mla_w_uq", "mla_w_ukv", "mla_w_o")
MLA_NAMES = ("mla_w_a", "mla_g_qa", "mla_w_uq", "mla_g_kva", "mla_w_ukv", "mla_g_q", "mla_g_k", "mla_w_o")


def _local_step(src, x, c, ctx, target):
    bsz, n_lat, d = x.shape
    n_ctx = ctx.shape[1]
    assert bsz == 2
    geo = Geo(n_lat, n_ctx)
    depth = src.depth
    tc = _conv_tile(d)
    tabs = _rope_tables(geo)

    h = jnp.concatenate([x.reshape(2 * n_lat, d), ctx.reshape(2 * n_ctx, d)], axis=0)
    tgt = target.reshape(2 * n_lat, d)
    cond = jnp.concatenate([c, src.c_ctx[None], jnp.zeros((8 - bsz - 1, d), F32)], axis=0)
    scond = _silu(cond)

    saved = []
    for i in range(depth):
        kind = i % 2
        wl, slots = src.weights(i), src.fwd_slots(i)
        gn = wl["g_norm"].astype(F32)
        mod = _mm(scond, wl["w_mod"], name=f"l{i}_mod") + wl["b_mod"][None]
        mod4 = mod[:N_SEG].reshape(N_SEG, N_MOD, 1, d)
        h, s1 = _ffn_fwd(h, gn[0:1], mod4, 0, wl, 0, geo, f"l{i}_f1", "f1", slots, slots)
        if kind == 0:
            mix = (_interleave(wl["sc_w_in"], 3, tc), wl["sc_conv"].astype(F32), wl["sc_w_out"])
            h, s2 = _sconv_fwd(h, gn[1:2], mod4, *mix, geo, f"l{i}_sc", slots, slots)
        else:
            mix = _mla_prepare(*[wl[name] for name in MLA_NAMES])
            h, s2 = _mla_fwd(h, gn[1:2], mod4, mix, i != depth - 1, tabs, geo, f"l{i}_mla", slots, slots)
        h, s3 = _ffn_fwd(h, gn[2:3], mod4, 6, wl, 1, geo, f"l{i}_f2", "f2", slots, slots)
        saved.append((wl, gn, mod4, mix, s1, s2, s3))

    loss, dh = _loss_head(h, tgt, geo, "loss_head")

    g_b_mod = [None] * depth
    dscond = jnp.zeros_like(scond)
    for i in reversed(range(depth)):
        kind = i % 2
        wl, gn, mod4, mix, s1, s2, s3 = saved[i]
        slots = src.bwd_slots(i)
        gbuf = {name: lax.empty(wl[name].shape, F32) for name in ("ffn_w1", "ffn_w3", "ffn_w2")}
        dh, dg2, dm2 = _ffn_bwd(dh, s3, gn[2:3], mod4, 6, wl, 1, gbuf, geo, f"l{i}_f2", "f2", slots, slots)
        if kind == 0:
            dh, dg1, dm1, dwin, dconv, dwout = _sconv_bwd(dh, s2, gn[1:2], mod4, *mix, geo, f"l{i}_sc", slots, slots)
            gl = dict(sc_w_in=_deinterleave(dwin, 3, tc), sc_conv=dconv, sc_w_out=dwout)
        else:
            dh, dg1, dm1, gm = _mla_bwd(dh, s2, gn[1:2], mod4, mix, i != depth - 1, tabs, geo, f"l{i}_mla", slots, slots)
            gl = _mla_unprepare(gm)
        dh, dg0, dm0 = _ffn_bwd(dh, s1, gn[0:1], mod4, 0, wl, 0, gbuf, geo, f"l{i}_f1", "f1", slots, slots)
        dmod = jnp.concatenate(list(dm0) + list(dm1) + list(dm2), axis=1).reshape(N_SEG, N_MOD * d)
        dmod8 = jnp.concatenate([dmod, jnp.zeros((8 - N_SEG, N_MOD * d), F32)], axis=0)
        g_b_mod[i] = jnp.sum(dmod, axis=0)
        gl.update(gbuf, g_norm=jnp.concatenate([dg0, dg1, dg2], axis=0),
                  w_mod=_mm(scond, dmod8, ta=True, name=f"l{i}_dwmod"))
        dscond = dscond + _mm(dmod8, wl["w_mod"], tb=True, name=f"l{i}_dcond")
        src.grads(i, gl)

    sg = jax.nn.sigmoid(cond)
    dcond = dscond * (sg * (1.0 + cond * (1.0 - sg)))
    grad_x = dh[:2 * n_lat].reshape(x.shape)
    return loss, grad_x, dcond[bsz], jnp.stack(g_b_mod)


class _Slots:
    def __init__(self, get, put):
        self.get, self._put = get, put

    def __setitem__(self, slot, outs):
        self._put(slot, outs)


FWD_PLAN = {
    0: {"f1_up": ("ffn_w1",), "f1_down": ("g_norm", "mix"), "mix_a": ("ffn_w3",), "mix_b": ("w_mod",), "f2_up": ("ffn_w2",)},
    1: {"f1_up": ("ffn_w1",), "mix_a": ("w_mod", "ffn_w3", "g_norm", "mix"), "f2_up": ("ffn_w2",)},
}
BWD_PLAN = {
    0: {"f2_dnx": ("ffn_w1",), "mix_a": ("w_mod",), "mix_b": ("ffn_w3",), "f1_dact": ("ffn_w2",), "f1_dnx": ("g_norm", "mix")},
    1: {"f2_dnx": ("ffn_w1",), "mix_a": ("w_mod", "ffn_w3", "ffn_w2"), "f1_dnx": ("g_norm", "mix")},
}


class _Exchange:
    def __init__(self, w):
        self.w = w
        self.depth = w["w_mod"].shape[0]
        self.c_ctx = w["c_ctx"]
        self.core = lax.axis_index("c").astype(jnp.int32).reshape(1)
        self.chip = (2 * lax.axis_index("x") + lax.axis_index("y")).astype(jnp.int32).reshape(1)
        self.full, self.gviews, self.parts, self.reduced, self.rep = {}, {}, {}, {}, {}

    def _layer_of(self, name, i):
        return i // 2 if name.startswith(("sc_", "mla_")) else i

    def _mixer(self, i):
        return SC_NAMES if i % 2 == 0 else MLA_SHARDED

    def _expand(self, names, i):
        out = []
        for name in names:
            out += list(self._mixer(i)) if name == "mix" else [name]
        return out

    def _group(self, i):
        return ["w_mod", "g_norm", "ffn_w1", "ffn_w3", "ffn_w2"] + list(self._mixer(i))

    def _local(self, name, i):
        arr = self.w[name][self._layer_of(name, i)]
        return arr[:, None] if name == "mla_g_qa" else arr

    def _shapes(self, name, i):
        local = tuple(self._local(name, i).shape)
        ax = SHARD_AXIS[name] - 1
        return local, local[:ax] + (N_DEV * local[ax],) + local[ax + 1:]

    def _gather(self, names, i):
        views = [_view(n, self._local(n, i).astype(BF16 if EXCHANGE[n][2] else F32)) for n in names]
        return _gather_work(views, [EXCHANGE[n][0] for n in names])

    def _gathered(self, names, i, outs):
        for name, fv in zip(names, outs):
            arr = _unview(name, fv, self._shapes(name, i)[1], keep_swapped=name in HIDDEN_MAJOR)
            self.full[name, i] = arr[:, 0] if name == "mla_g_qa" else arr

    def prefetch(self):
        names = self._group(0)
        self._gathered(names, 0, _run_hosted(self._gather(names, 0), "gather_l0"))

    def weights(self, i):
        wl = {name: self.full[name, i] for name in self._group(i)}
        wl["b_mod"] = self.w["b_mod"][i]
        if i % 2 == 1:
            for name in ("mla_g_kva", "mla_g_q", "mla_g_k"):
                wl[name] = self.w[name][i // 2]
        return wl

    def fwd_slots(self, i):
        plan = FWD_PLAN[i % 2] if i + 1 < self.depth else {}
        names = {slot: self._expand(plan[slot], i + 1) for slot in plan}
        return _Slots(lambda slot: self._gather(names[slot], i + 1) if slot in names else None,
                      lambda slot, outs: self._gathered(names[slot], i + 1, outs))

    def grads(self, i, gl):
        for name in self._group(i):
            g = gl[name][:, None] if name == "mla_g_qa" else gl[name]
            self.gviews[name, i] = _view(name, g, swapped=name in HIDDEN_MAJOR)
        for name in REPLICATED:
            if name in gl:
                self.rep[name, i // 2] = gl[name]

    def _forms(self, names):
        return [EXCHANGE[n][0] for n in names]

    def _partials(self, names, i, from_sibling):
        for name, recv in zip(names, from_sibling):
            self.parts[name, i] = _chip_partials(self.gviews[name, i], recv, self.core, EXCHANGE[name][0],
                                                 f"partial_{name}_{i}")

    def _finals(self, names, i, from_chips):
        for name, recv in zip(names, from_chips):
            rv = _reduce_final(self.parts[name, i], recv, self.chip, EXCHANGE[name][0], f"final_{name}_{i}")
            arr = _unview(name, rv, self._shapes(name, i)[0])
            self.reduced[name, i] = arr[:, 0] if name == "mla_g_qa" else arr

    def bwd_slots(self, i):
        if i + 1 >= self.depth:
            return _Slots(lambda slot: None, None)
        group = self._group(i + 1)
        plan = BWD_PLAN[i % 2]
        names = {slot: self._expand(plan[slot], i + 1) for slot in plan}

        def get(slot):
            if slot == "f2_dact":
                return _sibling_work([self.gviews[n, i + 1] for n in group], self._forms(group))
            if slot in names:
                return _chip_work([self.parts[n, i + 1] for n in names[slot]], self._forms(names[slot]))
            return None

        def put(slot, outs):
            if slot == "f2_dact":
                self._partials(group, i + 1, outs)
            else:
                self._finals(names[slot], i + 1, outs)

        return _Slots(get, put)

    def finish(self, rep_grads):
        group = self._group(0)
        for name in REPLICATED:
            if name not in rep_grads:
                rep_grads[name] = jnp.stack([self.rep[name, j] for j in range(self.w[name].shape[0])])
        rep = _pack_replicated([rep_grads[name] for name in REPLICATED])
        views = [self.gviews[n, 0] for n in group] + [jnp.tile(rep[None], (1, N_DEV, 1))]
        forms = self._forms(group) + ["mid"]
        from_sibling = _run_hosted(_sibling_work(views, forms), "reduce_sibling_l0")
        self._partials(group, 0, from_sibling[:-1])
        rep_part = _chip_partials(views[-1], from_sibling[-1], self.core, "mid", "partial_replicated")
        parts = [self.parts[n, 0] for n in group] + [rep_part]
        from_chips = _run_hosted(_chip_work(parts, forms), "reduce_chips_l0")
        self._finals(group, 0, from_chips[:-1])
        rep_sum = _reduce_final(rep_part, from_chips[-1], self.chip, "mid", "final_replicated")
        out = dict(zip(REPLICATED, _unpack_replicated(rep_sum, [self.w[name].shape for name in REPLICATED])))
        for name in EXCHANGE:
            layers = range(self.w[name].shape[0])
            step = 2 if name.startswith(("sc_", "mla_")) else 1
            first = 1 if name.startswith("mla_") else 0
            out[name] = jnp.stack([self.reduced[name, first + step * l] for l in layers])
        return out


def kernel(x, c, ctx, c_ctx, w_mod, b_mod, g_norm, ffn_w1, ffn_w3, ffn_w2, sc_w_in, sc_conv, sc_w_out, mla_w_a, mla_g_qa, mla_w_uq, mla_g_kva, mla_w_ukv, mla_g_q, mla_g_k, mla_w_o, loss_target, m_c_ctx, m_w_mod, m_b_mod, m_g_norm, m_ffn_w1, m_ffn_w3, m_ffn_w2, m_sc_w_in, m_sc_conv, m_sc_w_out, m_mla_w_a, m_mla_g_qa, m_mla_w_uq, m_mla_g_kva, m_mla_w_ukv, m_mla_g_q, m_mla_g_k, m_mla_w_o, v_c_ctx, v_w_mod, v_b_mod, v_g_norm, v_ffn_w1, v_ffn_w3, v_ffn_w2, v_sc_w_in, v_sc_conv, v_sc_w_out, v_mla_w_a, v_mla_g_qa, v_mla_w_uq, v_mla_g_kva, v_mla_w_ukv, v_mla_g_q, v_mla_g_k, v_mla_w_o):
    w = dict(c_ctx=c_ctx, w_mod=w_mod, b_mod=b_mod, g_norm=g_norm, ffn_w1=ffn_w1, ffn_w3=ffn_w3, ffn_w2=ffn_w2,
             sc_w_in=sc_w_in, sc_conv=sc_conv, sc_w_out=sc_w_out, mla_w_a=mla_w_a, mla_g_qa=mla_g_qa, mla_w_uq=mla_w_uq,
             mla_g_kva=mla_g_kva, mla_w_ukv=mla_w_ukv, mla_g_q=mla_g_q, mla_g_k=mla_g_k, mla_w_o=mla_w_o)
    m = dict(c_ctx=m_c_ctx, w_mod=m_w_mod, b_mod=m_b_mod, g_norm=m_g_norm, ffn_w1=m_ffn_w1, ffn_w3=m_ffn_w3,
             ffn_w2=m_ffn_w2, sc_w_in=m_sc_w_in, sc_conv=m_sc_conv, sc_w_out=m_sc_w_out, mla_w_a=m_mla_w_a,
             mla_g_qa=m_mla_g_qa, mla_w_uq=m_mla_w_uq, mla_g_kva=m_mla_g_kva, mla_w_ukv=m_mla_w_ukv, mla_g_q=m_mla_g_q,
             mla_g_k=m_mla_g_k, mla_w_o=m_mla_w_o)
    v = dict(c_ctx=v_c_ctx, w_mod=v_w_mod, b_mod=v_b_mod, g_norm=v_g_norm, ffn_w1=v_ffn_w1, ffn_w3=v_ffn_w3,
             ffn_w2=v_ffn_w2, sc_w_in=v_sc_w_in, sc_conv=v_sc_conv, sc_w_out=v_sc_w_out, mla_w_a=v_mla_w_a,
             mla_g_qa=v_mla_g_qa, mla_w_uq=v_mla_w_uq, mla_g_kva=v_mla_g_kva, mla_w_ukv=v_mla_w_ukv, mla_g_q=v_mla_g_q,
             mla_g_k=v_mla_g_k, mla_w_o=v_mla_w_o)
    exchange = _Exchange(w)
    exchange.prefetch()
    loss, grad_x, g_c_ctx, g_b_mod = _local_step(exchange, x, c, ctx, loss_target)
    loss = lax.psum(loss, ("x", "y", "c"))
    reduced = exchange.finish(dict(c_ctx=g_c_ctx, b_mod=g_b_mod))

    outs = [[], [], [], []]
    for name in WEIGHTS:
        delta, new_m, new_v = _adamw(w[name], reduced[name], m[name], v[name], f"adamw_{name}")
        for lst, val in zip(outs, (reduced[name], delta, new_m, new_v)):
            lst.append(val)
    return (loss, grad_x, *outs[0], *outs[1], *outs[2], *outs[3])
```

```python
import functools
import math

import jax
import jax.numpy as jnp
import numpy as np
from jax import lax
from jax.experimental import pallas as pl
from jax.experimental.pallas import tpu as pltpu

F32 = jnp.float32
BF16 = jnp.bfloat16

N_MOD = 9
HEADS = 8
QK_NOPE = 128
QK_ROPE = 64
QK_HEAD = QK_NOPE + QK_ROPE
V_HEAD = 128
GRID_W = 64
ROPE_BASE = 10000.0
QK_SCALE = QK_HEAD ** -0.5
EPS = 1e-6
ADAM_LR, ADAM_B1, ADAM_B2, ADAM_EPS, ADAM_WD, ADAM_STEP = 0.001, 0.9, 0.999, 1e-08, 0.01, 10

N_DEV = 8
N_CHIP = 4
N_SEG = 3
LANE = 128
HEAD_PAD = 2 * LANE
VMEM_LIMIT_BYTES = 48 * 1024 * 1024
MESH = pl.DeviceIdType.MESH

WEIGHTS = ["c_ctx", "w_mod", "b_mod", "g_norm", "ffn_w1", "ffn_w3", "ffn_w2", "sc_w_in", "sc_conv", "sc_w_out",
           "mla_w_a", "mla_g_qa", "mla_w_uq", "mla_g_kva", "mla_w_ukv", "mla_g_q", "mla_g_k", "mla_w_o"]
EXCHANGE = {
    "w_mod": ("last", False, True), "ffn_w1": ("mid", True, True), "ffn_w3": ("mid", True, True),
    "ffn_w2": ("mid", False, True), "sc_w_in": ("last", False, True), "sc_w_out": ("mid", False, True),
    "mla_w_a": ("mid", False, True), "mla_w_uq": ("mid", True, True), "mla_w_ukv": ("last", False, True),
    "mla_w_o": ("mid", False, True), "g_norm": ("last", False, False), "sc_conv": ("last", False, False),
    "mla_g_qa": ("mid", False, False),
}
REPLICATED = ["c_ctx", "b_mod", "mla_g_kva", "mla_g_q", "mla_g_k"]


def _pick(n, cands):
    for cand in cands:
        if n % cand == 0:
            return cand
    return n


def _params(*sem):
    return pltpu.CompilerParams(dimension_semantics=sem, vmem_limit_bytes=VMEM_LIMIT_BYTES)


def _hbm():
    return pl.BlockSpec(memory_space=pl.ANY)


class Hosted:
    def __init__(self, inputs, out_shapes, scratch, start, finish):
        self.inputs, self.out_shapes, self.scratch, self.start, self.finish = inputs, out_shapes, scratch, start, finish


def _call(body, hosted, **kw):
    if hosted is None:
        return pl.pallas_call(body, **kw)
    single = not isinstance(kw["out_shape"], (tuple, list))
    out_shape = [kw["out_shape"]] if single else list(kw["out_shape"])
    out_specs = [kw["out_specs"]] if single else list(kw["out_specs"])
    in_specs, scratch, grid = list(kw["in_specs"]), list(kw.get("scratch_shapes", ())), kw["grid"]
    n_in, n_out, n_scr = len(in_specs), len(out_shape), len(scratch)
    h_in, h_out = len(hosted.inputs), len(hosted.out_shapes)

    def wrapped(*refs):
        ins, hins = refs[:n_in], refs[n_in:n_in + h_in]
        o0 = n_in + h_in
        outs, houts = refs[o0:o0 + n_out], refs[o0 + n_out:o0 + n_out + h_out]
        s0 = o0 + n_out + h_out
        scr, hscr = refs[s0:s0 + n_scr], refs[s0 + n_scr:]
        first = functools.reduce(jnp.logical_and, [pl.program_id(a) == 0 for a in range(len(grid))])
        last = functools.reduce(jnp.logical_and, [pl.program_id(a) == g - 1 for a, g in enumerate(grid)])

        @pl.when(first)
        def _():
            hosted.start(hins, houts, hscr)

        body(*ins, *outs, *scr)

        @pl.when(last)
        def _():
            hosted.finish(hins, houts, hscr)

    call = pl.pallas_call(
        wrapped, name=kw["name"], grid=grid, in_specs=in_specs + [_hbm()] * h_in,
        out_specs=tuple(out_specs + [_hbm()] * h_out), out_shape=tuple(out_shape + list(hosted.out_shapes)),
        scratch_shapes=scratch + list(hosted.scratch), input_output_aliases=kw.get("input_output_aliases", {}),
        compiler_params=_params(*["arbitrary"] * len(grid)))

    def run(*args):
        res = call(*args, *hosted.inputs)
        comp = res[:n_out]
        return (comp[0] if single else tuple(comp)), list(res[n_out:])

    return run


def _run_hosted(hosted, name):
    def body(*refs):
        h_in, h_out = len(hosted.inputs), len(hosted.out_shapes)
        hins, houts, hscr = refs[:h_in], refs[h_in:h_in + h_out], refs[h_in + h_out:]
        hosted.start(hins, houts, hscr)
        hosted.finish(hins, houts, hscr)

    return list(pl.pallas_call(
        body, name=name, in_specs=[_hbm()] * len(hosted.inputs), out_specs=tuple([_hbm()] * len(hosted.out_shapes)),
        out_shape=tuple(hosted.out_shapes), scratch_shapes=list(hosted.scratch))(*hosted.inputs))


class Geo:
    def __init__(self, n_lat, n_ctx):
        self.n_lat, self.n_ctx = n_lat, n_ctx
        self.rows = 2 * n_lat + 2 * n_ctx
        self.tile = n_ctx
        assert n_lat % n_ctx == 0 and n_ctx % 16 == 0
        self.mm_tile = _pick(n_lat, (512, 256, 128)) if self.rows % _pick(n_lat, (512, 256, 128)) == 0 else n_ctx
        self.big_tile = _pick(self.rows, (1536, 768, 512, 256))

    def seg(self, i, tile):
        return jnp.minimum((i * tile) // self.n_lat, N_SEG - 1)

    def seg_start(self, i, tile):
        row = i * tile
        return jnp.logical_or(row % self.n_lat == 0, row == 2 * self.n_lat) & (row <= 2 * self.n_lat)


_NT = (((1,), (1,)), ((), ()))
_NN = (((1,), (0,)), ((), ()))
_TN = (((0,), (0,)), ((), ()))


def _dot(a, b, dims):
    return lax.dot_general(a.astype(BF16), b.astype(BF16), dims, preferred_element_type=F32)


def _mm(a, b, *, ta=False, tb=False, out_dtype=F32, name, gate=None, hosted=None):
    (kdim, m) = a.shape if ta else a.shape[::-1]
    n = b.shape[0] if tb else b.shape[1]
    assert (b.shape[1] if tb else b.shape[0]) == kdim
    if gate is not None:
        tm = gate[4].mm_tile
    else:
        tm = _pick(m, (512, 256, 128))
    tn = _pick(n, (512, 256, 128))
    tk = _pick(kdim, (1024, 512, 256, 128))
    nk = kdim // tk
    dims = (((0 if ta else 1,), (1 if tb else 0,)), ((), ()))

    def body(*refs):
        if gate is not None:
            a_ref, b_ref, res_ref, gate_ref, o_ref, y_ref, acc_ref = refs
        else:
            a_ref, b_ref, o_ref, acc_ref = refs
        kk = pl.program_id(2)

        @pl.when(kk == 0)
        def _():
            acc_ref[...] = jnp.zeros_like(acc_ref)

        acc_ref[...] += lax.dot_general(a_ref[...].astype(BF16), b_ref[...].astype(BF16), dims,
                                        preferred_element_type=F32)

        @pl.when(kk == nk - 1)
        def _():
            acc = acc_ref[...]
            if gate is not None:
                y_ref[...] = acc.astype(y_ref.dtype)
                o_ref[...] = res_ref[...] + (gate[3] * gate_ref[...]) * acc
            else:
                o_ref[...] = acc.astype(o_ref.dtype)

    a_spec = pl.BlockSpec((tk, tm), lambda i, j, k: (k, i)) if ta else pl.BlockSpec((tm, tk), lambda i, j, k: (i, k))
    b_spec = pl.BlockSpec((tn, tk), lambda i, j, k: (j, k)) if tb else pl.BlockSpec((tk, tn), lambda i, j, k: (k, j))
    o_spec = pl.BlockSpec((tm, tn), lambda i, j, k: (i, j))
    in_specs, args = [a_spec, b_spec], [a, b]
    out_shape, out_specs = jax.ShapeDtypeStruct((m, n), out_dtype), o_spec
    if gate is not None:
        res, mod4, kmod, _, geo = gate
        in_specs += [o_spec, pl.BlockSpec((None, None, 1, tn), lambda i, j, k: (geo.seg(i, tm), kmod, 0, j))]
        args += [res, mod4]
        out_shape = (jax.ShapeDtypeStruct((m, n), F32), jax.ShapeDtypeStruct((m, n), BF16))
        out_specs = (o_spec, o_spec)
    return _call(
        body, hosted, name=name, grid=(m // tm, n // tn, nk), in_specs=in_specs, out_specs=out_specs,
        out_shape=out_shape, scratch_shapes=[pltpu.VMEM((tm, tn), F32)],
        compiler_params=_params("parallel", "parallel", "arbitrary"),
    )(*args)


def _tn_wide(lhs, rhs, name, into=None, s0=0, hosted=None):
    t, m = lhs.shape
    n = rhs.shape[1]
    tm = _pick(m, (1408, 1024, 512, 256, 128))
    while tm * n * 4 > 6.5 * 1024 * 1024 and tm % 256 == 0:
        tm //= 2
    tk = _pick(t, (768, 512, 256, 128))

    def body(l_ref, r_ref, *rest):
        o_ref = rest[-1]
        kk = pl.program_id(1)
        part = lax.dot_general(l_ref[...], r_ref[...], _TN, preferred_element_type=F32)

        @pl.when(kk == 0)
        def _():
            o_ref[...] = part

        @pl.when(kk > 0)
        def _():
            o_ref[...] += part

    in_specs = [pl.BlockSpec((tk, tm), lambda i, k: (k, i)), pl.BlockSpec((tk, n), lambda i, k: (k, 0))]
    if into is None:
        return _call(
            body, hosted, name=name, grid=(m // tm, t // tk), in_specs=in_specs,
            out_specs=pl.BlockSpec((tm, n), lambda i, k: (i, 0)), out_shape=jax.ShapeDtypeStruct((m, n), F32),
            compiler_params=_params("parallel", "arbitrary"),
        )(lhs, rhs)
    return _call(
        body, hosted, name=name, grid=(m // tm, t // tk), in_specs=in_specs + [pl.BlockSpec(memory_space=pl.ANY)],
        out_specs=pl.BlockSpec((None, tm, n), lambda i, k: (s0, i, 0)),
        out_shape=jax.ShapeDtypeStruct(into.shape, into.dtype), input_output_aliases={2: 0},
        compiler_params=_params("parallel", "arbitrary"),
    )(lhs, rhs, into)


def _mod_spec(geo, tile, kmod, d):
    return pl.BlockSpec((None, None, 1, d), lambda i: (geo.seg(i, tile), kmod, 0, 0))


def _pre_fwd(h, g, mod4, k_shift, geo, name):
    t, d = h.shape
    tile = geo.tile

    def body(h_ref, g_ref, sh_ref, sc_ref, o_ref):
        hv = h_ref[...]
        r = lax.rsqrt(jnp.mean(hv * hv, axis=-1, keepdims=True) + EPS)
        y = hv * r * g_ref[...]
        o_ref[...] = (y * (1.0 + sc_ref[...]) + sh_ref[...]).astype(o_ref.dtype)

    row = pl.BlockSpec((tile, d), lambda i: (i, 0))
    return pl.pallas_call(
        body, name=name, grid=(t // tile,),
        in_specs=[row, pl.BlockSpec((1, d), lambda i: (0, 0)), _mod_spec(geo, tile, k_shift, d),
                  _mod_spec(geo, tile, k_shift + 1, d)],
        out_specs=row, out_shape=jax.ShapeDtypeStruct((t, d), BF16), compiler_params=_params("parallel"),
    )(h, g, mod4, mod4)


def _pre_bwd(h, g, mod4, k_shift, dnx, dres, geo, name):
    t, d = h.shape
    tile = geo.tile

    def body(h_ref, g_ref, sc_ref, dnx_ref, dres_ref, dh_ref, dg_ref, dsh_ref, dsc_ref):
        i = pl.program_id(0)
        hv, gv, dout = h_ref[...], g_ref[...], dnx_ref[...].astype(F32)
        r = lax.rsqrt(jnp.mean(hv * hv, axis=-1, keepdims=True) + EPS)
        xhat = hv * r
        dy = dout * (1.0 + sc_ref[...])
        u = dy * gv
        dh_ref[...] = r * (u - xhat * jnp.mean(u * xhat, axis=-1, keepdims=True)) + dres_ref[...]

        @pl.when(i == 0)
        def _():
            dg_ref[...] = jnp.zeros_like(dg_ref)

        @pl.when(geo.seg_start(i, tile))
        def _():
            dsh_ref[...] = jnp.zeros_like(dsh_ref)
            dsc_ref[...] = jnp.zeros_like(dsc_ref)

        dg_ref[...] += jnp.sum(dy * xhat, axis=0, keepdims=True)
        dsh_ref[...] += jnp.sum(dout, axis=0, keepdims=True)
        dsc_ref[...] += jnp.sum(dout * (xhat * gv), axis=0, keepdims=True)

    row = pl.BlockSpec((tile, d), lambda i: (i, 0))
    vec = pl.BlockSpec((1, d), lambda i: (0, 0))
    segv = pl.BlockSpec((None, 1, d), lambda i: (geo.seg(i, tile), 0, 0))
    return pl.pallas_call(
        body, name=name, grid=(t // tile,),
        in_specs=[row, vec, _mod_spec(geo, tile, k_shift + 1, d), row, row],
        out_specs=(row, vec, segv, segv),
        out_shape=(jax.ShapeDtypeStruct((t, d), F32), jax.ShapeDtypeStruct((1, d), F32),
                   jax.ShapeDtypeStruct((N_SEG, 1, d), F32), jax.ShapeDtypeStruct((N_SEG, 1, d), F32)),
        compiler_params=_params("arbitrary"),
    )(h, g, mod4, dnx, dres)


def _gate_bwd(dh, y, mod4, k_gate, coef, geo, name):
    t, d = dh.shape
    tile = geo.tile

    def body(dh_ref, y_ref, gt_ref, dy_ref, dgt_ref):
        i = pl.program_id(0)
        dhv = dh_ref[...]
        dy_ref[...] = ((coef * gt_ref[...]) * dhv).astype(dy_ref.dtype)

        @pl.when(geo.seg_start(i, tile))
        def _():
            dgt_ref[...] = jnp.zeros_like(dgt_ref)

        dgt_ref[...] += coef * jnp.sum(dhv * y_ref[...].astype(F32), axis=0, keepdims=True)

    row = pl.BlockSpec((tile, d), lambda i: (i, 0))
    segv = pl.BlockSpec((None, 1, d), lambda i: (geo.seg(i, tile), 0, 0))
    return pl.pallas_call(
        body, name=name, grid=(t // tile,), in_specs=[row, row, _mod_spec(geo, tile, k_gate, d)],
        out_specs=(row, segv),
        out_shape=(jax.ShapeDtypeStruct((t, d), BF16), jax.ShapeDtypeStruct((N_SEG, 1, d), F32)),
        compiler_params=_params("arbitrary"),
    )(dh, y, mod4)


def _ff_tile(f):
    return _pick(f, (256, 128))


def _ffn_up(nx, w1t, w3t, s0, geo, name, hosted=None):
    t, d = nx.shape
    f = w1t.shape[1]
    tm, tn = geo.big_tile, _ff_tile(f)

    def body(x_ref, w1_ref, w3_ref, a_ref, b_ref, act_ref):
        xv = x_ref[...]
        a = lax.dot_general(xv, w1_ref[...], _NT, preferred_element_type=F32)
        bv = lax.dot_general(xv, w3_ref[...], _NT, preferred_element_type=F32)
        a_ref[...] = a.astype(a_ref.dtype)
        b_ref[...] = bv.astype(b_ref.dtype)
        act_ref[...] = (a * jax.nn.sigmoid(a) * bv).astype(act_ref.dtype)

    w_spec = pl.BlockSpec((None, tn, d), lambda i, j: (s0, j, 0))
    o_spec = pl.BlockSpec((tm, tn), lambda i, j: (i, j))
    sds = jax.ShapeDtypeStruct((t, f), BF16)
    return _call(
        body, hosted, name=name, grid=(t // tm, f // tn),
        in_specs=[pl.BlockSpec((tm, d), lambda i, j: (i, 0)), w_spec, w_spec],
        out_specs=(o_spec,) * 3, out_shape=(sds,) * 3, compiler_params=_params("parallel", "parallel"),
    )(nx, w1t, w3t)


def _ffn_down(act, w2, s0, res, mod4, k_gate, geo, name, hosted=None):
    t, f = act.shape
    d = w2.shape[2]
    tm, tn = geo.mm_tile, _pick(d, (1024, 512, 256, 128))

    def body(a_ref, w_ref, res_ref, gate_ref, o_ref, y_ref):
        acc = lax.dot_general(a_ref[...], w_ref[...], _NN, preferred_element_type=F32)
        y_ref[...] = acc.astype(y_ref.dtype)
        o_ref[...] = res_ref[...] + (0.5 * gate_ref[...]) * acc

    o_spec = pl.BlockSpec((tm, tn), lambda i, j: (i, j))
    return _call(
        body, hosted, name=name, grid=(t // tm, d // tn),
        in_specs=[pl.BlockSpec((tm, f), lambda i, j: (i, 0)), pl.BlockSpec((None, f, tn), lambda i, j: (s0, 0, j)),
                  o_spec, pl.BlockSpec((None, None, 1, tn), lambda i, j: (geo.seg(i, tm), k_gate, 0, j))],
        out_specs=(o_spec, o_spec),
        out_shape=(jax.ShapeDtypeStruct((t, d), F32), jax.ShapeDtypeStruct((t, d), BF16)),
        compiler_params=_params("parallel", "parallel"),
    )(act, w2, res, mod4)


def _ffn_dact(dy, w2, a, b, s0, geo, name, hosted=None):
    t, d = dy.shape
    f = w2.shape[1]
    tm, tn = geo.big_tile, _ff_tile(f)

    def body(dy_ref, w_ref, a_ref, b_ref, da_ref, db_ref):
        dact = lax.dot_general(dy_ref[...], w_ref[...], _NT, preferred_element_type=F32)
        av, bv = a_ref[...].astype(F32), b_ref[...].astype(F32)
        sg = jax.nn.sigmoid(av)
        da_ref[...] = (dact * bv * (sg * (1.0 + av * (1.0 - sg)))).astype(da_ref.dtype)
        db_ref[...] = (dact * (av * sg)).astype(db_ref.dtype)

    o_spec = pl.BlockSpec((tm, tn), lambda i, j: (i, j))
    sds = jax.ShapeDtypeStruct((t, f), BF16)
    return _call(
        body, hosted, name=name, grid=(t // tm, f // tn),
        in_specs=[pl.BlockSpec((tm, d), lambda i, j: (i, 0)), pl.BlockSpec((None, tn, d), lambda i, j: (s0, j, 0)),
                  o_spec, o_spec],
        out_specs=(o_spec, o_spec), out_shape=(sds, sds), compiler_params=_params("parallel", "parallel"),
    )(dy, w2, a, b)


def _ffn_dnx(da, db, w1t, w3t, s0, geo, name, hosted=None):
    t, f = da.shape
    d = w1t.shape[2]
    tm, tn = geo.mm_tile, _pick(d, (512, 256, 128))

    def body(da_ref, db_ref, w1_ref, w3_ref, o_ref):
        o_ref[...] = (lax.dot_general(da_ref[...], w1_ref[...], _NN, preferred_element_type=F32)
                      + lax.dot_general(db_ref[...], w3_ref[...], _NN, preferred_element_type=F32))

    x_spec = pl.BlockSpec((tm, f), lambda j, i: (i, 0))
    w_spec = pl.BlockSpec((None, f, tn), lambda j, i: (s0, 0, j))
    return _call(
        body, hosted, name=name, grid=(d // tn, t // tm), in_specs=[x_spec, x_spec, w_spec, w_spec],
        out_specs=pl.BlockSpec((tm, tn), lambda j, i: (i, j)), out_shape=jax.ShapeDtypeStruct((t, d), F32),
        compiler_params=_params("parallel", "parallel"),
    )(da, db, w1t, w3t)


def _with_host(fn, hosts, got, slot, *args, **kw):
    hosted = hosts.get(slot)
    if hosted is None:
        return fn(*args, **kw)
    out, got[slot] = fn(*args, hosted=hosted, **kw)
    return out


def _ffn_fwd(h, g, mod4, k0, w, s0, geo, tag, sub, hosts, got):
    nx = _pre_fwd(h, g, mod4, k0, geo, f"{tag}_pre")
    a, b, act = _with_host(_ffn_up, hosts, got, f"{sub}_up", nx, w["ffn_w1"], w["ffn_w3"], s0, geo, f"{tag}_up")
    h_out, y = _with_host(_ffn_down, hosts, got, f"{sub}_down", act, w["ffn_w2"], s0, h, mod4, k0 + 2, geo, f"{tag}_down")
    return h_out, (h, nx, a, b, act, y)


def _ffn_bwd(dh_out, saved, g, mod4, k0, w, s0, gbuf, geo, tag, sub, hosts, got):
    h, nx, a, b, act, y = saved
    dy, dgate = _gate_bwd(dh_out, y, mod4, k0 + 2, 0.5, geo, f"{tag}_dgate")
    da, db = _with_host(_ffn_dact, hosts, got, f"{sub}_dact", dy, w["ffn_w2"], a, b, s0, geo, f"{tag}_dact")
    gbuf["ffn_w2"] = _with_host(_tn_wide, hosts, got, f"{sub}_dw2", act, dy, f"{tag}_dw2", into=gbuf["ffn_w2"], s0=s0)
    dnx = _with_host(_ffn_dnx, hosts, got, f"{sub}_dnx", da, db, w["ffn_w1"], w["ffn_w3"], s0, geo, f"{tag}_dnx")
    gbuf["ffn_w1"] = _tn_wide(da, nx, f"{tag}_dw1", into=gbuf["ffn_w1"], s0=s0)
    gbuf["ffn_w3"] = _tn_wide(db, nx, f"{tag}_dw3", into=gbuf["ffn_w3"], s0=s0)
    dh, dg, dshift, dscale = _pre_bwd(h, g, mod4, k0, dnx, dh_out, geo, f"{tag}_dpre")
    return dh, dg, (dshift, dscale, dgate)


def _interleave(w, n_parts, tile):
    lead, cols = w.shape[:-1], w.shape[-1] // n_parts
    return w.reshape(*lead, n_parts, cols // tile, tile).swapaxes(-3, -2).reshape(*lead, n_parts * cols)


def _deinterleave(w, n_parts, tile):
    lead, cols = w.shape[:-1], w.shape[-1] // n_parts
    return w.reshape(*lead, cols // tile, n_parts, tile).swapaxes(-3, -2).reshape(*lead, n_parts * cols)


HALO = 16


def _conv_tile(c):
    return _pick(c, (256, 128))


def _conv_specs(geo, tc, t):
    tile = geo.tile
    per = tile // HALO
    last = t // HALO - 1
    cur = pl.BlockSpec((tile, 3 * tc), lambda j, i: (i, j))
    prev = pl.BlockSpec((HALO, 3 * tc), lambda j, i: (jnp.maximum(i * per - 1, 0), j))
    nxt = pl.BlockSpec((HALO, 3 * tc), lambda j, i: (jnp.minimum((i + 1) * per, last), j))
    return cur, prev, nxt


def _conv_edges(geo, i):
    tile = geo.tile
    row = i * tile
    lat = row < 2 * geo.n_lat
    first = jnp.where(lat, row % geo.n_lat == 0, (row - 2 * geo.n_lat) % geo.n_ctx == 0)
    end = row + tile
    last = jnp.where(lat, end % geo.n_lat == 0, (end - 2 * geo.n_lat) % geo.n_ctx == 0)
    return first, last


def _shift_rows(v, before, after):
    n = v.shape[0]
    rows = lax.broadcasted_iota(jnp.int32, v.shape, 0)
    down = jnp.where(rows == 0, before, pltpu.roll(v, 1, 0))
    up = jnp.where(rows == n - 1, after, pltpu.roll(v, n - 1, 0))
    return down, up


def _conv_fwd(proj, conv_w, geo, name, hosted=None):
    t, c3 = proj.shape
    c = c3 // 3
    tc, tile = _conv_tile(c), geo.tile

    def body(cur_ref, prev_ref, next_ref, w_ref, o_ref):
        first, last = _conv_edges(geo, pl.program_id(1))
        bv = cur_ref[:, :tc].astype(F32)
        p = cur_ref[:, tc:2 * tc].astype(F32) * cur_ref[:, 2 * tc:].astype(F32)
        p_before = prev_ref[HALO - 1:HALO, tc:2 * tc].astype(F32) * prev_ref[HALO - 1:HALO, 2 * tc:].astype(F32)
        p_after = next_ref[0:1, tc:2 * tc].astype(F32) * next_ref[0:1, 2 * tc:].astype(F32)
        p_before = jnp.where(first, 0.0, p_before)
        p_after = jnp.where(last, 0.0, p_after)
        pm1, pp1 = _shift_rows(p, p_before, p_after)
        w = w_ref[...]
        q = w[0:1] * pm1 + w[1:2] * p + w[2:3] * pp1
        o_ref[...] = (bv * q).astype(o_ref.dtype)

    cur, prev, nxt = _conv_specs(geo, tc, t)
    return _call(
        body, hosted, name=name, grid=(c // tc, t // tile),
        in_specs=[cur, prev, nxt, pl.BlockSpec((3, tc), lambda j, i: (0, j))],
        out_specs=pl.BlockSpec((tile, tc), lambda j, i: (i, j)), out_shape=jax.ShapeDtypeStruct((t, c), BF16),
        compiler_params=_params("parallel", "parallel"),
    )(proj, proj, proj, conv_w)


def _conv_bwd(proj, dyc, conv_w, geo, name):
    t, c3 = proj.shape
    c = c3 // 3
    tc, tile = _conv_tile(c), geo.tile

    def body(cur_ref, prev_ref, next_ref, d_ref, dprev_ref, dnext_ref, w_ref, o_ref, dw_ref):
        i = pl.program_id(1)
        first, last = _conv_edges(geo, i)
        bv = cur_ref[:, :tc].astype(F32)
        cv = cur_ref[:, tc:2 * tc].astype(F32)
        uv = cur_ref[:, 2 * tc:].astype(F32)
        p = cv * uv
        p_before = prev_ref[HALO - 1:HALO, tc:2 * tc].astype(F32) * prev_ref[HALO - 1:HALO, 2 * tc:].astype(F32)
        p_after = next_ref[0:1, tc:2 * tc].astype(F32) * next_ref[0:1, 2 * tc:].astype(F32)
        p_before = jnp.where(first, 0.0, p_before)
        p_after = jnp.where(last, 0.0, p_after)
        pm1, pp1 = _shift_rows(p, p_before, p_after)
        w = w_ref[...]
        q = w[0:1] * pm1 + w[1:2] * p + w[2:3] * pp1
        dy = d_ref[...].astype(F32)
        dq = dy * bv
        dq_before = dprev_ref[HALO - 1:HALO, :].astype(F32) * prev_ref[HALO - 1:HALO, :tc].astype(F32)
        dq_after = dnext_ref[0:1, :].astype(F32) * next_ref[0:1, :tc].astype(F32)
        dq_before = jnp.where(first, 0.0, dq_before)
        dq_after = jnp.where(last, 0.0, dq_after)
        dqm1, dqp1 = _shift_rows(dq, dq_before, dq_after)
        dp = w[0:1] * dqp1 + w[1:2] * dq + w[2:3] * dqm1
        o_ref[:, :tc] = (dy * q).astype(o_ref.dtype)
        o_ref[:, tc:2 * tc] = (dp * uv).astype(o_ref.dtype)
        o_ref[:, 2 * tc:] = (dp * cv).astype(o_ref.dtype)

        @pl.when(i == 0)
        def _():
            dw_ref[...] = jnp.zeros_like(dw_ref)

        dw_ref[0:1, :] += jnp.sum(dq * pm1, axis=0, keepdims=True)
        dw_ref[1:2, :] += jnp.sum(dq * p, axis=0, keepdims=True)
        dw_ref[2:3, :] += jnp.sum(dq * pp1, axis=0, keepdims=True)

    cur, prev, nxt = _conv_specs(geo, tc, t)
    per, lastb = tile // HALO, t // HALO - 1
    dcur = pl.BlockSpec((tile, tc), lambda j, i: (i, j))
    dprev = pl.BlockSpec((HALO, tc), lambda j, i: (jnp.maximum(i * per - 1, 0), j))
    dnext = pl.BlockSpec((HALO, tc), lambda j, i: (jnp.minimum((i + 1) * per, lastb), j))
    wspec = pl.BlockSpec((3, tc), lambda j, i: (0, j))
    return pl.pallas_call(
        body, name=name, grid=(c // tc, t // tile), in_specs=[cur, prev, nxt, dcur, dprev, dnext, wspec],
        out_specs=(cur, wspec), out_shape=(jax.ShapeDtypeStruct((t, c3), BF16), jax.ShapeDtypeStruct((3, c), F32)),
        compiler_params=_params("parallel", "arbitrary"),
    )(proj, proj, proj, dyc, dyc, dyc, conv_w)


def _sconv_fwd(h, g, mod4, w_in, conv_w, w_out, geo, tag, hosts, got):
    nx = _pre_fwd(h, g, mod4, 3, geo, f"{tag}_pre")
    proj = _with_host(_mm, hosts, got, "mix_a", nx, w_in, out_dtype=BF16, name=f"{tag}_in")
    yc = _with_host(_conv_fwd, hosts, got, "mix_b", proj, conv_w, geo, f"{tag}_conv")
    h_out, y = _mm(yc, w_out, name=f"{tag}_out", gate=(h, mod4, 5, 1.0, geo))
    return h_out, (h, nx, proj, yc, y)


def _sconv_bwd(dh_out, saved, g, mod4, w_in, conv_w, w_out, geo, tag, hosts, got):
    h, nx, proj, yc, y = saved
    dy, dgate = _gate_bwd(dh_out, y, mod4, 5, 1.0, geo, f"{tag}_dgate")
    dyc = _mm(dy, w_out, tb=True, out_dtype=BF16, name=f"{tag}_dyc")
    dw_out = _tn_wide(yc, dy, f"{tag}_dwout")
    dproj, dconv = _conv_bwd(proj, dyc, conv_w, geo, f"{tag}_dconv")
    dnx = _with_host(_mm, hosts, got, "mix_b", dproj, w_in, tb=True, name=f"{tag}_dnx")
    dw_in = _with_host(_tn_wide, hosts, got, "mix_a", nx, dproj, f"{tag}_dwin")
    dh, dg, dshift, dscale = _pre_bwd(h, g, mod4, 3, dnx, dh_out, geo, f"{tag}_dpre")
    return dh, dg, (dshift, dscale, dgate), dw_in, dconv, dw_out


def _rope_swap(v):
    nf = QK_ROPE // 4
    return v.reshape(v.shape[:-1] + (2, 2, nf)).swapaxes(-3, -2).reshape(v.shape)


def _rope_tables(geo):
    n = geo.n_lat
    nf = QK_ROPE // 4
    pos = np.arange(n)
    inv = ROPE_BASE ** (-np.arange(nf, dtype=np.float32) / nf)
    ang = np.concatenate([(pos // GRID_W)[:, None] * inv, (pos % GRID_W)[:, None] * inv], axis=1).astype(np.float32)
    cos, sin = np.cos(ang), np.sin(ang)
    zeros = np.zeros((n, LANE - QK_ROPE), np.float32)
    c_lat = np.concatenate([cos, cos, zeros], axis=1)
    s_lat = np.concatenate([-sin, sin, zeros], axis=1)
    c_ctx = np.concatenate([np.ones((2 * geo.n_ctx, QK_ROPE), np.float32), np.zeros((2 * geo.n_ctx, LANE - QK_ROPE), np.float32)], 1)
    s_ctx = np.zeros((2 * geo.n_ctx, LANE), np.float32)
    return (jnp.asarray(np.concatenate([c_lat, c_lat, c_ctx], 0)), jnp.asarray(np.concatenate([s_lat, s_lat, s_ctx], 0)))


def _swap_halves(v):
    lanes = lax.broadcasted_iota(jnp.int32, v.shape, 1)
    return jnp.where(lanes < QK_ROPE // 2, pltpu.roll(v, LANE - QK_ROPE // 2, 1), pltpu.roll(v, QK_ROPE // 2, 1))


def _latent_norm_fwd(down, g_qa, g_kva, geo, name):
    t, wd = down.shape
    ql, kl = g_qa.shape[1], g_kva.shape[1]
    tile = geo.tile

    def body(d_ref, gq_ref, gk_ref, cq_ref, ckv_ref):
        for lo, n, g_ref, o_ref in ((0, ql, gq_ref, cq_ref), (ql, kl, gk_ref, ckv_ref)):
            x = d_ref[:, lo:lo + n]
            r = lax.rsqrt(jnp.mean(x * x, axis=-1, keepdims=True) + EPS)
            o_ref[...] = (x * r * g_ref[...]).astype(o_ref.dtype)

    return pl.pallas_call(
        body, name=name, grid=(t // tile,),
        in_specs=[pl.BlockSpec((tile, wd), lambda i: (i, 0)), pl.BlockSpec((1, ql), lambda i: (0, 0)),
                  pl.BlockSpec((1, kl), lambda i: (0, 0))],
        out_specs=(pl.BlockSpec((tile, ql), lambda i: (i, 0)), pl.BlockSpec((tile, kl), lambda i: (i, 0))),
        out_shape=(jax.ShapeDtypeStruct((t, ql), BF16), jax.ShapeDtypeStruct((t, kl), BF16)),
        compiler_params=_params("parallel"),
    )(down, g_qa, g_kva)


def _latent_norm_bwd(down, g_qa, g_kva, dcqn, dckvn, dkr, geo, name):
    t, wd = down.shape
    ql, kl = g_qa.shape[1], g_kva.shape[1]
    tile = geo.tile

    def body(d_ref, gq_ref, gk_ref, dq_ref, dk_ref, dkr_ref, o_ref, dgq_ref, dgk_ref):
        i = pl.program_id(0)

        @pl.when(i == 0)
        def _():
            dgq_ref[...] = jnp.zeros_like(dgq_ref)
            dgk_ref[...] = jnp.zeros_like(dgk_ref)

        for lo, n, g_ref, dy_ref, dg_ref in ((0, ql, gq_ref, dq_ref, dgq_ref), (ql, kl, gk_ref, dk_ref, dgk_ref)):
            x = d_ref[:, lo:lo + n]
            dy = dy_ref[...].astype(F32)
            r = lax.rsqrt(jnp.mean(x * x, axis=-1, keepdims=True) + EPS)
            xhat = x * r
            u = dy * g_ref[...]
            o_ref[:, lo:lo + n] = (r * (u - xhat * jnp.mean(u * xhat, axis=-1, keepdims=True))).astype(o_ref.dtype)
            dg_ref[...] += jnp.sum(dy * xhat, axis=0, keepdims=True)
        o_ref[:, ql + kl:] = dkr_ref[...].astype(o_ref.dtype)

    def row(n):
        return pl.BlockSpec((tile, n), lambda i: (i, 0))

    def vec(n):
        return pl.BlockSpec((1, n), lambda i: (0, 0))

    return pl.pallas_call(
        body, name=name, grid=(t // tile,),
        in_specs=[row(wd), vec(ql), vec(kl), row(ql), row(kl), row(wd - ql - kl)],
        out_specs=(row(wd), vec(ql), vec(kl)),
        out_shape=(jax.ShapeDtypeStruct((t, wd), BF16), jax.ShapeDtypeStruct((1, ql), F32),
                   jax.ShapeDtypeStruct((1, kl), F32)),
        compiler_params=_params("arbitrary"),
    )(down, g_qa, g_kva, dcqn, dckvn, dkr)


def _qk_specs(geo, xr_col, shared_rope):
    tile = geo.mm_tile
    xn_spec = pl.BlockSpec((tile, HEADS * LANE), lambda i: (i, 0))
    if shared_rope:
        xr_spec = pl.BlockSpec((tile, LANE), lambda i: (i, xr_col))
    else:
        xr_spec = pl.BlockSpec((tile, HEADS * LANE), lambda i: (i, xr_col // HEADS))
    vec = pl.BlockSpec((1, LANE), lambda i: (0, 0))
    tab = pl.BlockSpec((tile, LANE), lambda i: (i, 0))
    return tile, xn_spec, xr_spec, vec, tab


def _qk_norm(xn, xr):
    ss = jnp.sum(xn * xn, axis=-1, keepdims=True) + jnp.sum(xr * xr, axis=-1, keepdims=True)
    return lax.rsqrt(ss * (1.0 / QK_HEAD) + EPS)


def _head_lanes(ref, hh, shared=False):
    return ref[...] if shared else ref[:, hh * LANE:(hh + 1) * LANE]


def _qk_fwd(xn_arr, xr_arr, xr_col, shared_rope, gn, gr, cos, sin, geo, name):
    t = xn_arr.shape[0]
    tile, xn_spec, xr_spec, vec, tab = _qk_specs(geo, xr_col, shared_rope)

    def body(xn_ref, xr_ref, gn_ref, gr_ref, c_ref, s_ref, o_ref):
        cv, sv, gnv, grv = c_ref[...], s_ref[...], gn_ref[...], gr_ref[...]
        for hh in range(HEADS):
            xn = _head_lanes(xn_ref, hh).astype(F32)
            xr = _head_lanes(xr_ref, hh, shared_rope).astype(F32)
            r = _qk_norm(xn, xr)
            yr = xr * r * grv
            o_ref[:, hh * HEAD_PAD:hh * HEAD_PAD + LANE] = (xn * r * gnv).astype(o_ref.dtype)
            o_ref[:, hh * HEAD_PAD + LANE:(hh + 1) * HEAD_PAD] = (yr * cv + _swap_halves(yr) * sv).astype(o_ref.dtype)

    return pl.pallas_call(
        body, name=name, grid=(t // tile,), in_specs=[xn_spec, xr_spec, vec, vec, tab, tab],
        out_specs=pl.BlockSpec((tile, HEADS * HEAD_PAD), lambda i: (i, 0)),
        out_shape=jax.ShapeDtypeStruct((t, HEADS * HEAD_PAD), BF16), compiler_params=_params("parallel"),
    )(xn_arr, xr_arr, gn, gr, cos, sin)


def _qk_bwd(xn_arr, xr_arr, xr_col, shared_rope, gn, gr, cos, sin, dout, geo, name):
    t = xn_arr.shape[0]
    tile, xn_spec, xr_spec, vec, tab = _qk_specs(geo, xr_col, shared_rope)

    def body(xn_ref, xr_ref, gn_ref, gr_ref, c_ref, s_ref, d_ref, dxn_ref, dxr_ref, dgn_ref, dgr_ref):
        i = pl.program_id(0)
        cv, sv, gnv, grv = c_ref[...], s_ref[...], gn_ref[...], gr_ref[...]
        dgn = jnp.zeros((1, LANE), F32)
        dgr = jnp.zeros((1, LANE), F32)
        dxr_sum = jnp.zeros((tile, LANE), F32)
        for hh in range(HEADS):
            xn = _head_lanes(xn_ref, hh).astype(F32)
            xr = _head_lanes(xr_ref, hh, shared_rope).astype(F32)
            r = _qk_norm(xn, xr)
            xhn, xhr = xn * r, xr * r
            dyn = d_ref[:, hh * HEAD_PAD:hh * HEAD_PAD + LANE].astype(F32)
            dro = d_ref[:, hh * HEAD_PAD + LANE:(hh + 1) * HEAD_PAD].astype(F32)
            dyr = dro * cv + _swap_halves(dro * sv)
            un, ur = dyn * gnv, dyr * grv
            mean = (jnp.sum(un * xhn, axis=-1, keepdims=True) + jnp.sum(ur * xhr, axis=-1, keepdims=True)) * (1.0 / QK_HEAD)
            dxn_ref[:, hh * LANE:(hh + 1) * LANE] = (r * (un - xhn * mean)).astype(dxn_ref.dtype)
            dxr = r * (ur - xhr * mean)
            if shared_rope:
                dxr_sum = dxr_sum + dxr
            else:
                dxr_ref[:, hh * LANE:(hh + 1) * LANE] = dxr.astype(dxr_ref.dtype)
            dgn = dgn + jnp.sum(dyn * xhn, axis=0, keepdims=True)
            dgr = dgr + jnp.sum(dyr * xhr, axis=0, keepdims=True)
        if shared_rope:
            dxr_ref[...] = dxr_sum

        @pl.when(i == 0)
        def _():
            dgn_ref[...] = jnp.zeros_like(dgn_ref)
            dgr_ref[...] = jnp.zeros_like(dgr_ref)

        dgn_ref[...] += dgn
        dgr_ref[...] += dgr

    heads = pl.BlockSpec((tile, HEADS * LANE), lambda i: (i, 0))
    if shared_rope:
        dxr_spec, dxr_shape = pl.BlockSpec((tile, LANE), lambda i: (i, 0)), jax.ShapeDtypeStruct((t, LANE), F32)
    else:
        dxr_spec, dxr_shape = heads, jax.ShapeDtypeStruct((t, HEADS * LANE), BF16)
    return pl.pallas_call(
        body, name=name, grid=(t // tile,),
        in_specs=[xn_spec, xr_spec, vec, vec, tab, tab, pl.BlockSpec((tile, HEADS * HEAD_PAD), lambda i: (i, 0))],
        out_specs=(heads, dxr_spec, vec, vec),
        out_shape=(jax.ShapeDtypeStruct((t, HEADS * LANE), BF16), dxr_shape, jax.ShapeDtypeStruct((1, LANE), F32),
                   jax.ShapeDtypeStruct((1, LANE), F32)),
        compiler_params=_params("arbitrary"),
    )(xn_arr, xr_arr, gn, gr, cos, sin, dout)


def _attn_specs(geo):
    tq, nq = geo.n_ctx, geo.n_lat // geo.n_ctx

    def qrow(b, i):
        return jnp.where(i < nq, b * nq + i, 2 * nq + b)

    q_spec = pl.BlockSpec((tq, HEAD_PAD), lambda b, hh, i: (qrow(b, i), hh))
    kc_spec = pl.BlockSpec((geo.n_ctx, HEAD_PAD), lambda b, hh, i: (2 * nq + b, hh))
    kl_spec = pl.BlockSpec((geo.n_lat, HEAD_PAD), lambda b, hh, i: (b, hh))
    vc_spec = pl.BlockSpec((geo.n_ctx, V_HEAD), lambda b, hh, i: (2 * nq + b, HEADS + hh))
    vl_spec = pl.BlockSpec((geo.n_lat, V_HEAD), lambda b, hh, i: (b, HEADS + hh))
    o_spec = pl.BlockSpec((tq, V_HEAD), lambda b, hh, i: (qrow(b, i), hh))
    return tq, nq, q_spec, kc_spec, kl_spec, vc_spec, vl_spec, o_spec


def _attn_fwd(q, k, kv, with_ctx_q, geo, name, hosted=None):
    t = q.shape[0]
    tq, nq, q_spec, kc_spec, kl_spec, vc_spec, vl_spec, o_spec = _attn_specs(geo)

    def body(q_ref, kc_ref, kl_ref, vc_ref, vl_ref, o_ref):
        i = pl.program_id(2)
        qv = q_ref[...]
        s_c = _dot(qv, kc_ref[...], _NT) * QK_SCALE

        @pl.when(i < nq)
        def _():
            s_l = _dot(qv, kl_ref[...], _NT) * QK_SCALE
            m = jnp.maximum(jnp.max(s_c, axis=-1, keepdims=True), jnp.max(s_l, axis=-1, keepdims=True))
            p_c, p_l = jnp.exp(s_c - m), jnp.exp(s_l - m)
            den = jnp.sum(p_c, axis=-1, keepdims=True) + jnp.sum(p_l, axis=-1, keepdims=True)
            o = _dot(p_c, vc_ref[...], _NN) + _dot(p_l, vl_ref[...], _NN)
            o_ref[...] = (o / den).astype(o_ref.dtype)

        @pl.when(i == nq)
        def _():
            if with_ctx_q:
                m = jnp.max(s_c, axis=-1, keepdims=True)
                p_c = jnp.exp(s_c - m)
                o = _dot(p_c, vc_ref[...], _NN) / jnp.sum(p_c, axis=-1, keepdims=True)
                o_ref[...] = o.astype(o_ref.dtype)
            else:
                o_ref[...] = jnp.zeros_like(o_ref)

    return _call(
        body, hosted, name=name, grid=(2, HEADS, nq + 1), in_specs=[q_spec, kc_spec, kl_spec, vc_spec, vl_spec],
        out_specs=o_spec, out_shape=jax.ShapeDtypeStruct((t, HEADS * V_HEAD), BF16),
        compiler_params=_params("parallel", "parallel", "arbitrary"),
    )(q, k, k, kv, kv)


def _attn_bwd(q, k, kv, do, with_ctx_q, geo, name, hosted=None):
    t = q.shape[0]
    tq, nq, q_spec, kc_spec, kl_spec, vc_spec, vl_spec, o_spec = _attn_specs(geo)

    def body(q_ref, kc_ref, kl_ref, vc_ref, vl_ref, do_ref, dq_ref, dkl_ref, dkc_ref, dvl_ref, dvc_ref,
             akl_ref, akc_ref, avl_ref, avc_ref):
        i = pl.program_id(2)

        @pl.when(i == 0)
        def _():
            for ref in (akl_ref, akc_ref, avl_ref, avc_ref):
                ref[...] = jnp.zeros_like(ref)

        qv, dov = q_ref[...], do_ref[...]
        s_c = _dot(qv, kc_ref[...], _NT) * QK_SCALE
        dp_c = _dot(dov, vc_ref[...], _NT)

        def ctx_part(p_c, delta):
            ds_c = (p_c * (dp_c - delta) * QK_SCALE).astype(BF16)
            akc_ref[...] += _dot(ds_c, qv, _TN)
            avc_ref[...] += _dot(p_c, dov, _TN)
            return _dot(ds_c, kc_ref[...], _NN)

        @pl.when(i < nq)
        def _():
            s_l = _dot(qv, kl_ref[...], _NT) * QK_SCALE
            m = jnp.maximum(jnp.max(s_c, axis=-1, keepdims=True), jnp.max(s_l, axis=-1, keepdims=True))
            p_c, p_l = jnp.exp(s_c - m), jnp.exp(s_l - m)
            inv = 1.0 / (jnp.sum(p_c, axis=-1, keepdims=True) + jnp.sum(p_l, axis=-1, keepdims=True))
            p_c, p_l = p_c * inv, p_l * inv
            dp_l = _dot(dov, vl_ref[...], _NT)
            delta = jnp.sum(p_c * dp_c, axis=-1, keepdims=True) + jnp.sum(p_l * dp_l, axis=-1, keepdims=True)
            ds_l = (p_l * (dp_l - delta) * QK_SCALE).astype(BF16)
            akl_ref[...] += _dot(ds_l, qv, _TN)
            avl_ref[...] += _dot(p_l, dov, _TN)
            dq_ref[...] = (ctx_part(p_c, delta) + _dot(ds_l, kl_ref[...], _NN)).astype(dq_ref.dtype)

        @pl.when(i == nq)
        def _():
            if with_ctx_q:
                m = jnp.max(s_c, axis=-1, keepdims=True)
                p_c = jnp.exp(s_c - m)
                p_c = p_c * (1.0 / jnp.sum(p_c, axis=-1, keepdims=True))
                delta = jnp.sum(p_c * dp_c, axis=-1, keepdims=True)
                dq_ref[...] = ctx_part(p_c, delta).astype(dq_ref.dtype)
            else:
                dq_ref[...] = jnp.zeros_like(dq_ref)
            dkl_ref[...] = akl_ref[...].astype(dkl_ref.dtype)
            dkc_ref[...] = akc_ref[...].astype(dkc_ref.dtype)
            dvl_ref[...] = avl_ref[...].astype(dvl_ref.dtype)
            dvc_ref[...] = avc_ref[...].astype(dvc_ref.dtype)

    def acc_spec(rows, width):
        return pl.BlockSpec((rows, width), lambda b, hh, i: (b, hh))

    return _call(
        body, hosted, name=name, grid=(2, HEADS, nq + 1), in_specs=[q_spec, kc_spec, kl_spec, vc_spec, vl_spec, o_spec],
        out_specs=(q_spec, acc_spec(geo.n_lat, HEAD_PAD), acc_spec(geo.n_ctx, HEAD_PAD), acc_spec(geo.n_lat, V_HEAD),
                   acc_spec(geo.n_ctx, V_HEAD)),
        out_shape=(jax.ShapeDtypeStruct((t, HEADS * HEAD_PAD), BF16),
                   jax.ShapeDtypeStruct((2 * geo.n_lat, HEADS * HEAD_PAD), BF16),
                   jax.ShapeDtypeStruct((2 * geo.n_ctx, HEADS * HEAD_PAD), BF16),
                   jax.ShapeDtypeStruct((2 * geo.n_lat, HEADS * V_HEAD), BF16),
                   jax.ShapeDtypeStruct((2 * geo.n_ctx, HEADS * V_HEAD), BF16)),
        scratch_shapes=[pltpu.VMEM((geo.n_lat, HEAD_PAD), F32), pltpu.VMEM((geo.n_ctx, HEAD_PAD), F32),
                        pltpu.VMEM((geo.n_lat, V_HEAD), F32), pltpu.VMEM((geo.n_ctx, V_HEAD), F32)],
        compiler_params=_params("parallel", "parallel", "arbitrary"),
    )(q, k, k, kv, kv, do)


def _mla_fwd(h, g, mod4, w, with_ctx_q, tabs, geo, tag, hosts, got):
    cos, sin = tabs
    ql, kl = w["g_qa"].shape[1], w["g_kva"].shape[1]
    kr_col = (ql + kl) // LANE
    nx = _pre_fwd(h, g, mod4, 3, geo, f"{tag}_pre")
    down = _mm(nx, w["w_a"], name=f"{tag}_down")
    cqn, ckvn = _latent_norm_fwd(down, w["g_qa"], w["g_kva"], geo, f"{tag}_lnorm")
    qraw = _mm(cqn, w["w_uq"], out_dtype=BF16, name=f"{tag}_uq")
    kvraw = _mm(ckvn, w["w_ukv"], out_dtype=BF16, name=f"{tag}_ukv")
    q = _qk_fwd(qraw, qraw, HEADS, False, w["gq_n"], w["gq_r"], cos, sin, geo, f"{tag}_qnorm")
    k = _qk_fwd(kvraw, down, kr_col, True, w["gk_n"], w["gk_r"], cos, sin, geo, f"{tag}_knorm")
    o = _with_host(_attn_fwd, hosts, got, "mix_a", q, k, kvraw, with_ctx_q, geo, f"{tag}_attn")
    h_out, y = _mm(o, w["w_o"], name=f"{tag}_o", gate=(h, mod4, 5, 1.0, geo))
    return h_out, (h, nx, down, cqn, ckvn, qraw, kvraw, q, k, o, y)


def _mla_bwd(dh_out, saved, g, mod4, w, with_ctx_q, tabs, geo, tag, hosts, got):
    cos, sin = tabs
    h, nx, down, cqn, ckvn, qraw, kvraw, q, k, o, y = saved
    ql, kl = w["g_qa"].shape[1], w["g_kva"].shape[1]
    kr_col = (ql + kl) // LANE
    dy, dgate = _gate_bwd(dh_out, y, mod4, 5, 1.0, geo, f"{tag}_dgate")
    do = _mm(dy, w["w_o"], tb=True, out_dtype=BF16, name=f"{tag}_do")
    dw_o = _tn_wide(o, dy, f"{tag}_dwo")
    dq, dk_lat, dk_ctx, dv_lat, dv_ctx = _with_host(_attn_bwd, hosts, got, "mix_a", q, k, kvraw, do, with_ctx_q, geo,
                                                    f"{tag}_dattn")
    dk = jnp.concatenate([dk_lat, dk_ctx], axis=0)
    dqn, dqr, dgq_n, dgq_r = _qk_bwd(qraw, qraw, HEADS, False, w["gq_n"], w["gq_r"], cos, sin, dq, geo, f"{tag}_dqnorm")
    dkn, dkr, dgk_n, dgk_r = _qk_bwd(kvraw, down, kr_col, True, w["gk_n"], w["gk_r"], cos, sin, dk, geo, f"{tag}_dknorm")
    dqraw = jnp.concatenate([dqn, dqr], axis=1)
    dkvraw = jnp.concatenate([dkn, jnp.concatenate([dv_lat, dv_ctx], axis=0)], axis=1)
    dcqn = _mm(dqraw, w["w_uq"], tb=True, out_dtype=BF16, name=f"{tag}_dcqn")
    dw_uq = _tn_wide(cqn, dqraw, f"{tag}_dwuq")
    dckvn = _mm(dkvraw, w["w_ukv"], tb=True, out_dtype=BF16, name=f"{tag}_dckvn")
    dw_ukv = _tn_wide(ckvn, dkvraw, f"{tag}_dwukv")
    ddown, dg_qa, dg_kva = _latent_norm_bwd(down, w["g_qa"], w["g_kva"], dcqn, dckvn, dkr, geo, f"{tag}_dlnorm")
    dnx = _mm(ddown, w["w_a"], tb=True, name=f"{tag}_dnx")
    dw_a = _tn_wide(nx, ddown, f"{tag}_dwa")
    dh, dg, dshift, dscale = _pre_bwd(h, g, mod4, 3, dnx, dh_out, geo, f"{tag}_dpre")
    grads = dict(w_a=dw_a, g_qa=dg_qa, w_uq=dw_uq, g_kva=dg_kva, w_ukv=dw_ukv, gq_n=dgq_n, gq_r=dgq_r, gk_n=dgk_n,
                 gk_r=dgk_r, w_o=dw_o)
    return dh, dg, (dshift, dscale, dgate), grads


def _mla_prepare(w_a, g_qa, w_uq, g_kva, w_ukv, g_q, g_k, w_o):
    ql, kl = g_qa.shape[0], g_kva.shape[0]
    d = w_a.shape[0]
    w_a_pad = jnp.concatenate([w_a[:, :ql + kl], _rope_swap(w_a[:, ql + kl:]), jnp.zeros((d, LANE - QK_ROPE), w_a.dtype)], axis=1)
    uq = w_uq.reshape(ql, HEADS, QK_HEAD)
    uq_r = jnp.pad(_rope_swap(uq[:, :, QK_NOPE:]), ((0, 0), (0, 0), (0, LANE - QK_ROPE)))
    w_uq_pad = jnp.concatenate([uq[:, :, :QK_NOPE].reshape(ql, HEADS * LANE), uq_r.reshape(ql, HEADS * LANE)], axis=1)
    ukv = w_ukv.reshape(kl, HEADS, QK_NOPE + V_HEAD)
    w_ukv_p = jnp.concatenate([ukv[:, :, :QK_NOPE].reshape(kl, HEADS * LANE), ukv[:, :, QK_NOPE:].reshape(kl, HEADS * V_HEAD)], axis=1)

    def gains(gv):
        gv = gv.astype(F32)
        return gv[None, :QK_NOPE], jnp.pad(_rope_swap(gv[QK_NOPE:]), (0, LANE - QK_ROPE))[None]

    gq_n, gq_r = gains(g_q)
    gk_n, gk_r = gains(g_k)
    return dict(w_a=w_a_pad, g_qa=g_qa.astype(F32)[None], w_uq=w_uq_pad, g_kva=g_kva.astype(F32)[None], w_ukv=w_ukv_p,
                gq_n=gq_n, gq_r=gq_r, gk_n=gk_n, gk_r=gk_r, w_o=w_o)


def _mla_unprepare(gr):
    ql, kl = gr["g_qa"].shape[1], gr["g_kva"].shape[1]
    dw_a = jnp.concatenate([gr["w_a"][:, :ql + kl], _rope_swap(gr["w_a"][:, ql + kl:ql + kl + QK_ROPE])], axis=1)
    uqn = gr["w_uq"][:, :HEADS * LANE].reshape(ql, HEADS, LANE)
    uqr = _rope_swap(gr["w_uq"][:, HEADS * LANE:].reshape(ql, HEADS, LANE)[:, :, :QK_ROPE])
    dw_uq = jnp.concatenate([uqn, uqr], axis=2).reshape(ql, HEADS * QK_HEAD)
    ukn = gr["w_ukv"][:, :HEADS * LANE].reshape(kl, HEADS, LANE)
    ukv = gr["w_ukv"][:, HEADS * LANE:].reshape(kl, HEADS, V_HEAD)
    dw_ukv = jnp.concatenate([ukn, ukv], axis=2).reshape(kl, HEADS * (QK_NOPE + V_HEAD))

    def gains(gn, grr):
        return jnp.concatenate([gn[0], _rope_swap(grr[0, :QK_ROPE])])

    return dict(mla_w_a=dw_a, mla_g_qa=gr["g_qa"][0], mla_w_uq=dw_uq, mla_g_kva=gr["g_kva"][0], mla_w_ukv=dw_ukv,
                mla_g_q=gains(gr["gq_n"], gr["gq_r"]), mla_g_k=gains(gr["gk_n"], gr["gk_r"]), mla_w_o=gr["w_o"])


def _loss_head(h, target, geo, name):
    t, d = h.shape
    tile = geo.tile
    n_lat_tiles = 2 * geo.n_lat // tile

    def body(h_ref, t_ref, dh_ref, loss_ref):
        i = pl.program_id(0)

        @pl.when(i == 0)
        def _():
            loss_ref[...] = jnp.zeros_like(loss_ref)

        @pl.when(i < n_lat_tiles)
        def _():
            e = h_ref[...] - t_ref[...]
            dh_ref[...] = e * (1.0 / d)
            part = jnp.sum(e * e, axis=0, keepdims=True) * (0.5 / d)
            loss_ref[...] += sum(part[:, j * LANE:(j + 1) * LANE] for j in range(d // LANE))

        @pl.when(i >= n_lat_tiles)
        def _():
            dh_ref[...] = jnp.zeros_like(dh_ref)

    row = pl.BlockSpec((tile, d), lambda i: (i, 0))
    tgt = pl.BlockSpec((tile, d), lambda i: (jnp.minimum(i, n_lat_tiles - 1), 0))
    dh, loss = pl.pallas_call(
        body, name=name, grid=(t // tile,), in_specs=[row, tgt],
        out_specs=(row, pl.BlockSpec((1, LANE), lambda i: (0, 0))),
        out_shape=(jax.ShapeDtypeStruct((t, d), F32), jax.ShapeDtypeStruct((1, LANE), F32)),
        compiler_params=_params("arbitrary"),
    )(h, target)
    return jnp.sum(loss), dh


def _adamw(w, g, m, v, name):
    shape = w.shape
    cols = shape[-1]
    rows = int(np.prod(shape[:-1])) if len(shape) > 1 else 1
    w2, g2, m2, v2 = (a.reshape(rows, cols) for a in (w, g, m, v))
    tr = _pick(rows, (512, 256, 128, 64, 32, 16, 8))
    c1 = 1.0 / (1.0 - ADAM_B1 ** ADAM_STEP)
    c2 = 1.0 / (1.0 - ADAM_B2 ** ADAM_STEP)

    def body(w_ref, g_ref, m_ref, v_ref, d_ref, mo_ref, vo_ref):
        gv = g_ref[...]
        mn = ADAM_B1 * m_ref[...] + (1.0 - ADAM_B1) * gv
        vn = ADAM_B2 * v_ref[...] + (1.0 - ADAM_B2) * (gv * gv)
        d_ref[...] = -ADAM_LR * ((mn * c1) / (jnp.sqrt(vn * c2) + ADAM_EPS) + ADAM_WD * w_ref[...])
        mo_ref[...] = mn
        vo_ref[...] = vn

    blk = pl.BlockSpec((tr, cols), lambda i: (i, 0))
    sds = jax.ShapeDtypeStruct((rows, cols), F32)
    d, mo, vo = pl.pallas_call(
        body, name=name, grid=(rows // tr,), in_specs=[blk] * 4, out_specs=(blk,) * 3, out_shape=(sds,) * 3,
        compiler_params=_params("parallel"),
    )(w2, g2, m2, v2)
    return d.reshape(shape), mo.reshape(shape), vo.reshape(shape)


SHARD_AXIS = {
    "w_mod": 2, "g_norm": 2, "ffn_w1": 3, "ffn_w3": 3, "ffn_w2": 2, "sc_w_in": 2, "sc_conv": 2, "sc_w_out": 1,
    "mla_w_a": 1, "mla_g_qa": 1, "mla_w_uq": 2, "mla_w_ukv": 2, "mla_w_o": 1,
}
HIDDEN_MAJOR = ("ffn_w1", "ffn_w3")


def _view(name, arr, swapped=False):
    form, swap, _ = EXCHANGE[name]
    if swap and not swapped:
        arr = jnp.swapaxes(arr, -1, -2)
    if form == "mid":
        arr = arr.reshape((-1,) + arr.shape[-2:])
        return jnp.pad(arr, ((0, 0), (0, 0), (0, -arr.shape[-1] % LANE)))
    arr = arr.reshape(-1, arr.shape[-1])
    return jnp.pad(arr, ((0, -arr.shape[0] % 16), (0, 0)))


def _unview(name, view, shape, keep_swapped=False):
    form, swap, _ = EXCHANGE[name]
    shape = shape[:-2] + (shape[-1], shape[-2]) if swap else shape
    if form == "mid":
        view = view[:, :, :shape[-1]]
    else:
        view = view[:int(np.prod(shape[:-1]))]
    arr = view.reshape(shape)
    return arr if (not swap or keep_swapped) else jnp.swapaxes(arr, -1, -2)


def _full_shape(name, local_shape):
    ax = SHARD_AXIS[name]
    return local_shape[:ax] + (N_DEV * local_shape[ax],) + local_shape[ax + 1:]


def _win(ref, form, n, j):
    start = j * n
    if not isinstance(start, int):
        start = pl.multiple_of(start, LANE if form == "last" else math.gcd(n, 16))
    if form == "mid":
        return ref.at[:, pl.ds(start, n), :]
    return ref.at[:, pl.ds(start, n)]


def _windows(view, count, of):
    return view.shape[:1] + (view.shape[1] * count // of,) + view.shape[2:]


def _gather_work(views, forms):
    na = len(views)

    def plan(x_refs, out_refs, sems):
        send_sems, recv_sems, local_sems = sems
        x, y, c = lax.axis_index("x"), lax.axis_index("y"), lax.axis_index("c")
        me, sibling = (x, y, c), (x, y, 1 - c)
        chips = [(1 - x, y), (x, 1 - y), (1 - x, 1 - y)]

        def copy(a, k, block, to, from_input):
            dst = _win(out_refs[a], forms[a], views[a].shape[1], 4 * block[0] + 2 * block[1] + block[2])
            return pltpu.make_async_remote_copy(
                src_ref=x_refs[a] if from_input else dst, dst_ref=dst, send_sem=send_sems.at[a, k],
                recv_sem=recv_sems.at[a, k], device_id=to, device_id_type=MESH)

        mine = [pltpu.make_async_copy(x_refs[a], _win(out_refs[a], forms[a], views[a].shape[1], 4 * x + 2 * y + c),
                                      local_sems.at[a]) for a in range(na)]
        first = []
        for a in range(na):
            first.append(copy(a, 0, me, sibling, True))
            first += [copy(a, 1 + j, me, (*chip, c), True) for j, chip in enumerate(chips)]
        return copy, mine, first, me, sibling, chips, c

    def start(x_refs, out_refs, sems):
        _, mine, first, *_ = plan(x_refs, out_refs, sems)
        for cp in mine + first:
            cp.start()

    def finish(x_refs, out_refs, sems):
        copy, mine, first, me, sibling, chips, c = plan(x_refs, out_refs, sems)
        passed = []
        for j, chip in enumerate(chips):
            for a in range(na):
                copy(a, 1 + j, (*chip, c), me, False).wait_recv()
                fwd = copy(a, 4 + j, (*chip, c), sibling, False)
                fwd.start()
                passed.append(fwd)
        for a in range(na):
            copy(a, 0, sibling, me, False).wait_recv()
            for j, chip in enumerate(chips):
                copy(a, 4 + j, (*chip, 1 - c), me, False).wait_recv()
        for cp in first + passed:
            cp.wait_send()
        for cp in mine:
            cp.wait()

    return Hosted(
        list(views), [jax.ShapeDtypeStruct(_windows(v, N_DEV, 1), v.dtype) for v in views],
        [pltpu.SemaphoreType.DMA((na, 7)), pltpu.SemaphoreType.DMA((na, 7)), pltpu.SemaphoreType.DMA((na,))], start, finish)


def _push_work(srcs, out_shapes, n_copies, make_copies):
    na = len(srcs)

    def start(s_refs, r_refs, sems):
        for cp in make_copies(s_refs, r_refs, sems[0], sems[1]):
            cp.start()

    def finish(s_refs, r_refs, sems):
        copies = make_copies(s_refs, r_refs, sems[0], sems[1])
        for cp in copies:
            cp.wait_recv()
        for cp in copies:
            cp.wait_send()

    return Hosted(list(srcs), out_shapes, [pltpu.SemaphoreType.DMA((na, n_copies)), pltpu.SemaphoreType.DMA((na, n_copies))],
                  start, finish)


def _sibling_work(fulls, forms):
    na = len(fulls)
    widths = [f.shape[1] // N_DEV for f in fulls]

    def make_copies(g_refs, r_refs, send_sems, recv_sems):
        x, y, c = lax.axis_index("x"), lax.axis_index("y"), lax.axis_index("c")
        return [
            pltpu.make_async_remote_copy(
                src_ref=_win(g_refs[a], forms[a], widths[a], 2 * chip + (1 - c)),
                dst_ref=_win(r_refs[a], forms[a], widths[a], chip), send_sem=send_sems.at[a, chip],
                recv_sem=recv_sems.at[a, chip], device_id=(x, y, 1 - c), device_id_type=MESH)
            for a in range(na) for chip in range(N_CHIP)
        ]

    return _push_work(fulls, [jax.ShapeDtypeStruct(_windows(f, N_CHIP, N_DEV), f.dtype) for f in fulls], N_CHIP, make_copies)


def _chip_work(parts, forms):
    na = len(parts)
    widths = [p.shape[1] // N_CHIP for p in parts]

    def make_copies(p_refs, r_refs, send_sems, recv_sems):
        x, y, c = lax.axis_index("x"), lax.axis_index("y"), lax.axis_index("c")
        chips = [(1 - x, y), (x, 1 - y), (1 - x, 1 - y)]
        return [
            pltpu.make_async_remote_copy(
                src_ref=_win(p_refs[a], forms[a], widths[a], 2 * px + py), dst_ref=_win(r_refs[a], forms[a], widths[a], j),
                send_sem=send_sems.at[a, j], recv_sem=recv_sems.at[a, j], device_id=(px, py, c), device_id_type=MESH)
            for a in range(na) for j, (px, py) in enumerate(chips)
        ]

    return _push_work(parts, [jax.ShapeDtypeStruct(_windows(p, 3, N_CHIP), p.dtype) for p in parts], 3, make_copies)


def _sum_tiles(view, form, n):
    if form == "mid":
        tr = n
        while tr * view.shape[2] * 4 > 2 * 1024 * 1024 and tr % 32 == 0:
            tr //= 2
        return 1, tr
    return _pick(view.shape[0], (512, 256, 128, 64, 32, 16)), n


def _window_spec(form, tl, tr, rest, window_of):
    if form == "mid":
        return lambda per: pl.BlockSpec((None, tr) + rest, lambda l, k, i, s: (l, window_of(k, s) * per + i, 0))
    return lambda per: pl.BlockSpec((tl, tr), lambda l, k, i, s: (l, window_of(k, s)))


def _chip_partials(g, recv, core, form, name):
    n = g.shape[1] // N_DEV
    tl, tr = _sum_tiles(g, form, n)
    per = n // tr
    rest = tuple(g.shape[2:])

    def body(core_ref, g_ref, r_ref, o_ref):
        o_ref[...] = (g_ref[...] + r_ref[...]).astype(o_ref.dtype)

    own = _window_spec(form, tl, tr, rest, lambda k, s: 2 * k + s[0])(per)
    by_chip = _window_spec(form, tl, tr, rest, lambda k, s: k)(per)
    return pl.pallas_call(
        body, name=name,
        grid_spec=pltpu.PrefetchScalarGridSpec(
            num_scalar_prefetch=1, grid=(g.shape[0] // tl, N_CHIP, per), in_specs=[own, by_chip], out_specs=by_chip),
        out_shape=jax.ShapeDtypeStruct(recv.shape, BF16), compiler_params=_params("parallel", "parallel", "parallel"),
    )(core, g, recv)


def _reduce_final(p, recv, chip, form, name):
    n = p.shape[1] // N_CHIP
    tl, tr = _sum_tiles(p, form, n)
    per = n // tr
    rest = tuple(p.shape[2:])

    def body(chip_ref, p_ref, ry_ref, rx_ref, rxy_ref, o_ref):
        own_pair = p_ref[...].astype(F32) + ry_ref[...].astype(F32)
        o_ref[...] = own_pair + (rx_ref[...].astype(F32) + rxy_ref[...].astype(F32))

    def rel(j):
        return _window_spec(form, tl, tr, rest, lambda k, s: j)(per)

    own = _window_spec(form, tl, tr, rest, lambda k, s: s[0])(per)
    return pl.pallas_call(
        body, name=name,
        grid_spec=pltpu.PrefetchScalarGridSpec(
            num_scalar_prefetch=1, grid=(p.shape[0] // tl, 1, per), in_specs=[own, rel(1), rel(0), rel(2)],
            out_specs=rel(0)),
        out_shape=jax.ShapeDtypeStruct(p.shape[:1] + (n,) + p.shape[2:], F32),
        compiler_params=_params("parallel", "parallel", "parallel"),
    )(chip, p, recv, recv, recv)


def _pack_replicated(arrays):
    pieces = []
    for a in arrays:
        flat = a.reshape(-1).astype(F32)
        pieces.append(jnp.pad(flat, (0, -flat.size % LANE)))
    total = sum(p.size for p in pieces)
    pieces.append(jnp.zeros((-total % (16 * LANE),), F32))
    return jnp.concatenate(pieces).reshape(-1, LANE)


def _unpack_replicated(buf, shapes):
    flat, out, off = buf.reshape(-1), [], 0
    for shape in shapes:
        size = int(np.prod(shape))
        out.append(flat[off:off + size].reshape(shape))
        off += size + (-size % LANE)
    return out


def _silu(v):
    return v * jax.nn.sigmoid(v)


SC_NAMES = ("sc_w_in", "sc_conv", "sc_w_out")
MLA_SHARDED = ("mla_w_a", "mla_g_qa", "mla_w_uq", "mla_w_ukv", "mla_w_o")
MLA_NAMES = ("mla_w_a", "mla_g_qa", "mla_w_uq", "mla_g_kva", "mla_w_ukv", "mla_g_q", "mla_g_k", "mla_w_o")


def _local_step(src, x, c, ctx, target):
    bsz, n_lat, d = x.shape
    n_ctx = ctx.shape[1]
    assert bsz == 2
    geo = Geo(n_lat, n_ctx)
    depth = src.depth
    tc = _conv_tile(d)
    tabs = _rope_tables(geo)

    h = jnp.concatenate([x.reshape(2 * n_lat, d), ctx.reshape(2 * n_ctx, d)], axis=0)
    tgt = target.reshape(2 * n_lat, d)
    cond = jnp.concatenate([c, src.c_ctx[None], jnp.zeros((8 - bsz - 1, d), F32)], axis=0)
    scond = _silu(cond)

    saved = []
    for i in range(depth):
        kind = i % 2
        wl, slots = src.weights(i), src.fwd_slots(i)
        gn = wl["g_norm"].astype(F32)
        mod = _mm(scond, wl["w_mod"], name=f"l{i}_mod") + wl["b_mod"][None]
        mod4 = mod[:N_SEG].reshape(N_SEG, N_MOD, 1, d)
        h, s1 = _ffn_fwd(h, gn[0:1], mod4, 0, wl, 0, geo, f"l{i}_f1", "f1", slots, slots)
        if kind == 0:
            mix = (_interleave(wl["sc_w_in"], 3, tc), wl["sc_conv"].astype(F32), wl["sc_w_out"])
            h, s2 = _sconv_fwd(h, gn[1:2], mod4, *mix, geo, f"l{i}_sc", slots, slots)
        else:
            mix = _mla_prepare(*[wl[name] for name in MLA_NAMES])
            h, s2 = _mla_fwd(h, gn[1:2], mod4, mix, i != depth - 1, tabs, geo, f"l{i}_mla", slots, slots)
        h, s3 = _ffn_fwd(h, gn[2:3], mod4, 6, wl, 1, geo, f"l{i}_f2", "f2", slots, slots)
        saved.append((wl, gn, mod4, mix, s1, s2, s3))

    loss, dh = _loss_head(h, tgt, geo, "loss_head")

    g_b_mod = [None] * depth
    dscond = jnp.zeros_like(scond)
    for i in reversed(range(depth)):
        kind = i % 2
        wl, gn, mod4, mix, s1, s2, s3 = saved[i]
        slots = src.bwd_slots(i)
        gbuf = {name: lax.empty(wl[name].shape, F32) for name in ("ffn_w1", "ffn_w3", "ffn_w2")}
        dh, dg2, dm2 = _ffn_bwd(dh, s3, gn[2:3], mod4, 6, wl, 1, gbuf, geo, f"l{i}_f2", "f2", slots, slots)
        if kind == 0:
            dh, dg1, dm1, dwin, dconv, dwout = _sconv_bwd(dh, s2, gn[1:2], mod4, *mix, geo, f"l{i}_sc", slots, slots)
            gl = dict(sc_w_in=_deinterleave(dwin, 3, tc), sc_conv=dconv, sc_w_out=dwout)
        else:
            dh, dg1, dm1, gm = _mla_bwd(dh, s2, gn[1:2], mod4, mix, i != depth - 1, tabs, geo, f"l{i}_mla", slots, slots)
            gl = _mla_unprepare(gm)
        dh, dg0, dm0 = _ffn_bwd(dh, s1, gn[0:1], mod4, 0, wl, 0, gbuf, geo, f"l{i}_f1", "f1", slots, slots)
        dmod = jnp.concatenate(list(dm0) + list(dm1) + list(dm2), axis=1).reshape(N_SEG, N_MOD * d)
        dmod8 = jnp.concatenate([dmod, jnp.zeros((8 - N_SEG, N_MOD * d), F32)], axis=0)
        g_b_mod[i] = jnp.sum(dmod, axis=0)
        gl.update(gbuf, g_norm=jnp.concatenate([dg0, dg1, dg2], axis=0),
                  w_mod=_mm(scond, dmod8, ta=True, name=f"l{i}_dwmod"))
        dscond = dscond + _mm(dmod8, wl["w_mod"], tb=True, name=f"l{i}_dcond")
        src.grads(i, gl)

    sg = jax.nn.sigmoid(cond)
    dcond = dscond * (sg * (1.0 + cond * (1.0 - sg)))
    grad_x = dh[:2 * n_lat].reshape(x.shape)
    return loss, grad_x, dcond[bsz], jnp.stack(g_b_mod)


class _Slots:
    def __init__(self, get, put):
        self.get, self._put = get, put

    def __setitem__(self, slot, outs):
        self._put(slot, outs)


FWD_PLAN = {
    0: {"f1_up": ("ffn_w1",), "f1_down": ("g_norm", "mix"), "mix_a": ("ffn_w3",), "mix_b": ("ffn_w2",), "f2_up": ("w_mod",)},
    1: {"f1_up": ("ffn_w1",), "mix_a": ("w_mod", "ffn_w3", "g_norm", "mix"), "f2_up": ("ffn_w2",)},
}
SIBLING_PLAN = {"f2_dact": ("ffn_w1", "w_mod", "g_norm", "mix"), "f2_dw2": ("ffn_w3", "ffn_w2")}
BWD_PLAN = {
    0: {"f2_dnx": ("ffn_w1",), "mix_a": ("w_mod",), "mix_b": ("ffn_w3",), "f1_dact": ("ffn_w2",), "f1_dnx": ("g_norm", "mix")},
    1: {"f2_dnx": ("ffn_w1",), "mix_a": ("w_mod", "ffn_w3", "ffn_w2"), "f1_dnx": ("g_norm", "mix")},
}


class _Exchange:
    def __init__(self, w):
        self.w = w
        self.depth = w["w_mod"].shape[0]
        self.c_ctx = w["c_ctx"]
        self.core = lax.axis_index("c").astype(jnp.int32).reshape(1)
        self.chip = (2 * lax.axis_index("x") + lax.axis_index("y")).astype(jnp.int32).reshape(1)
        self.full, self.gviews, self.parts, self.reduced, self.rep = {}, {}, {}, {}, {}

    def _layer_of(self, name, i):
        return i // 2 if name.startswith(("sc_", "mla_")) else i

    def _mixer(self, i):
        return SC_NAMES if i % 2 == 0 else MLA_SHARDED

    def _expand(self, names, i):
        out = []
        for name in names:
            out += list(self._mixer(i)) if name == "mix" else [name]
        return out

    def _group(self, i):
        return ["w_mod", "g_norm", "ffn_w1", "ffn_w3", "ffn_w2"] + list(self._mixer(i))

    def _local(self, name, i):
        arr = self.w[name][self._layer_of(name, i)]
        return arr[:, None] if name == "mla_g_qa" else arr

    def _shapes(self, name, i):
        local = tuple(self._local(name, i).shape)
        ax = SHARD_AXIS[name] - 1
        return local, local[:ax] + (N_DEV * local[ax],) + local[ax + 1:]

    def _gather(self, names, i):
        views = [_view(n, self._local(n, i).astype(BF16 if EXCHANGE[n][2] else F32)) for n in names]
        return _gather_work(views, [EXCHANGE[n][0] for n in names])

    def _gathered(self, names, i, outs):
        for name, fv in zip(names, outs):
            arr = _unview(name, fv, self._shapes(name, i)[1], keep_swapped=name in HIDDEN_MAJOR)
            self.full[name, i] = arr[:, 0] if name == "mla_g_qa" else arr

    def prefetch(self):
        names = self._group(0)
        self._gathered(names, 0, _run_hosted(self._gather(names, 0), "gather_l0"))

    def weights(self, i):
        wl = {name: self.full[name, i] for name in self._group(i)}
        wl["b_mod"] = self.w["b_mod"][i]
        if i % 2 == 1:
            for name in ("mla_g_kva", "mla_g_q", "mla_g_k"):
                wl[name] = self.w[name][i // 2]
        return wl

    def fwd_slots(self, i):
        plan = FWD_PLAN[i % 2] if i + 1 < self.depth else {}
        names = {slot: self._expand(plan[slot], i + 1) for slot in plan}
        return _Slots(lambda slot: self._gather(names[slot], i + 1) if slot in names else None,
                      lambda slot, outs: self._gathered(names[slot], i + 1, outs))

    def grads(self, i, gl):
        for name in self._group(i):
            g = gl[name][:, None] if name == "mla_g_qa" else gl[name]
            self.gviews[name, i] = _view(name, g, swapped=name in HIDDEN_MAJOR)
        for name in REPLICATED:
            if name in gl:
                self.rep[name, i // 2] = gl[name]

    def _forms(self, names):
        return [EXCHANGE[n][0] for n in names]

    def _partials(self, names, i, from_sibling):
        for name, recv in zip(names, from_sibling):
            self.parts[name, i] = _chip_partials(self.gviews[name, i], recv, self.core, EXCHANGE[name][0],
                                                 f"partial_{name}_{i}")

    def _finals(self, names, i, from_chips):
        for name, recv in zip(names, from_chips):
            rv = _reduce_final(self.parts[name, i], recv, self.chip, EXCHANGE[name][0], f"final_{name}_{i}")
            arr = _unview(name, rv, self._shapes(name, i)[0])
            self.reduced[name, i] = arr[:, 0] if name == "mla_g_qa" else arr

    def bwd_slots(self, i):
        if i + 1 >= self.depth:
            return _Slots(lambda slot: None, None)
        plan = BWD_PLAN[i % 2]
        names = {slot: self._expand(plan[slot], i + 1) for slot in plan}
        sibling = {slot: self._expand(SIBLING_PLAN[slot], i + 1) for slot in SIBLING_PLAN}

        def get(slot):
            if slot in sibling:
                return _sibling_work([self.gviews[n, i + 1] for n in sibling[slot]], self._forms(sibling[slot]))
            if slot in names:
                return _chip_work([self.parts[n, i + 1] for n in names[slot]], self._forms(names[slot]))
            return None

        def put(slot, outs):
            if slot in sibling:
                self._partials(sibling[slot], i + 1, outs)
            else:
                self._finals(names[slot], i + 1, outs)

        return _Slots(get, put)

    def finish(self, rep_grads):
        group = self._group(0)
        for name in REPLICATED:
            if name not in rep_grads:
                rep_grads[name] = jnp.stack([self.rep[name, j] for j in range(self.w[name].shape[0])])
        rep = _pack_replicated([rep_grads[name] for name in REPLICATED])
        views = [self.gviews[n, 0] for n in group] + [jnp.tile(rep[None], (1, N_DEV, 1))]
        forms = self._forms(group) + ["mid"]
        from_sibling = _run_hosted(_sibling_work(views, forms), "reduce_sibling_l0")
        self._partials(group, 0, from_sibling[:-1])
        rep_part = _chip_partials(views[-1], from_sibling[-1], self.core, "mid", "partial_replicated")
        parts = [self.parts[n, 0] for n in group] + [rep_part]
        from_chips = _run_hosted(_chip_work(parts, forms), "reduce_chips_l0")
        self._finals(group, 0, from_chips[:-1])
        rep_sum = _reduce_final(rep_part, from_chips[-1], self.chip, "mid", "final_replicated")
        out = dict(zip(REPLICATED, _unpack_replicated(rep_sum, [self.w[name].shape for name in REPLICATED])))
        for name in EXCHANGE:
            layers = range(self.w[name].shape[0])
            step = 2 if name.startswith(("sc_", "mla_")) else 1
            first = 1 if name.startswith("mla_") else 0
            out[name] = jnp.stack([self.reduced[name, first + step * l] for l in layers])
        return out


def kernel(x, c, ctx, c_ctx, w_mod, b_mod, g_norm, ffn_w1, ffn_w3, ffn_w2, sc_w_in, sc_conv, sc_w_out, mla_w_a, mla_g_qa, mla_w_uq, mla_g_kva, mla_w_ukv, mla_g_q, mla_g_k, mla_w_o, loss_target, m_c_ctx, m_w_mod, m_b_mod, m_g_norm, m_ffn_w1, m_ffn_w3, m_ffn_w2, m_sc_w_in, m_sc_conv, m_sc_w_out, m_mla_w_a, m_mla_g_qa, m_mla_w_uq, m_mla_g_kva, m_mla_w_ukv, m_mla_g_q, m_mla_g_k, m_mla_w_o, v_c_ctx, v_w_mod, v_b_mod, v_g_norm, v_ffn_w1, v_ffn_w3, v_ffn_w2, v_sc_w_in, v_sc_conv, v_sc_w_out, v_mla_w_a, v_mla_g_qa, v_mla_w_uq, v_mla_g_kva, v_mla_w_ukv, v_mla_g_q, v_mla_g_k, v_mla_w_o):
    w = dict(c_ctx=c_ctx, w_mod=w_mod, b_mod=b_mod, g_norm=g_norm, ffn_w1=ffn_w1, ffn_w3=ffn_w3, ffn_w2=ffn_w2,
             sc_w_in=sc_w_in, sc_conv=sc_conv, sc_w_out=sc_w_out, mla_w_a=mla_w_a, mla_g_qa=mla_g_qa, mla_w_uq=mla_w_uq,
             mla_g_kva=mla_g_kva, mla_w_ukv=mla_w_ukv, mla_g_q=mla_g_q, mla_g_k=mla_g_k, mla_w_o=mla_w_o)
    m = dict(c_ctx=m_c_ctx, w_mod=m_w_mod, b_mod=m_b_mod, g_norm=m_g_norm, ffn_w1=m_ffn_w1, ffn_w3=m_ffn_w3,
             ffn_w2=m_ffn_w2, sc_w_in=m_sc_w_in, sc_conv=m_sc_conv, sc_w_out=m_sc_w_out, mla_w_a=m_mla_w_a,
             mla_g_qa=m_mla_g_qa, mla_w_uq=m_mla_w_uq, mla_g_kva=m_mla_g_kva, mla_w_ukv=m_mla_w_ukv, mla_g_q=m_mla_g_q,
             mla_g_k=m_mla_g_k, mla_w_o=m_mla_w_o)
    v = dict(c_ctx=v_c_ctx, w_mod=v_w_mod, b_mod=v_b_mod, g_norm=v_g_norm, ffn_w1=v_ffn_w1, ffn_w3=v_ffn_w3,
             ffn_w2=v_ffn_w2, sc_w_in=v_sc_w_in, sc_conv=v_sc_conv, sc_w_out=v_sc_w_out, mla_w_a=v_mla_w_a,
             mla_g_qa=v_mla_g_qa, mla_w_uq=v_mla_w_uq, mla_g_kva=v_mla_g_kva, mla_w_ukv=v_mla_w_ukv, mla_g_q=v_mla_g_q,
             mla_g_k=v_mla_g_k, mla_w_o=v_mla_w_o)
    exchange = _Exchange(w)
    exchange.prefetch()
    loss, grad_x, g_c_ctx, g_b_mod = _local_step(exchange, x, c, ctx, loss_target)
    loss = lax.psum(loss, ("x", "y", "c"))
    reduced = exchange.finish(dict(c_ctx=g_c_ctx, b_mod=g_b_mod))

    outs = [[], [], [], []]
    for name in WEIGHTS:
        delta, new_m, new_v = _adamw(w[name], reduced[name], m[name], v[name], f"adamw_{name}")
        for lst, val in zip(outs, (reduced[name], delta, new_m, new_v)):
            lst.append(val)
    return (loss, grad_x, *outs[0], *outs[1], *outs[2], *outs[3])
```

```python
import functools
import math

import jax
import jax.numpy as jnp
import numpy as np
from jax import lax
from jax.experimental import pallas as pl
from jax.experimental.pallas import tpu as pltpu

F32 = jnp.float32
BF16 = jnp.bfloat16

N_MOD = 9
HEADS = 8
QK_NOPE = 128
QK_ROPE = 64
QK_HEAD = QK_NOPE + QK_ROPE
V_HEAD = 128
GRID_W = 64
ROPE_BASE = 10000.0
QK_SCALE = QK_HEAD ** -0.5
EPS = 1e-6
ADAM_LR, ADAM_B1, ADAM_B2, ADAM_EPS, ADAM_WD, ADAM_STEP = 0.001, 0.9, 0.999, 1e-08, 0.01, 10

N_DEV = 8
N_CHIP = 4
N_SEG = 3
LANE = 128
HEAD_PAD = 2 * LANE
VMEM_LIMIT_BYTES = 48 * 1024 * 1024
MESH = pl.DeviceIdType.MESH

WEIGHTS = ["c_ctx", "w_mod", "b_mod", "g_norm", "ffn_w1", "ffn_w3", "ffn_w2", "sc_w_in", "sc_conv", "sc_w_out",
           "mla_w_a", "mla_g_qa", "mla_w_uq", "mla_g_kva", "mla_w_ukv", "mla_g_q", "mla_g_k", "mla_w_o"]
EXCHANGE = {
    "w_mod": ("last", False, True), "ffn_w1": ("mid", True, True), "ffn_w3": ("mid", True, True),
    "ffn_w2": ("mid", False, True), "sc_w_in": ("last", False, True), "sc_w_out": ("mid", False, True),
    "mla_w_a": ("mid", False, True), "mla_w_uq": ("mid", True, True), "mla_w_ukv": ("last", False, True),
    "mla_w_o": ("mid", False, True), "g_norm": ("last", False, False), "sc_conv": ("last", False, False),
    "mla_g_qa": ("mid", False, False),
}
REPLICATED = ["c_ctx", "b_mod", "mla_g_kva", "mla_g_q", "mla_g_k"]


def _pick(n, cands):
    for cand in cands:
        if n % cand == 0:
            return cand
    return n


def _params(*sem):
    return pltpu.CompilerParams(dimension_semantics=sem, vmem_limit_bytes=VMEM_LIMIT_BYTES)


def _hbm():
    return pl.BlockSpec(memory_space=pl.ANY)


class Hosted:
    def __init__(self, inputs, out_shapes, scratch, start, finish):
        self.inputs, self.out_shapes, self.scratch, self.start, self.finish = inputs, out_shapes, scratch, start, finish


def _call(body, hosted, **kw):
    if hosted is None:
        return pl.pallas_call(body, **kw)
    single = not isinstance(kw["out_shape"], (tuple, list))
    out_shape = [kw["out_shape"]] if single else list(kw["out_shape"])
    out_specs = [kw["out_specs"]] if single else list(kw["out_specs"])
    in_specs, scratch, grid = list(kw["in_specs"]), list(kw.get("scratch_shapes", ())), kw["grid"]
    n_in, n_out, n_scr = len(in_specs), len(out_shape), len(scratch)
    h_in, h_out = len(hosted.inputs), len(hosted.out_shapes)

    def wrapped(*refs):
        ins, hins = refs[:n_in], refs[n_in:n_in + h_in]
        o0 = n_in + h_in
        outs, houts = refs[o0:o0 + n_out], refs[o0 + n_out:o0 + n_out + h_out]
        s0 = o0 + n_out + h_out
        scr, hscr = refs[s0:s0 + n_scr], refs[s0 + n_scr:]
        first = functools.reduce(jnp.logical_and, [pl.program_id(a) == 0 for a in range(len(grid))])
        last = functools.reduce(jnp.logical_and, [pl.program_id(a) == g - 1 for a, g in enumerate(grid)])

        @pl.when(first)
        def _():
            hosted.start(hins, houts, hscr)

        body(*ins, *outs, *scr)

        @pl.when(last)
        def _():
            hosted.finish(hins, houts, hscr)

    call = pl.pallas_call(
        wrapped, name=kw["name"], grid=grid, in_specs=in_specs + [_hbm()] * h_in,
        out_specs=tuple(out_specs + [_hbm()] * h_out), out_shape=tuple(out_shape + list(hosted.out_shapes)),
        scratch_shapes=scratch + list(hosted.scratch), input_output_aliases=kw.get("input_output_aliases", {}),
        compiler_params=_params(*["arbitrary"] * len(grid)))

    def run(*args):
        res = call(*args, *hosted.inputs)
        comp = res[:n_out]
        return (comp[0] if single else tuple(comp)), list(res[n_out:])

    return run


def _run_hosted(hosted, name):
    def body(*refs):
        h_in, h_out = len(hosted.inputs), len(hosted.out_shapes)
        hins, houts, hscr = refs[:h_in], refs[h_in:h_in + h_out], refs[h_in + h_out:]
        hosted.start(hins, houts, hscr)
        hosted.finish(hins, houts, hscr)

    return list(pl.pallas_call(
        body, name=name, in_specs=[_hbm()] * len(hosted.inputs), out_specs=tuple([_hbm()] * len(hosted.out_shapes)),
        out_shape=tuple(hosted.out_shapes), scratch_shapes=list(hosted.scratch))(*hosted.inputs))


class Geo:
    def __init__(self, n_lat, n_ctx):
        self.n_lat, self.n_ctx = n_lat, n_ctx
        self.rows = 2 * n_lat + 2 * n_ctx
        self.tile = n_ctx
        assert n_lat % n_ctx == 0 and n_ctx % 16 == 0
        self.mm_tile = _pick(n_lat, (512, 256, 128)) if self.rows % _pick(n_lat, (512, 256, 128)) == 0 else n_ctx
        self.big_tile = _pick(self.rows, (1536, 768, 512, 256))

    def seg(self, i, tile):
        return jnp.minimum((i * tile) // self.n_lat, N_SEG - 1)

    def seg_start(self, i, tile):
        row = i * tile
        return jnp.logical_or(row % self.n_lat == 0, row == 2 * self.n_lat) & (row <= 2 * self.n_lat)


_NT = (((1,), (1,)), ((), ()))
_NN = (((1,), (0,)), ((), ()))
_TN = (((0,), (0,)), ((), ()))


def _dot(a, b, dims):
    return lax.dot_general(a.astype(BF16), b.astype(BF16), dims, preferred_element_type=F32)


def _mm(a, b, *, ta=False, tb=False, out_dtype=F32, name, gate=None, hosted=None):
    (kdim, m) = a.shape if ta else a.shape[::-1]
    n = b.shape[0] if tb else b.shape[1]
    assert (b.shape[1] if tb else b.shape[0]) == kdim
    if gate is not None:
        tm = gate[4].mm_tile
    else:
        tm = _pick(m, (512, 256, 128))
    tn = _pick(n, (512, 256, 128))
    tk = _pick(kdim, (1024, 512, 256, 128))
    nk = kdim // tk
    dims = (((0 if ta else 1,), (1 if tb else 0,)), ((), ()))

    def body(*refs):
        if gate is not None:
            a_ref, b_ref, res_ref, gate_ref, o_ref, y_ref, acc_ref = refs
        else:
            a_ref, b_ref, o_ref, acc_ref = refs
        kk = pl.program_id(2)

        @pl.when(kk == 0)
        def _():
            acc_ref[...] = jnp.zeros_like(acc_ref)

        acc_ref[...] += lax.dot_general(a_ref[...].astype(BF16), b_ref[...].astype(BF16), dims,
                                        preferred_element_type=F32)

        @pl.when(kk == nk - 1)
        def _():
            acc = acc_ref[...]
            if gate is not None:
                y_ref[...] = acc.astype(y_ref.dtype)
                o_ref[...] = res_ref[...] + (gate[3] * gate_ref[...]) * acc
            else:
                o_ref[...] = acc.astype(o_ref.dtype)

    a_spec = pl.BlockSpec((tk, tm), lambda i, j, k: (k, i)) if ta else pl.BlockSpec((tm, tk), lambda i, j, k: (i, k))
    b_spec = pl.BlockSpec((tn, tk), lambda i, j, k: (j, k)) if tb else pl.BlockSpec((tk, tn), lambda i, j, k: (k, j))
    o_spec = pl.BlockSpec((tm, tn), lambda i, j, k: (i, j))
    in_specs, args = [a_spec, b_spec], [a, b]
    out_shape, out_specs = jax.ShapeDtypeStruct((m, n), out_dtype), o_spec
    if gate is not None:
        res, mod4, kmod, _, geo = gate
        in_specs += [o_spec, pl.BlockSpec((None, None, 1, tn), lambda i, j, k: (geo.seg(i, tm), kmod, 0, j))]
        args += [res, mod4]
        out_shape = (jax.ShapeDtypeStruct((m, n), F32), jax.ShapeDtypeStruct((m, n), BF16))
        out_specs = (o_spec, o_spec)
    return _call(
        body, hosted, name=name, grid=(m // tm, n // tn, nk), in_specs=in_specs, out_specs=out_specs,
        out_shape=out_shape, scratch_shapes=[pltpu.VMEM((tm, tn), F32)],
        compiler_params=_params("parallel", "parallel", "arbitrary"),
    )(*args)


def _tn_wide(lhs, rhs, name, into=None, s0=0, hosted=None):
    t, m = lhs.shape
    n = rhs.shape[1]
    tm = _pick(m, (1408, 1024, 512, 256, 128))
    while tm * n * 4 > 6.5 * 1024 * 1024 and tm % 256 == 0:
        tm //= 2
    tk = next(c for c in (1536, 768, 512, 256, 128, t)
              if t % c == 0 and c * (tm + n) * 4 + tm * n * 12 <= 36 * 1024 * 1024)

    def body(l_ref, r_ref, *rest):
        o_ref = rest[-1]
        kk = pl.program_id(1)
        part = lax.dot_general(l_ref[...], r_ref[...], _TN, preferred_element_type=F32)

        @pl.when(kk == 0)
        def _():
            o_ref[...] = part

        @pl.when(kk > 0)
        def _():
            o_ref[...] += part

    in_specs = [pl.BlockSpec((tk, tm), lambda i, k: (k, i)), pl.BlockSpec((tk, n), lambda i, k: (k, 0))]
    if into is None:
        return _call(
            body, hosted, name=name, grid=(m // tm, t // tk), in_specs=in_specs,
            out_specs=pl.BlockSpec((tm, n), lambda i, k: (i, 0)), out_shape=jax.ShapeDtypeStruct((m, n), F32),
            compiler_params=_params("parallel", "arbitrary"),
        )(lhs, rhs)
    return _call(
        body, hosted, name=name, grid=(m // tm, t // tk), in_specs=in_specs + [pl.BlockSpec(memory_space=pl.ANY)],
        out_specs=pl.BlockSpec((None, tm, n), lambda i, k: (s0, i, 0)),
        out_shape=jax.ShapeDtypeStruct(into.shape, into.dtype), input_output_aliases={2: 0},
        compiler_params=_params("parallel", "arbitrary"),
    )(lhs, rhs, into)


def _mod_spec(geo, tile, kmod, d):
    return pl.BlockSpec((None, None, 1, d), lambda i: (geo.seg(i, tile), kmod, 0, 0))


def _pre_fwd(h, g, mod4, k_shift, geo, name):
    t, d = h.shape
    tile = geo.tile

    def body(h_ref, g_ref, sh_ref, sc_ref, o_ref):
        hv = h_ref[...]
        r = lax.rsqrt(jnp.mean(hv * hv, axis=-1, keepdims=True) + EPS)
        y = hv * r * g_ref[...]
        o_ref[...] = (y * (1.0 + sc_ref[...]) + sh_ref[...]).astype(o_ref.dtype)

    row = pl.BlockSpec((tile, d), lambda i: (i, 0))
    return pl.pallas_call(
        body, name=name, grid=(t // tile,),
        in_specs=[row, pl.BlockSpec((1, d), lambda i: (0, 0)), _mod_spec(geo, tile, k_shift, d),
                  _mod_spec(geo, tile, k_shift + 1, d)],
        out_specs=row, out_shape=jax.ShapeDtypeStruct((t, d), BF16), compiler_params=_params("parallel"),
    )(h, g, mod4, mod4)


def _pre_bwd(h, g, mod4, k_shift, dnx, dres, geo, name):
    t, d = h.shape
    tile = geo.tile

    def body(h_ref, g_ref, sc_ref, dnx_ref, dres_ref, dh_ref, dg_ref, dsh_ref, dsc_ref):
        i = pl.program_id(0)
        hv, gv, dout = h_ref[...], g_ref[...], dnx_ref[...].astype(F32)
        r = lax.rsqrt(jnp.mean(hv * hv, axis=-1, keepdims=True) + EPS)
        xhat = hv * r
        dy = dout * (1.0 + sc_ref[...])
        u = dy * gv
        dh_ref[...] = r * (u - xhat * jnp.mean(u * xhat, axis=-1, keepdims=True)) + dres_ref[...]

        @pl.when(i == 0)
        def _():
            dg_ref[...] = jnp.zeros_like(dg_ref)

        @pl.when(geo.seg_start(i, tile))
        def _():
            dsh_ref[...] = jnp.zeros_like(dsh_ref)
            dsc_ref[...] = jnp.zeros_like(dsc_ref)

        dg_ref[...] += jnp.sum(dy * xhat, axis=0, keepdims=True)
        dsh_ref[...] += jnp.sum(dout, axis=0, keepdims=True)
        dsc_ref[...] += jnp.sum(dout * (xhat * gv), axis=0, keepdims=True)

    row = pl.BlockSpec((tile, d), lambda i: (i, 0))
    vec = pl.BlockSpec((1, d), lambda i: (0, 0))
    segv = pl.BlockSpec((None, 1, d), lambda i: (geo.seg(i, tile), 0, 0))
    return pl.pallas_call(
        body, name=name, grid=(t // tile,),
        in_specs=[row, vec, _mod_spec(geo, tile, k_shift + 1, d), row, row],
        out_specs=(row, vec, segv, segv),
        out_shape=(jax.ShapeDtypeStruct((t, d), F32), jax.ShapeDtypeStruct((1, d), F32),
                   jax.ShapeDtypeStruct((N_SEG, 1, d), F32), jax.ShapeDtypeStruct((N_SEG, 1, d), F32)),
        compiler_params=_params("arbitrary"),
    )(h, g, mod4, dnx, dres)


def _gate_bwd(dh, y, mod4, k_gate, coef, geo, name):
    t, d = dh.shape
    tile = geo.tile

    def body(dh_ref, y_ref, gt_ref, dy_ref, dgt_ref):
        i = pl.program_id(0)
        dhv = dh_ref[...]
        dy_ref[...] = ((coef * gt_ref[...]) * dhv).astype(dy_ref.dtype)

        @pl.when(geo.seg_start(i, tile))
        def _():
            dgt_ref[...] = jnp.zeros_like(dgt_ref)

        dgt_ref[...] += coef * jnp.sum(dhv * y_ref[...].astype(F32), axis=0, keepdims=True)

    row = pl.BlockSpec((tile, d), lambda i: (i, 0))
    segv = pl.BlockSpec((None, 1, d), lambda i: (geo.seg(i, tile), 0, 0))
    return pl.pallas_call(
        body, name=name, grid=(t // tile,), in_specs=[row, row, _mod_spec(geo, tile, k_gate, d)],
        out_specs=(row, segv),
        out_shape=(jax.ShapeDtypeStruct((t, d), BF16), jax.ShapeDtypeStruct((N_SEG, 1, d), F32)),
        compiler_params=_params("arbitrary"),
    )(dh, y, mod4)


def _ff_tile(f):
    return _pick(f, (256, 128))


def _ffn_up(nx, w1t, w3t, s0, geo, name, hosted=None):
    t, d = nx.shape
    f = w1t.shape[1]
    tm, tn = geo.big_tile, _ff_tile(f)

    def body(x_ref, w1_ref, w3_ref, ga_ref, gb_ref, act_ref):
        xv = x_ref[...]
        a = lax.dot_general(xv, w1_ref[...], _NT, preferred_element_type=F32)
        bv = lax.dot_general(xv, w3_ref[...], _NT, preferred_element_type=F32)
        sg = jax.nn.sigmoid(a)
        silu = a * sg
        ga_ref[...] = (bv * (sg + silu * (1.0 - sg))).astype(ga_ref.dtype)
        gb_ref[...] = silu.astype(gb_ref.dtype)
        act_ref[...] = (silu * bv).astype(act_ref.dtype)

    w_spec = pl.BlockSpec((None, tn, d), lambda i, j: (s0, j, 0))
    o_spec = pl.BlockSpec((tm, tn), lambda i, j: (i, j))
    sds = jax.ShapeDtypeStruct((t, f), BF16)
    return _call(
        body, hosted, name=name, grid=(t // tm, f // tn),
        in_specs=[pl.BlockSpec((tm, d), lambda i, j: (i, 0)), w_spec, w_spec],
        out_specs=(o_spec,) * 3, out_shape=(sds,) * 3, compiler_params=_params("parallel", "parallel"),
    )(nx, w1t, w3t)


def _ffn_down(act, w2, s0, res, mod4, k_gate, geo, name, hosted=None):
    t, f = act.shape
    d = w2.shape[2]
    tm, tn = geo.mm_tile, _pick(d, (1024, 512, 256, 128))

    def body(a_ref, w_ref, res_ref, gate_ref, o_ref, y_ref):
        acc = lax.dot_general(a_ref[...], w_ref[...], _NN, preferred_element_type=F32)
        y_ref[...] = acc.astype(y_ref.dtype)
        o_ref[...] = res_ref[...] + (0.5 * gate_ref[...]) * acc

    o_spec = pl.BlockSpec((tm, tn), lambda i, j: (i, j))
    return _call(
        body, hosted, name=name, grid=(t // tm, d // tn),
        in_specs=[pl.BlockSpec((tm, f), lambda i, j: (i, 0)), pl.BlockSpec((None, f, tn), lambda i, j: (s0, 0, j)),
                  o_spec, pl.BlockSpec((None, None, 1, tn), lambda i, j: (geo.seg(i, tm), k_gate, 0, j))],
        out_specs=(o_spec, o_spec),
        out_shape=(jax.ShapeDtypeStruct((t, d), F32), jax.ShapeDtypeStruct((t, d), BF16)),
        compiler_params=_params("parallel", "parallel"),
    )(act, w2, res, mod4)


def _ffn_dact(dy, w2, ga, gb, s0, geo, name, hosted=None):
    t, d = dy.shape
    f = w2.shape[1]
    tm, tn = geo.big_tile, _ff_tile(f)

    def body(dy_ref, w_ref, ga_ref, gb_ref, da_ref, db_ref):
        dact = lax.dot_general(dy_ref[...], w_ref[...], _NT, preferred_element_type=F32)
        da_ref[...] = (dact * ga_ref[...].astype(F32)).astype(da_ref.dtype)
        db_ref[...] = (dact * gb_ref[...].astype(F32)).astype(db_ref.dtype)

    o_spec = pl.BlockSpec((tm, tn), lambda i, j: (i, j))
    sds = jax.ShapeDtypeStruct((t, f), BF16)
    return _call(
        body, hosted, name=name, grid=(t // tm, f // tn),
        in_specs=[pl.BlockSpec((tm, d), lambda i, j: (i, 0)), pl.BlockSpec((None, tn, d), lambda i, j: (s0, j, 0)),
                  o_spec, o_spec],
        out_specs=(o_spec, o_spec), out_shape=(sds, sds), compiler_params=_params("parallel", "parallel"),
    )(dy, w2, ga, gb)


def _ffn_dnx(da, db, w1t, w3t, s0, geo, name, hosted=None):
    t, f = da.shape
    d = w1t.shape[2]
    tm, tn = geo.mm_tile, _pick(d, (512, 256, 128))

    def body(da_ref, db_ref, w1_ref, w3_ref, o_ref):
        o_ref[...] = (lax.dot_general(da_ref[...], w1_ref[...], _NN, preferred_element_type=F32)
                      + lax.dot_general(db_ref[...], w3_ref[...], _NN, preferred_element_type=F32))

    x_spec = pl.BlockSpec((tm, f), lambda j, i: (i, 0))
    w_spec = pl.BlockSpec((None, f, tn), lambda j, i: (s0, 0, j))
    return _call(
        body, hosted, name=name, grid=(d // tn, t // tm), in_specs=[x_spec, x_spec, w_spec, w_spec],
        out_specs=pl.BlockSpec((tm, tn), lambda j, i: (i, j)), out_shape=jax.ShapeDtypeStruct((t, d), F32),
        compiler_params=_params("parallel", "parallel"),
    )(da, db, w1t, w3t)


def _with_host(fn, hosts, got, slot, *args, **kw):
    hosted = hosts.get(slot)
    if hosted is None:
        return fn(*args, **kw)
    out, got[slot] = fn(*args, hosted=hosted, **kw)
    return out


def _ffn_fwd(h, g, mod4, k0, w, s0, geo, tag, sub, hosts, got):
    nx = _pre_fwd(h, g, mod4, k0, geo, f"{tag}_pre")
    a, b, act = _with_host(_ffn_up, hosts, got, f"{sub}_up", nx, w["ffn_w1"], w["ffn_w3"], s0, geo, f"{tag}_up")
    h_out, y = _with_host(_ffn_down, hosts, got, f"{sub}_down", act, w["ffn_w2"], s0, h, mod4, k0 + 2, geo, f"{tag}_down")
    return h_out, (h, nx, a, b, act, y)


def _ffn_bwd(dh_out, saved, g, mod4, k0, w, s0, gbuf, geo, tag, sub, hosts, got):
    h, nx, a, b, act, y = saved
    dy, dgate = _gate_bwd(dh_out, y, mod4, k0 + 2, 0.5, geo, f"{tag}_dgate")
    da, db = _with_host(_ffn_dact, hosts, got, f"{sub}_dact", dy, w["ffn_w2"], a, b, s0, geo, f"{tag}_dact")
    gbuf["ffn_w2"] = _with_host(_tn_wide, hosts, got, f"{sub}_dw2", act, dy, f"{tag}_dw2", into=gbuf["ffn_w2"], s0=s0)
    dnx = _with_host(_ffn_dnx, hosts, got, f"{sub}_dnx", da, db, w["ffn_w1"], w["ffn_w3"], s0, geo, f"{tag}_dnx")
    gbuf["ffn_w1"] = _tn_wide(da, nx, f"{tag}_dw1", into=gbuf["ffn_w1"], s0=s0)
    gbuf["ffn_w3"] = _tn_wide(db, nx, f"{tag}_dw3", into=gbuf["ffn_w3"], s0=s0)
    dh, dg, dshift, dscale = _pre_bwd(h, g, mod4, k0, dnx, dh_out, geo, f"{tag}_dpre")
    return dh, dg, (dshift, dscale, dgate)


def _interleave(w, n_parts, tile):
    lead, cols = w.shape[:-1], w.shape[-1] // n_parts
    return w.reshape(*lead, n_parts, cols // tile, tile).swapaxes(-3, -2).reshape(*lead, n_parts * cols)


def _deinterleave(w, n_parts, tile):
    lead, cols = w.shape[:-1], w.shape[-1] // n_parts
    return w.reshape(*lead, cols // tile, n_parts, tile).swapaxes(-3, -2).reshape(*lead, n_parts * cols)


HALO = 16


def _conv_tile(c):
    return _pick(c, (256, 128))


def _conv_specs(geo, tc, t):
    tile = geo.tile
    per = tile // HALO
    last = t // HALO - 1
    cur = pl.BlockSpec((tile, 3 * tc), lambda j, i: (i, j))
    prev = pl.BlockSpec((HALO, 3 * tc), lambda j, i: (jnp.maximum(i * per - 1, 0), j))
    nxt = pl.BlockSpec((HALO, 3 * tc), lambda j, i: (jnp.minimum((i + 1) * per, last), j))
    return cur, prev, nxt


def _conv_edges(geo, i):
    tile = geo.tile
    row = i * tile
    lat = row < 2 * geo.n_lat
    first = jnp.where(lat, row % geo.n_lat == 0, (row - 2 * geo.n_lat) % geo.n_ctx == 0)
    end = row + tile
    last = jnp.where(lat, end % geo.n_lat == 0, (end - 2 * geo.n_lat) % geo.n_ctx == 0)
    return first, last


def _shift_rows(v, before, after):
    n = v.shape[0]
    rows = lax.broadcasted_iota(jnp.int32, v.shape, 0)
    down = jnp.where(rows == 0, before, pltpu.roll(v, 1, 0))
    up = jnp.where(rows == n - 1, after, pltpu.roll(v, n - 1, 0))
    return down, up


def _conv_fwd(proj, conv_w, geo, name, hosted=None):
    t, c3 = proj.shape
    c = c3 // 3
    tc, tile = _conv_tile(c), geo.tile

    def body(cur_ref, prev_ref, next_ref, w_ref, o_ref):
        first, last = _conv_edges(geo, pl.program_id(1))
        bv = cur_ref[:, :tc].astype(F32)
        p = cur_ref[:, tc:2 * tc].astype(F32) * cur_ref[:, 2 * tc:].astype(F32)
        p_before = prev_ref[HALO - 1:HALO, tc:2 * tc].astype(F32) * prev_ref[HALO - 1:HALO, 2 * tc:].astype(F32)
        p_after = next_ref[0:1, tc:2 * tc].astype(F32) * next_ref[0:1, 2 * tc:].astype(F32)
        p_before = jnp.where(first, 0.0, p_before)
        p_after = jnp.where(last, 0.0, p_after)
        pm1, pp1 = _shift_rows(p, p_before, p_after)
        w = w_ref[...]
        q = w[0:1] * pm1 + w[1:2] * p + w[2:3] * pp1
        o_ref[...] = (bv * q).astype(o_ref.dtype)

    cur, prev, nxt = _conv_specs(geo, tc, t)
    return _call(
        body, hosted, name=name, grid=(c // tc, t // tile),
        in_specs=[cur, prev, nxt, pl.BlockSpec((3, tc), lambda j, i: (0, j))],
        out_specs=pl.BlockSpec((tile, tc), lambda j, i: (i, j)), out_shape=jax.ShapeDtypeStruct((t, c), BF16),
        compiler_params=_params("parallel", "parallel"),
    )(proj, proj, proj, conv_w)


def _conv_bwd(proj, dyc, conv_w, geo, name):
    t, c3 = proj.shape
    c = c3 // 3
    tc, tile = _conv_tile(c), geo.tile

    def body(cur_ref, prev_ref, next_ref, d_ref, dprev_ref, dnext_ref, w_ref, o_ref, dw_ref):
        i = pl.program_id(1)
        first, last = _conv_edges(geo, i)
        bv = cur_ref[:, :tc].astype(F32)
        cv = cur_ref[:, tc:2 * tc].astype(F32)
        uv = cur_ref[:, 2 * tc:].astype(F32)
        p = cv * uv
        p_before = prev_ref[HALO - 1:HALO, tc:2 * tc].astype(F32) * prev_ref[HALO - 1:HALO, 2 * tc:].astype(F32)
        p_after = next_ref[0:1, tc:2 * tc].astype(F32) * next_ref[0:1, 2 * tc:].astype(F32)
        p_before = jnp.where(first, 0.0, p_before)
        p_after = jnp.where(last, 0.0, p_after)
        pm1, pp1 = _shift_rows(p, p_before, p_after)
        w = w_ref[...]
        q = w[0:1] * pm1 + w[1:2] * p + w[2:3] * pp1
        dy = d_ref[...].astype(F32)
        dq = dy * bv
        dq_before = dprev_ref[HALO - 1:HALO, :].astype(F32) * prev_ref[HALO - 1:HALO, :tc].astype(F32)
        dq_after = dnext_ref[0:1, :].astype(F32) * next_ref[0:1, :tc].astype(F32)
        dq_before = jnp.where(first, 0.0, dq_before)
        dq_after = jnp.where(last, 0.0, dq_after)
        dqm1, dqp1 = _shift_rows(dq, dq_before, dq_after)
        dp = w[0:1] * dqp1 + w[1:2] * dq + w[2:3] * dqm1
        o_ref[:, :tc] = (dy * q).astype(o_ref.dtype)
        o_ref[:, tc:2 * tc] = (dp * uv).astype(o_ref.dtype)
        o_ref[:, 2 * tc:] = (dp * cv).astype(o_ref.dtype)

        @pl.when(i == 0)
        def _():
            dw_ref[...] = jnp.zeros_like(dw_ref)

        dw_ref[0:1, :] += jnp.sum(dq * pm1, axis=0, keepdims=True)
        dw_ref[1:2, :] += jnp.sum(dq * p, axis=0, keepdims=True)
        dw_ref[2:3, :] += jnp.sum(dq * pp1, axis=0, keepdims=True)

    cur, prev, nxt = _conv_specs(geo, tc, t)
    per, lastb = tile // HALO, t // HALO - 1
    dcur = pl.BlockSpec((tile, tc), lambda j, i: (i, j))
    dprev = pl.BlockSpec((HALO, tc), lambda j, i: (jnp.maximum(i * per - 1, 0), j))
    dnext = pl.BlockSpec((HALO, tc), lambda j, i: (jnp.minimum((i + 1) * per, lastb), j))
    wspec = pl.BlockSpec((3, tc), lambda j, i: (0, j))
    return pl.pallas_call(
        body, name=name, grid=(c // tc, t // tile), in_specs=[cur, prev, nxt, dcur, dprev, dnext, wspec],
        out_specs=(cur, wspec), out_shape=(jax.ShapeDtypeStruct((t, c3), BF16), jax.ShapeDtypeStruct((3, c), F32)),
        compiler_params=_params("parallel", "arbitrary"),
    )(proj, proj, proj, dyc, dyc, dyc, conv_w)


def _sconv_fwd(h, g, mod4, w_in, conv_w, w_out, geo, tag, hosts, got):
    nx = _pre_fwd(h, g, mod4, 3, geo, f"{tag}_pre")
    proj = _with_host(_mm, hosts, got, "mix_a", nx, w_in, out_dtype=BF16, name=f"{tag}_in")
    yc = _with_host(_conv_fwd, hosts, got, "mix_b", proj, conv_w, geo, f"{tag}_conv")
    h_out, y = _mm(yc, w_out, name=f"{tag}_out", gate=(h, mod4, 5, 1.0, geo))
    return h_out, (h, nx, proj, yc, y)


def _sconv_bwd(dh_out, saved, g, mod4, w_in, conv_w, w_out, geo, tag, hosts, got):
    h, nx, proj, yc, y = saved
    dy, dgate = _gate_bwd(dh_out, y, mod4, 5, 1.0, geo, f"{tag}_dgate")
    dyc = _mm(dy, w_out, tb=True, out_dtype=BF16, name=f"{tag}_dyc")
    dw_out = _tn_wide(yc, dy, f"{tag}_dwout")
    dproj, dconv = _conv_bwd(proj, dyc, conv_w, geo, f"{tag}_dconv")
    dnx = _with_host(_mm, hosts, got, "mix_b", dproj, w_in, tb=True, name=f"{tag}_dnx")
    dw_in = _with_host(_tn_wide, hosts, got, "mix_a", nx, dproj, f"{tag}_dwin")
    dh, dg, dshift, dscale = _pre_bwd(h, g, mod4, 3, dnx, dh_out, geo, f"{tag}_dpre")
    return dh, dg, (dshift, dscale, dgate), dw_in, dconv, dw_out


def _rope_swap(v):
    nf = QK_ROPE // 4
    return v.reshape(v.shape[:-1] + (2, 2, nf)).swapaxes(-3, -2).reshape(v.shape)


def _rope_tables(geo):
    n = geo.n_lat
    nf = QK_ROPE // 4
    pos = np.arange(n)
    inv = ROPE_BASE ** (-np.arange(nf, dtype=np.float32) / nf)
    ang = np.concatenate([(pos // GRID_W)[:, None] * inv, (pos % GRID_W)[:, None] * inv], axis=1).astype(np.float32)
    cos, sin = np.cos(ang), np.sin(ang)
    zeros = np.zeros((n, LANE - QK_ROPE), np.float32)
    c_lat = np.concatenate([cos, cos, zeros], axis=1)
    s_lat = np.concatenate([-sin, sin, zeros], axis=1)
    c_ctx = np.concatenate([np.ones((2 * geo.n_ctx, QK_ROPE), np.float32), np.zeros((2 * geo.n_ctx, LANE - QK_ROPE), np.float32)], 1)
    s_ctx = np.zeros((2 * geo.n_ctx, LANE), np.float32)
    return (jnp.asarray(np.concatenate([c_lat, c_lat, c_ctx], 0)), jnp.asarray(np.concatenate([s_lat, s_lat, s_ctx], 0)))


def _swap_halves(v):
    lanes = lax.broadcasted_iota(jnp.int32, v.shape, 1)
    return jnp.where(lanes < QK_ROPE // 2, pltpu.roll(v, LANE - QK_ROPE // 2, 1), pltpu.roll(v, QK_ROPE // 2, 1))


def _latent_norm_fwd(down, g_qa, g_kva, geo, name):
    t, wd = down.shape
    ql, kl = g_qa.shape[1], g_kva.shape[1]
    tile = geo.tile

    def body(d_ref, gq_ref, gk_ref, cq_ref, ckv_ref):
        for lo, n, g_ref, o_ref in ((0, ql, gq_ref, cq_ref), (ql, kl, gk_ref, ckv_ref)):
            x = d_ref[:, lo:lo + n]
            r = lax.rsqrt(jnp.mean(x * x, axis=-1, keepdims=True) + EPS)
            o_ref[...] = (x * r * g_ref[...]).astype(o_ref.dtype)

    return pl.pallas_call(
        body, name=name, grid=(t // tile,),
        in_specs=[pl.BlockSpec((tile, wd), lambda i: (i, 0)), pl.BlockSpec((1, ql), lambda i: (0, 0)),
                  pl.BlockSpec((1, kl), lambda i: (0, 0))],
        out_specs=(pl.BlockSpec((tile, ql), lambda i: (i, 0)), pl.BlockSpec((tile, kl), lambda i: (i, 0))),
        out_shape=(jax.ShapeDtypeStruct((t, ql), BF16), jax.ShapeDtypeStruct((t, kl), BF16)),
        compiler_params=_params("parallel"),
    )(down, g_qa, g_kva)


def _latent_norm_bwd(down, g_qa, g_kva, dcqn, dckvn, dkr, geo, name):
    t, wd = down.shape
    ql, kl = g_qa.shape[1], g_kva.shape[1]
    tile = geo.tile

    def body(d_ref, gq_ref, gk_ref, dq_ref, dk_ref, dkr_ref, o_ref, dgq_ref, dgk_ref):
        i = pl.program_id(0)

        @pl.when(i == 0)
        def _():
            dgq_ref[...] = jnp.zeros_like(dgq_ref)
            dgk_ref[...] = jnp.zeros_like(dgk_ref)

        for lo, n, g_ref, dy_ref, dg_ref in ((0, ql, gq_ref, dq_ref, dgq_ref), (ql, kl, gk_ref, dk_ref, dgk_ref)):
            x = d_ref[:, lo:lo + n]
            dy = dy_ref[...].astype(F32)
            r = lax.rsqrt(jnp.mean(x * x, axis=-1, keepdims=True) + EPS)
            xhat = x * r
            u = dy * g_ref[...]
            o_ref[:, lo:lo + n] = (r * (u - xhat * jnp.mean(u * xhat, axis=-1, keepdims=True))).astype(o_ref.dtype)
            dg_ref[...] += jnp.sum(dy * xhat, axis=0, keepdims=True)
        o_ref[:, ql + kl:] = dkr_ref[...].astype(o_ref.dtype)

    def row(n):
        return pl.BlockSpec((tile, n), lambda i: (i, 0))

    def vec(n):
        return pl.BlockSpec((1, n), lambda i: (0, 0))

    return pl.pallas_call(
        body, name=name, grid=(t // tile,),
        in_specs=[row(wd), vec(ql), vec(kl), row(ql), row(kl), row(wd - ql - kl)],
        out_specs=(row(wd), vec(ql), vec(kl)),
        out_shape=(jax.ShapeDtypeStruct((t, wd), BF16), jax.ShapeDtypeStruct((1, ql), F32),
                   jax.ShapeDtypeStruct((1, kl), F32)),
        compiler_params=_params("arbitrary"),
    )(down, g_qa, g_kva, dcqn, dckvn, dkr)


def _qk_specs(geo, xr_col, shared_rope):
    tile = geo.mm_tile
    xn_spec = pl.BlockSpec((tile, HEADS * LANE), lambda i: (i, 0))
    if shared_rope:
        xr_spec = pl.BlockSpec((tile, LANE), lambda i: (i, xr_col))
    else:
        xr_spec = pl.BlockSpec((tile, HEADS * LANE), lambda i: (i, xr_col // HEADS))
    vec = pl.BlockSpec((1, LANE), lambda i: (0, 0))
    tab = pl.BlockSpec((tile, LANE), lambda i: (i, 0))
    return tile, xn_spec, xr_spec, vec, tab


def _qk_norm(xn, xr):
    ss = jnp.sum(xn * xn, axis=-1, keepdims=True) + jnp.sum(xr * xr, axis=-1, keepdims=True)
    return lax.rsqrt(ss * (1.0 / QK_HEAD) + EPS)


def _head_lanes(ref, hh, shared=False):
    return ref[...] if shared else ref[:, hh * LANE:(hh + 1) * LANE]


def _qk_fwd(xn_arr, xr_arr, xr_col, shared_rope, gn, gr, cos, sin, geo, name):
    t = xn_arr.shape[0]
    tile, xn_spec, xr_spec, vec, tab = _qk_specs(geo, xr_col, shared_rope)

    def body(xn_ref, xr_ref, gn_ref, gr_ref, c_ref, s_ref, o_ref):
        cv, sv, gnv, grv = c_ref[...], s_ref[...], gn_ref[...], gr_ref[...]
        for hh in range(HEADS):
            xn = _head_lanes(xn_ref, hh).astype(F32)
            xr = _head_lanes(xr_ref, hh, shared_rope).astype(F32)
            r = _qk_norm(xn, xr)
            yr = xr * r * grv
            o_ref[:, hh * HEAD_PAD:hh * HEAD_PAD + LANE] = (xn * r * gnv).astype(o_ref.dtype)
            o_ref[:, hh * HEAD_PAD + LANE:(hh + 1) * HEAD_PAD] = (yr * cv + _swap_halves(yr) * sv).astype(o_ref.dtype)

    return pl.pallas_call(
        body, name=name, grid=(t // tile,), in_specs=[xn_spec, xr_spec, vec, vec, tab, tab],
        out_specs=pl.BlockSpec((tile, HEADS * HEAD_PAD), lambda i: (i, 0)),
        out_shape=jax.ShapeDtypeStruct((t, HEADS * HEAD_PAD), BF16), compiler_params=_params("parallel"),
    )(xn_arr, xr_arr, gn, gr, cos, sin)


def _qk_bwd(xn_arr, xr_arr, xr_col, shared_rope, gn, gr, cos, sin, dout, geo, name):
    t = xn_arr.shape[0]
    tile, xn_spec, xr_spec, vec, tab = _qk_specs(geo, xr_col, shared_rope)

    def body(xn_ref, xr_ref, gn_ref, gr_ref, c_ref, s_ref, d_ref, dxn_ref, dxr_ref, dgn_ref, dgr_ref):
        i = pl.program_id(0)
        cv, sv, gnv, grv = c_ref[...], s_ref[...], gn_ref[...], gr_ref[...]
        dgn = jnp.zeros((1, LANE), F32)
        dgr = jnp.zeros((1, LANE), F32)
        dxr_sum = jnp.zeros((tile, LANE), F32)
        for hh in range(HEADS):
            xn = _head_lanes(xn_ref, hh).astype(F32)
            xr = _head_lanes(xr_ref, hh, shared_rope).astype(F32)
            r = _qk_norm(xn, xr)
            xhn, xhr = xn * r, xr * r
            dyn = d_ref[:, hh * HEAD_PAD:hh * HEAD_PAD + LANE].astype(F32)
            dro = d_ref[:, hh * HEAD_PAD + LANE:(hh + 1) * HEAD_PAD].astype(F32)
            dyr = dro * cv + _swap_halves(dro * sv)
            un, ur = dyn * gnv, dyr * grv
            mean = (jnp.sum(un * xhn, axis=-1, keepdims=True) + jnp.sum(ur * xhr, axis=-1, keepdims=True)) * (1.0 / QK_HEAD)
            dxn_ref[:, hh * LANE:(hh + 1) * LANE] = (r * (un - xhn * mean)).astype(dxn_ref.dtype)
            dxr = r * (ur - xhr * mean)
            if shared_rope:
                dxr_sum = dxr_sum + dxr
            else:
                dxr_ref[:, hh * LANE:(hh + 1) * LANE] = dxr.astype(dxr_ref.dtype)
            dgn = dgn + jnp.sum(dyn * xhn, axis=0, keepdims=True)
            dgr = dgr + jnp.sum(dyr * xhr, axis=0, keepdims=True)
        if shared_rope:
            dxr_ref[...] = dxr_sum

        @pl.when(i == 0)
        def _():
            dgn_ref[...] = jnp.zeros_like(dgn_ref)
            dgr_ref[...] = jnp.zeros_like(dgr_ref)

        dgn_ref[...] += dgn
        dgr_ref[...] += dgr

    heads = pl.BlockSpec((tile, HEADS * LANE), lambda i: (i, 0))
    if shared_rope:
        dxr_spec, dxr_shape = pl.BlockSpec((tile, LANE), lambda i: (i, 0)), jax.ShapeDtypeStruct((t, LANE), F32)
    else:
        dxr_spec, dxr_shape = heads, jax.ShapeDtypeStruct((t, HEADS * LANE), BF16)
    return pl.pallas_call(
        body, name=name, grid=(t // tile,),
        in_specs=[xn_spec, xr_spec, vec, vec, tab, tab, pl.BlockSpec((tile, HEADS * HEAD_PAD), lambda i: (i, 0))],
        out_specs=(heads, dxr_spec, vec, vec),
        out_shape=(jax.ShapeDtypeStruct((t, HEADS * LANE), BF16), dxr_shape, jax.ShapeDtypeStruct((1, LANE), F32),
                   jax.ShapeDtypeStruct((1, LANE), F32)),
        compiler_params=_params("arbitrary"),
    )(xn_arr, xr_arr, gn, gr, cos, sin, dout)


def _attn_specs(geo):
    tq, nq = geo.n_ctx, geo.n_lat // geo.n_ctx

    def qrow(b, i):
        return jnp.where(i < nq, b * nq + i, 2 * nq + b)

    q_spec = pl.BlockSpec((tq, HEAD_PAD), lambda b, hh, i: (qrow(b, i), hh))
    kc_spec = pl.BlockSpec((geo.n_ctx, HEAD_PAD), lambda b, hh, i: (2 * nq + b, hh))
    kl_spec = pl.BlockSpec((geo.n_lat, HEAD_PAD), lambda b, hh, i: (b, hh))
    vc_spec = pl.BlockSpec((geo.n_ctx, V_HEAD), lambda b, hh, i: (2 * nq + b, HEADS + hh))
    vl_spec = pl.BlockSpec((geo.n_lat, V_HEAD), lambda b, hh, i: (b, HEADS + hh))
    o_spec = pl.BlockSpec((tq, V_HEAD), lambda b, hh, i: (qrow(b, i), hh))
    return tq, nq, q_spec, kc_spec, kl_spec, vc_spec, vl_spec, o_spec


def _attn_fwd(q, k, kv, with_ctx_q, geo, name, hosted=None):
    t = q.shape[0]
    tq, nq, q_spec, kc_spec, kl_spec, vc_spec, vl_spec, o_spec = _attn_specs(geo)

    def body(q_ref, kc_ref, kl_ref, vc_ref, vl_ref, o_ref):
        i = pl.program_id(2)
        qv = q_ref[...]
        s_c = _dot(qv, kc_ref[...], _NT) * QK_SCALE

        @pl.when(i < nq)
        def _():
            s_l = _dot(qv, kl_ref[...], _NT) * QK_SCALE
            m = jnp.maximum(jnp.max(s_c, axis=-1, keepdims=True), jnp.max(s_l, axis=-1, keepdims=True))
            p_c, p_l = jnp.exp(s_c - m), jnp.exp(s_l - m)
            den = jnp.sum(p_c, axis=-1, keepdims=True) + jnp.sum(p_l, axis=-1, keepdims=True)
            o = _dot(p_c, vc_ref[...], _NN) + _dot(p_l, vl_ref[...], _NN)
            o_ref[...] = (o / den).astype(o_ref.dtype)

        @pl.when(i == nq)
        def _():
            if with_ctx_q:
                m = jnp.max(s_c, axis=-1, keepdims=True)
                p_c = jnp.exp(s_c - m)
                o = _dot(p_c, vc_ref[...], _NN) / jnp.sum(p_c, axis=-1, keepdims=True)
                o_ref[...] = o.astype(o_ref.dtype)
            else:
                o_ref[...] = jnp.zeros_like(o_ref)

    return _call(
        body, hosted, name=name, grid=(2, HEADS, nq + 1), in_specs=[q_spec, kc_spec, kl_spec, vc_spec, vl_spec],
        out_specs=o_spec, out_shape=jax.ShapeDtypeStruct((t, HEADS * V_HEAD), BF16),
        compiler_params=_params("parallel", "parallel", "arbitrary"),
    )(q, k, k, kv, kv)


def _attn_bwd(q, k, kv, do, with_ctx_q, geo, name, hosted=None):
    t = q.shape[0]
    tq, nq, q_spec, kc_spec, kl_spec, vc_spec, vl_spec, o_spec = _attn_specs(geo)

    def body(q_ref, kc_ref, kl_ref, vc_ref, vl_ref, do_ref, dq_ref, dkl_ref, dkc_ref, dvl_ref, dvc_ref,
             akl_ref, akc_ref, avl_ref, avc_ref):
        i = pl.program_id(2)

        @pl.when(i == 0)
        def _():
            for ref in (akl_ref, akc_ref, avl_ref, avc_ref):
                ref[...] = jnp.zeros_like(ref)

        qv, dov = q_ref[...], do_ref[...]
        s_c = _dot(qv, kc_ref[...], _NT) * QK_SCALE
        dp_c = _dot(dov, vc_ref[...], _NT)

        def ctx_part(p_c, delta):
            ds_c = (p_c * (dp_c - delta) * QK_SCALE).astype(BF16)
            akc_ref[...] += _dot(ds_c, qv, _TN)
            avc_ref[...] += _dot(p_c, dov, _TN)
            return _dot(ds_c, kc_ref[...], _NN)

        @pl.when(i < nq)
        def _():
            s_l = _dot(qv, kl_ref[...], _NT) * QK_SCALE
            m = jnp.maximum(jnp.max(s_c, axis=-1, keepdims=True), jnp.max(s_l, axis=-1, keepdims=True))
            p_c, p_l = jnp.exp(s_c - m), jnp.exp(s_l - m)
            inv = 1.0 / (jnp.sum(p_c, axis=-1, keepdims=True) + jnp.sum(p_l, axis=-1, keepdims=True))
            p_c, p_l = p_c * inv, p_l * inv
            dp_l = _dot(dov, vl_ref[...], _NT)
            delta = jnp.sum(p_c * dp_c, axis=-1, keepdims=True) + jnp.sum(p_l * dp_l, axis=-1, keepdims=True)
            ds_l = (p_l * (dp_l - delta) * QK_SCALE).astype(BF16)
            akl_ref[...] += _dot(ds_l, qv, _TN)
            avl_ref[...] += _dot(p_l, dov, _TN)
            dq_ref[...] = (ctx_part(p_c, delta) + _dot(ds_l, kl_ref[...], _NN)).astype(dq_ref.dtype)

        @pl.when(i == nq)
        def _():
            if with_ctx_q:
                m = jnp.max(s_c, axis=-1, keepdims=True)
                p_c = jnp.exp(s_c - m)
                p_c = p_c * (1.0 / jnp.sum(p_c, axis=-1, keepdims=True))
                delta = jnp.sum(p_c * dp_c, axis=-1, keepdims=True)
                dq_ref[...] = ctx_part(p_c, delta).astype(dq_ref.dtype)
            else:
                dq_ref[...] = jnp.zeros_like(dq_ref)
            dkl_ref[...] = akl_ref[...].astype(dkl_ref.dtype)
            dkc_ref[...] = akc_ref[...].astype(dkc_ref.dtype)
            dvl_ref[...] = avl_ref[...].astype(dvl_ref.dtype)
            dvc_ref[...] = avc_ref[...].astype(dvc_ref.dtype)

    def acc_spec(rows, width):
        return pl.BlockSpec((rows, width), lambda b, hh, i: (b, hh))

    return _call(
        body, hosted, name=name, grid=(2, HEADS, nq + 1), in_specs=[q_spec, kc_spec, kl_spec, vc_spec, vl_spec, o_spec],
        out_specs=(q_spec, acc_spec(geo.n_lat, HEAD_PAD), acc_spec(geo.n_ctx, HEAD_PAD), acc_spec(geo.n_lat, V_HEAD),
                   acc_spec(geo.n_ctx, V_HEAD)),
        out_shape=(jax.ShapeDtypeStruct((t, HEADS * HEAD_PAD), BF16),
                   jax.ShapeDtypeStruct((2 * geo.n_lat, HEADS * HEAD_PAD), BF16),
                   jax.ShapeDtypeStruct((2 * geo.n_ctx, HEADS * HEAD_PAD), BF16),
                   jax.ShapeDtypeStruct((2 * geo.n_lat, HEADS * V_HEAD), BF16),
                   jax.ShapeDtypeStruct((2 * geo.n_ctx, HEADS * V_HEAD), BF16)),
        scratch_shapes=[pltpu.VMEM((geo.n_lat, HEAD_PAD), F32), pltpu.VMEM((geo.n_ctx, HEAD_PAD), F32),
                        pltpu.VMEM((geo.n_lat, V_HEAD), F32), pltpu.VMEM((geo.n_ctx, V_HEAD), F32)],
        compiler_params=_params("parallel", "parallel", "arbitrary"),
    )(q, k, k, kv, kv, do)


def _mla_fwd(h, g, mod4, w, with_ctx_q, tabs, geo, tag, hosts, got):
    cos, sin = tabs
    ql, kl = w["g_qa"].shape[1], w["g_kva"].shape[1]
    kr_col = (ql + kl) // LANE
    nx = _pre_fwd(h, g, mod4, 3, geo, f"{tag}_pre")
    down = _mm(nx, w["w_a"], name=f"{tag}_down")
    cqn, ckvn = _latent_norm_fwd(down, w["g_qa"], w["g_kva"], geo, f"{tag}_lnorm")
    qraw = _mm(cqn, w["w_uq"], out_dtype=BF16, name=f"{tag}_uq")
    kvraw = _mm(ckvn, w["w_ukv"], out_dtype=BF16, name=f"{tag}_ukv")
    q = _qk_fwd(qraw, qraw, HEADS, False, w["gq_n"], w["gq_r"], cos, sin, geo, f"{tag}_qnorm")
    k = _qk_fwd(kvraw, down, kr_col, True, w["gk_n"], w["gk_r"], cos, sin, geo, f"{tag}_knorm")
    o = _with_host(_attn_fwd, hosts, got, "mix_a", q, k, kvraw, with_ctx_q, geo, f"{tag}_attn")
    h_out, y = _mm(o, w["w_o"], name=f"{tag}_o", gate=(h, mod4, 5, 1.0, geo))
    return h_out, (h, nx, down, cqn, ckvn, qraw, kvraw, q, k, o, y)


def _mla_bwd(dh_out, saved, g, mod4, w, with_ctx_q, tabs, geo, tag, hosts, got):
    cos, sin = tabs
    h, nx, down, cqn, ckvn, qraw, kvraw, q, k, o, y = saved
    ql, kl = w["g_qa"].shape[1], w["g_kva"].shape[1]
    kr_col = (ql + kl) // LANE
    dy, dgate = _gate_bwd(dh_out, y, mod4, 5, 1.0, geo, f"{tag}_dgate")
    do = _mm(dy, w["w_o"], tb=True, out_dtype=BF16, name=f"{tag}_do")
    dw_o = _tn_wide(o, dy, f"{tag}_dwo")
    dq, dk_lat, dk_ctx, dv_lat, dv_ctx = _with_host(_attn_bwd, hosts, got, "mix_a", q, k, kvraw, do, with_ctx_q, geo,
                                                    f"{tag}_dattn")
    dk = jnp.concatenate([dk_lat, dk_ctx], axis=0)
    dqn, dqr, dgq_n, dgq_r = _qk_bwd(qraw, qraw, HEADS, False, w["gq_n"], w["gq_r"], cos, sin, dq, geo, f"{tag}_dqnorm")
    dkn, dkr, dgk_n, dgk_r = _qk_bwd(kvraw, down, kr_col, True, w["gk_n"], w["gk_r"], cos, sin, dk, geo, f"{tag}_dknorm")
    dqraw = jnp.concatenate([dqn, dqr], axis=1)
    dkvraw = jnp.concatenate([dkn, jnp.concatenate([dv_lat, dv_ctx], axis=0)], axis=1)
    dcqn = _mm(dqraw, w["w_uq"], tb=True, out_dtype=BF16, name=f"{tag}_dcqn")
    dw_uq = _tn_wide(cqn, dqraw, f"{tag}_dwuq")
    dckvn = _mm(dkvraw, w["w_ukv"], tb=True, out_dtype=BF16, name=f"{tag}_dckvn")
    dw_ukv = _tn_wide(ckvn, dkvraw, f"{tag}_dwukv")
    ddown, dg_qa, dg_kva = _latent_norm_bwd(down, w["g_qa"], w["g_kva"], dcqn, dckvn, dkr, geo, f"{tag}_dlnorm")
    dnx = _mm(ddown, w["w_a"], tb=True, name=f"{tag}_dnx")
    dw_a = _tn_wide(nx, ddown, f"{tag}_dwa")
    dh, dg, dshift, dscale = _pre_bwd(h, g, mod4, 3, dnx, dh_out, geo, f"{tag}_dpre")
    grads = dict(w_a=dw_a, g_qa=dg_qa, w_uq=dw_uq, g_kva=dg_kva, w_ukv=dw_ukv, gq_n=dgq_n, gq_r=dgq_r, gk_n=dgk_n,
                 gk_r=dgk_r, w_o=dw_o)
    return dh, dg, (dshift, dscale, dgate), grads


def _mla_prepare(w_a, g_qa, w_uq, g_kva, w_ukv, g_q, g_k, w_o):
    ql, kl = g_qa.shape[0], g_kva.shape[0]
    d = w_a.shape[0]
    w_a_pad = jnp.concatenate([w_a[:, :ql + kl], _rope_swap(w_a[:, ql + kl:]), jnp.zeros((d, LANE - QK_ROPE), w_a.dtype)], axis=1)
    uq = w_uq.reshape(ql, HEADS, QK_HEAD)
    uq_r = jnp.pad(_rope_swap(uq[:, :, QK_NOPE:]), ((0, 0), (0, 0), (0, LANE - QK_ROPE)))
    w_uq_pad = jnp.concatenate([uq[:, :, :QK_NOPE].reshape(ql, HEADS * LANE), uq_r.reshape(ql, HEADS * LANE)], axis=1)
    ukv = w_ukv.reshape(kl, HEADS, QK_NOPE + V_HEAD)
    w_ukv_p = jnp.concatenate([ukv[:, :, :QK_NOPE].reshape(kl, HEADS * LANE), ukv[:, :, QK_NOPE:].reshape(kl, HEADS * V_HEAD)], axis=1)

    def gains(gv):
        gv = gv.astype(F32)
        return gv[None, :QK_NOPE], jnp.pad(_rope_swap(gv[QK_NOPE:]), (0, LANE - QK_ROPE))[None]

    gq_n, gq_r = gains(g_q)
    gk_n, gk_r = gains(g_k)
    return dict(w_a=w_a_pad, g_qa=g_qa.astype(F32)[None], w_uq=w_uq_pad, g_kva=g_kva.astype(F32)[None], w_ukv=w_ukv_p,
                gq_n=gq_n, gq_r=gq_r, gk_n=gk_n, gk_r=gk_r, w_o=w_o)


def _mla_unprepare(gr):
    ql, kl = gr["g_qa"].shape[1], gr["g_kva"].shape[1]
    dw_a = jnp.concatenate([gr["w_a"][:, :ql + kl], _rope_swap(gr["w_a"][:, ql + kl:ql + kl + QK_ROPE])], axis=1)
    uqn = gr["w_uq"][:, :HEADS * LANE].reshape(ql, HEADS, LANE)
    uqr = _rope_swap(gr["w_uq"][:, HEADS * LANE:].reshape(ql, HEADS, LANE)[:, :, :QK_ROPE])
    dw_uq = jnp.concatenate([uqn, uqr], axis=2).reshape(ql, HEADS * QK_HEAD)
    ukn = gr["w_ukv"][:, :HEADS * LANE].reshape(kl, HEADS, LANE)
    ukv = gr["w_ukv"][:, HEADS * LANE:].reshape(kl, HEADS, V_HEAD)
    dw_ukv = jnp.concatenate([ukn, ukv], axis=2).reshape(kl, HEADS * (QK_NOPE + V_HEAD))

    def gains(gn, grr):
        return jnp.concatenate([gn[0], _rope_swap(grr[0, :QK_ROPE])])

    return dict(mla_w_a=dw_a, mla_g_qa=gr["g_qa"][0], mla_w_uq=dw_uq, mla_g_kva=gr["g_kva"][0], mla_w_ukv=dw_ukv,
                mla_g_q=gains(gr["gq_n"], gr["gq_r"]), mla_g_k=gains(gr["gk_n"], gr["gk_r"]), mla_w_o=gr["w_o"])


def _loss_head(h, target, geo, name):
    t, d = h.shape
    tile = geo.tile
    n_lat_tiles = 2 * geo.n_lat // tile

    def body(h_ref, t_ref, dh_ref, loss_ref):
        i = pl.program_id(0)

        @pl.when(i == 0)
        def _():
            loss_ref[...] = jnp.zeros_like(loss_ref)

        @pl.when(i < n_lat_tiles)
        def _():
            e = h_ref[...] - t_ref[...]
            dh_ref[...] = e * (1.0 / d)
            part = jnp.sum(e * e, axis=0, keepdims=True) * (0.5 / d)
            loss_ref[...] += sum(part[:, j * LANE:(j + 1) * LANE] for j in range(d // LANE))

        @pl.when(i >= n_lat_tiles)
        def _():
            dh_ref[...] = jnp.zeros_like(dh_ref)

    row = pl.BlockSpec((tile, d), lambda i: (i, 0))
    tgt = pl.BlockSpec((tile, d), lambda i: (jnp.minimum(i, n_lat_tiles - 1), 0))
    dh, loss = pl.pallas_call(
        body, name=name, grid=(t // tile,), in_specs=[row, tgt],
        out_specs=(row, pl.BlockSpec((1, LANE), lambda i: (0, 0))),
        out_shape=(jax.ShapeDtypeStruct((t, d), F32), jax.ShapeDtypeStruct((1, LANE), F32)),
        compiler_params=_params("arbitrary"),
    )(h, target)
    return jnp.sum(loss), dh


def _adamw(w, g, m, v, name):
    shape = w.shape
    cols = shape[-1]
    rows = int(np.prod(shape[:-1])) if len(shape) > 1 else 1
    w2, g2, m2, v2 = (a.reshape(rows, cols) for a in (w, g, m, v))
    tr = _pick(rows, (512, 256, 128, 64, 32, 16, 8))
    c1 = 1.0 / (1.0 - ADAM_B1 ** ADAM_STEP)
    c2 = 1.0 / (1.0 - ADAM_B2 ** ADAM_STEP)

    def body(w_ref, g_ref, m_ref, v_ref, d_ref, mo_ref, vo_ref):
        gv = g_ref[...]
        mn = ADAM_B1 * m_ref[...] + (1.0 - ADAM_B1) * gv
        vn = ADAM_B2 * v_ref[...] + (1.0 - ADAM_B2) * (gv * gv)
        d_ref[...] = -ADAM_LR * ((mn * c1) / (jnp.sqrt(vn * c2) + ADAM_EPS) + ADAM_WD * w_ref[...])
        mo_ref[...] = mn
        vo_ref[...] = vn

    blk = pl.BlockSpec((tr, cols), lambda i: (i, 0))
    sds = jax.ShapeDtypeStruct((rows, cols), F32)
    d, mo, vo = pl.pallas_call(
        body, name=name, grid=(rows // tr,), in_specs=[blk] * 4, out_specs=(blk,) * 3, out_shape=(sds,) * 3,
        compiler_params=_params("parallel"),
    )(w2, g2, m2, v2)
    return d.reshape(shape), mo.reshape(shape), vo.reshape(shape)


SHARD_AXIS = {
    "w_mod": 2, "g_norm": 2, "ffn_w1": 3, "ffn_w3": 3, "ffn_w2": 2, "sc_w_in": 2, "sc_conv": 2, "sc_w_out": 1,
    "mla_w_a": 1, "mla_g_qa": 1, "mla_w_uq": 2, "mla_w_ukv": 2, "mla_w_o": 1,
}
HIDDEN_MAJOR = ("ffn_w1", "ffn_w3")


def _view(name, arr, swapped=False):
    form, swap, _ = EXCHANGE[name]
    if swap and not swapped:
        arr = jnp.swapaxes(arr, -1, -2)
    if form == "mid":
        arr = arr.reshape((-1,) + arr.shape[-2:])
        return jnp.pad(arr, ((0, 0), (0, 0), (0, -arr.shape[-1] % LANE)))
    arr = arr.reshape(-1, arr.shape[-1])
    return jnp.pad(arr, ((0, -arr.shape[0] % 16), (0, 0)))


def _unview(name, view, shape, keep_swapped=False):
    form, swap, _ = EXCHANGE[name]
    shape = shape[:-2] + (shape[-1], shape[-2]) if swap else shape
    if form == "mid":
        view = view[:, :, :shape[-1]]
    else:
        view = view[:int(np.prod(shape[:-1]))]
    arr = view.reshape(shape)
    return arr if (not swap or keep_swapped) else jnp.swapaxes(arr, -1, -2)


def _full_shape(name, local_shape):
    ax = SHARD_AXIS[name]
    return local_shape[:ax] + (N_DEV * local_shape[ax],) + local_shape[ax + 1:]


def _win(ref, form, n, j):
    start = j * n
    if not isinstance(start, int):
        start = pl.multiple_of(start, LANE if form == "last" else math.gcd(n, 16))
    if form == "mid":
        return ref.at[:, pl.ds(start, n), :]
    return ref.at[:, pl.ds(start, n)]


def _windows(view, count, of):
    return view.shape[:1] + (view.shape[1] * count // of,) + view.shape[2:]


def _gather_work(views, forms):
    na = len(views)

    def plan(x_refs, out_refs, sems):
        send_sems, recv_sems, local_sems = sems
        x, y, c = lax.axis_index("x"), lax.axis_index("y"), lax.axis_index("c")
        me, sibling = (x, y, c), (x, y, 1 - c)
        chips = [(1 - x, y), (x, 1 - y), (1 - x, 1 - y)]

        def copy(a, k, block, to, from_input):
            dst = _win(out_refs[a], forms[a], views[a].shape[1], 4 * block[0] + 2 * block[1] + block[2])
            return pltpu.make_async_remote_copy(
                src_ref=x_refs[a] if from_input else dst, dst_ref=dst, send_sem=send_sems.at[a, k],
                recv_sem=recv_sems.at[a, k], device_id=to, device_id_type=MESH)

        mine = [pltpu.make_async_copy(x_refs[a], _win(out_refs[a], forms[a], views[a].shape[1], 4 * x + 2 * y + c),
                                      local_sems.at[a]) for a in range(na)]
        first = []
        for a in range(na):
            first.append(copy(a, 0, me, sibling, True))
            first += [copy(a, 1 + j, me, (*chip, c), True) for j, chip in enumerate(chips)]
        return copy, mine, first, me, sibling, chips, c

    def start(x_refs, out_refs, sems):
        _, mine, first, *_ = plan(x_refs, out_refs, sems)
        for cp in mine + first:
            cp.start()

    def finish(x_refs, out_refs, sems):
        copy, mine, first, me, sibling, chips, c = plan(x_refs, out_refs, sems)
        passed = []
        for j, chip in enumerate(chips):
            for a in range(na):
                copy(a, 1 + j, (*chip, c), me, False).wait_recv()
                fwd = copy(a, 4 + j, (*chip, c), sibling, False)
                fwd.start()
                passed.append(fwd)
        for a in range(na):
            copy(a, 0, sibling, me, False).wait_recv()
            for j, chip in enumerate(chips):
                copy(a, 4 + j, (*chip, 1 - c), me, False).wait_recv()
        for cp in first + passed:
            cp.wait_send()
        for cp in mine:
            cp.wait()

    return Hosted(
        list(views), [jax.ShapeDtypeStruct(_windows(v, N_DEV, 1), v.dtype) for v in views],
        [pltpu.SemaphoreType.DMA((na, 7)), pltpu.SemaphoreType.DMA((na, 7)), pltpu.SemaphoreType.DMA((na,))], start, finish)


def _push_work(srcs, out_shapes, n_copies, make_copies):
    na = len(srcs)

    def start(s_refs, r_refs, sems):
        for cp in make_copies(s_refs, r_refs, sems[0], sems[1]):
            cp.start()

    def finish(s_refs, r_refs, sems):
        copies = make_copies(s_refs, r_refs, sems[0], sems[1])
        for cp in copies:
            cp.wait_recv()
        for cp in copies:
            cp.wait_send()

    return Hosted(list(srcs), out_shapes, [pltpu.SemaphoreType.DMA((na, n_copies)), pltpu.SemaphoreType.DMA((na, n_copies))],
                  start, finish)


def _sibling_work(fulls, forms):
    na = len(fulls)
    widths = [f.shape[1] // N_DEV for f in fulls]

    def make_copies(g_refs, r_refs, send_sems, recv_sems):
        x, y, c = lax.axis_index("x"), lax.axis_index("y"), lax.axis_index("c")
        return [
            pltpu.make_async_remote_copy(
                src_ref=_win(g_refs[a], forms[a], widths[a], 2 * chip + (1 - c)),
                dst_ref=_win(r_refs[a], forms[a], widths[a], chip), send_sem=send_sems.at[a, chip],
                recv_sem=recv_sems.at[a, chip], device_id=(x, y, 1 - c), device_id_type=MESH)
            for a in range(na) for chip in range(N_CHIP)
        ]

    return _push_work(fulls, [jax.ShapeDtypeStruct(_windows(f, N_CHIP, N_DEV), f.dtype) for f in fulls], N_CHIP, make_copies)


def _chip_work(parts, forms):
    na = len(parts)
    widths = [p.shape[1] // N_CHIP for p in parts]

    def make_copies(p_refs, r_refs, send_sems, recv_sems):
        x, y, c = lax.axis_index("x"), lax.axis_index("y"), lax.axis_index("c")
        chips = [(1 - x, y), (x, 1 - y), (1 - x, 1 - y)]
        return [
            pltpu.make_async_remote_copy(
                src_ref=_win(p_refs[a], forms[a], widths[a], 2 * px + py), dst_ref=_win(r_refs[a], forms[a], widths[a], j),
                send_sem=send_sems.at[a, j], recv_sem=recv_sems.at[a, j], device_id=(px, py, c), device_id_type=MESH)
            for a in range(na) for j, (px, py) in enumerate(chips)
        ]

    return _push_work(parts, [jax.ShapeDtypeStruct(_windows(p, 3, N_CHIP), p.dtype) for p in parts], 3, make_copies)


def _sum_tiles(view, form, n):
    if form == "mid":
        tr = n
        while tr * view.shape[2] * 4 > 2 * 1024 * 1024 and tr % 32 == 0:
            tr //= 2
        return 1, tr
    return _pick(view.shape[0], (512, 256, 128, 64, 32, 16)), n


def _window_spec(form, tl, tr, rest, window_of):
    if form == "mid":
        return lambda per: pl.BlockSpec((None, tr) + rest, lambda l, k, i, s: (l, window_of(k, s) * per + i, 0))
    return lambda per: pl.BlockSpec((tl, tr), lambda l, k, i, s: (l, window_of(k, s)))


def _chip_partials(g, recv, core, form, name):
    n = g.shape[1] // N_DEV
    tl, tr = _sum_tiles(g, form, n)
    per = n // tr
    rest = tuple(g.shape[2:])

    def body(core_ref, g_ref, r_ref, o_ref):
        o_ref[...] = (g_ref[...] + r_ref[...]).astype(o_ref.dtype)

    own = _window_spec(form, tl, tr, rest, lambda k, s: 2 * k + s[0])(per)
    by_chip = _window_spec(form, tl, tr, rest, lambda k, s: k)(per)
    return pl.pallas_call(
        body, name=name,
        grid_spec=pltpu.PrefetchScalarGridSpec(
            num_scalar_prefetch=1, grid=(g.shape[0] // tl, N_CHIP, per), in_specs=[own, by_chip], out_specs=by_chip),
        out_shape=jax.ShapeDtypeStruct(recv.shape, BF16), compiler_params=_params("parallel", "parallel", "parallel"),
    )(core, g, recv)


def _reduce_final(p, recv, chip, form, name):
    n = p.shape[1] // N_CHIP
    tl, tr = _sum_tiles(p, form, n)
    per = n // tr
    rest = tuple(p.shape[2:])

    def body(chip_ref, p_ref, ry_ref, rx_ref, rxy_ref, o_ref):
        own_pair = p_ref[...].astype(F32) + ry_ref[...].astype(F32)
        o_ref[...] = own_pair + (rx_ref[...].astype(F32) + rxy_ref[...].astype(F32))

    def rel(j):
        return _window_spec(form, tl, tr, rest, lambda k, s: j)(per)

    own = _window_spec(form, tl, tr, rest, lambda k, s: s[0])(per)
    return pl.pallas_call(
        body, name=name,
        grid_spec=pltpu.PrefetchScalarGridSpec(
            num_scalar_prefetch=1, grid=(p.shape[0] // tl, 1, per), in_specs=[own, rel(1), rel(0), rel(2)],
            out_specs=rel(0)),
        out_shape=jax.ShapeDtypeStruct(p.shape[:1] + (n,) + p.shape[2:], F32),
        compiler_params=_params("parallel", "parallel", "parallel"),
    )(chip, p, recv, recv, recv)


def _pack_replicated(arrays):
    pieces = []
    for a in arrays:
        flat = a.reshape(-1).astype(F32)
        pieces.append(jnp.pad(flat, (0, -flat.size % LANE)))
    total = sum(p.size for p in pieces)
    pieces.append(jnp.zeros((-total % (16 * LANE),), F32))
    return jnp.concatenate(pieces).reshape(-1, LANE)


def _unpack_replicated(buf, shapes):
    flat, out, off = buf.reshape(-1), [], 0
    for shape in shapes:
        size = int(np.prod(shape))
        out.append(flat[off:off + size].reshape(shape))
        off += size + (-size % LANE)
    return out


def _silu(v):
    return v * jax.nn.sigmoid(v)


SC_NAMES = ("sc_w_in", "sc_conv", "sc_w_out")
MLA_SHARDED = ("mla_w_a", "mla_g_qa", "mla_w_uq", "mla_w_ukv", "mla_w_o")
MLA_NAMES = ("mla_w_a", "mla_g_qa", "mla_w_uq", "mla_g_kva", "mla_w_ukv", "mla_g_q", "mla_g_k", "mla_w_o")


def _local_step(src, x, c, ctx, target):
    bsz, n_lat, d = x.shape
    n_ctx = ctx.shape[1]
    assert bsz == 2
    geo = Geo(n_lat, n_ctx)
    depth = src.depth
    tc = _conv_tile(d)
    tabs = _rope_tables(geo)

    h = jnp.concatenate([x.reshape(2 * n_lat, d), ctx.reshape(2 * n_ctx, d)], axis=0)
    tgt = target.reshape(2 * n_lat, d)
    cond = jnp.concatenate([c, src.c_ctx[None], jnp.zeros((8 - bsz - 1, d), F32)], axis=0)
    scond = _silu(cond)

    saved = []
    for i in range(depth):
        kind = i % 2
        wl, slots = src.weights(i), src.fwd_slots(i)
        gn = wl["g_norm"].astype(F32)
        mod = _mm(scond, wl["w_mod"], name=f"l{i}_mod") + wl["b_mod"][None]
        mod4 = mod[:N_SEG].reshape(N_SEG, N_MOD, 1, d)
        h, s1 = _ffn_fwd(h, gn[0:1], mod4, 0, wl, 0, geo, f"l{i}_f1", "f1", slots, slots)
        if kind == 0:
            mix = (_interleave(wl["sc_w_in"], 3, tc), wl["sc_conv"].astype(F32), wl["sc_w_out"])
            h, s2 = _sconv_fwd(h, gn[1:2], mod4, *mix, geo, f"l{i}_sc", slots, slots)
        else:
            mix = _mla_prepare(*[wl[name] for name in MLA_NAMES])
            h, s2 = _mla_fwd(h, gn[1:2], mod4, mix, i != depth - 1, tabs, geo, f"l{i}_mla", slots, slots)
        h, s3 = _ffn_fwd(h, gn[2:3], mod4, 6, wl, 1, geo, f"l{i}_f2", "f2", slots, slots)
        saved.append((wl, gn, mod4, mix, s1, s2, s3))

    loss, dh = _loss_head(h, tgt, geo, "loss_head")

    g_b_mod = [None] * depth
    dscond = jnp.zeros_like(scond)
    for i in reversed(range(depth)):
        kind = i % 2
        wl, gn, mod4, mix, s1, s2, s3 = saved[i]
        slots = src.bwd_slots(i)
        gbuf = {name: lax.empty(wl[name].shape, F32) for name in ("ffn_w1", "ffn_w3", "ffn_w2")}
        dh, dg2, dm2 = _ffn_bwd(dh, s3, gn[2:3], mod4, 6, wl, 1, gbuf, geo, f"l{i}_f2", "f2", slots, slots)
        if kind == 0:
            dh, dg1, dm1, dwin, dconv, dwout = _sconv_bwd(dh, s2, gn[1:2], mod4, *mix, geo, f"l{i}_sc", slots, slots)
            gl = dict(sc_w_in=_deinterleave(dwin, 3, tc), sc_conv=dconv, sc_w_out=dwout)
        else:
            dh, dg1, dm1, gm = _mla_bwd(dh, s2, gn[1:2], mod4, mix, i != depth - 1, tabs, geo, f"l{i}_mla", slots, slots)
            gl = _mla_unprepare(gm)
        dh, dg0, dm0 = _ffn_bwd(dh, s1, gn[0:1], mod4, 0, wl, 0, gbuf, geo, f"l{i}_f1", "f1", slots, slots)
        dmod = jnp.concatenate(list(dm0) + list(dm1) + list(dm2), axis=1).reshape(N_SEG, N_MOD * d)
        dmod8 = jnp.concatenate([dmod, jnp.zeros((8 - N_SEG, N_MOD * d), F32)], axis=0)
        g_b_mod[i] = jnp.sum(dmod, axis=0)
        gl.update(gbuf, g_norm=jnp.concatenate([dg0, dg1, dg2], axis=0),
                  w_mod=_mm(scond, dmod8, ta=True, name=f"l{i}_dwmod"))
        dscond = dscond + _mm(dmod8, wl["w_mod"], tb=True, name=f"l{i}_dcond")
        src.grads(i, gl)

    sg = jax.nn.sigmoid(cond)
    dcond = dscond * (sg * (1.0 + cond * (1.0 - sg)))
    grad_x = dh[:2 * n_lat].reshape(x.shape)
    return loss, grad_x, dcond[bsz], jnp.stack(g_b_mod)


class _Slots:
    def __init__(self, get, put):
        self.get, self._put = get, put

    def __setitem__(self, slot, outs):
        self._put(slot, outs)


FWD_PLAN = {
    0: {"f1_up": ("ffn_w1",), "f1_down": ("g_norm", "mix"), "mix_a": ("ffn_w3",), "mix_b": ("ffn_w2",), "f2_up": ("w_mod",)},
    1: {"f1_up": ("ffn_w1",), "mix_a": ("w_mod", "ffn_w3", "g_norm", "mix"), "f2_up": ("ffn_w2",)},
}
SIBLING_PLAN = {"f2_dact": ("ffn_w1", "w_mod", "g_norm", "mix"), "f2_dw2": ("ffn_w3", "ffn_w2")}
BWD_PLAN = {
    0: {"f2_dnx": ("ffn_w1",), "mix_a": ("w_mod",), "mix_b": ("ffn_w3",), "f1_dact": ("ffn_w2",), "f1_dnx": ("g_norm", "mix")},
    1: {"f2_dnx": ("ffn_w1",), "mix_a": ("w_mod", "ffn_w3", "ffn_w2"), "f1_dnx": ("g_norm", "mix")},
}


class _Exchange:
    def __init__(self, w):
        self.w = w
        self.depth = w["w_mod"].shape[0]
        self.c_ctx = w["c_ctx"]
        self.core = lax.axis_index("c").astype(jnp.int32).reshape(1)
        self.chip = (2 * lax.axis_index("x") + lax.axis_index("y")).astype(jnp.int32).reshape(1)
        self.full, self.gviews, self.parts, self.reduced, self.rep = {}, {}, {}, {}, {}

    def _layer_of(self, name, i):
        return i // 2 if name.startswith(("sc_", "mla_")) else i

    def _mixer(self, i):
        return SC_NAMES if i % 2 == 0 else MLA_SHARDED

    def _expand(self, names, i):
        out = []
        for name in names:
            out += list(self._mixer(i)) if name == "mix" else [name]
        return out

    def _group(self, i):
        return ["w_mod", "g_norm", "ffn_w1", "ffn_w3", "ffn_w2"] + list(self._mixer(i))

    def _local(self, name, i):
        arr = self.w[name][self._layer_of(name, i)]
        return arr[:, None] if name == "mla_g_qa" else arr

    def _shapes(self, name, i):
        local = tuple(self._local(name, i).shape)
        ax = SHARD_AXIS[name] - 1
        return local, local[:ax] + (N_DEV * local[ax],) + local[ax + 1:]

    def _gather(self, names, i):
        views = [_view(n, self._local(n, i).astype(BF16 if EXCHANGE[n][2] else F32)) for n in names]
        return _gather_work(views, [EXCHANGE[n][0] for n in names])

    def _gathered(self, names, i, outs):
        for name, fv in zip(names, outs):
            arr = _unview(name, fv, self._shapes(name, i)[1], keep_swapped=name in HIDDEN_MAJOR)
            self.full[name, i] = arr[:, 0] if name == "mla_g_qa" else arr

    def prefetch(self):
        names = self._group(0)
        self._gathered(names, 0, _run_hosted(self._gather(names, 0), "gather_l0"))

    def weights(self, i):
        wl = {name: self.full[name, i] for name in self._group(i)}
        wl["b_mod"] = self.w["b_mod"][i]
        if i % 2 == 1:
            for name in ("mla_g_kva", "mla_g_q", "mla_g_k"):
                wl[name] = self.w[name][i // 2]
        return wl

    def fwd_slots(self, i):
        plan = FWD_PLAN[i % 2] if i + 1 < self.depth else {}
        names = {slot: self._expand(plan[slot], i + 1) for slot in plan}
        return _Slots(lambda slot: self._gather(names[slot], i + 1) if slot in names else None,
                      lambda slot, outs: self._gathered(names[slot], i + 1, outs))

    def grads(self, i, gl):
        for name in self._group(i):
            g = gl[name][:, None] if name == "mla_g_qa" else gl[name]
            self.gviews[name, i] = _view(name, g, swapped=name in HIDDEN_MAJOR)
        for name in REPLICATED:
            if name in gl:
                self.rep[name, i // 2] = gl[name]

    def _forms(self, names):
        return [EXCHANGE[n][0] for n in names]

    def _partials(self, names, i, from_sibling):
        for name, recv in zip(names, from_sibling):
            self.parts[name, i] = _chip_partials(self.gviews[name, i], recv, self.core, EXCHANGE[name][0],
                                                 f"partial_{name}_{i}")

    def _finals(self, names, i, from_chips):
        for name, recv in zip(names, from_chips):
            rv = _reduce_final(self.parts[name, i], recv, self.chip, EXCHANGE[name][0], f"final_{name}_{i}")
            arr = _unview(name, rv, self._shapes(name, i)[0])
            self.reduced[name, i] = arr[:, 0] if name == "mla_g_qa" else arr

    def bwd_slots(self, i):
        if i + 1 >= self.depth:
            return _Slots(lambda slot: None, None)
        plan = BWD_PLAN[i % 2]
        names = {slot: self._expand(plan[slot], i + 1) for slot in plan}
        sibling = {slot: self._expand(SIBLING_PLAN[slot], i + 1) for slot in SIBLING_PLAN}

        def get(slot):
            if slot in sibling:
                return _sibling_work([self.gviews[n, i + 1] for n in sibling[slot]], self._forms(sibling[slot]))
            if slot in names:
                return _chip_work([self.parts[n, i + 1] for n in names[slot]], self._forms(names[slot]))
            return None

        def put(slot, outs):
            if slot in sibling:
                self._partials(sibling[slot], i + 1, outs)
            else:
                self._finals(names[slot], i + 1, outs)

        return _Slots(get, put)

    def finish(self, rep_grads):
        group = self._group(0)
        for name in REPLICATED:
            if name not in rep_grads:
                rep_grads[name] = jnp.stack([self.rep[name, j] for j in range(self.w[name].shape[0])])
        rep = _pack_replicated([rep_grads[name] for name in REPLICATED])
        views = [self.gviews[n, 0] for n in group] + [jnp.tile(rep[None], (1, N_DEV, 1))]
        forms = self._forms(group) + ["mid"]
        from_sibling = _run_hosted(_sibling_work(views, forms), "reduce_sibling_l0")
        self._partials(group, 0, from_sibling[:-1])
        rep_part = _chip_partials(views[-1], from_sibling[-1], self.core, "mid", "partial_replicated")
        parts = [self.parts[n, 0] for n in group] + [rep_part]
        from_chips = _run_hosted(_chip_work(parts, forms), "reduce_chips_l0")
        self._finals(group, 0, from_chips[:-1])
        rep_sum = _reduce_final(rep_part, from_chips[-1], self.chip, "mid", "final_replicated")
        out = dict(zip(REPLICATED, _unpack_replicated(rep_sum, [self.w[name].shape for name in REPLICATED])))
        for name in EXCHANGE:
            layers = range(self.w[name].shape[0])
            step = 2 if name.startswith(("sc_", "mla_")) else 1
            first = 1 if name.startswith("mla_") else 0
            out[name] = jnp.stack([self.reduced[name, first + step * l] for l in layers])
        return out


def kernel(x, c, ctx, c_ctx, w_mod, b_mod, g_norm, ffn_w1, ffn_w3, ffn_w2, sc_w_in, sc_conv, sc_w_out, mla_w_a, mla_g_qa, mla_w_uq, mla_g_kva, mla_w_ukv, mla_g_q, mla_g_k, mla_w_o, loss_target, m_c_ctx, m_w_mod, m_b_mod, m_g_norm, m_ffn_w1, m_ffn_w3, m_ffn_w2, m_sc_w_in, m_sc_conv, m_sc_w_out, m_mla_w_a, m_mla_g_qa, m_mla_w_uq, m_mla_g_kva, m_mla_w_ukv, m_mla_g_q, m_mla_g_k, m_mla_w_o, v_c_ctx, v_w_mod, v_b_mod, v_g_norm, v_ffn_w1, v_ffn_w3, v_ffn_w2, v_sc_w_in, v_sc_conv, v_sc_w_out, v_mla_w_a, v_mla_g_qa, v_mla_w_uq, v_mla_g_kva, v_mla_w_ukv, v_mla_g_q, v_mla_g_k, v_mla_w_o):
    w = dict(c_ctx=c_ctx, w_mod=w_mod, b_mod=b_mod, g_norm=g_norm, ffn_w1=ffn_w1, ffn_w3=ffn_w3, ffn_w2=ffn_w2,
             sc_w_in=sc_w_in, sc_conv=sc_conv, sc_w_out=sc_w_out, mla_w_a=mla_w_a, mla_g_qa=mla_g_qa, mla_w_uq=mla_w_uq,
             mla_g_kva=mla_g_kva, mla_w_ukv=mla_w_ukv, mla_g_q=mla_g_q, mla_g_k=mla_g_k, mla_w_o=mla_w_o)
    m = dict(c_ctx=m_c_ctx, w_mod=m_w_mod, b_mod=m_b_mod, g_norm=m_g_norm, ffn_w1=m_ffn_w1, ffn_w3=m_ffn_w3,
             ffn_w2=m_ffn_w2, sc_w_in=m_sc_w_in, sc_conv=m_sc_conv, sc_w_out=m_sc_w_out, mla_w_a=m_mla_w_a,
             mla_g_qa=m_mla_g_qa, mla_w_uq=m_mla_w_uq, mla_g_kva=m_mla_g_kva, mla_w_ukv=m_mla_w_ukv, mla_g_q=m_mla_g_q,
             mla_g_k=m_mla_g_k, mla_w_o=m_mla_w_o)
    v = dict(c_ctx=v_c_ctx, w_mod=v_w_mod, b_mod=v_b_mod, g_norm=v_g_norm, ffn_w1=v_ffn_w1, ffn_w3=v_ffn_w3,
             ffn_w2=v_ffn_w2, sc_w_in=v_sc_w_in, sc_conv=v_sc_conv, sc_w_out=v_sc_w_out, mla_w_a=v_mla_w_a,
             mla_g_qa=v_mla_g_qa, mla_w_uq=v_mla_w_uq, mla_g_kva=v_mla_g_kva, mla_w_ukv=v_mla_w_ukv, mla_g_q=v_mla_g_q,
             mla_g_k=v_mla_g_k, mla_w_o=v_mla_w_o)
    exchange = _Exchange(w)
    exchange.prefetch()
    loss, grad_x, g_c_ctx, g_b_mod = _local_step(exchange, x, c, ctx, loss_target)
    loss = lax.psum(loss, ("x", "y", "c"))
    reduced = exchange.finish(dict(c_ctx=g_c_ctx, b_mod=g_b_mod))

    outs = [[], [], [], []]
    for name in WEIGHTS:
        delta, new_m, new_v = _adamw(w[name], reduced[name], m[name], v[name], f"adamw_{name}")
        for lst, val in zip(outs, (reduced[name], delta, new_m, new_v)):
            lst.append(val)
    return (loss, grad_x, *outs[0], *outs[1], *outs[2], *outs[3])
```

```python
import functools
import math

import jax
import jax.numpy as jnp
import numpy as np
from jax import lax
from jax.experimental import pallas as pl
from jax.experimental.pallas import tpu as pltpu

F32 = jnp.float32
BF16 = jnp.bfloat16

N_MOD = 9
HEADS = 8
QK_NOPE = 128
QK_ROPE = 64
QK_HEAD = QK_NOPE + QK_ROPE
V_HEAD = 128
GRID_W = 64
ROPE_BASE = 10000.0
QK_SCALE = QK_HEAD ** -0.5
EPS = 1e-6
ADAM_LR, ADAM_B1, ADAM_B2, ADAM_EPS, ADAM_WD, ADAM_STEP = 0.001, 0.9, 0.999, 1e-08, 0.01, 10

N_DEV = 8
N_CHIP = 4
N_SEG = 3
LANE = 128
HEAD_PAD = 2 * LANE
VMEM_LIMIT_BYTES = 48 * 1024 * 1024
MESH = pl.DeviceIdType.MESH

WEIGHTS = ["c_ctx", "w_mod", "b_mod", "g_norm", "ffn_w1", "ffn_w3", "ffn_w2", "sc_w_in", "sc_conv", "sc_w_out",
           "mla_w_a", "mla_g_qa", "mla_w_uq", "mla_g_kva", "mla_w_ukv", "mla_g_q", "mla_g_k", "mla_w_o"]
EXCHANGE = {
    "w_mod": ("last", False, True), "ffn_w1": ("mid", True, True), "ffn_w3": ("mid", True, True),
    "ffn_w2": ("mid", False, True), "sc_w_in": ("last", False, True), "sc_w_out": ("mid", False, True),
    "mla_w_a": ("mid", False, True), "mla_w_uq": ("mid", True, True), "mla_w_ukv": ("last", False, True),
    "mla_w_o": ("mid", False, True), "g_norm": ("last", False, False), "sc_conv": ("last", False, False),
    "mla_g_qa": ("mid", False, False),
}
REPLICATED = ["c_ctx", "b_mod", "mla_g_kva", "mla_g_q", "mla_g_k"]


def _pick(n, cands):
    for cand in cands:
        if n % cand == 0:
            return cand
    return n


def _params(*sem):
    return pltpu.CompilerParams(dimension_semantics=sem, vmem_limit_bytes=VMEM_LIMIT_BYTES)


def _hbm():
    return pl.BlockSpec(memory_space=pl.ANY)


class Hosted:
    def __init__(self, inputs, out_shapes, scratch, start, finish):
        self.inputs, self.out_shapes, self.scratch, self.start, self.finish = inputs, out_shapes, scratch, start, finish


def _call(body, hosted, **kw):
    if hosted is None:
        return pl.pallas_call(body, **kw)
    single = not isinstance(kw["out_shape"], (tuple, list))
    out_shape = [kw["out_shape"]] if single else list(kw["out_shape"])
    out_specs = [kw["out_specs"]] if single else list(kw["out_specs"])
    in_specs, scratch, grid = list(kw["in_specs"]), list(kw.get("scratch_shapes", ())), kw["grid"]
    n_in, n_out, n_scr = len(in_specs), len(out_shape), len(scratch)
    h_in, h_out = len(hosted.inputs), len(hosted.out_shapes)

    def wrapped(*refs):
        ins, hins = refs[:n_in], refs[n_in:n_in + h_in]
        o0 = n_in + h_in
        outs, houts = refs[o0:o0 + n_out], refs[o0 + n_out:o0 + n_out + h_out]
        s0 = o0 + n_out + h_out
        scr, hscr = refs[s0:s0 + n_scr], refs[s0 + n_scr:]
        first = functools.reduce(jnp.logical_and, [pl.program_id(a) == 0 for a in range(len(grid))])
        last = functools.reduce(jnp.logical_and, [pl.program_id(a) == g - 1 for a, g in enumerate(grid)])

        @pl.when(first)
        def _():
            hosted.start(hins, houts, hscr)

        body(*ins, *outs, *scr)

        @pl.when(last)
        def _():
            hosted.finish(hins, houts, hscr)

    call = pl.pallas_call(
        wrapped, name=kw["name"], grid=grid, in_specs=in_specs + [_hbm()] * h_in,
        out_specs=tuple(out_specs + [_hbm()] * h_out), out_shape=tuple(out_shape + list(hosted.out_shapes)),
        scratch_shapes=scratch + list(hosted.scratch), input_output_aliases=kw.get("input_output_aliases", {}),
        compiler_params=_params(*["arbitrary"] * len(grid)))

    def run(*args):
        res = call(*args, *hosted.inputs)
        comp = res[:n_out]
        return (comp[0] if single else tuple(comp)), list(res[n_out:])

    return run


def _run_hosted(hosted, name):
    def body(*refs):
        h_in, h_out = len(hosted.inputs), len(hosted.out_shapes)
        hins, houts, hscr = refs[:h_in], refs[h_in:h_in + h_out], refs[h_in + h_out:]
        hosted.start(hins, houts, hscr)
        hosted.finish(hins, houts, hscr)

    return list(pl.pallas_call(
        body, name=name, in_specs=[_hbm()] * len(hosted.inputs), out_specs=tuple([_hbm()] * len(hosted.out_shapes)),
        out_shape=tuple(hosted.out_shapes), scratch_shapes=list(hosted.scratch))(*hosted.inputs))


class Geo:
    def __init__(self, n_lat, n_ctx):
        self.n_lat, self.n_ctx = n_lat, n_ctx
        self.rows = 2 * n_lat + 2 * n_ctx
        self.tile = n_ctx
        assert n_lat % n_ctx == 0 and n_ctx % 16 == 0
        self.mm_tile = _pick(n_lat, (512, 256, 128)) if self.rows % _pick(n_lat, (512, 256, 128)) == 0 else n_ctx
        self.big_tile = _pick(self.rows, (1536, 768, 512, 256))

    def seg(self, i, tile):
        return jnp.minimum((i * tile) // self.n_lat, N_SEG - 1)

    def seg_start(self, i, tile):
        row = i * tile
        return jnp.logical_or(row % self.n_lat == 0, row == 2 * self.n_lat) & (row <= 2 * self.n_lat)


_NT = (((1,), (1,)), ((), ()))
_NN = (((1,), (0,)), ((), ()))
_TN = (((0,), (0,)), ((), ()))


def _dot(a, b, dims):
    return lax.dot_general(a.astype(BF16), b.astype(BF16), dims, preferred_element_type=F32)


def _mm(a, b, *, ta=False, tb=False, out_dtype=F32, name, gate=None, hosted=None):
    (kdim, m) = a.shape if ta else a.shape[::-1]
    n = b.shape[0] if tb else b.shape[1]
    assert (b.shape[1] if tb else b.shape[0]) == kdim
    if gate is not None:
        tm = gate[4].mm_tile
    else:
        tm = _pick(m, (512, 256, 128))
    tn = _pick(n, (512, 256, 128))
    tk = _pick(kdim, (1024, 512, 256, 128))
    nk = kdim // tk
    dims = (((0 if ta else 1,), (1 if tb else 0,)), ((), ()))

    def body(*refs):
        if gate is not None:
            a_ref, b_ref, res_ref, gate_ref, o_ref, y_ref, acc_ref = refs
        else:
            a_ref, b_ref, o_ref, acc_ref = refs
        kk = pl.program_id(2)

        @pl.when(kk == 0)
        def _():
            acc_ref[...] = jnp.zeros_like(acc_ref)

        acc_ref[...] += lax.dot_general(a_ref[...].astype(BF16), b_ref[...].astype(BF16), dims,
                                        preferred_element_type=F32)

        @pl.when(kk == nk - 1)
        def _():
            acc = acc_ref[...]
            if gate is not None:
                y_ref[...] = acc.astype(y_ref.dtype)
                o_ref[...] = res_ref[...] + (gate[3] * gate_ref[...]) * acc
            else:
                o_ref[...] = acc.astype(o_ref.dtype)

    a_spec = pl.BlockSpec((tk, tm), lambda i, j, k: (k, i)) if ta else pl.BlockSpec((tm, tk), lambda i, j, k: (i, k))
    b_spec = pl.BlockSpec((tn, tk), lambda i, j, k: (j, k)) if tb else pl.BlockSpec((tk, tn), lambda i, j, k: (k, j))
    o_spec = pl.BlockSpec((tm, tn), lambda i, j, k: (i, j))
    in_specs, args = [a_spec, b_spec], [a, b]
    out_shape, out_specs = jax.ShapeDtypeStruct((m, n), out_dtype), o_spec
    if gate is not None:
        res, mod4, kmod, _, geo = gate
        in_specs += [o_spec, pl.BlockSpec((None, None, 1, tn), lambda i, j, k: (geo.seg(i, tm), kmod, 0, j))]
        args += [res, mod4]
        out_shape = (jax.ShapeDtypeStruct((m, n), F32), jax.ShapeDtypeStruct((m, n), BF16))
        out_specs = (o_spec, o_spec)
    return _call(
        body, hosted, name=name, grid=(m // tm, n // tn, nk), in_specs=in_specs, out_specs=out_specs,
        out_shape=out_shape, scratch_shapes=[pltpu.VMEM((tm, tn), F32)],
        compiler_params=_params("parallel", "parallel", "arbitrary"),
    )(*args)


def _tn_wide(lhs, rhs, name, into=None, s0=0, hosted=None):
    t, m = lhs.shape
    n = rhs.shape[1]
    tm = _pick(m, (1408, 1024, 512, 256, 128))
    while tm * n * 4 > 6.5 * 1024 * 1024 and tm % 256 == 0:
        tm //= 2
    tk = next(c for c in (1536, 768, 512, 256, 128, t)
              if t % c == 0 and c * (tm + n) * 4 + tm * n * 12 <= 36 * 1024 * 1024)

    def body(l_ref, r_ref, *rest):
        o_ref = rest[-1]
        kk = pl.program_id(1)
        part = lax.dot_general(l_ref[...], r_ref[...], _TN, preferred_element_type=F32)

        @pl.when(kk == 0)
        def _():
            o_ref[...] = part

        @pl.when(kk > 0)
        def _():
            o_ref[...] += part

    in_specs = [pl.BlockSpec((tk, tm), lambda i, k: (k, i)), pl.BlockSpec((tk, n), lambda i, k: (k, 0))]
    if into is None:
        return _call(
            body, hosted, name=name, grid=(m // tm, t // tk), in_specs=in_specs,
            out_specs=pl.BlockSpec((tm, n), lambda i, k: (i, 0)), out_shape=jax.ShapeDtypeStruct((m, n), F32),
            compiler_params=_params("parallel", "arbitrary"),
        )(lhs, rhs)
    return _call(
        body, hosted, name=name, grid=(m // tm, t // tk), in_specs=in_specs + [pl.BlockSpec(memory_space=pl.ANY)],
        out_specs=pl.BlockSpec((None, tm, n), lambda i, k: (s0, i, 0)),
        out_shape=jax.ShapeDtypeStruct(into.shape, into.dtype), input_output_aliases={2: 0},
        compiler_params=_params("parallel", "arbitrary"),
    )(lhs, rhs, into)


def _mod_spec(geo, tile, kmod, d):
    return pl.BlockSpec((None, None, 1, d), lambda i: (geo.seg(i, tile), kmod, 0, 0))


def _pre_fwd(h, g, mod4, k_shift, geo, name):
    t, d = h.shape
    tile = geo.tile

    def body(h_ref, g_ref, sh_ref, sc_ref, o_ref):
        hv = h_ref[...]
        r = lax.rsqrt(jnp.mean(hv * hv, axis=-1, keepdims=True) + EPS)
        y = hv * r * g_ref[...]
        o_ref[...] = (y * (1.0 + sc_ref[...]) + sh_ref[...]).astype(o_ref.dtype)

    row = pl.BlockSpec((tile, d), lambda i: (i, 0))
    return pl.pallas_call(
        body, name=name, grid=(t // tile,),
        in_specs=[row, pl.BlockSpec((1, d), lambda i: (0, 0)), _mod_spec(geo, tile, k_shift, d),
                  _mod_spec(geo, tile, k_shift + 1, d)],
        out_specs=row, out_shape=jax.ShapeDtypeStruct((t, d), BF16), compiler_params=_params("parallel"),
    )(h, g, mod4, mod4)


def _pre_bwd(h, g, mod4, k_shift, dnx, dres, geo, name):
    t, d = h.shape
    tile = geo.tile

    def body(h_ref, g_ref, sc_ref, dnx_ref, dres_ref, dh_ref, dg_ref, dsh_ref, dsc_ref):
        i = pl.program_id(0)
        hv, gv, dout = h_ref[...], g_ref[...], dnx_ref[...].astype(F32)
        r = lax.rsqrt(jnp.mean(hv * hv, axis=-1, keepdims=True) + EPS)
        xhat = hv * r
        dy = dout * (1.0 + sc_ref[...])
        u = dy * gv
        dh_ref[...] = r * (u - xhat * jnp.mean(u * xhat, axis=-1, keepdims=True)) + dres_ref[...]

        @pl.when(i == 0)
        def _():
            dg_ref[...] = jnp.zeros_like(dg_ref)

        @pl.when(geo.seg_start(i, tile))
        def _():
            dsh_ref[...] = jnp.zeros_like(dsh_ref)
            dsc_ref[...] = jnp.zeros_like(dsc_ref)

        dg_ref[...] += jnp.sum(dy * xhat, axis=0, keepdims=True)
        dsh_ref[...] += jnp.sum(dout, axis=0, keepdims=True)
        dsc_ref[...] += jnp.sum(dout * (xhat * gv), axis=0, keepdims=True)

    row = pl.BlockSpec((tile, d), lambda i: (i, 0))
    vec = pl.BlockSpec((1, d), lambda i: (0, 0))
    segv = pl.BlockSpec((None, 1, d), lambda i: (geo.seg(i, tile), 0, 0))
    return pl.pallas_call(
        body, name=name, grid=(t // tile,),
        in_specs=[row, vec, _mod_spec(geo, tile, k_shift + 1, d), row, row],
        out_specs=(row, vec, segv, segv),
        out_shape=(jax.ShapeDtypeStruct((t, d), F32), jax.ShapeDtypeStruct((1, d), F32),
                   jax.ShapeDtypeStruct((N_SEG, 1, d), F32), jax.ShapeDtypeStruct((N_SEG, 1, d), F32)),
        compiler_params=_params("arbitrary"),
    )(h, g, mod4, dnx, dres)


def _gate_bwd(dh, y, mod4, k_gate, coef, geo, name):
    t, d = dh.shape
    tile = geo.tile

    def body(dh_ref, y_ref, gt_ref, dy_ref, dgt_ref):
        i = pl.program_id(0)
        dhv = dh_ref[...]
        dy_ref[...] = ((coef * gt_ref[...]) * dhv).astype(dy_ref.dtype)

        @pl.when(geo.seg_start(i, tile))
        def _():
            dgt_ref[...] = jnp.zeros_like(dgt_ref)

        dgt_ref[...] += coef * jnp.sum(dhv * y_ref[...].astype(F32), axis=0, keepdims=True)

    row = pl.BlockSpec((tile, d), lambda i: (i, 0))
    segv = pl.BlockSpec((None, 1, d), lambda i: (geo.seg(i, tile), 0, 0))
    return pl.pallas_call(
        body, name=name, grid=(t // tile,), in_specs=[row, row, _mod_spec(geo, tile, k_gate, d)],
        out_specs=(row, segv),
        out_shape=(jax.ShapeDtypeStruct((t, d), BF16), jax.ShapeDtypeStruct((N_SEG, 1, d), F32)),
        compiler_params=_params("arbitrary"),
    )(dh, y, mod4)


def _ff_tile(f):
    return _pick(f, (256, 128))


def _ffn_up(nx, w1t, w3t, s0, geo, name, hosted=None):
    t, d = nx.shape
    f = w1t.shape[1]
    tm, tn = geo.big_tile, _ff_tile(f)

    def body(x_ref, w1_ref, w3_ref, ga_ref, gb_ref, act_ref):
        xv = x_ref[...]
        a = lax.dot_general(xv, w1_ref[...], _NT, preferred_element_type=F32)
        bv = lax.dot_general(xv, w3_ref[...], _NT, preferred_element_type=F32)
        sg = jax.nn.sigmoid(a)
        silu = a * sg
        ga_ref[...] = (bv * (sg + silu * (1.0 - sg))).astype(ga_ref.dtype)
        gb_ref[...] = silu.astype(gb_ref.dtype)
        act_ref[...] = (silu * bv).astype(act_ref.dtype)

    w_spec = pl.BlockSpec((None, tn, d), lambda i, j: (s0, j, 0))
    o_spec = pl.BlockSpec((tm, tn), lambda i, j: (i, j))
    sds = jax.ShapeDtypeStruct((t, f), BF16)
    return _call(
        body, hosted, name=name, grid=(t // tm, f // tn),
        in_specs=[pl.BlockSpec((tm, d), lambda i, j: (i, 0)), w_spec, w_spec],
        out_specs=(o_spec,) * 3, out_shape=(sds,) * 3, compiler_params=_params("parallel", "parallel"),
    )(nx, w1t, w3t)


def _ffn_down(act, w2, s0, res, mod4, k_gate, geo, name, hosted=None):
    t, f = act.shape
    d = w2.shape[2]
    tm, tn = geo.mm_tile, _pick(d, (1024, 512, 256, 128))

    def body(a_ref, w_ref, res_ref, gate_ref, o_ref, y_ref):
        acc = lax.dot_general(a_ref[...], w_ref[...], _NN, preferred_element_type=F32)
        y_ref[...] = acc.astype(y_ref.dtype)
        o_ref[...] = res_ref[...] + (0.5 * gate_ref[...]) * acc

    o_spec = pl.BlockSpec((tm, tn), lambda i, j: (i, j))
    return _call(
        body, hosted, name=name, grid=(t // tm, d // tn),
        in_specs=[pl.BlockSpec((tm, f), lambda i, j: (i, 0)), pl.BlockSpec((None, f, tn), lambda i, j: (s0, 0, j)),
                  o_spec, pl.BlockSpec((None, None, 1, tn), lambda i, j: (geo.seg(i, tm), k_gate, 0, j))],
        out_specs=(o_spec, o_spec),
        out_shape=(jax.ShapeDtypeStruct((t, d), F32), jax.ShapeDtypeStruct((t, d), BF16)),
        compiler_params=_params("parallel", "parallel"),
    )(act, w2, res, mod4)


def _ffn_dhidden(dy, w2, ga, gb, w1t, w3t, s0, geo, name, hosted=None):
    t, d = dy.shape
    f = w2.shape[1]
    tm, tn = geo.big_tile, _ff_tile(f)
    nj = f // tn

    def body(dy_ref, w2_ref, ga_ref, gb_ref, w1_ref, w3_ref, da_ref, db_ref, dnx_ref, acc_ref):
        j = pl.program_id(1)
        dact = lax.dot_general(dy_ref[...], w2_ref[...], _NT, preferred_element_type=F32)
        da = (dact * ga_ref[...].astype(F32)).astype(da_ref.dtype)
        db = (dact * gb_ref[...].astype(F32)).astype(db_ref.dtype)
        da_ref[...] = da
        db_ref[...] = db
        part = (lax.dot_general(da, w1_ref[...], _NN, preferred_element_type=F32)
                + lax.dot_general(db, w3_ref[...], _NN, preferred_element_type=F32))

        @pl.when(j == 0)
        def _():
            acc_ref[...] = part

        @pl.when(j > 0)
        def _():
            acc_ref[...] += part

        @pl.when(j == nj - 1)
        def _():
            dnx_ref[...] = acc_ref[...]

    row = pl.BlockSpec((tm, d), lambda i, j: (i, 0))
    w_spec = pl.BlockSpec((None, tn, d), lambda i, j: (s0, j, 0))
    o_spec = pl.BlockSpec((tm, tn), lambda i, j: (i, j))
    sds = jax.ShapeDtypeStruct((t, f), BF16)
    return _call(
        body, hosted, name=name, grid=(t // tm, nj), in_specs=[row, w_spec, o_spec, o_spec, w_spec, w_spec],
        out_specs=(o_spec, o_spec, row), out_shape=(sds, sds, jax.ShapeDtypeStruct((t, d), F32)),
        scratch_shapes=[pltpu.VMEM((tm, d), F32)], compiler_params=_params("parallel", "arbitrary"),
    )(dy, w2, ga, gb, w1t, w3t)


def _with_host(fn, hosts, got, slot, *args, **kw):
    hosted = hosts.get(slot)
    if hosted is None:
        return fn(*args, **kw)
    out, got[slot] = fn(*args, hosted=hosted, **kw)
    return out


def _ffn_fwd(h, g, mod4, k0, w, s0, geo, tag, sub, hosts, got):
    nx = _pre_fwd(h, g, mod4, k0, geo, f"{tag}_pre")
    a, b, act = _with_host(_ffn_up, hosts, got, f"{sub}_up", nx, w["ffn_w1"], w["ffn_w3"], s0, geo, f"{tag}_up")
    h_out, y = _with_host(_ffn_down, hosts, got, f"{sub}_down", act, w["ffn_w2"], s0, h, mod4, k0 + 2, geo, f"{tag}_down")
    return h_out, (h, nx, a, b, act, y)


def _ffn_bwd(dh_out, saved, g, mod4, k0, w, s0, gbuf, geo, tag, sub, hosts, got):
    h, nx, a, b, act, y = saved
    dy, dgate = _gate_bwd(dh_out, y, mod4, k0 + 2, 0.5, geo, f"{tag}_dgate")
    da, db, dnx = _with_host(_ffn_dhidden, hosts, got, f"{sub}_dact", dy, w["ffn_w2"], a, b, w["ffn_w1"], w["ffn_w3"], s0,
                             geo, f"{tag}_dact")
    gbuf["ffn_w2"] = _with_host(_tn_wide, hosts, got, f"{sub}_dw2", act, dy, f"{tag}_dw2", into=gbuf["ffn_w2"], s0=s0)
    gbuf["ffn_w1"] = _tn_wide(da, nx, f"{tag}_dw1", into=gbuf["ffn_w1"], s0=s0)
    gbuf["ffn_w3"] = _tn_wide(db, nx, f"{tag}_dw3", into=gbuf["ffn_w3"], s0=s0)
    dh, dg, dshift, dscale = _pre_bwd(h, g, mod4, k0, dnx, dh_out, geo, f"{tag}_dpre")
    return dh, dg, (dshift, dscale, dgate)


def _interleave(w, n_parts, tile):
    lead, cols = w.shape[:-1], w.shape[-1] // n_parts
    return w.reshape(*lead, n_parts, cols // tile, tile).swapaxes(-3, -2).reshape(*lead, n_parts * cols)


def _deinterleave(w, n_parts, tile):
    lead, cols = w.shape[:-1], w.shape[-1] // n_parts
    return w.reshape(*lead, cols // tile, n_parts, tile).swapaxes(-3, -2).reshape(*lead, n_parts * cols)


HALO = 16


def _conv_tile(c):
    return _pick(c, (256, 128))


def _conv_specs(geo, tc, t):
    tile = geo.tile
    per = tile // HALO
    last = t // HALO - 1
    cur = pl.BlockSpec((tile, 3 * tc), lambda j, i: (i, j))
    prev = pl.BlockSpec((HALO, 3 * tc), lambda j, i: (jnp.maximum(i * per - 1, 0), j))
    nxt = pl.BlockSpec((HALO, 3 * tc), lambda j, i: (jnp.minimum((i + 1) * per, last), j))
    return cur, prev, nxt


def _conv_edges(geo, i):
    tile = geo.tile
    row = i * tile
    lat = row < 2 * geo.n_lat
    first = jnp.where(lat, row % geo.n_lat == 0, (row - 2 * geo.n_lat) % geo.n_ctx == 0)
    end = row + tile
    last = jnp.where(lat, end % geo.n_lat == 0, (end - 2 * geo.n_lat) % geo.n_ctx == 0)
    return first, last


def _shift_rows(v, before, after):
    n = v.shape[0]
    rows = lax.broadcasted_iota(jnp.int32, v.shape, 0)
    down = jnp.where(rows == 0, before, pltpu.roll(v, 1, 0))
    up = jnp.where(rows == n - 1, after, pltpu.roll(v, n - 1, 0))
    return down, up


def _conv_fwd(proj, conv_w, geo, name, hosted=None):
    t, c3 = proj.shape
    c = c3 // 3
    tc, tile = _conv_tile(c), geo.tile

    def body(cur_ref, prev_ref, next_ref, w_ref, o_ref):
        first, last = _conv_edges(geo, pl.program_id(1))
        bv = cur_ref[:, :tc].astype(F32)
        p = cur_ref[:, tc:2 * tc].astype(F32) * cur_ref[:, 2 * tc:].astype(F32)
        p_before = prev_ref[HALO - 1:HALO, tc:2 * tc].astype(F32) * prev_ref[HALO - 1:HALO, 2 * tc:].astype(F32)
        p_after = next_ref[0:1, tc:2 * tc].astype(F32) * next_ref[0:1, 2 * tc:].astype(F32)
        p_before = jnp.where(first, 0.0, p_before)
        p_after = jnp.where(last, 0.0, p_after)
        pm1, pp1 = _shift_rows(p, p_before, p_after)
        w = w_ref[...]
        q = w[0:1] * pm1 + w[1:2] * p + w[2:3] * pp1
        o_ref[...] = (bv * q).astype(o_ref.dtype)

    cur, prev, nxt = _conv_specs(geo, tc, t)
    return _call(
        body, hosted, name=name, grid=(c // tc, t // tile),
        in_specs=[cur, prev, nxt, pl.BlockSpec((3, tc), lambda j, i: (0, j))],
        out_specs=pl.BlockSpec((tile, tc), lambda j, i: (i, j)), out_shape=jax.ShapeDtypeStruct((t, c), BF16),
        compiler_params=_params("parallel", "parallel"),
    )(proj, proj, proj, conv_w)


def _conv_bwd(proj, dyc, conv_w, geo, name, hosted=None):
    t, c3 = proj.shape
    c = c3 // 3
    tc, tile = _conv_tile(c), geo.tile

    def body(cur_ref, prev_ref, next_ref, d_ref, dprev_ref, dnext_ref, w_ref, o_ref, dw_ref):
        i = pl.program_id(1)
        first, last = _conv_edges(geo, i)
        bv = cur_ref[:, :tc].astype(F32)
        cv = cur_ref[:, tc:2 * tc].astype(F32)
        uv = cur_ref[:, 2 * tc:].astype(F32)
        p = cv * uv
        p_before = prev_ref[HALO - 1:HALO, tc:2 * tc].astype(F32) * prev_ref[HALO - 1:HALO, 2 * tc:].astype(F32)
        p_after = next_ref[0:1, tc:2 * tc].astype(F32) * next_ref[0:1, 2 * tc:].astype(F32)
        p_before = jnp.where(first, 0.0, p_before)
        p_after = jnp.where(last, 0.0, p_after)
        pm1, pp1 = _shift_rows(p, p_before, p_after)
        w = w_ref[...]
        q = w[0:1] * pm1 + w[1:2] * p + w[2:3] * pp1
        dy = d_ref[...].astype(F32)
        dq = dy * bv
        dq_before = dprev_ref[HALO - 1:HALO, :].astype(F32) * prev_ref[HALO - 1:HALO, :tc].astype(F32)
        dq_after = dnext_ref[0:1, :].astype(F32) * next_ref[0:1, :tc].astype(F32)
        dq_before = jnp.where(first, 0.0, dq_before)
        dq_after = jnp.where(last, 0.0, dq_after)
        dqm1, dqp1 = _shift_rows(dq, dq_before, dq_after)
        dp = w[0:1] * dqp1 + w[1:2] * dq + w[2:3] * dqm1
        o_ref[:, :tc] = (dy * q).astype(o_ref.dtype)
        o_ref[:, tc:2 * tc] = (dp * uv).astype(o_ref.dtype)
        o_ref[:, 2 * tc:] = (dp * cv).astype(o_ref.dtype)

        @pl.when(i == 0)
        def _():
            dw_ref[...] = jnp.zeros_like(dw_ref)

        dw_ref[0:1, :] += jnp.sum(dq * pm1, axis=0, keepdims=True)
        dw_ref[1:2, :] += jnp.sum(dq * p, axis=0, keepdims=True)
        dw_ref[2:3, :] += jnp.sum(dq * pp1, axis=0, keepdims=True)

    cur, prev, nxt = _conv_specs(geo, tc, t)
    per, lastb = tile // HALO, t // HALO - 1
    dcur = pl.BlockSpec((tile, tc), lambda j, i: (i, j))
    dprev = pl.BlockSpec((HALO, tc), lambda j, i: (jnp.maximum(i * per - 1, 0), j))
    dnext = pl.BlockSpec((HALO, tc), lambda j, i: (jnp.minimum((i + 1) * per, lastb), j))
    wspec = pl.BlockSpec((3, tc), lambda j, i: (0, j))
    return _call(
        body, hosted, name=name, grid=(c // tc, t // tile), in_specs=[cur, prev, nxt, dcur, dprev, dnext, wspec],
        out_specs=(cur, wspec), out_shape=(jax.ShapeDtypeStruct((t, c3), BF16), jax.ShapeDtypeStruct((3, c), F32)),
        compiler_params=_params("parallel", "arbitrary"),
    )(proj, proj, proj, dyc, dyc, dyc, conv_w)


def _sconv_fwd(h, g, mod4, w_in, conv_w, w_out, geo, tag, hosts, got):
    nx = _pre_fwd(h, g, mod4, 3, geo, f"{tag}_pre")
    proj = _with_host(_mm, hosts, got, "mix_a", nx, w_in, out_dtype=BF16, name=f"{tag}_in")
    yc = _with_host(_conv_fwd, hosts, got, "mix_b", proj, conv_w, geo, f"{tag}_conv")
    h_out, y = _mm(yc, w_out, name=f"{tag}_out", gate=(h, mod4, 5, 1.0, geo))
    return h_out, (h, nx, proj, yc, y)


def _sconv_bwd(dh_out, saved, g, mod4, w_in, conv_w, w_out, geo, tag, hosts, got):
    h, nx, proj, yc, y = saved
    dy, dgate = _gate_bwd(dh_out, y, mod4, 5, 1.0, geo, f"{tag}_dgate")
    dyc = _mm(dy, w_out, tb=True, out_dtype=BF16, name=f"{tag}_dyc")
    dw_out = _tn_wide(yc, dy, f"{tag}_dwout")
    dproj, dconv = _with_host(_conv_bwd, hosts, got, "mix_c", proj, dyc, conv_w, geo, f"{tag}_dconv")
    dnx = _with_host(_mm, hosts, got, "mix_b", dproj, w_in, tb=True, name=f"{tag}_dnx")
    dw_in = _with_host(_tn_wide, hosts, got, "mix_a", nx, dproj, f"{tag}_dwin")
    dh, dg, dshift, dscale = _pre_bwd(h, g, mod4, 3, dnx, dh_out, geo, f"{tag}_dpre")
    return dh, dg, (dshift, dscale, dgate), dw_in, dconv, dw_out


def _rope_swap(v):
    nf = QK_ROPE // 4
    return v.reshape(v.shape[:-1] + (2, 2, nf)).swapaxes(-3, -2).reshape(v.shape)


def _rope_tables(geo):
    n = geo.n_lat
    nf = QK_ROPE // 4
    pos = np.arange(n)
    inv = ROPE_BASE ** (-np.arange(nf, dtype=np.float32) / nf)
    ang = np.concatenate([(pos // GRID_W)[:, None] * inv, (pos % GRID_W)[:, None] * inv], axis=1).astype(np.float32)
    cos, sin = np.cos(ang), np.sin(ang)
    zeros = np.zeros((n, LANE - QK_ROPE), np.float32)
    c_lat = np.concatenate([cos, cos, zeros], axis=1)
    s_lat = np.concatenate([-sin, sin, zeros], axis=1)
    c_ctx = np.concatenate([np.ones((2 * geo.n_ctx, QK_ROPE), np.float32), np.zeros((2 * geo.n_ctx, LANE - QK_ROPE), np.float32)], 1)
    s_ctx = np.zeros((2 * geo.n_ctx, LANE), np.float32)
    return (jnp.asarray(np.concatenate([c_lat, c_lat, c_ctx], 0)), jnp.asarray(np.concatenate([s_lat, s_lat, s_ctx], 0)))


def _swap_halves(v):
    lanes = lax.broadcasted_iota(jnp.int32, v.shape, 1)
    return jnp.where(lanes < QK_ROPE // 2, pltpu.roll(v, LANE - QK_ROPE // 2, 1), pltpu.roll(v, QK_ROPE // 2, 1))


def _latent_norm_fwd(down, g_qa, g_kva, geo, name):
    t, wd = down.shape
    ql, kl = g_qa.shape[1], g_kva.shape[1]
    tile = geo.tile

    def body(d_ref, gq_ref, gk_ref, cq_ref, ckv_ref):
        for lo, n, g_ref, o_ref in ((0, ql, gq_ref, cq_ref), (ql, kl, gk_ref, ckv_ref)):
            x = d_ref[:, lo:lo + n]
            r = lax.rsqrt(jnp.mean(x * x, axis=-1, keepdims=True) + EPS)
            o_ref[...] = (x * r * g_ref[...]).astype(o_ref.dtype)

    return pl.pallas_call(
        body, name=name, grid=(t // tile,),
        in_specs=[pl.BlockSpec((tile, wd), lambda i: (i, 0)), pl.BlockSpec((1, ql), lambda i: (0, 0)),
                  pl.BlockSpec((1, kl), lambda i: (0, 0))],
        out_specs=(pl.BlockSpec((tile, ql), lambda i: (i, 0)), pl.BlockSpec((tile, kl), lambda i: (i, 0))),
        out_shape=(jax.ShapeDtypeStruct((t, ql), BF16), jax.ShapeDtypeStruct((t, kl), BF16)),
        compiler_params=_params("parallel"),
    )(down, g_qa, g_kva)


def _latent_norm_bwd(down, g_qa, g_kva, dcqn, dckvn, dkr, geo, name):
    t, wd = down.shape
    ql, kl = g_qa.shape[1], g_kva.shape[1]
    tile = geo.tile

    def body(d_ref, gq_ref, gk_ref, dq_ref, dk_ref, dkr_ref, o_ref, dgq_ref, dgk_ref):
        i = pl.program_id(0)

        @pl.when(i == 0)
        def _():
            dgq_ref[...] = jnp.zeros_like(dgq_ref)
            dgk_ref[...] = jnp.zeros_like(dgk_ref)

        for lo, n, g_ref, dy_ref, dg_ref in ((0, ql, gq_ref, dq_ref, dgq_ref), (ql, kl, gk_ref, dk_ref, dgk_ref)):
            x = d_ref[:, lo:lo + n]
            dy = dy_ref[...].astype(F32)
            r = lax.rsqrt(jnp.mean(x * x, axis=-1, keepdims=True) + EPS)
            xhat = x * r
            u = dy * g_ref[...]
            o_ref[:, lo:lo + n] = (r * (u - xhat * jnp.mean(u * xhat, axis=-1, keepdims=True))).astype(o_ref.dtype)
            dg_ref[...] += jnp.sum(dy * xhat, axis=0, keepdims=True)
        o_ref[:, ql + kl:] = dkr_ref[...].astype(o_ref.dtype)

    def row(n):
        return pl.BlockSpec((tile, n), lambda i: (i, 0))

    def vec(n):
        return pl.BlockSpec((1, n), lambda i: (0, 0))

    return pl.pallas_call(
        body, name=name, grid=(t // tile,),
        in_specs=[row(wd), vec(ql), vec(kl), row(ql), row(kl), row(wd - ql - kl)],
        out_specs=(row(wd), vec(ql), vec(kl)),
        out_shape=(jax.ShapeDtypeStruct((t, wd), BF16), jax.ShapeDtypeStruct((1, ql), F32),
                   jax.ShapeDtypeStruct((1, kl), F32)),
        compiler_params=_params("arbitrary"),
    )(down, g_qa, g_kva, dcqn, dckvn, dkr)


def _qk_specs(geo, xr_col, shared_rope):
    tile = geo.mm_tile
    xn_spec = pl.BlockSpec((tile, HEADS * LANE), lambda i: (i, 0))
    if shared_rope:
        xr_spec = pl.BlockSpec((tile, LANE), lambda i: (i, xr_col))
    else:
        xr_spec = pl.BlockSpec((tile, HEADS * LANE), lambda i: (i, xr_col // HEADS))
    vec = pl.BlockSpec((1, LANE), lambda i: (0, 0))
    tab = pl.BlockSpec((tile, LANE), lambda i: (i, 0))
    return tile, xn_spec, xr_spec, vec, tab


def _qk_norm(xn, xr):
    ss = jnp.sum(xn * xn, axis=-1, keepdims=True) + jnp.sum(xr * xr, axis=-1, keepdims=True)
    return lax.rsqrt(ss * (1.0 / QK_HEAD) + EPS)


def _head_lanes(ref, hh, shared=False):
    return ref[...] if shared else ref[:, hh * LANE:(hh + 1) * LANE]


def _qk_fwd(xn_arr, xr_arr, xr_col, shared_rope, gn, gr, cos, sin, geo, name):
    t = xn_arr.shape[0]
    tile, xn_spec, xr_spec, vec, tab = _qk_specs(geo, xr_col, shared_rope)

    def body(xn_ref, xr_ref, gn_ref, gr_ref, c_ref, s_ref, o_ref):
        cv, sv, gnv, grv = c_ref[...], s_ref[...], gn_ref[...], gr_ref[...]
        for hh in range(HEADS):
            xn = _head_lanes(xn_ref, hh).astype(F32)
            xr = _head_lanes(xr_ref, hh, shared_rope).astype(F32)
            r = _qk_norm(xn, xr)
            yr = xr * r * grv
            o_ref[:, hh * HEAD_PAD:hh * HEAD_PAD + LANE] = (xn * r * gnv).astype(o_ref.dtype)
            o_ref[:, hh * HEAD_PAD + LANE:(hh + 1) * HEAD_PAD] = (yr * cv + _swap_halves(yr) * sv).astype(o_ref.dtype)

    return pl.pallas_call(
        body, name=name, grid=(t // tile,), in_specs=[xn_spec, xr_spec, vec, vec, tab, tab],
        out_specs=pl.BlockSpec((tile, HEADS * HEAD_PAD), lambda i: (i, 0)),
        out_shape=jax.ShapeDtypeStruct((t, HEADS * HEAD_PAD), BF16), compiler_params=_params("parallel"),
    )(xn_arr, xr_arr, gn, gr, cos, sin)


def _qk_bwd(xn_arr, xr_arr, xr_col, shared_rope, gn, gr, cos, sin, dout, geo, name):
    t = xn_arr.shape[0]
    tile, xn_spec, xr_spec, vec, tab = _qk_specs(geo, xr_col, shared_rope)

    def body(xn_ref, xr_ref, gn_ref, gr_ref, c_ref, s_ref, d_ref, dxn_ref, dxr_ref, dgn_ref, dgr_ref):
        i = pl.program_id(0)
        cv, sv, gnv, grv = c_ref[...], s_ref[...], gn_ref[...], gr_ref[...]
        dgn = jnp.zeros((1, LANE), F32)
        dgr = jnp.zeros((1, LANE), F32)
        dxr_sum = jnp.zeros((tile, LANE), F32)
        for hh in range(HEADS):
            xn = _head_lanes(xn_ref, hh).astype(F32)
            xr = _head_lanes(xr_ref, hh, shared_rope).astype(F32)
            r = _qk_norm(xn, xr)
            xhn, xhr = xn * r, xr * r
            dyn = d_ref[:, hh * HEAD_PAD:hh * HEAD_PAD + LANE].astype(F32)
            dro = d_ref[:, hh * HEAD_PAD + LANE:(hh + 1) * HEAD_PAD].astype(F32)
            dyr = dro * cv + _swap_halves(dro * sv)
            un, ur = dyn * gnv, dyr * grv
            mean = (jnp.sum(un * xhn, axis=-1, keepdims=True) + jnp.sum(ur * xhr, axis=-1, keepdims=True)) * (1.0 / QK_HEAD)
            dxn_ref[:, hh * LANE:(hh + 1) * LANE] = (r * (un - xhn * mean)).astype(dxn_ref.dtype)
            dxr = r * (ur - xhr * mean)
            if shared_rope:
                dxr_sum = dxr_sum + dxr
            else:
                dxr_ref[:, hh * LANE:(hh + 1) * LANE] = dxr.astype(dxr_ref.dtype)
            dgn = dgn + jnp.sum(dyn * xhn, axis=0, keepdims=True)
            dgr = dgr + jnp.sum(dyr * xhr, axis=0, keepdims=True)
        if shared_rope:
            dxr_ref[...] = dxr_sum

        @pl.when(i == 0)
        def _():
            dgn_ref[...] = jnp.zeros_like(dgn_ref)
            dgr_ref[...] = jnp.zeros_like(dgr_ref)

        dgn_ref[...] += dgn
        dgr_ref[...] += dgr

    heads = pl.BlockSpec((tile, HEADS * LANE), lambda i: (i, 0))
    if shared_rope:
        dxr_spec, dxr_shape = pl.BlockSpec((tile, LANE), lambda i: (i, 0)), jax.ShapeDtypeStruct((t, LANE), F32)
    else:
        dxr_spec, dxr_shape = heads, jax.ShapeDtypeStruct((t, HEADS * LANE), BF16)
    return pl.pallas_call(
        body, name=name, grid=(t // tile,),
        in_specs=[xn_spec, xr_spec, vec, vec, tab, tab, pl.BlockSpec((tile, HEADS * HEAD_PAD), lambda i: (i, 0))],
        out_specs=(heads, dxr_spec, vec, vec),
        out_shape=(jax.ShapeDtypeStruct((t, HEADS * LANE), BF16), dxr_shape, jax.ShapeDtypeStruct((1, LANE), F32),
                   jax.ShapeDtypeStruct((1, LANE), F32)),
        compiler_params=_params("arbitrary"),
    )(xn_arr, xr_arr, gn, gr, cos, sin, dout)


def _attn_specs(geo):
    tq, nq = geo.n_ctx, geo.n_lat // geo.n_ctx

    def qrow(b, i):
        return jnp.where(i < nq, b * nq + i, 2 * nq + b)

    q_spec = pl.BlockSpec((tq, HEAD_PAD), lambda b, hh, i: (qrow(b, i), hh))
    kc_spec = pl.BlockSpec((geo.n_ctx, HEAD_PAD), lambda b, hh, i: (2 * nq + b, hh))
    kl_spec = pl.BlockSpec((geo.n_lat, HEAD_PAD), lambda b, hh, i: (b, hh))
    vc_spec = pl.BlockSpec((geo.n_ctx, V_HEAD), lambda b, hh, i: (2 * nq + b, HEADS + hh))
    vl_spec = pl.BlockSpec((geo.n_lat, V_HEAD), lambda b, hh, i: (b, HEADS + hh))
    o_spec = pl.BlockSpec((tq, V_HEAD), lambda b, hh, i: (qrow(b, i), hh))
    return tq, nq, q_spec, kc_spec, kl_spec, vc_spec, vl_spec, o_spec


def _attn_fwd(q, k, kv, with_ctx_q, geo, name, hosted=None):
    t = q.shape[0]
    tq, nq, q_spec, kc_spec, kl_spec, vc_spec, vl_spec, o_spec = _attn_specs(geo)

    def body(q_ref, kc_ref, kl_ref, vc_ref, vl_ref, o_ref):
        i = pl.program_id(2)
        qv = q_ref[...]
        s_c = _dot(qv, kc_ref[...], _NT) * QK_SCALE

        @pl.when(i < nq)
        def _():
            s_l = _dot(qv, kl_ref[...], _NT) * QK_SCALE
            m = jnp.maximum(jnp.max(s_c, axis=-1, keepdims=True), jnp.max(s_l, axis=-1, keepdims=True))
            p_c, p_l = jnp.exp(s_c - m), jnp.exp(s_l - m)
            den = jnp.sum(p_c, axis=-1, keepdims=True) + jnp.sum(p_l, axis=-1, keepdims=True)
            o = _dot(p_c, vc_ref[...], _NN) + _dot(p_l, vl_ref[...], _NN)
            o_ref[...] = (o / den).astype(o_ref.dtype)

        @pl.when(i == nq)
        def _():
            if with_ctx_q:
                m = jnp.max(s_c, axis=-1, keepdims=True)
                p_c = jnp.exp(s_c - m)
                o = _dot(p_c, vc_ref[...], _NN) / jnp.sum(p_c, axis=-1, keepdims=True)
                o_ref[...] = o.astype(o_ref.dtype)
            else:
                o_ref[...] = jnp.zeros_like(o_ref)

    return _call(
        body, hosted, name=name, grid=(2, HEADS, nq + 1), in_specs=[q_spec, kc_spec, kl_spec, vc_spec, vl_spec],
        out_specs=o_spec, out_shape=jax.ShapeDtypeStruct((t, HEADS * V_HEAD), BF16),
        compiler_params=_params("parallel", "parallel", "arbitrary"),
    )(q, k, k, kv, kv)


def _attn_bwd(q, k, kv, do, with_ctx_q, geo, name, hosted=None):
    t = q.shape[0]
    tq, nq, q_spec, kc_spec, kl_spec, vc_spec, vl_spec, o_spec = _attn_specs(geo)

    def body(q_ref, kc_ref, kl_ref, vc_ref, vl_ref, do_ref, dq_ref, dkl_ref, dkc_ref, dvl_ref, dvc_ref,
             akl_ref, akc_ref, avl_ref, avc_ref):
        i = pl.program_id(2)

        @pl.when(i == 0)
        def _():
            for ref in (akl_ref, akc_ref, avl_ref, avc_ref):
                ref[...] = jnp.zeros_like(ref)

        qv, dov = q_ref[...], do_ref[...]
        s_c = _dot(qv, kc_ref[...], _NT) * QK_SCALE
        dp_c = _dot(dov, vc_ref[...], _NT)

        def ctx_part(p_c, delta):
            ds_c = (p_c * (dp_c - delta) * QK_SCALE).astype(BF16)
            akc_ref[...] += _dot(ds_c, qv, _TN)
            avc_ref[...] += _dot(p_c, dov, _TN)
            return _dot(ds_c, kc_ref[...], _NN)

        @pl.when(i < nq)
        def _():
            s_l = _dot(qv, kl_ref[...], _NT) * QK_SCALE
            m = jnp.maximum(jnp.max(s_c, axis=-1, keepdims=True), jnp.max(s_l, axis=-1, keepdims=True))
            p_c, p_l = jnp.exp(s_c - m), jnp.exp(s_l - m)
            inv = 1.0 / (jnp.sum(p_c, axis=-1, keepdims=True) + jnp.sum(p_l, axis=-1, keepdims=True))
            p_c, p_l = p_c * inv, p_l * inv
            dp_l = _dot(dov, vl_ref[...], _NT)
            delta = jnp.sum(p_c * dp_c, axis=-1, keepdims=True) + jnp.sum(p_l * dp_l, axis=-1, keepdims=True)
            ds_l = (p_l * (dp_l - delta) * QK_SCALE).astype(BF16)
            akl_ref[...] += _dot(ds_l, qv, _TN)
            avl_ref[...] += _dot(p_l, dov, _TN)
            dq_ref[...] = (ctx_part(p_c, delta) + _dot(ds_l, kl_ref[...], _NN)).astype(dq_ref.dtype)

        @pl.when(i == nq)
        def _():
            if with_ctx_q:
                m = jnp.max(s_c, axis=-1, keepdims=True)
                p_c = jnp.exp(s_c - m)
                p_c = p_c * (1.0 / jnp.sum(p_c, axis=-1, keepdims=True))
                delta = jnp.sum(p_c * dp_c, axis=-1, keepdims=True)
                dq_ref[...] = ctx_part(p_c, delta).astype(dq_ref.dtype)
            else:
                dq_ref[...] = jnp.zeros_like(dq_ref)
            dkl_ref[...] = akl_ref[...].astype(dkl_ref.dtype)
            dkc_ref[...] = akc_ref[...].astype(dkc_ref.dtype)
            dvl_ref[...] = avl_ref[...].astype(dvl_ref.dtype)
            dvc_ref[...] = avc_ref[...].astype(dvc_ref.dtype)

    def acc_spec(rows, width):
        return pl.BlockSpec((rows, width), lambda b, hh, i: (b, hh))

    return _call(
        body, hosted, name=name, grid=(2, HEADS, nq + 1), in_specs=[q_spec, kc_spec, kl_spec, vc_spec, vl_spec, o_spec],
        out_specs=(q_spec, acc_spec(geo.n_lat, HEAD_PAD), acc_spec(geo.n_ctx, HEAD_PAD), acc_spec(geo.n_lat, V_HEAD),
                   acc_spec(geo.n_ctx, V_HEAD)),
        out_shape=(jax.ShapeDtypeStruct((t, HEADS * HEAD_PAD), BF16),
                   jax.ShapeDtypeStruct((2 * geo.n_lat, HEADS * HEAD_PAD), BF16),
                   jax.ShapeDtypeStruct((2 * geo.n_ctx, HEADS * HEAD_PAD), BF16),
                   jax.ShapeDtypeStruct((2 * geo.n_lat, HEADS * V_HEAD), BF16),
                   jax.ShapeDtypeStruct((2 * geo.n_ctx, HEADS * V_HEAD), BF16)),
        scratch_shapes=[pltpu.VMEM((geo.n_lat, HEAD_PAD), F32), pltpu.VMEM((geo.n_ctx, HEAD_PAD), F32),
                        pltpu.VMEM((geo.n_lat, V_HEAD), F32), pltpu.VMEM((geo.n_ctx, V_HEAD), F32)],
        compiler_params=_params("parallel", "parallel", "arbitrary"),
    )(q, k, k, kv, kv, do)


def _mla_fwd(h, g, mod4, w, with_ctx_q, tabs, geo, tag, hosts, got):
    cos, sin = tabs
    ql, kl = w["g_qa"].shape[1], w["g_kva"].shape[1]
    kr_col = (ql + kl) // LANE
    nx = _pre_fwd(h, g, mod4, 3, geo, f"{tag}_pre")
    down = _mm(nx, w["w_a"], name=f"{tag}_down")
    cqn, ckvn = _latent_norm_fwd(down, w["g_qa"], w["g_kva"], geo, f"{tag}_lnorm")
    qraw = _mm(cqn, w["w_uq"], out_dtype=BF16, name=f"{tag}_uq")
    kvraw = _mm(ckvn, w["w_ukv"], out_dtype=BF16, name=f"{tag}_ukv")
    q = _qk_fwd(qraw, qraw, HEADS, False, w["gq_n"], w["gq_r"], cos, sin, geo, f"{tag}_qnorm")
    k = _qk_fwd(kvraw, down, kr_col, True, w["gk_n"], w["gk_r"], cos, sin, geo, f"{tag}_knorm")
    o = _with_host(_attn_fwd, hosts, got, "mix_a", q, k, kvraw, with_ctx_q, geo, f"{tag}_attn")
    h_out, y = _mm(o, w["w_o"], name=f"{tag}_o", gate=(h, mod4, 5, 1.0, geo))
    return h_out, (h, nx, down, cqn, ckvn, qraw, kvraw, q, k, o, y)


def _mla_bwd(dh_out, saved, g, mod4, w, with_ctx_q, tabs, geo, tag, hosts, got):
    cos, sin = tabs
    h, nx, down, cqn, ckvn, qraw, kvraw, q, k, o, y = saved
    ql, kl = w["g_qa"].shape[1], w["g_kva"].shape[1]
    kr_col = (ql + kl) // LANE
    dy, dgate = _gate_bwd(dh_out, y, mod4, 5, 1.0, geo, f"{tag}_dgate")
    do = _mm(dy, w["w_o"], tb=True, out_dtype=BF16, name=f"{tag}_do")
    dw_o = _tn_wide(o, dy, f"{tag}_dwo")
    dq, dk_lat, dk_ctx, dv_lat, dv_ctx = _with_host(_attn_bwd, hosts, got, "mix_a", q, k, kvraw, do, with_ctx_q, geo,
                                                    f"{tag}_dattn")
    dk = jnp.concatenate([dk_lat, dk_ctx], axis=0)
    dqn, dqr, dgq_n, dgq_r = _qk_bwd(qraw, qraw, HEADS, False, w["gq_n"], w["gq_r"], cos, sin, dq, geo, f"{tag}_dqnorm")
    dkn, dkr, dgk_n, dgk_r = _qk_bwd(kvraw, down, kr_col, True, w["gk_n"], w["gk_r"], cos, sin, dk, geo, f"{tag}_dknorm")
    dqraw = jnp.concatenate([dqn, dqr], axis=1)
    dkvraw = jnp.concatenate([dkn, jnp.concatenate([dv_lat, dv_ctx], axis=0)], axis=1)
    dcqn = _mm(dqraw, w["w_uq"], tb=True, out_dtype=BF16, name=f"{tag}_dcqn")
    dw_uq = _tn_wide(cqn, dqraw, f"{tag}_dwuq")
    dckvn = _mm(dkvraw, w["w_ukv"], tb=True, out_dtype=BF16, name=f"{tag}_dckvn")
    dw_ukv = _tn_wide(ckvn, dkvraw, f"{tag}_dwukv")
    ddown, dg_qa, dg_kva = _latent_norm_bwd(down, w["g_qa"], w["g_kva"], dcqn, dckvn, dkr, geo, f"{tag}_dlnorm")
    dnx = _mm(ddown, w["w_a"], tb=True, name=f"{tag}_dnx")
    dw_a = _tn_wide(nx, ddown, f"{tag}_dwa")
    dh, dg, dshift, dscale = _pre_bwd(h, g, mod4, 3, dnx, dh_out, geo, f"{tag}_dpre")
    grads = dict(w_a=dw_a, g_qa=dg_qa, w_uq=dw_uq, g_kva=dg_kva, w_ukv=dw_ukv, gq_n=dgq_n, gq_r=dgq_r, gk_n=dgk_n,
                 gk_r=dgk_r, w_o=dw_o)
    return dh, dg, (dshift, dscale, dgate), grads


def _mla_prepare(w_a, g_qa, w_uq, g_kva, w_ukv, g_q, g_k, w_o):
    ql, kl = g_qa.shape[0], g_kva.shape[0]
    d = w_a.shape[0]
    w_a_pad = jnp.concatenate([w_a[:, :ql + kl], _rope_swap(w_a[:, ql + kl:]), jnp.zeros((d, LANE - QK_ROPE), w_a.dtype)], axis=1)
    uq = w_uq.reshape(ql, HEADS, QK_HEAD)
    uq_r = jnp.pad(_rope_swap(uq[:, :, QK_NOPE:]), ((0, 0), (0, 0), (0, LANE - QK_ROPE)))
    w_uq_pad = jnp.concatenate([uq[:, :, :QK_NOPE].reshape(ql, HEADS * LANE), uq_r.reshape(ql, HEADS * LANE)], axis=1)
    ukv = w_ukv.reshape(kl, HEADS, QK_NOPE + V_HEAD)
    w_ukv_p = jnp.concatenate([ukv[:, :, :QK_NOPE].reshape(kl, HEADS * LANE), ukv[:, :, QK_NOPE:].reshape(kl, HEADS * V_HEAD)], axis=1)

    def gains(gv):
        gv = gv.astype(F32)
        return gv[None, :QK_NOPE], jnp.pad(_rope_swap(gv[QK_NOPE:]), (0, LANE - QK_ROPE))[None]

    gq_n, gq_r = gains(g_q)
    gk_n, gk_r = gains(g_k)
    return dict(w_a=w_a_pad, g_qa=g_qa.astype(F32)[None], w_uq=w_uq_pad, g_kva=g_kva.astype(F32)[None], w_ukv=w_ukv_p,
                gq_n=gq_n, gq_r=gq_r, gk_n=gk_n, gk_r=gk_r, w_o=w_o)


def _mla_unprepare(gr):
    ql, kl = gr["g_qa"].shape[1], gr["g_kva"].shape[1]
    dw_a = jnp.concatenate([gr["w_a"][:, :ql + kl], _rope_swap(gr["w_a"][:, ql + kl:ql + kl + QK_ROPE])], axis=1)
    uqn = gr["w_uq"][:, :HEADS * LANE].reshape(ql, HEADS, LANE)
    uqr = _rope_swap(gr["w_uq"][:, HEADS * LANE:].reshape(ql, HEADS, LANE)[:, :, :QK_ROPE])
    dw_uq = jnp.concatenate([uqn, uqr], axis=2).reshape(ql, HEADS * QK_HEAD)
    ukn = gr["w_ukv"][:, :HEADS * LANE].reshape(kl, HEADS, LANE)
    ukv = gr["w_ukv"][:, HEADS * LANE:].reshape(kl, HEADS, V_HEAD)
    dw_ukv = jnp.concatenate([ukn, ukv], axis=2).reshape(kl, HEADS * (QK_NOPE + V_HEAD))

    def gains(gn, grr):
        return jnp.concatenate([gn[0], _rope_swap(grr[0, :QK_ROPE])])

    return dict(mla_w_a=dw_a, mla_g_qa=gr["g_qa"][0], mla_w_uq=dw_uq, mla_g_kva=gr["g_kva"][0], mla_w_ukv=dw_ukv,
                mla_g_q=gains(gr["gq_n"], gr["gq_r"]), mla_g_k=gains(gr["gk_n"], gr["gk_r"]), mla_w_o=gr["w_o"])


def _loss_head(h, target, geo, name):
    t, d = h.shape
    tile = geo.tile
    n_lat_tiles = 2 * geo.n_lat // tile

    def body(h_ref, t_ref, dh_ref, loss_ref):
        i = pl.program_id(0)

        @pl.when(i == 0)
        def _():
            loss_ref[...] = jnp.zeros_like(loss_ref)

        @pl.when(i < n_lat_tiles)
        def _():
            e = h_ref[...] - t_ref[...]
            dh_ref[...] = e * (1.0 / d)
            part = jnp.sum(e * e, axis=0, keepdims=True) * (0.5 / d)
            loss_ref[...] += sum(part[:, j * LANE:(j + 1) * LANE] for j in range(d // LANE))

        @pl.when(i >= n_lat_tiles)
        def _():
            dh_ref[...] = jnp.zeros_like(dh_ref)

    row = pl.BlockSpec((tile, d), lambda i: (i, 0))
    tgt = pl.BlockSpec((tile, d), lambda i: (jnp.minimum(i, n_lat_tiles - 1), 0))
    dh, loss = pl.pallas_call(
        body, name=name, grid=(t // tile,), in_specs=[row, tgt],
        out_specs=(row, pl.BlockSpec((1, LANE), lambda i: (0, 0))),
        out_shape=(jax.ShapeDtypeStruct((t, d), F32), jax.ShapeDtypeStruct((1, LANE), F32)),
        compiler_params=_params("arbitrary"),
    )(h, target)
    return jnp.sum(loss), dh


def _adamw(w, g, m, v, name):
    shape = w.shape
    cols = shape[-1]
    rows = int(np.prod(shape[:-1])) if len(shape) > 1 else 1
    w2, g2, m2, v2 = (a.reshape(rows, cols) for a in (w, g, m, v))
    tr = _pick(rows, (512, 256, 128, 64, 32, 16, 8))
    c1 = 1.0 / (1.0 - ADAM_B1 ** ADAM_STEP)
    c2 = 1.0 / (1.0 - ADAM_B2 ** ADAM_STEP)

    def body(w_ref, g_ref, m_ref, v_ref, d_ref, mo_ref, vo_ref):
        gv = g_ref[...]
        mn = ADAM_B1 * m_ref[...] + (1.0 - ADAM_B1) * gv
        vn = ADAM_B2 * v_ref[...] + (1.0 - ADAM_B2) * (gv * gv)
        d_ref[...] = -ADAM_LR * ((mn * c1) / (jnp.sqrt(vn * c2) + ADAM_EPS) + ADAM_WD * w_ref[...])
        mo_ref[...] = mn
        vo_ref[...] = vn

    blk = pl.BlockSpec((tr, cols), lambda i: (i, 0))
    sds = jax.ShapeDtypeStruct((rows, cols), F32)
    d, mo, vo = pl.pallas_call(
        body, name=name, grid=(rows // tr,), in_specs=[blk] * 4, out_specs=(blk,) * 3, out_shape=(sds,) * 3,
        compiler_params=_params("parallel"),
    )(w2, g2, m2, v2)
    return d.reshape(shape), mo.reshape(shape), vo.reshape(shape)


SHARD_AXIS = {
    "w_mod": 2, "g_norm": 2, "ffn_w1": 3, "ffn_w3": 3, "ffn_w2": 2, "sc_w_in": 2, "sc_conv": 2, "sc_w_out": 1,
    "mla_w_a": 1, "mla_g_qa": 1, "mla_w_uq": 2, "mla_w_ukv": 2, "mla_w_o": 1,
}
HIDDEN_MAJOR = ("ffn_w1", "ffn_w3")


def _view(name, arr, swapped=False):
    form, swap, _ = EXCHANGE[name]
    if swap and not swapped:
        arr = jnp.swapaxes(arr, -1, -2)
    if form == "mid":
        arr = arr.reshape((-1,) + arr.shape[-2:])
        return jnp.pad(arr, ((0, 0), (0, 0), (0, -arr.shape[-1] % LANE)))
    arr = arr.reshape(-1, arr.shape[-1])
    return jnp.pad(arr, ((0, -arr.shape[0] % 16), (0, 0)))


def _unview(name, view, shape, keep_swapped=False):
    form, swap, _ = EXCHANGE[name]
    shape = shape[:-2] + (shape[-1], shape[-2]) if swap else shape
    if form == "mid":
        view = view[:, :, :shape[-1]]
    else:
        view = view[:int(np.prod(shape[:-1]))]
    arr = view.reshape(shape)
    return arr if (not swap or keep_swapped) else jnp.swapaxes(arr, -1, -2)


def _full_shape(name, local_shape):
    ax = SHARD_AXIS[name]
    return local_shape[:ax] + (N_DEV * local_shape[ax],) + local_shape[ax + 1:]


def _win(ref, form, n, j):
    start = j * n
    if not isinstance(start, int):
        start = pl.multiple_of(start, LANE if form == "last" else math.gcd(n, 16))
    if form == "mid":
        return ref.at[:, pl.ds(start, n), :]
    return ref.at[:, pl.ds(start, n)]


def _windows(view, count, of):
    return view.shape[:1] + (view.shape[1] * count // of,) + view.shape[2:]


def _gather_work(views, forms):
    na = len(views)

    def plan(x_refs, out_refs, sems):
        send_sems, recv_sems, local_sems = sems
        x, y, c = lax.axis_index("x"), lax.axis_index("y"), lax.axis_index("c")
        me, sibling = (x, y, c), (x, y, 1 - c)
        chips = [(1 - x, y), (x, 1 - y), (1 - x, 1 - y)]

        def copy(a, k, block, to, from_input):
            dst = _win(out_refs[a], forms[a], views[a].shape[1], 4 * block[0] + 2 * block[1] + block[2])
            return pltpu.make_async_remote_copy(
                src_ref=x_refs[a] if from_input else dst, dst_ref=dst, send_sem=send_sems.at[a, k],
                recv_sem=recv_sems.at[a, k], device_id=to, device_id_type=MESH)

        mine = [pltpu.make_async_copy(x_refs[a], _win(out_refs[a], forms[a], views[a].shape[1], 4 * x + 2 * y + c),
                                      local_sems.at[a]) for a in range(na)]
        first = []
        for a in range(na):
            first.append(copy(a, 0, me, sibling, True))
            first += [copy(a, 1 + j, me, (*chip, c), True) for j, chip in enumerate(chips)]
        return copy, mine, first, me, sibling, chips, c

    def start(x_refs, out_refs, sems):
        _, mine, first, *_ = plan(x_refs, out_refs, sems)
        for cp in mine + first:
            cp.start()

    def finish(x_refs, out_refs, sems):
        copy, mine, first, me, sibling, chips, c = plan(x_refs, out_refs, sems)
        passed = []
        for j, chip in enumerate(chips):
            for a in range(na):
                copy(a, 1 + j, (*chip, c), me, False).wait_recv()
                fwd = copy(a, 4 + j, (*chip, c), sibling, False)
                fwd.start()
                passed.append(fwd)
        for a in range(na):
            copy(a, 0, sibling, me, False).wait_recv()
            for j, chip in enumerate(chips):
                copy(a, 4 + j, (*chip, 1 - c), me, False).wait_recv()
        for cp in first + passed:
            cp.wait_send()
        for cp in mine:
            cp.wait()

    return Hosted(
        list(views), [jax.ShapeDtypeStruct(_windows(v, N_DEV, 1), v.dtype) for v in views],
        [pltpu.SemaphoreType.DMA((na, 7)), pltpu.SemaphoreType.DMA((na, 7)), pltpu.SemaphoreType.DMA((na,))], start, finish)


def _push_work(srcs, out_shapes, n_copies, make_copies):
    na = len(srcs)

    def start(s_refs, r_refs, sems):
        for cp in make_copies(s_refs, r_refs, sems[0], sems[1]):
            cp.start()

    def finish(s_refs, r_refs, sems):
        copies = make_copies(s_refs, r_refs, sems[0], sems[1])
        for cp in copies:
            cp.wait_recv()
        for cp in copies:
            cp.wait_send()

    return Hosted(list(srcs), out_shapes, [pltpu.SemaphoreType.DMA((na, n_copies)), pltpu.SemaphoreType.DMA((na, n_copies))],
                  start, finish)


def _sibling_work(fulls, forms):
    na = len(fulls)
    widths = [f.shape[1] // N_DEV for f in fulls]

    def make_copies(g_refs, r_refs, send_sems, recv_sems):
        x, y, c = lax.axis_index("x"), lax.axis_index("y"), lax.axis_index("c")
        return [
            pltpu.make_async_remote_copy(
                src_ref=_win(g_refs[a], forms[a], widths[a], 2 * chip + (1 - c)),
                dst_ref=_win(r_refs[a], forms[a], widths[a], chip), send_sem=send_sems.at[a, chip],
                recv_sem=recv_sems.at[a, chip], device_id=(x, y, 1 - c), device_id_type=MESH)
            for a in range(na) for chip in range(N_CHIP)
        ]

    return _push_work(fulls, [jax.ShapeDtypeStruct(_windows(f, N_CHIP, N_DEV), f.dtype) for f in fulls], N_CHIP, make_copies)


def _chip_work(parts, forms):
    na = len(parts)
    widths = [p.shape[1] // N_CHIP for p in parts]

    def make_copies(p_refs, r_refs, send_sems, recv_sems):
        x, y, c = lax.axis_index("x"), lax.axis_index("y"), lax.axis_index("c")
        chips = [(1 - x, y), (x, 1 - y), (1 - x, 1 - y)]
        return [
            pltpu.make_async_remote_copy(
                src_ref=_win(p_refs[a], forms[a], widths[a], 2 * px + py), dst_ref=_win(r_refs[a], forms[a], widths[a], j),
                send_sem=send_sems.at[a, j], recv_sem=recv_sems.at[a, j], device_id=(px, py, c), device_id_type=MESH)
            for a in range(na) for j, (px, py) in enumerate(chips)
        ]

    return _push_work(parts, [jax.ShapeDtypeStruct(_windows(p, 3, N_CHIP), p.dtype) for p in parts], 3, make_copies)


def _sum_tiles(view, form, n):
    if form == "mid":
        tr = n
        while tr * view.shape[2] * 4 > 2 * 1024 * 1024 and tr % 32 == 0:
            tr //= 2
        return 1, tr
    return _pick(view.shape[0], (512, 256, 128, 64, 32, 16)), n


def _window_spec(form, tl, tr, rest, window_of):
    if form == "mid":
        return lambda per: pl.BlockSpec((None, tr) + rest, lambda l, k, i, s: (l, window_of(k, s) * per + i, 0))
    return lambda per: pl.BlockSpec((tl, tr), lambda l, k, i, s: (l, window_of(k, s)))


def _chip_partials(g, recv, core, form, name):
    n = g.shape[1] // N_DEV
    tl, tr = _sum_tiles(g, form, n)
    per = n // tr
    rest = tuple(g.shape[2:])

    def body(core_ref, g_ref, r_ref, o_ref):
        o_ref[...] = (g_ref[...] + r_ref[...]).astype(o_ref.dtype)

    own = _window_spec(form, tl, tr, rest, lambda k, s: 2 * k + s[0])(per)
    by_chip = _window_spec(form, tl, tr, rest, lambda k, s: k)(per)
    return pl.pallas_call(
        body, name=name,
        grid_spec=pltpu.PrefetchScalarGridSpec(
            num_scalar_prefetch=1, grid=(g.shape[0] // tl, N_CHIP, per), in_specs=[own, by_chip], out_specs=by_chip),
        out_shape=jax.ShapeDtypeStruct(recv.shape, BF16), compiler_params=_params("parallel", "parallel", "parallel"),
    )(core, g, recv)


def _reduce_final(p, recv, chip, form, name):
    n = p.shape[1] // N_CHIP
    tl, tr = _sum_tiles(p, form, n)
    per = n // tr
    rest = tuple(p.shape[2:])

    def body(chip_ref, p_ref, ry_ref, rx_ref, rxy_ref, o_ref):
        own_pair = p_ref[...].astype(F32) + ry_ref[...].astype(F32)
        o_ref[...] = own_pair + (rx_ref[...].astype(F32) + rxy_ref[...].astype(F32))

    def rel(j):
        return _window_spec(form, tl, tr, rest, lambda k, s: j)(per)

    own = _window_spec(form, tl, tr, rest, lambda k, s: s[0])(per)
    return pl.pallas_call(
        body, name=name,
        grid_spec=pltpu.PrefetchScalarGridSpec(
            num_scalar_prefetch=1, grid=(p.shape[0] // tl, 1, per), in_specs=[own, rel(1), rel(0), rel(2)],
            out_specs=rel(0)),
        out_shape=jax.ShapeDtypeStruct(p.shape[:1] + (n,) + p.shape[2:], F32),
        compiler_params=_params("parallel", "parallel", "parallel"),
    )(chip, p, recv, recv, recv)


def _pack_replicated(arrays):
    pieces = []
    for a in arrays:
        flat = a.reshape(-1).astype(F32)
        pieces.append(jnp.pad(flat, (0, -flat.size % LANE)))
    total = sum(p.size for p in pieces)
    pieces.append(jnp.zeros((-total % (16 * LANE),), F32))
    return jnp.concatenate(pieces).reshape(-1, LANE)


def _unpack_replicated(buf, shapes):
    flat, out, off = buf.reshape(-1), [], 0
    for shape in shapes:
        size = int(np.prod(shape))
        out.append(flat[off:off + size].reshape(shape))
        off += size + (-size % LANE)
    return out


def _silu(v):
    return v * jax.nn.sigmoid(v)


SC_NAMES = ("sc_w_in", "sc_conv", "sc_w_out")
MLA_SHARDED = ("mla_w_a", "mla_g_qa", "mla_w_uq", "mla_w_ukv", "mla_w_o")
MLA_NAMES = ("mla_w_a", "mla_g_qa", "mla_w_uq", "mla_g_kva", "mla_w_ukv", "mla_g_q", "mla_g_k", "mla_w_o")


def _local_step(src, x, c, ctx, target):
    bsz, n_lat, d = x.shape
    n_ctx = ctx.shape[1]
    assert bsz == 2
    geo = Geo(n_lat, n_ctx)
    depth = src.depth
    tc = _conv_tile(d)
    tabs = _rope_tables(geo)

    h = jnp.concatenate([x.reshape(2 * n_lat, d), ctx.reshape(2 * n_ctx, d)], axis=0)
    tgt = target.reshape(2 * n_lat, d)
    cond = jnp.concatenate([c, src.c_ctx[None], jnp.zeros((8 - bsz - 1, d), F32)], axis=0)
    scond = _silu(cond)

    saved = []
    for i in range(depth):
        kind = i % 2
        wl, slots = src.weights(i), src.fwd_slots(i)
        gn = wl["g_norm"].astype(F32)
        mod = _mm(scond, wl["w_mod"], name=f"l{i}_mod") + wl["b_mod"][None]
        mod4 = mod[:N_SEG].reshape(N_SEG, N_MOD, 1, d)
        h, s1 = _ffn_fwd(h, gn[0:1], mod4, 0, wl, 0, geo, f"l{i}_f1", "f1", slots, slots)
        if kind == 0:
            mix = (_interleave(wl["sc_w_in"], 3, tc), wl["sc_conv"].astype(F32), wl["sc_w_out"])
            h, s2 = _sconv_fwd(h, gn[1:2], mod4, *mix, geo, f"l{i}_sc", slots, slots)
        else:
            mix = _mla_prepare(*[wl[name] for name in MLA_NAMES])
            h, s2 = _mla_fwd(h, gn[1:2], mod4, mix, i != depth - 1, tabs, geo, f"l{i}_mla", slots, slots)
        h, s3 = _ffn_fwd(h, gn[2:3], mod4, 6, wl, 1, geo, f"l{i}_f2", "f2", slots, slots)
        saved.append((wl, gn, mod4, mix, s1, s2, s3))

    loss, dh = _loss_head(h, tgt, geo, "loss_head")

    g_b_mod = [None] * depth
    dscond = jnp.zeros_like(scond)
    for i in reversed(range(depth)):
        kind = i % 2
        wl, gn, mod4, mix, s1, s2, s3 = saved[i]
        slots = src.bwd_slots(i)
        gbuf = {name: lax.empty(wl[name].shape, F32) for name in ("ffn_w1", "ffn_w3", "ffn_w2")}
        dh, dg2, dm2 = _ffn_bwd(dh, s3, gn[2:3], mod4, 6, wl, 1, gbuf, geo, f"l{i}_f2", "f2", slots, slots)
        if kind == 0:
            dh, dg1, dm1, dwin, dconv, dwout = _sconv_bwd(dh, s2, gn[1:2], mod4, *mix, geo, f"l{i}_sc", slots, slots)
            gl = dict(sc_w_in=_deinterleave(dwin, 3, tc), sc_conv=dconv, sc_w_out=dwout)
        else:
            dh, dg1, dm1, gm = _mla_bwd(dh, s2, gn[1:2], mod4, mix, i != depth - 1, tabs, geo, f"l{i}_mla", slots, slots)
            gl = _mla_unprepare(gm)
        dh, dg0, dm0 = _ffn_bwd(dh, s1, gn[0:1], mod4, 0, wl, 0, gbuf, geo, f"l{i}_f1", "f1", slots, slots)
        dmod = jnp.concatenate(list(dm0) + list(dm1) + list(dm2), axis=1).reshape(N_SEG, N_MOD * d)
        dmod8 = jnp.concatenate([dmod, jnp.zeros((8 - N_SEG, N_MOD * d), F32)], axis=0)
        g_b_mod[i] = jnp.sum(dmod, axis=0)
        gl.update(gbuf, g_norm=jnp.concatenate([dg0, dg1, dg2], axis=0),
                  w_mod=_mm(scond, dmod8, ta=True, name=f"l{i}_dwmod"))
        dscond = dscond + _mm(dmod8, wl["w_mod"], tb=True, name=f"l{i}_dcond")
        src.grads(i, gl)

    sg = jax.nn.sigmoid(cond)
    dcond = dscond * (sg * (1.0 + cond * (1.0 - sg)))
    grad_x = dh[:2 * n_lat].reshape(x.shape)
    return loss, grad_x, dcond[bsz], jnp.stack(g_b_mod)


class _Slots:
    def __init__(self, get, put):
        self.get, self._put = get, put

    def __setitem__(self, slot, outs):
        self._put(slot, outs)


FWD_PLAN = {
    0: {"f1_up": ("ffn_w1",), "f1_down": ("g_norm", "mix"), "mix_a": ("ffn_w3",), "mix_b": ("ffn_w2",), "f2_up": ("w_mod",)},
    1: {"f1_up": ("ffn_w1",), "mix_a": ("w_mod", "ffn_w3", "g_norm", "mix"), "f2_up": ("ffn_w2",)},
}
SIBLING_PLAN = {"f2_dact": ("ffn_w1", "w_mod", "g_norm", "mix"), "f2_dw2": ("ffn_w3", "ffn_w2")}
BWD_PLAN = {
    0: {"mix_c": ("ffn_w1",), "mix_a": ("w_mod",), "mix_b": ("ffn_w3",), "f1_dact": ("ffn_w2", "g_norm", "mix")},
    1: {"mix_a": ("ffn_w1", "w_mod", "ffn_w3", "ffn_w2"), "f1_dact": ("g_norm", "mix")},
}


class _Exchange:
    def __init__(self, w):
        self.w = w
        self.depth = w["w_mod"].shape[0]
        self.c_ctx = w["c_ctx"]
        self.core = lax.axis_index("c").astype(jnp.int32).reshape(1)
        self.chip = (2 * lax.axis_index("x") + lax.axis_index("y")).astype(jnp.int32).reshape(1)
        self.full, self.gviews, self.parts, self.reduced, self.rep = {}, {}, {}, {}, {}

    def _layer_of(self, name, i):
        return i // 2 if name.startswith(("sc_", "mla_")) else i

    def _mixer(self, i):
        return SC_NAMES if i % 2 == 0 else MLA_SHARDED

    def _expand(self, names, i):
        out = []
        for name in names:
            out += list(self._mixer(i)) if name == "mix" else [name]
        return out

    def _group(self, i):
        return ["w_mod", "g_norm", "ffn_w1", "ffn_w3", "ffn_w2"] + list(self._mixer(i))

    def _local(self, name, i):
        arr = self.w[name][self._layer_of(name, i)]
        return arr[:, None] if name == "mla_g_qa" else arr

    def _shapes(self, name, i):
        local = tuple(self._local(name, i).shape)
        ax = SHARD_AXIS[name] - 1
        return local, local[:ax] + (N_DEV * local[ax],) + local[ax + 1:]

    def _gather(self, names, i):
        views = [_view(n, self._local(n, i).astype(BF16 if EXCHANGE[n][2] else F32)) for n in names]
        return _gather_work(views, [EXCHANGE[n][0] for n in names])

    def _gathered(self, names, i, outs):
        for name, fv in zip(names, outs):
            arr = _unview(name, fv, self._shapes(name, i)[1], keep_swapped=name in HIDDEN_MAJOR)
            self.full[name, i] = arr[:, 0] if name == "mla_g_qa" else arr

    def prefetch(self):
        names = self._group(0)
        self._gathered(names, 0, _run_hosted(self._gather(names, 0), "gather_l0"))

    def weights(self, i):
        wl = {name: self.full[name, i] for name in self._group(i)}
        wl["b_mod"] = self.w["b_mod"][i]
        if i % 2 == 1:
            for name in ("mla_g_kva", "mla_g_q", "mla_g_k"):
                wl[name] = self.w[name][i // 2]
        return wl

    def fwd_slots(self, i):
        plan = FWD_PLAN[i % 2] if i + 1 < self.depth else {}
        names = {slot: self._expand(plan[slot], i + 1) for slot in plan}
        return _Slots(lambda slot: self._gather(names[slot], i + 1) if slot in names else None,
                      lambda slot, outs: self._gathered(names[slot], i + 1, outs))

    def grads(self, i, gl):
        for name in self._group(i):
            g = gl[name][:, None] if name == "mla_g_qa" else gl[name]
            self.gviews[name, i] = _view(name, g, swapped=name in HIDDEN_MAJOR)
        for name in REPLICATED:
            if name in gl:
                self.rep[name, i // 2] = gl[name]

    def _forms(self, names):
        return [EXCHANGE[n][0] for n in names]

    def _partials(self, names, i, from_sibling):
        for name, recv in zip(names, from_sibling):
            self.parts[name, i] = _chip_partials(self.gviews[name, i], recv, self.core, EXCHANGE[name][0],
                                                 f"partial_{name}_{i}")

    def _finals(self, names, i, from_chips):
        for name, recv in zip(names, from_chips):
            rv = _reduce_final(self.parts[name, i], recv, self.chip, EXCHANGE[name][0], f"final_{name}_{i}")
            arr = _unview(name, rv, self._shapes(name, i)[0])
            self.reduced[name, i] = arr[:, 0] if name == "mla_g_qa" else arr

    def bwd_slots(self, i):
        if i + 1 >= self.depth:
            return _Slots(lambda slot: None, None)
        plan = BWD_PLAN[i % 2]
        names = {slot: self._expand(plan[slot], i + 1) for slot in plan}
        sibling = {slot: self._expand(SIBLING_PLAN[slot], i + 1) for slot in SIBLING_PLAN}

        def get(slot):
            if slot in sibling:
                return _sibling_work([self.gviews[n, i + 1] for n in sibling[slot]], self._forms(sibling[slot]))
            if slot in names:
                return _chip_work([self.parts[n, i + 1] for n in names[slot]], self._forms(names[slot]))
            return None

        def put(slot, outs):
            if slot in sibling:
                self._partials(sibling[slot], i + 1, outs)
            else:
                self._finals(names[slot], i + 1, outs)

        return _Slots(get, put)

    def finish(self, rep_grads):
        group = self._group(0)
        for name in REPLICATED:
            if name not in rep_grads:
                rep_grads[name] = jnp.stack([self.rep[name, j] for j in range(self.w[name].shape[0])])
        rep = _pack_replicated([rep_grads[name] for name in REPLICATED])
        views = [self.gviews[n, 0] for n in group] + [jnp.tile(rep[None], (1, N_DEV, 1))]
        forms = self._forms(group) + ["mid"]
        from_sibling = _run_hosted(_sibling_work(views, forms), "reduce_sibling_l0")
        self._partials(group, 0, from_sibling[:-1])
        rep_part = _chip_partials(views[-1], from_sibling[-1], self.core, "mid", "partial_replicated")
        parts = [self.parts[n, 0] for n in group] + [rep_part]
        from_chips = _run_hosted(_chip_work(parts, forms), "reduce_chips_l0")
        self._finals(group, 0, from_chips[:-1])
        rep_sum = _reduce_final(rep_part, from_chips[-1], self.chip, "mid", "final_replicated")
        out = dict(zip(REPLICATED, _unpack_replicated(rep_sum, [self.w[name].shape for name in REPLICATED])))
        for name in EXCHANGE:
            layers = range(self.w[name].shape[0])
            step = 2 if name.startswith(("sc_", "mla_")) else 1
            first = 1 if name.startswith("mla_") else 0
            out[name] = jnp.stack([self.reduced[name, first + step * l] for l in layers])
        return out


def kernel(x, c, ctx, c_ctx, w_mod, b_mod, g_norm, ffn_w1, ffn_w3, ffn_w2, sc_w_in, sc_conv, sc_w_out, mla_w_a, mla_g_qa, mla_w_uq, mla_g_kva, mla_w_ukv, mla_g_q, mla_g_k, mla_w_o, loss_target, m_c_ctx, m_w_mod, m_b_mod, m_g_norm, m_ffn_w1, m_ffn_w3, m_ffn_w2, m_sc_w_in, m_sc_conv, m_sc_w_out, m_mla_w_a, m_mla_g_qa, m_mla_w_uq, m_mla_g_kva, m_mla_w_ukv, m_mla_g_q, m_mla_g_k, m_mla_w_o, v_c_ctx, v_w_mod, v_b_mod, v_g_norm, v_ffn_w1, v_ffn_w3, v_ffn_w2, v_sc_w_in, v_sc_conv, v_sc_w_out, v_mla_w_a, v_mla_g_qa, v_mla_w_uq, v_mla_g_kva, v_mla_w_ukv, v_mla_g_q, v_mla_g_k, v_mla_w_o):
    w = dict(c_ctx=c_ctx, w_mod=w_mod, b_mod=b_mod, g_norm=g_norm, ffn_w1=ffn_w1, ffn_w3=ffn_w3, ffn_w2=ffn_w2,
             sc_w_in=sc_w_in, sc_conv=sc_conv, sc_w_out=sc_w_out, mla_w_a=mla_w_a, mla_g_qa=mla_g_qa, mla_w_uq=mla_w_uq,
             mla_g_kva=mla_g_kva, mla_w_ukv=mla_w_ukv, mla_g_q=mla_g_q, mla_g_k=mla_g_k, mla_w_o=mla_w_o)
    m = dict(c_ctx=m_c_ctx, w_mod=m_w_mod, b_mod=m_b_mod, g_norm=m_g_norm, ffn_w1=m_ffn_w1, ffn_w3=m_ffn_w3,
             ffn_w2=m_ffn_w2, sc_w_in=m_sc_w_in, sc_conv=m_sc_conv, sc_w_out=m_sc_w_out, mla_w_a=m_mla_w_a,
             mla_g_qa=m_mla_g_qa, mla_w_uq=m_mla_w_uq, mla_g_kva=m_mla_g_kva, mla_w_ukv=m_mla_w_ukv, mla_g_q=m_mla_g_q,
             mla_g_k=m_mla_g_k, mla_w_o=m_mla_w_o)
    v = dict(c_ctx=v_c_ctx, w_mod=v_w_mod, b_mod=v_b_mod, g_norm=v_g_norm, ffn_w1=v_ffn_w1, ffn_w3=v_ffn_w3,
             ffn_w2=v_ffn_w2, sc_w_in=v_sc_w_in, sc_conv=v_sc_conv, sc_w_out=v_sc_w_out, mla_w_a=v_mla_w_a,
             mla_g_qa=v_mla_g_qa, mla_w_uq=v_mla_w_uq, mla_g_kva=v_mla_g_kva, mla_w_ukv=v_mla_w_ukv, mla_g_q=v_mla_g_q,
             mla_g_k=v_mla_g_k, mla_w_o=v_mla_w_o)
    exchange = _Exchange(w)
    exchange.prefetch()
    loss, grad_x, g_c_ctx, g_b_mod = _local_step(exchange, x, c, ctx, loss_target)
    loss = lax.psum(loss, ("x", "y", "c"))
    reduced = exchange.finish(dict(c_ctx=g_c_ctx, b_mod=g_b_mod))

    outs = [[], [], [], []]
    for name in WEIGHTS:
        delta, new_m, new_v = _adamw(w[name], reduced[name], m[name], v[name], f"adamw_{name}")
        for lst, val in zip(outs, (reduced[name], delta, new_m, new_v)):
            lst.append(val)
    return (loss, grad_x, *outs[0], *outs[1], *outs[2], *outs[3])
```

```python
import functools
import math

import jax
import jax.numpy as jnp
import numpy as np
from jax import lax
from jax.experimental import pallas as pl
from jax.experimental.pallas import tpu as pltpu

F32 = jnp.float32
BF16 = jnp.bfloat16

N_MOD = 9
HEADS = 8
QK_NOPE = 128
QK_ROPE = 64
QK_HEAD = QK_NOPE + QK_ROPE
V_HEAD = 128
GRID_W = 64
ROPE_BASE = 10000.0
QK_SCALE = QK_HEAD ** -0.5
EPS = 1e-6
ADAM_LR, ADAM_B1, ADAM_B2, ADAM_EPS, ADAM_WD, ADAM_STEP = 0.001, 0.9, 0.999, 1e-08, 0.01, 10

N_DEV = 8
N_CHIP = 4
N_SEG = 3
LANE = 128
HEAD_PAD = 2 * LANE
VMEM_LIMIT_BYTES = 48 * 1024 * 1024
MESH = pl.DeviceIdType.MESH

WEIGHTS = ["c_ctx", "w_mod", "b_mod", "g_norm", "ffn_w1", "ffn_w3", "ffn_w2", "sc_w_in", "sc_conv", "sc_w_out",
           "mla_w_a", "mla_g_qa", "mla_w_uq", "mla_g_kva", "mla_w_ukv", "mla_g_q", "mla_g_k", "mla_w_o"]
EXCHANGE = {
    "w_mod": ("last", False, True), "ffn_w1": ("mid", True, True), "ffn_w3": ("mid", True, True),
    "ffn_w2": ("mid", False, True), "sc_w_in": ("last", False, True), "sc_w_out": ("mid", False, True),
    "mla_w_a": ("mid", False, True), "mla_w_uq": ("mid", True, True), "mla_w_ukv": ("last", False, True),
    "mla_w_o": ("mid", False, True), "g_norm": ("last", False, False), "sc_conv": ("last", False, False),
    "mla_g_qa": ("mid", False, False),
}
REPLICATED = ["c_ctx", "b_mod", "mla_g_kva", "mla_g_q", "mla_g_k"]


def _pick(n, cands):
    for cand in cands:
        if n % cand == 0:
            return cand
    return n


def _params(*sem):
    return pltpu.CompilerParams(dimension_semantics=sem, vmem_limit_bytes=VMEM_LIMIT_BYTES)


def _hbm():
    return pl.BlockSpec(memory_space=pl.ANY)


class Hosted:
    def __init__(self, inputs, out_shapes, scratch, start, finish):
        self.inputs, self.out_shapes, self.scratch, self.start, self.finish = inputs, out_shapes, scratch, start, finish


def _call(body, hosted, **kw):
    if hosted is None:
        return pl.pallas_call(body, **kw)
    single = not isinstance(kw["out_shape"], (tuple, list))
    out_shape = [kw["out_shape"]] if single else list(kw["out_shape"])
    out_specs = [kw["out_specs"]] if single else list(kw["out_specs"])
    in_specs, scratch, grid = list(kw["in_specs"]), list(kw.get("scratch_shapes", ())), kw["grid"]
    n_in, n_out, n_scr = len(in_specs), len(out_shape), len(scratch)
    h_in, h_out = len(hosted.inputs), len(hosted.out_shapes)

    def wrapped(*refs):
        ins, hins = refs[:n_in], refs[n_in:n_in + h_in]
        o0 = n_in + h_in
        outs, houts = refs[o0:o0 + n_out], refs[o0 + n_out:o0 + n_out + h_out]
        s0 = o0 + n_out + h_out
        scr, hscr = refs[s0:s0 + n_scr], refs[s0 + n_scr:]
        first = functools.reduce(jnp.logical_and, [pl.program_id(a) == 0 for a in range(len(grid))])
        last = functools.reduce(jnp.logical_and, [pl.program_id(a) == g - 1 for a, g in enumerate(grid)])

        @pl.when(first)
        def _():
            hosted.start(hins, houts, hscr)

        body(*ins, *outs, *scr)

        @pl.when(last)
        def _():
            hosted.finish(hins, houts, hscr)

    call = pl.pallas_call(
        wrapped, name=kw["name"], grid=grid, in_specs=in_specs + [_hbm()] * h_in,
        out_specs=tuple(out_specs + [_hbm()] * h_out), out_shape=tuple(out_shape + list(hosted.out_shapes)),
        scratch_shapes=scratch + list(hosted.scratch), input_output_aliases=kw.get("input_output_aliases", {}),
        compiler_params=_params(*["arbitrary"] * len(grid)))

    def run(*args):
        res = call(*args, *hosted.inputs)
        comp = res[:n_out]
        return (comp[0] if single else tuple(comp)), list(res[n_out:])

    return run


def _run_hosted(hosted, name):
    def body(*refs):
        h_in, h_out = len(hosted.inputs), len(hosted.out_shapes)
        hins, houts, hscr = refs[:h_in], refs[h_in:h_in + h_out], refs[h_in + h_out:]
        hosted.start(hins, houts, hscr)
        hosted.finish(hins, houts, hscr)

    return list(pl.pallas_call(
        body, name=name, in_specs=[_hbm()] * len(hosted.inputs), out_specs=tuple([_hbm()] * len(hosted.out_shapes)),
        out_shape=tuple(hosted.out_shapes), scratch_shapes=list(hosted.scratch))(*hosted.inputs))


class Geo:
    def __init__(self, n_lat, n_ctx):
        self.n_lat, self.n_ctx = n_lat, n_ctx
        self.rows = 2 * n_lat + 2 * n_ctx
        self.tile = n_ctx
        assert n_lat % n_ctx == 0 and n_ctx % 16 == 0
        self.mm_tile = _pick(n_lat, (512, 256, 128)) if self.rows % _pick(n_lat, (512, 256, 128)) == 0 else n_ctx
        self.big_tile = _pick(self.rows, (1536, 768, 512, 256))

    def seg(self, i, tile):
        return jnp.minimum((i * tile) // self.n_lat, N_SEG - 1)

    def seg_start(self, i, tile):
        row = i * tile
        return jnp.logical_or(row % self.n_lat == 0, row == 2 * self.n_lat) & (row <= 2 * self.n_lat)


_NT = (((1,), (1,)), ((), ()))
_NN = (((1,), (0,)), ((), ()))
_TN = (((0,), (0,)), ((), ()))


def _dot(a, b, dims):
    return lax.dot_general(a.astype(BF16), b.astype(BF16), dims, preferred_element_type=F32)


def _mm(a, b, *, ta=False, tb=False, out_dtype=F32, name, gate=None, hosted=None):
    (kdim, m) = a.shape if ta else a.shape[::-1]
    n = b.shape[0] if tb else b.shape[1]
    assert (b.shape[1] if tb else b.shape[0]) == kdim
    if gate is not None:
        tm = gate[4].mm_tile
    else:
        tm = _pick(m, (512, 256, 128))
    tn = _pick(n, (512, 256, 128))
    tk = _pick(kdim, (1024, 512, 256, 128))
    nk = kdim // tk
    dims = (((0 if ta else 1,), (1 if tb else 0,)), ((), ()))

    def body(*refs):
        if gate is not None:
            a_ref, b_ref, res_ref, gate_ref, o_ref, y_ref, acc_ref = refs
        else:
            a_ref, b_ref, o_ref, acc_ref = refs
        kk = pl.program_id(2)

        @pl.when(kk == 0)
        def _():
            acc_ref[...] = jnp.zeros_like(acc_ref)

        acc_ref[...] += lax.dot_general(a_ref[...].astype(BF16), b_ref[...].astype(BF16), dims,
                                        preferred_element_type=F32)

        @pl.when(kk == nk - 1)
        def _():
            acc = acc_ref[...]
            if gate is not None:
                y_ref[...] = acc.astype(y_ref.dtype)
                o_ref[...] = res_ref[...] + (gate[3] * gate_ref[...]) * acc
            else:
                o_ref[...] = acc.astype(o_ref.dtype)

    a_spec = pl.BlockSpec((tk, tm), lambda i, j, k: (k, i)) if ta else pl.BlockSpec((tm, tk), lambda i, j, k: (i, k))
    b_spec = pl.BlockSpec((tn, tk), lambda i, j, k: (j, k)) if tb else pl.BlockSpec((tk, tn), lambda i, j, k: (k, j))
    o_spec = pl.BlockSpec((tm, tn), lambda i, j, k: (i, j))
    in_specs, args = [a_spec, b_spec], [a, b]
    out_shape, out_specs = jax.ShapeDtypeStruct((m, n), out_dtype), o_spec
    if gate is not None:
        res, mod4, kmod, _, geo = gate
        in_specs += [o_spec, pl.BlockSpec((None, None, 1, tn), lambda i, j, k: (geo.seg(i, tm), kmod, 0, j))]
        args += [res, mod4]
        out_shape = (jax.ShapeDtypeStruct((m, n), F32), jax.ShapeDtypeStruct((m, n), BF16))
        out_specs = (o_spec, o_spec)
    return _call(
        body, hosted, name=name, grid=(m // tm, n // tn, nk), in_specs=in_specs, out_specs=out_specs,
        out_shape=out_shape, scratch_shapes=[pltpu.VMEM((tm, tn), F32)],
        compiler_params=_params("parallel", "parallel", "arbitrary"),
    )(*args)


def _tn_wide(lhs, rhs, name, into=None, s0=0, hosted=None):
    t, m = lhs.shape
    n = rhs.shape[1]
    tm = _pick(m, (1408, 1024, 512, 256, 128))
    while tm * n * 4 > 6.5 * 1024 * 1024 and tm % 256 == 0:
        tm //= 2
    tk = next(c for c in (1536, 768, 512, 256, 128, t)
              if t % c == 0 and c * (tm + n) * 4 + tm * n * 12 <= 36 * 1024 * 1024)

    def body(l_ref, r_ref, *rest):
        o_ref = rest[-1]
        kk = pl.program_id(1)
        part = lax.dot_general(l_ref[...], r_ref[...], _TN, preferred_element_type=F32)

        @pl.when(kk == 0)
        def _():
            o_ref[...] = part

        @pl.when(kk > 0)
        def _():
            o_ref[...] += part

    in_specs = [pl.BlockSpec((tk, tm), lambda i, k: (k, i)), pl.BlockSpec((tk, n), lambda i, k: (k, 0))]
    if into is None:
        return _call(
            body, hosted, name=name, grid=(m // tm, t // tk), in_specs=in_specs,
            out_specs=pl.BlockSpec((tm, n), lambda i, k: (i, 0)), out_shape=jax.ShapeDtypeStruct((m, n), F32),
            compiler_params=_params("parallel", "arbitrary"),
        )(lhs, rhs)
    return _call(
        body, hosted, name=name, grid=(m // tm, t // tk), in_specs=in_specs + [pl.BlockSpec(memory_space=pl.ANY)],
        out_specs=pl.BlockSpec((None, tm, n), lambda i, k: (s0, i, 0)),
        out_shape=jax.ShapeDtypeStruct(into.shape, into.dtype), input_output_aliases={2: 0},
        compiler_params=_params("parallel", "arbitrary"),
    )(lhs, rhs, into)


def _mod_spec(geo, tile, kmod, d):
    return pl.BlockSpec((None, None, 1, d), lambda i: (geo.seg(i, tile), kmod, 0, 0))


def _pre_fwd(h, g, mod4, k_shift, geo, name):
    t, d = h.shape
    tile = geo.tile

    def body(h_ref, g_ref, sh_ref, sc_ref, o_ref):
        hv = h_ref[...]
        r = lax.rsqrt(jnp.mean(hv * hv, axis=-1, keepdims=True) + EPS)
        y = hv * r * g_ref[...]
        o_ref[...] = (y * (1.0 + sc_ref[...]) + sh_ref[...]).astype(o_ref.dtype)

    row = pl.BlockSpec((tile, d), lambda i: (i, 0))
    return pl.pallas_call(
        body, name=name, grid=(t // tile,),
        in_specs=[row, pl.BlockSpec((1, d), lambda i: (0, 0)), _mod_spec(geo, tile, k_shift, d),
                  _mod_spec(geo, tile, k_shift + 1, d)],
        out_specs=row, out_shape=jax.ShapeDtypeStruct((t, d), BF16), compiler_params=_params("parallel"),
    )(h, g, mod4, mod4)


def _pre_bwd(h, g, mod4, k_shift, dnx, dres, geo, name):
    t, d = h.shape
    tile = geo.tile

    def body(h_ref, g_ref, sc_ref, dnx_ref, dres_ref, dh_ref, dg_ref, dsh_ref, dsc_ref):
        i = pl.program_id(0)
        hv, gv, dout = h_ref[...], g_ref[...], dnx_ref[...].astype(F32)
        r = lax.rsqrt(jnp.mean(hv * hv, axis=-1, keepdims=True) + EPS)
        xhat = hv * r
        dy = dout * (1.0 + sc_ref[...])
        u = dy * gv
        dh_ref[...] = r * (u - xhat * jnp.mean(u * xhat, axis=-1, keepdims=True)) + dres_ref[...]

        @pl.when(i == 0)
        def _():
            dg_ref[...] = jnp.zeros_like(dg_ref)

        @pl.when(geo.seg_start(i, tile))
        def _():
            dsh_ref[...] = jnp.zeros_like(dsh_ref)
            dsc_ref[...] = jnp.zeros_like(dsc_ref)

        dg_ref[...] += jnp.sum(dy * xhat, axis=0, keepdims=True)
        dsh_ref[...] += jnp.sum(dout, axis=0, keepdims=True)
        dsc_ref[...] += jnp.sum(dout * (xhat * gv), axis=0, keepdims=True)

    row = pl.BlockSpec((tile, d), lambda i: (i, 0))
    vec = pl.BlockSpec((1, d), lambda i: (0, 0))
    segv = pl.BlockSpec((None, 1, d), lambda i: (geo.seg(i, tile), 0, 0))
    return pl.pallas_call(
        body, name=name, grid=(t // tile,),
        in_specs=[row, vec, _mod_spec(geo, tile, k_shift + 1, d), row, row],
        out_specs=(row, vec, segv, segv),
        out_shape=(jax.ShapeDtypeStruct((t, d), F32), jax.ShapeDtypeStruct((1, d), F32),
                   jax.ShapeDtypeStruct((N_SEG, 1, d), F32), jax.ShapeDtypeStruct((N_SEG, 1, d), F32)),
        compiler_params=_params("arbitrary"),
    )(h, g, mod4, dnx, dres)


def _gate_bwd(dh, y, mod4, k_gate, coef, geo, name):
    t, d = dh.shape
    tile = geo.tile

    def body(dh_ref, y_ref, gt_ref, dy_ref, dgt_ref):
        i = pl.program_id(0)
        dhv = dh_ref[...]
        dy_ref[...] = ((coef * gt_ref[...]) * dhv).astype(dy_ref.dtype)

        @pl.when(geo.seg_start(i, tile))
        def _():
            dgt_ref[...] = jnp.zeros_like(dgt_ref)

        dgt_ref[...] += coef * jnp.sum(dhv * y_ref[...].astype(F32), axis=0, keepdims=True)

    row = pl.BlockSpec((tile, d), lambda i: (i, 0))
    segv = pl.BlockSpec((None, 1, d), lambda i: (geo.seg(i, tile), 0, 0))
    return pl.pallas_call(
        body, name=name, grid=(t // tile,), in_specs=[row, row, _mod_spec(geo, tile, k_gate, d)],
        out_specs=(row, segv),
        out_shape=(jax.ShapeDtypeStruct((t, d), BF16), jax.ShapeDtypeStruct((N_SEG, 1, d), F32)),
        compiler_params=_params("arbitrary"),
    )(dh, y, mod4)


def _ff_tile(f):
    return _pick(f, (256, 128))


def _ffn_up(nx, w1t, w3t, s0, geo, name, hosted=None):
    t, d = nx.shape
    f = w1t.shape[1]
    tm, tn = geo.big_tile, _ff_tile(f)

    def body(x_ref, w1_ref, w3_ref, ga_ref, gb_ref, act_ref):
        xv = x_ref[...]
        a = lax.dot_general(xv, w1_ref[...], _NT, preferred_element_type=F32)
        bv = lax.dot_general(xv, w3_ref[...], _NT, preferred_element_type=F32)
        sg = jax.nn.sigmoid(a)
        silu = a * sg
        ga_ref[...] = (bv * (sg + silu * (1.0 - sg))).astype(ga_ref.dtype)
        gb_ref[...] = silu.astype(gb_ref.dtype)
        act_ref[...] = (silu * bv).astype(act_ref.dtype)

    w_spec = pl.BlockSpec((None, tn, d), lambda i, j: (s0, j, 0))
    o_spec = pl.BlockSpec((tm, tn), lambda i, j: (i, j))
    sds = jax.ShapeDtypeStruct((t, f), BF16)
    return _call(
        body, hosted, name=name, grid=(t // tm, f // tn),
        in_specs=[pl.BlockSpec((tm, d), lambda i, j: (i, 0)), w_spec, w_spec],
        out_specs=(o_spec,) * 3, out_shape=(sds,) * 3, compiler_params=_params("parallel", "parallel"),
    )(nx, w1t, w3t)


def _ffn_down(act, w2, s0, res, mod4, k_gate, geo, name, hosted=None):
    t, f = act.shape
    d = w2.shape[2]
    tm, tn = geo.mm_tile, _pick(d, (1024, 512, 256, 128))

    def body(a_ref, w_ref, res_ref, gate_ref, o_ref, y_ref):
        acc = lax.dot_general(a_ref[...], w_ref[...], _NN, preferred_element_type=F32)
        y_ref[...] = acc.astype(y_ref.dtype)
        o_ref[...] = res_ref[...] + (0.5 * gate_ref[...]) * acc

    o_spec = pl.BlockSpec((tm, tn), lambda i, j: (i, j))
    return _call(
        body, hosted, name=name, grid=(t // tm, d // tn),
        in_specs=[pl.BlockSpec((tm, f), lambda i, j: (i, 0)), pl.BlockSpec((None, f, tn), lambda i, j: (s0, 0, j)),
                  o_spec, pl.BlockSpec((None, None, 1, tn), lambda i, j: (geo.seg(i, tm), k_gate, 0, j))],
        out_specs=(o_spec, o_spec),
        out_shape=(jax.ShapeDtypeStruct((t, d), F32), jax.ShapeDtypeStruct((t, d), BF16)),
        compiler_params=_params("parallel", "parallel"),
    )(act, w2, res, mod4)


def _ffn_dact(dy, w2, ga, gb, s0, geo, name, hosted=None):
    t, d = dy.shape
    f = w2.shape[1]
    tm, tn = geo.big_tile, _ff_tile(f)

    def body(dy_ref, w_ref, ga_ref, gb_ref, da_ref, db_ref):
        dact = lax.dot_general(dy_ref[...], w_ref[...], _NT, preferred_element_type=F32)
        da_ref[...] = (dact * ga_ref[...].astype(F32)).astype(da_ref.dtype)
        db_ref[...] = (dact * gb_ref[...].astype(F32)).astype(db_ref.dtype)

    o_spec = pl.BlockSpec((tm, tn), lambda i, j: (i, j))
    sds = jax.ShapeDtypeStruct((t, f), BF16)
    return _call(
        body, hosted, name=name, grid=(t // tm, f // tn),
        in_specs=[pl.BlockSpec((tm, d), lambda i, j: (i, 0)), pl.BlockSpec((None, tn, d), lambda i, j: (s0, j, 0)),
                  o_spec, o_spec],
        out_specs=(o_spec, o_spec), out_shape=(sds, sds), compiler_params=_params("parallel", "parallel"),
    )(dy, w2, ga, gb)


def _ffn_dnx(da, db, w1t, w3t, s0, geo, name, hosted=None):
    t, f = da.shape
    d = w1t.shape[2]
    tm, tn = geo.mm_tile, _pick(d, (512, 256, 128))

    def body(da_ref, db_ref, w1_ref, w3_ref, o_ref):
        o_ref[...] = (lax.dot_general(da_ref[...], w1_ref[...], _NN, preferred_element_type=F32)
                      + lax.dot_general(db_ref[...], w3_ref[...], _NN, preferred_element_type=F32))

    x_spec = pl.BlockSpec((tm, f), lambda j, i: (i, 0))
    w_spec = pl.BlockSpec((None, f, tn), lambda j, i: (s0, 0, j))
    return _call(
        body, hosted, name=name, grid=(d // tn, t // tm), in_specs=[x_spec, x_spec, w_spec, w_spec],
        out_specs=pl.BlockSpec((tm, tn), lambda j, i: (i, j)), out_shape=jax.ShapeDtypeStruct((t, d), F32),
        compiler_params=_params("parallel", "parallel"),
    )(da, db, w1t, w3t)


def _with_host(fn, hosts, got, slot, *args, **kw):
    hosted = hosts.get(slot)
    if hosted is None:
        return fn(*args, **kw)
    out, got[slot] = fn(*args, hosted=hosted, **kw)
    return out


def _ffn_fwd(h, g, mod4, k0, w, s0, geo, tag, sub, hosts, got):
    nx = _pre_fwd(h, g, mod4, k0, geo, f"{tag}_pre")
    a, b, act = _with_host(_ffn_up, hosts, got, f"{sub}_up", nx, w["ffn_w1"], w["ffn_w3"], s0, geo, f"{tag}_up")
    h_out, y = _with_host(_ffn_down, hosts, got, f"{sub}_down", act, w["ffn_w2"], s0, h, mod4, k0 + 2, geo, f"{tag}_down")
    return h_out, (h, nx, a, b, act, y)


def _ffn_bwd(dh_out, saved, g, mod4, k0, w, s0, gbuf, geo, tag, sub, hosts, got):
    h, nx, a, b, act, y = saved
    dy, dgate = _gate_bwd(dh_out, y, mod4, k0 + 2, 0.5, geo, f"{tag}_dgate")
    da, db = _with_host(_ffn_dact, hosts, got, f"{sub}_dact", dy, w["ffn_w2"], a, b, s0, geo, f"{tag}_dact")
    gbuf["ffn_w2"] = _with_host(_tn_wide, hosts, got, f"{sub}_dw2", act, dy, f"{tag}_dw2", into=gbuf["ffn_w2"], s0=s0)
    dnx = _with_host(_ffn_dnx, hosts, got, f"{sub}_dnx", da, db, w["ffn_w1"], w["ffn_w3"], s0, geo, f"{tag}_dnx")
    gbuf["ffn_w1"] = _tn_wide(da, nx, f"{tag}_dw1", into=gbuf["ffn_w1"], s0=s0)
    gbuf["ffn_w3"] = _tn_wide(db, nx, f"{tag}_dw3", into=gbuf["ffn_w3"], s0=s0)
    dh, dg, dshift, dscale = _pre_bwd(h, g, mod4, k0, dnx, dh_out, geo, f"{tag}_dpre")
    return dh, dg, (dshift, dscale, dgate)


def _interleave(w, n_parts, tile):
    lead, cols = w.shape[:-1], w.shape[-1] // n_parts
    return w.reshape(*lead, n_parts, cols // tile, tile).swapaxes(-3, -2).reshape(*lead, n_parts * cols)


def _deinterleave(w, n_parts, tile):
    lead, cols = w.shape[:-1], w.shape[-1] // n_parts
    return w.reshape(*lead, cols // tile, n_parts, tile).swapaxes(-3, -2).reshape(*lead, n_parts * cols)


HALO = 16


def _conv_tile(c):
    return _pick(c, (256, 128))


def _conv_specs(geo, tc, t):
    tile = geo.tile
    per = tile // HALO
    last = t // HALO - 1
    cur = pl.BlockSpec((tile, 3 * tc), lambda j, i: (i, j))
    prev = pl.BlockSpec((HALO, 3 * tc), lambda j, i: (jnp.maximum(i * per - 1, 0), j))
    nxt = pl.BlockSpec((HALO, 3 * tc), lambda j, i: (jnp.minimum((i + 1) * per, last), j))
    return cur, prev, nxt


def _conv_edges(geo, i):
    tile = geo.tile
    row = i * tile
    lat = row < 2 * geo.n_lat
    first = jnp.where(lat, row % geo.n_lat == 0, (row - 2 * geo.n_lat) % geo.n_ctx == 0)
    end = row + tile
    last = jnp.where(lat, end % geo.n_lat == 0, (end - 2 * geo.n_lat) % geo.n_ctx == 0)
    return first, last


def _shift_rows(v, before, after):
    n = v.shape[0]
    rows = lax.broadcasted_iota(jnp.int32, v.shape, 0)
    down = jnp.where(rows == 0, before, pltpu.roll(v, 1, 0))
    up = jnp.where(rows == n - 1, after, pltpu.roll(v, n - 1, 0))
    return down, up


def _conv_fwd(proj, conv_w, geo, name, hosted=None):
    t, c3 = proj.shape
    c = c3 // 3
    tc, tile = _conv_tile(c), geo.tile

    def body(cur_ref, prev_ref, next_ref, w_ref, o_ref):
        first, last = _conv_edges(geo, pl.program_id(1))
        bv = cur_ref[:, :tc].astype(F32)
        p = cur_ref[:, tc:2 * tc].astype(F32) * cur_ref[:, 2 * tc:].astype(F32)
        p_before = prev_ref[HALO - 1:HALO, tc:2 * tc].astype(F32) * prev_ref[HALO - 1:HALO, 2 * tc:].astype(F32)
        p_after = next_ref[0:1, tc:2 * tc].astype(F32) * next_ref[0:1, 2 * tc:].astype(F32)
        p_before = jnp.where(first, 0.0, p_before)
        p_after = jnp.where(last, 0.0, p_after)
        pm1, pp1 = _shift_rows(p, p_before, p_after)
        w = w_ref[...]
        q = w[0:1] * pm1 + w[1:2] * p + w[2:3] * pp1
        o_ref[...] = (bv * q).astype(o_ref.dtype)

    cur, prev, nxt = _conv_specs(geo, tc, t)
    return _call(
        body, hosted, name=name, grid=(c // tc, t // tile),
        in_specs=[cur, prev, nxt, pl.BlockSpec((3, tc), lambda j, i: (0, j))],
        out_specs=pl.BlockSpec((tile, tc), lambda j, i: (i, j)), out_shape=jax.ShapeDtypeStruct((t, c), BF16),
        compiler_params=_params("parallel", "parallel"),
    )(proj, proj, proj, conv_w)


def _conv_bwd(proj, dyc, conv_w, geo, name, hosted=None):
    t, c3 = proj.shape
    c = c3 // 3
    tc, tile = _conv_tile(c), geo.tile

    def body(cur_ref, prev_ref, next_ref, d_ref, dprev_ref, dnext_ref, w_ref, o_ref, dw_ref):
        i = pl.program_id(1)
        first, last = _conv_edges(geo, i)
        bv = cur_ref[:, :tc].astype(F32)
        cv = cur_ref[:, tc:2 * tc].astype(F32)
        uv = cur_ref[:, 2 * tc:].astype(F32)
        p = cv * uv
        p_before = prev_ref[HALO - 1:HALO, tc:2 * tc].astype(F32) * prev_ref[HALO - 1:HALO, 2 * tc:].astype(F32)
        p_after = next_ref[0:1, tc:2 * tc].astype(F32) * next_ref[0:1, 2 * tc:].astype(F32)
        p_before = jnp.where(first, 0.0, p_before)
        p_after = jnp.where(last, 0.0, p_after)
        pm1, pp1 = _shift_rows(p, p_before, p_after)
        w = w_ref[...]
        q = w[0:1] * pm1 + w[1:2] * p + w[2:3] * pp1
        dy = d_ref[...].astype(F32)
        dq = dy * bv
        dq_before = dprev_ref[HALO - 1:HALO, :].astype(F32) * prev_ref[HALO - 1:HALO, :tc].astype(F32)
        dq_after = dnext_ref[0:1, :].astype(F32) * next_ref[0:1, :tc].astype(F32)
        dq_before = jnp.where(first, 0.0, dq_before)
        dq_after = jnp.where(last, 0.0, dq_after)
        dqm1, dqp1 = _shift_rows(dq, dq_before, dq_after)
        dp = w[0:1] * dqp1 + w[1:2] * dq + w[2:3] * dqm1
        o_ref[:, :tc] = (dy * q).astype(o_ref.dtype)
        o_ref[:, tc:2 * tc] = (dp * uv).astype(o_ref.dtype)
        o_ref[:, 2 * tc:] = (dp * cv).astype(o_ref.dtype)

        @pl.when(i == 0)
        def _():
            dw_ref[...] = jnp.zeros_like(dw_ref)

        dw_ref[0:1, :] += jnp.sum(dq * pm1, axis=0, keepdims=True)
        dw_ref[1:2, :] += jnp.sum(dq * p, axis=0, keepdims=True)
        dw_ref[2:3, :] += jnp.sum(dq * pp1, axis=0, keepdims=True)

    cur, prev, nxt = _conv_specs(geo, tc, t)
    per, lastb = tile // HALO, t // HALO - 1
    dcur = pl.BlockSpec((tile, tc), lambda j, i: (i, j))
    dprev = pl.BlockSpec((HALO, tc), lambda j, i: (jnp.maximum(i * per - 1, 0), j))
    dnext = pl.BlockSpec((HALO, tc), lambda j, i: (jnp.minimum((i + 1) * per, lastb), j))
    wspec = pl.BlockSpec((3, tc), lambda j, i: (0, j))
    return _call(
        body, hosted, name=name, grid=(c // tc, t // tile), in_specs=[cur, prev, nxt, dcur, dprev, dnext, wspec],
        out_specs=(cur, wspec), out_shape=(jax.ShapeDtypeStruct((t, c3), BF16), jax.ShapeDtypeStruct((3, c), F32)),
        compiler_params=_params("parallel", "arbitrary"),
    )(proj, proj, proj, dyc, dyc, dyc, conv_w)


def _sconv_fwd(h, g, mod4, w_in, conv_w, w_out, geo, tag, hosts, got):
    nx = _pre_fwd(h, g, mod4, 3, geo, f"{tag}_pre")
    proj = _with_host(_mm, hosts, got, "mix_a", nx, w_in, out_dtype=BF16, name=f"{tag}_in")
    yc = _with_host(_conv_fwd, hosts, got, "mix_b", proj, conv_w, geo, f"{tag}_conv")
    h_out, y = _mm(yc, w_out, name=f"{tag}_out", gate=(h, mod4, 5, 1.0, geo))
    return h_out, (h, nx, proj, yc, y)


def _sconv_bwd(dh_out, saved, g, mod4, w_in, conv_w, w_out, geo, tag, hosts, got):
    h, nx, proj, yc, y = saved
    dy, dgate = _gate_bwd(dh_out, y, mod4, 5, 1.0, geo, f"{tag}_dgate")
    dyc = _mm(dy, w_out, tb=True, out_dtype=BF16, name=f"{tag}_dyc")
    dw_out = _tn_wide(yc, dy, f"{tag}_dwout")
    dproj, dconv = _with_host(_conv_bwd, hosts, got, "mix_c", proj, dyc, conv_w, geo, f"{tag}_dconv")
    dnx = _with_host(_mm, hosts, got, "mix_b", dproj, w_in, tb=True, name=f"{tag}_dnx")
    dw_in = _with_host(_tn_wide, hosts, got, "mix_a", nx, dproj, f"{tag}_dwin")
    dh, dg, dshift, dscale = _pre_bwd(h, g, mod4, 3, dnx, dh_out, geo, f"{tag}_dpre")
    return dh, dg, (dshift, dscale, dgate), dw_in, dconv, dw_out


def _rope_swap(v):
    nf = QK_ROPE // 4
    return v.reshape(v.shape[:-1] + (2, 2, nf)).swapaxes(-3, -2).reshape(v.shape)


def _rope_tables(geo):
    n = geo.n_lat
    nf = QK_ROPE // 4
    pos = np.arange(n)
    inv = ROPE_BASE ** (-np.arange(nf, dtype=np.float32) / nf)
    ang = np.concatenate([(pos // GRID_W)[:, None] * inv, (pos % GRID_W)[:, None] * inv], axis=1).astype(np.float32)
    cos, sin = np.cos(ang), np.sin(ang)
    zeros = np.zeros((n, LANE - QK_ROPE), np.float32)
    c_lat = np.concatenate([cos, cos, zeros], axis=1)
    s_lat = np.concatenate([-sin, sin, zeros], axis=1)
    c_ctx = np.concatenate([np.ones((2 * geo.n_ctx, QK_ROPE), np.float32), np.zeros((2 * geo.n_ctx, LANE - QK_ROPE), np.float32)], 1)
    s_ctx = np.zeros((2 * geo.n_ctx, LANE), np.float32)
    return (jnp.asarray(np.concatenate([c_lat, c_lat, c_ctx], 0)), jnp.asarray(np.concatenate([s_lat, s_lat, s_ctx], 0)))


def _swap_halves(v):
    lanes = lax.broadcasted_iota(jnp.int32, v.shape, 1)
    return jnp.where(lanes < QK_ROPE // 2, pltpu.roll(v, LANE - QK_ROPE // 2, 1), pltpu.roll(v, QK_ROPE // 2, 1))


def _latent_norm_fwd(down, g_qa, g_kva, geo, name):
    t, wd = down.shape
    ql, kl = g_qa.shape[1], g_kva.shape[1]
    tile = geo.tile

    def body(d_ref, gq_ref, gk_ref, cq_ref, ckv_ref):
        for lo, n, g_ref, o_ref in ((0, ql, gq_ref, cq_ref), (ql, kl, gk_ref, ckv_ref)):
            x = d_ref[:, lo:lo + n]
            r = lax.rsqrt(jnp.mean(x * x, axis=-1, keepdims=True) + EPS)
            o_ref[...] = (x * r * g_ref[...]).astype(o_ref.dtype)

    return pl.pallas_call(
        body, name=name, grid=(t // tile,),
        in_specs=[pl.BlockSpec((tile, wd), lambda i: (i, 0)), pl.BlockSpec((1, ql), lambda i: (0, 0)),
                  pl.BlockSpec((1, kl), lambda i: (0, 0))],
        out_specs=(pl.BlockSpec((tile, ql), lambda i: (i, 0)), pl.BlockSpec((tile, kl), lambda i: (i, 0))),
        out_shape=(jax.ShapeDtypeStruct((t, ql), BF16), jax.ShapeDtypeStruct((t, kl), BF16)),
        compiler_params=_params("parallel"),
    )(down, g_qa, g_kva)


def _latent_norm_bwd(down, g_qa, g_kva, dcqn, dckvn, dkr, geo, name):
    t, wd = down.shape
    ql, kl = g_qa.shape[1], g_kva.shape[1]
    tile = geo.tile

    def body(d_ref, gq_ref, gk_ref, dq_ref, dk_ref, dkr_ref, o_ref, dgq_ref, dgk_ref):
        i = pl.program_id(0)

        @pl.when(i == 0)
        def _():
            dgq_ref[...] = jnp.zeros_like(dgq_ref)
            dgk_ref[...] = jnp.zeros_like(dgk_ref)

        for lo, n, g_ref, dy_ref, dg_ref in ((0, ql, gq_ref, dq_ref, dgq_ref), (ql, kl, gk_ref, dk_ref, dgk_ref)):
            x = d_ref[:, lo:lo + n]
            dy = dy_ref[...].astype(F32)
            r = lax.rsqrt(jnp.mean(x * x, axis=-1, keepdims=True) + EPS)
            xhat = x * r
            u = dy * g_ref[...]
            o_ref[:, lo:lo + n] = (r * (u - xhat * jnp.mean(u * xhat, axis=-1, keepdims=True))).astype(o_ref.dtype)
            dg_ref[...] += jnp.sum(dy * xhat, axis=0, keepdims=True)
        o_ref[:, ql + kl:] = dkr_ref[...].astype(o_ref.dtype)

    def row(n):
        return pl.BlockSpec((tile, n), lambda i: (i, 0))

    def vec(n):
        return pl.BlockSpec((1, n), lambda i: (0, 0))

    return pl.pallas_call(
        body, name=name, grid=(t // tile,),
        in_specs=[row(wd), vec(ql), vec(kl), row(ql), row(kl), row(wd - ql - kl)],
        out_specs=(row(wd), vec(ql), vec(kl)),
        out_shape=(jax.ShapeDtypeStruct((t, wd), BF16), jax.ShapeDtypeStruct((1, ql), F32),
                   jax.ShapeDtypeStruct((1, kl), F32)),
        compiler_params=_params("arbitrary"),
    )(down, g_qa, g_kva, dcqn, dckvn, dkr)


def _qk_specs(geo, xr_col, shared_rope):
    tile = geo.mm_tile
    xn_spec = pl.BlockSpec((tile, HEADS * LANE), lambda i: (i, 0))
    if shared_rope:
        xr_spec = pl.BlockSpec((tile, LANE), lambda i: (i, xr_col))
    else:
        xr_spec = pl.BlockSpec((tile, HEADS * LANE), lambda i: (i, xr_col // HEADS))
    vec = pl.BlockSpec((1, LANE), lambda i: (0, 0))
    tab = pl.BlockSpec((tile, LANE), lambda i: (i, 0))
    return tile, xn_spec, xr_spec, vec, tab


def _qk_norm(xn, xr):
    ss = jnp.sum(xn * xn, axis=-1, keepdims=True) + jnp.sum(xr * xr, axis=-1, keepdims=True)
    return lax.rsqrt(ss * (1.0 / QK_HEAD) + EPS)


def _head_lanes(ref, hh, shared=False):
    return ref[...] if shared else ref[:, hh * LANE:(hh + 1) * LANE]


def _qk_fwd(xn_arr, xr_arr, xr_col, shared_rope, gn, gr, cos, sin, geo, name):
    t = xn_arr.shape[0]
    tile, xn_spec, xr_spec, vec, tab = _qk_specs(geo, xr_col, shared_rope)

    def body(xn_ref, xr_ref, gn_ref, gr_ref, c_ref, s_ref, o_ref):
        cv, sv, gnv, grv = c_ref[...], s_ref[...], gn_ref[...], gr_ref[...]
        for hh in range(HEADS):
            xn = _head_lanes(xn_ref, hh).astype(F32)
            xr = _head_lanes(xr_ref, hh, shared_rope).astype(F32)
            r = _qk_norm(xn, xr)
            yr = xr * r * grv
            o_ref[:, hh * HEAD_PAD:hh * HEAD_PAD + LANE] = (xn * r * gnv).astype(o_ref.dtype)
            o_ref[:, hh * HEAD_PAD + LANE:(hh + 1) * HEAD_PAD] = (yr * cv + _swap_halves(yr) * sv).astype(o_ref.dtype)

    return pl.pallas_call(
        body, name=name, grid=(t // tile,), in_specs=[xn_spec, xr_spec, vec, vec, tab, tab],
        out_specs=pl.BlockSpec((tile, HEADS * HEAD_PAD), lambda i: (i, 0)),
        out_shape=jax.ShapeDtypeStruct((t, HEADS * HEAD_PAD), BF16), compiler_params=_params("parallel"),
    )(xn_arr, xr_arr, gn, gr, cos, sin)


def _qk_bwd(xn_arr, xr_arr, xr_col, shared_rope, gn, gr, cos, sin, dout, geo, name):
    t = xn_arr.shape[0]
    tile, xn_spec, xr_spec, vec, tab = _qk_specs(geo, xr_col, shared_rope)

    def body(xn_ref, xr_ref, gn_ref, gr_ref, c_ref, s_ref, d_ref, dxn_ref, dxr_ref, dgn_ref, dgr_ref):
        i = pl.program_id(0)
        cv, sv, gnv, grv = c_ref[...], s_ref[...], gn_ref[...], gr_ref[...]
        dgn = jnp.zeros((1, LANE), F32)
        dgr = jnp.zeros((1, LANE), F32)
        dxr_sum = jnp.zeros((tile, LANE), F32)
        for hh in range(HEADS):
            xn = _head_lanes(xn_ref, hh).astype(F32)
            xr = _head_lanes(xr_ref, hh, shared_rope).astype(F32)
            r = _qk_norm(xn, xr)
            xhn, xhr = xn * r, xr * r
            dyn = d_ref[:, hh * HEAD_PAD:hh * HEAD_PAD + LANE].astype(F32)
            dro = d_ref[:, hh * HEAD_PAD + LANE:(hh + 1) * HEAD_PAD].astype(F32)
            dyr = dro * cv + _swap_halves(dro * sv)
            un, ur = dyn * gnv, dyr * grv
            mean = (jnp.sum(un * xhn, axis=-1, keepdims=True) + jnp.sum(ur * xhr, axis=-1, keepdims=True)) * (1.0 / QK_HEAD)
            dxn_ref[:, hh * LANE:(hh + 1) * LANE] = (r * (un - xhn * mean)).astype(dxn_ref.dtype)
            dxr = r * (ur - xhr * mean)
            if shared_rope:
                dxr_sum = dxr_sum + dxr
            else:
                dxr_ref[:, hh * LANE:(hh + 1) * LANE] = dxr.astype(dxr_ref.dtype)
            dgn = dgn + jnp.sum(dyn * xhn, axis=0, keepdims=True)
            dgr = dgr + jnp.sum(dyr * xhr, axis=0, keepdims=True)
        if shared_rope:
            dxr_ref[...] = dxr_sum

        @pl.when(i == 0)
        def _():
            dgn_ref[...] = jnp.zeros_like(dgn_ref)
            dgr_ref[...] = jnp.zeros_like(dgr_ref)

        dgn_ref[...] += dgn
        dgr_ref[...] += dgr

    heads = pl.BlockSpec((tile, HEADS * LANE), lambda i: (i, 0))
    if shared_rope:
        dxr_spec, dxr_shape = pl.BlockSpec((tile, LANE), lambda i: (i, 0)), jax.ShapeDtypeStruct((t, LANE), F32)
    else:
        dxr_spec, dxr_shape = heads, jax.ShapeDtypeStruct((t, HEADS * LANE), BF16)
    return pl.pallas_call(
        body, name=name, grid=(t // tile,),
        in_specs=[xn_spec, xr_spec, vec, vec, tab, tab, pl.BlockSpec((tile, HEADS * HEAD_PAD), lambda i: (i, 0))],
        out_specs=(heads, dxr_spec, vec, vec),
        out_shape=(jax.ShapeDtypeStruct((t, HEADS * LANE), BF16), dxr_shape, jax.ShapeDtypeStruct((1, LANE), F32),
                   jax.ShapeDtypeStruct((1, LANE), F32)),
        compiler_params=_params("arbitrary"),
    )(xn_arr, xr_arr, gn, gr, cos, sin, dout)


def _attn_specs(geo):
    tq, nq = geo.n_ctx, geo.n_lat // geo.n_ctx

    def qrow(b, i):
        return jnp.where(i < nq, b * nq + i, 2 * nq + b)

    q_spec = pl.BlockSpec((tq, HEAD_PAD), lambda b, hh, i: (qrow(b, i), hh))
    kc_spec = pl.BlockSpec((geo.n_ctx, HEAD_PAD), lambda b, hh, i: (2 * nq + b, hh))
    kl_spec = pl.BlockSpec((geo.n_lat, HEAD_PAD), lambda b, hh, i: (b, hh))
    vc_spec = pl.BlockSpec((geo.n_ctx, V_HEAD), lambda b, hh, i: (2 * nq + b, HEADS + hh))
    vl_spec = pl.BlockSpec((geo.n_lat, V_HEAD), lambda b, hh, i: (b, HEADS + hh))
    o_spec = pl.BlockSpec((tq, V_HEAD), lambda b, hh, i: (qrow(b, i), hh))
    return tq, nq, q_spec, kc_spec, kl_spec, vc_spec, vl_spec, o_spec


def _attn_fwd(q, k, kv, with_ctx_q, geo, name, hosted=None):
    t = q.shape[0]
    tq, nq, q_spec, kc_spec, kl_spec, vc_spec, vl_spec, o_spec = _attn_specs(geo)

    def body(q_ref, kc_ref, kl_ref, vc_ref, vl_ref, o_ref):
        i = pl.program_id(2)
        qv = q_ref[...]
        s_c = _dot(qv, kc_ref[...], _NT) * QK_SCALE

        @pl.when(i < nq)
        def _():
            s_l = _dot(qv, kl_ref[...], _NT) * QK_SCALE
            m = jnp.maximum(jnp.max(s_c, axis=-1, keepdims=True), jnp.max(s_l, axis=-1, keepdims=True))
            p_c, p_l = jnp.exp(s_c - m), jnp.exp(s_l - m)
            den = jnp.sum(p_c, axis=-1, keepdims=True) + jnp.sum(p_l, axis=-1, keepdims=True)
            o = _dot(p_c, vc_ref[...], _NN) + _dot(p_l, vl_ref[...], _NN)
            o_ref[...] = (o / den).astype(o_ref.dtype)

        @pl.when(i == nq)
        def _():
            if with_ctx_q:
                m = jnp.max(s_c, axis=-1, keepdims=True)
                p_c = jnp.exp(s_c - m)
                o = _dot(p_c, vc_ref[...], _NN) / jnp.sum(p_c, axis=-1, keepdims=True)
                o_ref[...] = o.astype(o_ref.dtype)
            else:
                o_ref[...] = jnp.zeros_like(o_ref)

    return _call(
        body, hosted, name=name, grid=(2, HEADS, nq + 1), in_specs=[q_spec, kc_spec, kl_spec, vc_spec, vl_spec],
        out_specs=o_spec, out_shape=jax.ShapeDtypeStruct((t, HEADS * V_HEAD), BF16),
        compiler_params=_params("parallel", "parallel", "arbitrary"),
    )(q, k, k, kv, kv)


def _attn_bwd(q, k, kv, do, with_ctx_q, geo, name, hosted=None):
    t = q.shape[0]
    tq, nq, q_spec, kc_spec, kl_spec, vc_spec, vl_spec, o_spec = _attn_specs(geo)

    def body(q_ref, kc_ref, kl_ref, vc_ref, vl_ref, do_ref, dq_ref, dkl_ref, dkc_ref, dvl_ref, dvc_ref,
             akl_ref, akc_ref, avl_ref, avc_ref):
        i = pl.program_id(2)

        @pl.when(i == 0)
        def _():
            for ref in (akl_ref, akc_ref, avl_ref, avc_ref):
                ref[...] = jnp.zeros_like(ref)

        qv, dov = q_ref[...], do_ref[...]
        s_c = _dot(qv, kc_ref[...], _NT) * QK_SCALE
        dp_c = _dot(dov, vc_ref[...], _NT)

        def ctx_part(p_c, delta):
            ds_c = (p_c * (dp_c - delta) * QK_SCALE).astype(BF16)
            akc_ref[...] += _dot(ds_c, qv, _TN)
            avc_ref[...] += _dot(p_c, dov, _TN)
            return _dot(ds_c, kc_ref[...], _NN)

        @pl.when(i < nq)
        def _():
            s_l = _dot(qv, kl_ref[...], _NT) * QK_SCALE
            m = jnp.maximum(jnp.max(s_c, axis=-1, keepdims=True), jnp.max(s_l, axis=-1, keepdims=True))
            p_c, p_l = jnp.exp(s_c - m), jnp.exp(s_l - m)
            inv = 1.0 / (jnp.sum(p_c, axis=-1, keepdims=True) + jnp.sum(p_l, axis=-1, keepdims=True))
            p_c, p_l = p_c * inv, p_l * inv
            dp_l = _dot(dov, vl_ref[...], _NT)
            delta = jnp.sum(p_c * dp_c, axis=-1, keepdims=True) + jnp.sum(p_l * dp_l, axis=-1, keepdims=True)
            ds_l = (p_l * (dp_l - delta) * QK_SCALE).astype(BF16)
            akl_ref[...] += _dot(ds_l, qv, _TN)
            avl_ref[...] += _dot(p_l, dov, _TN)
            dq_ref[...] = (ctx_part(p_c, delta) + _dot(ds_l, kl_ref[...], _NN)).astype(dq_ref.dtype)

        @pl.when(i == nq)
        def _():
            if with_ctx_q:
                m = jnp.max(s_c, axis=-1, keepdims=True)
                p_c = jnp.exp(s_c - m)
                p_c = p_c * (1.0 / jnp.sum(p_c, axis=-1, keepdims=True))
                delta = jnp.sum(p_c * dp_c, axis=-1, keepdims=True)
                dq_ref[...] = ctx_part(p_c, delta).astype(dq_ref.dtype)
            else:
                dq_ref[...] = jnp.zeros_like(dq_ref)
            dkl_ref[...] = akl_ref[...].astype(dkl_ref.dtype)
            dkc_ref[...] = akc_ref[...].astype(dkc_ref.dtype)
            dvl_ref[...] = avl_ref[...].astype(dvl_ref.dtype)
            dvc_ref[...] = avc_ref[...].astype(dvc_ref.dtype)

    def acc_spec(rows, width):
        return pl.BlockSpec((rows, width), lambda b, hh, i: (b, hh))

    return _call(
        body, hosted, name=name, grid=(2, HEADS, nq + 1), in_specs=[q_spec, kc_spec, kl_spec, vc_spec, vl_spec, o_spec],
        out_specs=(q_spec, acc_spec(geo.n_lat, HEAD_PAD), acc_spec(geo.n_ctx, HEAD_PAD), acc_spec(geo.n_lat, V_HEAD),
                   acc_spec(geo.n_ctx, V_HEAD)),
        out_shape=(jax.ShapeDtypeStruct((t, HEADS * HEAD_PAD), BF16),
                   jax.ShapeDtypeStruct((2 * geo.n_lat, HEADS * HEAD_PAD), BF16),
                   jax.ShapeDtypeStruct((2 * geo.n_ctx, HEADS * HEAD_PAD), BF16),
                   jax.ShapeDtypeStruct((2 * geo.n_lat, HEADS * V_HEAD), BF16),
                   jax.ShapeDtypeStruct((2 * geo.n_ctx, HEADS * V_HEAD), BF16)),
        scratch_shapes=[pltpu.VMEM((geo.n_lat, HEAD_PAD), F32), pltpu.VMEM((geo.n_ctx, HEAD_PAD), F32),
                        pltpu.VMEM((geo.n_lat, V_HEAD), F32), pltpu.VMEM((geo.n_ctx, V_HEAD), F32)],
        compiler_params=_params("parallel", "parallel", "arbitrary"),
    )(q, k, k, kv, kv, do)


def _mla_fwd(h, g, mod4, w, with_ctx_q, tabs, geo, tag, hosts, got):
    cos, sin = tabs
    ql, kl = w["g_qa"].shape[1], w["g_kva"].shape[1]
    kr_col = (ql + kl) // LANE
    nx = _pre_fwd(h, g, mod4, 3, geo, f"{tag}_pre")
    down = _mm(nx, w["w_a"], name=f"{tag}_down")
    cqn, ckvn = _latent_norm_fwd(down, w["g_qa"], w["g_kva"], geo, f"{tag}_lnorm")
    qraw = _mm(cqn, w["w_uq"], out_dtype=BF16, name=f"{tag}_uq")
    kvraw = _mm(ckvn, w["w_ukv"], out_dtype=BF16, name=f"{tag}_ukv")
    q = _qk_fwd(qraw, qraw, HEADS, False, w["gq_n"], w["gq_r"], cos, sin, geo, f"{tag}_qnorm")
    k = _qk_fwd(kvraw, down, kr_col, True, w["gk_n"], w["gk_r"], cos, sin, geo, f"{tag}_knorm")
    o = _with_host(_attn_fwd, hosts, got, "mix_a", q, k, kvraw, with_ctx_q, geo, f"{tag}_attn")
    h_out, y = _mm(o, w["w_o"], name=f"{tag}_o", gate=(h, mod4, 5, 1.0, geo))
    return h_out, (h, nx, down, cqn, ckvn, qraw, kvraw, q, k, o, y)


def _mla_bwd(dh_out, saved, g, mod4, w, with_ctx_q, tabs, geo, tag, hosts, got):
    cos, sin = tabs
    h, nx, down, cqn, ckvn, qraw, kvraw, q, k, o, y = saved
    ql, kl = w["g_qa"].shape[1], w["g_kva"].shape[1]
    kr_col = (ql + kl) // LANE
    dy, dgate = _gate_bwd(dh_out, y, mod4, 5, 1.0, geo, f"{tag}_dgate")
    do = _mm(dy, w["w_o"], tb=True, out_dtype=BF16, name=f"{tag}_do")
    dw_o = _tn_wide(o, dy, f"{tag}_dwo")
    dq, dk_lat, dk_ctx, dv_lat, dv_ctx = _with_host(_attn_bwd, hosts, got, "mix_a", q, k, kvraw, do, with_ctx_q, geo,
                                                    f"{tag}_dattn")
    dk = jnp.concatenate([dk_lat, dk_ctx], axis=0)
    dqn, dqr, dgq_n, dgq_r = _qk_bwd(qraw, qraw, HEADS, False, w["gq_n"], w["gq_r"], cos, sin, dq, geo, f"{tag}_dqnorm")
    dkn, dkr, dgk_n, dgk_r = _qk_bwd(kvraw, down, kr_col, True, w["gk_n"], w["gk_r"], cos, sin, dk, geo, f"{tag}_dknorm")
    dqraw = jnp.concatenate([dqn, dqr], axis=1)
    dkvraw = jnp.concatenate([dkn, jnp.concatenate([dv_lat, dv_ctx], axis=0)], axis=1)
    dcqn = _mm(dqraw, w["w_uq"], tb=True, out_dtype=BF16, name=f"{tag}_dcqn")
    dw_uq = _tn_wide(cqn, dqraw, f"{tag}_dwuq")
    dckvn = _mm(dkvraw, w["w_ukv"], tb=True, out_dtype=BF16, name=f"{tag}_dckvn")
    dw_ukv = _tn_wide(ckvn, dkvraw, f"{tag}_dwukv")
    ddown, dg_qa, dg_kva = _latent_norm_bwd(down, w["g_qa"], w["g_kva"], dcqn, dckvn, dkr, geo, f"{tag}_dlnorm")
    dnx = _mm(ddown, w["w_a"], tb=True, name=f"{tag}_dnx")
    dw_a = _tn_wide(nx, ddown, f"{tag}_dwa")
    dh, dg, dshift, dscale = _pre_bwd(h, g, mod4, 3, dnx, dh_out, geo, f"{tag}_dpre")
    grads = dict(w_a=dw_a, g_qa=dg_qa, w_uq=dw_uq, g_kva=dg_kva, w_ukv=dw_ukv, gq_n=dgq_n, gq_r=dgq_r, gk_n=dgk_n,
                 gk_r=dgk_r, w_o=dw_o)
    return dh, dg, (dshift, dscale, dgate), grads


def _mla_prepare(w_a, g_qa, w_uq, g_kva, w_ukv, g_q, g_k, w_o):
    ql, kl = g_qa.shape[0], g_kva.shape[0]
    d = w_a.shape[0]
    w_a_pad = jnp.concatenate([w_a[:, :ql + kl], _rope_swap(w_a[:, ql + kl:]), jnp.zeros((d, LANE - QK_ROPE), w_a.dtype)], axis=1)
    uq = w_uq.reshape(ql, HEADS, QK_HEAD)
    uq_r = jnp.pad(_rope_swap(uq[:, :, QK_NOPE:]), ((0, 0), (0, 0), (0, LANE - QK_ROPE)))
    w_uq_pad = jnp.concatenate([uq[:, :, :QK_NOPE].reshape(ql, HEADS * LANE), uq_r.reshape(ql, HEADS * LANE)], axis=1)
    ukv = w_ukv.reshape(kl, HEADS, QK_NOPE + V_HEAD)
    w_ukv_p = jnp.concatenate([ukv[:, :, :QK_NOPE].reshape(kl, HEADS * LANE), ukv[:, :, QK_NOPE:].reshape(kl, HEADS * V_HEAD)], axis=1)

    def gains(gv):
        gv = gv.astype(F32)
        return gv[None, :QK_NOPE], jnp.pad(_rope_swap(gv[QK_NOPE:]), (0, LANE - QK_ROPE))[None]

    gq_n, gq_r = gains(g_q)
    gk_n, gk_r = gains(g_k)
    return dict(w_a=w_a_pad, g_qa=g_qa.astype(F32)[None], w_uq=w_uq_pad, g_kva=g_kva.astype(F32)[None], w_ukv=w_ukv_p,
                gq_n=gq_n, gq_r=gq_r, gk_n=gk_n, gk_r=gk_r, w_o=w_o)


def _mla_unprepare(gr):
    ql, kl = gr["g_qa"].shape[1], gr["g_kva"].shape[1]
    dw_a = jnp.concatenate([gr["w_a"][:, :ql + kl], _rope_swap(gr["w_a"][:, ql + kl:ql + kl + QK_ROPE])], axis=1)
    uqn = gr["w_uq"][:, :HEADS * LANE].reshape(ql, HEADS, LANE)
    uqr = _rope_swap(gr["w_uq"][:, HEADS * LANE:].reshape(ql, HEADS, LANE)[:, :, :QK_ROPE])
    dw_uq = jnp.concatenate([uqn, uqr], axis=2).reshape(ql, HEADS * QK_HEAD)
    ukn = gr["w_ukv"][:, :HEADS * LANE].reshape(kl, HEADS, LANE)
    ukv = gr["w_ukv"][:, HEADS * LANE:].reshape(kl, HEADS, V_HEAD)
    dw_ukv = jnp.concatenate([ukn, ukv], axis=2).reshape(kl, HEADS * (QK_NOPE + V_HEAD))

    def gains(gn, grr):
        return jnp.concatenate([gn[0], _rope_swap(grr[0, :QK_ROPE])])

    return dict(mla_w_a=dw_a, mla_g_qa=gr["g_qa"][0], mla_w_uq=dw_uq, mla_g_kva=gr["g_kva"][0], mla_w_ukv=dw_ukv,
                mla_g_q=gains(gr["gq_n"], gr["gq_r"]), mla_g_k=gains(gr["gk_n"], gr["gk_r"]), mla_w_o=gr["w_o"])


def _loss_head(h, target, geo, name):
    t, d = h.shape
    tile = geo.tile
    n_lat_tiles = 2 * geo.n_lat // tile

    def body(h_ref, t_ref, dh_ref, loss_ref):
        i = pl.program_id(0)

        @pl.when(i == 0)
        def _():
            loss_ref[...] = jnp.zeros_like(loss_ref)

        @pl.when(i < n_lat_tiles)
        def _():
            e = h_ref[...] - t_ref[...]
            dh_ref[...] = e * (1.0 / d)
            part = jnp.sum(e * e, axis=0, keepdims=True) * (0.5 / d)
            loss_ref[...] += sum(part[:, j * LANE:(j + 1) * LANE] for j in range(d // LANE))

        @pl.when(i >= n_lat_tiles)
        def _():
            dh_ref[...] = jnp.zeros_like(dh_ref)

    row = pl.BlockSpec((tile, d), lambda i: (i, 0))
    tgt = pl.BlockSpec((tile, d), lambda i: (jnp.minimum(i, n_lat_tiles - 1), 0))
    dh, loss = pl.pallas_call(
        body, name=name, grid=(t // tile,), in_specs=[row, tgt],
        out_specs=(row, pl.BlockSpec((1, LANE), lambda i: (0, 0))),
        out_shape=(jax.ShapeDtypeStruct((t, d), F32), jax.ShapeDtypeStruct((1, LANE), F32)),
        compiler_params=_params("arbitrary"),
    )(h, target)
    return jnp.sum(loss), dh


def _adamw(w, g, m, v, name):
    shape = w.shape
    cols = shape[-1]
    rows = int(np.prod(shape[:-1])) if len(shape) > 1 else 1
    w2, g2, m2, v2 = (a.reshape(rows, cols) for a in (w, g, m, v))
    tr = _pick(rows, (512, 256, 128, 64, 32, 16, 8))
    c1 = 1.0 / (1.0 - ADAM_B1 ** ADAM_STEP)
    c2 = 1.0 / (1.0 - ADAM_B2 ** ADAM_STEP)

    def body(w_ref, g_ref, m_ref, v_ref, d_ref, mo_ref, vo_ref):
        gv = g_ref[...]
        mn = ADAM_B1 * m_ref[...] + (1.0 - ADAM_B1) * gv
        vn = ADAM_B2 * v_ref[...] + (1.0 - ADAM_B2) * (gv * gv)
        d_ref[...] = -ADAM_LR * ((mn * c1) / (jnp.sqrt(vn * c2) + ADAM_EPS) + ADAM_WD * w_ref[...])
        mo_ref[...] = mn
        vo_ref[...] = vn

    blk = pl.BlockSpec((tr, cols), lambda i: (i, 0))
    sds = jax.ShapeDtypeStruct((rows, cols), F32)
    d, mo, vo = pl.pallas_call(
        body, name=name, grid=(rows // tr,), in_specs=[blk] * 4, out_specs=(blk,) * 3, out_shape=(sds,) * 3,
        compiler_params=_params("parallel"),
    )(w2, g2, m2, v2)
    return d.reshape(shape), mo.reshape(shape), vo.reshape(shape)


SHARD_AXIS = {
    "w_mod": 2, "g_norm": 2, "ffn_w1": 3, "ffn_w3": 3, "ffn_w2": 2, "sc_w_in": 2, "sc_conv": 2, "sc_w_out": 1,
    "mla_w_a": 1, "mla_g_qa": 1, "mla_w_uq": 2, "mla_w_ukv": 2, "mla_w_o": 1,
}
HIDDEN_MAJOR = ("ffn_w1", "ffn_w3")


def _view(name, arr, swapped=False):
    form, swap, _ = EXCHANGE[name]
    if swap and not swapped:
        arr = jnp.swapaxes(arr, -1, -2)
    if form == "mid":
        arr = arr.reshape((-1,) + arr.shape[-2:])
        return jnp.pad(arr, ((0, 0), (0, 0), (0, -arr.shape[-1] % LANE)))
    arr = arr.reshape(-1, arr.shape[-1])
    return jnp.pad(arr, ((0, -arr.shape[0] % 16), (0, 0)))


def _unview(name, view, shape, keep_swapped=False):
    form, swap, _ = EXCHANGE[name]
    shape = shape[:-2] + (shape[-1], shape[-2]) if swap else shape
    if form == "mid":
        view = view[:, :, :shape[-1]]
    else:
        view = view[:int(np.prod(shape[:-1]))]
    arr = view.reshape(shape)
    return arr if (not swap or keep_swapped) else jnp.swapaxes(arr, -1, -2)


def _full_shape(name, local_shape):
    ax = SHARD_AXIS[name]
    return local_shape[:ax] + (N_DEV * local_shape[ax],) + local_shape[ax + 1:]


def _win(ref, form, n, j):
    start = j * n
    if not isinstance(start, int):
        start = pl.multiple_of(start, LANE if form == "last" else math.gcd(n, 16))
    if form == "mid":
        return ref.at[:, pl.ds(start, n), :]
    return ref.at[:, pl.ds(start, n)]


def _windows(view, count, of):
    return view.shape[:1] + (view.shape[1] * count // of,) + view.shape[2:]


def _gather_work(views, forms):
    na = len(views)

    def plan(x_refs, out_refs, sems):
        send_sems, recv_sems, local_sems = sems
        x, y, c = lax.axis_index("x"), lax.axis_index("y"), lax.axis_index("c")
        me, sibling = (x, y, c), (x, y, 1 - c)
        chips = [(1 - x, y), (x, 1 - y), (1 - x, 1 - y)]

        def copy(a, k, block, to, from_input):
            dst = _win(out_refs[a], forms[a], views[a].shape[1], 4 * block[0] + 2 * block[1] + block[2])
            return pltpu.make_async_remote_copy(
                src_ref=x_refs[a] if from_input else dst, dst_ref=dst, send_sem=send_sems.at[a, k],
                recv_sem=recv_sems.at[a, k], device_id=to, device_id_type=MESH)

        mine = [pltpu.make_async_copy(x_refs[a], _win(out_refs[a], forms[a], views[a].shape[1], 4 * x + 2 * y + c),
                                      local_sems.at[a]) for a in range(na)]
        first = []
        for a in range(na):
            first.append(copy(a, 0, me, sibling, True))
            first += [copy(a, 1 + j, me, (*chip, c), True) for j, chip in enumerate(chips)]
        return copy, mine, first, me, sibling, chips, c

    def start(x_refs, out_refs, sems):
        _, mine, first, *_ = plan(x_refs, out_refs, sems)
        for cp in mine + first:
            cp.start()

    def finish(x_refs, out_refs, sems):
        copy, mine, first, me, sibling, chips, c = plan(x_refs, out_refs, sems)
        passed = []
        for j, chip in enumerate(chips):
            for a in range(na):
                copy(a, 1 + j, (*chip, c), me, False).wait_recv()
                fwd = copy(a, 4 + j, (*chip, c), sibling, False)
                fwd.start()
                passed.append(fwd)
        for a in range(na):
            copy(a, 0, sibling, me, False).wait_recv()
            for j, chip in enumerate(chips):
                copy(a, 4 + j, (*chip, 1 - c), me, False).wait_recv()
        for cp in first + passed:
            cp.wait_send()
        for cp in mine:
            cp.wait()

    return Hosted(
        list(views), [jax.ShapeDtypeStruct(_windows(v, N_DEV, 1), v.dtype) for v in views],
        [pltpu.SemaphoreType.DMA((na, 7)), pltpu.SemaphoreType.DMA((na, 7)), pltpu.SemaphoreType.DMA((na,))], start, finish)


def _push_work(srcs, out_shapes, n_copies, make_copies):
    na = len(srcs)

    def start(s_refs, r_refs, sems):
        for cp in make_copies(s_refs, r_refs, sems[0], sems[1]):
            cp.start()

    def finish(s_refs, r_refs, sems):
        copies = make_copies(s_refs, r_refs, sems[0], sems[1])
        for cp in copies:
            cp.wait_recv()
        for cp in copies:
            cp.wait_send()

    return Hosted(list(srcs), out_shapes, [pltpu.SemaphoreType.DMA((na, n_copies)), pltpu.SemaphoreType.DMA((na, n_copies))],
                  start, finish)


def _sibling_work(fulls, forms):
    na = len(fulls)
    widths = [f.shape[1] // N_DEV for f in fulls]

    def make_copies(g_refs, r_refs, send_sems, recv_sems):
        x, y, c = lax.axis_index("x"), lax.axis_index("y"), lax.axis_index("c")
        return [
            pltpu.make_async_remote_copy(
                src_ref=_win(g_refs[a], forms[a], widths[a], 2 * chip + (1 - c)),
                dst_ref=_win(r_refs[a], forms[a], widths[a], chip), send_sem=send_sems.at[a, chip],
                recv_sem=recv_sems.at[a, chip], device_id=(x, y, 1 - c), device_id_type=MESH)
            for a in range(na) for chip in range(N_CHIP)
        ]

    return _push_work(fulls, [jax.ShapeDtypeStruct(_windows(f, N_CHIP, N_DEV), f.dtype) for f in fulls], N_CHIP, make_copies)


def _chip_work(parts, forms):
    na = len(parts)
    widths = [p.shape[1] // N_CHIP for p in parts]

    def make_copies(p_refs, r_refs, send_sems, recv_sems):
        x, y, c = lax.axis_index("x"), lax.axis_index("y"), lax.axis_index("c")
        chips = [(1 - x, y), (x, 1 - y), (1 - x, 1 - y)]
        return [
            pltpu.make_async_remote_copy(
                src_ref=_win(p_refs[a], forms[a], widths[a], 2 * px + py), dst_ref=_win(r_refs[a], forms[a], widths[a], j),
                send_sem=send_sems.at[a, j], recv_sem=recv_sems.at[a, j], device_id=(px, py, c), device_id_type=MESH)
            for a in range(na) for j, (px, py) in enumerate(chips)
        ]

    return _push_work(parts, [jax.ShapeDtypeStruct(_windows(p, 3, N_CHIP), p.dtype) for p in parts], 3, make_copies)


def _sum_tiles(view, form, n):
    if form == "mid":
        tr = n
        while tr * view.shape[2] * 4 > 2 * 1024 * 1024 and tr % 32 == 0:
            tr //= 2
        return 1, tr
    return _pick(view.shape[0], (512, 256, 128, 64, 32, 16)), n


def _window_spec(form, tl, tr, rest, window_of):
    if form == "mid":
        return lambda per: pl.BlockSpec((None, tr) + rest, lambda l, k, i, s: (l, window_of(k, s) * per + i, 0))
    return lambda per: pl.BlockSpec((tl, tr), lambda l, k, i, s: (l, window_of(k, s)))


def _chip_partials(g, recv, core, form, name):
    n = g.shape[1] // N_DEV
    tl, tr = _sum_tiles(g, form, n)
    per = n // tr
    rest = tuple(g.shape[2:])

    def body(core_ref, g_ref, r_ref, o_ref):
        o_ref[...] = (g_ref[...] + r_ref[...]).astype(o_ref.dtype)

    own = _window_spec(form, tl, tr, rest, lambda k, s: 2 * k + s[0])(per)
    by_chip = _window_spec(form, tl, tr, rest, lambda k, s: k)(per)
    return pl.pallas_call(
        body, name=name,
        grid_spec=pltpu.PrefetchScalarGridSpec(
            num_scalar_prefetch=1, grid=(g.shape[0] // tl, N_CHIP, per), in_specs=[own, by_chip], out_specs=by_chip),
        out_shape=jax.ShapeDtypeStruct(recv.shape, BF16), compiler_params=_params("parallel", "parallel", "parallel"),
    )(core, g, recv)


def _reduce_final(p, recv, chip, form, name):
    n = p.shape[1] // N_CHIP
    tl, tr = _sum_tiles(p, form, n)
    per = n // tr
    rest = tuple(p.shape[2:])

    def body(chip_ref, p_ref, ry_ref, rx_ref, rxy_ref, o_ref):
        own_pair = p_ref[...].astype(F32) + ry_ref[...].astype(F32)
        o_ref[...] = own_pair + (rx_ref[...].astype(F32) + rxy_ref[...].astype(F32))

    def rel(j):
        return _window_spec(form, tl, tr, rest, lambda k, s: j)(per)

    own = _window_spec(form, tl, tr, rest, lambda k, s: s[0])(per)
    return pl.pallas_call(
        body, name=name,
        grid_spec=pltpu.PrefetchScalarGridSpec(
            num_scalar_prefetch=1, grid=(p.shape[0] // tl, 1, per), in_specs=[own, rel(1), rel(0), rel(2)],
            out_specs=rel(0)),
        out_shape=jax.ShapeDtypeStruct(p.shape[:1] + (n,) + p.shape[2:], F32),
        compiler_params=_params("parallel", "parallel", "parallel"),
    )(chip, p, recv, recv, recv)


def _pack_replicated(arrays):
    pieces = []
    for a in arrays:
        flat = a.reshape(-1).astype(F32)
        pieces.append(jnp.pad(flat, (0, -flat.size % LANE)))
    total = sum(p.size for p in pieces)
    pieces.append(jnp.zeros((-total % (16 * LANE),), F32))
    return jnp.concatenate(pieces).reshape(-1, LANE)


def _unpack_replicated(buf, shapes):
    flat, out, off = buf.reshape(-1), [], 0
    for shape in shapes:
        size = int(np.prod(shape))
        out.append(flat[off:off + size].reshape(shape))
        off += size + (-size % LANE)
    return out


def _silu(v):
    return v * jax.nn.sigmoid(v)


SC_NAMES = ("sc_w_in", "sc_conv", "sc_w_out")
MLA_SHARDED = ("mla_w_a", "mla_g_qa", "mla_w_uq", "mla_w_ukv", "mla_w_o")
MLA_NAMES = ("mla_w_a", "mla_g_qa", "mla_w_uq", "mla_g_kva", "mla_w_ukv", "mla_g_q", "mla_g_k", "mla_w_o")


def _local_step(src, x, c, ctx, target):
    bsz, n_lat, d = x.shape
    n_ctx = ctx.shape[1]
    assert bsz == 2
    geo = Geo(n_lat, n_ctx)
    depth = src.depth
    tc = _conv_tile(d)
    tabs = _rope_tables(geo)

    h = jnp.concatenate([x.reshape(2 * n_lat, d), ctx.reshape(2 * n_ctx, d)], axis=0)
    tgt = target.reshape(2 * n_lat, d)

    saved = []
    for i in range(depth):
        kind = i % 2
        wl, slots = src.weights(i), src.fwd_slots(i)
        gn = wl["g_norm"].astype(F32)
        mod4 = src.mod(i).reshape(N_SEG, N_MOD, 1, d)
        h, s1 = _ffn_fwd(h, gn[0:1], mod4, 0, wl, 0, geo, f"l{i}_f1", "f1", slots, slots)
        if kind == 0:
            mix = (_interleave(wl["sc_w_in"], 3, tc), wl["sc_conv"].astype(F32), wl["sc_w_out"])
            h, s2 = _sconv_fwd(h, gn[1:2], mod4, *mix, geo, f"l{i}_sc", slots, slots)
        else:
            mix = _mla_prepare(*[wl[name] for name in MLA_NAMES])
            h, s2 = _mla_fwd(h, gn[1:2], mod4, mix, i != depth - 1, tabs, geo, f"l{i}_mla", slots, slots)
        h, s3 = _ffn_fwd(h, gn[2:3], mod4, 6, wl, 1, geo, f"l{i}_f2", "f2", slots, slots)
        saved.append((wl, gn, mod4, mix, s1, s2, s3))

    loss, dh = _loss_head(h, tgt, geo, "loss_head")

    g_b_mod = [None] * depth
    for i in reversed(range(depth)):
        kind = i % 2
        wl, gn, mod4, mix, s1, s2, s3 = saved[i]
        slots = src.bwd_slots(i)
        gbuf = {name: lax.empty(wl[name].shape, F32) for name in ("ffn_w1", "ffn_w3", "ffn_w2")}
        dh, dg2, dm2 = _ffn_bwd(dh, s3, gn[2:3], mod4, 6, wl, 1, gbuf, geo, f"l{i}_f2", "f2", slots, slots)
        if kind == 0:
            dh, dg1, dm1, dwin, dconv, dwout = _sconv_bwd(dh, s2, gn[1:2], mod4, *mix, geo, f"l{i}_sc", slots, slots)
            gl = dict(sc_w_in=_deinterleave(dwin, 3, tc), sc_conv=dconv, sc_w_out=dwout)
        else:
            dh, dg1, dm1, gm = _mla_bwd(dh, s2, gn[1:2], mod4, mix, i != depth - 1, tabs, geo, f"l{i}_mla", slots, slots)
            gl = _mla_unprepare(gm)
        dh, dg0, dm0 = _ffn_bwd(dh, s1, gn[0:1], mod4, 0, wl, 0, gbuf, geo, f"l{i}_f1", "f1", slots, slots)
        dmod = jnp.concatenate(list(dm0) + list(dm1) + list(dm2), axis=1).reshape(N_SEG, N_MOD * d)
        dmod8 = jnp.concatenate([dmod, jnp.zeros((8 - N_SEG, N_MOD * d), F32)], axis=0)
        g_b_mod[i] = jnp.sum(dmod, axis=0)
        gl.update(gbuf, g_norm=jnp.concatenate([dg0, dg1, dg2], axis=0))
        src.dmod(i, dmod8)
        src.grads(i, gl)

    grad_x = dh[:2 * n_lat].reshape(x.shape)
    return loss, grad_x, jnp.stack(g_b_mod)


class _Slots:
    def __init__(self, get, put):
        self.get, self._put = get, put

    def __setitem__(self, slot, outs):
        self._put(slot, outs)


FWD_PLAN = {
    0: {"f1_up": ("ffn_w1",), "f1_down": ("g_norm", "mix"), "mix_a": ("ffn_w3",), "mix_b": ("ffn_w2",)},
    1: {"f1_up": ("ffn_w1",), "mix_a": ("ffn_w3", "g_norm", "mix"), "f2_up": ("ffn_w2",)},
}
SIBLING_PLAN = {"f2_dact": ("ffn_w1", "g_norm", "mix"), "f2_dw2": ("ffn_w3", "ffn_w2")}
BWD_PLAN = {
    0: {"f2_dnx": ("ffn_w1",), "mix_b": ("ffn_w3",), "mix_a": ("ffn_w2",), "f1_dact": ("g_norm", "mix")},
    1: {"f2_dnx": ("ffn_w1",), "mix_a": ("ffn_w3", "ffn_w2"), "f1_dnx": ("g_norm", "mix")},
}
DMOD_SLOT = {0: "mix_c", 1: "f1_dact"}
MOD_ROWS = 32


class _Exchange:
    def __init__(self, w):
        self.w = w
        self.depth = w["w_mod"].shape[0]
        self.c_ctx = w["c_ctx"]
        self.me = 4 * lax.axis_index("x") + 2 * lax.axis_index("y") + lax.axis_index("c")
        self.core = lax.axis_index("c").astype(jnp.int32).reshape(1)
        self.chip = (2 * lax.axis_index("x") + lax.axis_index("y")).astype(jnp.int32).reshape(1)
        self.full, self.gviews, self.parts, self.reduced, self.rep, self.dmods = {}, {}, {}, {}, {}, {}
        self.ctx_pre = jnp.zeros_like(self.c_ctx)

    def _layer_of(self, name, i):
        return i // 2 if name.startswith(("sc_", "mla_")) else i

    def _mixer(self, i):
        return SC_NAMES if i % 2 == 0 else MLA_SHARDED

    def _expand(self, names, i):
        out = []
        for name in names:
            out += list(self._mixer(i)) if name == "mix" else [name]
        return out

    def _group(self, i):
        return ["g_norm", "ffn_w1", "ffn_w3", "ffn_w2"] + list(self._mixer(i))

    def _local(self, name, i):
        arr = self.w[name][self._layer_of(name, i)]
        return arr[:, None] if name == "mla_g_qa" else arr

    def _shapes(self, name, i):
        local = tuple(self._local(name, i).shape)
        ax = SHARD_AXIS[name] - 1
        return local, local[:ax] + (N_DEV * local[ax],) + local[ax + 1:]

    def _gather(self, names, i):
        views = [_view(n, self._local(n, i).astype(BF16 if EXCHANGE[n][2] else F32)) for n in names]
        return _gather_work(views, [EXCHANGE[n][0] for n in names])

    def _gathered(self, names, i, outs):
        for name, fv in zip(names, outs):
            arr = _unview(name, fv, self._shapes(name, i)[1], keep_swapped=name in HIDDEN_MAJOR)
            self.full[name, i] = arr[:, 0] if name == "mla_g_qa" else arr

    def prefetch(self, c):
        bsz, d = c.shape
        (conds,) = _run_hosted(_gather_work([jnp.pad(c, ((0, 8 - bsz), (0, 0)))[None]], ["mid"]), "gather_cond")
        conds = conds.reshape(N_DEV, 8, d)[:, :bsz]
        act = _silu(jnp.concatenate([conds, jnp.broadcast_to(self.c_ctx, (N_DEV, 1, d))], axis=1))
        self.s_rows = jnp.pad(act.reshape(N_DEV * N_SEG, d), ((0, MOD_ROWS - N_DEV * N_SEG), (0, 0)))
        cols = jnp.stack([_mm(self.s_rows, self.w["w_mod"][l], name=f"mod_cols_{l}") for l in range(self.depth)])
        names = self._group(0)
        work = self._gather(names, 0)
        both = _gather_work(work.inputs + [cols.reshape(self.depth * MOD_ROWS, -1)], self._forms(names) + ["last"])
        outs = _run_hosted(both, "gather_l0")
        self._gathered(names, 0, outs[:-1])
        mods = lax.dynamic_slice_in_dim(outs[-1].reshape(self.depth, MOD_ROWS, -1), N_SEG * self.me, N_SEG, axis=1)
        self.mods = mods + self.w["b_mod"][:, None, :]

    def mod(self, i):
        return self.mods[i]

    def weights(self, i):
        wl = {name: self.full[name, i] for name in self._group(i)}
        if i % 2 == 1:
            for name in ("mla_g_kva", "mla_g_q", "mla_g_k"):
                wl[name] = self.w[name][i // 2]
        return wl

    def fwd_slots(self, i):
        plan = FWD_PLAN[i % 2] if i + 1 < self.depth else {}
        names = {slot: self._expand(plan[slot], i + 1) for slot in plan}
        return _Slots(lambda slot: self._gather(names[slot], i + 1) if slot in names else None,
                      lambda slot, outs: self._gathered(names[slot], i + 1, outs))

    def grads(self, i, gl):
        for name in self._group(i):
            g = gl[name][:, None] if name == "mla_g_qa" else gl[name]
            self.gviews[name, i] = _view(name, g, swapped=name in HIDDEN_MAJOR)
        for name in REPLICATED:
            if name in gl:
                self.rep[name, i // 2] = gl[name]

    def dmod(self, i, dmod8):
        self.dmods[i] = dmod8

    def _dmod_gather(self, i):
        return _gather_work([self.dmods[i][None]], ["mid"])

    def _dmod_gathered(self, i, outs):
        n = self.w["w_mod"].shape[2]
        rows = outs[0].reshape(N_DEV, 8, -1)[:, :N_SEG]
        mine = lax.dynamic_slice_in_dim(rows, n * self.me, n, axis=2)
        flat = jnp.pad(mine.reshape(N_DEV * N_SEG, n), ((0, MOD_ROWS - N_DEV * N_SEG), (0, 0)))
        self.reduced["w_mod", i] = _mm(self.s_rows, flat, ta=True, name=f"dwmod_{i}")
        ctx_rows = jnp.pad(jnp.sum(mine[:, N_SEG - 1], axis=0, keepdims=True), ((0, 7), (0, 0)))
        self.ctx_pre = self.ctx_pre + _mm(ctx_rows, self.w["w_mod"][i], tb=True, name=f"dcond_{i}")[0]

    def _forms(self, names):
        return [EXCHANGE[n][0] for n in names]

    def _partials(self, names, i, from_sibling):
        for name, recv in zip(names, from_sibling):
            self.parts[name, i] = _chip_partials(self.gviews[name, i], recv, self.core, EXCHANGE[name][0],
                                                 f"partial_{name}_{i}")

    def _finals(self, names, i, from_chips):
        for name, recv in zip(names, from_chips):
            rv = _reduce_final(self.parts[name, i], recv, self.chip, EXCHANGE[name][0], f"final_{name}_{i}")
            arr = _unview(name, rv, self._shapes(name, i)[0])
            self.reduced[name, i] = arr[:, 0] if name == "mla_g_qa" else arr

    def bwd_slots(self, i):
        if i + 1 >= self.depth:
            return _Slots(lambda slot: None, None)
        plan = BWD_PLAN[i % 2]
        names = {slot: self._expand(plan[slot], i + 1) for slot in plan}
        sibling = {slot: self._expand(SIBLING_PLAN[slot], i + 1) for slot in SIBLING_PLAN}

        def get(slot):
            if slot in sibling:
                return _sibling_work([self.gviews[n, i + 1] for n in sibling[slot]], self._forms(sibling[slot]))
            if slot in names:
                return _chip_work([self.parts[n, i + 1] for n in names[slot]], self._forms(names[slot]))
            if slot == DMOD_SLOT[i % 2]:
                return self._dmod_gather(i + 1)
            return None

        def put(slot, outs):
            if slot in sibling:
                self._partials(sibling[slot], i + 1, outs)
            elif slot in names:
                self._finals(names[slot], i + 1, outs)
            else:
                self._dmod_gathered(i + 1, outs)

        return _Slots(get, put)

    def finish(self, rep_grads):
        group = self._group(0)
        self._dmod_gathered(0, _run_hosted(self._dmod_gather(0), "gather_dmod_l0"))
        rep_grads["c_ctx"] = self.ctx_pre
        for name in REPLICATED:
            if name not in rep_grads:
                rep_grads[name] = jnp.stack([self.rep[name, j] for j in range(self.w[name].shape[0])])
        rep = _pack_replicated([rep_grads[name] for name in REPLICATED])
        views = [self.gviews[n, 0] for n in group] + [jnp.tile(rep[None], (1, N_DEV, 1))]
        forms = self._forms(group) + ["mid"]
        from_sibling = _run_hosted(_sibling_work(views, forms), "reduce_sibling_l0")
        self._partials(group, 0, from_sibling[:-1])
        rep_part = _chip_partials(views[-1], from_sibling[-1], self.core, "mid", "partial_replicated")
        parts = [self.parts[n, 0] for n in group] + [rep_part]
        from_chips = _run_hosted(_chip_work(parts, forms), "reduce_chips_l0")
        self._finals(group, 0, from_chips[:-1])
        rep_sum = _reduce_final(rep_part, from_chips[-1], self.chip, "mid", "final_replicated")
        out = dict(zip(REPLICATED, _unpack_replicated(rep_sum, [self.w[name].shape for name in REPLICATED])))
        sg = jax.nn.sigmoid(self.c_ctx)
        out["c_ctx"] = out["c_ctx"] * (sg * (1.0 + self.c_ctx * (1.0 - sg)))
        for name in EXCHANGE:
            layers = range(self.w[name].shape[0])
            step = 2 if name.startswith(("sc_", "mla_")) else 1
            first = 1 if name.startswith("mla_") else 0
            out[name] = jnp.stack([self.reduced[name, first + step * l] for l in layers])
        return out


def kernel(x, c, ctx, c_ctx, w_mod, b_mod, g_norm, ffn_w1, ffn_w3, ffn_w2, sc_w_in, sc_conv, sc_w_out, mla_w_a, mla_g_qa, mla_w_uq, mla_g_kva, mla_w_ukv, mla_g_q, mla_g_k, mla_w_o, loss_target, m_c_ctx, m_w_mod, m_b_mod, m_g_norm, m_ffn_w1, m_ffn_w3, m_ffn_w2, m_sc_w_in, m_sc_conv, m_sc_w_out, m_mla_w_a, m_mla_g_qa, m_mla_w_uq, m_mla_g_kva, m_mla_w_ukv, m_mla_g_q, m_mla_g_k, m_mla_w_o, v_c_ctx, v_w_mod, v_b_mod, v_g_norm, v_ffn_w1, v_ffn_w3, v_ffn_w2, v_sc_w_in, v_sc_conv, v_sc_w_out, v_mla_w_a, v_mla_g_qa, v_mla_w_uq, v_mla_g_kva, v_mla_w_ukv, v_mla_g_q, v_mla_g_k, v_mla_w_o):
    w = dict(c_ctx=c_ctx, w_mod=w_mod, b_mod=b_mod, g_norm=g_norm, ffn_w1=ffn_w1, ffn_w3=ffn_w3, ffn_w2=ffn_w2,
             sc_w_in=sc_w_in, sc_conv=sc_conv, sc_w_out=sc_w_out, mla_w_a=mla_w_a, mla_g_qa=mla_g_qa, mla_w_uq=mla_w_uq,
             mla_g_kva=mla_g_kva, mla_w_ukv=mla_w_ukv, mla_g_q=mla_g_q, mla_g_k=mla_g_k, mla_w_o=mla_w_o)
    m = dict(c_ctx=m_c_ctx, w_mod=m_w_mod, b_mod=m_b_mod, g_norm=m_g_norm, ffn_w1=m_ffn_w1, ffn_w3=m_ffn_w3,
             ffn_w2=m_ffn_w2, sc_w_in=m_sc_w_in, sc_conv=m_sc_conv, sc_w_out=m_sc_w_out, mla_w_a=m_mla_w_a,
             mla_g_qa=m_mla_g_qa, mla_w_uq=m_mla_w_uq, mla_g_kva=m_mla_g_kva, mla_w_ukv=m_mla_w_ukv, mla_g_q=m_mla_g_q,
             mla_g_k=m_mla_g_k, mla_w_o=m_mla_w_o)
    v = dict(c_ctx=v_c_ctx, w_mod=v_w_mod, b_mod=v_b_mod, g_norm=v_g_norm, ffn_w1=v_ffn_w1, ffn_w3=v_ffn_w3,
             ffn_w2=v_ffn_w2, sc_w_in=v_sc_w_in, sc_conv=v_sc_conv, sc_w_out=v_sc_w_out, mla_w_a=v_mla_w_a,
             mla_g_qa=v_mla_g_qa, mla_w_uq=v_mla_w_uq, mla_g_kva=v_mla_g_kva, mla_w_ukv=v_mla_w_ukv, mla_g_q=v_mla_g_q,
             mla_g_k=v_mla_g_k, mla_w_o=v_mla_w_o)
    exchange = _Exchange(w)
    exchange.prefetch(c)
    loss, grad_x, g_b_mod = _local_step(exchange, x, c, ctx, loss_target)
    loss = lax.psum(loss, ("x", "y", "c"))
    reduced = exchange.finish(dict(b_mod=g_b_mod))

    outs = [[], [], [], []]
    for name in WEIGHTS:
        delta, new_m, new_v = _adamw(w[name], reduced[name], m[name], v[name], f"adamw_{name}")
        for lst, val in zip(outs, (reduced[name], delta, new_m, new_v)):
            lst.append(val)
    return (loss, grad_x, *outs[0], *outs[1], *outs[2], *outs[3])
```

```python
import functools
import math

import jax
import jax.numpy as jnp
import numpy as np
from jax import lax
from jax.experimental import pallas as pl
from jax.experimental.pallas import tpu as pltpu

F32 = jnp.float32
BF16 = jnp.bfloat16

N_MOD = 9
HEADS = 8
QK_NOPE = 128
QK_ROPE = 64
QK_HEAD = QK_NOPE + QK_ROPE
V_HEAD = 128
GRID_W = 64
ROPE_BASE = 10000.0
QK_SCALE = QK_HEAD ** -0.5
EPS = 1e-6
ADAM_LR, ADAM_B1, ADAM_B2, ADAM_EPS, ADAM_WD, ADAM_STEP = 0.001, 0.9, 0.999, 1e-08, 0.01, 10

N_DEV = 8
N_CHIP = 4
N_SEG = 3
LANE = 128
HEAD_PAD = 2 * LANE
VMEM_LIMIT_BYTES = 48 * 1024 * 1024
MESH = pl.DeviceIdType.MESH

WEIGHTS = ["c_ctx", "w_mod", "b_mod", "g_norm", "ffn_w1", "ffn_w3", "ffn_w2", "sc_w_in", "sc_conv", "sc_w_out",
           "mla_w_a", "mla_g_qa", "mla_w_uq", "mla_g_kva", "mla_w_ukv", "mla_g_q", "mla_g_k", "mla_w_o"]
EXCHANGE = {
    "w_mod": ("last", False, True), "ffn_w1": ("mid", True, True), "ffn_w3": ("mid", True, True),
    "ffn_w2": ("mid", False, True), "sc_w_in": ("last", False, True), "sc_w_out": ("mid", False, True),
    "mla_w_a": ("mid", False, True), "mla_w_uq": ("mid", True, True), "mla_w_ukv": ("last", False, True),
    "mla_w_o": ("mid", False, True), "g_norm": ("last", False, False), "sc_conv": ("last", False, False),
    "mla_g_qa": ("mid", False, False),
}
REPLICATED = ["c_ctx", "b_mod", "mla_g_kva", "mla_g_q", "mla_g_k"]


def _pick(n, cands):
    for cand in cands:
        if n % cand == 0:
            return cand
    return n


def _params(*sem):
    return pltpu.CompilerParams(dimension_semantics=sem, vmem_limit_bytes=VMEM_LIMIT_BYTES)


def _hbm():
    return pl.BlockSpec(memory_space=pl.ANY)


class Hosted:
    def __init__(self, inputs, out_shapes, scratch, start, finish):
        self.inputs, self.out_shapes, self.scratch, self.start, self.finish = inputs, out_shapes, scratch, start, finish


def _call(body, hosted, **kw):
    if hosted is None:
        return pl.pallas_call(body, **kw)
    single = not isinstance(kw["out_shape"], (tuple, list))
    out_shape = [kw["out_shape"]] if single else list(kw["out_shape"])
    out_specs = [kw["out_specs"]] if single else list(kw["out_specs"])
    in_specs, scratch, grid = list(kw["in_specs"]), list(kw.get("scratch_shapes", ())), kw["grid"]
    n_in, n_out, n_scr = len(in_specs), len(out_shape), len(scratch)
    h_in, h_out = len(hosted.inputs), len(hosted.out_shapes)

    def wrapped(*refs):
        ins, hins = refs[:n_in], refs[n_in:n_in + h_in]
        o0 = n_in + h_in
        outs, houts = refs[o0:o0 + n_out], refs[o0 + n_out:o0 + n_out + h_out]
        s0 = o0 + n_out + h_out
        scr, hscr = refs[s0:s0 + n_scr], refs[s0 + n_scr:]
        first = functools.reduce(jnp.logical_and, [pl.program_id(a) == 0 for a in range(len(grid))])
        last = functools.reduce(jnp.logical_and, [pl.program_id(a) == g - 1 for a, g in enumerate(grid)])

        @pl.when(first)
        def _():
            hosted.start(hins, houts, hscr)

        body(*ins, *outs, *scr)

        @pl.when(last)
        def _():
            hosted.finish(hins, houts, hscr)

    call = pl.pallas_call(
        wrapped, name=kw["name"], grid=grid, in_specs=in_specs + [_hbm()] * h_in,
        out_specs=tuple(out_specs + [_hbm()] * h_out), out_shape=tuple(out_shape + list(hosted.out_shapes)),
        scratch_shapes=scratch + list(hosted.scratch), input_output_aliases=kw.get("input_output_aliases", {}),
        compiler_params=_params(*["arbitrary"] * len(grid)))

    def run(*args):
        res = call(*args, *hosted.inputs)
        comp = res[:n_out]
        return (comp[0] if single else tuple(comp)), list(res[n_out:])

    return run


def _run_hosted(hosted, name):
    def body(*refs):
        h_in, h_out = len(hosted.inputs), len(hosted.out_shapes)
        hins, houts, hscr = refs[:h_in], refs[h_in:h_in + h_out], refs[h_in + h_out:]
        hosted.start(hins, houts, hscr)
        hosted.finish(hins, houts, hscr)

    return list(pl.pallas_call(
        body, name=name, in_specs=[_hbm()] * len(hosted.inputs), out_specs=tuple([_hbm()] * len(hosted.out_shapes)),
        out_shape=tuple(hosted.out_shapes), scratch_shapes=list(hosted.scratch))(*hosted.inputs))


class Geo:
    def __init__(self, n_lat, n_ctx):
        self.n_lat, self.n_ctx = n_lat, n_ctx
        self.rows = 2 * n_lat + 2 * n_ctx
        self.tile = n_ctx
        assert n_lat % n_ctx == 0 and n_ctx % 16 == 0
        self.mm_tile = _pick(n_lat, (512, 256, 128)) if self.rows % _pick(n_lat, (512, 256, 128)) == 0 else n_ctx
        self.big_tile = _pick(self.rows, (1536, 768, 512, 256))

    def seg(self, i, tile):
        return jnp.minimum((i * tile) // self.n_lat, N_SEG - 1)

    def seg_start(self, i, tile):
        row = i * tile
        return jnp.logical_or(row % self.n_lat == 0, row == 2 * self.n_lat) & (row <= 2 * self.n_lat)


_NT = (((1,), (1,)), ((), ()))
_NN = (((1,), (0,)), ((), ()))
_TN = (((0,), (0,)), ((), ()))


def _dot(a, b, dims):
    return lax.dot_general(a.astype(BF16), b.astype(BF16), dims, preferred_element_type=F32)


def _mm(a, b, *, ta=False, tb=False, out_dtype=F32, name, gate=None, hosted=None):
    (kdim, m) = a.shape if ta else a.shape[::-1]
    n = b.shape[0] if tb else b.shape[1]
    assert (b.shape[1] if tb else b.shape[0]) == kdim
    if gate is not None:
        tm = gate[4].mm_tile
    else:
        tm = _pick(m, (512, 256, 128))
    tn = _pick(n, (512, 256, 128))
    tk = _pick(kdim, (1024, 512, 256, 128))
    nk = kdim // tk
    dims = (((0 if ta else 1,), (1 if tb else 0,)), ((), ()))

    def body(*refs):
        if gate is not None:
            a_ref, b_ref, res_ref, gate_ref, o_ref, y_ref, acc_ref = refs
        else:
            a_ref, b_ref, o_ref, acc_ref = refs
        kk = pl.program_id(2)

        @pl.when(kk == 0)
        def _():
            acc_ref[...] = jnp.zeros_like(acc_ref)

        acc_ref[...] += lax.dot_general(a_ref[...].astype(BF16), b_ref[...].astype(BF16), dims,
                                        preferred_element_type=F32)

        @pl.when(kk == nk - 1)
        def _():
            acc = acc_ref[...]
            if gate is not None:
                y_ref[...] = acc.astype(y_ref.dtype)
                o_ref[...] = res_ref[...] + (gate[3] * gate_ref[...]) * acc
            else:
                o_ref[...] = acc.astype(o_ref.dtype)

    a_spec = pl.BlockSpec((tk, tm), lambda i, j, k: (k, i)) if ta else pl.BlockSpec((tm, tk), lambda i, j, k: (i, k))
    b_spec = pl.BlockSpec((tn, tk), lambda i, j, k: (j, k)) if tb else pl.BlockSpec((tk, tn), lambda i, j, k: (k, j))
    o_spec = pl.BlockSpec((tm, tn), lambda i, j, k: (i, j))
    in_specs, args = [a_spec, b_spec], [a, b]
    out_shape, out_specs = jax.ShapeDtypeStruct((m, n), out_dtype), o_spec
    if gate is not None:
        res, mod4, kmod, _, geo = gate
        in_specs += [o_spec, pl.BlockSpec((None, None, 1, tn), lambda i, j, k: (geo.seg(i, tm), kmod, 0, j))]
        args += [res, mod4]
        out_shape = (jax.ShapeDtypeStruct((m, n), F32), jax.ShapeDtypeStruct((m, n), BF16))
        out_specs = (o_spec, o_spec)
    return _call(
        body, hosted, name=name, grid=(m // tm, n // tn, nk), in_specs=in_specs, out_specs=out_specs,
        out_shape=out_shape, scratch_shapes=[pltpu.VMEM((tm, tn), F32)],
        compiler_params=_params("parallel", "parallel", "arbitrary"),
    )(*args)


def _tn_wide(lhs, rhs, name, into=None, s0=0, hosted=None):
    t, m = lhs.shape
    n = rhs.shape[1]
    tm = _pick(m, (1408, 1024, 512, 256, 128))
    while tm * n * 4 > 6.5 * 1024 * 1024 and tm % 256 == 0:
        tm //= 2
    tk = next(c for c in (1536, 768, 512, 256, 128, t)
              if t % c == 0 and c * (tm + n) * 4 + tm * n * 12 <= 36 * 1024 * 1024)

    def body(l_ref, r_ref, *rest):
        o_ref = rest[-1]
        kk = pl.program_id(1)
        part = lax.dot_general(l_ref[...], r_ref[...], _TN, preferred_element_type=F32)

        @pl.when(kk == 0)
        def _():
            o_ref[...] = part

        @pl.when(kk > 0)
        def _():
            o_ref[...] += part

    in_specs = [pl.BlockSpec((tk, tm), lambda i, k: (k, i)), pl.BlockSpec((tk, n), lambda i, k: (k, 0))]
    if into is None:
        return _call(
            body, hosted, name=name, grid=(m // tm, t // tk), in_specs=in_specs,
            out_specs=pl.BlockSpec((tm, n), lambda i, k: (i, 0)), out_shape=jax.ShapeDtypeStruct((m, n), F32),
            compiler_params=_params("parallel", "arbitrary"),
        )(lhs, rhs)
    return _call(
        body, hosted, name=name, grid=(m // tm, t // tk), in_specs=in_specs + [pl.BlockSpec(memory_space=pl.ANY)],
        out_specs=pl.BlockSpec((None, tm, n), lambda i, k: (s0, i, 0)),
        out_shape=jax.ShapeDtypeStruct(into.shape, into.dtype), input_output_aliases={2: 0},
        compiler_params=_params("parallel", "arbitrary"),
    )(lhs, rhs, into)


def _mod_spec(geo, tile, kmod, d):
    return pl.BlockSpec((None, None, 1, d), lambda i: (geo.seg(i, tile), kmod, 0, 0))


def _pre_fwd(h, g, mod4, k_shift, geo, name):
    t, d = h.shape
    tile = geo.tile

    def body(h_ref, g_ref, sh_ref, sc_ref, o_ref):
        hv = h_ref[...]
        r = lax.rsqrt(jnp.mean(hv * hv, axis=-1, keepdims=True) + EPS)
        y = hv * r * g_ref[...]
        o_ref[...] = (y * (1.0 + sc_ref[...]) + sh_ref[...]).astype(o_ref.dtype)

    row = pl.BlockSpec((tile, d), lambda i: (i, 0))
    return pl.pallas_call(
        body, name=name, grid=(t // tile,),
        in_specs=[row, pl.BlockSpec((1, d), lambda i: (0, 0)), _mod_spec(geo, tile, k_shift, d),
                  _mod_spec(geo, tile, k_shift + 1, d)],
        out_specs=row, out_shape=jax.ShapeDtypeStruct((t, d), BF16), compiler_params=_params("parallel"),
    )(h, g, mod4, mod4)


def _pre_bwd(h, g, mod4, k_shift, dnx, dres, geo, name):
    t, d = h.shape
    tile = geo.tile

    def body(h_ref, g_ref, sc_ref, dnx_ref, dres_ref, dh_ref, dg_ref, dsh_ref, dsc_ref):
        i = pl.program_id(0)
        hv, gv, dout = h_ref[...], g_ref[...], dnx_ref[...].astype(F32)
        r = lax.rsqrt(jnp.mean(hv * hv, axis=-1, keepdims=True) + EPS)
        xhat = hv * r
        dy = dout * (1.0 + sc_ref[...])
        u = dy * gv
        dh_ref[...] = r * (u - xhat * jnp.mean(u * xhat, axis=-1, keepdims=True)) + dres_ref[...]

        @pl.when(i == 0)
        def _():
            dg_ref[...] = jnp.zeros_like(dg_ref)

        @pl.when(geo.seg_start(i, tile))
        def _():
            dsh_ref[...] = jnp.zeros_like(dsh_ref)
            dsc_ref[...] = jnp.zeros_like(dsc_ref)

        dg_ref[...] += jnp.sum(dy * xhat, axis=0, keepdims=True)
        dsh_ref[...] += jnp.sum(dout, axis=0, keepdims=True)
        dsc_ref[...] += jnp.sum(dout * (xhat * gv), axis=0, keepdims=True)

    row = pl.BlockSpec((tile, d), lambda i: (i, 0))
    vec = pl.BlockSpec((1, d), lambda i: (0, 0))
    segv = pl.BlockSpec((None, 1, d), lambda i: (geo.seg(i, tile), 0, 0))
    return pl.pallas_call(
        body, name=name, grid=(t // tile,),
        in_specs=[row, vec, _mod_spec(geo, tile, k_shift + 1, d), row, row],
        out_specs=(row, vec, segv, segv),
        out_shape=(jax.ShapeDtypeStruct((t, d), F32), jax.ShapeDtypeStruct((1, d), F32),
                   jax.ShapeDtypeStruct((N_SEG, 1, d), F32), jax.ShapeDtypeStruct((N_SEG, 1, d), F32)),
        compiler_params=_params("arbitrary"),
    )(h, g, mod4, dnx, dres)


def _gate_bwd(dh, y, mod4, k_gate, coef, geo, name):
    t, d = dh.shape
    tile = geo.tile

    def body(dh_ref, y_ref, gt_ref, dy_ref, dgt_ref):
        i = pl.program_id(0)
        dhv = dh_ref[...]
        dy_ref[...] = ((coef * gt_ref[...]) * dhv).astype(dy_ref.dtype)

        @pl.when(geo.seg_start(i, tile))
        def _():
            dgt_ref[...] = jnp.zeros_like(dgt_ref)

        dgt_ref[...] += coef * jnp.sum(dhv * y_ref[...].astype(F32), axis=0, keepdims=True)

    row = pl.BlockSpec((tile, d), lambda i: (i, 0))
    segv = pl.BlockSpec((None, 1, d), lambda i: (geo.seg(i, tile), 0, 0))
    return pl.pallas_call(
        body, name=name, grid=(t // tile,), in_specs=[row, row, _mod_spec(geo, tile, k_gate, d)],
        out_specs=(row, segv),
        out_shape=(jax.ShapeDtypeStruct((t, d), BF16), jax.ShapeDtypeStruct((N_SEG, 1, d), F32)),
        compiler_params=_params("arbitrary"),
    )(dh, y, mod4)


def _ff_tile(f):
    return _pick(f, (256, 128))


def _segment_rows(ref, rows, geo):
    return jnp.where(rows >= 2 * geo.n_lat, ref[2], jnp.where(rows >= geo.n_lat, ref[1], ref[0]))


def _ffn_up(h, g, mod4, k_shift, w1t, w3t, s0, geo, name, hosted=None):
    t, d = h.shape
    f = w1t.shape[1]
    tm, tn = _pick(t, (768, 512, 256)), _ff_tile(f)

    def body(h_ref, g_ref, sh_ref, sc_ref, w1_ref, w3_ref, nx_ref, ga_ref, gb_ref, act_ref, x_ref):
        i = pl.program_id(0)

        @pl.when(pl.program_id(1) == 0)
        def _():
            rows = i * tm + lax.broadcasted_iota(jnp.int32, (tm, 1), 0)
            hv = h_ref[...]
            r = lax.rsqrt(jnp.mean(hv * hv, axis=-1, keepdims=True) + EPS)
            y = hv * r * g_ref[...]
            nx = (y * (1.0 + _segment_rows(sc_ref, rows, geo)) + _segment_rows(sh_ref, rows, geo)).astype(x_ref.dtype)
            x_ref[...] = nx
            nx_ref[...] = nx

        xv = x_ref[...]
        a = lax.dot_general(xv, w1_ref[...], _NT, preferred_element_type=F32)
        bv = lax.dot_general(xv, w3_ref[...], _NT, preferred_element_type=F32)
        sg = jax.nn.sigmoid(a)
        silu = a * sg
        ga_ref[...] = (bv * (sg + silu * (1.0 - sg))).astype(ga_ref.dtype)
        gb_ref[...] = silu.astype(gb_ref.dtype)
        act_ref[...] = (silu * bv).astype(act_ref.dtype)

    w_spec = pl.BlockSpec((None, tn, d), lambda i, j: (s0, j, 0))
    o_spec = pl.BlockSpec((tm, tn), lambda i, j: (i, j))
    row = pl.BlockSpec((tm, d), lambda i, j: (i, 0))

    def seg_spec(k):
        return pl.BlockSpec((N_SEG, None, 1, d), lambda i, j: (0, k, 0, 0))

    sds = jax.ShapeDtypeStruct((t, f), BF16)
    return _call(
        body, hosted, name=name, grid=(t // tm, f // tn),
        in_specs=[row, pl.BlockSpec((1, d), lambda i, j: (0, 0)), seg_spec(k_shift), seg_spec(k_shift + 1), w_spec, w_spec],
        out_specs=(row, o_spec, o_spec, o_spec), out_shape=(jax.ShapeDtypeStruct((t, d), BF16), sds, sds, sds),
        scratch_shapes=[pltpu.VMEM((tm, d), BF16)], compiler_params=_params("parallel", "arbitrary"),
    )(h, g, mod4, mod4, w1t, w3t)


def _ffn_down(act, w2, s0, res, mod4, k_gate, geo, name, hosted=None):
    t, f = act.shape
    d = w2.shape[2]
    tm, tn = geo.mm_tile, _pick(d, (1024, 512, 256, 128))

    def body(a_ref, w_ref, res_ref, gate_ref, o_ref, y_ref):
        acc = lax.dot_general(a_ref[...], w_ref[...], _NN, preferred_element_type=F32)
        y_ref[...] = acc.astype(y_ref.dtype)
        o_ref[...] = res_ref[...] + (0.5 * gate_ref[...]) * acc

    o_spec = pl.BlockSpec((tm, tn), lambda i, j: (i, j))
    return _call(
        body, hosted, name=name, grid=(t // tm, d // tn),
        in_specs=[pl.BlockSpec((tm, f), lambda i, j: (i, 0)), pl.BlockSpec((None, f, tn), lambda i, j: (s0, 0, j)),
                  o_spec, pl.BlockSpec((None, None, 1, tn), lambda i, j: (geo.seg(i, tm), k_gate, 0, j))],
        out_specs=(o_spec, o_spec),
        out_shape=(jax.ShapeDtypeStruct((t, d), F32), jax.ShapeDtypeStruct((t, d), BF16)),
        compiler_params=_params("parallel", "parallel"),
    )(act, w2, res, mod4)


def _ffn_dact(dh, y, mod4, k_gate, w2, ga, gb, s0, geo, name, hosted=None):
    t, d = dh.shape
    f = w2.shape[1]
    tm, tn = _pick(t, (768, 512, 256)), _ff_tile(f)

    def body(dh_ref, y_ref, gt_ref, w_ref, ga_ref, gb_ref, dy_ref, dgt_ref, da_ref, db_ref, x_ref):
        i, j = pl.program_id(0), pl.program_id(1)

        @pl.when(jnp.logical_and(i == 0, j == 0))
        def _():
            dgt_ref[...] = jnp.zeros_like(dgt_ref)

        @pl.when(j == 0)
        def _():
            rows = i * tm + lax.broadcasted_iota(jnp.int32, (tm, 1), 0)
            dhv = dh_ref[...]
            dyv = ((0.5 * _segment_rows(gt_ref, rows, geo)) * dhv).astype(x_ref.dtype)
            x_ref[...] = dyv
            dy_ref[...] = dyv
            prod = dhv * y_ref[...].astype(F32)
            seg = (rows >= geo.n_lat).astype(jnp.int32) + (rows >= 2 * geo.n_lat).astype(jnp.int32)
            for s in range(N_SEG):
                dgt_ref[s] += 0.5 * jnp.sum(jnp.where(seg == s, prod, 0.0), axis=0, keepdims=True)

        dact = lax.dot_general(x_ref[...], w_ref[...], _NT, preferred_element_type=F32)
        da_ref[...] = (dact * ga_ref[...].astype(F32)).astype(da_ref.dtype)
        db_ref[...] = (dact * gb_ref[...].astype(F32)).astype(db_ref.dtype)

    row = pl.BlockSpec((tm, d), lambda i, j: (i, 0))
    o_spec = pl.BlockSpec((tm, tn), lambda i, j: (i, j))
    sds = jax.ShapeDtypeStruct((t, f), BF16)
    return _call(
        body, hosted, name=name, grid=(t // tm, f // tn),
        in_specs=[row, row, pl.BlockSpec((N_SEG, None, 1, d), lambda i, j: (0, k_gate, 0, 0)),
                  pl.BlockSpec((None, tn, d), lambda i, j: (s0, j, 0)), o_spec, o_spec],
        out_specs=(row, pl.BlockSpec((N_SEG, 1, d), lambda i, j: (0, 0, 0)), o_spec, o_spec),
        out_shape=(jax.ShapeDtypeStruct((t, d), BF16), jax.ShapeDtypeStruct((N_SEG, 1, d), F32), sds, sds),
        scratch_shapes=[pltpu.VMEM((tm, d), BF16)], compiler_params=_params("arbitrary", "arbitrary"),
    )(dh, y, mod4, w2, ga, gb)


def _ffn_dnx(da, db, w1t, w3t, s0, geo, name, hosted=None):
    t, f = da.shape
    d = w1t.shape[2]
    tm, tn = geo.mm_tile, _pick(d, (512, 256, 128))

    def body(da_ref, db_ref, w1_ref, w3_ref, o_ref):
        o_ref[...] = (lax.dot_general(da_ref[...], w1_ref[...], _NN, preferred_element_type=F32)
                      + lax.dot_general(db_ref[...], w3_ref[...], _NN, preferred_element_type=F32))

    x_spec = pl.BlockSpec((tm, f), lambda j, i: (i, 0))
    w_spec = pl.BlockSpec((None, f, tn), lambda j, i: (s0, 0, j))
    return _call(
        body, hosted, name=name, grid=(d // tn, t // tm), in_specs=[x_spec, x_spec, w_spec, w_spec],
        out_specs=pl.BlockSpec((tm, tn), lambda j, i: (i, j)), out_shape=jax.ShapeDtypeStruct((t, d), F32),
        compiler_params=_params("parallel", "parallel"),
    )(da, db, w1t, w3t)


def _with_host(fn, hosts, got, slot, *args, **kw):
    hosted = hosts.get(slot)
    if hosted is None:
        return fn(*args, **kw)
    out, got[slot] = fn(*args, hosted=hosted, **kw)
    return out


def _ffn_fwd(h, g, mod4, k0, w, s0, geo, tag, sub, hosts, got):
    nx, a, b, act = _with_host(_ffn_up, hosts, got, f"{sub}_up", h, g, mod4, k0, w["ffn_w1"], w["ffn_w3"], s0, geo,
                               f"{tag}_up")
    h_out, y = _with_host(_ffn_down, hosts, got, f"{sub}_down", act, w["ffn_w2"], s0, h, mod4, k0 + 2, geo, f"{tag}_down")
    return h_out, (h, nx, a, b, act, y)


def _ffn_bwd(dh_out, saved, g, mod4, k0, w, s0, gbuf, geo, tag, sub, hosts, got):
    h, nx, a, b, act, y = saved
    dy, dgate, da, db = _with_host(_ffn_dact, hosts, got, f"{sub}_dact", dh_out, y, mod4, k0 + 2, w["ffn_w2"], a, b, s0,
                                   geo, f"{tag}_dact")
    gbuf["ffn_w2"] = _with_host(_tn_wide, hosts, got, f"{sub}_dw2", act, dy, f"{tag}_dw2", into=gbuf["ffn_w2"], s0=s0)
    dnx = _with_host(_ffn_dnx, hosts, got, f"{sub}_dnx", da, db, w["ffn_w1"], w["ffn_w3"], s0, geo, f"{tag}_dnx")
    gbuf["ffn_w1"] = _tn_wide(da, nx, f"{tag}_dw1", into=gbuf["ffn_w1"], s0=s0)
    gbuf["ffn_w3"] = _tn_wide(db, nx, f"{tag}_dw3", into=gbuf["ffn_w3"], s0=s0)
    dh, dg, dshift, dscale = _pre_bwd(h, g, mod4, k0, dnx, dh_out, geo, f"{tag}_dpre")
    return dh, dg, (dshift, dscale, dgate)


def _interleave(w, n_parts, tile):
    lead, cols = w.shape[:-1], w.shape[-1] // n_parts
    return w.reshape(*lead, n_parts, cols // tile, tile).swapaxes(-3, -2).reshape(*lead, n_parts * cols)


def _deinterleave(w, n_parts, tile):
    lead, cols = w.shape[:-1], w.shape[-1] // n_parts
    return w.reshape(*lead, cols // tile, n_parts, tile).swapaxes(-3, -2).reshape(*lead, n_parts * cols)


HALO = 16


def _conv_tile(c):
    return _pick(c, (256, 128))


def _conv_specs(geo, tc, t):
    tile = geo.tile
    per = tile // HALO
    last = t // HALO - 1
    cur = pl.BlockSpec((tile, 3 * tc), lambda j, i: (i, j))
    prev = pl.BlockSpec((HALO, 3 * tc), lambda j, i: (jnp.maximum(i * per - 1, 0), j))
    nxt = pl.BlockSpec((HALO, 3 * tc), lambda j, i: (jnp.minimum((i + 1) * per, last), j))
    return cur, prev, nxt


def _conv_edges(geo, i):
    tile = geo.tile
    row = i * tile
    lat = row < 2 * geo.n_lat
    first = jnp.where(lat, row % geo.n_lat == 0, (row - 2 * geo.n_lat) % geo.n_ctx == 0)
    end = row + tile
    last = jnp.where(lat, end % geo.n_lat == 0, (end - 2 * geo.n_lat) % geo.n_ctx == 0)
    return first, last


def _shift_rows(v, before, after):
    n = v.shape[0]
    rows = lax.broadcasted_iota(jnp.int32, v.shape, 0)
    down = jnp.where(rows == 0, before, pltpu.roll(v, 1, 0))
    up = jnp.where(rows == n - 1, after, pltpu.roll(v, n - 1, 0))
    return down, up


def _conv_fwd(proj, conv_w, geo, name, hosted=None):
    t, c3 = proj.shape
    c = c3 // 3
    tc, tile = _conv_tile(c), geo.tile

    def body(cur_ref, prev_ref, next_ref, w_ref, o_ref):
        first, last = _conv_edges(geo, pl.program_id(1))
        bv = cur_ref[:, :tc].astype(F32)
        p = cur_ref[:, tc:2 * tc].astype(F32) * cur_ref[:, 2 * tc:].astype(F32)
        p_before = prev_ref[HALO - 1:HALO, tc:2 * tc].astype(F32) * prev_ref[HALO - 1:HALO, 2 * tc:].astype(F32)
        p_after = next_ref[0:1, tc:2 * tc].astype(F32) * next_ref[0:1, 2 * tc:].astype(F32)
        p_before = jnp.where(first, 0.0, p_before)
        p_after = jnp.where(last, 0.0, p_after)
        pm1, pp1 = _shift_rows(p, p_before, p_after)
        w = w_ref[...]
        q = w[0:1] * pm1 + w[1:2] * p + w[2:3] * pp1
        o_ref[...] = (bv * q).astype(o_ref.dtype)

    cur, prev, nxt = _conv_specs(geo, tc, t)
    return _call(
        body, hosted, name=name, grid=(c // tc, t // tile),
        in_specs=[cur, prev, nxt, pl.BlockSpec((3, tc), lambda j, i: (0, j))],
        out_specs=pl.BlockSpec((tile, tc), lambda j, i: (i, j)), out_shape=jax.ShapeDtypeStruct((t, c), BF16),
        compiler_params=_params("parallel", "parallel"),
    )(proj, proj, proj, conv_w)


def _conv_bwd(proj, dyc, conv_w, geo, name, hosted=None):
    t, c3 = proj.shape
    c = c3 // 3
    tc, tile = _conv_tile(c), geo.tile

    def body(cur_ref, prev_ref, next_ref, d_ref, dprev_ref, dnext_ref, w_ref, o_ref, dw_ref):
        i = pl.program_id(1)
        first, last = _conv_edges(geo, i)
        bv = cur_ref[:, :tc].astype(F32)
        cv = cur_ref[:, tc:2 * tc].astype(F32)
        uv = cur_ref[:, 2 * tc:].astype(F32)
        p = cv * uv
        p_before = prev_ref[HALO - 1:HALO, tc:2 * tc].astype(F32) * prev_ref[HALO - 1:HALO, 2 * tc:].astype(F32)
        p_after = next_ref[0:1, tc:2 * tc].astype(F32) * next_ref[0:1, 2 * tc:].astype(F32)
        p_before = jnp.where(first, 0.0, p_before)
        p_after = jnp.where(last, 0.0, p_after)
        pm1, pp1 = _shift_rows(p, p_before, p_after)
        w = w_ref[...]
        q = w[0:1] * pm1 + w[1:2] * p + w[2:3] * pp1
        dy = d_ref[...].astype(F32)
        dq = dy * bv
        dq_before = dprev_ref[HALO - 1:HALO, :].astype(F32) * prev_ref[HALO - 1:HALO, :tc].astype(F32)
        dq_after = dnext_ref[0:1, :].astype(F32) * next_ref[0:1, :tc].astype(F32)
        dq_before = jnp.where(first, 0.0, dq_before)
        dq_after = jnp.where(last, 0.0, dq_after)
        dqm1, dqp1 = _shift_rows(dq, dq_before, dq_after)
        dp = w[0:1] * dqp1 + w[1:2] * dq + w[2:3] * dqm1
        o_ref[:, :tc] = (dy * q).astype(o_ref.dtype)
        o_ref[:, tc:2 * tc] = (dp * uv).astype(o_ref.dtype)
        o_ref[:, 2 * tc:] = (dp * cv).astype(o_ref.dtype)

        @pl.when(i == 0)
        def _():
            dw_ref[...] = jnp.zeros_like(dw_ref)

        dw_ref[0:1, :] += jnp.sum(dq * pm1, axis=0, keepdims=True)
        dw_ref[1:2, :] += jnp.sum(dq * p, axis=0, keepdims=True)
        dw_ref[2:3, :] += jnp.sum(dq * pp1, axis=0, keepdims=True)

    cur, prev, nxt = _conv_specs(geo, tc, t)
    per, lastb = tile // HALO, t // HALO - 1
    dcur = pl.BlockSpec((tile, tc), lambda j, i: (i, j))
    dprev = pl.BlockSpec((HALO, tc), lambda j, i: (jnp.maximum(i * per - 1, 0), j))
    dnext = pl.BlockSpec((HALO, tc), lambda j, i: (jnp.minimum((i + 1) * per, lastb), j))
    wspec = pl.BlockSpec((3, tc), lambda j, i: (0, j))
    return _call(
        body, hosted, name=name, grid=(c // tc, t // tile), in_specs=[cur, prev, nxt, dcur, dprev, dnext, wspec],
        out_specs=(cur, wspec), out_shape=(jax.ShapeDtypeStruct((t, c3), BF16), jax.ShapeDtypeStruct((3, c), F32)),
        compiler_params=_params("parallel", "arbitrary"),
    )(proj, proj, proj, dyc, dyc, dyc, conv_w)


def _sconv_fwd(h, g, mod4, w_in, conv_w, w_out, geo, tag, hosts, got):
    nx = _pre_fwd(h, g, mod4, 3, geo, f"{tag}_pre")
    proj = _with_host(_mm, hosts, got, "mix_a", nx, w_in, out_dtype=BF16, name=f"{tag}_in")
    yc = _with_host(_conv_fwd, hosts, got, "mix_b", proj, conv_w, geo, f"{tag}_conv")
    h_out, y = _mm(yc, w_out, name=f"{tag}_out", gate=(h, mod4, 5, 1.0, geo))
    return h_out, (h, nx, proj, yc, y)


def _sconv_bwd(dh_out, saved, g, mod4, w_in, conv_w, w_out, geo, tag, hosts, got):
    h, nx, proj, yc, y = saved
    dy, dgate = _gate_bwd(dh_out, y, mod4, 5, 1.0, geo, f"{tag}_dgate")
    dyc = _mm(dy, w_out, tb=True, out_dtype=BF16, name=f"{tag}_dyc")
    dw_out = _tn_wide(yc, dy, f"{tag}_dwout")
    dproj, dconv = _with_host(_conv_bwd, hosts, got, "mix_c", proj, dyc, conv_w, geo, f"{tag}_dconv")
    dnx = _with_host(_mm, hosts, got, "mix_b", dproj, w_in, tb=True, name=f"{tag}_dnx")
    dw_in = _with_host(_tn_wide, hosts, got, "mix_a", nx, dproj, f"{tag}_dwin")
    dh, dg, dshift, dscale = _pre_bwd(h, g, mod4, 3, dnx, dh_out, geo, f"{tag}_dpre")
    return dh, dg, (dshift, dscale, dgate), dw_in, dconv, dw_out


def _rope_swap(v):
    nf = QK_ROPE // 4
    return v.reshape(v.shape[:-1] + (2, 2, nf)).swapaxes(-3, -2).reshape(v.shape)


def _rope_tables(geo):
    n = geo.n_lat
    nf = QK_ROPE // 4
    pos = np.arange(n)
    inv = ROPE_BASE ** (-np.arange(nf, dtype=np.float32) / nf)
    ang = np.concatenate([(pos // GRID_W)[:, None] * inv, (pos % GRID_W)[:, None] * inv], axis=1).astype(np.float32)
    cos, sin = np.cos(ang), np.sin(ang)
    zeros = np.zeros((n, LANE - QK_ROPE), np.float32)
    c_lat = np.concatenate([cos, cos, zeros], axis=1)
    s_lat = np.concatenate([-sin, sin, zeros], axis=1)
    c_ctx = np.concatenate([np.ones((2 * geo.n_ctx, QK_ROPE), np.float32), np.zeros((2 * geo.n_ctx, LANE - QK_ROPE), np.float32)], 1)
    s_ctx = np.zeros((2 * geo.n_ctx, LANE), np.float32)
    return (jnp.asarray(np.concatenate([c_lat, c_lat, c_ctx], 0)), jnp.asarray(np.concatenate([s_lat, s_lat, s_ctx], 0)))


def _swap_halves(v):
    lanes = lax.broadcasted_iota(jnp.int32, v.shape, 1)
    return jnp.where(lanes < QK_ROPE // 2, pltpu.roll(v, LANE - QK_ROPE // 2, 1), pltpu.roll(v, QK_ROPE // 2, 1))


def _latent_norm_fwd(down, g_qa, g_kva, geo, name):
    t, wd = down.shape
    ql, kl = g_qa.shape[1], g_kva.shape[1]
    tile = geo.tile

    def body(d_ref, gq_ref, gk_ref, cq_ref, ckv_ref):
        for lo, n, g_ref, o_ref in ((0, ql, gq_ref, cq_ref), (ql, kl, gk_ref, ckv_ref)):
            x = d_ref[:, lo:lo + n]
            r = lax.rsqrt(jnp.mean(x * x, axis=-1, keepdims=True) + EPS)
            o_ref[...] = (x * r * g_ref[...]).astype(o_ref.dtype)

    return pl.pallas_call(
        body, name=name, grid=(t // tile,),
        in_specs=[pl.BlockSpec((tile, wd), lambda i: (i, 0)), pl.BlockSpec((1, ql), lambda i: (0, 0)),
                  pl.BlockSpec((1, kl), lambda i: (0, 0))],
        out_specs=(pl.BlockSpec((tile, ql), lambda i: (i, 0)), pl.BlockSpec((tile, kl), lambda i: (i, 0))),
        out_shape=(jax.ShapeDtypeStruct((t, ql), BF16), jax.ShapeDtypeStruct((t, kl), BF16)),
        compiler_params=_params("parallel"),
    )(down, g_qa, g_kva)


def _latent_norm_bwd(down, g_qa, g_kva, dcqn, dckvn, dkr, geo, name):
    t, wd = down.shape
    ql, kl = g_qa.shape[1], g_kva.shape[1]
    tile = geo.tile

    def body(d_ref, gq_ref, gk_ref, dq_ref, dk_ref, dkr_ref, o_ref, dgq_ref, dgk_ref):
        i = pl.program_id(0)

        @pl.when(i == 0)
        def _():
            dgq_ref[...] = jnp.zeros_like(dgq_ref)
            dgk_ref[...] = jnp.zeros_like(dgk_ref)

        for lo, n, g_ref, dy_ref, dg_ref in ((0, ql, gq_ref, dq_ref, dgq_ref), (ql, kl, gk_ref, dk_ref, dgk_ref)):
            x = d_ref[:, lo:lo + n]
            dy = dy_ref[...].astype(F32)
            r = lax.rsqrt(jnp.mean(x * x, axis=-1, keepdims=True) + EPS)
            xhat = x * r
            u = dy * g_ref[...]
            o_ref[:, lo:lo + n] = (r * (u - xhat * jnp.mean(u * xhat, axis=-1, keepdims=True))).astype(o_ref.dtype)
            dg_ref[...] += jnp.sum(dy * xhat, axis=0, keepdims=True)
        o_ref[:, ql + kl:] = dkr_ref[...].astype(o_ref.dtype)

    def row(n):
        return pl.BlockSpec((tile, n), lambda i: (i, 0))

    def vec(n):
        return pl.BlockSpec((1, n), lambda i: (0, 0))

    return pl.pallas_call(
        body, name=name, grid=(t // tile,),
        in_specs=[row(wd), vec(ql), vec(kl), row(ql), row(kl), row(wd - ql - kl)],
        out_specs=(row(wd), vec(ql), vec(kl)),
        out_shape=(jax.ShapeDtypeStruct((t, wd), BF16), jax.ShapeDtypeStruct((1, ql), F32),
                   jax.ShapeDtypeStruct((1, kl), F32)),
        compiler_params=_params("arbitrary"),
    )(down, g_qa, g_kva, dcqn, dckvn, dkr)


def _qk_specs(geo, xr_col, shared_rope):
    tile = geo.mm_tile
    xn_spec = pl.BlockSpec((tile, HEADS * LANE), lambda i: (i, 0))
    if shared_rope:
        xr_spec = pl.BlockSpec((tile, LANE), lambda i: (i, xr_col))
    else:
        xr_spec = pl.BlockSpec((tile, HEADS * LANE), lambda i: (i, xr_col // HEADS))
    vec = pl.BlockSpec((1, LANE), lambda i: (0, 0))
    tab = pl.BlockSpec((tile, LANE), lambda i: (i, 0))
    return tile, xn_spec, xr_spec, vec, tab


def _qk_norm(xn, xr):
    ss = jnp.sum(xn * xn, axis=-1, keepdims=True) + jnp.sum(xr * xr, axis=-1, keepdims=True)
    return lax.rsqrt(ss * (1.0 / QK_HEAD) + EPS)


def _head_lanes(ref, hh, shared=False):
    return ref[...] if shared else ref[:, hh * LANE:(hh + 1) * LANE]


def _qk_fwd(xn_arr, xr_arr, xr_col, shared_rope, gn, gr, cos, sin, geo, name):
    t = xn_arr.shape[0]
    tile, xn_spec, xr_spec, vec, tab = _qk_specs(geo, xr_col, shared_rope)

    def body(xn_ref, xr_ref, gn_ref, gr_ref, c_ref, s_ref, o_ref):
        cv, sv, gnv, grv = c_ref[...], s_ref[...], gn_ref[...], gr_ref[...]
        for hh in range(HEADS):
            xn = _head_lanes(xn_ref, hh).astype(F32)
            xr = _head_lanes(xr_ref, hh, shared_rope).astype(F32)
            r = _qk_norm(xn, xr)
            yr = xr * r * grv
            o_ref[:, hh * HEAD_PAD:hh * HEAD_PAD + LANE] = (xn * r * gnv).astype(o_ref.dtype)
            o_ref[:, hh * HEAD_PAD + LANE:(hh + 1) * HEAD_PAD] = (yr * cv + _swap_halves(yr) * sv).astype(o_ref.dtype)

    return pl.pallas_call(
        body, name=name, grid=(t // tile,), in_specs=[xn_spec, xr_spec, vec, vec, tab, tab],
        out_specs=pl.BlockSpec((tile, HEADS * HEAD_PAD), lambda i: (i, 0)),
        out_shape=jax.ShapeDtypeStruct((t, HEADS * HEAD_PAD), BF16), compiler_params=_params("parallel"),
    )(xn_arr, xr_arr, gn, gr, cos, sin)


def _qk_bwd(xn_arr, xr_arr, xr_col, shared_rope, gn, gr, cos, sin, dout, geo, name):
    t = xn_arr.shape[0]
    tile, xn_spec, xr_spec, vec, tab = _qk_specs(geo, xr_col, shared_rope)

    def body(xn_ref, xr_ref, gn_ref, gr_ref, c_ref, s_ref, d_ref, dxn_ref, dxr_ref, dgn_ref, dgr_ref):
        i = pl.program_id(0)
        cv, sv, gnv, grv = c_ref[...], s_ref[...], gn_ref[...], gr_ref[...]
        dgn = jnp.zeros((1, LANE), F32)
        dgr = jnp.zeros((1, LANE), F32)
        dxr_sum = jnp.zeros((tile, LANE), F32)
        for hh in range(HEADS):
            xn = _head_lanes(xn_ref, hh).astype(F32)
            xr = _head_lanes(xr_ref, hh, shared_rope).astype(F32)
            r = _qk_norm(xn, xr)
            xhn, xhr = xn * r, xr * r
            dyn = d_ref[:, hh * HEAD_PAD:hh * HEAD_PAD + LANE].astype(F32)
            dro = d_ref[:, hh * HEAD_PAD + LANE:(hh + 1) * HEAD_PAD].astype(F32)
            dyr = dro * cv + _swap_halves(dro * sv)
            un, ur = dyn * gnv, dyr * grv
            mean = (jnp.sum(un * xhn, axis=-1, keepdims=True) + jnp.sum(ur * xhr, axis=-1, keepdims=True)) * (1.0 / QK_HEAD)
            dxn_ref[:, hh * LANE:(hh + 1) * LANE] = (r * (un - xhn * mean)).astype(dxn_ref.dtype)
            dxr = r * (ur - xhr * mean)
            if shared_rope:
                dxr_sum = dxr_sum + dxr
            else:
                dxr_ref[:, hh * LANE:(hh + 1) * LANE] = dxr.astype(dxr_ref.dtype)
            dgn = dgn + jnp.sum(dyn * xhn, axis=0, keepdims=True)
            dgr = dgr + jnp.sum(dyr * xhr, axis=0, keepdims=True)
        if shared_rope:
            dxr_ref[...] = dxr_sum

        @pl.when(i == 0)
        def _():
            dgn_ref[...] = jnp.zeros_like(dgn_ref)
            dgr_ref[...] = jnp.zeros_like(dgr_ref)

        dgn_ref[...] += dgn
        dgr_ref[...] += dgr

    heads = pl.BlockSpec((tile, HEADS * LANE), lambda i: (i, 0))
    if shared_rope:
        dxr_spec, dxr_shape = pl.BlockSpec((tile, LANE), lambda i: (i, 0)), jax.ShapeDtypeStruct((t, LANE), F32)
    else:
        dxr_spec, dxr_shape = heads, jax.ShapeDtypeStruct((t, HEADS * LANE), BF16)
    return pl.pallas_call(
        body, name=name, grid=(t // tile,),
        in_specs=[xn_spec, xr_spec, vec, vec, tab, tab, pl.BlockSpec((tile, HEADS * HEAD_PAD), lambda i: (i, 0))],
        out_specs=(heads, dxr_spec, vec, vec),
        out_shape=(jax.ShapeDtypeStruct((t, HEADS * LANE), BF16), dxr_shape, jax.ShapeDtypeStruct((1, LANE), F32),
                   jax.ShapeDtypeStruct((1, LANE), F32)),
        compiler_params=_params("arbitrary"),
    )(xn_arr, xr_arr, gn, gr, cos, sin, dout)


def _attn_specs(geo):
    tq, nq = geo.n_ctx, geo.n_lat // geo.n_ctx

    def qrow(b, i):
        return jnp.where(i < nq, b * nq + i, 2 * nq + b)

    q_spec = pl.BlockSpec((tq, HEAD_PAD), lambda b, hh, i: (qrow(b, i), hh))
    kc_spec = pl.BlockSpec((geo.n_ctx, HEAD_PAD), lambda b, hh, i: (2 * nq + b, hh))
    kl_spec = pl.BlockSpec((geo.n_lat, HEAD_PAD), lambda b, hh, i: (b, hh))
    vc_spec = pl.BlockSpec((geo.n_ctx, V_HEAD), lambda b, hh, i: (2 * nq + b, HEADS + hh))
    vl_spec = pl.BlockSpec((geo.n_lat, V_HEAD), lambda b, hh, i: (b, HEADS + hh))
    o_spec = pl.BlockSpec((tq, V_HEAD), lambda b, hh, i: (qrow(b, i), hh))
    return tq, nq, q_spec, kc_spec, kl_spec, vc_spec, vl_spec, o_spec


def _attn_fwd(q, k, kv, with_ctx_q, geo, name, hosted=None):
    t = q.shape[0]
    tq, nq, q_spec, kc_spec, kl_spec, vc_spec, vl_spec, o_spec = _attn_specs(geo)

    def body(q_ref, kc_ref, kl_ref, vc_ref, vl_ref, o_ref):
        i = pl.program_id(2)
        qv = q_ref[...]
        s_c = _dot(qv, kc_ref[...], _NT) * QK_SCALE

        @pl.when(i < nq)
        def _():
            s_l = _dot(qv, kl_ref[...], _NT) * QK_SCALE
            m = jnp.maximum(jnp.max(s_c, axis=-1, keepdims=True), jnp.max(s_l, axis=-1, keepdims=True))
            p_c, p_l = jnp.exp(s_c - m), jnp.exp(s_l - m)
            den = jnp.sum(p_c, axis=-1, keepdims=True) + jnp.sum(p_l, axis=-1, keepdims=True)
            o = _dot(p_c, vc_ref[...], _NN) + _dot(p_l, vl_ref[...], _NN)
            o_ref[...] = (o / den).astype(o_ref.dtype)

        @pl.when(i == nq)
        def _():
            if with_ctx_q:
                m = jnp.max(s_c, axis=-1, keepdims=True)
                p_c = jnp.exp(s_c - m)
                o = _dot(p_c, vc_ref[...], _NN) / jnp.sum(p_c, axis=-1, keepdims=True)
                o_ref[...] = o.astype(o_ref.dtype)
            else:
                o_ref[...] = jnp.zeros_like(o_ref)

    return _call(
        body, hosted, name=name, grid=(2, HEADS, nq + 1), in_specs=[q_spec, kc_spec, kl_spec, vc_spec, vl_spec],
        out_specs=o_spec, out_shape=jax.ShapeDtypeStruct((t, HEADS * V_HEAD), BF16),
        compiler_params=_params("parallel", "parallel", "arbitrary"),
    )(q, k, k, kv, kv)


def _attn_bwd(q, k, kv, do, with_ctx_q, geo, name, hosted=None):
    t = q.shape[0]
    tq, nq, q_spec, kc_spec, kl_spec, vc_spec, vl_spec, o_spec = _attn_specs(geo)

    def body(q_ref, kc_ref, kl_ref, vc_ref, vl_ref, do_ref, dq_ref, dkl_ref, dkc_ref, dvl_ref, dvc_ref,
             akl_ref, akc_ref, avl_ref, avc_ref):
        i = pl.program_id(2)

        @pl.when(i == 0)
        def _():
            for ref in (akl_ref, akc_ref, avl_ref, avc_ref):
                ref[...] = jnp.zeros_like(ref)

        qv, dov = q_ref[...], do_ref[...]
        s_c = _dot(qv, kc_ref[...], _NT) * QK_SCALE
        dp_c = _dot(dov, vc_ref[...], _NT)

        def ctx_part(p_c, delta):
            ds_c = (p_c * (dp_c - delta) * QK_SCALE).astype(BF16)
            akc_ref[...] += _dot(ds_c, qv, _TN)
            avc_ref[...] += _dot(p_c, dov, _TN)
            return _dot(ds_c, kc_ref[...], _NN)

        @pl.when(i < nq)
        def _():
            s_l = _dot(qv, kl_ref[...], _NT) * QK_SCALE
            m = jnp.maximum(jnp.max(s_c, axis=-1, keepdims=True), jnp.max(s_l, axis=-1, keepdims=True))
            p_c, p_l = jnp.exp(s_c - m), jnp.exp(s_l - m)
            inv = 1.0 / (jnp.sum(p_c, axis=-1, keepdims=True) + jnp.sum(p_l, axis=-1, keepdims=True))
            p_c, p_l = p_c * inv, p_l * inv
            dp_l = _dot(dov, vl_ref[...], _NT)
            delta = jnp.sum(p_c * dp_c, axis=-1, keepdims=True) + jnp.sum(p_l * dp_l, axis=-1, keepdims=True)
            ds_l = (p_l * (dp_l - delta) * QK_SCALE).astype(BF16)
            akl_ref[...] += _dot(ds_l, qv, _TN)
            avl_ref[...] += _dot(p_l, dov, _TN)
            dq_ref[...] = (ctx_part(p_c, delta) + _dot(ds_l, kl_ref[...], _NN)).astype(dq_ref.dtype)

        @pl.when(i == nq)
        def _():
            if with_ctx_q:
                m = jnp.max(s_c, axis=-1, keepdims=True)
                p_c = jnp.exp(s_c - m)
                p_c = p_c * (1.0 / jnp.sum(p_c, axis=-1, keepdims=True))
                delta = jnp.sum(p_c * dp_c, axis=-1, keepdims=True)
                dq_ref[...] = ctx_part(p_c, delta).astype(dq_ref.dtype)
            else:
                dq_ref[...] = jnp.zeros_like(dq_ref)
            dkl_ref[...] = akl_ref[...].astype(dkl_ref.dtype)
            dkc_ref[...] = akc_ref[...].astype(dkc_ref.dtype)
            dvl_ref[...] = avl_ref[...].astype(dvl_ref.dtype)
            dvc_ref[...] = avc_ref[...].astype(dvc_ref.dtype)

    def acc_spec(rows, width):
        return pl.BlockSpec((rows, width), lambda b, hh, i: (b, hh))

    return _call(
        body, hosted, name=name, grid=(2, HEADS, nq + 1), in_specs=[q_spec, kc_spec, kl_spec, vc_spec, vl_spec, o_spec],
        out_specs=(q_spec, acc_spec(geo.n_lat, HEAD_PAD), acc_spec(geo.n_ctx, HEAD_PAD), acc_spec(geo.n_lat, V_HEAD),
                   acc_spec(geo.n_ctx, V_HEAD)),
        out_shape=(jax.ShapeDtypeStruct((t, HEADS * HEAD_PAD), BF16),
                   jax.ShapeDtypeStruct((2 * geo.n_lat, HEADS * HEAD_PAD), BF16),
                   jax.ShapeDtypeStruct((2 * geo.n_ctx, HEADS * HEAD_PAD), BF16),
                   jax.ShapeDtypeStruct((2 * geo.n_lat, HEADS * V_HEAD), BF16),
                   jax.ShapeDtypeStruct((2 * geo.n_ctx, HEADS * V_HEAD), BF16)),
        scratch_shapes=[pltpu.VMEM((geo.n_lat, HEAD_PAD), F32), pltpu.VMEM((geo.n_ctx, HEAD_PAD), F32),
                        pltpu.VMEM((geo.n_lat, V_HEAD), F32), pltpu.VMEM((geo.n_ctx, V_HEAD), F32)],
        compiler_params=_params("parallel", "parallel", "arbitrary"),
    )(q, k, k, kv, kv, do)


def _mla_fwd(h, g, mod4, w, with_ctx_q, tabs, geo, tag, hosts, got):
    cos, sin = tabs
    ql, kl = w["g_qa"].shape[1], w["g_kva"].shape[1]
    kr_col = (ql + kl) // LANE
    nx = _pre_fwd(h, g, mod4, 3, geo, f"{tag}_pre")
    down = _mm(nx, w["w_a"], name=f"{tag}_down")
    cqn, ckvn = _latent_norm_fwd(down, w["g_qa"], w["g_kva"], geo, f"{tag}_lnorm")
    qraw = _mm(cqn, w["w_uq"], out_dtype=BF16, name=f"{tag}_uq")
    kvraw = _mm(ckvn, w["w_ukv"], out_dtype=BF16, name=f"{tag}_ukv")
    q = _qk_fwd(qraw, qraw, HEADS, False, w["gq_n"], w["gq_r"], cos, sin, geo, f"{tag}_qnorm")
    k = _qk_fwd(kvraw, down, kr_col, True, w["gk_n"], w["gk_r"], cos, sin, geo, f"{tag}_knorm")
    o = _with_host(_attn_fwd, hosts, got, "mix_a", q, k, kvraw, with_ctx_q, geo, f"{tag}_attn")
    h_out, y = _mm(o, w["w_o"], name=f"{tag}_o", gate=(h, mod4, 5, 1.0, geo))
    return h_out, (h, nx, down, cqn, ckvn, qraw, kvraw, q, k, o, y)


def _mla_bwd(dh_out, saved, g, mod4, w, with_ctx_q, tabs, geo, tag, hosts, got):
    cos, sin = tabs
    h, nx, down, cqn, ckvn, qraw, kvraw, q, k, o, y = saved
    ql, kl = w["g_qa"].shape[1], w["g_kva"].shape[1]
    kr_col = (ql + kl) // LANE
    dy, dgate = _gate_bwd(dh_out, y, mod4, 5, 1.0, geo, f"{tag}_dgate")
    do = _mm(dy, w["w_o"], tb=True, out_dtype=BF16, name=f"{tag}_do")
    dw_o = _tn_wide(o, dy, f"{tag}_dwo")
    dq, dk_lat, dk_ctx, dv_lat, dv_ctx = _with_host(_attn_bwd, hosts, got, "mix_a", q, k, kvraw, do, with_ctx_q, geo,
                                                    f"{tag}_dattn")
    dk = jnp.concatenate([dk_lat, dk_ctx], axis=0)
    dqn, dqr, dgq_n, dgq_r = _qk_bwd(qraw, qraw, HEADS, False, w["gq_n"], w["gq_r"], cos, sin, dq, geo, f"{tag}_dqnorm")
    dkn, dkr, dgk_n, dgk_r = _qk_bwd(kvraw, down, kr_col, True, w["gk_n"], w["gk_r"], cos, sin, dk, geo, f"{tag}_dknorm")
    dqraw = jnp.concatenate([dqn, dqr], axis=1)
    dkvraw = jnp.concatenate([dkn, jnp.concatenate([dv_lat, dv_ctx], axis=0)], axis=1)
    dcqn = _mm(dqraw, w["w_uq"], tb=True, out_dtype=BF16, name=f"{tag}_dcqn")
    dw_uq = _tn_wide(cqn, dqraw, f"{tag}_dwuq")
    dckvn = _mm(dkvraw, w["w_ukv"], tb=True, out_dtype=BF16, name=f"{tag}_dckvn")
    dw_ukv = _tn_wide(ckvn, dkvraw, f"{tag}_dwukv")
    ddown, dg_qa, dg_kva = _latent_norm_bwd(down, w["g_qa"], w["g_kva"], dcqn, dckvn, dkr, geo, f"{tag}_dlnorm")
    dnx = _mm(ddown, w["w_a"], tb=True, name=f"{tag}_dnx")
    dw_a = _tn_wide(nx, ddown, f"{tag}_dwa")
    dh, dg, dshift, dscale = _pre_bwd(h, g, mod4, 3, dnx, dh_out, geo, f"{tag}_dpre")
    grads = dict(w_a=dw_a, g_qa=dg_qa, w_uq=dw_uq, g_kva=dg_kva, w_ukv=dw_ukv, gq_n=dgq_n, gq_r=dgq_r, gk_n=dgk_n,
                 gk_r=dgk_r, w_o=dw_o)
    return dh, dg, (dshift, dscale, dgate), grads


def _mla_prepare(w_a, g_qa, w_uq, g_kva, w_ukv, g_q, g_k, w_o):
    ql, kl = g_qa.shape[0], g_kva.shape[0]
    d = w_a.shape[0]
    w_a_pad = jnp.concatenate([w_a[:, :ql + kl], _rope_swap(w_a[:, ql + kl:]), jnp.zeros((d, LANE - QK_ROPE), w_a.dtype)], axis=1)
    uq = w_uq.reshape(ql, HEADS, QK_HEAD)
    uq_r = jnp.pad(_rope_swap(uq[:, :, QK_NOPE:]), ((0, 0), (0, 0), (0, LANE - QK_ROPE)))
    w_uq_pad = jnp.concatenate([uq[:, :, :QK_NOPE].reshape(ql, HEADS * LANE), uq_r.reshape(ql, HEADS * LANE)], axis=1)
    ukv = w_ukv.reshape(kl, HEADS, QK_NOPE + V_HEAD)
    w_ukv_p = jnp.concatenate([ukv[:, :, :QK_NOPE].reshape(kl, HEADS * LANE), ukv[:, :, QK_NOPE:].reshape(kl, HEADS * V_HEAD)], axis=1)

    def gains(gv):
        gv = gv.astype(F32)
        return gv[None, :QK_NOPE], jnp.pad(_rope_swap(gv[QK_NOPE:]), (0, LANE - QK_ROPE))[None]

    gq_n, gq_r = gains(g_q)
    gk_n, gk_r = gains(g_k)
    return dict(w_a=w_a_pad, g_qa=g_qa.astype(F32)[None], w_uq=w_uq_pad, g_kva=g_kva.astype(F32)[None], w_ukv=w_ukv_p,
                gq_n=gq_n, gq_r=gq_r, gk_n=gk_n, gk_r=gk_r, w_o=w_o)


def _mla_unprepare(gr):
    ql, kl = gr["g_qa"].shape[1], gr["g_kva"].shape[1]
    dw_a = jnp.concatenate([gr["w_a"][:, :ql + kl], _rope_swap(gr["w_a"][:, ql + kl:ql + kl + QK_ROPE])], axis=1)
    uqn = gr["w_uq"][:, :HEADS * LANE].reshape(ql, HEADS, LANE)
    uqr = _rope_swap(gr["w_uq"][:, HEADS * LANE:].reshape(ql, HEADS, LANE)[:, :, :QK_ROPE])
    dw_uq = jnp.concatenate([uqn, uqr], axis=2).reshape(ql, HEADS * QK_HEAD)
    ukn = gr["w_ukv"][:, :HEADS * LANE].reshape(kl, HEADS, LANE)
    ukv = gr["w_ukv"][:, HEADS * LANE:].reshape(kl, HEADS, V_HEAD)
    dw_ukv = jnp.concatenate([ukn, ukv], axis=2).reshape(kl, HEADS * (QK_NOPE + V_HEAD))

    def gains(gn, grr):
        return jnp.concatenate([gn[0], _rope_swap(grr[0, :QK_ROPE])])

    return dict(mla_w_a=dw_a, mla_g_qa=gr["g_qa"][0], mla_w_uq=dw_uq, mla_g_kva=gr["g_kva"][0], mla_w_ukv=dw_ukv,
                mla_g_q=gains(gr["gq_n"], gr["gq_r"]), mla_g_k=gains(gr["gk_n"], gr["gk_r"]), mla_w_o=gr["w_o"])


def _loss_head(h, target, geo, name):
    t, d = h.shape
    tile = geo.tile
    n_lat_tiles = 2 * geo.n_lat // tile

    def body(h_ref, t_ref, dh_ref, loss_ref):
        i = pl.program_id(0)

        @pl.when(i == 0)
        def _():
            loss_ref[...] = jnp.zeros_like(loss_ref)

        @pl.when(i < n_lat_tiles)
        def _():
            e = h_ref[...] - t_ref[...]
            dh_ref[...] = e * (1.0 / d)
            part = jnp.sum(e * e, axis=0, keepdims=True) * (0.5 / d)
            loss_ref[...] += sum(part[:, j * LANE:(j + 1) * LANE] for j in range(d // LANE))

        @pl.when(i >= n_lat_tiles)
        def _():
            dh_ref[...] = jnp.zeros_like(dh_ref)

    row = pl.BlockSpec((tile, d), lambda i: (i, 0))
    tgt = pl.BlockSpec((tile, d), lambda i: (jnp.minimum(i, n_lat_tiles - 1), 0))
    dh, loss = pl.pallas_call(
        body, name=name, grid=(t // tile,), in_specs=[row, tgt],
        out_specs=(row, pl.BlockSpec((1, LANE), lambda i: (0, 0))),
        out_shape=(jax.ShapeDtypeStruct((t, d), F32), jax.ShapeDtypeStruct((1, LANE), F32)),
        compiler_params=_params("arbitrary"),
    )(h, target)
    return jnp.sum(loss), dh


def _adamw(w, g, m, v, name):
    shape = w.shape
    cols = shape[-1]
    rows = int(np.prod(shape[:-1])) if len(shape) > 1 else 1
    w2, g2, m2, v2 = (a.reshape(rows, cols) for a in (w, g, m, v))
    tr = _pick(rows, (512, 256, 128, 64, 32, 16, 8))
    c1 = 1.0 / (1.0 - ADAM_B1 ** ADAM_STEP)
    c2 = 1.0 / (1.0 - ADAM_B2 ** ADAM_STEP)

    def body(w_ref, g_ref, m_ref, v_ref, d_ref, mo_ref, vo_ref):
        gv = g_ref[...]
        mn = ADAM_B1 * m_ref[...] + (1.0 - ADAM_B1) * gv
        vn = ADAM_B2 * v_ref[...] + (1.0 - ADAM_B2) * (gv * gv)
        d_ref[...] = -ADAM_LR * ((mn * c1) / (jnp.sqrt(vn * c2) + ADAM_EPS) + ADAM_WD * w_ref[...])
        mo_ref[...] = mn
        vo_ref[...] = vn

    blk = pl.BlockSpec((tr, cols), lambda i: (i, 0))
    sds = jax.ShapeDtypeStruct((rows, cols), F32)
    d, mo, vo = pl.pallas_call(
        body, name=name, grid=(rows // tr,), in_specs=[blk] * 4, out_specs=(blk,) * 3, out_shape=(sds,) * 3,
        compiler_params=_params("parallel"),
    )(w2, g2, m2, v2)
    return d.reshape(shape), mo.reshape(shape), vo.reshape(shape)


SHARD_AXIS = {
    "w_mod": 2, "g_norm": 2, "ffn_w1": 3, "ffn_w3": 3, "ffn_w2": 2, "sc_w_in": 2, "sc_conv": 2, "sc_w_out": 1,
    "mla_w_a": 1, "mla_g_qa": 1, "mla_w_uq": 2, "mla_w_ukv": 2, "mla_w_o": 1,
}
HIDDEN_MAJOR = ("ffn_w1", "ffn_w3")


def _view(name, arr, swapped=False):
    form, swap, _ = EXCHANGE[name]
    if swap and not swapped:
        arr = jnp.swapaxes(arr, -1, -2)
    if form == "mid":
        arr = arr.reshape((-1,) + arr.shape[-2:])
        return jnp.pad(arr, ((0, 0), (0, 0), (0, -arr.shape[-1] % LANE)))
    arr = arr.reshape(-1, arr.shape[-1])
    return jnp.pad(arr, ((0, -arr.shape[0] % 16), (0, 0)))


def _unview(name, view, shape, keep_swapped=False):
    form, swap, _ = EXCHANGE[name]
    shape = shape[:-2] + (shape[-1], shape[-2]) if swap else shape
    if form == "mid":
        view = view[:, :, :shape[-1]]
    else:
        view = view[:int(np.prod(shape[:-1]))]
    arr = view.reshape(shape)
    return arr if (not swap or keep_swapped) else jnp.swapaxes(arr, -1, -2)


def _full_shape(name, local_shape):
    ax = SHARD_AXIS[name]
    return local_shape[:ax] + (N_DEV * local_shape[ax],) + local_shape[ax + 1:]


def _win(ref, form, n, j):
    start = j * n
    if not isinstance(start, int):
        start = pl.multiple_of(start, LANE if form == "last" else math.gcd(n, 16))
    if form == "mid":
        return ref.at[:, pl.ds(start, n), :]
    return ref.at[:, pl.ds(start, n)]


def _windows(view, count, of):
    return view.shape[:1] + (view.shape[1] * count // of,) + view.shape[2:]


def _gather_work(views, forms):
    na = len(views)

    def plan(x_refs, out_refs, sems):
        send_sems, recv_sems, local_sems = sems
        x, y, c = lax.axis_index("x"), lax.axis_index("y"), lax.axis_index("c")
        me, sibling = (x, y, c), (x, y, 1 - c)
        chips = [(1 - x, y), (x, 1 - y), (1 - x, 1 - y)]

        def copy(a, k, block, to, from_input):
            dst = _win(out_refs[a], forms[a], views[a].shape[1], 4 * block[0] + 2 * block[1] + block[2])
            return pltpu.make_async_remote_copy(
                src_ref=x_refs[a] if from_input else dst, dst_ref=dst, send_sem=send_sems.at[a, k],
                recv_sem=recv_sems.at[a, k], device_id=to, device_id_type=MESH)

        mine = [pltpu.make_async_copy(x_refs[a], _win(out_refs[a], forms[a], views[a].shape[1], 4 * x + 2 * y + c),
                                      local_sems.at[a]) for a in range(na)]
        first = []
        for a in range(na):
            first.append(copy(a, 0, me, sibling, True))
            first += [copy(a, 1 + j, me, (*chip, c), True) for j, chip in enumerate(chips)]
        return copy, mine, first, me, sibling, chips, c

    def start(x_refs, out_refs, sems):
        _, mine, first, *_ = plan(x_refs, out_refs, sems)
        for cp in mine + first:
            cp.start()

    def finish(x_refs, out_refs, sems):
        copy, mine, first, me, sibling, chips, c = plan(x_refs, out_refs, sems)
        passed = []
        for j, chip in enumerate(chips):
            for a in range(na):
                copy(a, 1 + j, (*chip, c), me, False).wait_recv()
                fwd = copy(a, 4 + j, (*chip, c), sibling, False)
                fwd.start()
                passed.append(fwd)
        for a in range(na):
            copy(a, 0, sibling, me, False).wait_recv()
            for j, chip in enumerate(chips):
                copy(a, 4 + j, (*chip, 1 - c), me, False).wait_recv()
        for cp in first + passed:
            cp.wait_send()
        for cp in mine:
            cp.wait()

    return Hosted(
        list(views), [jax.ShapeDtypeStruct(_windows(v, N_DEV, 1), v.dtype) for v in views],
        [pltpu.SemaphoreType.DMA((na, 7)), pltpu.SemaphoreType.DMA((na, 7)), pltpu.SemaphoreType.DMA((na,))], start, finish)


def _push_work(srcs, out_shapes, n_copies, make_copies):
    na = len(srcs)

    def start(s_refs, r_refs, sems):
        for cp in make_copies(s_refs, r_refs, sems[0], sems[1]):
            cp.start()

    def finish(s_refs, r_refs, sems):
        copies = make_copies(s_refs, r_refs, sems[0], sems[1])
        for cp in copies:
            cp.wait_recv()
        for cp in copies:
            cp.wait_send()

    return Hosted(list(srcs), out_shapes, [pltpu.SemaphoreType.DMA((na, n_copies)), pltpu.SemaphoreType.DMA((na, n_copies))],
                  start, finish)


def _sibling_work(fulls, forms):
    na = len(fulls)
    widths = [f.shape[1] // N_DEV for f in fulls]

    def make_copies(g_refs, r_refs, send_sems, recv_sems):
        x, y, c = lax.axis_index("x"), lax.axis_index("y"), lax.axis_index("c")
        return [
            pltpu.make_async_remote_copy(
                src_ref=_win(g_refs[a], forms[a], widths[a], 2 * chip + (1 - c)),
                dst_ref=_win(r_refs[a], forms[a], widths[a], chip), send_sem=send_sems.at[a, chip],
                recv_sem=recv_sems.at[a, chip], device_id=(x, y, 1 - c), device_id_type=MESH)
            for a in range(na) for chip in range(N_CHIP)
        ]

    return _push_work(fulls, [jax.ShapeDtypeStruct(_windows(f, N_CHIP, N_DEV), f.dtype) for f in fulls], N_CHIP, make_copies)


def _chip_work(parts, forms):
    na = len(parts)
    widths = [p.shape[1] // N_CHIP for p in parts]

    def make_copies(p_refs, r_refs, send_sems, recv_sems):
        x, y, c = lax.axis_index("x"), lax.axis_index("y"), lax.axis_index("c")
        chips = [(1 - x, y), (x, 1 - y), (1 - x, 1 - y)]
        return [
            pltpu.make_async_remote_copy(
                src_ref=_win(p_refs[a], forms[a], widths[a], 2 * px + py), dst_ref=_win(r_refs[a], forms[a], widths[a], j),
                send_sem=send_sems.at[a, j], recv_sem=recv_sems.at[a, j], device_id=(px, py, c), device_id_type=MESH)
            for a in range(na) for j, (px, py) in enumerate(chips)
        ]

    return _push_work(parts, [jax.ShapeDtypeStruct(_windows(p, 3, N_CHIP), p.dtype) for p in parts], 3, make_copies)


def _sum_tiles(view, form, n):
    if form == "mid":
        tr = n
        while tr * view.shape[2] * 4 > 2 * 1024 * 1024 and tr % 32 == 0:
            tr //= 2
        return 1, tr
    return _pick(view.shape[0], (512, 256, 128, 64, 32, 16)), n


def _window_spec(form, tl, tr, rest, window_of):
    if form == "mid":
        return lambda per: pl.BlockSpec((None, tr) + rest, lambda l, k, i, s: (l, window_of(k, s) * per + i, 0))
    return lambda per: pl.BlockSpec((tl, tr), lambda l, k, i, s: (l, window_of(k, s)))


def _chip_partials(g, recv, core, form, name):
    n = g.shape[1] // N_DEV
    tl, tr = _sum_tiles(g, form, n)
    per = n // tr
    rest = tuple(g.shape[2:])

    def body(core_ref, g_ref, r_ref, o_ref):
        o_ref[...] = (g_ref[...] + r_ref[...]).astype(o_ref.dtype)

    own = _window_spec(form, tl, tr, rest, lambda k, s: 2 * k + s[0])(per)
    by_chip = _window_spec(form, tl, tr, rest, lambda k, s: k)(per)
    return pl.pallas_call(
        body, name=name,
        grid_spec=pltpu.PrefetchScalarGridSpec(
            num_scalar_prefetch=1, grid=(g.shape[0] // tl, N_CHIP, per), in_specs=[own, by_chip], out_specs=by_chip),
        out_shape=jax.ShapeDtypeStruct(recv.shape, BF16), compiler_params=_params("parallel", "parallel", "parallel"),
    )(core, g, recv)


def _reduce_final(p, recv, chip, form, name):
    n = p.shape[1] // N_CHIP
    tl, tr = _sum_tiles(p, form, n)
    per = n // tr
    rest = tuple(p.shape[2:])

    def body(chip_ref, p_ref, ry_ref, rx_ref, rxy_ref, o_ref):
        own_pair = p_ref[...].astype(F32) + ry_ref[...].astype(F32)
        o_ref[...] = own_pair + (rx_ref[...].astype(F32) + rxy_ref[...].astype(F32))

    def rel(j):
        return _window_spec(form, tl, tr, rest, lambda k, s: j)(per)

    own = _window_spec(form, tl, tr, rest, lambda k, s: s[0])(per)
    return pl.pallas_call(
        body, name=name,
        grid_spec=pltpu.PrefetchScalarGridSpec(
            num_scalar_prefetch=1, grid=(p.shape[0] // tl, 1, per), in_specs=[own, rel(1), rel(0), rel(2)],
            out_specs=rel(0)),
        out_shape=jax.ShapeDtypeStruct(p.shape[:1] + (n,) + p.shape[2:], F32),
        compiler_params=_params("parallel", "parallel", "parallel"),
    )(chip, p, recv, recv, recv)


def _pack_replicated(arrays):
    pieces = []
    for a in arrays:
        flat = a.reshape(-1).astype(F32)
        pieces.append(jnp.pad(flat, (0, -flat.size % LANE)))
    total = sum(p.size for p in pieces)
    pieces.append(jnp.zeros((-total % (16 * LANE),), F32))
    return jnp.concatenate(pieces).reshape(-1, LANE)


def _unpack_replicated(buf, shapes):
    flat, out, off = buf.reshape(-1), [], 0
    for shape in shapes:
        size = int(np.prod(shape))
        out.append(flat[off:off + size].reshape(shape))
        off += size + (-size % LANE)
    return out


def _silu(v):
    return v * jax.nn.sigmoid(v)


SC_NAMES = ("sc_w_in", "sc_conv", "sc_w_out")
MLA_SHARDED = ("mla_w_a", "mla_g_qa", "mla_w_uq", "mla_w_ukv", "mla_w_o")
MLA_NAMES = ("mla_w_a", "mla_g_qa", "mla_w_uq", "mla_g_kva", "mla_w_ukv", "mla_g_q", "mla_g_k", "mla_w_o")


def _local_step(src, x, c, ctx, target):
    bsz, n_lat, d = x.shape
    n_ctx = ctx.shape[1]
    assert bsz == 2
    geo = Geo(n_lat, n_ctx)
    depth = src.depth
    tc = _conv_tile(d)
    tabs = _rope_tables(geo)

    h = jnp.concatenate([x.reshape(2 * n_lat, d), ctx.reshape(2 * n_ctx, d)], axis=0)
    tgt = target.reshape(2 * n_lat, d)

    saved = []
    for i in range(depth):
        kind = i % 2
        wl, slots = src.weights(i), src.fwd_slots(i)
        gn = wl["g_norm"].astype(F32)
        mod4 = src.mod(i).reshape(N_SEG, N_MOD, 1, d)
        h, s1 = _ffn_fwd(h, gn[0:1], mod4, 0, wl, 0, geo, f"l{i}_f1", "f1", slots, slots)
        if kind == 0:
            mix = (_interleave(wl["sc_w_in"], 3, tc), wl["sc_conv"].astype(F32), wl["sc_w_out"])
            h, s2 = _sconv_fwd(h, gn[1:2], mod4, *mix, geo, f"l{i}_sc", slots, slots)
        else:
            mix = _mla_prepare(*[wl[name] for name in MLA_NAMES])
            h, s2 = _mla_fwd(h, gn[1:2], mod4, mix, i != depth - 1, tabs, geo, f"l{i}_mla", slots, slots)
        h, s3 = _ffn_fwd(h, gn[2:3], mod4, 6, wl, 1, geo, f"l{i}_f2", "f2", slots, slots)
        saved.append((wl, gn, mod4, mix, s1, s2, s3))

    loss, dh = _loss_head(h, tgt, geo, "loss_head")

    g_b_mod = [None] * depth
    for i in reversed(range(depth)):
        kind = i % 2
        wl, gn, mod4, mix, s1, s2, s3 = saved[i]
        slots = src.bwd_slots(i)
        gbuf = {name: lax.empty(wl[name].shape, F32) for name in ("ffn_w1", "ffn_w3", "ffn_w2")}
        dh, dg2, dm2 = _ffn_bwd(dh, s3, gn[2:3], mod4, 6, wl, 1, gbuf, geo, f"l{i}_f2", "f2", slots, slots)
        if kind == 0:
            dh, dg1, dm1, dwin, dconv, dwout = _sconv_bwd(dh, s2, gn[1:2], mod4, *mix, geo, f"l{i}_sc", slots, slots)
            gl = dict(sc_w_in=_deinterleave(dwin, 3, tc), sc_conv=dconv, sc_w_out=dwout)
        else:
            dh, dg1, dm1, gm = _mla_bwd(dh, s2, gn[1:2], mod4, mix, i != depth - 1, tabs, geo, f"l{i}_mla", slots, slots)
            gl = _mla_unprepare(gm)
        dh, dg0, dm0 = _ffn_bwd(dh, s1, gn[0:1], mod4, 0, wl, 0, gbuf, geo, f"l{i}_f1", "f1", slots, slots)
        dmod = jnp.concatenate(list(dm0) + list(dm1) + list(dm2), axis=1).reshape(N_SEG, N_MOD * d)
        dmod8 = jnp.concatenate([dmod, jnp.zeros((8 - N_SEG, N_MOD * d), F32)], axis=0)
        g_b_mod[i] = jnp.sum(dmod, axis=0)
        gl.update(gbuf, g_norm=jnp.concatenate([dg0, dg1, dg2], axis=0))
        src.dmod(i, dmod8)
        src.grads(i, gl)

    grad_x = dh[:2 * n_lat].reshape(x.shape)
    return loss, grad_x, jnp.stack(g_b_mod)


class _Slots:
    def __init__(self, get, put):
        self.get, self._put = get, put

    def __setitem__(self, slot, outs):
        self._put(slot, outs)


FWD_PLAN = {
    0: {"f1_up": ("ffn_w1",), "f1_down": ("g_norm", "mix"), "mix_a": ("ffn_w3",), "mix_b": ("ffn_w2",)},
    1: {"f1_up": ("ffn_w1",), "mix_a": ("ffn_w3", "g_norm", "mix"), "f2_up": ("ffn_w2",)},
}
SIBLING_PLAN = {"f2_dact": ("ffn_w1", "g_norm", "mix"), "f2_dw2": ("ffn_w3", "ffn_w2")}
BWD_PLAN = {
    0: {"f2_dnx": ("ffn_w1",), "mix_b": ("ffn_w3",), "mix_a": ("ffn_w2",), "f1_dact": ("g_norm", "mix")},
    1: {"f2_dnx": ("ffn_w1",), "mix_a": ("ffn_w3", "ffn_w2"), "f1_dnx": ("g_norm", "mix")},
}
DMOD_SLOT = {0: "mix_c", 1: "f1_dact"}
MOD_ROWS = 32


class _Exchange:
    def __init__(self, w):
        self.w = w
        self.depth = w["w_mod"].shape[0]
        self.c_ctx = w["c_ctx"]
        self.me = 4 * lax.axis_index("x") + 2 * lax.axis_index("y") + lax.axis_index("c")
        self.core = lax.axis_index("c").astype(jnp.int32).reshape(1)
        self.chip = (2 * lax.axis_index("x") + lax.axis_index("y")).astype(jnp.int32).reshape(1)
        self.full, self.gviews, self.parts, self.reduced, self.rep, self.dmods = {}, {}, {}, {}, {}, {}
        self.ctx_pre = jnp.zeros_like(self.c_ctx)

    def _layer_of(self, name, i):
        return i // 2 if name.startswith(("sc_", "mla_")) else i

    def _mixer(self, i):
        return SC_NAMES if i % 2 == 0 else MLA_SHARDED

    def _expand(self, names, i):
        out = []
        for name in names:
            out += list(self._mixer(i)) if name == "mix" else [name]
        return out

    def _group(self, i):
        return ["g_norm", "ffn_w1", "ffn_w3", "ffn_w2"] + list(self._mixer(i))

    def _local(self, name, i):
        arr = self.w[name][self._layer_of(name, i)]
        return arr[:, None] if name == "mla_g_qa" else arr

    def _shapes(self, name, i):
        local = tuple(self._local(name, i).shape)
        ax = SHARD_AXIS[name] - 1
        return local, local[:ax] + (N_DEV * local[ax],) + local[ax + 1:]

    def _gather(self, names, i):
        views = [_view(n, self._local(n, i).astype(BF16 if EXCHANGE[n][2] else F32)) for n in names]
        return _gather_work(views, [EXCHANGE[n][0] for n in names])

    def _gathered(self, names, i, outs):
        for name, fv in zip(names, outs):
            arr = _unview(name, fv, self._shapes(name, i)[1], keep_swapped=name in HIDDEN_MAJOR)
            self.full[name, i] = arr[:, 0] if name == "mla_g_qa" else arr

    def prefetch(self, c):
        bsz, d = c.shape
        (conds,) = _run_hosted(_gather_work([jnp.pad(c, ((0, 8 - bsz), (0, 0)))[None]], ["mid"]), "gather_cond")
        conds = conds.reshape(N_DEV, 8, d)[:, :bsz]
        act = _silu(jnp.concatenate([conds, jnp.broadcast_to(self.c_ctx, (N_DEV, 1, d))], axis=1))
        self.s_rows = jnp.pad(act.reshape(N_DEV * N_SEG, d), ((0, MOD_ROWS - N_DEV * N_SEG), (0, 0)))
        cols = jnp.stack([_mm(self.s_rows, self.w["w_mod"][l], name=f"mod_cols_{l}") for l in range(self.depth)])
        names = self._group(0)
        work = self._gather(names, 0)
        both = _gather_work(work.inputs + [cols.reshape(self.depth * MOD_ROWS, -1)], self._forms(names) + ["last"])
        outs = _run_hosted(both, "gather_l0")
        self._gathered(names, 0, outs[:-1])
        mods = lax.dynamic_slice_in_dim(outs[-1].reshape(self.depth, MOD_ROWS, -1), N_SEG * self.me, N_SEG, axis=1)
        self.mods = mods + self.w["b_mod"][:, None, :]

    def mod(self, i):
        return self.mods[i]

    def weights(self, i):
        wl = {name: self.full[name, i] for name in self._group(i)}
        if i % 2 == 1:
            for name in ("mla_g_kva", "mla_g_q", "mla_g_k"):
                wl[name] = self.w[name][i // 2]
        return wl

    def fwd_slots(self, i):
        plan = FWD_PLAN[i % 2] if i + 1 < self.depth else {}
        names = {slot: self._expand(plan[slot], i + 1) for slot in plan}
        return _Slots(lambda slot: self._gather(names[slot], i + 1) if slot in names else None,
                      lambda slot, outs: self._gathered(names[slot], i + 1, outs))

    def grads(self, i, gl):
        for name in self._group(i):
            g = gl[name][:, None] if name == "mla_g_qa" else gl[name]
            self.gviews[name, i] = _view(name, g, swapped=name in HIDDEN_MAJOR)
        for name in REPLICATED:
            if name in gl:
                self.rep[name, i // 2] = gl[name]

    def dmod(self, i, dmod8):
        self.dmods[i] = dmod8

    def _dmod_gather(self, i):
        return _gather_work([self.dmods[i][None]], ["mid"])

    def _dmod_gathered(self, i, outs):
        n = self.w["w_mod"].shape[2]
        rows = outs[0].reshape(N_DEV, 8, -1)[:, :N_SEG]
        mine = lax.dynamic_slice_in_dim(rows, n * self.me, n, axis=2)
        flat = jnp.pad(mine.reshape(N_DEV * N_SEG, n), ((0, MOD_ROWS - N_DEV * N_SEG), (0, 0)))
        self.reduced["w_mod", i] = _mm(self.s_rows, flat, ta=True, name=f"dwmod_{i}")
        ctx_rows = jnp.pad(jnp.sum(mine[:, N_SEG - 1], axis=0, keepdims=True), ((0, 7), (0, 0)))
        self.ctx_pre = self.ctx_pre + _mm(ctx_rows, self.w["w_mod"][i], tb=True, name=f"dcond_{i}")[0]

    def _forms(self, names):
        return [EXCHANGE[n][0] for n in names]

    def _partials(self, names, i, from_sibling):
        for name, recv in zip(names, from_sibling):
            self.parts[name, i] = _chip_partials(self.gviews[name, i], recv, self.core, EXCHANGE[name][0],
                                                 f"partial_{name}_{i}")

    def _finals(self, names, i, from_chips):
        for name, recv in zip(names, from_chips):
            rv = _reduce_final(self.parts[name, i], recv, self.chip, EXCHANGE[name][0], f"final_{name}_{i}")
            arr = _unview(name, rv, self._shapes(name, i)[0])
            self.reduced[name, i] = arr[:, 0] if name == "mla_g_qa" else arr

    def bwd_slots(self, i):
        if i + 1 >= self.depth:
            return _Slots(lambda slot: None, None)
        plan = BWD_PLAN[i % 2]
        names = {slot: self._expand(plan[slot], i + 1) for slot in plan}
        sibling = {slot: self._expand(SIBLING_PLAN[slot], i + 1) for slot in SIBLING_PLAN}

        def get(slot):
            if slot in sibling:
                return _sibling_work([self.gviews[n, i + 1] for n in sibling[slot]], self._forms(sibling[slot]))
            if slot in names:
                return _chip_work([self.parts[n, i + 1] for n in names[slot]], self._forms(names[slot]))
            if slot == DMOD_SLOT[i % 2]:
                return self._dmod_gather(i + 1)
            return None

        def put(slot, outs):
            if slot in sibling:
                self._partials(sibling[slot], i + 1, outs)
            elif slot in names:
                self._finals(names[slot], i + 1, outs)
            else:
                self._dmod_gathered(i + 1, outs)

        return _Slots(get, put)

    def finish(self, rep_grads):
        group = self._group(0)
        self._dmod_gathered(0, _run_hosted(self._dmod_gather(0), "gather_dmod_l0"))
        rep_grads["c_ctx"] = self.ctx_pre
        for name in REPLICATED:
            if name not in rep_grads:
                rep_grads[name] = jnp.stack([self.rep[name, j] for j in range(self.w[name].shape[0])])
        rep = _pack_replicated([rep_grads[name] for name in REPLICATED])
        views = [self.gviews[n, 0] for n in group] + [jnp.tile(rep[None], (1, N_DEV, 1))]
        forms = self._forms(group) + ["mid"]
        from_sibling = _run_hosted(_sibling_work(views, forms), "reduce_sibling_l0")
        self._partials(group, 0, from_sibling[:-1])
        rep_part = _chip_partials(views[-1], from_sibling[-1], self.core, "mid", "partial_replicated")
        parts = [self.parts[n, 0] for n in group] + [rep_part]
        from_chips = _run_hosted(_chip_work(parts, forms), "reduce_chips_l0")
        self._finals(group, 0, from_chips[:-1])
        rep_sum = _reduce_final(rep_part, from_chips[-1], self.chip, "mid", "final_replicated")
        out = dict(zip(REPLICATED, _unpack_replicated(rep_sum, [self.w[name].shape for name in REPLICATED])))
        sg = jax.nn.sigmoid(self.c_ctx)
        out["c_ctx"] = out["c_ctx"] * (sg * (1.0 + self.c_ctx * (1.0 - sg)))
        for name in EXCHANGE:
            layers = range(self.w[name].shape[0])
            step = 2 if name.startswith(("sc_", "mla_")) else 1
            first = 1 if name.startswith("mla_") else 0
            out[name] = jnp.stack([self.reduced[name, first + step * l] for l in layers])
        return out


def kernel(x, c, ctx, c_ctx, w_mod, b_mod, g_norm, ffn_w1, ffn_w3, ffn_w2, sc_w_in, sc_conv, sc_w_out, mla_w_a, mla_g_qa, mla_w_uq, mla_g_kva, mla_w_ukv, mla_g_q, mla_g_k, mla_w_o, loss_target, m_c_ctx, m_w_mod, m_b_mod, m_g_norm, m_ffn_w1, m_ffn_w3, m_ffn_w2, m_sc_w_in, m_sc_conv, m_sc_w_out, m_mla_w_a, m_mla_g_qa, m_mla_w_uq, m_mla_g_kva, m_mla_w_ukv, m_mla_g_q, m_mla_g_k, m_mla_w_o, v_c_ctx, v_w_mod, v_b_mod, v_g_norm, v_ffn_w1, v_ffn_w3, v_ffn_w2, v_sc_w_in, v_sc_conv, v_sc_w_out, v_mla_w_a, v_mla_g_qa, v_mla_w_uq, v_mla_g_kva, v_mla_w_ukv, v_mla_g_q, v_mla_g_k, v_mla_w_o):
    w = dict(c_ctx=c_ctx, w_mod=w_mod, b_mod=b_mod, g_norm=g_norm, ffn_w1=ffn_w1, ffn_w3=ffn_w3, ffn_w2=ffn_w2,
             sc_w_in=sc_w_in, sc_conv=sc_conv, sc_w_out=sc_w_out, mla_w_a=mla_w_a, mla_g_qa=mla_g_qa, mla_w_uq=mla_w_uq,
             mla_g_kva=mla_g_kva, mla_w_ukv=mla_w_ukv, mla_g_q=mla_g_q, mla_g_k=mla_g_k, mla_w_o=mla_w_o)
    m = dict(c_ctx=m_c_ctx, w_mod=m_w_mod, b_mod=m_b_mod, g_norm=m_g_norm, ffn_w1=m_ffn_w1, ffn_w3=m_ffn_w3,
             ffn_w2=m_ffn_w2, sc_w_in=m_sc_w_in, sc_conv=m_sc_conv, sc_w_out=m_sc_w_out, mla_w_a=m_mla_w_a,
             mla_g_qa=m_mla_g_qa, mla_w_uq=m_mla_w_uq, mla_g_kva=m_mla_g_kva, mla_w_ukv=m_mla_w_ukv, mla_g_q=m_mla_g_q,
             mla_g_k=m_mla_g_k, mla_w_o=m_mla_w_o)
    v = dict(c_ctx=v_c_ctx, w_mod=v_w_mod, b_mod=v_b_mod, g_norm=v_g_norm, ffn_w1=v_ffn_w1, ffn_w3=v_ffn_w3,
             ffn_w2=v_ffn_w2, sc_w_in=v_sc_w_in, sc_conv=v_sc_conv, sc_w_out=v_sc_w_out, mla_w_a=v_mla_w_a,
             mla_g_qa=v_mla_g_qa, mla_w_uq=v_mla_w_uq, mla_g_kva=v_mla_g_kva, mla_w_ukv=v_mla_w_ukv, mla_g_q=v_mla_g_q,
             mla_g_k=v_mla_g_k, mla_w_o=v_mla_w_o)
    exchange = _Exchange(w)
    exchange.prefetch(c)
    loss, grad_x, g_b_mod = _local_step(exchange, x, c, ctx, loss_target)
    loss = lax.psum(loss, ("x", "y", "c"))
    reduced = exchange.finish(dict(b_mod=g_b_mod))

    outs = [[], [], [], []]
    for name in WEIGHTS:
        delta, new_m, new_v = _adamw(w[name], reduced[name], m[name], v[name], f"adamw_{name}")
        for lst, val in zip(outs, (reduced[name], delta, new_m, new_v)):
            lst.append(val)
    return (loss, grad_x, *outs[0], *outs[1], *outs[2], *outs[3])
```

```python
import functools
import math

import jax
import jax.numpy as jnp
import numpy as np
from jax import lax
from jax.experimental import pallas as pl
from jax.experimental.pallas import tpu as pltpu

F32 = jnp.float32
BF16 = jnp.bfloat16

N_MOD = 9
HEADS = 8
QK_NOPE = 128
QK_ROPE = 64
QK_HEAD = QK_NOPE + QK_ROPE
V_HEAD = 128
GRID_W = 64
ROPE_BASE = 10000.0
QK_SCALE = QK_HEAD ** -0.5
EPS = 1e-6
ADAM_LR, ADAM_B1, ADAM_B2, ADAM_EPS, ADAM_WD, ADAM_STEP = 0.001, 0.9, 0.999, 1e-08, 0.01, 10

N_DEV = 8
N_CHIP = 4
N_SEG = 3
LANE = 128
HEAD_PAD = 2 * LANE
VMEM_LIMIT_BYTES = 48 * 1024 * 1024
MESH = pl.DeviceIdType.MESH

WEIGHTS = ["c_ctx", "w_mod", "b_mod", "g_norm", "ffn_w1", "ffn_w3", "ffn_w2", "sc_w_in", "sc_conv", "sc_w_out",
           "mla_w_a", "mla_g_qa", "mla_w_uq", "mla_g_kva", "mla_w_ukv", "mla_g_q", "mla_g_k", "mla_w_o"]
EXCHANGE = {
    "w_mod": ("last", False, True), "ffn_w1": ("mid", True, True), "ffn_w3": ("mid", True, True),
    "ffn_w2": ("mid", False, True), "sc_w_in": ("last", False, True), "sc_w_out": ("mid", False, True),
    "mla_w_a": ("mid", False, True), "mla_w_uq": ("mid", True, True), "mla_w_ukv": ("last", False, True),
    "mla_w_o": ("mid", False, True), "g_norm": ("last", False, False), "sc_conv": ("last", False, False),
    "mla_g_qa": ("mid", False, False),
}
REPLICATED = ["c_ctx", "b_mod", "mla_g_kva", "mla_g_q", "mla_g_k"]


def _pick(n, cands):
    for cand in cands:
        if n % cand == 0:
            return cand
    return n


def _params(*sem):
    return pltpu.CompilerParams(dimension_semantics=sem, vmem_limit_bytes=VMEM_LIMIT_BYTES)


def _hbm():
    return pl.BlockSpec(memory_space=pl.ANY)


class Hosted:
    def __init__(self, inputs, out_shapes, scratch, start, finish):
        self.inputs, self.out_shapes, self.scratch, self.start, self.finish = inputs, out_shapes, scratch, start, finish


def _call(body, hosted, **kw):
    if hosted is None:
        return pl.pallas_call(body, **kw)
    single = not isinstance(kw["out_shape"], (tuple, list))
    out_shape = [kw["out_shape"]] if single else list(kw["out_shape"])
    out_specs = [kw["out_specs"]] if single else list(kw["out_specs"])
    in_specs, scratch, grid = list(kw["in_specs"]), list(kw.get("scratch_shapes", ())), kw["grid"]
    n_in, n_out, n_scr = len(in_specs), len(out_shape), len(scratch)
    h_in, h_out = len(hosted.inputs), len(hosted.out_shapes)

    def wrapped(*refs):
        ins, hins = refs[:n_in], refs[n_in:n_in + h_in]
        o0 = n_in + h_in
        outs, houts = refs[o0:o0 + n_out], refs[o0 + n_out:o0 + n_out + h_out]
        s0 = o0 + n_out + h_out
        scr, hscr = refs[s0:s0 + n_scr], refs[s0 + n_scr:]
        first = functools.reduce(jnp.logical_and, [pl.program_id(a) == 0 for a in range(len(grid))])
        last = functools.reduce(jnp.logical_and, [pl.program_id(a) == g - 1 for a, g in enumerate(grid)])

        @pl.when(first)
        def _():
            hosted.start(hins, houts, hscr)

        body(*ins, *outs, *scr)

        @pl.when(last)
        def _():
            hosted.finish(hins, houts, hscr)

    call = pl.pallas_call(
        wrapped, name=kw["name"], grid=grid, in_specs=in_specs + [_hbm()] * h_in,
        out_specs=tuple(out_specs + [_hbm()] * h_out), out_shape=tuple(out_shape + list(hosted.out_shapes)),
        scratch_shapes=scratch + list(hosted.scratch), input_output_aliases=kw.get("input_output_aliases", {}),
        compiler_params=_params(*["arbitrary"] * len(grid)))

    def run(*args):
        res = call(*args, *hosted.inputs)
        comp = res[:n_out]
        return (comp[0] if single else tuple(comp)), list(res[n_out:])

    return run


def _run_hosted(hosted, name):
    def body(*refs):
        h_in, h_out = len(hosted.inputs), len(hosted.out_shapes)
        hins, houts, hscr = refs[:h_in], refs[h_in:h_in + h_out], refs[h_in + h_out:]
        hosted.start(hins, houts, hscr)
        hosted.finish(hins, houts, hscr)

    return list(pl.pallas_call(
        body, name=name, in_specs=[_hbm()] * len(hosted.inputs), out_specs=tuple([_hbm()] * len(hosted.out_shapes)),
        out_shape=tuple(hosted.out_shapes), scratch_shapes=list(hosted.scratch))(*hosted.inputs))


class Geo:
    def __init__(self, n_lat, n_ctx):
        self.n_lat, self.n_ctx = n_lat, n_ctx
        self.rows = 2 * n_lat + 2 * n_ctx
        self.tile = n_ctx
        assert n_lat % n_ctx == 0 and n_ctx % 16 == 0
        self.mm_tile = _pick(n_lat, (512, 256, 128)) if self.rows % _pick(n_lat, (512, 256, 128)) == 0 else n_ctx
        self.big_tile = _pick(self.rows, (1536, 768, 512, 256))

    def seg(self, i, tile):
        return jnp.minimum((i * tile) // self.n_lat, N_SEG - 1)

    def seg_start(self, i, tile):
        row = i * tile
        return jnp.logical_or(row % self.n_lat == 0, row == 2 * self.n_lat) & (row <= 2 * self.n_lat)


_NT = (((1,), (1,)), ((), ()))
_NN = (((1,), (0,)), ((), ()))
_TN = (((0,), (0,)), ((), ()))


def _dot(a, b, dims):
    return lax.dot_general(a.astype(BF16), b.astype(BF16), dims, preferred_element_type=F32)


def _mm(a, b, *, ta=False, tb=False, out_dtype=F32, name, gate=None, hosted=None):
    (kdim, m) = a.shape if ta else a.shape[::-1]
    n = b.shape[0] if tb else b.shape[1]
    assert (b.shape[1] if tb else b.shape[0]) == kdim
    if gate is not None:
        tm = gate[4].mm_tile
    else:
        tm = _pick(m, (512, 256, 128))
    tn = _pick(n, (512, 256, 128))
    tk = _pick(kdim, (1024, 512, 256, 128))
    nk = kdim // tk
    dims = (((0 if ta else 1,), (1 if tb else 0,)), ((), ()))

    def body(*refs):
        if gate is not None:
            a_ref, b_ref, res_ref, gate_ref, o_ref, y_ref, acc_ref = refs
        else:
            a_ref, b_ref, o_ref, acc_ref = refs
        kk = pl.program_id(2)

        @pl.when(kk == 0)
        def _():
            acc_ref[...] = jnp.zeros_like(acc_ref)

        acc_ref[...] += lax.dot_general(a_ref[...].astype(BF16), b_ref[...].astype(BF16), dims,
                                        preferred_element_type=F32)

        @pl.when(kk == nk - 1)
        def _():
            acc = acc_ref[...]
            if gate is not None:
                y_ref[...] = acc.astype(y_ref.dtype)
                o_ref[...] = res_ref[...] + (gate[3] * gate_ref[...]) * acc
            else:
                o_ref[...] = acc.astype(o_ref.dtype)

    a_spec = pl.BlockSpec((tk, tm), lambda i, j, k: (k, i)) if ta else pl.BlockSpec((tm, tk), lambda i, j, k: (i, k))
    b_spec = pl.BlockSpec((tn, tk), lambda i, j, k: (j, k)) if tb else pl.BlockSpec((tk, tn), lambda i, j, k: (k, j))
    o_spec = pl.BlockSpec((tm, tn), lambda i, j, k: (i, j))
    in_specs, args = [a_spec, b_spec], [a, b]
    out_shape, out_specs = jax.ShapeDtypeStruct((m, n), out_dtype), o_spec
    if gate is not None:
        res, mod4, kmod, _, geo = gate
        in_specs += [o_spec, pl.BlockSpec((None, None, 1, tn), lambda i, j, k: (geo.seg(i, tm), kmod, 0, j))]
        args += [res, mod4]
        out_shape = (jax.ShapeDtypeStruct((m, n), F32), jax.ShapeDtypeStruct((m, n), BF16))
        out_specs = (o_spec, o_spec)
    return _call(
        body, hosted, name=name, grid=(m // tm, n // tn, nk), in_specs=in_specs, out_specs=out_specs,
        out_shape=out_shape, scratch_shapes=[pltpu.VMEM((tm, tn), F32)],
        compiler_params=_params("parallel", "parallel", "arbitrary"),
    )(*args)


def _tn_wide(lhs, rhs, name, into=None, s0=0, hosted=None):
    t, m = lhs.shape
    n = rhs.shape[1]
    tm = _pick(m, (1408, 1024, 512, 256, 128))
    while tm * n * 4 > 6.5 * 1024 * 1024 and tm % 256 == 0:
        tm //= 2
    tk = next(c for c in (1536, 768, 512, 256, 128, t)
              if t % c == 0 and c * (tm + n) * 4 + tm * n * 12 <= 36 * 1024 * 1024)

    def body(l_ref, r_ref, *rest):
        o_ref = rest[-1]
        kk = pl.program_id(1)
        part = lax.dot_general(l_ref[...], r_ref[...], _TN, preferred_element_type=F32)

        @pl.when(kk == 0)
        def _():
            o_ref[...] = part

        @pl.when(kk > 0)
        def _():
            o_ref[...] += part

    in_specs = [pl.BlockSpec((tk, tm), lambda i, k: (k, i)), pl.BlockSpec((tk, n), lambda i, k: (k, 0))]
    if into is None:
        return _call(
            body, hosted, name=name, grid=(m // tm, t // tk), in_specs=in_specs,
            out_specs=pl.BlockSpec((tm, n), lambda i, k: (i, 0)), out_shape=jax.ShapeDtypeStruct((m, n), F32),
            compiler_params=_params("parallel", "arbitrary"),
        )(lhs, rhs)
    return _call(
        body, hosted, name=name, grid=(m // tm, t // tk), in_specs=in_specs + [pl.BlockSpec(memory_space=pl.ANY)],
        out_specs=pl.BlockSpec((None, tm, n), lambda i, k: (s0, i, 0)),
        out_shape=jax.ShapeDtypeStruct(into.shape, into.dtype), input_output_aliases={2: 0},
        compiler_params=_params("parallel", "arbitrary"),
    )(lhs, rhs, into)


def _mod_spec(geo, tile, kmod, d):
    return pl.BlockSpec((None, None, 1, d), lambda i: (geo.seg(i, tile), kmod, 0, 0))


def _pre_fwd(h, g, mod4, k_shift, geo, name):
    t, d = h.shape
    tile = geo.tile

    def body(h_ref, g_ref, sh_ref, sc_ref, o_ref):
        hv = h_ref[...]
        r = lax.rsqrt(jnp.mean(hv * hv, axis=-1, keepdims=True) + EPS)
        y = hv * r * g_ref[...]
        o_ref[...] = (y * (1.0 + sc_ref[...]) + sh_ref[...]).astype(o_ref.dtype)

    row = pl.BlockSpec((tile, d), lambda i: (i, 0))
    return pl.pallas_call(
        body, name=name, grid=(t // tile,),
        in_specs=[row, pl.BlockSpec((1, d), lambda i: (0, 0)), _mod_spec(geo, tile, k_shift, d),
                  _mod_spec(geo, tile, k_shift + 1, d)],
        out_specs=row, out_shape=jax.ShapeDtypeStruct((t, d), BF16), compiler_params=_params("parallel"),
    )(h, g, mod4, mod4)


def _pre_bwd(h, g, mod4, k_shift, dnx, dres, geo, name):
    t, d = h.shape
    tile = geo.tile

    def body(h_ref, g_ref, sc_ref, dnx_ref, dres_ref, dh_ref, dg_ref, dsh_ref, dsc_ref):
        i = pl.program_id(0)
        hv, gv, dout = h_ref[...], g_ref[...], dnx_ref[...].astype(F32)
        r = lax.rsqrt(jnp.mean(hv * hv, axis=-1, keepdims=True) + EPS)
        xhat = hv * r
        dy = dout * (1.0 + sc_ref[...])
        u = dy * gv
        dh_ref[...] = r * (u - xhat * jnp.mean(u * xhat, axis=-1, keepdims=True)) + dres_ref[...]

        @pl.when(i == 0)
        def _():
            dg_ref[...] = jnp.zeros_like(dg_ref)

        @pl.when(geo.seg_start(i, tile))
        def _():
            dsh_ref[...] = jnp.zeros_like(dsh_ref)
            dsc_ref[...] = jnp.zeros_like(dsc_ref)

        dg_ref[...] += jnp.sum(dy * xhat, axis=0, keepdims=True)
        dsh_ref[...] += jnp.sum(dout, axis=0, keepdims=True)
        dsc_ref[...] += jnp.sum(dout * (xhat * gv), axis=0, keepdims=True)

    row = pl.BlockSpec((tile, d), lambda i: (i, 0))
    vec = pl.BlockSpec((1, d), lambda i: (0, 0))
    segv = pl.BlockSpec((None, 1, d), lambda i: (geo.seg(i, tile), 0, 0))
    return pl.pallas_call(
        body, name=name, grid=(t // tile,),
        in_specs=[row, vec, _mod_spec(geo, tile, k_shift + 1, d), row, row],
        out_specs=(row, vec, segv, segv),
        out_shape=(jax.ShapeDtypeStruct((t, d), F32), jax.ShapeDtypeStruct((1, d), F32),
                   jax.ShapeDtypeStruct((N_SEG, 1, d), F32), jax.ShapeDtypeStruct((N_SEG, 1, d), F32)),
        compiler_params=_params("arbitrary"),
    )(h, g, mod4, dnx, dres)


def _gate_bwd(dh, y, mod4, k_gate, coef, geo, name):
    t, d = dh.shape
    tile = geo.tile

    def body(dh_ref, y_ref, gt_ref, dy_ref, dgt_ref):
        i = pl.program_id(0)
        dhv = dh_ref[...]
        dy_ref[...] = ((coef * gt_ref[...]) * dhv).astype(dy_ref.dtype)

        @pl.when(geo.seg_start(i, tile))
        def _():
            dgt_ref[...] = jnp.zeros_like(dgt_ref)

        dgt_ref[...] += coef * jnp.sum(dhv * y_ref[...].astype(F32), axis=0, keepdims=True)

    row = pl.BlockSpec((tile, d), lambda i: (i, 0))
    segv = pl.BlockSpec((None, 1, d), lambda i: (geo.seg(i, tile), 0, 0))
    return pl.pallas_call(
        body, name=name, grid=(t // tile,), in_specs=[row, row, _mod_spec(geo, tile, k_gate, d)],
        out_specs=(row, segv),
        out_shape=(jax.ShapeDtypeStruct((t, d), BF16), jax.ShapeDtypeStruct((N_SEG, 1, d), F32)),
        compiler_params=_params("arbitrary"),
    )(dh, y, mod4)


def _ff_tile(f):
    return _pick(f, (256, 128))


def _ffn_up(nx, w1t, w3t, s0, geo, name, hosted=None):
    t, d = nx.shape
    f = w1t.shape[1]
    tm, tn = geo.big_tile, _ff_tile(f)

    def body(x_ref, w1_ref, w3_ref, ga_ref, gb_ref, act_ref):
        xv = x_ref[...]
        a = lax.dot_general(xv, w1_ref[...], _NT, preferred_element_type=F32)
        bv = lax.dot_general(xv, w3_ref[...], _NT, preferred_element_type=F32)
        sg = jax.nn.sigmoid(a)
        silu = a * sg
        ga_ref[...] = (bv * (sg + silu * (1.0 - sg))).astype(ga_ref.dtype)
        gb_ref[...] = silu.astype(gb_ref.dtype)
        act_ref[...] = (silu * bv).astype(act_ref.dtype)

    w_spec = pl.BlockSpec((None, tn, d), lambda i, j: (s0, j, 0))
    o_spec = pl.BlockSpec((tm, tn), lambda i, j: (i, j))
    sds = jax.ShapeDtypeStruct((t, f), BF16)
    return _call(
        body, hosted, name=name, grid=(t // tm, f // tn),
        in_specs=[pl.BlockSpec((tm, d), lambda i, j: (i, 0)), w_spec, w_spec],
        out_specs=(o_spec,) * 3, out_shape=(sds,) * 3, compiler_params=_params("parallel", "parallel"),
    )(nx, w1t, w3t)


def _ffn_down(act, w2, s0, res, mod4, k_gate, geo, name, hosted=None):
    t, f = act.shape
    d = w2.shape[2]
    tm, tn = geo.mm_tile, _pick(d, (1024, 512, 256, 128))

    def body(a_ref, w_ref, res_ref, gate_ref, o_ref, y_ref):
        acc = lax.dot_general(a_ref[...], w_ref[...], _NN, preferred_element_type=F32)
        y_ref[...] = acc.astype(y_ref.dtype)
        o_ref[...] = res_ref[...] + (0.5 * gate_ref[...]) * acc

    o_spec = pl.BlockSpec((tm, tn), lambda i, j: (i, j))
    return _call(
        body, hosted, name=name, grid=(t // tm, d // tn),
        in_specs=[pl.BlockSpec((tm, f), lambda i, j: (i, 0)), pl.BlockSpec((None, f, tn), lambda i, j: (s0, 0, j)),
                  o_spec, pl.BlockSpec((None, None, 1, tn), lambda i, j: (geo.seg(i, tm), k_gate, 0, j))],
        out_specs=(o_spec, o_spec),
        out_shape=(jax.ShapeDtypeStruct((t, d), F32), jax.ShapeDtypeStruct((t, d), BF16)),
        compiler_params=_params("parallel", "parallel"),
    )(act, w2, res, mod4)


def _ffn_dact(dy, w2, ga, gb, s0, geo, name, hosted=None):
    t, d = dy.shape
    f = w2.shape[1]
    tm, tn = geo.big_tile, _ff_tile(f)

    def body(dy_ref, w_ref, ga_ref, gb_ref, da_ref, db_ref):
        dact = lax.dot_general(dy_ref[...], w_ref[...], _NT, preferred_element_type=F32)
        da_ref[...] = (dact * ga_ref[...].astype(F32)).astype(da_ref.dtype)
        db_ref[...] = (dact * gb_ref[...].astype(F32)).astype(db_ref.dtype)

    o_spec = pl.BlockSpec((tm, tn), lambda i, j: (i, j))
    sds = jax.ShapeDtypeStruct((t, f), BF16)
    return _call(
        body, hosted, name=name, grid=(t // tm, f // tn),
        in_specs=[pl.BlockSpec((tm, d), lambda i, j: (i, 0)), pl.BlockSpec((None, tn, d), lambda i, j: (s0, j, 0)),
                  o_spec, o_spec],
        out_specs=(o_spec, o_spec), out_shape=(sds, sds), compiler_params=_params("parallel", "parallel"),
    )(dy, w2, ga, gb)


def _ffn_dnx(da, db, w1t, w3t, s0, geo, name, hosted=None):
    t, f = da.shape
    d = w1t.shape[2]
    tm, tn = geo.mm_tile, _pick(d, (512, 256, 128))

    def body(da_ref, db_ref, w1_ref, w3_ref, o_ref):
        o_ref[...] = (lax.dot_general(da_ref[...], w1_ref[...], _NN, preferred_element_type=F32)
                      + lax.dot_general(db_ref[...], w3_ref[...], _NN, preferred_element_type=F32))

    x_spec = pl.BlockSpec((tm, f), lambda j, i: (i, 0))
    w_spec = pl.BlockSpec((None, f, tn), lambda j, i: (s0, 0, j))
    return _call(
        body, hosted, name=name, grid=(d // tn, t // tm), in_specs=[x_spec, x_spec, w_spec, w_spec],
        out_specs=pl.BlockSpec((tm, tn), lambda j, i: (i, j)), out_shape=jax.ShapeDtypeStruct((t, d), F32),
        compiler_params=_params("parallel", "parallel"),
    )(da, db, w1t, w3t)


def _with_host(fn, hosts, got, slot, *args, **kw):
    hosted = hosts.get(slot)
    if hosted is None:
        return fn(*args, **kw)
    out, got[slot] = fn(*args, hosted=hosted, **kw)
    return out


def _ffn_fwd(h, g, mod4, k0, w, s0, geo, tag, sub, hosts, got):
    nx = _pre_fwd(h, g, mod4, k0, geo, f"{tag}_pre")
    a, b, act = _with_host(_ffn_up, hosts, got, f"{sub}_up", nx, w["ffn_w1"], w["ffn_w3"], s0, geo, f"{tag}_up")
    h_out, y = _with_host(_ffn_down, hosts, got, f"{sub}_down", act, w["ffn_w2"], s0, h, mod4, k0 + 2, geo, f"{tag}_down")
    return h_out, (h, nx, a, b, act, y)


def _ffn_bwd(dh_out, saved, g, mod4, k0, w, s0, gbuf, geo, tag, sub, hosts, got):
    h, nx, a, b, act, y = saved
    dy, dgate = _gate_bwd(dh_out, y, mod4, k0 + 2, 0.5, geo, f"{tag}_dgate")
    da, db = _with_host(_ffn_dact, hosts, got, f"{sub}_dact", dy, w["ffn_w2"], a, b, s0, geo, f"{tag}_dact")
    gbuf["ffn_w2"] = _with_host(_tn_wide, hosts, got, f"{sub}_dw2", act, dy, f"{tag}_dw2", into=gbuf["ffn_w2"], s0=s0)
    dnx = _with_host(_ffn_dnx, hosts, got, f"{sub}_dnx", da, db, w["ffn_w1"], w["ffn_w3"], s0, geo, f"{tag}_dnx")
    gbuf["ffn_w1"] = _tn_wide(da, nx, f"{tag}_dw1", into=gbuf["ffn_w1"], s0=s0)
    gbuf["ffn_w3"] = _tn_wide(db, nx, f"{tag}_dw3", into=gbuf["ffn_w3"], s0=s0)
    dh, dg, dshift, dscale = _pre_bwd(h, g, mod4, k0, dnx, dh_out, geo, f"{tag}_dpre")
    return dh, dg, (dshift, dscale, dgate)


def _interleave(w, n_parts, tile):
    lead, cols = w.shape[:-1], w.shape[-1] // n_parts
    return w.reshape(*lead, n_parts, cols // tile, tile).swapaxes(-3, -2).reshape(*lead, n_parts * cols)


def _deinterleave(w, n_parts, tile):
    lead, cols = w.shape[:-1], w.shape[-1] // n_parts
    return w.reshape(*lead, cols // tile, n_parts, tile).swapaxes(-3, -2).reshape(*lead, n_parts * cols)


HALO = 16


def _conv_tile(c):
    return _pick(c, (256, 128))


def _conv_specs(geo, tc, t):
    tile = geo.tile
    per = tile // HALO
    last = t // HALO - 1
    cur = pl.BlockSpec((tile, 3 * tc), lambda j, i: (i, j))
    prev = pl.BlockSpec((HALO, 3 * tc), lambda j, i: (jnp.maximum(i * per - 1, 0), j))
    nxt = pl.BlockSpec((HALO, 3 * tc), lambda j, i: (jnp.minimum((i + 1) * per, last), j))
    return cur, prev, nxt


def _conv_edges(geo, i):
    tile = geo.tile
    row = i * tile
    lat = row < 2 * geo.n_lat
    first = jnp.where(lat, row % geo.n_lat == 0, (row - 2 * geo.n_lat) % geo.n_ctx == 0)
    end = row + tile
    last = jnp.where(lat, end % geo.n_lat == 0, (end - 2 * geo.n_lat) % geo.n_ctx == 0)
    return first, last


def _shift_rows(v, before, after):
    n = v.shape[0]
    rows = lax.broadcasted_iota(jnp.int32, v.shape, 0)
    down = jnp.where(rows == 0, before, pltpu.roll(v, 1, 0))
    up = jnp.where(rows == n - 1, after, pltpu.roll(v, n - 1, 0))
    return down, up


def _conv_fwd(proj, conv_w, geo, name, hosted=None):
    t, c3 = proj.shape
    c = c3 // 3
    tc, tile = _conv_tile(c), geo.tile

    def body(cur_ref, prev_ref, next_ref, w_ref, o_ref):
        first, last = _conv_edges(geo, pl.program_id(1))
        bv = cur_ref[:, :tc].astype(F32)
        p = cur_ref[:, tc:2 * tc].astype(F32) * cur_ref[:, 2 * tc:].astype(F32)
        p_before = prev_ref[HALO - 1:HALO, tc:2 * tc].astype(F32) * prev_ref[HALO - 1:HALO, 2 * tc:].astype(F32)
        p_after = next_ref[0:1, tc:2 * tc].astype(F32) * next_ref[0:1, 2 * tc:].astype(F32)
        p_before = jnp.where(first, 0.0, p_before)
        p_after = jnp.where(last, 0.0, p_after)
        pm1, pp1 = _shift_rows(p, p_before, p_after)
        w = w_ref[...]
        q = w[0:1] * pm1 + w[1:2] * p + w[2:3] * pp1
        o_ref[...] = (bv * q).astype(o_ref.dtype)

    cur, prev, nxt = _conv_specs(geo, tc, t)
    return _call(
        body, hosted, name=name, grid=(c // tc, t // tile),
        in_specs=[cur, prev, nxt, pl.BlockSpec((3, tc), lambda j, i: (0, j))],
        out_specs=pl.BlockSpec((tile, tc), lambda j, i: (i, j)), out_shape=jax.ShapeDtypeStruct((t, c), BF16),
        compiler_params=_params("parallel", "parallel"),
    )(proj, proj, proj, conv_w)


def _conv_bwd(proj, dyc, conv_w, geo, name, hosted=None):
    t, c3 = proj.shape
    c = c3 // 3
    tc, tile = _conv_tile(c), geo.tile

    def body(cur_ref, prev_ref, next_ref, d_ref, dprev_ref, dnext_ref, w_ref, o_ref, dw_ref):
        i = pl.program_id(1)
        first, last = _conv_edges(geo, i)
        bv = cur_ref[:, :tc].astype(F32)
        cv = cur_ref[:, tc:2 * tc].astype(F32)
        uv = cur_ref[:, 2 * tc:].astype(F32)
        p = cv * uv
        p_before = prev_ref[HALO - 1:HALO, tc:2 * tc].astype(F32) * prev_ref[HALO - 1:HALO, 2 * tc:].astype(F32)
        p_after = next_ref[0:1, tc:2 * tc].astype(F32) * next_ref[0:1, 2 * tc:].astype(F32)
        p_before = jnp.where(first, 0.0, p_before)
        p_after = jnp.where(last, 0.0, p_after)
        pm1, pp1 = _shift_rows(p, p_before, p_after)
        w = w_ref[...]
        q = w[0:1] * pm1 + w[1:2] * p + w[2:3] * pp1
        dy = d_ref[...].astype(F32)
        dq = dy * bv
        dq_before = dprev_ref[HALO - 1:HALO, :].astype(F32) * prev_ref[HALO - 1:HALO, :tc].astype(F32)
        dq_after = dnext_ref[0:1, :].astype(F32) * next_ref[0:1, :tc].astype(F32)
        dq_before = jnp.where(first, 0.0, dq_before)
        dq_after = jnp.where(last, 0.0, dq_after)
        dqm1, dqp1 = _shift_rows(dq, dq_before, dq_after)
        dp = w[0:1] * dqp1 + w[1:2] * dq + w[2:3] * dqm1
        o_ref[:, :tc] = (dy * q).astype(o_ref.dtype)
        o_ref[:, tc:2 * tc] = (dp * uv).astype(o_ref.dtype)
        o_ref[:, 2 * tc:] = (dp * cv).astype(o_ref.dtype)

        @pl.when(i == 0)
        def _():
            dw_ref[...] = jnp.zeros_like(dw_ref)

        dw_ref[0:1, :] += jnp.sum(dq * pm1, axis=0, keepdims=True)
        dw_ref[1:2, :] += jnp.sum(dq * p, axis=0, keepdims=True)
        dw_ref[2:3, :] += jnp.sum(dq * pp1, axis=0, keepdims=True)

    cur, prev, nxt = _conv_specs(geo, tc, t)
    per, lastb = tile // HALO, t // HALO - 1
    dcur = pl.BlockSpec((tile, tc), lambda j, i: (i, j))
    dprev = pl.BlockSpec((HALO, tc), lambda j, i: (jnp.maximum(i * per - 1, 0), j))
    dnext = pl.BlockSpec((HALO, tc), lambda j, i: (jnp.minimum((i + 1) * per, lastb), j))
    wspec = pl.BlockSpec((3, tc), lambda j, i: (0, j))
    return _call(
        body, hosted, name=name, grid=(c // tc, t // tile), in_specs=[cur, prev, nxt, dcur, dprev, dnext, wspec],
        out_specs=(cur, wspec), out_shape=(jax.ShapeDtypeStruct((t, c3), BF16), jax.ShapeDtypeStruct((3, c), F32)),
        compiler_params=_params("parallel", "arbitrary"),
    )(proj, proj, proj, dyc, dyc, dyc, conv_w)


def _sconv_fwd(h, g, mod4, w_in, conv_w, w_out, geo, tag, hosts, got):
    nx = _pre_fwd(h, g, mod4, 3, geo, f"{tag}_pre")
    proj = _with_host(_mm, hosts, got, "mix_a", nx, w_in, out_dtype=BF16, name=f"{tag}_in")
    yc = _with_host(_conv_fwd, hosts, got, "mix_b", proj, conv_w, geo, f"{tag}_conv")
    h_out, y = _mm(yc, w_out, name=f"{tag}_out", gate=(h, mod4, 5, 1.0, geo))
    return h_out, (h, nx, proj, yc, y)


def _sconv_bwd(dh_out, saved, g, mod4, w_in, conv_w, w_out, geo, tag, hosts, got):
    h, nx, proj, yc, y = saved
    dy, dgate = _gate_bwd(dh_out, y, mod4, 5, 1.0, geo, f"{tag}_dgate")
    dyc = _with_host(_mm, hosts, got, "mix_d", dy, w_out, tb=True, out_dtype=BF16, name=f"{tag}_dyc")
    dw_out = _tn_wide(yc, dy, f"{tag}_dwout")
    dproj, dconv = _with_host(_conv_bwd, hosts, got, "mix_c", proj, dyc, conv_w, geo, f"{tag}_dconv")
    dnx = _with_host(_mm, hosts, got, "mix_b", dproj, w_in, tb=True, name=f"{tag}_dnx")
    dw_in = _with_host(_tn_wide, hosts, got, "mix_a", nx, dproj, f"{tag}_dwin")
    dh, dg, dshift, dscale = _pre_bwd(h, g, mod4, 3, dnx, dh_out, geo, f"{tag}_dpre")
    return dh, dg, (dshift, dscale, dgate), dw_in, dconv, dw_out


def _rope_swap(v):
    nf = QK_ROPE // 4
    return v.reshape(v.shape[:-1] + (2, 2, nf)).swapaxes(-3, -2).reshape(v.shape)


def _rope_tables(geo):
    n = geo.n_lat
    nf = QK_ROPE // 4
    pos = np.arange(n)
    inv = ROPE_BASE ** (-np.arange(nf, dtype=np.float32) / nf)
    ang = np.concatenate([(pos // GRID_W)[:, None] * inv, (pos % GRID_W)[:, None] * inv], axis=1).astype(np.float32)
    cos, sin = np.cos(ang), np.sin(ang)
    zeros = np.zeros((n, LANE - QK_ROPE), np.float32)
    c_lat = np.concatenate([cos, cos, zeros], axis=1)
    s_lat = np.concatenate([-sin, sin, zeros], axis=1)
    c_ctx = np.concatenate([np.ones((2 * geo.n_ctx, QK_ROPE), np.float32), np.zeros((2 * geo.n_ctx, LANE - QK_ROPE), np.float32)], 1)
    s_ctx = np.zeros((2 * geo.n_ctx, LANE), np.float32)
    return (jnp.asarray(np.concatenate([c_lat, c_lat, c_ctx], 0)), jnp.asarray(np.concatenate([s_lat, s_lat, s_ctx], 0)))


def _swap_halves(v):
    lanes = lax.broadcasted_iota(jnp.int32, v.shape, 1)
    return jnp.where(lanes < QK_ROPE // 2, pltpu.roll(v, LANE - QK_ROPE // 2, 1), pltpu.roll(v, QK_ROPE // 2, 1))


def _latent_norm_fwd(down, g_qa, g_kva, geo, name):
    t, wd = down.shape
    ql, kl = g_qa.shape[1], g_kva.shape[1]
    tile = geo.tile

    def body(d_ref, gq_ref, gk_ref, cq_ref, ckv_ref):
        for lo, n, g_ref, o_ref in ((0, ql, gq_ref, cq_ref), (ql, kl, gk_ref, ckv_ref)):
            x = d_ref[:, lo:lo + n]
            r = lax.rsqrt(jnp.mean(x * x, axis=-1, keepdims=True) + EPS)
            o_ref[...] = (x * r * g_ref[...]).astype(o_ref.dtype)

    return pl.pallas_call(
        body, name=name, grid=(t // tile,),
        in_specs=[pl.BlockSpec((tile, wd), lambda i: (i, 0)), pl.BlockSpec((1, ql), lambda i: (0, 0)),
                  pl.BlockSpec((1, kl), lambda i: (0, 0))],
        out_specs=(pl.BlockSpec((tile, ql), lambda i: (i, 0)), pl.BlockSpec((tile, kl), lambda i: (i, 0))),
        out_shape=(jax.ShapeDtypeStruct((t, ql), BF16), jax.ShapeDtypeStruct((t, kl), BF16)),
        compiler_params=_params("parallel"),
    )(down, g_qa, g_kva)


def _latent_norm_bwd(down, g_qa, g_kva, dcqn, dckvn, dkr, geo, name):
    t, wd = down.shape
    ql, kl = g_qa.shape[1], g_kva.shape[1]
    tile = geo.tile

    def body(d_ref, gq_ref, gk_ref, dq_ref, dk_ref, dkr_ref, o_ref, dgq_ref, dgk_ref):
        i = pl.program_id(0)

        @pl.when(i == 0)
        def _():
            dgq_ref[...] = jnp.zeros_like(dgq_ref)
            dgk_ref[...] = jnp.zeros_like(dgk_ref)

        for lo, n, g_ref, dy_ref, dg_ref in ((0, ql, gq_ref, dq_ref, dgq_ref), (ql, kl, gk_ref, dk_ref, dgk_ref)):
            x = d_ref[:, lo:lo + n]
            dy = dy_ref[...].astype(F32)
            r = lax.rsqrt(jnp.mean(x * x, axis=-1, keepdims=True) + EPS)
            xhat = x * r
            u = dy * g_ref[...]
            o_ref[:, lo:lo + n] = (r * (u - xhat * jnp.mean(u * xhat, axis=-1, keepdims=True))).astype(o_ref.dtype)
            dg_ref[...] += jnp.sum(dy * xhat, axis=0, keepdims=True)
        o_ref[:, ql + kl:] = dkr_ref[...].astype(o_ref.dtype)

    def row(n):
        return pl.BlockSpec((tile, n), lambda i: (i, 0))

    def vec(n):
        return pl.BlockSpec((1, n), lambda i: (0, 0))

    return pl.pallas_call(
        body, name=name, grid=(t // tile,),
        in_specs=[row(wd), vec(ql), vec(kl), row(ql), row(kl), row(wd - ql - kl)],
        out_specs=(row(wd), vec(ql), vec(kl)),
        out_shape=(jax.ShapeDtypeStruct((t, wd), BF16), jax.ShapeDtypeStruct((1, ql), F32),
                   jax.ShapeDtypeStruct((1, kl), F32)),
        compiler_params=_params("arbitrary"),
    )(down, g_qa, g_kva, dcqn, dckvn, dkr)


def _qk_specs(geo, xr_col, shared_rope):
    tile = geo.mm_tile
    xn_spec = pl.BlockSpec((tile, HEADS * LANE), lambda i: (i, 0))
    if shared_rope:
        xr_spec = pl.BlockSpec((tile, LANE), lambda i: (i, xr_col))
    else:
        xr_spec = pl.BlockSpec((tile, HEADS * LANE), lambda i: (i, xr_col // HEADS))
    vec = pl.BlockSpec((1, LANE), lambda i: (0, 0))
    tab = pl.BlockSpec((tile, LANE), lambda i: (i, 0))
    return tile, xn_spec, xr_spec, vec, tab


def _qk_norm(xn, xr):
    ss = jnp.sum(xn * xn, axis=-1, keepdims=True) + jnp.sum(xr * xr, axis=-1, keepdims=True)
    return lax.rsqrt(ss * (1.0 / QK_HEAD) + EPS)


def _head_lanes(ref, hh, shared=False):
    return ref[...] if shared else ref[:, hh * LANE:(hh + 1) * LANE]


def _qk_fwd(xn_arr, xr_arr, xr_col, shared_rope, gn, gr, cos, sin, geo, name):
    t = xn_arr.shape[0]
    tile, xn_spec, xr_spec, vec, tab = _qk_specs(geo, xr_col, shared_rope)

    def body(xn_ref, xr_ref, gn_ref, gr_ref, c_ref, s_ref, o_ref):
        cv, sv, gnv, grv = c_ref[...], s_ref[...], gn_ref[...], gr_ref[...]
        for hh in range(HEADS):
            xn = _head_lanes(xn_ref, hh).astype(F32)
            xr = _head_lanes(xr_ref, hh, shared_rope).astype(F32)
            r = _qk_norm(xn, xr)
            yr = xr * r * grv
            o_ref[:, hh * HEAD_PAD:hh * HEAD_PAD + LANE] = (xn * r * gnv).astype(o_ref.dtype)
            o_ref[:, hh * HEAD_PAD + LANE:(hh + 1) * HEAD_PAD] = (yr * cv + _swap_halves(yr) * sv).astype(o_ref.dtype)

    return pl.pallas_call(
        body, name=name, grid=(t // tile,), in_specs=[xn_spec, xr_spec, vec, vec, tab, tab],
        out_specs=pl.BlockSpec((tile, HEADS * HEAD_PAD), lambda i: (i, 0)),
        out_shape=jax.ShapeDtypeStruct((t, HEADS * HEAD_PAD), BF16), compiler_params=_params("parallel"),
    )(xn_arr, xr_arr, gn, gr, cos, sin)


def _qk_bwd(xn_arr, xr_arr, xr_col, shared_rope, gn, gr, cos, sin, dout, geo, name):
    t = xn_arr.shape[0]
    tile, xn_spec, xr_spec, vec, tab = _qk_specs(geo, xr_col, shared_rope)

    def body(xn_ref, xr_ref, gn_ref, gr_ref, c_ref, s_ref, d_ref, dxn_ref, dxr_ref, dgn_ref, dgr_ref):
        i = pl.program_id(0)
        cv, sv, gnv, grv = c_ref[...], s_ref[...], gn_ref[...], gr_ref[...]
        dgn = jnp.zeros((1, LANE), F32)
        dgr = jnp.zeros((1, LANE), F32)
        dxr_sum = jnp.zeros((tile, LANE), F32)
        for hh in range(HEADS):
            xn = _head_lanes(xn_ref, hh).astype(F32)
            xr = _head_lanes(xr_ref, hh, shared_rope).astype(F32)
            r = _qk_norm(xn, xr)
            xhn, xhr = xn * r, xr * r
            dyn = d_ref[:, hh * HEAD_PAD:hh * HEAD_PAD + LANE].astype(F32)
            dro = d_ref[:, hh * HEAD_PAD + LANE:(hh + 1) * HEAD_PAD].astype(F32)
            dyr = dro * cv + _swap_halves(dro * sv)
            un, ur = dyn * gnv, dyr * grv
            mean = (jnp.sum(un * xhn, axis=-1, keepdims=True) + jnp.sum(ur * xhr, axis=-1, keepdims=True)) * (1.0 / QK_HEAD)
            dxn_ref[:, hh * LANE:(hh + 1) * LANE] = (r * (un - xhn * mean)).astype(dxn_ref.dtype)
            dxr = r * (ur - xhr * mean)
            if shared_rope:
                dxr_sum = dxr_sum + dxr
            else:
                dxr_ref[:, hh * LANE:(hh + 1) * LANE] = dxr.astype(dxr_ref.dtype)
            dgn = dgn + jnp.sum(dyn * xhn, axis=0, keepdims=True)
            dgr = dgr + jnp.sum(dyr * xhr, axis=0, keepdims=True)
        if shared_rope:
            dxr_ref[...] = dxr_sum

        @pl.when(i == 0)
        def _():
            dgn_ref[...] = jnp.zeros_like(dgn_ref)
            dgr_ref[...] = jnp.zeros_like(dgr_ref)

        dgn_ref[...] += dgn
        dgr_ref[...] += dgr

    heads = pl.BlockSpec((tile, HEADS * LANE), lambda i: (i, 0))
    if shared_rope:
        dxr_spec, dxr_shape = pl.BlockSpec((tile, LANE), lambda i: (i, 0)), jax.ShapeDtypeStruct((t, LANE), F32)
    else:
        dxr_spec, dxr_shape = heads, jax.ShapeDtypeStruct((t, HEADS * LANE), BF16)
    return pl.pallas_call(
        body, name=name, grid=(t // tile,),
        in_specs=[xn_spec, xr_spec, vec, vec, tab, tab, pl.BlockSpec((tile, HEADS * HEAD_PAD), lambda i: (i, 0))],
        out_specs=(heads, dxr_spec, vec, vec),
        out_shape=(jax.ShapeDtypeStruct((t, HEADS * LANE), BF16), dxr_shape, jax.ShapeDtypeStruct((1, LANE), F32),
                   jax.ShapeDtypeStruct((1, LANE), F32)),
        compiler_params=_params("arbitrary"),
    )(xn_arr, xr_arr, gn, gr, cos, sin, dout)


def _attn_specs(geo):
    tq, nq = geo.n_ctx, geo.n_lat // geo.n_ctx

    def qrow(b, i):
        return jnp.where(i < nq, b * nq + i, 2 * nq + b)

    q_spec = pl.BlockSpec((tq, HEAD_PAD), lambda b, hh, i: (qrow(b, i), hh))
    kc_spec = pl.BlockSpec((geo.n_ctx, HEAD_PAD), lambda b, hh, i: (2 * nq + b, hh))
    kl_spec = pl.BlockSpec((geo.n_lat, HEAD_PAD), lambda b, hh, i: (b, hh))
    vc_spec = pl.BlockSpec((geo.n_ctx, V_HEAD), lambda b, hh, i: (2 * nq + b, HEADS + hh))
    vl_spec = pl.BlockSpec((geo.n_lat, V_HEAD), lambda b, hh, i: (b, HEADS + hh))
    o_spec = pl.BlockSpec((tq, V_HEAD), lambda b, hh, i: (qrow(b, i), hh))
    return tq, nq, q_spec, kc_spec, kl_spec, vc_spec, vl_spec, o_spec


def _attn_fwd(q, k, kv, with_ctx_q, geo, name, hosted=None):
    t = q.shape[0]
    tq, nq, q_spec, kc_spec, kl_spec, vc_spec, vl_spec, o_spec = _attn_specs(geo)

    def body(q_ref, kc_ref, kl_ref, vc_ref, vl_ref, o_ref):
        i = pl.program_id(2)
        qv = q_ref[...]
        s_c = _dot(qv, kc_ref[...], _NT) * QK_SCALE

        @pl.when(i < nq)
        def _():
            s_l = _dot(qv, kl_ref[...], _NT) * QK_SCALE
            m = jnp.maximum(jnp.max(s_c, axis=-1, keepdims=True), jnp.max(s_l, axis=-1, keepdims=True))
            p_c, p_l = jnp.exp(s_c - m), jnp.exp(s_l - m)
            den = jnp.sum(p_c, axis=-1, keepdims=True) + jnp.sum(p_l, axis=-1, keepdims=True)
            o = _dot(p_c, vc_ref[...], _NN) + _dot(p_l, vl_ref[...], _NN)
            o_ref[...] = (o / den).astype(o_ref.dtype)

        @pl.when(i == nq)
        def _():
            if with_ctx_q:
                m = jnp.max(s_c, axis=-1, keepdims=True)
                p_c = jnp.exp(s_c - m)
                o = _dot(p_c, vc_ref[...], _NN) / jnp.sum(p_c, axis=-1, keepdims=True)
                o_ref[...] = o.astype(o_ref.dtype)
            else:
                o_ref[...] = jnp.zeros_like(o_ref)

    return _call(
        body, hosted, name=name, grid=(2, HEADS, nq + 1), in_specs=[q_spec, kc_spec, kl_spec, vc_spec, vl_spec],
        out_specs=o_spec, out_shape=jax.ShapeDtypeStruct((t, HEADS * V_HEAD), BF16),
        compiler_params=_params("parallel", "parallel", "arbitrary"),
    )(q, k, k, kv, kv)


def _attn_bwd(q, k, kv, do, with_ctx_q, geo, name, hosted=None):
    t = q.shape[0]
    tq, nq, q_spec, kc_spec, kl_spec, vc_spec, vl_spec, o_spec = _attn_specs(geo)

    def body(q_ref, kc_ref, kl_ref, vc_ref, vl_ref, do_ref, dq_ref, dkl_ref, dkc_ref, dvl_ref, dvc_ref,
             akl_ref, akc_ref, avl_ref, avc_ref):
        i = pl.program_id(2)

        @pl.when(i == 0)
        def _():
            for ref in (akl_ref, akc_ref, avl_ref, avc_ref):
                ref[...] = jnp.zeros_like(ref)

        qv, dov = q_ref[...], do_ref[...]
        s_c = _dot(qv, kc_ref[...], _NT) * QK_SCALE
        dp_c = _dot(dov, vc_ref[...], _NT)

        def ctx_part(p_c, delta):
            ds_c = (p_c * (dp_c - delta) * QK_SCALE).astype(BF16)
            akc_ref[...] += _dot(ds_c, qv, _TN)
            avc_ref[...] += _dot(p_c, dov, _TN)
            return _dot(ds_c, kc_ref[...], _NN)

        @pl.when(i < nq)
        def _():
            s_l = _dot(qv, kl_ref[...], _NT) * QK_SCALE
            m = jnp.maximum(jnp.max(s_c, axis=-1, keepdims=True), jnp.max(s_l, axis=-1, keepdims=True))
            p_c, p_l = jnp.exp(s_c - m), jnp.exp(s_l - m)
            inv = 1.0 / (jnp.sum(p_c, axis=-1, keepdims=True) + jnp.sum(p_l, axis=-1, keepdims=True))
            p_c, p_l = p_c * inv, p_l * inv
            dp_l = _dot(dov, vl_ref[...], _NT)
            delta = jnp.sum(p_c * dp_c, axis=-1, keepdims=True) + jnp.sum(p_l * dp_l, axis=-1, keepdims=True)
            ds_l = (p_l * (dp_l - delta) * QK_SCALE).astype(BF16)
            akl_ref[...] += _dot(ds_l, qv, _TN)
            avl_ref[...] += _dot(p_l, dov, _TN)
            dq_ref[...] = (ctx_part(p_c, delta) + _dot(ds_l, kl_ref[...], _NN)).astype(dq_ref.dtype)

        @pl.when(i == nq)
        def _():
            if with_ctx_q:
                m = jnp.max(s_c, axis=-1, keepdims=True)
                p_c = jnp.exp(s_c - m)
                p_c = p_c * (1.0 / jnp.sum(p_c, axis=-1, keepdims=True))
                delta = jnp.sum(p_c * dp_c, axis=-1, keepdims=True)
                dq_ref[...] = ctx_part(p_c, delta).astype(dq_ref.dtype)
            else:
                dq_ref[...] = jnp.zeros_like(dq_ref)
            dkl_ref[...] = akl_ref[...].astype(dkl_ref.dtype)
            dkc_ref[...] = akc_ref[...].astype(dkc_ref.dtype)
            dvl_ref[...] = avl_ref[...].astype(dvl_ref.dtype)
            dvc_ref[...] = avc_ref[...].astype(dvc_ref.dtype)

    def acc_spec(rows, width):
        return pl.BlockSpec((rows, width), lambda b, hh, i: (b, hh))

    return _call(
        body, hosted, name=name, grid=(2, HEADS, nq + 1), in_specs=[q_spec, kc_spec, kl_spec, vc_spec, vl_spec, o_spec],
        out_specs=(q_spec, acc_spec(geo.n_lat, HEAD_PAD), acc_spec(geo.n_ctx, HEAD_PAD), acc_spec(geo.n_lat, V_HEAD),
                   acc_spec(geo.n_ctx, V_HEAD)),
        out_shape=(jax.ShapeDtypeStruct((t, HEADS * HEAD_PAD), BF16),
                   jax.ShapeDtypeStruct((2 * geo.n_lat, HEADS * HEAD_PAD), BF16),
                   jax.ShapeDtypeStruct((2 * geo.n_ctx, HEADS * HEAD_PAD), BF16),
                   jax.ShapeDtypeStruct((2 * geo.n_lat, HEADS * V_HEAD), BF16),
                   jax.ShapeDtypeStruct((2 * geo.n_ctx, HEADS * V_HEAD), BF16)),
        scratch_shapes=[pltpu.VMEM((geo.n_lat, HEAD_PAD), F32), pltpu.VMEM((geo.n_ctx, HEAD_PAD), F32),
                        pltpu.VMEM((geo.n_lat, V_HEAD), F32), pltpu.VMEM((geo.n_ctx, V_HEAD), F32)],
        compiler_params=_params("parallel", "parallel", "arbitrary"),
    )(q, k, k, kv, kv, do)


def _mla_fwd(h, g, mod4, w, with_ctx_q, tabs, geo, tag, hosts, got):
    cos, sin = tabs
    ql, kl = w["g_qa"].shape[1], w["g_kva"].shape[1]
    kr_col = (ql + kl) // LANE
    nx = _pre_fwd(h, g, mod4, 3, geo, f"{tag}_pre")
    down = _mm(nx, w["w_a"], name=f"{tag}_down")
    cqn, ckvn = _latent_norm_fwd(down, w["g_qa"], w["g_kva"], geo, f"{tag}_lnorm")
    qraw = _mm(cqn, w["w_uq"], out_dtype=BF16, name=f"{tag}_uq")
    kvraw = _mm(ckvn, w["w_ukv"], out_dtype=BF16, name=f"{tag}_ukv")
    q = _qk_fwd(qraw, qraw, HEADS, False, w["gq_n"], w["gq_r"], cos, sin, geo, f"{tag}_qnorm")
    k = _qk_fwd(kvraw, down, kr_col, True, w["gk_n"], w["gk_r"], cos, sin, geo, f"{tag}_knorm")
    o = _with_host(_attn_fwd, hosts, got, "mix_a", q, k, kvraw, with_ctx_q, geo, f"{tag}_attn")
    h_out, y = _mm(o, w["w_o"], name=f"{tag}_o", gate=(h, mod4, 5, 1.0, geo))
    return h_out, (h, nx, down, cqn, ckvn, qraw, kvraw, q, k, o, y)


def _mla_bwd(dh_out, saved, g, mod4, w, with_ctx_q, tabs, geo, tag, hosts, got):
    cos, sin = tabs
    h, nx, down, cqn, ckvn, qraw, kvraw, q, k, o, y = saved
    ql, kl = w["g_qa"].shape[1], w["g_kva"].shape[1]
    kr_col = (ql + kl) // LANE
    dy, dgate = _gate_bwd(dh_out, y, mod4, 5, 1.0, geo, f"{tag}_dgate")
    do = _mm(dy, w["w_o"], tb=True, out_dtype=BF16, name=f"{tag}_do")
    dw_o = _tn_wide(o, dy, f"{tag}_dwo")
    dq, dk_lat, dk_ctx, dv_lat, dv_ctx = _with_host(_attn_bwd, hosts, got, "mix_a", q, k, kvraw, do, with_ctx_q, geo,
                                                    f"{tag}_dattn")
    dk = jnp.concatenate([dk_lat, dk_ctx], axis=0)
    dqn, dqr, dgq_n, dgq_r = _qk_bwd(qraw, qraw, HEADS, False, w["gq_n"], w["gq_r"], cos, sin, dq, geo, f"{tag}_dqnorm")
    dkn, dkr, dgk_n, dgk_r = _qk_bwd(kvraw, down, kr_col, True, w["gk_n"], w["gk_r"], cos, sin, dk, geo, f"{tag}_dknorm")
    dqraw = jnp.concatenate([dqn, dqr], axis=1)
    dkvraw = jnp.concatenate([dkn, jnp.concatenate([dv_lat, dv_ctx], axis=0)], axis=1)
    dcqn = _mm(dqraw, w["w_uq"], tb=True, out_dtype=BF16, name=f"{tag}_dcqn")
    dw_uq = _tn_wide(cqn, dqraw, f"{tag}_dwuq")
    dckvn = _mm(dkvraw, w["w_ukv"], tb=True, out_dtype=BF16, name=f"{tag}_dckvn")
    dw_ukv = _tn_wide(ckvn, dkvraw, f"{tag}_dwukv")
    ddown, dg_qa, dg_kva = _latent_norm_bwd(down, w["g_qa"], w["g_kva"], dcqn, dckvn, dkr, geo, f"{tag}_dlnorm")
    dnx = _mm(ddown, w["w_a"], tb=True, name=f"{tag}_dnx")
    dw_a = _tn_wide(nx, ddown, f"{tag}_dwa")
    dh, dg, dshift, dscale = _pre_bwd(h, g, mod4, 3, dnx, dh_out, geo, f"{tag}_dpre")
    grads = dict(w_a=dw_a, g_qa=dg_qa, w_uq=dw_uq, g_kva=dg_kva, w_ukv=dw_ukv, gq_n=dgq_n, gq_r=dgq_r, gk_n=dgk_n,
                 gk_r=dgk_r, w_o=dw_o)
    return dh, dg, (dshift, dscale, dgate), grads


def _mla_prepare(w_a, g_qa, w_uq, g_kva, w_ukv, g_q, g_k, w_o):
    ql, kl = g_qa.shape[0], g_kva.shape[0]
    d = w_a.shape[0]
    w_a_pad = jnp.concatenate([w_a[:, :ql + kl], _rope_swap(w_a[:, ql + kl:]), jnp.zeros((d, LANE - QK_ROPE), w_a.dtype)], axis=1)
    uq = w_uq.reshape(ql, HEADS, QK_HEAD)
    uq_r = jnp.pad(_rope_swap(uq[:, :, QK_NOPE:]), ((0, 0), (0, 0), (0, LANE - QK_ROPE)))
    w_uq_pad = jnp.concatenate([uq[:, :, :QK_NOPE].reshape(ql, HEADS * LANE), uq_r.reshape(ql, HEADS * LANE)], axis=1)
    ukv = w_ukv.reshape(kl, HEADS, QK_NOPE + V_HEAD)
    w_ukv_p = jnp.concatenate([ukv[:, :, :QK_NOPE].reshape(kl, HEADS * LANE), ukv[:, :, QK_NOPE:].reshape(kl, HEADS * V_HEAD)], axis=1)

    def gains(gv):
        gv = gv.astype(F32)
        return gv[None, :QK_NOPE], jnp.pad(_rope_swap(gv[QK_NOPE:]), (0, LANE - QK_ROPE))[None]

    gq_n, gq_r = gains(g_q)
    gk_n, gk_r = gains(g_k)
    return dict(w_a=w_a_pad, g_qa=g_qa.astype(F32)[None], w_uq=w_uq_pad, g_kva=g_kva.astype(F32)[None], w_ukv=w_ukv_p,
                gq_n=gq_n, gq_r=gq_r, gk_n=gk_n, gk_r=gk_r, w_o=w_o)


def _mla_unprepare(gr):
    ql, kl = gr["g_qa"].shape[1], gr["g_kva"].shape[1]
    dw_a = jnp.concatenate([gr["w_a"][:, :ql + kl], _rope_swap(gr["w_a"][:, ql + kl:ql + kl + QK_ROPE])], axis=1)
    uqn = gr["w_uq"][:, :HEADS * LANE].reshape(ql, HEADS, LANE)
    uqr = _rope_swap(gr["w_uq"][:, HEADS * LANE:].reshape(ql, HEADS, LANE)[:, :, :QK_ROPE])
    dw_uq = jnp.concatenate([uqn, uqr], axis=2).reshape(ql, HEADS * QK_HEAD)
    ukn = gr["w_ukv"][:, :HEADS * LANE].reshape(kl, HEADS, LANE)
    ukv = gr["w_ukv"][:, HEADS * LANE:].reshape(kl, HEADS, V_HEAD)
    dw_ukv = jnp.concatenate([ukn, ukv], axis=2).reshape(kl, HEADS * (QK_NOPE + V_HEAD))

    def gains(gn, grr):
        return jnp.concatenate([gn[0], _rope_swap(grr[0, :QK_ROPE])])

    return dict(mla_w_a=dw_a, mla_g_qa=gr["g_qa"][0], mla_w_uq=dw_uq, mla_g_kva=gr["g_kva"][0], mla_w_ukv=dw_ukv,
                mla_g_q=gains(gr["gq_n"], gr["gq_r"]), mla_g_k=gains(gr["gk_n"], gr["gk_r"]), mla_w_o=gr["w_o"])


def _loss_head(h, target, geo, name):
    t, d = h.shape
    tile = geo.tile
    n_lat_tiles = 2 * geo.n_lat // tile

    def body(h_ref, t_ref, dh_ref, loss_ref):
        i = pl.program_id(0)

        @pl.when(i == 0)
        def _():
            loss_ref[...] = jnp.zeros_like(loss_ref)

        @pl.when(i < n_lat_tiles)
        def _():
            e = h_ref[...] - t_ref[...]
            dh_ref[...] = e * (1.0 / d)
            part = jnp.sum(e * e, axis=0, keepdims=True) * (0.5 / d)
            loss_ref[...] += sum(part[:, j * LANE:(j + 1) * LANE] for j in range(d // LANE))

        @pl.when(i >= n_lat_tiles)
        def _():
            dh_ref[...] = jnp.zeros_like(dh_ref)

    row = pl.BlockSpec((tile, d), lambda i: (i, 0))
    tgt = pl.BlockSpec((tile, d), lambda i: (jnp.minimum(i, n_lat_tiles - 1), 0))
    dh, loss = pl.pallas_call(
        body, name=name, grid=(t // tile,), in_specs=[row, tgt],
        out_specs=(row, pl.BlockSpec((1, LANE), lambda i: (0, 0))),
        out_shape=(jax.ShapeDtypeStruct((t, d), F32), jax.ShapeDtypeStruct((1, LANE), F32)),
        compiler_params=_params("arbitrary"),
    )(h, target)
    return jnp.sum(loss), dh


def _adamw(w, g, m, v, name):
    shape = w.shape
    cols = shape[-1]
    rows = int(np.prod(shape[:-1])) if len(shape) > 1 else 1
    w2, g2, m2, v2 = (a.reshape(rows, cols) for a in (w, g, m, v))
    tr = _pick(rows, (512, 256, 128, 64, 32, 16, 8))
    c1 = 1.0 / (1.0 - ADAM_B1 ** ADAM_STEP)
    c2 = 1.0 / (1.0 - ADAM_B2 ** ADAM_STEP)

    def body(w_ref, g_ref, m_ref, v_ref, d_ref, mo_ref, vo_ref):
        gv = g_ref[...]
        mn = ADAM_B1 * m_ref[...] + (1.0 - ADAM_B1) * gv
        vn = ADAM_B2 * v_ref[...] + (1.0 - ADAM_B2) * (gv * gv)
        d_ref[...] = -ADAM_LR * ((mn * c1) / (jnp.sqrt(vn * c2) + ADAM_EPS) + ADAM_WD * w_ref[...])
        mo_ref[...] = mn
        vo_ref[...] = vn

    blk = pl.BlockSpec((tr, cols), lambda i: (i, 0))
    sds = jax.ShapeDtypeStruct((rows, cols), F32)
    d, mo, vo = pl.pallas_call(
        body, name=name, grid=(rows // tr,), in_specs=[blk] * 4, out_specs=(blk,) * 3, out_shape=(sds,) * 3,
        compiler_params=_params("parallel"),
    )(w2, g2, m2, v2)
    return d.reshape(shape), mo.reshape(shape), vo.reshape(shape)


SHARD_AXIS = {
    "w_mod": 2, "g_norm": 2, "ffn_w1": 3, "ffn_w3": 3, "ffn_w2": 2, "sc_w_in": 2, "sc_conv": 2, "sc_w_out": 1,
    "mla_w_a": 1, "mla_g_qa": 1, "mla_w_uq": 2, "mla_w_ukv": 2, "mla_w_o": 1,
}
HIDDEN_MAJOR = ("ffn_w1", "ffn_w3")


def _view(name, arr, swapped=False):
    form, swap, _ = EXCHANGE[name]
    if swap and not swapped:
        arr = jnp.swapaxes(arr, -1, -2)
    if form == "mid":
        arr = arr.reshape((-1,) + arr.shape[-2:])
        return jnp.pad(arr, ((0, 0), (0, 0), (0, -arr.shape[-1] % LANE)))
    arr = arr.reshape(-1, arr.shape[-1])
    return jnp.pad(arr, ((0, -arr.shape[0] % 16), (0, 0)))


def _unview(name, view, shape, keep_swapped=False):
    form, swap, _ = EXCHANGE[name]
    shape = shape[:-2] + (shape[-1], shape[-2]) if swap else shape
    if form == "mid":
        view = view[:, :, :shape[-1]]
    else:
        view = view[:int(np.prod(shape[:-1]))]
    arr = view.reshape(shape)
    return arr if (not swap or keep_swapped) else jnp.swapaxes(arr, -1, -2)


def _full_shape(name, local_shape):
    ax = SHARD_AXIS[name]
    return local_shape[:ax] + (N_DEV * local_shape[ax],) + local_shape[ax + 1:]


def _win(ref, form, n, j):
    start = j * n
    if not isinstance(start, int):
        start = pl.multiple_of(start, LANE if form == "last" else math.gcd(n, 16))
    if form == "mid":
        return ref.at[:, pl.ds(start, n), :]
    return ref.at[:, pl.ds(start, n)]


def _windows(view, count, of):
    return view.shape[:1] + (view.shape[1] * count // of,) + view.shape[2:]


def _gather_work(views, forms):
    na = len(views)

    def plan(x_refs, out_refs, sems):
        send_sems, recv_sems, local_sems = sems
        x, y, c = lax.axis_index("x"), lax.axis_index("y"), lax.axis_index("c")
        me, sibling = (x, y, c), (x, y, 1 - c)
        chips = [(1 - x, y), (x, 1 - y), (1 - x, 1 - y)]

        def copy(a, k, block, to, from_input):
            dst = _win(out_refs[a], forms[a], views[a].shape[1], 4 * block[0] + 2 * block[1] + block[2])
            return pltpu.make_async_remote_copy(
                src_ref=x_refs[a] if from_input else dst, dst_ref=dst, send_sem=send_sems.at[a, k],
                recv_sem=recv_sems.at[a, k], device_id=to, device_id_type=MESH)

        mine = [pltpu.make_async_copy(x_refs[a], _win(out_refs[a], forms[a], views[a].shape[1], 4 * x + 2 * y + c),
                                      local_sems.at[a]) for a in range(na)]
        first = []
        for a in range(na):
            first.append(copy(a, 0, me, sibling, True))
            first += [copy(a, 1 + j, me, (*chip, c), True) for j, chip in enumerate(chips)]
        return copy, mine, first, me, sibling, chips, c

    def start(x_refs, out_refs, sems):
        _, mine, first, *_ = plan(x_refs, out_refs, sems)
        for cp in mine + first:
            cp.start()

    def finish(x_refs, out_refs, sems):
        copy, mine, first, me, sibling, chips, c = plan(x_refs, out_refs, sems)
        passed = []
        for j, chip in enumerate(chips):
            for a in range(na):
                copy(a, 1 + j, (*chip, c), me, False).wait_recv()
                fwd = copy(a, 4 + j, (*chip, c), sibling, False)
                fwd.start()
                passed.append(fwd)
        for a in range(na):
            copy(a, 0, sibling, me, False).wait_recv()
            for j, chip in enumerate(chips):
                copy(a, 4 + j, (*chip, 1 - c), me, False).wait_recv()
        for cp in first + passed:
            cp.wait_send()
        for cp in mine:
            cp.wait()

    return Hosted(
        list(views), [jax.ShapeDtypeStruct(_windows(v, N_DEV, 1), v.dtype) for v in views],
        [pltpu.SemaphoreType.DMA((na, 7)), pltpu.SemaphoreType.DMA((na, 7)), pltpu.SemaphoreType.DMA((na,))], start, finish)


def _push_work(srcs, out_shapes, n_copies, make_copies):
    na = len(srcs)

    def start(s_refs, r_refs, sems):
        for cp in make_copies(s_refs, r_refs, sems[0], sems[1]):
            cp.start()

    def finish(s_refs, r_refs, sems):
        copies = make_copies(s_refs, r_refs, sems[0], sems[1])
        for cp in copies:
            cp.wait_recv()
        for cp in copies:
            cp.wait_send()

    return Hosted(list(srcs), out_shapes, [pltpu.SemaphoreType.DMA((na, n_copies)), pltpu.SemaphoreType.DMA((na, n_copies))],
                  start, finish)


def _sibling_work(fulls, forms):
    na = len(fulls)
    widths = [f.shape[1] // N_DEV for f in fulls]

    def make_copies(g_refs, r_refs, send_sems, recv_sems):
        x, y, c = lax.axis_index("x"), lax.axis_index("y"), lax.axis_index("c")
        return [
            pltpu.make_async_remote_copy(
                src_ref=_win(g_refs[a], forms[a], widths[a], 2 * chip + (1 - c)),
                dst_ref=_win(r_refs[a], forms[a], widths[a], chip), send_sem=send_sems.at[a, chip],
                recv_sem=recv_sems.at[a, chip], device_id=(x, y, 1 - c), device_id_type=MESH)
            for a in range(na) for chip in range(N_CHIP)
        ]

    return _push_work(fulls, [jax.ShapeDtypeStruct(_windows(f, N_CHIP, N_DEV), f.dtype) for f in fulls], N_CHIP, make_copies)


def _chip_work(parts, forms):
    na = len(parts)
    widths = [p.shape[1] // N_CHIP for p in parts]

    def make_copies(p_refs, r_refs, send_sems, recv_sems):
        x, y, c = lax.axis_index("x"), lax.axis_index("y"), lax.axis_index("c")
        chips = [(1 - x, y), (x, 1 - y), (1 - x, 1 - y)]
        return [
            pltpu.make_async_remote_copy(
                src_ref=_win(p_refs[a], forms[a], widths[a], 2 * px + py), dst_ref=_win(r_refs[a], forms[a], widths[a], j),
                send_sem=send_sems.at[a, j], recv_sem=recv_sems.at[a, j], device_id=(px, py, c), device_id_type=MESH)
            for a in range(na) for j, (px, py) in enumerate(chips)
        ]

    return _push_work(parts, [jax.ShapeDtypeStruct(_windows(p, 3, N_CHIP), p.dtype) for p in parts], 3, make_copies)


def _sum_tiles(view, form, n):
    if form == "mid":
        tr = n
        while tr * view.shape[2] * 4 > 2 * 1024 * 1024 and tr % 32 == 0:
            tr //= 2
        return 1, tr
    return _pick(view.shape[0], (512, 256, 128, 64, 32, 16)), n


def _window_spec(form, tl, tr, rest, window_of):
    if form == "mid":
        return lambda per: pl.BlockSpec((None, tr) + rest, lambda l, k, i, s: (l, window_of(k, s) * per + i, 0))
    return lambda per: pl.BlockSpec((tl, tr), lambda l, k, i, s: (l, window_of(k, s)))


def _chip_partials(g, recv, core, form, name):
    n = g.shape[1] // N_DEV
    tl, tr = _sum_tiles(g, form, n)
    per = n // tr
    rest = tuple(g.shape[2:])

    def body(core_ref, g_ref, r_ref, o_ref):
        o_ref[...] = (g_ref[...] + r_ref[...]).astype(o_ref.dtype)

    own = _window_spec(form, tl, tr, rest, lambda k, s: 2 * k + s[0])(per)
    by_chip = _window_spec(form, tl, tr, rest, lambda k, s: k)(per)
    return pl.pallas_call(
        body, name=name,
        grid_spec=pltpu.PrefetchScalarGridSpec(
            num_scalar_prefetch=1, grid=(g.shape[0] // tl, N_CHIP, per), in_specs=[own, by_chip], out_specs=by_chip),
        out_shape=jax.ShapeDtypeStruct(recv.shape, BF16), compiler_params=_params("parallel", "parallel", "parallel"),
    )(core, g, recv)


def _reduce_final(p, recv, chip, form, name):
    n = p.shape[1] // N_CHIP
    tl, tr = _sum_tiles(p, form, n)
    per = n // tr
    rest = tuple(p.shape[2:])

    def body(chip_ref, p_ref, ry_ref, rx_ref, rxy_ref, o_ref):
        own_pair = p_ref[...].astype(F32) + ry_ref[...].astype(F32)
        o_ref[...] = own_pair + (rx_ref[...].astype(F32) + rxy_ref[...].astype(F32))

    def rel(j):
        return _window_spec(form, tl, tr, rest, lambda k, s: j)(per)

    own = _window_spec(form, tl, tr, rest, lambda k, s: s[0])(per)
    return pl.pallas_call(
        body, name=name,
        grid_spec=pltpu.PrefetchScalarGridSpec(
            num_scalar_prefetch=1, grid=(p.shape[0] // tl, 1, per), in_specs=[own, rel(1), rel(0), rel(2)],
            out_specs=rel(0)),
        out_shape=jax.ShapeDtypeStruct(p.shape[:1] + (n,) + p.shape[2:], F32),
        compiler_params=_params("parallel", "parallel", "parallel"),
    )(chip, p, recv, recv, recv)


def _pack_replicated(arrays):
    pieces = []
    for a in arrays:
        flat = a.reshape(-1).astype(F32)
        pieces.append(jnp.pad(flat, (0, -flat.size % LANE)))
    total = sum(p.size for p in pieces)
    pieces.append(jnp.zeros((-total % (16 * LANE),), F32))
    return jnp.concatenate(pieces).reshape(-1, LANE)


def _unpack_replicated(buf, shapes):
    flat, out, off = buf.reshape(-1), [], 0
    for shape in shapes:
        size = int(np.prod(shape))
        out.append(flat[off:off + size].reshape(shape))
        off += size + (-size % LANE)
    return out


def _silu(v):
    return v * jax.nn.sigmoid(v)


FFN_NAMES = ("ffn_w1", "ffn_w3", "ffn_w2")
SC_NAMES = ("sc_w_in", "sc_conv", "sc_w_out")
MLA_SHARDED = ("mla_w_a", "mla_g_qa", "mla_w_uq", "mla_w_ukv", "mla_w_o")
MLA_NAMES = ("mla_w_a", "mla_g_qa", "mla_w_uq", "mla_g_kva", "mla_w_ukv", "mla_g_q", "mla_g_k", "mla_w_o")


def _local_step(src, x, c, ctx, target):
    bsz, n_lat, d = x.shape
    n_ctx = ctx.shape[1]
    assert bsz == 2
    geo = Geo(n_lat, n_ctx)
    depth = src.depth
    tc = _conv_tile(d)
    tabs = _rope_tables(geo)

    h = jnp.concatenate([x.reshape(2 * n_lat, d), ctx.reshape(2 * n_ctx, d)], axis=0)
    tgt = target.reshape(2 * n_lat, d)

    saved = []
    for i in range(depth):
        kind = i % 2
        wl, slots = src.weights(i), src.fwd_slots(i)
        gn = wl["g_norm"].astype(F32)
        mod4 = src.mod(i).reshape(N_SEG, N_MOD, 1, d)
        h, s1 = _ffn_fwd(h, gn[0:1], mod4, 0, wl, 0, geo, f"l{i}_f1", "f1", slots, slots)
        if kind == 0:
            mix = (_interleave(wl["sc_w_in"], 3, tc), wl["sc_conv"].astype(F32), wl["sc_w_out"])
            h, s2 = _sconv_fwd(h, gn[1:2], mod4, *mix, geo, f"l{i}_sc", slots, slots)
        else:
            mix = _mla_prepare(*[wl[name] for name in MLA_NAMES])
            h, s2 = _mla_fwd(h, gn[1:2], mod4, mix, i != depth - 1, tabs, geo, f"l{i}_mla", slots, slots)
        h, s3 = _ffn_fwd(h, gn[2:3], mod4, 6, wl, 1, geo, f"l{i}_f2", "f2", slots, slots)
        saved.append((wl, gn, mod4, mix, s1, s2, s3))

    loss, dh = _loss_head(h, tgt, geo, "loss_head")

    g_b_mod = [None] * depth
    for i in reversed(range(depth)):
        kind = i % 2
        wl, gn, mod4, mix, s1, s2, s3 = saved[i]
        slots = src.bwd_slots(i)
        gbuf = {name: lax.empty(wl[name].shape, F32) for name in ("ffn_w1", "ffn_w3", "ffn_w2")}
        dh, dg2, dm2 = _ffn_bwd(dh, s3, gn[2:3], mod4, 6, wl, 1, gbuf, geo, f"l{i}_f2", "f2", slots, slots)
        src.ffn2_grads(i, gbuf)
        if kind == 0:
            dh, dg1, dm1, dwin, dconv, dwout = _sconv_bwd(dh, s2, gn[1:2], mod4, *mix, geo, f"l{i}_sc", slots, slots)
            gl = dict(sc_w_in=_deinterleave(dwin, 3, tc), sc_conv=dconv, sc_w_out=dwout)
        else:
            dh, dg1, dm1, gm = _mla_bwd(dh, s2, gn[1:2], mod4, mix, i != depth - 1, tabs, geo, f"l{i}_mla", slots, slots)
            gl = _mla_unprepare(gm)
        dh, dg0, dm0 = _ffn_bwd(dh, s1, gn[0:1], mod4, 0, wl, 0, gbuf, geo, f"l{i}_f1", "f1", slots, slots)
        dmod = jnp.concatenate(list(dm0) + list(dm1) + list(dm2), axis=1).reshape(N_SEG, N_MOD * d)
        dmod8 = jnp.concatenate([dmod, jnp.zeros((8 - N_SEG, N_MOD * d), F32)], axis=0)
        g_b_mod[i] = jnp.sum(dmod, axis=0)
        gl.update(gbuf, g_norm=jnp.concatenate([dg0, dg1, dg2], axis=0))
        src.dmod(i, dmod8)
        src.grads(i, gl)

    grad_x = dh[:2 * n_lat].reshape(x.shape)
    return loss, grad_x, jnp.stack(g_b_mod)


class _Slots:
    def __init__(self, get, put):
        self.get, self._put = get, put

    def __setitem__(self, slot, outs):
        self._put(slot, outs)


FWD_PLAN = {
    0: {"f1_up": ("ffn_w1",), "f1_down": ("g_norm", "mix"), "mix_a": ("ffn_w3",), "mix_b": ("ffn_w2",)},
    1: {"f1_up": ("ffn_w1",), "mix_a": ("ffn_w3", "g_norm", "mix"), "f2_up": ("ffn_w2",)},
}
SIBLING_PLAN = {"f2_dact": ("ffn_w1", "g_norm", "mix"), "f2_dw2": ("ffn_w3", "ffn_w2")}
BWD_PLAN = {
    0: {"f2_dnx": ("ffn_w1",), "mix_b": ("ffn_w3",), "mix_a": ("ffn_w2",), "f1_dact": ("g_norm", "mix")},
    1: {"f2_dnx": ("ffn_w1",), "mix_a": ("ffn_w3", "ffn_w2"), "f1_dnx": ("g_norm", "mix")},
}
DMOD_SLOT = {0: "mix_c", 1: "f1_dact"}
MOD_ROWS = 32


class _Exchange:
    def __init__(self, w):
        self.w = w
        self.depth = w["w_mod"].shape[0]
        self.c_ctx = w["c_ctx"]
        self.me = 4 * lax.axis_index("x") + 2 * lax.axis_index("y") + lax.axis_index("c")
        self.core = lax.axis_index("c").astype(jnp.int32).reshape(1)
        self.chip = (2 * lax.axis_index("x") + lax.axis_index("y")).astype(jnp.int32).reshape(1)
        self.full, self.gviews, self.parts, self.reduced, self.rep, self.dmods = {}, {}, {}, {}, {}, {}
        self.ctx_pre = jnp.zeros_like(self.c_ctx)

    def _layer_of(self, name, i):
        return i // 2 if name.startswith(("sc_", "mla_")) else i

    def _mixer(self, i):
        return SC_NAMES if i % 2 == 0 else MLA_SHARDED

    def _expand(self, names, i):
        out = []
        for name in names:
            out += list(self._mixer(i)) if name == "mix" else [name]
        return out

    def _group(self, i):
        return ["g_norm", "ffn_w1", "ffn_w3", "ffn_w2"] + list(self._mixer(i))

    def _local(self, name, i):
        arr = self.w[name][self._layer_of(name, i)]
        return arr[:, None] if name == "mla_g_qa" else arr

    def _shapes(self, name, i):
        local = tuple(self._local(name, i).shape)
        ax = SHARD_AXIS[name] - 1
        return local, local[:ax] + (N_DEV * local[ax],) + local[ax + 1:]

    def _gather(self, names, i):
        views = [_view(n, self._local(n, i).astype(BF16 if EXCHANGE[n][2] else F32)) for n in names]
        return _gather_work(views, [EXCHANGE[n][0] for n in names])

    def _gathered(self, names, i, outs):
        for name, fv in zip(names, outs):
            arr = _unview(name, fv, self._shapes(name, i)[1], keep_swapped=name in HIDDEN_MAJOR)
            self.full[name, i] = arr[:, 0] if name == "mla_g_qa" else arr

    def prefetch(self, c):
        bsz, d = c.shape
        (conds,) = _run_hosted(_gather_work([jnp.pad(c, ((0, 8 - bsz), (0, 0)))[None]], ["mid"]), "gather_cond")
        conds = conds.reshape(N_DEV, 8, d)[:, :bsz]
        act = _silu(jnp.concatenate([conds, jnp.broadcast_to(self.c_ctx, (N_DEV, 1, d))], axis=1))
        self.s_rows = jnp.pad(act.reshape(N_DEV * N_SEG, d), ((0, MOD_ROWS - N_DEV * N_SEG), (0, 0)))
        cols = jnp.stack([_mm(self.s_rows, self.w["w_mod"][l], name=f"mod_cols_{l}") for l in range(self.depth)])
        names = self._group(0)
        work = self._gather(names, 0)
        both = _gather_work(work.inputs + [cols.reshape(self.depth * MOD_ROWS, -1)], self._forms(names) + ["last"])
        outs = _run_hosted(both, "gather_l0")
        self._gathered(names, 0, outs[:-1])
        mods = lax.dynamic_slice_in_dim(outs[-1].reshape(self.depth, MOD_ROWS, -1), N_SEG * self.me, N_SEG, axis=1)
        self.mods = mods + self.w["b_mod"][:, None, :]

    def mod(self, i):
        return self.mods[i]

    def weights(self, i):
        wl = {name: self.full[name, i] for name in self._group(i)}
        if i % 2 == 1:
            for name in ("mla_g_kva", "mla_g_q", "mla_g_k"):
                wl[name] = self.w[name][i // 2]
        return wl

    def fwd_slots(self, i):
        plan = FWD_PLAN[i % 2] if i + 1 < self.depth else {}
        names = {slot: self._expand(plan[slot], i + 1) for slot in plan}
        return _Slots(lambda slot: self._gather(names[slot], i + 1) if slot in names else None,
                      lambda slot, outs: self._gathered(names[slot], i + 1, outs))

    def ffn2_grads(self, i, gbuf):
        if i == 0:
            for name in FFN_NAMES:
                self.gviews[name + "#1", 0] = _view(name, gbuf[name][1:2], swapped=True)

    def grads(self, i, gl):
        for name in self._group(i):
            g = gl[name][:, None] if name == "mla_g_qa" else gl[name]
            if i == 0 and name in FFN_NAMES:
                self.gviews[name + "#0", 0] = _view(name, g[0:1], swapped=True)
            else:
                self.gviews[name, i] = _view(name, g, swapped=name in HIDDEN_MAJOR)
        for name in REPLICATED:
            if name in gl:
                self.rep[name, i // 2] = gl[name]

    def dmod(self, i, dmod8):
        self.dmods[i] = dmod8

    def _dmod_gather(self, i):
        return _gather_work([self.dmods[i][None]], ["mid"])

    def _dmod_gathered(self, i, outs):
        n = self.w["w_mod"].shape[2]
        rows = outs[0].reshape(N_DEV, 8, -1)[:, :N_SEG]
        mine = lax.dynamic_slice_in_dim(rows, n * self.me, n, axis=2)
        flat = jnp.pad(mine.reshape(N_DEV * N_SEG, n), ((0, MOD_ROWS - N_DEV * N_SEG), (0, 0)))
        self.reduced["w_mod", i] = _mm(self.s_rows, flat, ta=True, name=f"dwmod_{i}")
        ctx_rows = jnp.pad(jnp.sum(mine[:, N_SEG - 1], axis=0, keepdims=True), ((0, 7), (0, 0)))
        self.ctx_pre = self.ctx_pre + _mm(ctx_rows, self.w["w_mod"][i], tb=True, name=f"dcond_{i}")[0]

    def _forms(self, names):
        return [EXCHANGE[n.split("#")[0]][0] for n in names]

    def _partials(self, names, i, from_sibling):
        for name, recv in zip(names, from_sibling):
            self.parts[name, i] = _chip_partials(self.gviews[name, i], recv, self.core, self._forms([name])[0],
                                                 f"partial_{name.replace('#', '_')}_{i}")

    def _finals(self, names, i, from_chips):
        for name, recv in zip(names, from_chips):
            rv = _reduce_final(self.parts[name, i], recv, self.chip, self._forms([name])[0],
                               f"final_{name.replace('#', '_')}_{i}")
            base = name.split("#")[0]
            shape = self._shapes(base, i)[0]
            arr = _unview(base, rv, (1,) + shape[1:] if "#" in name else shape)
            self.reduced[name, i] = arr[:, 0] if name == "mla_g_qa" else arr

    def bwd_slots(self, i):
        if i + 1 >= self.depth:
            return _Slots(lambda slot: None, None)
        plan = BWD_PLAN[i % 2]
        chips = {slot: (self._expand(plan[slot], i + 1), i + 1) for slot in plan}
        sibling = {slot: (self._expand(SIBLING_PLAN[slot], i + 1), i + 1) for slot in SIBLING_PLAN}
        if i == 0:
            sibling["mix_d"] = ([name + "#1" for name in FFN_NAMES], 0)
            chips["f1_dw2"] = (["ffn_w1#1"], 0)
            chips["f1_dnx"] = (["ffn_w3#1", "ffn_w2#1"], 0)

        def get(slot):
            if slot in sibling:
                names, group = sibling[slot]
                return _sibling_work([self.gviews[n, group] for n in names], self._forms(names))
            if slot in chips:
                names, group = chips[slot]
                return _chip_work([self.parts[n, group] for n in names], self._forms(names))
            if slot == DMOD_SLOT[i % 2]:
                return self._dmod_gather(i + 1)
            return None

        def put(slot, outs):
            if slot in sibling:
                self._partials(*sibling[slot], outs)
            elif slot in chips:
                self._finals(*chips[slot], outs)
            else:
                self._dmod_gathered(i + 1, outs)

        return _Slots(get, put)

    def finish(self, rep_grads):
        group = [name + "#0" if name in FFN_NAMES else name for name in self._group(0)]
        self._dmod_gathered(0, _run_hosted(self._dmod_gather(0), "gather_dmod_l0"))
        rep_grads["c_ctx"] = self.ctx_pre
        for name in REPLICATED:
            if name not in rep_grads:
                rep_grads[name] = jnp.stack([self.rep[name, j] for j in range(self.w[name].shape[0])])
        rep = _pack_replicated([rep_grads[name] for name in REPLICATED])
        views = [self.gviews[n, 0] for n in group] + [jnp.tile(rep[None], (1, N_DEV, 1))]
        forms = self._forms(group) + ["mid"]
        from_sibling = _run_hosted(_sibling_work(views, forms), "reduce_sibling_l0")
        self._partials(group, 0, from_sibling[:-1])
        rep_part = _chip_partials(views[-1], from_sibling[-1], self.core, "mid", "partial_replicated")
        parts = [self.parts[n, 0] for n in group] + [rep_part]
        from_chips = _run_hosted(_chip_work(parts, forms), "reduce_chips_l0")
        self._finals(group, 0, from_chips[:-1])
        for name in FFN_NAMES:
            self.reduced[name, 0] = jnp.concatenate([self.reduced[name + "#0", 0], self.reduced[name + "#1", 0]], axis=0)
        rep_sum = _reduce_final(rep_part, from_chips[-1], self.chip, "mid", "final_replicated")
        out = dict(zip(REPLICATED, _unpack_replicated(rep_sum, [self.w[name].shape for name in REPLICATED])))
        sg = jax.nn.sigmoid(self.c_ctx)
        out["c_ctx"] = out["c_ctx"] * (sg * (1.0 + self.c_ctx * (1.0 - sg)))
        for name in EXCHANGE:
            layers = range(self.w[name].shape[0])
            step = 2 if name.startswith(("sc_", "mla_")) else 1
            first = 1 if name.startswith("mla_") else 0
            out[name] = jnp.stack([self.reduced[name, first + step * l] for l in layers])
        return out


def kernel(x, c, ctx, c_ctx, w_mod, b_mod, g_norm, ffn_w1, ffn_w3, ffn_w2, sc_w_in, sc_conv, sc_w_out, mla_w_a, mla_g_qa, mla_w_uq, mla_g_kva, mla_w_ukv, mla_g_q, mla_g_k, mla_w_o, loss_target, m_c_ctx, m_w_mod, m_b_mod, m_g_norm, m_ffn_w1, m_ffn_w3, m_ffn_w2, m_sc_w_in, m_sc_conv, m_sc_w_out, m_mla_w_a, m_mla_g_qa, m_mla_w_uq, m_mla_g_kva, m_mla_w_ukv, m_mla_g_q, m_mla_g_k, m_mla_w_o, v_c_ctx, v_w_mod, v_b_mod, v_g_norm, v_ffn_w1, v_ffn_w3, v_ffn_w2, v_sc_w_in, v_sc_conv, v_sc_w_out, v_mla_w_a, v_mla_g_qa, v_mla_w_uq, v_mla_g_kva, v_mla_w_ukv, v_mla_g_q, v_mla_g_k, v_mla_w_o):
    w = dict(c_ctx=c_ctx, w_mod=w_mod, b_mod=b_mod, g_norm=g_norm, ffn_w1=ffn_w1, ffn_w3=ffn_w3, ffn_w2=ffn_w2,
             sc_w_in=sc_w_in, sc_conv=sc_conv, sc_w_out=sc_w_out, mla_w_a=mla_w_a, mla_g_qa=mla_g_qa, mla_w_uq=mla_w_uq,
             mla_g_kva=mla_g_kva, mla_w_ukv=mla_w_ukv, mla_g_q=mla_g_q, mla_g_k=mla_g_k, mla_w_o=mla_w_o)
    m = dict(c_ctx=m_c_ctx, w_mod=m_w_mod, b_mod=m_b_mod, g_norm=m_g_norm, ffn_w1=m_ffn_w1, ffn_w3=m_ffn_w3,
             ffn_w2=m_ffn_w2, sc_w_in=m_sc_w_in, sc_conv=m_sc_conv, sc_w_out=m_sc_w_out, mla_w_a=m_mla_w_a,
             mla_g_qa=m_mla_g_qa, mla_w_uq=m_mla_w_uq, mla_g_kva=m_mla_g_kva, mla_w_ukv=m_mla_w_ukv, mla_g_q=m_mla_g_q,
             mla_g_k=m_mla_g_k, mla_w_o=m_mla_w_o)
    v = dict(c_ctx=v_c_ctx, w_mod=v_w_mod, b_mod=v_b_mod, g_norm=v_g_norm, ffn_w1=v_ffn_w1, ffn_w3=v_ffn_w3,
             ffn_w2=v_ffn_w2, sc_w_in=v_sc_w_in, sc_conv=v_sc_conv, sc_w_out=v_sc_w_out, mla_w_a=v_mla_w_a,
             mla_g_qa=v_mla_g_qa, mla_w_uq=v_mla_w_uq, mla_g_kva=v_mla_g_kva, mla_w_ukv=v_mla_w_ukv, mla_g_q=v_mla_g_q,
             mla_g_k=v_mla_g_k, mla_w_o=v_mla_w_o)
    exchange = _Exchange(w)
    exchange.prefetch(c)
    loss, grad_x, g_b_mod = _local_step(exchange, x, c, ctx, loss_target)
    loss = lax.psum(loss, ("x", "y", "c"))
    reduced = exchange.finish(dict(b_mod=g_b_mod))

    outs = [[], [], [], []]
    for name in WEIGHTS:
        delta, new_m, new_v = _adamw(w[name], reduced[name], m[name], v[name], f"adamw_{name}")
        for lst, val in zip(outs, (reduced[name], delta, new_m, new_v)):
            lst.append(val)
    return (loss, grad_x, *outs[0], *outs[1], *outs[2], *outs[3])
```

```python
import functools
import math

import jax
import jax.numpy as jnp
import numpy as np
from jax import lax
from jax.experimental import pallas as pl
from jax.experimental.pallas import tpu as pltpu

F32 = jnp.float32
BF16 = jnp.bfloat16

N_MOD = 9
HEADS = 8
QK_NOPE = 128
QK_ROPE = 64
QK_HEAD = QK_NOPE + QK_ROPE
V_HEAD = 128
GRID_W = 64
ROPE_BASE = 10000.0
QK_SCALE = QK_HEAD ** -0.5
EPS = 1e-6
ADAM_LR, ADAM_B1, ADAM_B2, ADAM_EPS, ADAM_WD, ADAM_STEP = 0.001, 0.9, 0.999, 1e-08, 0.01, 10

N_DEV = 8
N_CHIP = 4
N_SEG = 3
LANE = 128
HEAD_PAD = 2 * LANE
VMEM_LIMIT_BYTES = 48 * 1024 * 1024
MESH = pl.DeviceIdType.MESH

WEIGHTS = ["c_ctx", "w_mod", "b_mod", "g_norm", "ffn_w1", "ffn_w3", "ffn_w2", "sc_w_in", "sc_conv", "sc_w_out",
           "mla_w_a", "mla_g_qa", "mla_w_uq", "mla_g_kva", "mla_w_ukv", "mla_g_q", "mla_g_k", "mla_w_o"]
EXCHANGE = {
    "w_mod": ("last", False, True), "ffn_w1": ("mid", True, True), "ffn_w3": ("mid", True, True),
    "ffn_w2": ("mid", False, True), "sc_w_in": ("last", False, True), "sc_w_out": ("mid", False, True),
    "mla_w_a": ("mid", False, True), "mla_w_uq": ("mid", True, True), "mla_w_ukv": ("last", False, True),
    "mla_w_o": ("mid", False, True), "g_norm": ("last", False, False), "sc_conv": ("last", False, False),
    "mla_g_qa": ("mid", False, False),
}
REPLICATED = ["c_ctx", "b_mod", "mla_g_kva", "mla_g_q", "mla_g_k"]


def _pick(n, cands):
    for cand in cands:
        if n % cand == 0:
            return cand
    return n


def _params(*sem):
    return pltpu.CompilerParams(dimension_semantics=sem, vmem_limit_bytes=VMEM_LIMIT_BYTES)


def _hbm():
    return pl.BlockSpec(memory_space=pl.ANY)


class Hosted:
    def __init__(self, inputs, out_shapes, scratch, start, finish):
        self.inputs, self.out_shapes, self.scratch, self.start, self.finish = inputs, out_shapes, scratch, start, finish


def _call(body, hosted, **kw):
    if hosted is None:
        return pl.pallas_call(body, **kw)
    single = not isinstance(kw["out_shape"], (tuple, list))
    out_shape = [kw["out_shape"]] if single else list(kw["out_shape"])
    out_specs = [kw["out_specs"]] if single else list(kw["out_specs"])
    in_specs, scratch, grid = list(kw["in_specs"]), list(kw.get("scratch_shapes", ())), kw["grid"]
    n_in, n_out, n_scr = len(in_specs), len(out_shape), len(scratch)
    h_in, h_out = len(hosted.inputs), len(hosted.out_shapes)

    def wrapped(*refs):
        ins, hins = refs[:n_in], refs[n_in:n_in + h_in]
        o0 = n_in + h_in
        outs, houts = refs[o0:o0 + n_out], refs[o0 + n_out:o0 + n_out + h_out]
        s0 = o0 + n_out + h_out
        scr, hscr = refs[s0:s0 + n_scr], refs[s0 + n_scr:]
        first = functools.reduce(jnp.logical_and, [pl.program_id(a) == 0 for a in range(len(grid))])
        last = functools.reduce(jnp.logical_and, [pl.program_id(a) == g - 1 for a, g in enumerate(grid)])

        @pl.when(first)
        def _():
            hosted.start(hins, houts, hscr)

        body(*ins, *outs, *scr)

        @pl.when(last)
        def _():
            hosted.finish(hins, houts, hscr)

    call = pl.pallas_call(
        wrapped, name=kw["name"], grid=grid, in_specs=in_specs + [_hbm()] * h_in,
        out_specs=tuple(out_specs + [_hbm()] * h_out), out_shape=tuple(out_shape + list(hosted.out_shapes)),
        scratch_shapes=scratch + list(hosted.scratch), input_output_aliases=kw.get("input_output_aliases", {}),
        compiler_params=_params(*["arbitrary"] * len(grid)))

    def run(*args):
        res = call(*args, *hosted.inputs)
        comp = res[:n_out]
        return (comp[0] if single else tuple(comp)), list(res[n_out:])

    return run


def _run_hosted(hosted, name):
    def body(*refs):
        h_in, h_out = len(hosted.inputs), len(hosted.out_shapes)
        hins, houts, hscr = refs[:h_in], refs[h_in:h_in + h_out], refs[h_in + h_out:]
        hosted.start(hins, houts, hscr)
        hosted.finish(hins, houts, hscr)

    return list(pl.pallas_call(
        body, name=name, in_specs=[_hbm()] * len(hosted.inputs), out_specs=tuple([_hbm()] * len(hosted.out_shapes)),
        out_shape=tuple(hosted.out_shapes), scratch_shapes=list(hosted.scratch))(*hosted.inputs))


class Geo:
    def __init__(self, n_lat, n_ctx):
        self.n_lat, self.n_ctx = n_lat, n_ctx
        self.rows = 2 * n_lat + 2 * n_ctx
        self.tile = n_ctx
        assert n_lat % n_ctx == 0 and n_ctx % 16 == 0
        self.mm_tile = _pick(n_lat, (512, 256, 128)) if self.rows % _pick(n_lat, (512, 256, 128)) == 0 else n_ctx
        self.big_tile = _pick(self.rows, (1536, 768, 512, 256))

    def seg(self, i, tile):
        return jnp.minimum((i * tile) // self.n_lat, N_SEG - 1)

    def seg_start(self, i, tile):
        row = i * tile
        return jnp.logical_or(row % self.n_lat == 0, row == 2 * self.n_lat) & (row <= 2 * self.n_lat)


_NT = (((1,), (1,)), ((), ()))
_NN = (((1,), (0,)), ((), ()))
_TN = (((0,), (0,)), ((), ()))


def _dot(a, b, dims):
    return lax.dot_general(a.astype(BF16), b.astype(BF16), dims, preferred_element_type=F32)


def _mm(a, b, *, ta=False, tb=False, out_dtype=F32, name, gate=None, hosted=None):
    (kdim, m) = a.shape if ta else a.shape[::-1]
    n = b.shape[0] if tb else b.shape[1]
    assert (b.shape[1] if tb else b.shape[0]) == kdim
    if gate is not None:
        tm = gate[4].mm_tile
    else:
        tm = _pick(m, (512, 256, 128))
    tn = _pick(n, (512, 256, 128))
    tk = _pick(kdim, (1024, 512, 256, 128))
    nk = kdim // tk
    dims = (((0 if ta else 1,), (1 if tb else 0,)), ((), ()))

    def body(*refs):
        if gate is not None:
            a_ref, b_ref, res_ref, gate_ref, o_ref, y_ref, acc_ref = refs
        else:
            a_ref, b_ref, o_ref, acc_ref = refs
        kk = pl.program_id(2)

        @pl.when(kk == 0)
        def _():
            acc_ref[...] = jnp.zeros_like(acc_ref)

        acc_ref[...] += lax.dot_general(a_ref[...].astype(BF16), b_ref[...].astype(BF16), dims,
                                        preferred_element_type=F32)

        @pl.when(kk == nk - 1)
        def _():
            acc = acc_ref[...]
            if gate is not None:
                y_ref[...] = acc.astype(y_ref.dtype)
                o_ref[...] = res_ref[...] + (gate[3] * gate_ref[...]) * acc
            else:
                o_ref[...] = acc.astype(o_ref.dtype)

    a_spec = pl.BlockSpec((tk, tm), lambda i, j, k: (k, i)) if ta else pl.BlockSpec((tm, tk), lambda i, j, k: (i, k))
    b_spec = pl.BlockSpec((tn, tk), lambda i, j, k: (j, k)) if tb else pl.BlockSpec((tk, tn), lambda i, j, k: (k, j))
    o_spec = pl.BlockSpec((tm, tn), lambda i, j, k: (i, j))
    in_specs, args = [a_spec, b_spec], [a, b]
    out_shape, out_specs = jax.ShapeDtypeStruct((m, n), out_dtype), o_spec
    if gate is not None:
        res, mod4, kmod, _, geo = gate
        in_specs += [o_spec, pl.BlockSpec((None, None, 1, tn), lambda i, j, k: (geo.seg(i, tm), kmod, 0, j))]
        args += [res, mod4]
        out_shape = (jax.ShapeDtypeStruct((m, n), F32), jax.ShapeDtypeStruct((m, n), BF16))
        out_specs = (o_spec, o_spec)
    return _call(
        body, hosted, name=name, grid=(m // tm, n // tn, nk), in_specs=in_specs, out_specs=out_specs,
        out_shape=out_shape, scratch_shapes=[pltpu.VMEM((tm, tn), F32)],
        compiler_params=_params("parallel", "parallel", "arbitrary"),
    )(*args)


def _tn_wide(lhs, rhs, name, into=None, s0=0, hosted=None):
    t, m = lhs.shape
    n = rhs.shape[1]
    tm = _pick(m, (1408, 1024, 512, 256, 128))
    while tm * n * 4 > 6.5 * 1024 * 1024 and tm % 256 == 0:
        tm //= 2
    tk = next(c for c in (1536, 768, 512, 256, 128, t)
              if t % c == 0 and c * (tm + n) * 4 + tm * n * 12 <= 36 * 1024 * 1024)

    def body(l_ref, r_ref, *rest):
        o_ref = rest[-1]
        kk = pl.program_id(1)
        part = lax.dot_general(l_ref[...], r_ref[...], _TN, preferred_element_type=F32)

        @pl.when(kk == 0)
        def _():
            o_ref[...] = part

        @pl.when(kk > 0)
        def _():
            o_ref[...] += part

    in_specs = [pl.BlockSpec((tk, tm), lambda i, k: (k, i)), pl.BlockSpec((tk, n), lambda i, k: (k, 0))]
    if into is None:
        return _call(
            body, hosted, name=name, grid=(m // tm, t // tk), in_specs=in_specs,
            out_specs=pl.BlockSpec((tm, n), lambda i, k: (i, 0)), out_shape=jax.ShapeDtypeStruct((m, n), F32),
            compiler_params=_params("parallel", "arbitrary"),
        )(lhs, rhs)
    return _call(
        body, hosted, name=name, grid=(m // tm, t // tk), in_specs=in_specs + [pl.BlockSpec(memory_space=pl.ANY)],
        out_specs=pl.BlockSpec((None, tm, n), lambda i, k: (s0, i, 0)),
        out_shape=jax.ShapeDtypeStruct(into.shape, into.dtype), input_output_aliases={2: 0},
        compiler_params=_params("parallel", "arbitrary"),
    )(lhs, rhs, into)


def _mod_spec(geo, tile, kmod, d):
    return pl.BlockSpec((None, None, 1, d), lambda i: (geo.seg(i, tile), kmod, 0, 0))


def _pre_fwd(h, g, mod4, k_shift, geo, name):
    t, d = h.shape
    tile = geo.tile

    def body(h_ref, g_ref, sh_ref, sc_ref, o_ref):
        hv = h_ref[...]
        r = lax.rsqrt(jnp.mean(hv * hv, axis=-1, keepdims=True) + EPS)
        y = hv * r * g_ref[...]
        o_ref[...] = (y * (1.0 + sc_ref[...]) + sh_ref[...]).astype(o_ref.dtype)

    row = pl.BlockSpec((tile, d), lambda i: (i, 0))
    return pl.pallas_call(
        body, name=name, grid=(t // tile,),
        in_specs=[row, pl.BlockSpec((1, d), lambda i: (0, 0)), _mod_spec(geo, tile, k_shift, d),
                  _mod_spec(geo, tile, k_shift + 1, d)],
        out_specs=row, out_shape=jax.ShapeDtypeStruct((t, d), BF16), compiler_params=_params("parallel"),
    )(h, g, mod4, mod4)


def _pre_bwd(h, g, mod4, k_shift, dnx, dres, geo, name):
    t, d = h.shape
    tile = geo.tile

    def body(h_ref, g_ref, sc_ref, dnx_ref, dres_ref, dh_ref, dg_ref, dsh_ref, dsc_ref):
        i = pl.program_id(0)
        hv, gv, dout = h_ref[...], g_ref[...], dnx_ref[...].astype(F32)
        r = lax.rsqrt(jnp.mean(hv * hv, axis=-1, keepdims=True) + EPS)
        xhat = hv * r
        dy = dout * (1.0 + sc_ref[...])
        u = dy * gv
        dh_ref[...] = r * (u - xhat * jnp.mean(u * xhat, axis=-1, keepdims=True)) + dres_ref[...]

        @pl.when(i == 0)
        def _():
            dg_ref[...] = jnp.zeros_like(dg_ref)

        @pl.when(geo.seg_start(i, tile))
        def _():
            dsh_ref[...] = jnp.zeros_like(dsh_ref)
            dsc_ref[...] = jnp.zeros_like(dsc_ref)

        dg_ref[...] += jnp.sum(dy * xhat, axis=0, keepdims=True)
        dsh_ref[...] += jnp.sum(dout, axis=0, keepdims=True)
        dsc_ref[...] += jnp.sum(dout * (xhat * gv), axis=0, keepdims=True)

    row = pl.BlockSpec((tile, d), lambda i: (i, 0))
    vec = pl.BlockSpec((1, d), lambda i: (0, 0))
    segv = pl.BlockSpec((None, 1, d), lambda i: (geo.seg(i, tile), 0, 0))
    return pl.pallas_call(
        body, name=name, grid=(t // tile,),
        in_specs=[row, vec, _mod_spec(geo, tile, k_shift + 1, d), row, row],
        out_specs=(row, vec, segv, segv),
        out_shape=(jax.ShapeDtypeStruct((t, d), F32), jax.ShapeDtypeStruct((1, d), F32),
                   jax.ShapeDtypeStruct((N_SEG, 1, d), F32), jax.ShapeDtypeStruct((N_SEG, 1, d), F32)),
        compiler_params=_params("arbitrary"),
    )(h, g, mod4, dnx, dres)


def _gate_bwd(dh, y, mod4, k_gate, coef, geo, name):
    t, d = dh.shape
    tile = geo.tile

    def body(dh_ref, y_ref, gt_ref, dy_ref, dgt_ref):
        i = pl.program_id(0)
        dhv = dh_ref[...]
        dy_ref[...] = ((coef * gt_ref[...]) * dhv).astype(dy_ref.dtype)

        @pl.when(geo.seg_start(i, tile))
        def _():
            dgt_ref[...] = jnp.zeros_like(dgt_ref)

        dgt_ref[...] += coef * jnp.sum(dhv * y_ref[...].astype(F32), axis=0, keepdims=True)

    row = pl.BlockSpec((tile, d), lambda i: (i, 0))
    segv = pl.BlockSpec((None, 1, d), lambda i: (geo.seg(i, tile), 0, 0))
    return pl.pallas_call(
        body, name=name, grid=(t // tile,), in_specs=[row, row, _mod_spec(geo, tile, k_gate, d)],
        out_specs=(row, segv),
        out_shape=(jax.ShapeDtypeStruct((t, d), BF16), jax.ShapeDtypeStruct((N_SEG, 1, d), F32)),
        compiler_params=_params("arbitrary"),
    )(dh, y, mod4)


def _ff_tile(f):
    return _pick(f, (256, 128))


def _ffn_up(nx, w1t, w3t, s0, geo, name, hosted=None):
    t, d = nx.shape
    f = w1t.shape[1]
    tm, tn = geo.big_tile, _ff_tile(f)

    def body(x_ref, w1_ref, w3_ref, ga_ref, gb_ref, act_ref):
        xv = x_ref[...]
        a = lax.dot_general(xv, w1_ref[...], _NT, preferred_element_type=F32)
        bv = lax.dot_general(xv, w3_ref[...], _NT, preferred_element_type=F32)
        sg = jax.nn.sigmoid(a)
        silu = a * sg
        ga_ref[...] = (bv * (sg + silu * (1.0 - sg))).astype(ga_ref.dtype)
        gb_ref[...] = silu.astype(gb_ref.dtype)
        act_ref[...] = (silu * bv).astype(act_ref.dtype)

    w_spec = pl.BlockSpec((None, tn, d), lambda i, j: (s0, j, 0))
    o_spec = pl.BlockSpec((tm, tn), lambda i, j: (i, j))
    sds = jax.ShapeDtypeStruct((t, f), BF16)
    return _call(
        body, hosted, name=name, grid=(t // tm, f // tn),
        in_specs=[pl.BlockSpec((tm, d), lambda i, j: (i, 0)), w_spec, w_spec],
        out_specs=(o_spec,) * 3, out_shape=(sds,) * 3, compiler_params=_params("parallel", "parallel"),
    )(nx, w1t, w3t)


def _ffn_down(act, w2, s0, res, mod4, k_gate, geo, name, hosted=None):
    t, f = act.shape
    d = w2.shape[2]
    tm, tn = geo.mm_tile, _pick(d, (1024, 512, 256, 128))

    def body(a_ref, w_ref, res_ref, gate_ref, o_ref, y_ref):
        acc = lax.dot_general(a_ref[...], w_ref[...], _NN, preferred_element_type=F32)
        y_ref[...] = acc.astype(y_ref.dtype)
        o_ref[...] = res_ref[...] + (0.5 * gate_ref[...]) * acc

    o_spec = pl.BlockSpec((tm, tn), lambda i, j: (i, j))
    return _call(
        body, hosted, name=name, grid=(t // tm, d // tn),
        in_specs=[pl.BlockSpec((tm, f), lambda i, j: (i, 0)), pl.BlockSpec((None, f, tn), lambda i, j: (s0, 0, j)),
                  o_spec, pl.BlockSpec((None, None, 1, tn), lambda i, j: (geo.seg(i, tm), k_gate, 0, j))],
        out_specs=(o_spec, o_spec),
        out_shape=(jax.ShapeDtypeStruct((t, d), F32), jax.ShapeDtypeStruct((t, d), BF16)),
        compiler_params=_params("parallel", "parallel"),
    )(act, w2, res, mod4)


def _ffn_dact(dy, w2, ga, gb, s0, geo, name, hosted=None):
    t, d = dy.shape
    f = w2.shape[1]
    tm, tn = geo.big_tile, _ff_tile(f)

    def body(dy_ref, w_ref, ga_ref, gb_ref, da_ref, db_ref):
        dact = lax.dot_general(dy_ref[...], w_ref[...], _NT, preferred_element_type=F32)
        da_ref[...] = (dact * ga_ref[...].astype(F32)).astype(da_ref.dtype)
        db_ref[...] = (dact * gb_ref[...].astype(F32)).astype(db_ref.dtype)

    o_spec = pl.BlockSpec((tm, tn), lambda i, j: (i, j))
    sds = jax.ShapeDtypeStruct((t, f), BF16)
    return _call(
        body, hosted, name=name, grid=(t // tm, f // tn),
        in_specs=[pl.BlockSpec((tm, d), lambda i, j: (i, 0)), pl.BlockSpec((None, tn, d), lambda i, j: (s0, j, 0)),
                  o_spec, o_spec],
        out_specs=(o_spec, o_spec), out_shape=(sds, sds), compiler_params=_params("parallel", "parallel"),
    )(dy, w2, ga, gb)


def _ffn_dnx(da, db, w1t, w3t, s0, geo, name, hosted=None):
    t, f = da.shape
    d = w1t.shape[2]
    tm, tn = geo.mm_tile, _pick(d, (512, 256, 128))

    def body(da_ref, db_ref, w1_ref, w3_ref, o_ref):
        o_ref[...] = (lax.dot_general(da_ref[...], w1_ref[...], _NN, preferred_element_type=F32)
                      + lax.dot_general(db_ref[...], w3_ref[...], _NN, preferred_element_type=F32))

    x_spec = pl.BlockSpec((tm, f), lambda j, i: (i, 0))
    w_spec = pl.BlockSpec((None, f, tn), lambda j, i: (s0, 0, j))
    return _call(
        body, hosted, name=name, grid=(d // tn, t // tm), in_specs=[x_spec, x_spec, w_spec, w_spec],
        out_specs=pl.BlockSpec((tm, tn), lambda j, i: (i, j)), out_shape=jax.ShapeDtypeStruct((t, d), F32),
        compiler_params=_params("parallel", "parallel"),
    )(da, db, w1t, w3t)


def _with_host(fn, hosts, got, slot, *args, **kw):
    hosted = hosts.get(slot)
    if hosted is None:
        return fn(*args, **kw)
    out, got[slot] = fn(*args, hosted=hosted, **kw)
    return out


def _ffn_fwd(h, g, mod4, k0, w, s0, geo, tag, sub, hosts, got):
    nx = _pre_fwd(h, g, mod4, k0, geo, f"{tag}_pre")
    a, b, act = _with_host(_ffn_up, hosts, got, f"{sub}_up", nx, w["ffn_w1"], w["ffn_w3"], s0, geo, f"{tag}_up")
    h_out, y = _with_host(_ffn_down, hosts, got, f"{sub}_down", act, w["ffn_w2"], s0, h, mod4, k0 + 2, geo, f"{tag}_down")
    return h_out, (h, nx, a, b, act, y)


def _ffn_bwd(dh_out, saved, g, mod4, k0, w, s0, gbuf, geo, tag, sub, hosts, got):
    h, nx, a, b, act, y = saved
    dy, dgate = _gate_bwd(dh_out, y, mod4, k0 + 2, 0.5, geo, f"{tag}_dgate")
    da, db = _with_host(_ffn_dact, hosts, got, f"{sub}_dact", dy, w["ffn_w2"], a, b, s0, geo, f"{tag}_dact")
    gbuf["ffn_w2"] = _with_host(_tn_wide, hosts, got, f"{sub}_dw2", act, dy, f"{tag}_dw2", into=gbuf["ffn_w2"], s0=s0)
    dnx = _with_host(_ffn_dnx, hosts, got, f"{sub}_dnx", da, db, w["ffn_w1"], w["ffn_w3"], s0, geo, f"{tag}_dnx")
    gbuf["ffn_w1"] = _tn_wide(da, nx, f"{tag}_dw1", into=gbuf["ffn_w1"], s0=s0)
    gbuf["ffn_w3"] = _tn_wide(db, nx, f"{tag}_dw3", into=gbuf["ffn_w3"], s0=s0)
    dh, dg, dshift, dscale = _pre_bwd(h, g, mod4, k0, dnx, dh_out, geo, f"{tag}_dpre")
    return dh, dg, (dshift, dscale, dgate)


def _interleave(w, n_parts, tile):
    lead, cols = w.shape[:-1], w.shape[-1] // n_parts
    return w.reshape(*lead, n_parts, cols // tile, tile).swapaxes(-3, -2).reshape(*lead, n_parts * cols)


def _deinterleave(w, n_parts, tile):
    lead, cols = w.shape[:-1], w.shape[-1] // n_parts
    return w.reshape(*lead, cols // tile, n_parts, tile).swapaxes(-3, -2).reshape(*lead, n_parts * cols)


HALO = 16


def _conv_tile(c):
    return _pick(c, (256, 128))


def _conv_specs(geo, tc, t):
    tile = geo.tile
    per = tile // HALO
    last = t // HALO - 1
    cur = pl.BlockSpec((tile, 3 * tc), lambda j, i: (i, j))
    prev = pl.BlockSpec((HALO, 3 * tc), lambda j, i: (jnp.maximum(i * per - 1, 0), j))
    nxt = pl.BlockSpec((HALO, 3 * tc), lambda j, i: (jnp.minimum((i + 1) * per, last), j))
    return cur, prev, nxt


def _conv_edges(geo, i):
    tile = geo.tile
    row = i * tile
    lat = row < 2 * geo.n_lat
    first = jnp.where(lat, row % geo.n_lat == 0, (row - 2 * geo.n_lat) % geo.n_ctx == 0)
    end = row + tile
    last = jnp.where(lat, end % geo.n_lat == 0, (end - 2 * geo.n_lat) % geo.n_ctx == 0)
    return first, last


def _shift_rows(v, before, after):
    n = v.shape[0]
    rows = lax.broadcasted_iota(jnp.int32, v.shape, 0)
    down = jnp.where(rows == 0, before, pltpu.roll(v, 1, 0))
    up = jnp.where(rows == n - 1, after, pltpu.roll(v, n - 1, 0))
    return down, up


def _conv_fwd(proj, conv_w, geo, name, hosted=None):
    t, c3 = proj.shape
    c = c3 // 3
    tc, tile = _conv_tile(c), geo.tile

    def body(cur_ref, prev_ref, next_ref, w_ref, o_ref):
        first, last = _conv_edges(geo, pl.program_id(1))
        bv = cur_ref[:, :tc].astype(F32)
        p = cur_ref[:, tc:2 * tc].astype(F32) * cur_ref[:, 2 * tc:].astype(F32)
        p_before = prev_ref[HALO - 1:HALO, tc:2 * tc].astype(F32) * prev_ref[HALO - 1:HALO, 2 * tc:].astype(F32)
        p_after = next_ref[0:1, tc:2 * tc].astype(F32) * next_ref[0:1, 2 * tc:].astype(F32)
        p_before = jnp.where(first, 0.0, p_before)
        p_after = jnp.where(last, 0.0, p_after)
        pm1, pp1 = _shift_rows(p, p_before, p_after)
        w = w_ref[...]
        q = w[0:1] * pm1 + w[1:2] * p + w[2:3] * pp1
        o_ref[...] = (bv * q).astype(o_ref.dtype)

    cur, prev, nxt = _conv_specs(geo, tc, t)
    return _call(
        body, hosted, name=name, grid=(c // tc, t // tile),
        in_specs=[cur, prev, nxt, pl.BlockSpec((3, tc), lambda j, i: (0, j))],
        out_specs=pl.BlockSpec((tile, tc), lambda j, i: (i, j)), out_shape=jax.ShapeDtypeStruct((t, c), BF16),
        compiler_params=_params("parallel", "parallel"),
    )(proj, proj, proj, conv_w)


def _conv_bwd(proj, dyc, conv_w, geo, name, hosted=None):
    t, c3 = proj.shape
    c = c3 // 3
    tc, tile = _conv_tile(c), geo.tile

    def body(cur_ref, prev_ref, next_ref, d_ref, dprev_ref, dnext_ref, w_ref, o_ref, dw_ref):
        i = pl.program_id(1)
        first, last = _conv_edges(geo, i)
        bv = cur_ref[:, :tc].astype(F32)
        cv = cur_ref[:, tc:2 * tc].astype(F32)
        uv = cur_ref[:, 2 * tc:].astype(F32)
        p = cv * uv
        p_before = prev_ref[HALO - 1:HALO, tc:2 * tc].astype(F32) * prev_ref[HALO - 1:HALO, 2 * tc:].astype(F32)
        p_after = next_ref[0:1, tc:2 * tc].astype(F32) * next_ref[0:1, 2 * tc:].astype(F32)
        p_before = jnp.where(first, 0.0, p_before)
        p_after = jnp.where(last, 0.0, p_after)
        pm1, pp1 = _shift_rows(p, p_before, p_after)
        w = w_ref[...]
        q = w[0:1] * pm1 + w[1:2] * p + w[2:3] * pp1
        dy = d_ref[...].astype(F32)
        dq = dy * bv
        dq_before = dprev_ref[HALO - 1:HALO, :].astype(F32) * prev_ref[HALO - 1:HALO, :tc].astype(F32)
        dq_after = dnext_ref[0:1, :].astype(F32) * next_ref[0:1, :tc].astype(F32)
        dq_before = jnp.where(first, 0.0, dq_before)
        dq_after = jnp.where(last, 0.0, dq_after)
        dqm1, dqp1 = _shift_rows(dq, dq_before, dq_after)
        dp = w[0:1] * dqp1 + w[1:2] * dq + w[2:3] * dqm1
        o_ref[:, :tc] = (dy * q).astype(o_ref.dtype)
        o_ref[:, tc:2 * tc] = (dp * uv).astype(o_ref.dtype)
        o_ref[:, 2 * tc:] = (dp * cv).astype(o_ref.dtype)

        @pl.when(i == 0)
        def _():
            dw_ref[...] = jnp.zeros_like(dw_ref)

        dw_ref[0:1, :] += jnp.sum(dq * pm1, axis=0, keepdims=True)
        dw_ref[1:2, :] += jnp.sum(dq * p, axis=0, keepdims=True)
        dw_ref[2:3, :] += jnp.sum(dq * pp1, axis=0, keepdims=True)

    cur, prev, nxt = _conv_specs(geo, tc, t)
    per, lastb = tile // HALO, t // HALO - 1
    dcur = pl.BlockSpec((tile, tc), lambda j, i: (i, j))
    dprev = pl.BlockSpec((HALO, tc), lambda j, i: (jnp.maximum(i * per - 1, 0), j))
    dnext = pl.BlockSpec((HALO, tc), lambda j, i: (jnp.minimum((i + 1) * per, lastb), j))
    wspec = pl.BlockSpec((3, tc), lambda j, i: (0, j))
    return _call(
        body, hosted, name=name, grid=(c // tc, t // tile), in_specs=[cur, prev, nxt, dcur, dprev, dnext, wspec],
        out_specs=(cur, wspec), out_shape=(jax.ShapeDtypeStruct((t, c3), BF16), jax.ShapeDtypeStruct((3, c), F32)),
        compiler_params=_params("parallel", "arbitrary"),
    )(proj, proj, proj, dyc, dyc, dyc, conv_w)


def _sconv_fwd(h, g, mod4, w_in, conv_w, w_out, geo, tag, hosts, got):
    nx = _pre_fwd(h, g, mod4, 3, geo, f"{tag}_pre")
    proj = _with_host(_mm, hosts, got, "mix_a", nx, w_in, out_dtype=BF16, name=f"{tag}_in")
    yc = _with_host(_conv_fwd, hosts, got, "mix_b", proj, conv_w, geo, f"{tag}_conv")
    h_out, y = _mm(yc, w_out, name=f"{tag}_out", gate=(h, mod4, 5, 1.0, geo))
    return h_out, (h, nx, proj, yc, y)


def _sconv_bwd(dh_out, saved, g, mod4, w_in, conv_w, w_out, geo, tag, hosts, got):
    h, nx, proj, yc, y = saved
    dy, dgate = _gate_bwd(dh_out, y, mod4, 5, 1.0, geo, f"{tag}_dgate")
    dyc = _with_host(_mm, hosts, got, "mix_d", dy, w_out, tb=True, out_dtype=BF16, name=f"{tag}_dyc")
    dw_out = _tn_wide(yc, dy, f"{tag}_dwout")
    dproj, dconv = _with_host(_conv_bwd, hosts, got, "mix_c", proj, dyc, conv_w, geo, f"{tag}_dconv")
    dnx = _with_host(_mm, hosts, got, "mix_b", dproj, w_in, tb=True, name=f"{tag}_dnx")
    dw_in = _with_host(_tn_wide, hosts, got, "mix_a", nx, dproj, f"{tag}_dwin")
    dh, dg, dshift, dscale = _pre_bwd(h, g, mod4, 3, dnx, dh_out, geo, f"{tag}_dpre")
    return dh, dg, (dshift, dscale, dgate), dw_in, dconv, dw_out


def _rope_swap(v):
    nf = QK_ROPE // 4
    return v.reshape(v.shape[:-1] + (2, 2, nf)).swapaxes(-3, -2).reshape(v.shape)


def _rope_tables(geo):
    n = geo.n_lat
    nf = QK_ROPE // 4
    pos = np.arange(n)
    inv = ROPE_BASE ** (-np.arange(nf, dtype=np.float32) / nf)
    ang = np.concatenate([(pos // GRID_W)[:, None] * inv, (pos % GRID_W)[:, None] * inv], axis=1).astype(np.float32)
    cos, sin = np.cos(ang), np.sin(ang)
    zeros = np.zeros((n, LANE - QK_ROPE), np.float32)
    c_lat = np.concatenate([cos, cos, zeros], axis=1)
    s_lat = np.concatenate([-sin, sin, zeros], axis=1)
    c_ctx = np.concatenate([np.ones((2 * geo.n_ctx, QK_ROPE), np.float32), np.zeros((2 * geo.n_ctx, LANE - QK_ROPE), np.float32)], 1)
    s_ctx = np.zeros((2 * geo.n_ctx, LANE), np.float32)
    return (jnp.asarray(np.concatenate([c_lat, c_lat, c_ctx], 0)), jnp.asarray(np.concatenate([s_lat, s_lat, s_ctx], 0)))


def _swap_halves(v):
    lanes = lax.broadcasted_iota(jnp.int32, v.shape, 1)
    return jnp.where(lanes < QK_ROPE // 2, pltpu.roll(v, LANE - QK_ROPE // 2, 1), pltpu.roll(v, QK_ROPE // 2, 1))


def _latent_norm_fwd(down, g_qa, g_kva, geo, name):
    t, wd = down.shape
    ql, kl = g_qa.shape[1], g_kva.shape[1]
    tile = geo.tile

    def body(d_ref, gq_ref, gk_ref, cq_ref, ckv_ref):
        for lo, n, g_ref, o_ref in ((0, ql, gq_ref, cq_ref), (ql, kl, gk_ref, ckv_ref)):
            x = d_ref[:, lo:lo + n]
            r = lax.rsqrt(jnp.mean(x * x, axis=-1, keepdims=True) + EPS)
            o_ref[...] = (x * r * g_ref[...]).astype(o_ref.dtype)

    return pl.pallas_call(
        body, name=name, grid=(t // tile,),
        in_specs=[pl.BlockSpec((tile, wd), lambda i: (i, 0)), pl.BlockSpec((1, ql), lambda i: (0, 0)),
                  pl.BlockSpec((1, kl), lambda i: (0, 0))],
        out_specs=(pl.BlockSpec((tile, ql), lambda i: (i, 0)), pl.BlockSpec((tile, kl), lambda i: (i, 0))),
        out_shape=(jax.ShapeDtypeStruct((t, ql), BF16), jax.ShapeDtypeStruct((t, kl), BF16)),
        compiler_params=_params("parallel"),
    )(down, g_qa, g_kva)


def _latent_norm_bwd(down, g_qa, g_kva, dcqn, dckvn, dkr, geo, name):
    t, wd = down.shape
    ql, kl = g_qa.shape[1], g_kva.shape[1]
    tile = geo.tile

    def body(d_ref, gq_ref, gk_ref, dq_ref, dk_ref, dkr_ref, o_ref, dgq_ref, dgk_ref):
        i = pl.program_id(0)

        @pl.when(i == 0)
        def _():
            dgq_ref[...] = jnp.zeros_like(dgq_ref)
            dgk_ref[...] = jnp.zeros_like(dgk_ref)

        for lo, n, g_ref, dy_ref, dg_ref in ((0, ql, gq_ref, dq_ref, dgq_ref), (ql, kl, gk_ref, dk_ref, dgk_ref)):
            x = d_ref[:, lo:lo + n]
            dy = dy_ref[...].astype(F32)
            r = lax.rsqrt(jnp.mean(x * x, axis=-1, keepdims=True) + EPS)
            xhat = x * r
            u = dy * g_ref[...]
            o_ref[:, lo:lo + n] = (r * (u - xhat * jnp.mean(u * xhat, axis=-1, keepdims=True))).astype(o_ref.dtype)
            dg_ref[...] += jnp.sum(dy * xhat, axis=0, keepdims=True)
        o_ref[:, ql + kl:] = dkr_ref[...].astype(o_ref.dtype)

    def row(n):
        return pl.BlockSpec((tile, n), lambda i: (i, 0))

    def vec(n):
        return pl.BlockSpec((1, n), lambda i: (0, 0))

    return pl.pallas_call(
        body, name=name, grid=(t // tile,),
        in_specs=[row(wd), vec(ql), vec(kl), row(ql), row(kl), row(wd - ql - kl)],
        out_specs=(row(wd), vec(ql), vec(kl)),
        out_shape=(jax.ShapeDtypeStruct((t, wd), BF16), jax.ShapeDtypeStruct((1, ql), F32),
                   jax.ShapeDtypeStruct((1, kl), F32)),
        compiler_params=_params("arbitrary"),
    )(down, g_qa, g_kva, dcqn, dckvn, dkr)


def _qk_specs(geo, xr_col, shared_rope):
    tile = geo.mm_tile
    xn_spec = pl.BlockSpec((tile, HEADS * LANE), lambda i: (i, 0))
    if shared_rope:
        xr_spec = pl.BlockSpec((tile, LANE), lambda i: (i, xr_col))
    else:
        xr_spec = pl.BlockSpec((tile, HEADS * LANE), lambda i: (i, xr_col // HEADS))
    vec = pl.BlockSpec((1, LANE), lambda i: (0, 0))
    tab = pl.BlockSpec((tile, LANE), lambda i: (i, 0))
    return tile, xn_spec, xr_spec, vec, tab


def _qk_norm(xn, xr):
    ss = jnp.sum(xn * xn, axis=-1, keepdims=True) + jnp.sum(xr * xr, axis=-1, keepdims=True)
    return lax.rsqrt(ss * (1.0 / QK_HEAD) + EPS)


def _head_lanes(ref, hh, shared=False):
    return ref[...] if shared else ref[:, hh * LANE:(hh + 1) * LANE]


def _qk_fwd(xn_arr, xr_arr, xr_col, shared_rope, gn, gr, cos, sin, geo, name):
    t = xn_arr.shape[0]
    tile, xn_spec, xr_spec, vec, tab = _qk_specs(geo, xr_col, shared_rope)

    def body(xn_ref, xr_ref, gn_ref, gr_ref, c_ref, s_ref, o_ref):
        cv, sv, gnv, grv = c_ref[...], s_ref[...], gn_ref[...], gr_ref[...]
        for hh in range(HEADS):
            xn = _head_lanes(xn_ref, hh).astype(F32)
            xr = _head_lanes(xr_ref, hh, shared_rope).astype(F32)
            r = _qk_norm(xn, xr)
            yr = xr * r * grv
            o_ref[:, hh * HEAD_PAD:hh * HEAD_PAD + LANE] = (xn * r * gnv).astype(o_ref.dtype)
            o_ref[:, hh * HEAD_PAD + LANE:(hh + 1) * HEAD_PAD] = (yr * cv + _swap_halves(yr) * sv).astype(o_ref.dtype)

    return pl.pallas_call(
        body, name=name, grid=(t // tile,), in_specs=[xn_spec, xr_spec, vec, vec, tab, tab],
        out_specs=pl.BlockSpec((tile, HEADS * HEAD_PAD), lambda i: (i, 0)),
        out_shape=jax.ShapeDtypeStruct((t, HEADS * HEAD_PAD), BF16), compiler_params=_params("parallel"),
    )(xn_arr, xr_arr, gn, gr, cos, sin)


def _qk_bwd(xn_arr, xr_arr, xr_col, shared_rope, gn, gr, cos, sin, dout, geo, name):
    t = xn_arr.shape[0]
    tile, xn_spec, xr_spec, vec, tab = _qk_specs(geo, xr_col, shared_rope)
    half = HEADS * LANE
    if shared_rope:
        n_lat_tiles = dout[0].shape[0] // tile
        assert dout[0].shape[0] % tile == 0 and dout[1].shape[0] % tile == 0

    def body(*refs):
        if shared_rope:
            xn_ref, xr_ref, gn_ref, gr_ref, c_ref, s_ref, dl_ref, dc_ref, vl_ref, vc_ref, raw_ref, dxr_ref, dgn_ref, dgr_ref = refs
        else:
            xn_ref, xr_ref, gn_ref, gr_ref, c_ref, s_ref, d_ref, raw_ref, dgn_ref, dgr_ref = refs
        i = pl.program_id(0)
        cv, sv, gnv, grv = c_ref[...], s_ref[...], gn_ref[...], gr_ref[...]
        dgn = jnp.zeros((1, LANE), F32)
        dgr = jnp.zeros((1, LANE), F32)
        dxr_sum = jnp.zeros((tile, LANE), F32)
        if shared_rope:
            latent = i < n_lat_tiles
            raw_ref[:, half:] = jnp.where(latent, vl_ref[...], vc_ref[...])
        for hh in range(HEADS):
            xn = _head_lanes(xn_ref, hh).astype(F32)
            xr = _head_lanes(xr_ref, hh, shared_rope).astype(F32)
            r = _qk_norm(xn, xr)
            xhn, xhr = xn * r, xr * r
            lo = hh * HEAD_PAD
            if shared_rope:
                dhead = jnp.where(latent, dl_ref[:, lo:lo + HEAD_PAD], dc_ref[:, lo:lo + HEAD_PAD]).astype(F32)
            else:
                dhead = d_ref[:, lo:lo + HEAD_PAD].astype(F32)
            dyn, dro = dhead[:, :LANE], dhead[:, LANE:]
            dyr = dro * cv + _swap_halves(dro * sv)
            un, ur = dyn * gnv, dyr * grv
            mean = (jnp.sum(un * xhn, axis=-1, keepdims=True) + jnp.sum(ur * xhr, axis=-1, keepdims=True)) * (1.0 / QK_HEAD)
            raw_ref[:, hh * LANE:(hh + 1) * LANE] = (r * (un - xhn * mean)).astype(raw_ref.dtype)
            dxr = r * (ur - xhr * mean)
            if shared_rope:
                dxr_sum = dxr_sum + dxr
            else:
                raw_ref[:, half + hh * LANE:half + (hh + 1) * LANE] = dxr.astype(raw_ref.dtype)
            dgn = dgn + jnp.sum(dyn * xhn, axis=0, keepdims=True)
            dgr = dgr + jnp.sum(dyr * xhr, axis=0, keepdims=True)
        if shared_rope:
            dxr_ref[...] = dxr_sum

        @pl.when(i == 0)
        def _():
            dgn_ref[...] = jnp.zeros_like(dgn_ref)
            dgr_ref[...] = jnp.zeros_like(dgr_ref)

        dgn_ref[...] += dgn
        dgr_ref[...] += dgr

    raw_spec = pl.BlockSpec((tile, 2 * half), lambda i: (i, 0))
    raw_shape = jax.ShapeDtypeStruct((t, 2 * half), BF16)
    vec_shape = jax.ShapeDtypeStruct((1, LANE), F32)
    in_specs = [xn_spec, xr_spec, vec, vec, tab, tab]
    if shared_rope:
        def two(width):
            return [pl.BlockSpec((tile, width), lambda i: (jnp.minimum(i, n_lat_tiles - 1), 0)),
                    pl.BlockSpec((tile, width), lambda i: (jnp.maximum(i - n_lat_tiles, 0), 0))]

        return pl.pallas_call(
            body, name=name, grid=(t // tile,), in_specs=in_specs + two(HEADS * HEAD_PAD) + two(half),
            out_specs=(raw_spec, pl.BlockSpec((tile, LANE), lambda i: (i, 0)), vec, vec),
            out_shape=(raw_shape, jax.ShapeDtypeStruct((t, LANE), F32), vec_shape, vec_shape),
            compiler_params=_params("arbitrary"),
        )(xn_arr, xr_arr, gn, gr, cos, sin, *dout)
    return pl.pallas_call(
        body, name=name, grid=(t // tile,),
        in_specs=in_specs + [pl.BlockSpec((tile, HEADS * HEAD_PAD), lambda i: (i, 0))],
        out_specs=(raw_spec, vec, vec), out_shape=(raw_shape, vec_shape, vec_shape),
        compiler_params=_params("arbitrary"),
    )(xn_arr, xr_arr, gn, gr, cos, sin, dout)


def _attn_specs(geo):
    tq, nq = geo.n_ctx, geo.n_lat // geo.n_ctx

    def qrow(b, i):
        return jnp.where(i < nq, b * nq + i, 2 * nq + b)

    q_spec = pl.BlockSpec((tq, HEAD_PAD), lambda b, hh, i: (qrow(b, i), hh))
    kc_spec = pl.BlockSpec((geo.n_ctx, HEAD_PAD), lambda b, hh, i: (2 * nq + b, hh))
    kl_spec = pl.BlockSpec((geo.n_lat, HEAD_PAD), lambda b, hh, i: (b, hh))
    vc_spec = pl.BlockSpec((geo.n_ctx, V_HEAD), lambda b, hh, i: (2 * nq + b, HEADS + hh))
    vl_spec = pl.BlockSpec((geo.n_lat, V_HEAD), lambda b, hh, i: (b, HEADS + hh))
    o_spec = pl.BlockSpec((tq, V_HEAD), lambda b, hh, i: (qrow(b, i), hh))
    return tq, nq, q_spec, kc_spec, kl_spec, vc_spec, vl_spec, o_spec


def _attn_fwd(q, k, kv, with_ctx_q, geo, name, hosted=None):
    t = q.shape[0]
    tq, nq, q_spec, kc_spec, kl_spec, vc_spec, vl_spec, o_spec = _attn_specs(geo)

    def body(q_ref, kc_ref, kl_ref, vc_ref, vl_ref, o_ref):
        i = pl.program_id(2)
        qv = q_ref[...]
        s_c = _dot(qv, kc_ref[...], _NT) * QK_SCALE

        @pl.when(i < nq)
        def _():
            s_l = _dot(qv, kl_ref[...], _NT) * QK_SCALE
            m = jnp.maximum(jnp.max(s_c, axis=-1, keepdims=True), jnp.max(s_l, axis=-1, keepdims=True))
            p_c, p_l = jnp.exp(s_c - m), jnp.exp(s_l - m)
            den = jnp.sum(p_c, axis=-1, keepdims=True) + jnp.sum(p_l, axis=-1, keepdims=True)
            o = _dot(p_c, vc_ref[...], _NN) + _dot(p_l, vl_ref[...], _NN)
            o_ref[...] = (o / den).astype(o_ref.dtype)

        @pl.when(i == nq)
        def _():
            if with_ctx_q:
                m = jnp.max(s_c, axis=-1, keepdims=True)
                p_c = jnp.exp(s_c - m)
                o = _dot(p_c, vc_ref[...], _NN) / jnp.sum(p_c, axis=-1, keepdims=True)
                o_ref[...] = o.astype(o_ref.dtype)
            else:
                o_ref[...] = jnp.zeros_like(o_ref)

    return _call(
        body, hosted, name=name, grid=(2, HEADS, nq + 1), in_specs=[q_spec, kc_spec, kl_spec, vc_spec, vl_spec],
        out_specs=o_spec, out_shape=jax.ShapeDtypeStruct((t, HEADS * V_HEAD), BF16),
        compiler_params=_params("parallel", "parallel", "arbitrary"),
    )(q, k, k, kv, kv)


def _attn_bwd(q, k, kv, do, with_ctx_q, geo, name, hosted=None):
    t = q.shape[0]
    tq, nq, q_spec, kc_spec, kl_spec, vc_spec, vl_spec, o_spec = _attn_specs(geo)

    def body(q_ref, kc_ref, kl_ref, vc_ref, vl_ref, do_ref, dq_ref, dkl_ref, dkc_ref, dvl_ref, dvc_ref,
             akl_ref, akc_ref, avl_ref, avc_ref):
        i = pl.program_id(2)

        @pl.when(i == 0)
        def _():
            for ref in (akl_ref, akc_ref, avl_ref, avc_ref):
                ref[...] = jnp.zeros_like(ref)

        qv, dov = q_ref[...], do_ref[...]
        s_c = _dot(qv, kc_ref[...], _NT) * QK_SCALE
        dp_c = _dot(dov, vc_ref[...], _NT)

        def ctx_part(p_c, delta):
            ds_c = (p_c * (dp_c - delta) * QK_SCALE).astype(BF16)
            akc_ref[...] += _dot(ds_c, qv, _TN)
            avc_ref[...] += _dot(p_c, dov, _TN)
            return _dot(ds_c, kc_ref[...], _NN)

        @pl.when(i < nq)
        def _():
            s_l = _dot(qv, kl_ref[...], _NT) * QK_SCALE
            m = jnp.maximum(jnp.max(s_c, axis=-1, keepdims=True), jnp.max(s_l, axis=-1, keepdims=True))
            p_c, p_l = jnp.exp(s_c - m), jnp.exp(s_l - m)
            inv = 1.0 / (jnp.sum(p_c, axis=-1, keepdims=True) + jnp.sum(p_l, axis=-1, keepdims=True))
            p_c, p_l = p_c * inv, p_l * inv
            dp_l = _dot(dov, vl_ref[...], _NT)
            delta = jnp.sum(p_c * dp_c, axis=-1, keepdims=True) + jnp.sum(p_l * dp_l, axis=-1, keepdims=True)
            ds_l = (p_l * (dp_l - delta) * QK_SCALE).astype(BF16)
            akl_ref[...] += _dot(ds_l, qv, _TN)
            avl_ref[...] += _dot(p_l, dov, _TN)
            dq_ref[...] = (ctx_part(p_c, delta) + _dot(ds_l, kl_ref[...], _NN)).astype(dq_ref.dtype)

        @pl.when(i == nq)
        def _():
            if with_ctx_q:
                m = jnp.max(s_c, axis=-1, keepdims=True)
                p_c = jnp.exp(s_c - m)
                p_c = p_c * (1.0 / jnp.sum(p_c, axis=-1, keepdims=True))
                delta = jnp.sum(p_c * dp_c, axis=-1, keepdims=True)
                dq_ref[...] = ctx_part(p_c, delta).astype(dq_ref.dtype)
            else:
                dq_ref[...] = jnp.zeros_like(dq_ref)
            dkl_ref[...] = akl_ref[...].astype(dkl_ref.dtype)
            dkc_ref[...] = akc_ref[...].astype(dkc_ref.dtype)
            dvl_ref[...] = avl_ref[...].astype(dvl_ref.dtype)
            dvc_ref[...] = avc_ref[...].astype(dvc_ref.dtype)

    def acc_spec(rows, width):
        return pl.BlockSpec((rows, width), lambda b, hh, i: (b, hh))

    return _call(
        body, hosted, name=name, grid=(2, HEADS, nq + 1), in_specs=[q_spec, kc_spec, kl_spec, vc_spec, vl_spec, o_spec],
        out_specs=(q_spec, acc_spec(geo.n_lat, HEAD_PAD), acc_spec(geo.n_ctx, HEAD_PAD), acc_spec(geo.n_lat, V_HEAD),
                   acc_spec(geo.n_ctx, V_HEAD)),
        out_shape=(jax.ShapeDtypeStruct((t, HEADS * HEAD_PAD), BF16),
                   jax.ShapeDtypeStruct((2 * geo.n_lat, HEADS * HEAD_PAD), BF16),
                   jax.ShapeDtypeStruct((2 * geo.n_ctx, HEADS * HEAD_PAD), BF16),
                   jax.ShapeDtypeStruct((2 * geo.n_lat, HEADS * V_HEAD), BF16),
                   jax.ShapeDtypeStruct((2 * geo.n_ctx, HEADS * V_HEAD), BF16)),
        scratch_shapes=[pltpu.VMEM((geo.n_lat, HEAD_PAD), F32), pltpu.VMEM((geo.n_ctx, HEAD_PAD), F32),
                        pltpu.VMEM((geo.n_lat, V_HEAD), F32), pltpu.VMEM((geo.n_ctx, V_HEAD), F32)],
        compiler_params=_params("parallel", "parallel", "arbitrary"),
    )(q, k, k, kv, kv, do)


def _mla_fwd(h, g, mod4, w, with_ctx_q, tabs, geo, tag, hosts, got):
    cos, sin = tabs
    ql, kl = w["g_qa"].shape[1], w["g_kva"].shape[1]
    kr_col = (ql + kl) // LANE
    nx = _pre_fwd(h, g, mod4, 3, geo, f"{tag}_pre")
    down = _mm(nx, w["w_a"], name=f"{tag}_down")
    cqn, ckvn = _latent_norm_fwd(down, w["g_qa"], w["g_kva"], geo, f"{tag}_lnorm")
    qraw = _mm(cqn, w["w_uq"], out_dtype=BF16, name=f"{tag}_uq")
    kvraw = _mm(ckvn, w["w_ukv"], out_dtype=BF16, name=f"{tag}_ukv")
    q = _qk_fwd(qraw, qraw, HEADS, False, w["gq_n"], w["gq_r"], cos, sin, geo, f"{tag}_qnorm")
    k = _qk_fwd(kvraw, down, kr_col, True, w["gk_n"], w["gk_r"], cos, sin, geo, f"{tag}_knorm")
    o = _with_host(_attn_fwd, hosts, got, "mix_a", q, k, kvraw, with_ctx_q, geo, f"{tag}_attn")
    h_out, y = _mm(o, w["w_o"], name=f"{tag}_o", gate=(h, mod4, 5, 1.0, geo))
    return h_out, (h, nx, down, cqn, ckvn, qraw, kvraw, q, k, o, y)


def _mla_bwd(dh_out, saved, g, mod4, w, with_ctx_q, tabs, geo, tag, hosts, got):
    cos, sin = tabs
    h, nx, down, cqn, ckvn, qraw, kvraw, q, k, o, y = saved
    ql, kl = w["g_qa"].shape[1], w["g_kva"].shape[1]
    kr_col = (ql + kl) // LANE
    dy, dgate = _gate_bwd(dh_out, y, mod4, 5, 1.0, geo, f"{tag}_dgate")
    do = _mm(dy, w["w_o"], tb=True, out_dtype=BF16, name=f"{tag}_do")
    dw_o = _tn_wide(o, dy, f"{tag}_dwo")
    dq, dk_lat, dk_ctx, dv_lat, dv_ctx = _with_host(_attn_bwd, hosts, got, "mix_a", q, k, kvraw, do, with_ctx_q, geo,
                                                    f"{tag}_dattn")
    dqraw, dgq_n, dgq_r = _qk_bwd(qraw, qraw, HEADS, False, w["gq_n"], w["gq_r"], cos, sin, dq, geo, f"{tag}_dqnorm")
    dkvraw, dkr, dgk_n, dgk_r = _qk_bwd(kvraw, down, kr_col, True, w["gk_n"], w["gk_r"], cos, sin,
                                        (dk_lat, dk_ctx, dv_lat, dv_ctx), geo, f"{tag}_dknorm")
    dcqn = _mm(dqraw, w["w_uq"], tb=True, out_dtype=BF16, name=f"{tag}_dcqn")
    dw_uq = _tn_wide(cqn, dqraw, f"{tag}_dwuq")
    dckvn = _mm(dkvraw, w["w_ukv"], tb=True, out_dtype=BF16, name=f"{tag}_dckvn")
    dw_ukv = _tn_wide(ckvn, dkvraw, f"{tag}_dwukv")
    ddown, dg_qa, dg_kva = _latent_norm_bwd(down, w["g_qa"], w["g_kva"], dcqn, dckvn, dkr, geo, f"{tag}_dlnorm")
    dnx = _mm(ddown, w["w_a"], tb=True, name=f"{tag}_dnx")
    dw_a = _tn_wide(nx, ddown, f"{tag}_dwa")
    dh, dg, dshift, dscale = _pre_bwd(h, g, mod4, 3, dnx, dh_out, geo, f"{tag}_dpre")
    grads = dict(w_a=dw_a, g_qa=dg_qa, w_uq=dw_uq, g_kva=dg_kva, w_ukv=dw_ukv, gq_n=dgq_n, gq_r=dgq_r, gk_n=dgk_n,
                 gk_r=dgk_r, w_o=dw_o)
    return dh, dg, (dshift, dscale, dgate), grads


def _mla_prepare(w_a, g_qa, w_uq, g_kva, w_ukv, g_q, g_k, w_o):
    ql, kl = g_qa.shape[0], g_kva.shape[0]
    d = w_a.shape[0]
    w_a_pad = jnp.concatenate([w_a[:, :ql + kl], _rope_swap(w_a[:, ql + kl:]), jnp.zeros((d, LANE - QK_ROPE), w_a.dtype)], axis=1)
    uq = w_uq.reshape(ql, HEADS, QK_HEAD)
    uq_r = jnp.pad(_rope_swap(uq[:, :, QK_NOPE:]), ((0, 0), (0, 0), (0, LANE - QK_ROPE)))
    w_uq_pad = jnp.concatenate([uq[:, :, :QK_NOPE].reshape(ql, HEADS * LANE), uq_r.reshape(ql, HEADS * LANE)], axis=1)
    ukv = w_ukv.reshape(kl, HEADS, QK_NOPE + V_HEAD)
    w_ukv_p = jnp.concatenate([ukv[:, :, :QK_NOPE].reshape(kl, HEADS * LANE), ukv[:, :, QK_NOPE:].reshape(kl, HEADS * V_HEAD)], axis=1)

    def gains(gv):
        gv = gv.astype(F32)
        return gv[None, :QK_NOPE], jnp.pad(_rope_swap(gv[QK_NOPE:]), (0, LANE - QK_ROPE))[None]

    gq_n, gq_r = gains(g_q)
    gk_n, gk_r = gains(g_k)
    return dict(w_a=w_a_pad, g_qa=g_qa.astype(F32)[None], w_uq=w_uq_pad, g_kva=g_kva.astype(F32)[None], w_ukv=w_ukv_p,
                gq_n=gq_n, gq_r=gq_r, gk_n=gk_n, gk_r=gk_r, w_o=w_o)


def _mla_unprepare(gr):
    ql, kl = gr["g_qa"].shape[1], gr["g_kva"].shape[1]
    dw_a = jnp.concatenate([gr["w_a"][:, :ql + kl], _rope_swap(gr["w_a"][:, ql + kl:ql + kl + QK_ROPE])], axis=1)
    uqn = gr["w_uq"][:, :HEADS * LANE].reshape(ql, HEADS, LANE)
    uqr = _rope_swap(gr["w_uq"][:, HEADS * LANE:].reshape(ql, HEADS, LANE)[:, :, :QK_ROPE])
    dw_uq = jnp.concatenate([uqn, uqr], axis=2).reshape(ql, HEADS * QK_HEAD)
    ukn = gr["w_ukv"][:, :HEADS * LANE].reshape(kl, HEADS, LANE)
    ukv = gr["w_ukv"][:, HEADS * LANE:].reshape(kl, HEADS, V_HEAD)
    dw_ukv = jnp.concatenate([ukn, ukv], axis=2).reshape(kl, HEADS * (QK_NOPE + V_HEAD))

    def gains(gn, grr):
        return jnp.concatenate([gn[0], _rope_swap(grr[0, :QK_ROPE])])

    return dict(mla_w_a=dw_a, mla_g_qa=gr["g_qa"][0], mla_w_uq=dw_uq, mla_g_kva=gr["g_kva"][0], mla_w_ukv=dw_ukv,
                mla_g_q=gains(gr["gq_n"], gr["gq_r"]), mla_g_k=gains(gr["gk_n"], gr["gk_r"]), mla_w_o=gr["w_o"])


def _loss_head(h, target, geo, name):
    t, d = h.shape
    tile = geo.tile
    n_lat_tiles = 2 * geo.n_lat // tile

    def body(h_ref, t_ref, dh_ref, loss_ref):
        i = pl.program_id(0)

        @pl.when(i == 0)
        def _():
            loss_ref[...] = jnp.zeros_like(loss_ref)

        @pl.when(i < n_lat_tiles)
        def _():
            e = h_ref[...] - t_ref[...]
            dh_ref[...] = e * (1.0 / d)
            part = jnp.sum(e * e, axis=0, keepdims=True) * (0.5 / d)
            loss_ref[...] += sum(part[:, j * LANE:(j + 1) * LANE] for j in range(d // LANE))

        @pl.when(i >= n_lat_tiles)
        def _():
            dh_ref[...] = jnp.zeros_like(dh_ref)

    row = pl.BlockSpec((tile, d), lambda i: (i, 0))
    tgt = pl.BlockSpec((tile, d), lambda i: (jnp.minimum(i, n_lat_tiles - 1), 0))
    dh, loss = pl.pallas_call(
        body, name=name, grid=(t // tile,), in_specs=[row, tgt],
        out_specs=(row, pl.BlockSpec((1, LANE), lambda i: (0, 0))),
        out_shape=(jax.ShapeDtypeStruct((t, d), F32), jax.ShapeDtypeStruct((1, LANE), F32)),
        compiler_params=_params("arbitrary"),
    )(h, target)
    return jnp.sum(loss), dh


def _adamw(w, g, m, v, name):
    shape = w.shape
    cols = shape[-1]
    rows = int(np.prod(shape[:-1])) if len(shape) > 1 else 1
    w2, g2, m2, v2 = (a.reshape(rows, cols) for a in (w, g, m, v))
    tr = _pick(rows, (512, 256, 128, 64, 32, 16, 8))
    c1 = 1.0 / (1.0 - ADAM_B1 ** ADAM_STEP)
    c2 = 1.0 / (1.0 - ADAM_B2 ** ADAM_STEP)

    def body(w_ref, g_ref, m_ref, v_ref, d_ref, mo_ref, vo_ref):
        gv = g_ref[...]
        mn = ADAM_B1 * m_ref[...] + (1.0 - ADAM_B1) * gv
        vn = ADAM_B2 * v_ref[...] + (1.0 - ADAM_B2) * (gv * gv)
        d_ref[...] = -ADAM_LR * ((mn * c1) / (jnp.sqrt(vn * c2) + ADAM_EPS) + ADAM_WD * w_ref[...])
        mo_ref[...] = mn
        vo_ref[...] = vn

    blk = pl.BlockSpec((tr, cols), lambda i: (i, 0))
    sds = jax.ShapeDtypeStruct((rows, cols), F32)
    d, mo, vo = pl.pallas_call(
        body, name=name, grid=(rows // tr,), in_specs=[blk] * 4, out_specs=(blk,) * 3, out_shape=(sds,) * 3,
        compiler_params=_params("parallel"),
    )(w2, g2, m2, v2)
    return d.reshape(shape), mo.reshape(shape), vo.reshape(shape)


SHARD_AXIS = {
    "w_mod": 2, "g_norm": 2, "ffn_w1": 3, "ffn_w3": 3, "ffn_w2": 2, "sc_w_in": 2, "sc_conv": 2, "sc_w_out": 1,
    "mla_w_a": 1, "mla_g_qa": 1, "mla_w_uq": 2, "mla_w_ukv": 2, "mla_w_o": 1,
}
HIDDEN_MAJOR = ("ffn_w1", "ffn_w3")


def _view(name, arr, swapped=False):
    form, swap, _ = EXCHANGE[name]
    if swap and not swapped:
        arr = jnp.swapaxes(arr, -1, -2)
    if form == "mid":
        arr = arr.reshape((-1,) + arr.shape[-2:])
        return jnp.pad(arr, ((0, 0), (0, 0), (0, -arr.shape[-1] % LANE)))
    arr = arr.reshape(-1, arr.shape[-1])
    return jnp.pad(arr, ((0, -arr.shape[0] % 16), (0, 0)))


def _unview(name, view, shape, keep_swapped=False):
    form, swap, _ = EXCHANGE[name]
    shape = shape[:-2] + (shape[-1], shape[-2]) if swap else shape
    if form == "mid":
        view = view[:, :, :shape[-1]]
    else:
        view = view[:int(np.prod(shape[:-1]))]
    arr = view.reshape(shape)
    return arr if (not swap or keep_swapped) else jnp.swapaxes(arr, -1, -2)


def _full_shape(name, local_shape):
    ax = SHARD_AXIS[name]
    return local_shape[:ax] + (N_DEV * local_shape[ax],) + local_shape[ax + 1:]


def _win(ref, form, n, j):
    start = j * n
    if not isinstance(start, int):
        start = pl.multiple_of(start, LANE if form == "last" else math.gcd(n, 16))
    if form == "mid":
        return ref.at[:, pl.ds(start, n), :]
    return ref.at[:, pl.ds(start, n)]


def _windows(view, count, of):
    return view.shape[:1] + (view.shape[1] * count // of,) + view.shape[2:]


def _gather_work(views, forms):
    na = len(views)

    def plan(x_refs, out_refs, sems):
        send_sems, recv_sems, local_sems = sems
        x, y, c = lax.axis_index("x"), lax.axis_index("y"), lax.axis_index("c")
        me, sibling = (x, y, c), (x, y, 1 - c)
        chips = [(1 - x, y), (x, 1 - y), (1 - x, 1 - y)]

        def copy(a, k, block, to, from_input):
            dst = _win(out_refs[a], forms[a], views[a].shape[1], 4 * block[0] + 2 * block[1] + block[2])
            return pltpu.make_async_remote_copy(
                src_ref=x_refs[a] if from_input else dst, dst_ref=dst, send_sem=send_sems.at[a, k],
                recv_sem=recv_sems.at[a, k], device_id=to, device_id_type=MESH)

        mine = [pltpu.make_async_copy(x_refs[a], _win(out_refs[a], forms[a], views[a].shape[1], 4 * x + 2 * y + c),
                                      local_sems.at[a]) for a in range(na)]
        first = []
        for a in range(na):
            first.append(copy(a, 0, me, sibling, True))
            first += [copy(a, 1 + j, me, (*chip, c), True) for j, chip in enumerate(chips)]
        return copy, mine, first, me, sibling, chips, c

    def start(x_refs, out_refs, sems):
        _, mine, first, *_ = plan(x_refs, out_refs, sems)
        for cp in mine + first:
            cp.start()

    def finish(x_refs, out_refs, sems):
        copy, mine, first, me, sibling, chips, c = plan(x_refs, out_refs, sems)
        passed = []
        for j, chip in enumerate(chips):
            for a in range(na):
                copy(a, 1 + j, (*chip, c), me, False).wait_recv()
                fwd = copy(a, 4 + j, (*chip, c), sibling, False)
                fwd.start()
                passed.append(fwd)
        for a in range(na):
            copy(a, 0, sibling, me, False).wait_recv()
            for j, chip in enumerate(chips):
                copy(a, 4 + j, (*chip, 1 - c), me, False).wait_recv()
        for cp in first + passed:
            cp.wait_send()
        for cp in mine:
            cp.wait()

    return Hosted(
        list(views), [jax.ShapeDtypeStruct(_windows(v, N_DEV, 1), v.dtype) for v in views],
        [pltpu.SemaphoreType.DMA((na, 7)), pltpu.SemaphoreType.DMA((na, 7)), pltpu.SemaphoreType.DMA((na,))], start, finish)


def _push_work(srcs, out_shapes, n_copies, make_copies):
    na = len(srcs)

    def start(s_refs, r_refs, sems):
        for cp in make_copies(s_refs, r_refs, sems[0], sems[1]):
            cp.start()

    def finish(s_refs, r_refs, sems):
        copies = make_copies(s_refs, r_refs, sems[0], sems[1])
        for cp in copies:
            cp.wait_recv()
        for cp in copies:
            cp.wait_send()

    return Hosted(list(srcs), out_shapes, [pltpu.SemaphoreType.DMA((na, n_copies)), pltpu.SemaphoreType.DMA((na, n_copies))],
                  start, finish)


def _sibling_work(fulls, forms):
    na = len(fulls)
    widths = [f.shape[1] // N_DEV for f in fulls]

    def make_copies(g_refs, r_refs, send_sems, recv_sems):
        x, y, c = lax.axis_index("x"), lax.axis_index("y"), lax.axis_index("c")
        return [
            pltpu.make_async_remote_copy(
                src_ref=_win(g_refs[a], forms[a], widths[a], 2 * chip + (1 - c)),
                dst_ref=_win(r_refs[a], forms[a], widths[a], chip), send_sem=send_sems.at[a, chip],
                recv_sem=recv_sems.at[a, chip], device_id=(x, y, 1 - c), device_id_type=MESH)
            for a in range(na) for chip in range(N_CHIP)
        ]

    return _push_work(fulls, [jax.ShapeDtypeStruct(_windows(f, N_CHIP, N_DEV), f.dtype) for f in fulls], N_CHIP, make_copies)


def _chip_work(parts, forms):
    na = len(parts)
    widths = [p.shape[1] // N_CHIP for p in parts]

    def make_copies(p_refs, r_refs, send_sems, recv_sems):
        x, y, c = lax.axis_index("x"), lax.axis_index("y"), lax.axis_index("c")
        chips = [(1 - x, y), (x, 1 - y), (1 - x, 1 - y)]
        return [
            pltpu.make_async_remote_copy(
                src_ref=_win(p_refs[a], forms[a], widths[a], 2 * px + py), dst_ref=_win(r_refs[a], forms[a], widths[a], j),
                send_sem=send_sems.at[a, j], recv_sem=recv_sems.at[a, j], device_id=(px, py, c), device_id_type=MESH)
            for a in range(na) for j, (px, py) in enumerate(chips)
        ]

    return _push_work(parts, [jax.ShapeDtypeStruct(_windows(p, 3, N_CHIP), p.dtype) for p in parts], 3, make_copies)


def _sum_tiles(view, form, n):
    if form == "mid":
        tr = n
        while tr * view.shape[2] * 4 > 2 * 1024 * 1024 and tr % 32 == 0:
            tr //= 2
        return 1, tr
    return _pick(view.shape[0], (512, 256, 128, 64, 32, 16)), n


def _window_spec(form, tl, tr, rest, window_of):
    if form == "mid":
        return lambda per: pl.BlockSpec((None, tr) + rest, lambda l, k, i, s: (l, window_of(k, s) * per + i, 0))
    return lambda per: pl.BlockSpec((tl, tr), lambda l, k, i, s: (l, window_of(k, s)))


def _chip_partials(g, recv, core, form, name):
    n = g.shape[1] // N_DEV
    tl, tr = _sum_tiles(g, form, n)
    per = n // tr
    rest = tuple(g.shape[2:])

    def body(core_ref, g_ref, r_ref, o_ref):
        o_ref[...] = (g_ref[...] + r_ref[...]).astype(o_ref.dtype)

    own = _window_spec(form, tl, tr, rest, lambda k, s: 2 * k + s[0])(per)
    by_chip = _window_spec(form, tl, tr, rest, lambda k, s: k)(per)
    return pl.pallas_call(
        body, name=name,
        grid_spec=pltpu.PrefetchScalarGridSpec(
            num_scalar_prefetch=1, grid=(g.shape[0] // tl, N_CHIP, per), in_specs=[own, by_chip], out_specs=by_chip),
        out_shape=jax.ShapeDtypeStruct(recv.shape, BF16), compiler_params=_params("parallel", "parallel", "parallel"),
    )(core, g, recv)


def _reduce_final(p, recv, chip, form, name):
    n = p.shape[1] // N_CHIP
    tl, tr = _sum_tiles(p, form, n)
    per = n // tr
    rest = tuple(p.shape[2:])

    def body(chip_ref, p_ref, ry_ref, rx_ref, rxy_ref, o_ref):
        own_pair = p_ref[...].astype(F32) + ry_ref[...].astype(F32)
        o_ref[...] = own_pair + (rx_ref[...].astype(F32) + rxy_ref[...].astype(F32))

    def rel(j):
        return _window_spec(form, tl, tr, rest, lambda k, s: j)(per)

    own = _window_spec(form, tl, tr, rest, lambda k, s: s[0])(per)
    return pl.pallas_call(
        body, name=name,
        grid_spec=pltpu.PrefetchScalarGridSpec(
            num_scalar_prefetch=1, grid=(p.shape[0] // tl, 1, per), in_specs=[own, rel(1), rel(0), rel(2)],
            out_specs=rel(0)),
        out_shape=jax.ShapeDtypeStruct(p.shape[:1] + (n,) + p.shape[2:], F32),
        compiler_params=_params("parallel", "parallel", "parallel"),
    )(chip, p, recv, recv, recv)


def _pack_replicated(arrays):
    pieces = []
    for a in arrays:
        flat = a.reshape(-1).astype(F32)
        pieces.append(jnp.pad(flat, (0, -flat.size % LANE)))
    total = sum(p.size for p in pieces)
    pieces.append(jnp.zeros((-total % (16 * LANE),), F32))
    return jnp.concatenate(pieces).reshape(-1, LANE)


def _unpack_replicated(buf, shapes):
    flat, out, off = buf.reshape(-1), [], 0
    for shape in shapes:
        size = int(np.prod(shape))
        out.append(flat[off:off + size].reshape(shape))
        off += size + (-size % LANE)
    return out


def _silu(v):
    return v * jax.nn.sigmoid(v)


FFN_NAMES = ("ffn_w1", "ffn_w3", "ffn_w2")
SC_NAMES = ("sc_w_in", "sc_conv", "sc_w_out")
MLA_SHARDED = ("mla_w_a", "mla_g_qa", "mla_w_uq", "mla_w_ukv", "mla_w_o")
MLA_NAMES = ("mla_w_a", "mla_g_qa", "mla_w_uq", "mla_g_kva", "mla_w_ukv", "mla_g_q", "mla_g_k", "mla_w_o")


def _local_step(src, x, c, ctx, target):
    bsz, n_lat, d = x.shape
    n_ctx = ctx.shape[1]
    assert bsz == 2
    geo = Geo(n_lat, n_ctx)
    depth = src.depth
    tc = _conv_tile(d)
    tabs = _rope_tables(geo)

    h = jnp.concatenate([x.reshape(2 * n_lat, d), ctx.reshape(2 * n_ctx, d)], axis=0)
    tgt = target.reshape(2 * n_lat, d)

    saved = []
    for i in range(depth):
        kind = i % 2
        wl, slots = src.weights(i), src.fwd_slots(i)
        gn = wl["g_norm"].astype(F32)
        mod4 = src.mod(i).reshape(N_SEG, N_MOD, 1, d)
        h, s1 = _ffn_fwd(h, gn[0:1], mod4, 0, wl, 0, geo, f"l{i}_f1", "f1", slots, slots)
        if kind == 0:
            mix = (_interleave(wl["sc_w_in"], 3, tc), wl["sc_conv"].astype(F32), wl["sc_w_out"])
            h, s2 = _sconv_fwd(h, gn[1:2], mod4, *mix, geo, f"l{i}_sc", slots, slots)
        else:
            mix = _mla_prepare(*[wl[name] for name in MLA_NAMES])
            h, s2 = _mla_fwd(h, gn[1:2], mod4, mix, i != depth - 1, tabs, geo, f"l{i}_mla", slots, slots)
        h, s3 = _ffn_fwd(h, gn[2:3], mod4, 6, wl, 1, geo, f"l{i}_f2", "f2", slots, slots)
        saved.append((wl, gn, mod4, mix, s1, s2, s3))

    loss, dh = _loss_head(h, tgt, geo, "loss_head")

    g_b_mod = [None] * depth
    for i in reversed(range(depth)):
        kind = i % 2
        wl, gn, mod4, mix, s1, s2, s3 = saved[i]
        slots = src.bwd_slots(i)
        gbuf = {name: lax.empty(wl[name].shape, F32) for name in ("ffn_w1", "ffn_w3", "ffn_w2")}
        dh, dg2, dm2 = _ffn_bwd(dh, s3, gn[2:3], mod4, 6, wl, 1, gbuf, geo, f"l{i}_f2", "f2", slots, slots)
        src.ffn2_grads(i, gbuf)
        if kind == 0:
            dh, dg1, dm1, dwin, dconv, dwout = _sconv_bwd(dh, s2, gn[1:2], mod4, *mix, geo, f"l{i}_sc", slots, slots)
            gl = dict(sc_w_in=_deinterleave(dwin, 3, tc), sc_conv=dconv, sc_w_out=dwout)
        else:
            dh, dg1, dm1, gm = _mla_bwd(dh, s2, gn[1:2], mod4, mix, i != depth - 1, tabs, geo, f"l{i}_mla", slots, slots)
            gl = _mla_unprepare(gm)
        dh, dg0, dm0 = _ffn_bwd(dh, s1, gn[0:1], mod4, 0, wl, 0, gbuf, geo, f"l{i}_f1", "f1", slots, slots)
        dmod = jnp.concatenate(list(dm0) + list(dm1) + list(dm2), axis=1).reshape(N_SEG, N_MOD * d)
        dmod8 = jnp.concatenate([dmod, jnp.zeros((8 - N_SEG, N_MOD * d), F32)], axis=0)
        g_b_mod[i] = jnp.sum(dmod, axis=0)
        gl.update(gbuf, g_norm=jnp.concatenate([dg0, dg1, dg2], axis=0))
        src.dmod(i, dmod8)
        src.grads(i, gl)

    grad_x = dh[:2 * n_lat].reshape(x.shape)
    return loss, grad_x, jnp.stack(g_b_mod)


class _Slots:
    def __init__(self, get, put):
        self.get, self._put = get, put

    def __setitem__(self, slot, outs):
        self._put(slot, outs)


FWD_PLAN = {
    0: {"f1_up": ("ffn_w1",), "f1_down": ("g_norm", "mix"), "mix_a": ("ffn_w3",), "mix_b": ("ffn_w2",)},
    1: {"f1_up": ("ffn_w1",), "mix_a": ("ffn_w3", "g_norm", "mix"), "f2_up": ("ffn_w2",)},
}
SIBLING_PLAN = {"f2_dact": ("ffn_w1", "g_norm", "mix"), "f2_dw2": ("ffn_w3", "ffn_w2")}
BWD_PLAN = {
    0: {"f2_dnx": ("ffn_w1",), "mix_b": ("ffn_w3",), "mix_a": ("ffn_w2",), "f1_dact": ("g_norm", "mix")},
    1: {"f2_dnx": ("ffn_w1",), "mix_a": ("ffn_w3", "ffn_w2"), "f1_dnx": ("g_norm", "mix")},
}
DMOD_SLOT = {0: "mix_c", 1: "f1_dact"}
MOD_ROWS = 32


class _Exchange:
    def __init__(self, w):
        self.w = w
        self.depth = w["w_mod"].shape[0]
        self.c_ctx = w["c_ctx"]
        self.me = 4 * lax.axis_index("x") + 2 * lax.axis_index("y") + lax.axis_index("c")
        self.core = lax.axis_index("c").astype(jnp.int32).reshape(1)
        self.chip = (2 * lax.axis_index("x") + lax.axis_index("y")).astype(jnp.int32).reshape(1)
        self.full, self.gviews, self.parts, self.reduced, self.rep, self.dmods = {}, {}, {}, {}, {}, {}
        self.ctx_pre = jnp.zeros_like(self.c_ctx)

    def _layer_of(self, name, i):
        return i // 2 if name.startswith(("sc_", "mla_")) else i

    def _mixer(self, i):
        return SC_NAMES if i % 2 == 0 else MLA_SHARDED

    def _expand(self, names, i):
        out = []
        for name in names:
            out += list(self._mixer(i)) if name == "mix" else [name]
        return out

    def _group(self, i):
        return ["g_norm", "ffn_w1", "ffn_w3", "ffn_w2"] + list(self._mixer(i))

    def _local(self, name, i):
        arr = self.w[name][self._layer_of(name, i)]
        return arr[:, None] if name == "mla_g_qa" else arr

    def _shapes(self, name, i):
        local = tuple(self._local(name, i).shape)
        ax = SHARD_AXIS[name] - 1
        return local, local[:ax] + (N_DEV * local[ax],) + local[ax + 1:]

    def _gather(self, names, i):
        views = [_view(n, self._local(n, i).astype(BF16 if EXCHANGE[n][2] else F32)) for n in names]
        return _gather_work(views, [EXCHANGE[n][0] for n in names])

    def _gathered(self, names, i, outs):
        for name, fv in zip(names, outs):
            arr = _unview(name, fv, self._shapes(name, i)[1], keep_swapped=name in HIDDEN_MAJOR)
            self.full[name, i] = arr[:, 0] if name == "mla_g_qa" else arr

    def prefetch(self, c):
        bsz, d = c.shape
        (conds,) = _run_hosted(_gather_work([jnp.pad(c, ((0, 8 - bsz), (0, 0)))[None]], ["mid"]), "gather_cond")
        conds = conds.reshape(N_DEV, 8, d)[:, :bsz]
        act = _silu(jnp.concatenate([conds, jnp.broadcast_to(self.c_ctx, (N_DEV, 1, d))], axis=1))
        self.s_rows = jnp.pad(act.reshape(N_DEV * N_SEG, d), ((0, MOD_ROWS - N_DEV * N_SEG), (0, 0)))
        cols = jnp.stack([_mm(self.s_rows, self.w["w_mod"][l], name=f"mod_cols_{l}") for l in range(self.depth)])
        names = self._group(0)
        work = self._gather(names, 0)
        both = _gather_work(work.inputs + [cols.reshape(self.depth * MOD_ROWS, -1)], self._forms(names) + ["last"])
        outs = _run_hosted(both, "gather_l0")
        self._gathered(names, 0, outs[:-1])
        mods = lax.dynamic_slice_in_dim(outs[-1].reshape(self.depth, MOD_ROWS, -1), N_SEG * self.me, N_SEG, axis=1)
        self.mods = mods + self.w["b_mod"][:, None, :]

    def mod(self, i):
        return self.mods[i]

    def weights(self, i):
        wl = {name: self.full[name, i] for name in self._group(i)}
        if i % 2 == 1:
            for name in ("mla_g_kva", "mla_g_q", "mla_g_k"):
                wl[name] = self.w[name][i // 2]
        return wl

    def fwd_slots(self, i):
        plan = FWD_PLAN[i % 2] if i + 1 < self.depth else {}
        names = {slot: self._expand(plan[slot], i + 1) for slot in plan}
        return _Slots(lambda slot: self._gather(names[slot], i + 1) if slot in names else None,
                      lambda slot, outs: self._gathered(names[slot], i + 1, outs))

    def ffn2_grads(self, i, gbuf):
        if i == 0:
            for name in FFN_NAMES:
                self.gviews[name + "#1", 0] = _view(name, gbuf[name][1:2], swapped=True)

    def grads(self, i, gl):
        for name in self._group(i):
            g = gl[name][:, None] if name == "mla_g_qa" else gl[name]
            if i == 0 and name in FFN_NAMES:
                self.gviews[name + "#0", 0] = _view(name, g[0:1], swapped=True)
            else:
                self.gviews[name, i] = _view(name, g, swapped=name in HIDDEN_MAJOR)
        for name in REPLICATED:
            if name in gl:
                self.rep[name, i // 2] = gl[name]

    def dmod(self, i, dmod8):
        self.dmods[i] = dmod8

    def _dmod_gather(self, i):
        return _gather_work([self.dmods[i][None]], ["mid"])

    def _dmod_gathered(self, i, outs):
        n = self.w["w_mod"].shape[2]
        rows = outs[0].reshape(N_DEV, 8, -1)[:, :N_SEG]
        mine = lax.dynamic_slice_in_dim(rows, n * self.me, n, axis=2)
        flat = jnp.pad(mine.reshape(N_DEV * N_SEG, n), ((0, MOD_ROWS - N_DEV * N_SEG), (0, 0)))
        self.reduced["w_mod", i] = _mm(self.s_rows, flat, ta=True, name=f"dwmod_{i}")
        ctx_rows = jnp.pad(jnp.sum(mine[:, N_SEG - 1], axis=0, keepdims=True), ((0, 7), (0, 0)))
        self.ctx_pre = self.ctx_pre + _mm(ctx_rows, self.w["w_mod"][i], tb=True, name=f"dcond_{i}")[0]

    def _forms(self, names):
        return [EXCHANGE[n.split("#")[0]][0] for n in names]

    def _partials(self, names, i, from_sibling):
        for name, recv in zip(names, from_sibling):
            self.parts[name, i] = _chip_partials(self.gviews[name, i], recv, self.core, self._forms([name])[0],
                                                 f"partial_{name.replace('#', '_')}_{i}")

    def _finals(self, names, i, from_chips):
        for name, recv in zip(names, from_chips):
            rv = _reduce_final(self.parts[name, i], recv, self.chip, self._forms([name])[0],
                               f"final_{name.replace('#', '_')}_{i}")
            base = name.split("#")[0]
            shape = self._shapes(base, i)[0]
            arr = _unview(base, rv, (1,) + shape[1:] if "#" in name else shape)
            self.reduced[name, i] = arr[:, 0] if name == "mla_g_qa" else arr

    def bwd_slots(self, i):
        if i + 1 >= self.depth:
            return _Slots(lambda slot: None, None)
        plan = BWD_PLAN[i % 2]
        chips = {slot: (self._expand(plan[slot], i + 1), i + 1) for slot in plan}
        sibling = {slot: (self._expand(SIBLING_PLAN[slot], i + 1), i + 1) for slot in SIBLING_PLAN}
        if i == 0:
            sibling["mix_d"] = ([name + "#1" for name in FFN_NAMES], 0)
            chips["f1_dw2"] = (["ffn_w1#1"], 0)
            chips["f1_dnx"] = (["ffn_w3#1", "ffn_w2#1"], 0)

        def get(slot):
            if slot in sibling:
                names, group = sibling[slot]
                return _sibling_work([self.gviews[n, group] for n in names], self._forms(names))
            if slot in chips:
                names, group = chips[slot]
                return _chip_work([self.parts[n, group] for n in names], self._forms(names))
            if slot == DMOD_SLOT[i % 2]:
                return self._dmod_gather(i + 1)
            return None

        def put(slot, outs):
            if slot in sibling:
                self._partials(*sibling[slot], outs)
            elif slot in chips:
                self._finals(*chips[slot], outs)
            else:
                self._dmod_gathered(i + 1, outs)

        return _Slots(get, put)

    def finish(self, rep_grads):
        group = [name + "#0" if name in FFN_NAMES else name for name in self._group(0)]
        self._dmod_gathered(0, _run_hosted(self._dmod_gather(0), "gather_dmod_l0"))
        rep_grads["c_ctx"] = self.ctx_pre
        for name in REPLICATED:
            if name not in rep_grads:
                rep_grads[name] = jnp.stack([self.rep[name, j] for j in range(self.w[name].shape[0])])
        rep = _pack_replicated([rep_grads[name] for name in REPLICATED])
        views = [self.gviews[n, 0] for n in group] + [jnp.tile(rep[None], (1, N_DEV, 1))]
        forms = self._forms(group) + ["mid"]
        from_sibling = _run_hosted(_sibling_work(views, forms), "reduce_sibling_l0")
        self._partials(group, 0, from_sibling[:-1])
        rep_part = _chip_partials(views[-1], from_sibling[-1], self.core, "mid", "partial_replicated")
        parts = [self.parts[n, 0] for n in group] + [rep_part]
        from_chips = _run_hosted(_chip_work(parts, forms), "reduce_chips_l0")
        self._finals(group, 0, from_chips[:-1])
        for name in FFN_NAMES:
            self.reduced[name, 0] = jnp.concatenate([self.reduced[name + "#0", 0], self.reduced[name + "#1", 0]], axis=0)
        rep_sum = _reduce_final(rep_part, from_chips[-1], self.chip, "mid", "final_replicated")
        out = dict(zip(REPLICATED, _unpack_replicated(rep_sum, [self.w[name].shape for name in REPLICATED])))
        sg = jax.nn.sigmoid(self.c_ctx)
        out["c_ctx"] = out["c_ctx"] * (sg * (1.0 + self.c_ctx * (1.0 - sg)))
        for name in EXCHANGE:
            layers = range(self.w[name].shape[0])
            step = 2 if name.startswith(("sc_", "mla_")) else 1
            first = 1 if name.startswith("mla_") else 0
            out[name] = jnp.stack([self.reduced[name, first + step * l] for l in layers])
        return out


def kernel(x, c, ctx, c_ctx, w_mod, b_mod, g_norm, ffn_w1, ffn_w3, ffn_w2, sc_w_in, sc_conv, sc_w_out, mla_w_a, mla_g_qa, mla_w_uq, mla_g_kva, mla_w_ukv, mla_g_q, mla_g_k, mla_w_o, loss_target, m_c_ctx, m_w_mod, m_b_mod, m_g_norm, m_ffn_w1, m_ffn_w3, m_ffn_w2, m_sc_w_in, m_sc_conv, m_sc_w_out, m_mla_w_a, m_mla_g_qa, m_mla_w_uq, m_mla_g_kva, m_mla_w_ukv, m_mla_g_q, m_mla_g_k, m_mla_w_o, v_c_ctx, v_w_mod, v_b_mod, v_g_norm, v_ffn_w1, v_ffn_w3, v_ffn_w2, v_sc_w_in, v_sc_conv, v_sc_w_out, v_mla_w_a, v_mla_g_qa, v_mla_w_uq, v_mla_g_kva, v_mla_w_ukv, v_mla_g_q, v_mla_g_k, v_mla_w_o):
    w = dict(c_ctx=c_ctx, w_mod=w_mod, b_mod=b_mod, g_norm=g_norm, ffn_w1=ffn_w1, ffn_w3=ffn_w3, ffn_w2=ffn_w2,
             sc_w_in=sc_w_in, sc_conv=sc_conv, sc_w_out=sc_w_out, mla_w_a=mla_w_a, mla_g_qa=mla_g_qa, mla_w_uq=mla_w_uq,
             mla_g_kva=mla_g_kva, mla_w_ukv=mla_w_ukv, mla_g_q=mla_g_q, mla_g_k=mla_g_k, mla_w_o=mla_w_o)
    m = dict(c_ctx=m_c_ctx, w_mod=m_w_mod, b_mod=m_b_mod, g_norm=m_g_norm, ffn_w1=m_ffn_w1, ffn_w3=m_ffn_w3,
             ffn_w2=m_ffn_w2, sc_w_in=m_sc_w_in, sc_conv=m_sc_conv, sc_w_out=m_sc_w_out, mla_w_a=m_mla_w_a,
             mla_g_qa=m_mla_g_qa, mla_w_uq=m_mla_w_uq, mla_g_kva=m_mla_g_kva, mla_w_ukv=m_mla_w_ukv, mla_g_q=m_mla_g_q,
             mla_g_k=m_mla_g_k, mla_w_o=m_mla_w_o)
    v = dict(c_ctx=v_c_ctx, w_mod=v_w_mod, b_mod=v_b_mod, g_norm=v_g_norm, ffn_w1=v_ffn_w1, ffn_w3=v_ffn_w3,
             ffn_w2=v_ffn_w2, sc_w_in=v_sc_w_in, sc_conv=v_sc_conv, sc_w_out=v_sc_w_out, mla_w_a=v_mla_w_a,
             mla_g_qa=v_mla_g_qa, mla_w_uq=v_mla_w_uq, mla_g_kva=v_mla_g_kva, mla_w_ukv=v_mla_w_ukv, mla_g_q=v_mla_g_q,
             mla_g_k=v_mla_g_k, mla_w_o=v_mla_w_o)
    exchange = _Exchange(w)
    exchange.prefetch(c)
    loss, grad_x, g_b_mod = _local_step(exchange, x, c, ctx, loss_target)
    loss = lax.psum(loss, ("x", "y", "c"))
    reduced = exchange.finish(dict(b_mod=g_b_mod))

    outs = [[], [], [], []]
    for name in WEIGHTS:
        delta, new_m, new_v = _adamw(w[name], reduced[name], m[name], v[name], f"adamw_{name}")
        for lst, val in zip(outs, (reduced[name], delta, new_m, new_v)):
            lst.append(val)
    return (loss, grad_x, *outs[0], *outs[1], *outs[2], *outs[3])
```

```python
import functools
import math

import jax
import jax.numpy as jnp
import numpy as np
from jax import lax
from jax.experimental import pallas as pl
from jax.experimental.pallas import tpu as pltpu

F32 = jnp.float32
BF16 = jnp.bfloat16

N_MOD = 9
HEADS = 8
QK_NOPE = 128
QK_ROPE = 64
QK_HEAD = QK_NOPE + QK_ROPE
V_HEAD = 128
GRID_W = 64
ROPE_BASE = 10000.0
QK_SCALE = QK_HEAD ** -0.5
EPS = 1e-6
ADAM_LR, ADAM_B1, ADAM_B2, ADAM_EPS, ADAM_WD, ADAM_STEP = 0.001, 0.9, 0.999, 1e-08, 0.01, 10

N_DEV = 8
N_CHIP = 4
N_SEG = 3
LANE = 128
HEAD_PAD = 2 * LANE
VMEM_LIMIT_BYTES = 48 * 1024 * 1024
MESH = pl.DeviceIdType.MESH

WEIGHTS = ["c_ctx", "w_mod", "b_mod", "g_norm", "ffn_w1", "ffn_w3", "ffn_w2", "sc_w_in", "sc_conv", "sc_w_out",
           "mla_w_a", "mla_g_qa", "mla_w_uq", "mla_g_kva", "mla_w_ukv", "mla_g_q", "mla_g_k", "mla_w_o"]
EXCHANGE = {
    "w_mod": ("last", False, True), "ffn_w1": ("mid", True, True), "ffn_w3": ("mid", True, True),
    "ffn_w2": ("mid", False, True), "sc_w_in": ("last", False, True), "sc_w_out": ("mid", False, True),
    "mla_w_a": ("mid", False, True), "mla_w_uq": ("mid", True, True), "mla_w_ukv": ("last", False, True),
    "mla_w_o": ("mid", False, True), "g_norm": ("last", False, False), "sc_conv": ("last", False, False),
    "mla_g_qa": ("mid", False, False),
}
REPLICATED = ["c_ctx", "b_mod", "mla_g_kva", "mla_g_q", "mla_g_k"]


def _pick(n, cands):
    for cand in cands:
        if n % cand == 0:
            return cand
    return n


def _params(*sem):
    return pltpu.CompilerParams(dimension_semantics=sem, vmem_limit_bytes=VMEM_LIMIT_BYTES)


def _hbm():
    return pl.BlockSpec(memory_space=pl.ANY)


class Hosted:
    def __init__(self, inputs, out_shapes, scratch, start, finish):
        self.inputs, self.out_shapes, self.scratch, self.start, self.finish = inputs, out_shapes, scratch, start, finish


def _call(body, hosted, **kw):
    if hosted is None:
        return pl.pallas_call(body, **kw)
    single = not isinstance(kw["out_shape"], (tuple, list))
    out_shape = [kw["out_shape"]] if single else list(kw["out_shape"])
    out_specs = [kw["out_specs"]] if single else list(kw["out_specs"])
    in_specs, scratch, grid = list(kw["in_specs"]), list(kw.get("scratch_shapes", ())), kw["grid"]
    n_in, n_out, n_scr = len(in_specs), len(out_shape), len(scratch)
    h_in, h_out = len(hosted.inputs), len(hosted.out_shapes)

    def wrapped(*refs):
        ins, hins = refs[:n_in], refs[n_in:n_in + h_in]
        o0 = n_in + h_in
        outs, houts = refs[o0:o0 + n_out], refs[o0 + n_out:o0 + n_out + h_out]
        s0 = o0 + n_out + h_out
        scr, hscr = refs[s0:s0 + n_scr], refs[s0 + n_scr:]
        first = functools.reduce(jnp.logical_and, [pl.program_id(a) == 0 for a in range(len(grid))])
        last = functools.reduce(jnp.logical_and, [pl.program_id(a) == g - 1 for a, g in enumerate(grid)])

        @pl.when(first)
        def _():
            hosted.start(hins, houts, hscr)

        body(*ins, *outs, *scr)

        @pl.when(last)
        def _():
            hosted.finish(hins, houts, hscr)

    call = pl.pallas_call(
        wrapped, name=kw["name"], grid=grid, in_specs=in_specs + [_hbm()] * h_in,
        out_specs=tuple(out_specs + [_hbm()] * h_out), out_shape=tuple(out_shape + list(hosted.out_shapes)),
        scratch_shapes=scratch + list(hosted.scratch), input_output_aliases=kw.get("input_output_aliases", {}),
        compiler_params=_params(*["arbitrary"] * len(grid)))

    def run(*args):
        res = call(*args, *hosted.inputs)
        comp = res[:n_out]
        return (comp[0] if single else tuple(comp)), list(res[n_out:])

    return run


def _run_hosted(hosted, name):
    def body(*refs):
        h_in, h_out = len(hosted.inputs), len(hosted.out_shapes)
        hins, houts, hscr = refs[:h_in], refs[h_in:h_in + h_out], refs[h_in + h_out:]
        hosted.start(hins, houts, hscr)
        hosted.finish(hins, houts, hscr)

    return list(pl.pallas_call(
        body, name=name, in_specs=[_hbm()] * len(hosted.inputs), out_specs=tuple([_hbm()] * len(hosted.out_shapes)),
        out_shape=tuple(hosted.out_shapes), scratch_shapes=list(hosted.scratch))(*hosted.inputs))


class Geo:
    def __init__(self, n_lat, n_ctx):
        self.n_lat, self.n_ctx = n_lat, n_ctx
        self.rows = 2 * n_lat + 2 * n_ctx
        self.tile = n_ctx
        assert n_lat % n_ctx == 0 and n_ctx % 16 == 0
        self.mm_tile = _pick(n_lat, (512, 256, 128)) if self.rows % _pick(n_lat, (512, 256, 128)) == 0 else n_ctx
        self.big_tile = _pick(self.rows, (1536, 768, 512, 256))

    def seg(self, i, tile):
        return jnp.minimum((i * tile) // self.n_lat, N_SEG - 1)

    def seg_start(self, i, tile):
        row = i * tile
        return jnp.logical_or(row % self.n_lat == 0, row == 2 * self.n_lat) & (row <= 2 * self.n_lat)


_NT = (((1,), (1,)), ((), ()))
_NN = (((1,), (0,)), ((), ()))
_TN = (((0,), (0,)), ((), ()))


def _dot(a, b, dims):
    return lax.dot_general(a.astype(BF16), b.astype(BF16), dims, preferred_element_type=F32)


def _mm(a, b, *, ta=False, tb=False, out_dtype=F32, name, gate=None, hosted=None):
    (kdim, m) = a.shape if ta else a.shape[::-1]
    n = b.shape[0] if tb else b.shape[1]
    assert (b.shape[1] if tb else b.shape[0]) == kdim
    if gate is not None:
        tm = gate[4].mm_tile
    else:
        tm = _pick(m, (1536, 768, 512, 256, 128))
    tn = _pick(n, (512, 256, 128))
    tk = _pick(kdim, (1024, 512, 256, 128))
    nk = kdim // tk
    dims = (((0 if ta else 1,), (1 if tb else 0,)), ((), ()))

    def body(*refs):
        if gate is not None:
            a_ref, b_ref, res_ref, gate_ref, o_ref, y_ref, acc_ref = refs
        else:
            a_ref, b_ref, o_ref, acc_ref = refs
        kk = pl.program_id(2)

        @pl.when(kk == 0)
        def _():
            acc_ref[...] = jnp.zeros_like(acc_ref)

        acc_ref[...] += lax.dot_general(a_ref[...].astype(BF16), b_ref[...].astype(BF16), dims,
                                        preferred_element_type=F32)

        @pl.when(kk == nk - 1)
        def _():
            acc = acc_ref[...]
            if gate is not None:
                y_ref[...] = acc.astype(y_ref.dtype)
                o_ref[...] = res_ref[...] + (gate[3] * gate_ref[...]) * acc
            else:
                o_ref[...] = acc.astype(o_ref.dtype)

    a_spec = pl.BlockSpec((tk, tm), lambda i, j, k: (k, i)) if ta else pl.BlockSpec((tm, tk), lambda i, j, k: (i, k))
    b_spec = pl.BlockSpec((tn, tk), lambda i, j, k: (j, k)) if tb else pl.BlockSpec((tk, tn), lambda i, j, k: (k, j))
    o_spec = pl.BlockSpec((tm, tn), lambda i, j, k: (i, j))
    in_specs, args = [a_spec, b_spec], [a, b]
    out_shape, out_specs = jax.ShapeDtypeStruct((m, n), out_dtype), o_spec
    if gate is not None:
        res, mod4, kmod, _, geo = gate
        in_specs += [o_spec, pl.BlockSpec((None, None, 1, tn), lambda i, j, k: (geo.seg(i, tm), kmod, 0, j))]
        args += [res, mod4]
        out_shape = (jax.ShapeDtypeStruct((m, n), F32), jax.ShapeDtypeStruct((m, n), BF16))
        out_specs = (o_spec, o_spec)
    return _call(
        body, hosted, name=name, grid=(m // tm, n // tn, nk), in_specs=in_specs, out_specs=out_specs,
        out_shape=out_shape, scratch_shapes=[pltpu.VMEM((tm, tn), F32)],
        compiler_params=_params("parallel", "parallel", "arbitrary"),
    )(*args)


def _tn_wide(lhs, rhs, name, into=None, s0=0, hosted=None):
    t, m = lhs.shape
    n = rhs.shape[1]
    tm = _pick(m, (1408, 1024, 512, 256, 128))
    while tm * n * 4 > 6.5 * 1024 * 1024 and tm % 256 == 0:
        tm //= 2
    tk = next(c for c in (1536, 768, 512, 256, 128, t)
              if t % c == 0 and c * (tm + n) * 4 + tm * n * 12 <= 36 * 1024 * 1024)

    def body(l_ref, r_ref, *rest):
        o_ref = rest[-1]
        kk = pl.program_id(1)
        part = lax.dot_general(l_ref[...], r_ref[...], _TN, preferred_element_type=F32)

        @pl.when(kk == 0)
        def _():
            o_ref[...] = part

        @pl.when(kk > 0)
        def _():
            o_ref[...] += part

    in_specs = [pl.BlockSpec((tk, tm), lambda i, k: (k, i)), pl.BlockSpec((tk, n), lambda i, k: (k, 0))]
    if into is None:
        return _call(
            body, hosted, name=name, grid=(m // tm, t // tk), in_specs=in_specs,
            out_specs=pl.BlockSpec((tm, n), lambda i, k: (i, 0)), out_shape=jax.ShapeDtypeStruct((m, n), F32),
            compiler_params=_params("parallel", "arbitrary"),
        )(lhs, rhs)
    return _call(
        body, hosted, name=name, grid=(m // tm, t // tk), in_specs=in_specs + [pl.BlockSpec(memory_space=pl.ANY)],
        out_specs=pl.BlockSpec((None, tm, n), lambda i, k: (s0, i, 0)),
        out_shape=jax.ShapeDtypeStruct(into.shape, into.dtype), input_output_aliases={2: 0},
        compiler_params=_params("parallel", "arbitrary"),
    )(lhs, rhs, into)


def _mod_spec(geo, tile, kmod, d):
    return pl.BlockSpec((None, None, 1, d), lambda i: (geo.seg(i, tile), kmod, 0, 0))


def _pre_fwd(h, g, mod4, k_shift, geo, name):
    t, d = h.shape
    tile = geo.tile

    def body(h_ref, g_ref, sh_ref, sc_ref, o_ref):
        hv = h_ref[...]
        r = lax.rsqrt(jnp.mean(hv * hv, axis=-1, keepdims=True) + EPS)
        y = hv * r * g_ref[...]
        o_ref[...] = (y * (1.0 + sc_ref[...]) + sh_ref[...]).astype(o_ref.dtype)

    row = pl.BlockSpec((tile, d), lambda i: (i, 0))
    return pl.pallas_call(
        body, name=name, grid=(t // tile,),
        in_specs=[row, pl.BlockSpec((1, d), lambda i: (0, 0)), _mod_spec(geo, tile, k_shift, d),
                  _mod_spec(geo, tile, k_shift + 1, d)],
        out_specs=row, out_shape=jax.ShapeDtypeStruct((t, d), BF16), compiler_params=_params("parallel"),
    )(h, g, mod4, mod4)


def _pre_bwd(h, g, mod4, k_shift, dnx, dres, geo, name):
    t, d = h.shape
    tile = geo.tile

    def body(h_ref, g_ref, sc_ref, dnx_ref, dres_ref, dh_ref, dg_ref, dsh_ref, dsc_ref):
        i = pl.program_id(0)
        hv, gv, dout = h_ref[...], g_ref[...], dnx_ref[...].astype(F32)
        r = lax.rsqrt(jnp.mean(hv * hv, axis=-1, keepdims=True) + EPS)
        xhat = hv * r
        dy = dout * (1.0 + sc_ref[...])
        u = dy * gv
        dh_ref[...] = r * (u - xhat * jnp.mean(u * xhat, axis=-1, keepdims=True)) + dres_ref[...]

        @pl.when(i == 0)
        def _():
            dg_ref[...] = jnp.zeros_like(dg_ref)

        @pl.when(geo.seg_start(i, tile))
        def _():
            dsh_ref[...] = jnp.zeros_like(dsh_ref)
            dsc_ref[...] = jnp.zeros_like(dsc_ref)

        dg_ref[...] += jnp.sum(dy * xhat, axis=0, keepdims=True)
        dsh_ref[...] += jnp.sum(dout, axis=0, keepdims=True)
        dsc_ref[...] += jnp.sum(dout * (xhat * gv), axis=0, keepdims=True)

    row = pl.BlockSpec((tile, d), lambda i: (i, 0))
    vec = pl.BlockSpec((1, d), lambda i: (0, 0))
    segv = pl.BlockSpec((None, 1, d), lambda i: (geo.seg(i, tile), 0, 0))
    return pl.pallas_call(
        body, name=name, grid=(t // tile,),
        in_specs=[row, vec, _mod_spec(geo, tile, k_shift + 1, d), row, row],
        out_specs=(row, vec, segv, segv),
        out_shape=(jax.ShapeDtypeStruct((t, d), F32), jax.ShapeDtypeStruct((1, d), F32),
                   jax.ShapeDtypeStruct((N_SEG, 1, d), F32), jax.ShapeDtypeStruct((N_SEG, 1, d), F32)),
        compiler_params=_params("arbitrary"),
    )(h, g, mod4, dnx, dres)


def _gate_bwd(dh, y, mod4, k_gate, coef, geo, name):
    t, d = dh.shape
    tile = geo.tile

    def body(dh_ref, y_ref, gt_ref, dy_ref, dgt_ref):
        i = pl.program_id(0)
        dhv = dh_ref[...]
        dy_ref[...] = ((coef * gt_ref[...]) * dhv).astype(dy_ref.dtype)

        @pl.when(geo.seg_start(i, tile))
        def _():
            dgt_ref[...] = jnp.zeros_like(dgt_ref)

        dgt_ref[...] += coef * jnp.sum(dhv * y_ref[...].astype(F32), axis=0, keepdims=True)

    row = pl.BlockSpec((tile, d), lambda i: (i, 0))
    segv = pl.BlockSpec((None, 1, d), lambda i: (geo.seg(i, tile), 0, 0))
    return pl.pallas_call(
        body, name=name, grid=(t // tile,), in_specs=[row, row, _mod_spec(geo, tile, k_gate, d)],
        out_specs=(row, segv),
        out_shape=(jax.ShapeDtypeStruct((t, d), BF16), jax.ShapeDtypeStruct((N_SEG, 1, d), F32)),
        compiler_params=_params("arbitrary"),
    )(dh, y, mod4)


def _ff_tile(f):
    return _pick(f, (256, 128))


def _ffn_up(nx, w1t, w3t, s0, geo, name, hosted=None):
    t, d = nx.shape
    f = w1t.shape[1]
    tm, tn = geo.big_tile, _ff_tile(f)

    def body(x_ref, w1_ref, w3_ref, ga_ref, gb_ref, act_ref):
        xv = x_ref[...]
        a = lax.dot_general(xv, w1_ref[...], _NT, preferred_element_type=F32)
        bv = lax.dot_general(xv, w3_ref[...], _NT, preferred_element_type=F32)
        sg = jax.nn.sigmoid(a)
        silu = a * sg
        ga_ref[...] = (bv * (sg + silu * (1.0 - sg))).astype(ga_ref.dtype)
        gb_ref[...] = silu.astype(gb_ref.dtype)
        act_ref[...] = (silu * bv).astype(act_ref.dtype)

    w_spec = pl.BlockSpec((None, tn, d), lambda i, j: (s0, j, 0))
    o_spec = pl.BlockSpec((tm, tn), lambda i, j: (i, j))
    sds = jax.ShapeDtypeStruct((t, f), BF16)
    return _call(
        body, hosted, name=name, grid=(t // tm, f // tn),
        in_specs=[pl.BlockSpec((tm, d), lambda i, j: (i, 0)), w_spec, w_spec],
        out_specs=(o_spec,) * 3, out_shape=(sds,) * 3, compiler_params=_params("parallel", "parallel"),
    )(nx, w1t, w3t)


def _ffn_down(act, w2, s0, res, mod4, k_gate, geo, name, hosted=None):
    t, f = act.shape
    d = w2.shape[2]
    tm, tn = geo.mm_tile, _pick(d, (1024, 512, 256, 128))

    def body(a_ref, w_ref, res_ref, gate_ref, o_ref, y_ref):
        acc = lax.dot_general(a_ref[...], w_ref[...], _NN, preferred_element_type=F32)
        y_ref[...] = acc.astype(y_ref.dtype)
        o_ref[...] = res_ref[...] + (0.5 * gate_ref[...]) * acc

    o_spec = pl.BlockSpec((tm, tn), lambda i, j: (i, j))
    return _call(
        body, hosted, name=name, grid=(t // tm, d // tn),
        in_specs=[pl.BlockSpec((tm, f), lambda i, j: (i, 0)), pl.BlockSpec((None, f, tn), lambda i, j: (s0, 0, j)),
                  o_spec, pl.BlockSpec((None, None, 1, tn), lambda i, j: (geo.seg(i, tm), k_gate, 0, j))],
        out_specs=(o_spec, o_spec),
        out_shape=(jax.ShapeDtypeStruct((t, d), F32), jax.ShapeDtypeStruct((t, d), BF16)),
        compiler_params=_params("parallel", "parallel"),
    )(act, w2, res, mod4)


def _ffn_dact(dy, w2, ga, gb, s0, geo, name, hosted=None):
    t, d = dy.shape
    f = w2.shape[1]
    tm, tn = geo.big_tile, _ff_tile(f)

    def body(dy_ref, w_ref, ga_ref, gb_ref, da_ref, db_ref):
        dact = lax.dot_general(dy_ref[...], w_ref[...], _NT, preferred_element_type=F32)
        da_ref[...] = (dact * ga_ref[...].astype(F32)).astype(da_ref.dtype)
        db_ref[...] = (dact * gb_ref[...].astype(F32)).astype(db_ref.dtype)

    o_spec = pl.BlockSpec((tm, tn), lambda i, j: (i, j))
    sds = jax.ShapeDtypeStruct((t, f), BF16)
    return _call(
        body, hosted, name=name, grid=(t // tm, f // tn),
        in_specs=[pl.BlockSpec((tm, d), lambda i, j: (i, 0)), pl.BlockSpec((None, tn, d), lambda i, j: (s0, j, 0)),
                  o_spec, o_spec],
        out_specs=(o_spec, o_spec), out_shape=(sds, sds), compiler_params=_params("parallel", "parallel"),
    )(dy, w2, ga, gb)


def _ffn_dnx(da, db, w1t, w3t, s0, geo, name, hosted=None):
    t, f = da.shape
    d = w1t.shape[2]
    tm, tn = geo.mm_tile, _pick(d, (1024, 512, 256, 128))

    def body(da_ref, db_ref, w1_ref, w3_ref, o_ref):
        o_ref[...] = (lax.dot_general(da_ref[...], w1_ref[...], _NN, preferred_element_type=F32)
                      + lax.dot_general(db_ref[...], w3_ref[...], _NN, preferred_element_type=F32))

    x_spec = pl.BlockSpec((tm, f), lambda j, i: (i, 0))
    w_spec = pl.BlockSpec((None, f, tn), lambda j, i: (s0, 0, j))
    return _call(
        body, hosted, name=name, grid=(d // tn, t // tm), in_specs=[x_spec, x_spec, w_spec, w_spec],
        out_specs=pl.BlockSpec((tm, tn), lambda j, i: (i, j)), out_shape=jax.ShapeDtypeStruct((t, d), F32),
        compiler_params=_params("parallel", "parallel"),
    )(da, db, w1t, w3t)


def _with_host(fn, hosts, got, slot, *args, **kw):
    hosted = hosts.get(slot)
    if hosted is None:
        return fn(*args, **kw)
    out, got[slot] = fn(*args, hosted=hosted, **kw)
    return out


def _ffn_fwd(h, g, mod4, k0, w, s0, geo, tag, sub, hosts, got):
    nx = _pre_fwd(h, g, mod4, k0, geo, f"{tag}_pre")
    a, b, act = _with_host(_ffn_up, hosts, got, f"{sub}_up", nx, w["ffn_w1"], w["ffn_w3"], s0, geo, f"{tag}_up")
    h_out, y = _with_host(_ffn_down, hosts, got, f"{sub}_down", act, w["ffn_w2"], s0, h, mod4, k0 + 2, geo, f"{tag}_down")
    return h_out, (h, nx, a, b, act, y)


def _ffn_bwd(dh_out, saved, g, mod4, k0, w, s0, gbuf, geo, tag, sub, hosts, got):
    h, nx, a, b, act, y = saved
    dy, dgate = _gate_bwd(dh_out, y, mod4, k0 + 2, 0.5, geo, f"{tag}_dgate")
    da, db = _with_host(_ffn_dact, hosts, got, f"{sub}_dact", dy, w["ffn_w2"], a, b, s0, geo, f"{tag}_dact")
    gbuf["ffn_w2"] = _with_host(_tn_wide, hosts, got, f"{sub}_dw2", act, dy, f"{tag}_dw2", into=gbuf["ffn_w2"], s0=s0)
    dnx = _with_host(_ffn_dnx, hosts, got, f"{sub}_dnx", da, db, w["ffn_w1"], w["ffn_w3"], s0, geo, f"{tag}_dnx")
    gbuf["ffn_w1"] = _tn_wide(da, nx, f"{tag}_dw1", into=gbuf["ffn_w1"], s0=s0)
    gbuf["ffn_w3"] = _tn_wide(db, nx, f"{tag}_dw3", into=gbuf["ffn_w3"], s0=s0)
    dh, dg, dshift, dscale = _pre_bwd(h, g, mod4, k0, dnx, dh_out, geo, f"{tag}_dpre")
    return dh, dg, (dshift, dscale, dgate)


def _interleave(w, n_parts, tile):
    lead, cols = w.shape[:-1], w.shape[-1] // n_parts
    return w.reshape(*lead, n_parts, cols // tile, tile).swapaxes(-3, -2).reshape(*lead, n_parts * cols)


def _deinterleave(w, n_parts, tile):
    lead, cols = w.shape[:-1], w.shape[-1] // n_parts
    return w.reshape(*lead, cols // tile, n_parts, tile).swapaxes(-3, -2).reshape(*lead, n_parts * cols)


HALO = 16


def _conv_tile(c):
    return _pick(c, (256, 128))


def _conv_specs(geo, tc, t):
    tile = geo.tile
    per = tile // HALO
    last = t // HALO - 1
    cur = pl.BlockSpec((tile, 3 * tc), lambda j, i: (i, j))
    prev = pl.BlockSpec((HALO, 3 * tc), lambda j, i: (jnp.maximum(i * per - 1, 0), j))
    nxt = pl.BlockSpec((HALO, 3 * tc), lambda j, i: (jnp.minimum((i + 1) * per, last), j))
    return cur, prev, nxt


def _conv_edges(geo, i):
    tile = geo.tile
    row = i * tile
    lat = row < 2 * geo.n_lat
    first = jnp.where(lat, row % geo.n_lat == 0, (row - 2 * geo.n_lat) % geo.n_ctx == 0)
    end = row + tile
    last = jnp.where(lat, end % geo.n_lat == 0, (end - 2 * geo.n_lat) % geo.n_ctx == 0)
    return first, last


def _shift_rows(v, before, after):
    n = v.shape[0]
    rows = lax.broadcasted_iota(jnp.int32, v.shape, 0)
    down = jnp.where(rows == 0, before, pltpu.roll(v, 1, 0))
    up = jnp.where(rows == n - 1, after, pltpu.roll(v, n - 1, 0))
    return down, up


def _conv_fwd(proj, conv_w, geo, name, hosted=None):
    t, c3 = proj.shape
    c = c3 // 3
    tc, tile = _conv_tile(c), geo.tile

    def body(cur_ref, prev_ref, next_ref, w_ref, o_ref):
        first, last = _conv_edges(geo, pl.program_id(1))
        bv = cur_ref[:, :tc].astype(F32)
        p = cur_ref[:, tc:2 * tc].astype(F32) * cur_ref[:, 2 * tc:].astype(F32)
        p_before = prev_ref[HALO - 1:HALO, tc:2 * tc].astype(F32) * prev_ref[HALO - 1:HALO, 2 * tc:].astype(F32)
        p_after = next_ref[0:1, tc:2 * tc].astype(F32) * next_ref[0:1, 2 * tc:].astype(F32)
        p_before = jnp.where(first, 0.0, p_before)
        p_after = jnp.where(last, 0.0, p_after)
        pm1, pp1 = _shift_rows(p, p_before, p_after)
        w = w_ref[...]
        q = w[0:1] * pm1 + w[1:2] * p + w[2:3] * pp1
        o_ref[...] = (bv * q).astype(o_ref.dtype)

    cur, prev, nxt = _conv_specs(geo, tc, t)
    return _call(
        body, hosted, name=name, grid=(c // tc, t // tile),
        in_specs=[cur, prev, nxt, pl.BlockSpec((3, tc), lambda j, i: (0, j))],
        out_specs=pl.BlockSpec((tile, tc), lambda j, i: (i, j)), out_shape=jax.ShapeDtypeStruct((t, c), BF16),
        compiler_params=_params("parallel", "parallel"),
    )(proj, proj, proj, conv_w)


def _conv_bwd(proj, dyc, conv_w, geo, name, hosted=None):
    t, c3 = proj.shape
    c = c3 // 3
    tc, tile = _conv_tile(c), geo.tile

    def body(cur_ref, prev_ref, next_ref, d_ref, dprev_ref, dnext_ref, w_ref, o_ref, dw_ref):
        i = pl.program_id(1)
        first, last = _conv_edges(geo, i)
        bv = cur_ref[:, :tc].astype(F32)
        cv = cur_ref[:, tc:2 * tc].astype(F32)
        uv = cur_ref[:, 2 * tc:].astype(F32)
        p = cv * uv
        p_before = prev_ref[HALO - 1:HALO, tc:2 * tc].astype(F32) * prev_ref[HALO - 1:HALO, 2 * tc:].astype(F32)
        p_after = next_ref[0:1, tc:2 * tc].astype(F32) * next_ref[0:1, 2 * tc:].astype(F32)
        p_before = jnp.where(first, 0.0, p_before)
        p_after = jnp.where(last, 0.0, p_after)
        pm1, pp1 = _shift_rows(p, p_before, p_after)
        w = w_ref[...]
        q = w[0:1] * pm1 + w[1:2] * p + w[2:3] * pp1
        dy = d_ref[...].astype(F32)
        dq = dy * bv
        dq_before = dprev_ref[HALO - 1:HALO, :].astype(F32) * prev_ref[HALO - 1:HALO, :tc].astype(F32)
        dq_after = dnext_ref[0:1, :].astype(F32) * next_ref[0:1, :tc].astype(F32)
        dq_before = jnp.where(first, 0.0, dq_before)
        dq_after = jnp.where(last, 0.0, dq_after)
        dqm1, dqp1 = _shift_rows(dq, dq_before, dq_after)
        dp = w[0:1] * dqp1 + w[1:2] * dq + w[2:3] * dqm1
        o_ref[:, :tc] = (dy * q).astype(o_ref.dtype)
        o_ref[:, tc:2 * tc] = (dp * uv).astype(o_ref.dtype)
        o_ref[:, 2 * tc:] = (dp * cv).astype(o_ref.dtype)

        @pl.when(i == 0)
        def _():
            dw_ref[...] = jnp.zeros_like(dw_ref)

        dw_ref[0:1, :] += jnp.sum(dq * pm1, axis=0, keepdims=True)
        dw_ref[1:2, :] += jnp.sum(dq * p, axis=0, keepdims=True)
        dw_ref[2:3, :] += jnp.sum(dq * pp1, axis=0, keepdims=True)

    cur, prev, nxt = _conv_specs(geo, tc, t)
    per, lastb = tile // HALO, t // HALO - 1
    dcur = pl.BlockSpec((tile, tc), lambda j, i: (i, j))
    dprev = pl.BlockSpec((HALO, tc), lambda j, i: (jnp.maximum(i * per - 1, 0), j))
    dnext = pl.BlockSpec((HALO, tc), lambda j, i: (jnp.minimum((i + 1) * per, lastb), j))
    wspec = pl.BlockSpec((3, tc), lambda j, i: (0, j))
    return _call(
        body, hosted, name=name, grid=(c // tc, t // tile), in_specs=[cur, prev, nxt, dcur, dprev, dnext, wspec],
        out_specs=(cur, wspec), out_shape=(jax.ShapeDtypeStruct((t, c3), BF16), jax.ShapeDtypeStruct((3, c), F32)),
        compiler_params=_params("parallel", "arbitrary"),
    )(proj, proj, proj, dyc, dyc, dyc, conv_w)


def _sconv_fwd(h, g, mod4, w_in, conv_w, w_out, geo, tag, hosts, got):
    nx = _pre_fwd(h, g, mod4, 3, geo, f"{tag}_pre")
    proj = _with_host(_mm, hosts, got, "mix_a", nx, w_in, out_dtype=BF16, name=f"{tag}_in")
    yc = _with_host(_conv_fwd, hosts, got, "mix_b", proj, conv_w, geo, f"{tag}_conv")
    h_out, y = _mm(yc, w_out, name=f"{tag}_out", gate=(h, mod4, 5, 1.0, geo))
    return h_out, (h, nx, proj, yc, y)


def _sconv_bwd(dh_out, saved, g, mod4, w_in, conv_w, w_out, geo, tag, hosts, got):
    h, nx, proj, yc, y = saved
    dy, dgate = _gate_bwd(dh_out, y, mod4, 5, 1.0, geo, f"{tag}_dgate")
    dyc = _with_host(_mm, hosts, got, "mix_d", dy, w_out, tb=True, out_dtype=BF16, name=f"{tag}_dyc")
    dw_out = _tn_wide(yc, dy, f"{tag}_dwout")
    dproj, dconv = _with_host(_conv_bwd, hosts, got, "mix_c", proj, dyc, conv_w, geo, f"{tag}_dconv")
    dnx = _with_host(_mm, hosts, got, "mix_b", dproj, w_in, tb=True, name=f"{tag}_dnx")
    dw_in = _with_host(_tn_wide, hosts, got, "mix_a", nx, dproj, f"{tag}_dwin")
    dh, dg, dshift, dscale = _pre_bwd(h, g, mod4, 3, dnx, dh_out, geo, f"{tag}_dpre")
    return dh, dg, (dshift, dscale, dgate), dw_in, dconv, dw_out


def _rope_swap(v):
    nf = QK_ROPE // 4
    return v.reshape(v.shape[:-1] + (2, 2, nf)).swapaxes(-3, -2).reshape(v.shape)


def _rope_tables(geo):
    n = geo.n_lat
    nf = QK_ROPE // 4
    pos = np.arange(n)
    inv = ROPE_BASE ** (-np.arange(nf, dtype=np.float32) / nf)
    ang = np.concatenate([(pos // GRID_W)[:, None] * inv, (pos % GRID_W)[:, None] * inv], axis=1).astype(np.float32)
    cos, sin = np.cos(ang), np.sin(ang)
    zeros = np.zeros((n, LANE - QK_ROPE), np.float32)
    c_lat = np.concatenate([cos, cos, zeros], axis=1)
    s_lat = np.concatenate([-sin, sin, zeros], axis=1)
    c_ctx = np.concatenate([np.ones((2 * geo.n_ctx, QK_ROPE), np.float32), np.zeros((2 * geo.n_ctx, LANE - QK_ROPE), np.float32)], 1)
    s_ctx = np.zeros((2 * geo.n_ctx, LANE), np.float32)
    return (jnp.asarray(np.concatenate([c_lat, c_lat, c_ctx], 0)), jnp.asarray(np.concatenate([s_lat, s_lat, s_ctx], 0)))


def _swap_halves(v):
    lanes = lax.broadcasted_iota(jnp.int32, v.shape, 1)
    return jnp.where(lanes < QK_ROPE // 2, pltpu.roll(v, LANE - QK_ROPE // 2, 1), pltpu.roll(v, QK_ROPE // 2, 1))


def _latent_norm_fwd(down, g_qa, g_kva, geo, name):
    t, wd = down.shape
    ql, kl = g_qa.shape[1], g_kva.shape[1]
    tile = geo.tile

    def body(d_ref, gq_ref, gk_ref, cq_ref, ckv_ref):
        for lo, n, g_ref, o_ref in ((0, ql, gq_ref, cq_ref), (ql, kl, gk_ref, ckv_ref)):
            x = d_ref[:, lo:lo + n]
            r = lax.rsqrt(jnp.mean(x * x, axis=-1, keepdims=True) + EPS)
            o_ref[...] = (x * r * g_ref[...]).astype(o_ref.dtype)

    return pl.pallas_call(
        body, name=name, grid=(t // tile,),
        in_specs=[pl.BlockSpec((tile, wd), lambda i: (i, 0)), pl.BlockSpec((1, ql), lambda i: (0, 0)),
                  pl.BlockSpec((1, kl), lambda i: (0, 0))],
        out_specs=(pl.BlockSpec((tile, ql), lambda i: (i, 0)), pl.BlockSpec((tile, kl), lambda i: (i, 0))),
        out_shape=(jax.ShapeDtypeStruct((t, ql), BF16), jax.ShapeDtypeStruct((t, kl), BF16)),
        compiler_params=_params("parallel"),
    )(down, g_qa, g_kva)


def _latent_norm_bwd(down, g_qa, g_kva, dcqn, dckvn, dkr, geo, name):
    t, wd = down.shape
    ql, kl = g_qa.shape[1], g_kva.shape[1]
    tile = geo.tile

    def body(d_ref, gq_ref, gk_ref, dq_ref, dk_ref, dkr_ref, o_ref, dgq_ref, dgk_ref):
        i = pl.program_id(0)

        @pl.when(i == 0)
        def _():
            dgq_ref[...] = jnp.zeros_like(dgq_ref)
            dgk_ref[...] = jnp.zeros_like(dgk_ref)

        for lo, n, g_ref, dy_ref, dg_ref in ((0, ql, gq_ref, dq_ref, dgq_ref), (ql, kl, gk_ref, dk_ref, dgk_ref)):
            x = d_ref[:, lo:lo + n]
            dy = dy_ref[...].astype(F32)
            r = lax.rsqrt(jnp.mean(x * x, axis=-1, keepdims=True) + EPS)
            xhat = x * r
            u = dy * g_ref[...]
            o_ref[:, lo:lo + n] = (r * (u - xhat * jnp.mean(u * xhat, axis=-1, keepdims=True))).astype(o_ref.dtype)
            dg_ref[...] += jnp.sum(dy * xhat, axis=0, keepdims=True)
        o_ref[:, ql + kl:] = dkr_ref[...].astype(o_ref.dtype)

    def row(n):
        return pl.BlockSpec((tile, n), lambda i: (i, 0))

    def vec(n):
        return pl.BlockSpec((1, n), lambda i: (0, 0))

    return pl.pallas_call(
        body, name=name, grid=(t // tile,),
        in_specs=[row(wd), vec(ql), vec(kl), row(ql), row(kl), row(wd - ql - kl)],
        out_specs=(row(wd), vec(ql), vec(kl)),
        out_shape=(jax.ShapeDtypeStruct((t, wd), BF16), jax.ShapeDtypeStruct((1, ql), F32),
                   jax.ShapeDtypeStruct((1, kl), F32)),
        compiler_params=_params("arbitrary"),
    )(down, g_qa, g_kva, dcqn, dckvn, dkr)


def _qk_specs(geo, xr_col, shared_rope):
    tile = geo.mm_tile
    xn_spec = pl.BlockSpec((tile, HEADS * LANE), lambda i: (i, 0))
    if shared_rope:
        xr_spec = pl.BlockSpec((tile, LANE), lambda i: (i, xr_col))
    else:
        xr_spec = pl.BlockSpec((tile, HEADS * LANE), lambda i: (i, xr_col // HEADS))
    vec = pl.BlockSpec((1, LANE), lambda i: (0, 0))
    tab = pl.BlockSpec((tile, LANE), lambda i: (i, 0))
    return tile, xn_spec, xr_spec, vec, tab


def _qk_norm(xn, xr):
    ss = jnp.sum(xn * xn, axis=-1, keepdims=True) + jnp.sum(xr * xr, axis=-1, keepdims=True)
    return lax.rsqrt(ss * (1.0 / QK_HEAD) + EPS)


def _head_lanes(ref, hh, shared=False):
    return ref[...] if shared else ref[:, hh * LANE:(hh + 1) * LANE]


def _qk_fwd(xn_arr, xr_arr, xr_col, shared_rope, gn, gr, cos, sin, geo, name):
    t = xn_arr.shape[0]
    tile, xn_spec, xr_spec, vec, tab = _qk_specs(geo, xr_col, shared_rope)

    def body(xn_ref, xr_ref, gn_ref, gr_ref, c_ref, s_ref, o_ref):
        cv, sv, gnv, grv = c_ref[...], s_ref[...], gn_ref[...], gr_ref[...]
        for hh in range(HEADS):
            xn = _head_lanes(xn_ref, hh).astype(F32)
            xr = _head_lanes(xr_ref, hh, shared_rope).astype(F32)
            r = _qk_norm(xn, xr)
            yr = xr * r * grv
            o_ref[:, hh * HEAD_PAD:hh * HEAD_PAD + LANE] = (xn * r * gnv).astype(o_ref.dtype)
            o_ref[:, hh * HEAD_PAD + LANE:(hh + 1) * HEAD_PAD] = (yr * cv + _swap_halves(yr) * sv).astype(o_ref.dtype)

    return pl.pallas_call(
        body, name=name, grid=(t // tile,), in_specs=[xn_spec, xr_spec, vec, vec, tab, tab],
        out_specs=pl.BlockSpec((tile, HEADS * HEAD_PAD), lambda i: (i, 0)),
        out_shape=jax.ShapeDtypeStruct((t, HEADS * HEAD_PAD), BF16), compiler_params=_params("parallel"),
    )(xn_arr, xr_arr, gn, gr, cos, sin)


def _qk_bwd(xn_arr, xr_arr, xr_col, shared_rope, gn, gr, cos, sin, dout, geo, name):
    t = xn_arr.shape[0]
    tile, xn_spec, xr_spec, vec, tab = _qk_specs(geo, xr_col, shared_rope)
    half = HEADS * LANE
    if shared_rope:
        n_lat_tiles = dout[0].shape[0] // tile
        assert dout[0].shape[0] % tile == 0 and dout[1].shape[0] % tile == 0

    def body(*refs):
        if shared_rope:
            xn_ref, xr_ref, gn_ref, gr_ref, c_ref, s_ref, dl_ref, dc_ref, vl_ref, vc_ref, raw_ref, dxr_ref, dgn_ref, dgr_ref = refs
        else:
            xn_ref, xr_ref, gn_ref, gr_ref, c_ref, s_ref, d_ref, raw_ref, dgn_ref, dgr_ref = refs
        i = pl.program_id(0)
        cv, sv, gnv, grv = c_ref[...], s_ref[...], gn_ref[...], gr_ref[...]
        dgn = jnp.zeros((1, LANE), F32)
        dgr = jnp.zeros((1, LANE), F32)
        dxr_sum = jnp.zeros((tile, LANE), F32)
        if shared_rope:
            latent = i < n_lat_tiles
            raw_ref[:, half:] = jnp.where(latent, vl_ref[...], vc_ref[...])
        for hh in range(HEADS):
            xn = _head_lanes(xn_ref, hh).astype(F32)
            xr = _head_lanes(xr_ref, hh, shared_rope).astype(F32)
            r = _qk_norm(xn, xr)
            xhn, xhr = xn * r, xr * r
            lo = hh * HEAD_PAD
            if shared_rope:
                dhead = jnp.where(latent, dl_ref[:, lo:lo + HEAD_PAD], dc_ref[:, lo:lo + HEAD_PAD]).astype(F32)
            else:
                dhead = d_ref[:, lo:lo + HEAD_PAD].astype(F32)
            dyn, dro = dhead[:, :LANE], dhead[:, LANE:]
            dyr = dro * cv + _swap_halves(dro * sv)
            un, ur = dyn * gnv, dyr * grv
            mean = (jnp.sum(un * xhn, axis=-1, keepdims=True) + jnp.sum(ur * xhr, axis=-1, keepdims=True)) * (1.0 / QK_HEAD)
            raw_ref[:, hh * LANE:(hh + 1) * LANE] = (r * (un - xhn * mean)).astype(raw_ref.dtype)
            dxr = r * (ur - xhr * mean)
            if shared_rope:
                dxr_sum = dxr_sum + dxr
            else:
                raw_ref[:, half + hh * LANE:half + (hh + 1) * LANE] = dxr.astype(raw_ref.dtype)
            dgn = dgn + jnp.sum(dyn * xhn, axis=0, keepdims=True)
            dgr = dgr + jnp.sum(dyr * xhr, axis=0, keepdims=True)
        if shared_rope:
            dxr_ref[...] = dxr_sum

        @pl.when(i == 0)
        def _():
            dgn_ref[...] = jnp.zeros_like(dgn_ref)
            dgr_ref[...] = jnp.zeros_like(dgr_ref)

        dgn_ref[...] += dgn
        dgr_ref[...] += dgr

    raw_spec = pl.BlockSpec((tile, 2 * half), lambda i: (i, 0))
    raw_shape = jax.ShapeDtypeStruct((t, 2 * half), BF16)
    vec_shape = jax.ShapeDtypeStruct((1, LANE), F32)
    in_specs = [xn_spec, xr_spec, vec, vec, tab, tab]
    if shared_rope:
        def two(width):
            return [pl.BlockSpec((tile, width), lambda i: (jnp.minimum(i, n_lat_tiles - 1), 0)),
                    pl.BlockSpec((tile, width), lambda i: (jnp.maximum(i - n_lat_tiles, 0), 0))]

        return pl.pallas_call(
            body, name=name, grid=(t // tile,), in_specs=in_specs + two(HEADS * HEAD_PAD) + two(half),
            out_specs=(raw_spec, pl.BlockSpec((tile, LANE), lambda i: (i, 0)), vec, vec),
            out_shape=(raw_shape, jax.ShapeDtypeStruct((t, LANE), F32), vec_shape, vec_shape),
            compiler_params=_params("arbitrary"),
        )(xn_arr, xr_arr, gn, gr, cos, sin, *dout)
    return pl.pallas_call(
        body, name=name, grid=(t // tile,),
        in_specs=in_specs + [pl.BlockSpec((tile, HEADS * HEAD_PAD), lambda i: (i, 0))],
        out_specs=(raw_spec, vec, vec), out_shape=(raw_shape, vec_shape, vec_shape),
        compiler_params=_params("arbitrary"),
    )(xn_arr, xr_arr, gn, gr, cos, sin, dout)


def _attn_specs(geo):
    tq, nq = geo.n_ctx, geo.n_lat // geo.n_ctx

    def qrow(b, i):
        return jnp.where(i < nq, b * nq + i, 2 * nq + b)

    q_spec = pl.BlockSpec((tq, HEAD_PAD), lambda b, hh, i: (qrow(b, i), hh))
    kc_spec = pl.BlockSpec((geo.n_ctx, HEAD_PAD), lambda b, hh, i: (2 * nq + b, hh))
    kl_spec = pl.BlockSpec((geo.n_lat, HEAD_PAD), lambda b, hh, i: (b, hh))
    vc_spec = pl.BlockSpec((geo.n_ctx, V_HEAD), lambda b, hh, i: (2 * nq + b, HEADS + hh))
    vl_spec = pl.BlockSpec((geo.n_lat, V_HEAD), lambda b, hh, i: (b, HEADS + hh))
    o_spec = pl.BlockSpec((tq, V_HEAD), lambda b, hh, i: (qrow(b, i), hh))
    return tq, nq, q_spec, kc_spec, kl_spec, vc_spec, vl_spec, o_spec


def _attn_fwd(q, k, kv, with_ctx_q, geo, name, hosted=None):
    t = q.shape[0]
    tq, nq, q_spec, kc_spec, kl_spec, vc_spec, vl_spec, o_spec = _attn_specs(geo)

    def body(q_ref, kc_ref, kl_ref, vc_ref, vl_ref, o_ref):
        i = pl.program_id(2)
        qv = q_ref[...]
        s_c = _dot(qv, kc_ref[...], _NT) * QK_SCALE

        @pl.when(i < nq)
        def _():
            s_l = _dot(qv, kl_ref[...], _NT) * QK_SCALE
            m = jnp.maximum(jnp.max(s_c, axis=-1, keepdims=True), jnp.max(s_l, axis=-1, keepdims=True))
            p_c, p_l = jnp.exp(s_c - m), jnp.exp(s_l - m)
            den = jnp.sum(p_c, axis=-1, keepdims=True) + jnp.sum(p_l, axis=-1, keepdims=True)
            o = _dot(p_c, vc_ref[...], _NN) + _dot(p_l, vl_ref[...], _NN)
            o_ref[...] = (o / den).astype(o_ref.dtype)

        @pl.when(i == nq)
        def _():
            if with_ctx_q:
                m = jnp.max(s_c, axis=-1, keepdims=True)
                p_c = jnp.exp(s_c - m)
                o = _dot(p_c, vc_ref[...], _NN) / jnp.sum(p_c, axis=-1, keepdims=True)
                o_ref[...] = o.astype(o_ref.dtype)
            else:
                o_ref[...] = jnp.zeros_like(o_ref)

    return _call(
        body, hosted, name=name, grid=(2, HEADS, nq + 1), in_specs=[q_spec, kc_spec, kl_spec, vc_spec, vl_spec],
        out_specs=o_spec, out_shape=jax.ShapeDtypeStruct((t, HEADS * V_HEAD), BF16),
        compiler_params=_params("parallel", "parallel", "arbitrary"),
    )(q, k, k, kv, kv)


def _attn_bwd(q, k, kv, do, with_ctx_q, geo, name, hosted=None):
    t = q.shape[0]
    tq, nq, q_spec, kc_spec, kl_spec, vc_spec, vl_spec, o_spec = _attn_specs(geo)

    def body(q_ref, kc_ref, kl_ref, vc_ref, vl_ref, do_ref, dq_ref, dkl_ref, dkc_ref, dvl_ref, dvc_ref,
             akl_ref, akc_ref, avl_ref, avc_ref):
        i = pl.program_id(2)

        @pl.when(i == 0)
        def _():
            for ref in (akl_ref, akc_ref, avl_ref, avc_ref):
                ref[...] = jnp.zeros_like(ref)

        qv, dov = q_ref[...], do_ref[...]
        s_c = _dot(qv, kc_ref[...], _NT) * QK_SCALE
        dp_c = _dot(dov, vc_ref[...], _NT)

        def ctx_part(p_c, delta):
            ds_c = (p_c * (dp_c - delta) * QK_SCALE).astype(BF16)
            akc_ref[...] += _dot(ds_c, qv, _TN)
            avc_ref[...] += _dot(p_c, dov, _TN)
            return _dot(ds_c, kc_ref[...], _NN)

        @pl.when(i < nq)
        def _():
            s_l = _dot(qv, kl_ref[...], _NT) * QK_SCALE
            m = jnp.maximum(jnp.max(s_c, axis=-1, keepdims=True), jnp.max(s_l, axis=-1, keepdims=True))
            p_c, p_l = jnp.exp(s_c - m), jnp.exp(s_l - m)
            inv = 1.0 / (jnp.sum(p_c, axis=-1, keepdims=True) + jnp.sum(p_l, axis=-1, keepdims=True))
            p_c, p_l = p_c * inv, p_l * inv
            dp_l = _dot(dov, vl_ref[...], _NT)
            delta = jnp.sum(p_c * dp_c, axis=-1, keepdims=True) + jnp.sum(p_l * dp_l, axis=-1, keepdims=True)
            ds_l = (p_l * (dp_l - delta) * QK_SCALE).astype(BF16)
            akl_ref[...] += _dot(ds_l, qv, _TN)
            avl_ref[...] += _dot(p_l, dov, _TN)
            dq_ref[...] = (ctx_part(p_c, delta) + _dot(ds_l, kl_ref[...], _NN)).astype(dq_ref.dtype)

        @pl.when(i == nq)
        def _():
            if with_ctx_q:
                m = jnp.max(s_c, axis=-1, keepdims=True)
                p_c = jnp.exp(s_c - m)
                p_c = p_c * (1.0 / jnp.sum(p_c, axis=-1, keepdims=True))
                delta = jnp.sum(p_c * dp_c, axis=-1, keepdims=True)
                dq_ref[...] = ctx_part(p_c, delta).astype(dq_ref.dtype)
            else:
                dq_ref[...] = jnp.zeros_like(dq_ref)
            dkl_ref[...] = akl_ref[...].astype(dkl_ref.dtype)
            dkc_ref[...] = akc_ref[...].astype(dkc_ref.dtype)
            dvl_ref[...] = avl_ref[...].astype(dvl_ref.dtype)
            dvc_ref[...] = avc_ref[...].astype(dvc_ref.dtype)

    def acc_spec(rows, width):
        return pl.BlockSpec((rows, width), lambda b, hh, i: (b, hh))

    return _call(
        body, hosted, name=name, grid=(2, HEADS, nq + 1), in_specs=[q_spec, kc_spec, kl_spec, vc_spec, vl_spec, o_spec],
        out_specs=(q_spec, acc_spec(geo.n_lat, HEAD_PAD), acc_spec(geo.n_ctx, HEAD_PAD), acc_spec(geo.n_lat, V_HEAD),
                   acc_spec(geo.n_ctx, V_HEAD)),
        out_shape=(jax.ShapeDtypeStruct((t, HEADS * HEAD_PAD), BF16),
                   jax.ShapeDtypeStruct((2 * geo.n_lat, HEADS * HEAD_PAD), BF16),
                   jax.ShapeDtypeStruct((2 * geo.n_ctx, HEADS * HEAD_PAD), BF16),
                   jax.ShapeDtypeStruct((2 * geo.n_lat, HEADS * V_HEAD), BF16),
                   jax.ShapeDtypeStruct((2 * geo.n_ctx, HEADS * V_HEAD), BF16)),
        scratch_shapes=[pltpu.VMEM((geo.n_lat, HEAD_PAD), F32), pltpu.VMEM((geo.n_ctx, HEAD_PAD), F32),
                        pltpu.VMEM((geo.n_lat, V_HEAD), F32), pltpu.VMEM((geo.n_ctx, V_HEAD), F32)],
        compiler_params=_params("parallel", "parallel", "arbitrary"),
    )(q, k, k, kv, kv, do)


def _mla_fwd(h, g, mod4, w, with_ctx_q, tabs, geo, tag, hosts, got):
    cos, sin = tabs
    ql, kl = w["g_qa"].shape[1], w["g_kva"].shape[1]
    kr_col = (ql + kl) // LANE
    nx = _pre_fwd(h, g, mod4, 3, geo, f"{tag}_pre")
    down = _mm(nx, w["w_a"], name=f"{tag}_down")
    cqn, ckvn = _latent_norm_fwd(down, w["g_qa"], w["g_kva"], geo, f"{tag}_lnorm")
    qraw = _mm(cqn, w["w_uq"], out_dtype=BF16, name=f"{tag}_uq")
    kvraw = _mm(ckvn, w["w_ukv"], out_dtype=BF16, name=f"{tag}_ukv")
    q = _qk_fwd(qraw, qraw, HEADS, False, w["gq_n"], w["gq_r"], cos, sin, geo, f"{tag}_qnorm")
    k = _qk_fwd(kvraw, down, kr_col, True, w["gk_n"], w["gk_r"], cos, sin, geo, f"{tag}_knorm")
    o = _with_host(_attn_fwd, hosts, got, "mix_a", q, k, kvraw, with_ctx_q, geo, f"{tag}_attn")
    h_out, y = _mm(o, w["w_o"], name=f"{tag}_o", gate=(h, mod4, 5, 1.0, geo))
    return h_out, (h, nx, down, cqn, ckvn, qraw, kvraw, q, k, o, y)


def _mla_bwd(dh_out, saved, g, mod4, w, with_ctx_q, tabs, geo, tag, hosts, got):
    cos, sin = tabs
    h, nx, down, cqn, ckvn, qraw, kvraw, q, k, o, y = saved
    ql, kl = w["g_qa"].shape[1], w["g_kva"].shape[1]
    kr_col = (ql + kl) // LANE
    dy, dgate = _gate_bwd(dh_out, y, mod4, 5, 1.0, geo, f"{tag}_dgate")
    do = _mm(dy, w["w_o"], tb=True, out_dtype=BF16, name=f"{tag}_do")
    dw_o = _tn_wide(o, dy, f"{tag}_dwo")
    dq, dk_lat, dk_ctx, dv_lat, dv_ctx = _with_host(_attn_bwd, hosts, got, "mix_a", q, k, kvraw, do, with_ctx_q, geo,
                                                    f"{tag}_dattn")
    dqraw, dgq_n, dgq_r = _qk_bwd(qraw, qraw, HEADS, False, w["gq_n"], w["gq_r"], cos, sin, dq, geo, f"{tag}_dqnorm")
    dkvraw, dkr, dgk_n, dgk_r = _qk_bwd(kvraw, down, kr_col, True, w["gk_n"], w["gk_r"], cos, sin,
                                        (dk_lat, dk_ctx, dv_lat, dv_ctx), geo, f"{tag}_dknorm")
    dcqn = _mm(dqraw, w["w_uq"], tb=True, out_dtype=BF16, name=f"{tag}_dcqn")
    dw_uq = _tn_wide(cqn, dqraw, f"{tag}_dwuq")
    dckvn = _mm(dkvraw, w["w_ukv"], tb=True, out_dtype=BF16, name=f"{tag}_dckvn")
    dw_ukv = _tn_wide(ckvn, dkvraw, f"{tag}_dwukv")
    ddown, dg_qa, dg_kva = _latent_norm_bwd(down, w["g_qa"], w["g_kva"], dcqn, dckvn, dkr, geo, f"{tag}_dlnorm")
    dnx = _mm(ddown, w["w_a"], tb=True, name=f"{tag}_dnx")
    dw_a = _tn_wide(nx, ddown, f"{tag}_dwa")
    dh, dg, dshift, dscale = _pre_bwd(h, g, mod4, 3, dnx, dh_out, geo, f"{tag}_dpre")
    grads = dict(w_a=dw_a, g_qa=dg_qa, w_uq=dw_uq, g_kva=dg_kva, w_ukv=dw_ukv, gq_n=dgq_n, gq_r=dgq_r, gk_n=dgk_n,
                 gk_r=dgk_r, w_o=dw_o)
    return dh, dg, (dshift, dscale, dgate), grads


def _mla_prepare(w_a, g_qa, w_uq, g_kva, w_ukv, g_q, g_k, w_o):
    ql, kl = g_qa.shape[0], g_kva.shape[0]
    d = w_a.shape[0]
    w_a_pad = jnp.concatenate([w_a[:, :ql + kl], _rope_swap(w_a[:, ql + kl:]), jnp.zeros((d, LANE - QK_ROPE), w_a.dtype)], axis=1)
    uq = w_uq.reshape(ql, HEADS, QK_HEAD)
    uq_r = jnp.pad(_rope_swap(uq[:, :, QK_NOPE:]), ((0, 0), (0, 0), (0, LANE - QK_ROPE)))
    w_uq_pad = jnp.concatenate([uq[:, :, :QK_NOPE].reshape(ql, HEADS * LANE), uq_r.reshape(ql, HEADS * LANE)], axis=1)
    ukv = w_ukv.reshape(kl, HEADS, QK_NOPE + V_HEAD)
    w_ukv_p = jnp.concatenate([ukv[:, :, :QK_NOPE].reshape(kl, HEADS * LANE), ukv[:, :, QK_NOPE:].reshape(kl, HEADS * V_HEAD)], axis=1)

    def gains(gv):
        gv = gv.astype(F32)
        return gv[None, :QK_NOPE], jnp.pad(_rope_swap(gv[QK_NOPE:]), (0, LANE - QK_ROPE))[None]

    gq_n, gq_r = gains(g_q)
    gk_n, gk_r = gains(g_k)
    return dict(w_a=w_a_pad, g_qa=g_qa.astype(F32)[None], w_uq=w_uq_pad, g_kva=g_kva.astype(F32)[None], w_ukv=w_ukv_p,
                gq_n=gq_n, gq_r=gq_r, gk_n=gk_n, gk_r=gk_r, w_o=w_o)


def _mla_unprepare(gr):
    ql, kl = gr["g_qa"].shape[1], gr["g_kva"].shape[1]
    dw_a = jnp.concatenate([gr["w_a"][:, :ql + kl], _rope_swap(gr["w_a"][:, ql + kl:ql + kl + QK_ROPE])], axis=1)
    uqn = gr["w_uq"][:, :HEADS * LANE].reshape(ql, HEADS, LANE)
    uqr = _rope_swap(gr["w_uq"][:, HEADS * LANE:].reshape(ql, HEADS, LANE)[:, :, :QK_ROPE])
    dw_uq = jnp.concatenate([uqn, uqr], axis=2).reshape(ql, HEADS * QK_HEAD)
    ukn = gr["w_ukv"][:, :HEADS * LANE].reshape(kl, HEADS, LANE)
    ukv = gr["w_ukv"][:, HEADS * LANE:].reshape(kl, HEADS, V_HEAD)
    dw_ukv = jnp.concatenate([ukn, ukv], axis=2).reshape(kl, HEADS * (QK_NOPE + V_HEAD))

    def gains(gn, grr):
        return jnp.concatenate([gn[0], _rope_swap(grr[0, :QK_ROPE])])

    return dict(mla_w_a=dw_a, mla_g_qa=gr["g_qa"][0], mla_w_uq=dw_uq, mla_g_kva=gr["g_kva"][0], mla_w_ukv=dw_ukv,
                mla_g_q=gains(gr["gq_n"], gr["gq_r"]), mla_g_k=gains(gr["gk_n"], gr["gk_r"]), mla_w_o=gr["w_o"])


def _loss_head(h, target, geo, name):
    t, d = h.shape
    tile = geo.tile
    n_lat_tiles = 2 * geo.n_lat // tile

    def body(h_ref, t_ref, dh_ref, loss_ref):
        i = pl.program_id(0)

        @pl.when(i == 0)
        def _():
            loss_ref[...] = jnp.zeros_like(loss_ref)

        @pl.when(i < n_lat_tiles)
        def _():
            e = h_ref[...] - t_ref[...]
            dh_ref[...] = e * (1.0 / d)
            part = jnp.sum(e * e, axis=0, keepdims=True) * (0.5 / d)
            loss_ref[...] += sum(part[:, j * LANE:(j + 1) * LANE] for j in range(d // LANE))

        @pl.when(i >= n_lat_tiles)
        def _():
            dh_ref[...] = jnp.zeros_like(dh_ref)

    row = pl.BlockSpec((tile, d), lambda i: (i, 0))
    tgt = pl.BlockSpec((tile, d), lambda i: (jnp.minimum(i, n_lat_tiles - 1), 0))
    dh, loss = pl.pallas_call(
        body, name=name, grid=(t // tile,), in_specs=[row, tgt],
        out_specs=(row, pl.BlockSpec((1, LANE), lambda i: (0, 0))),
        out_shape=(jax.ShapeDtypeStruct((t, d), F32), jax.ShapeDtypeStruct((1, LANE), F32)),
        compiler_params=_params("arbitrary"),
    )(h, target)
    return jnp.sum(loss), dh


def _adamw(w, g, m, v, name):
    shape = w.shape
    cols = shape[-1]
    rows = int(np.prod(shape[:-1])) if len(shape) > 1 else 1
    w2, g2, m2, v2 = (a.reshape(rows, cols) for a in (w, g, m, v))
    tr = _pick(rows, (512, 256, 128, 64, 32, 16, 8))
    c1 = 1.0 / (1.0 - ADAM_B1 ** ADAM_STEP)
    c2 = 1.0 / (1.0 - ADAM_B2 ** ADAM_STEP)

    def body(w_ref, g_ref, m_ref, v_ref, d_ref, mo_ref, vo_ref):
        gv = g_ref[...]
        mn = ADAM_B1 * m_ref[...] + (1.0 - ADAM_B1) * gv
        vn = ADAM_B2 * v_ref[...] + (1.0 - ADAM_B2) * (gv * gv)
        d_ref[...] = -ADAM_LR * ((mn * c1) / (jnp.sqrt(vn * c2) + ADAM_EPS) + ADAM_WD * w_ref[...])
        mo_ref[...] = mn
        vo_ref[...] = vn

    blk = pl.BlockSpec((tr, cols), lambda i: (i, 0))
    sds = jax.ShapeDtypeStruct((rows, cols), F32)
    d, mo, vo = pl.pallas_call(
        body, name=name, grid=(rows // tr,), in_specs=[blk] * 4, out_specs=(blk,) * 3, out_shape=(sds,) * 3,
        compiler_params=_params("parallel"),
    )(w2, g2, m2, v2)
    return d.reshape(shape), mo.reshape(shape), vo.reshape(shape)


SHARD_AXIS = {
    "w_mod": 2, "g_norm": 2, "ffn_w1": 3, "ffn_w3": 3, "ffn_w2": 2, "sc_w_in": 2, "sc_conv": 2, "sc_w_out": 1,
    "mla_w_a": 1, "mla_g_qa": 1, "mla_w_uq": 2, "mla_w_ukv": 2, "mla_w_o": 1,
}
HIDDEN_MAJOR = ("ffn_w1", "ffn_w3")


def _view(name, arr, swapped=False):
    form, swap, _ = EXCHANGE[name]
    if swap and not swapped:
        arr = jnp.swapaxes(arr, -1, -2)
    if form == "mid":
        arr = arr.reshape((-1,) + arr.shape[-2:])
        return jnp.pad(arr, ((0, 0), (0, 0), (0, -arr.shape[-1] % LANE)))
    arr = arr.reshape(-1, arr.shape[-1])
    return jnp.pad(arr, ((0, -arr.shape[0] % 16), (0, 0)))


def _unview(name, view, shape, keep_swapped=False):
    form, swap, _ = EXCHANGE[name]
    shape = shape[:-2] + (shape[-1], shape[-2]) if swap else shape
    if form == "mid":
        view = view[:, :, :shape[-1]]
    else:
        view = view[:int(np.prod(shape[:-1]))]
    arr = view.reshape(shape)
    return arr if (not swap or keep_swapped) else jnp.swapaxes(arr, -1, -2)


def _full_shape(name, local_shape):
    ax = SHARD_AXIS[name]
    return local_shape[:ax] + (N_DEV * local_shape[ax],) + local_shape[ax + 1:]


def _win(ref, form, n, j):
    start = j * n
    if not isinstance(start, int):
        start = pl.multiple_of(start, LANE if form == "last" else math.gcd(n, 16))
    if form == "mid":
        return ref.at[:, pl.ds(start, n), :]
    return ref.at[:, pl.ds(start, n)]


def _windows(view, count, of):
    return view.shape[:1] + (view.shape[1] * count // of,) + view.shape[2:]


def _gather_work(views, forms):
    na = len(views)

    def plan(x_refs, out_refs, sems):
        send_sems, recv_sems, local_sems = sems
        x, y, c = lax.axis_index("x"), lax.axis_index("y"), lax.axis_index("c")
        me, sibling = (x, y, c), (x, y, 1 - c)
        chips = [(1 - x, y), (x, 1 - y), (1 - x, 1 - y)]

        def copy(a, k, block, to, from_input):
            dst = _win(out_refs[a], forms[a], views[a].shape[1], 4 * block[0] + 2 * block[1] + block[2])
            return pltpu.make_async_remote_copy(
                src_ref=x_refs[a] if from_input else dst, dst_ref=dst, send_sem=send_sems.at[a, k],
                recv_sem=recv_sems.at[a, k], device_id=to, device_id_type=MESH)

        mine = [pltpu.make_async_copy(x_refs[a], _win(out_refs[a], forms[a], views[a].shape[1], 4 * x + 2 * y + c),
                                      local_sems.at[a]) for a in range(na)]
        first = []
        for a in range(na):
            first.append(copy(a, 0, me, sibling, True))
            first += [copy(a, 1 + j, me, (*chip, c), True) for j, chip in enumerate(chips)]
        return copy, mine, first, me, sibling, chips, c

    def start(x_refs, out_refs, sems):
        _, mine, first, *_ = plan(x_refs, out_refs, sems)
        for cp in mine + first:
            cp.start()

    def finish(x_refs, out_refs, sems):
        copy, mine, first, me, sibling, chips, c = plan(x_refs, out_refs, sems)
        passed = []
        for j, chip in enumerate(chips):
            for a in range(na):
                copy(a, 1 + j, (*chip, c), me, False).wait_recv()
                fwd = copy(a, 4 + j, (*chip, c), sibling, False)
                fwd.start()
                passed.append(fwd)
        for a in range(na):
            copy(a, 0, sibling, me, False).wait_recv()
            for j, chip in enumerate(chips):
                copy(a, 4 + j, (*chip, 1 - c), me, False).wait_recv()
        for cp in first + passed:
            cp.wait_send()
        for cp in mine:
            cp.wait()

    return Hosted(
        list(views), [jax.ShapeDtypeStruct(_windows(v, N_DEV, 1), v.dtype) for v in views],
        [pltpu.SemaphoreType.DMA((na, 7)), pltpu.SemaphoreType.DMA((na, 7)), pltpu.SemaphoreType.DMA((na,))], start, finish)


def _push_work(srcs, out_shapes, n_copies, make_copies):
    na = len(srcs)

    def start(s_refs, r_refs, sems):
        for cp in make_copies(s_refs, r_refs, sems[0], sems[1]):
            cp.start()

    def finish(s_refs, r_refs, sems):
        copies = make_copies(s_refs, r_refs, sems[0], sems[1])
        for cp in copies:
            cp.wait_recv()
        for cp in copies:
            cp.wait_send()

    return Hosted(list(srcs), out_shapes, [pltpu.SemaphoreType.DMA((na, n_copies)), pltpu.SemaphoreType.DMA((na, n_copies))],
                  start, finish)


def _sibling_work(fulls, forms):
    na = len(fulls)
    widths = [f.shape[1] // N_DEV for f in fulls]

    def make_copies(g_refs, r_refs, send_sems, recv_sems):
        x, y, c = lax.axis_index("x"), lax.axis_index("y"), lax.axis_index("c")
        return [
            pltpu.make_async_remote_copy(
                src_ref=_win(g_refs[a], forms[a], widths[a], 2 * chip + (1 - c)),
                dst_ref=_win(r_refs[a], forms[a], widths[a], chip), send_sem=send_sems.at[a, chip],
                recv_sem=recv_sems.at[a, chip], device_id=(x, y, 1 - c), device_id_type=MESH)
            for a in range(na) for chip in range(N_CHIP)
        ]

    return _push_work(fulls, [jax.ShapeDtypeStruct(_windows(f, N_CHIP, N_DEV), f.dtype) for f in fulls], N_CHIP, make_copies)


def _chip_work(parts, forms):
    na = len(parts)
    widths = [p.shape[1] // N_CHIP for p in parts]

    def make_copies(p_refs, r_refs, send_sems, recv_sems):
        x, y, c = lax.axis_index("x"), lax.axis_index("y"), lax.axis_index("c")
        chips = [(1 - x, y), (x, 1 - y), (1 - x, 1 - y)]
        return [
            pltpu.make_async_remote_copy(
                src_ref=_win(p_refs[a], forms[a], widths[a], 2 * px + py), dst_ref=_win(r_refs[a], forms[a], widths[a], j),
                send_sem=send_sems.at[a, j], recv_sem=recv_sems.at[a, j], device_id=(px, py, c), device_id_type=MESH)
            for a in range(na) for j, (px, py) in enumerate(chips)
        ]

    return _push_work(parts, [jax.ShapeDtypeStruct(_windows(p, 3, N_CHIP), p.dtype) for p in parts], 3, make_copies)


def _sum_tiles(view, form, n):
    if form == "mid":
        tr = n
        while tr * view.shape[2] * 4 > 2 * 1024 * 1024 and tr % 32 == 0:
            tr //= 2
        return 1, tr
    return _pick(view.shape[0], (512, 256, 128, 64, 32, 16)), n


def _window_spec(form, tl, tr, rest, window_of):
    if form == "mid":
        return lambda per: pl.BlockSpec((None, tr) + rest, lambda l, k, i, s: (l, window_of(k, s) * per + i, 0))
    return lambda per: pl.BlockSpec((tl, tr), lambda l, k, i, s: (l, window_of(k, s)))


def _chip_partials(g, recv, core, form, name):
    n = g.shape[1] // N_DEV
    tl, tr = _sum_tiles(g, form, n)
    per = n // tr
    rest = tuple(g.shape[2:])

    def body(core_ref, g_ref, r_ref, o_ref):
        o_ref[...] = (g_ref[...] + r_ref[...]).astype(o_ref.dtype)

    own = _window_spec(form, tl, tr, rest, lambda k, s: 2 * k + s[0])(per)
    by_chip = _window_spec(form, tl, tr, rest, lambda k, s: k)(per)
    return pl.pallas_call(
        body, name=name,
        grid_spec=pltpu.PrefetchScalarGridSpec(
            num_scalar_prefetch=1, grid=(g.shape[0] // tl, N_CHIP, per), in_specs=[own, by_chip], out_specs=by_chip),
        out_shape=jax.ShapeDtypeStruct(recv.shape, BF16), compiler_params=_params("parallel", "parallel", "parallel"),
    )(core, g, recv)


def _reduce_final(p, recv, chip, form, name):
    n = p.shape[1] // N_CHIP
    tl, tr = _sum_tiles(p, form, n)
    per = n // tr
    rest = tuple(p.shape[2:])

    def body(chip_ref, p_ref, ry_ref, rx_ref, rxy_ref, o_ref):
        own_pair = p_ref[...].astype(F32) + ry_ref[...].astype(F32)
        o_ref[...] = own_pair + (rx_ref[...].astype(F32) + rxy_ref[...].astype(F32))

    def rel(j):
        return _window_spec(form, tl, tr, rest, lambda k, s: j)(per)

    own = _window_spec(form, tl, tr, rest, lambda k, s: s[0])(per)
    return pl.pallas_call(
        body, name=name,
        grid_spec=pltpu.PrefetchScalarGridSpec(
            num_scalar_prefetch=1, grid=(p.shape[0] // tl, 1, per), in_specs=[own, rel(1), rel(0), rel(2)],
            out_specs=rel(0)),
        out_shape=jax.ShapeDtypeStruct(p.shape[:1] + (n,) + p.shape[2:], F32),
        compiler_params=_params("parallel", "parallel", "parallel"),
    )(chip, p, recv, recv, recv)


def _pack_replicated(arrays):
    pieces = []
    for a in arrays:
        flat = a.reshape(-1).astype(F32)
        pieces.append(jnp.pad(flat, (0, -flat.size % LANE)))
    total = sum(p.size for p in pieces)
    pieces.append(jnp.zeros((-total % (16 * LANE),), F32))
    return jnp.concatenate(pieces).reshape(-1, LANE)


def _unpack_replicated(buf, shapes):
    flat, out, off = buf.reshape(-1), [], 0
    for shape in shapes:
        size = int(np.prod(shape))
        out.append(flat[off:off + size].reshape(shape))
        off += size + (-size % LANE)
    return out


def _silu(v):
    return v * jax.nn.sigmoid(v)


FFN_NAMES = ("ffn_w1", "ffn_w3", "ffn_w2")
SC_NAMES = ("sc_w_in", "sc_conv", "sc_w_out")
MLA_SHARDED = ("mla_w_a", "mla_g_qa", "mla_w_uq", "mla_w_ukv", "mla_w_o")
MLA_NAMES = ("mla_w_a", "mla_g_qa", "mla_w_uq", "mla_g_kva", "mla_w_ukv", "mla_g_q", "mla_g_k", "mla_w_o")


def _local_step(src, x, c, ctx, target):
    bsz, n_lat, d = x.shape
    n_ctx = ctx.shape[1]
    assert bsz == 2
    geo = Geo(n_lat, n_ctx)
    depth = src.depth
    tc = _conv_tile(d)
    tabs = _rope_tables(geo)

    h = jnp.concatenate([x.reshape(2 * n_lat, d), ctx.reshape(2 * n_ctx, d)], axis=0)
    tgt = target.reshape(2 * n_lat, d)

    saved = []
    for i in range(depth):
        kind = i % 2
        wl, slots = src.weights(i), src.fwd_slots(i)
        gn = wl["g_norm"].astype(F32)
        mod4 = src.mod(i).reshape(N_SEG, N_MOD, 1, d)
        h, s1 = _ffn_fwd(h, gn[0:1], mod4, 0, wl, 0, geo, f"l{i}_f1", "f1", slots, slots)
        if kind == 0:
            mix = (_interleave(wl["sc_w_in"], 3, tc), wl["sc_conv"].astype(F32), wl["sc_w_out"])
            h, s2 = _sconv_fwd(h, gn[1:2], mod4, *mix, geo, f"l{i}_sc", slots, slots)
        else:
            mix = _mla_prepare(*[wl[name] for name in MLA_NAMES])
            h, s2 = _mla_fwd(h, gn[1:2], mod4, mix, i != depth - 1, tabs, geo, f"l{i}_mla", slots, slots)
        h, s3 = _ffn_fwd(h, gn[2:3], mod4, 6, wl, 1, geo, f"l{i}_f2", "f2", slots, slots)
        saved.append((wl, gn, mod4, mix, s1, s2, s3))

    loss, dh = _loss_head(h, tgt, geo, "loss_head")

    g_b_mod = [None] * depth
    for i in reversed(range(depth)):
        kind = i % 2
        wl, gn, mod4, mix, s1, s2, s3 = saved[i]
        slots = src.bwd_slots(i)
        gbuf = {name: lax.empty(wl[name].shape, F32) for name in ("ffn_w1", "ffn_w3", "ffn_w2")}
        dh, dg2, dm2 = _ffn_bwd(dh, s3, gn[2:3], mod4, 6, wl, 1, gbuf, geo, f"l{i}_f2", "f2", slots, slots)
        src.ffn2_grads(i, gbuf)
        if kind == 0:
            dh, dg1, dm1, dwin, dconv, dwout = _sconv_bwd(dh, s2, gn[1:2], mod4, *mix, geo, f"l{i}_sc", slots, slots)
            gl = dict(sc_w_in=_deinterleave(dwin, 3, tc), sc_conv=dconv, sc_w_out=dwout)
        else:
            dh, dg1, dm1, gm = _mla_bwd(dh, s2, gn[1:2], mod4, mix, i != depth - 1, tabs, geo, f"l{i}_mla", slots, slots)
            gl = _mla_unprepare(gm)
        dh, dg0, dm0 = _ffn_bwd(dh, s1, gn[0:1], mod4, 0, wl, 0, gbuf, geo, f"l{i}_f1", "f1", slots, slots)
        dmod = jnp.concatenate(list(dm0) + list(dm1) + list(dm2), axis=1).reshape(N_SEG, N_MOD * d)
        dmod8 = jnp.concatenate([dmod, jnp.zeros((8 - N_SEG, N_MOD * d), F32)], axis=0)
        g_b_mod[i] = jnp.sum(dmod, axis=0)
        gl.update(gbuf, g_norm=jnp.concatenate([dg0, dg1, dg2], axis=0))
        src.dmod(i, dmod8)
        src.grads(i, gl)

    grad_x = dh[:2 * n_lat].reshape(x.shape)
    return loss, grad_x, jnp.stack(g_b_mod)


class _Slots:
    def __init__(self, get, put):
        self.get, self._put = get, put

    def __setitem__(self, slot, outs):
        self._put(slot, outs)


FWD_PLAN = {
    0: {"f1_up": ("ffn_w1",), "f1_down": ("g_norm", "mix"), "mix_a": ("ffn_w3",), "mix_b": ("ffn_w2",)},
    1: {"f1_up": ("ffn_w1",), "mix_a": ("ffn_w3", "g_norm", "mix"), "f2_up": ("ffn_w2",)},
}
SIBLING_PLAN = {"f2_dact": ("ffn_w1", "g_norm", "mix"), "f2_dw2": ("ffn_w3", "ffn_w2")}
BWD_PLAN = {
    0: {"f2_dnx": ("ffn_w1",), "mix_b": ("ffn_w3",), "mix_a": ("ffn_w2",), "f1_dact": ("g_norm", "mix")},
    1: {"f2_dnx": ("ffn_w1",), "mix_a": ("ffn_w3", "ffn_w2"), "f1_dnx": ("g_norm", "mix")},
}
DMOD_SLOT = {0: "mix_c", 1: "f1_dact"}
MOD_ROWS = 32


class _Exchange:
    def __init__(self, w):
        self.w = w
        self.depth = w["w_mod"].shape[0]
        self.c_ctx = w["c_ctx"]
        self.me = 4 * lax.axis_index("x") + 2 * lax.axis_index("y") + lax.axis_index("c")
        self.core = lax.axis_index("c").astype(jnp.int32).reshape(1)
        self.chip = (2 * lax.axis_index("x") + lax.axis_index("y")).astype(jnp.int32).reshape(1)
        self.full, self.gviews, self.parts, self.reduced, self.rep, self.dmods = {}, {}, {}, {}, {}, {}
        self.ctx_pre = jnp.zeros_like(self.c_ctx)

    def _layer_of(self, name, i):
        return i // 2 if name.startswith(("sc_", "mla_")) else i

    def _mixer(self, i):
        return SC_NAMES if i % 2 == 0 else MLA_SHARDED

    def _expand(self, names, i):
        out = []
        for name in names:
            out += list(self._mixer(i)) if name == "mix" else [name]
        return out

    def _group(self, i):
        return ["g_norm", "ffn_w1", "ffn_w3", "ffn_w2"] + list(self._mixer(i))

    def _local(self, name, i):
        arr = self.w[name][self._layer_of(name, i)]
        return arr[:, None] if name == "mla_g_qa" else arr

    def _shapes(self, name, i):
        local = tuple(self._local(name, i).shape)
        ax = SHARD_AXIS[name] - 1
        return local, local[:ax] + (N_DEV * local[ax],) + local[ax + 1:]

    def _gather(self, names, i):
        views = [_view(n, self._local(n, i).astype(BF16 if EXCHANGE[n][2] else F32)) for n in names]
        return _gather_work(views, [EXCHANGE[n][0] for n in names])

    def _gathered(self, names, i, outs):
        for name, fv in zip(names, outs):
            arr = _unview(name, fv, self._shapes(name, i)[1], keep_swapped=name in HIDDEN_MAJOR)
            self.full[name, i] = arr[:, 0] if name == "mla_g_qa" else arr

    def prefetch(self, c):
        bsz, d = c.shape
        (conds,) = _run_hosted(_gather_work([jnp.pad(c, ((0, 8 - bsz), (0, 0)))[None]], ["mid"]), "gather_cond")
        conds = conds.reshape(N_DEV, 8, d)[:, :bsz]
        act = _silu(jnp.concatenate([conds, jnp.broadcast_to(self.c_ctx, (N_DEV, 1, d))], axis=1))
        self.s_rows = jnp.pad(act.reshape(N_DEV * N_SEG, d), ((0, MOD_ROWS - N_DEV * N_SEG), (0, 0)))
        cols = jnp.stack([_mm(self.s_rows, self.w["w_mod"][l], name=f"mod_cols_{l}") for l in range(self.depth)])
        names = self._group(0)
        work = self._gather(names, 0)
        both = _gather_work(work.inputs + [cols.reshape(self.depth * MOD_ROWS, -1)], self._forms(names) + ["last"])
        outs = _run_hosted(both, "gather_l0")
        self._gathered(names, 0, outs[:-1])
        mods = lax.dynamic_slice_in_dim(outs[-1].reshape(self.depth, MOD_ROWS, -1), N_SEG * self.me, N_SEG, axis=1)
        self.mods = mods + self.w["b_mod"][:, None, :]

    def mod(self, i):
        return self.mods[i]

    def weights(self, i):
        wl = {name: self.full[name, i] for name in self._group(i)}
        if i % 2 == 1:
            for name in ("mla_g_kva", "mla_g_q", "mla_g_k"):
                wl[name] = self.w[name][i // 2]
        return wl

    def fwd_slots(self, i):
        plan = FWD_PLAN[i % 2] if i + 1 < self.depth else {}
        names = {slot: self._expand(plan[slot], i + 1) for slot in plan}
        return _Slots(lambda slot: self._gather(names[slot], i + 1) if slot in names else None,
                      lambda slot, outs: self._gathered(names[slot], i + 1, outs))

    def ffn2_grads(self, i, gbuf):
        if i == 0:
            for name in FFN_NAMES:
                self.gviews[name + "#1", 0] = _view(name, gbuf[name][1:2], swapped=True)

    def grads(self, i, gl):
        for name in self._group(i):
            g = gl[name][:, None] if name == "mla_g_qa" else gl[name]
            if i == 0 and name in FFN_NAMES:
                self.gviews[name + "#0", 0] = _view(name, g[0:1], swapped=True)
            else:
                self.gviews[name, i] = _view(name, g, swapped=name in HIDDEN_MAJOR)
        for name in REPLICATED:
            if name in gl:
                self.rep[name, i // 2] = gl[name]

    def dmod(self, i, dmod8):
        self.dmods[i] = dmod8

    def _dmod_gather(self, i):
        return _gather_work([self.dmods[i][None]], ["mid"])

    def _dmod_gathered(self, i, outs):
        n = self.w["w_mod"].shape[2]
        rows = outs[0].reshape(N_DEV, 8, -1)[:, :N_SEG]
        mine = lax.dynamic_slice_in_dim(rows, n * self.me, n, axis=2)
        flat = jnp.pad(mine.reshape(N_DEV * N_SEG, n), ((0, MOD_ROWS - N_DEV * N_SEG), (0, 0)))
        self.reduced["w_mod", i] = _mm(self.s_rows, flat, ta=True, name=f"dwmod_{i}")
        ctx_rows = jnp.pad(jnp.sum(mine[:, N_SEG - 1], axis=0, keepdims=True), ((0, 7), (0, 0)))
        self.ctx_pre = self.ctx_pre + _mm(ctx_rows, self.w["w_mod"][i], tb=True, name=f"dcond_{i}")[0]

    def _forms(self, names):
        return [EXCHANGE[n.split("#")[0]][0] for n in names]

    def _partials(self, names, i, from_sibling):
        for name, recv in zip(names, from_sibling):
            self.parts[name, i] = _chip_partials(self.gviews[name, i], recv, self.core, self._forms([name])[0],
                                                 f"partial_{name.replace('#', '_')}_{i}")

    def _finals(self, names, i, from_chips):
        for name, recv in zip(names, from_chips):
            rv = _reduce_final(self.parts[name, i], recv, self.chip, self._forms([name])[0],
                               f"final_{name.replace('#', '_')}_{i}")
            base = name.split("#")[0]
            shape = self._shapes(base, i)[0]
            arr = _unview(base, rv, (1,) + shape[1:] if "#" in name else shape)
            self.reduced[name, i] = arr[:, 0] if name == "mla_g_qa" else arr

    def bwd_slots(self, i):
        if i + 1 >= self.depth:
            return _Slots(lambda slot: None, None)
        plan = BWD_PLAN[i % 2]
        chips = {slot: (self._expand(plan[slot], i + 1), i + 1) for slot in plan}
        sibling = {slot: (self._expand(SIBLING_PLAN[slot], i + 1), i + 1) for slot in SIBLING_PLAN}
        if i == 0:
            sibling["mix_d"] = ([name + "#1" for name in FFN_NAMES], 0)
            chips["f1_dw2"] = (["ffn_w1#1"], 0)
            chips["f1_dnx"] = (["ffn_w3#1", "ffn_w2#1"], 0)

        def get(slot):
            if slot in sibling:
                names, group = sibling[slot]
                return _sibling_work([self.gviews[n, group] for n in names], self._forms(names))
            if slot in chips:
                names, group = chips[slot]
                return _chip_work([self.parts[n, group] for n in names], self._forms(names))
            if slot == DMOD_SLOT[i % 2]:
                return self._dmod_gather(i + 1)
            return None

        def put(slot, outs):
            if slot in sibling:
                self._partials(*sibling[slot], outs)
            elif slot in chips:
                self._finals(*chips[slot], outs)
            else:
                self._dmod_gathered(i + 1, outs)

        return _Slots(get, put)

    def finish(self, rep_grads):
        group = [name + "#0" if name in FFN_NAMES else name for name in self._group(0)]
        self._dmod_gathered(0, _run_hosted(self._dmod_gather(0), "gather_dmod_l0"))
        rep_grads["c_ctx"] = self.ctx_pre
        for name in REPLICATED:
            if name not in rep_grads:
                rep_grads[name] = jnp.stack([self.rep[name, j] for j in range(self.w[name].shape[0])])
        rep = _pack_replicated([rep_grads[name] for name in REPLICATED])
        views = [self.gviews[n, 0] for n in group] + [jnp.tile(rep[None], (1, N_DEV, 1))]
        forms = self._forms(group) + ["mid"]
        from_sibling = _run_hosted(_sibling_work(views, forms), "reduce_sibling_l0")
        self._partials(group, 0, from_sibling[:-1])
        rep_part = _chip_partials(views[-1], from_sibling[-1], self.core, "mid", "partial_replicated")
        parts = [self.parts[n, 0] for n in group] + [rep_part]
        from_chips = _run_hosted(_chip_work(parts, forms), "reduce_chips_l0")
        self._finals(group, 0, from_chips[:-1])
        for name in FFN_NAMES:
            self.reduced[name, 0] = jnp.concatenate([self.reduced[name + "#0", 0], self.reduced[name + "#1", 0]], axis=0)
        rep_sum = _reduce_final(rep_part, from_chips[-1], self.chip, "mid", "final_replicated")
        out = dict(zip(REPLICATED, _unpack_replicated(rep_sum, [self.w[name].shape for name in REPLICATED])))
        sg = jax.nn.sigmoid(self.c_ctx)
        out["c_ctx"] = out["c_ctx"] * (sg * (1.0 + self.c_ctx * (1.0 - sg)))
        for name in EXCHANGE:
            layers = range(self.w[name].shape[0])
            step = 2 if name.startswith(("sc_", "mla_")) else 1
            first = 1 if name.startswith("mla_") else 0
            out[name] = jnp.stack([self.reduced[name, first + step * l] for l in layers])
        return out


def kernel(x, c, ctx, c_ctx, w_mod, b_mod, g_norm, ffn_w1, ffn_w3, ffn_w2, sc_w_in, sc_conv, sc_w_out, mla_w_a, mla_g_qa, mla_w_uq, mla_g_kva, mla_w_ukv, mla_g_q, mla_g_k, mla_w_o, loss_target, m_c_ctx, m_w_mod, m_b_mod, m_g_norm, m_ffn_w1, m_ffn_w3, m_ffn_w2, m_sc_w_in, m_sc_conv, m_sc_w_out, m_mla_w_a, m_mla_g_qa, m_mla_w_uq, m_mla_g_kva, m_mla_w_ukv, m_mla_g_q, m_mla_g_k, m_mla_w_o, v_c_ctx, v_w_mod, v_b_mod, v_g_norm, v_ffn_w1, v_ffn_w3, v_ffn_w2, v_sc_w_in, v_sc_conv, v_sc_w_out, v_mla_w_a, v_mla_g_qa, v_mla_w_uq, v_mla_g_kva, v_mla_w_ukv, v_mla_g_q, v_mla_g_k, v_mla_w_o):
    w = dict(c_ctx=c_ctx, w_mod=w_mod, b_mod=b_mod, g_norm=g_norm, ffn_w1=ffn_w1, ffn_w3=ffn_w3, ffn_w2=ffn_w2,
             sc_w_in=sc_w_in, sc_conv=sc_conv, sc_w_out=sc_w_out, mla_w_a=mla_w_a, mla_g_qa=mla_g_qa, mla_w_uq=mla_w_uq,
             mla_g_kva=mla_g_kva, mla_w_ukv=mla_w_ukv, mla_g_q=mla_g_q, mla_g_k=mla_g_k, mla_w_o=mla_w_o)
    m = dict(c_ctx=m_c_ctx, w_mod=m_w_mod, b_mod=m_b_mod, g_norm=m_g_norm, ffn_w1=m_ffn_w1, ffn_w3=m_ffn_w3,
             ffn_w2=m_ffn_w2, sc_w_in=m_sc_w_in, sc_conv=m_sc_conv, sc_w_out=m_sc_w_out, mla_w_a=m_mla_w_a,
             mla_g_qa=m_mla_g_qa, mla_w_uq=m_mla_w_uq, mla_g_kva=m_mla_g_kva, mla_w_ukv=m_mla_w_ukv, mla_g_q=m_mla_g_q,
             mla_g_k=m_mla_g_k, mla_w_o=m_mla_w_o)
    v = dict(c_ctx=v_c_ctx, w_mod=v_w_mod, b_mod=v_b_mod, g_norm=v_g_norm, ffn_w1=v_ffn_w1, ffn_w3=v_ffn_w3,
             ffn_w2=v_ffn_w2, sc_w_in=v_sc_w_in, sc_conv=v_sc_conv, sc_w_out=v_sc_w_out, mla_w_a=v_mla_w_a,
             mla_g_qa=v_mla_g_qa, mla_w_uq=v_mla_w_uq, mla_g_kva=v_mla_g_kva, mla_w_ukv=v_mla_w_ukv, mla_g_q=v_mla_g_q,
             mla_g_k=v_mla_g_k, mla_w_o=v_mla_w_o)
    exchange = _Exchange(w)
    exchange.prefetch(c)
    loss, grad_x, g_b_mod = _local_step(exchange, x, c, ctx, loss_target)
    loss = lax.psum(loss, ("x", "y", "c"))
    reduced = exchange.finish(dict(b_mod=g_b_mod))

    outs = [[], [], [], []]
    for name in WEIGHTS:
        delta, new_m, new_v = _adamw(w[name], reduced[name], m[name], v[name], f"adamw_{name}")
        for lst, val in zip(outs, (reduced[name], delta, new_m, new_v)):
            lst.append(val)
    return (loss, grad_x, *outs[0], *outs[1], *outs[2], *outs[3])
```

```python
import functools
import math

import jax
import jax.numpy as jnp
import numpy as np
from jax import lax
from jax.experimental import pallas as pl
from jax.experimental.pallas import tpu as pltpu

F32 = jnp.float32
BF16 = jnp.bfloat16

N_MOD = 9
HEADS = 8
QK_NOPE = 128
QK_ROPE = 64
QK_HEAD = QK_NOPE + QK_ROPE
V_HEAD = 128
GRID_W = 64
ROPE_BASE = 10000.0
QK_SCALE = QK_HEAD ** -0.5
EPS = 1e-6
ADAM_LR, ADAM_B1, ADAM_B2, ADAM_EPS, ADAM_WD, ADAM_STEP = 0.001, 0.9, 0.999, 1e-08, 0.01, 10

N_DEV = 8
N_CHIP = 4
N_SEG = 3
LANE = 128
HEAD_PAD = 2 * LANE
VMEM_LIMIT_BYTES = 48 * 1024 * 1024
MESH = pl.DeviceIdType.MESH

WEIGHTS = ["c_ctx", "w_mod", "b_mod", "g_norm", "ffn_w1", "ffn_w3", "ffn_w2", "sc_w_in", "sc_conv", "sc_w_out",
           "mla_w_a", "mla_g_qa", "mla_w_uq", "mla_g_kva", "mla_w_ukv", "mla_g_q", "mla_g_k", "mla_w_o"]
EXCHANGE = {
    "w_mod": ("last", False, True), "ffn_w1": ("mid", True, True), "ffn_w3": ("mid", True, True),
    "ffn_w2": ("mid", False, True), "sc_w_in": ("last", False, True), "sc_w_out": ("mid", False, True),
    "mla_w_a": ("mid", False, True), "mla_w_uq": ("mid", True, True), "mla_w_ukv": ("last", False, True),
    "mla_w_o": ("mid", False, True), "g_norm": ("last", False, False), "sc_conv": ("last", False, False),
    "mla_g_qa": ("mid", False, False),
}
REPLICATED = ["c_ctx", "b_mod", "mla_g_kva", "mla_g_q", "mla_g_k"]


def _pick(n, cands):
    for cand in cands:
        if n % cand == 0:
            return cand
    return n


def _params(*sem):
    return pltpu.CompilerParams(dimension_semantics=sem, vmem_limit_bytes=VMEM_LIMIT_BYTES)


def _hbm():
    return pl.BlockSpec(memory_space=pl.ANY)


class Hosted:
    def __init__(self, inputs, out_shapes, scratch, start, finish):
        self.inputs, self.out_shapes, self.scratch, self.start, self.finish = inputs, out_shapes, scratch, start, finish


def _call(body, hosted, **kw):
    if hosted is None:
        return pl.pallas_call(body, **kw)
    single = not isinstance(kw["out_shape"], (tuple, list))
    out_shape = [kw["out_shape"]] if single else list(kw["out_shape"])
    out_specs = [kw["out_specs"]] if single else list(kw["out_specs"])
    in_specs, scratch, grid = list(kw["in_specs"]), list(kw.get("scratch_shapes", ())), kw["grid"]
    n_in, n_out, n_scr = len(in_specs), len(out_shape), len(scratch)
    h_in, h_out = len(hosted.inputs), len(hosted.out_shapes)

    def wrapped(*refs):
        ins, hins = refs[:n_in], refs[n_in:n_in + h_in]
        o0 = n_in + h_in
        outs, houts = refs[o0:o0 + n_out], refs[o0 + n_out:o0 + n_out + h_out]
        s0 = o0 + n_out + h_out
        scr, hscr = refs[s0:s0 + n_scr], refs[s0 + n_scr:]
        first = functools.reduce(jnp.logical_and, [pl.program_id(a) == 0 for a in range(len(grid))])
        last = functools.reduce(jnp.logical_and, [pl.program_id(a) == g - 1 for a, g in enumerate(grid)])

        @pl.when(first)
        def _():
            hosted.start(hins, houts, hscr)

        body(*ins, *outs, *scr)

        @pl.when(last)
        def _():
            hosted.finish(hins, houts, hscr)

    call = pl.pallas_call(
        wrapped, name=kw["name"], grid=grid, in_specs=in_specs + [_hbm()] * h_in,
        out_specs=tuple(out_specs + [_hbm()] * h_out), out_shape=tuple(out_shape + list(hosted.out_shapes)),
        scratch_shapes=scratch + list(hosted.scratch), input_output_aliases=kw.get("input_output_aliases", {}),
        compiler_params=_params(*["arbitrary"] * len(grid)))

    def run(*args):
        res = call(*args, *hosted.inputs)
        comp = res[:n_out]
        return (comp[0] if single else tuple(comp)), list(res[n_out:])

    return run


def _run_hosted(hosted, name):
    def body(*refs):
        h_in, h_out = len(hosted.inputs), len(hosted.out_shapes)
        hins, houts, hscr = refs[:h_in], refs[h_in:h_in + h_out], refs[h_in + h_out:]
        hosted.start(hins, houts, hscr)
        hosted.finish(hins, houts, hscr)

    return list(pl.pallas_call(
        body, name=name, in_specs=[_hbm()] * len(hosted.inputs), out_specs=tuple([_hbm()] * len(hosted.out_shapes)),
        out_shape=tuple(hosted.out_shapes), scratch_shapes=list(hosted.scratch))(*hosted.inputs))


class Geo:
    def __init__(self, n_lat, n_ctx):
        self.n_lat, self.n_ctx = n_lat, n_ctx
        self.rows = 2 * n_lat + 2 * n_ctx
        self.tile = n_ctx
        assert n_lat % n_ctx == 0 and n_ctx % 16 == 0
        self.mm_tile = _pick(n_lat, (512, 256, 128)) if self.rows % _pick(n_lat, (512, 256, 128)) == 0 else n_ctx
        self.big_tile = _pick(self.rows, (1536, 768, 512, 256))

    def seg(self, i, tile):
        return jnp.minimum((i * tile) // self.n_lat, N_SEG - 1)

    def seg_start(self, i, tile):
        row = i * tile
        return jnp.logical_or(row % self.n_lat == 0, row == 2 * self.n_lat) & (row <= 2 * self.n_lat)


_NT = (((1,), (1,)), ((), ()))
_NN = (((1,), (0,)), ((), ()))
_TN = (((0,), (0,)), ((), ()))


def _dot(a, b, dims):
    return lax.dot_general(a.astype(BF16), b.astype(BF16), dims, preferred_element_type=F32)


def _mm(a, b, *, ta=False, tb=False, out_dtype=F32, name, gate=None, hosted=None):
    (kdim, m) = a.shape if ta else a.shape[::-1]
    n = b.shape[0] if tb else b.shape[1]
    assert (b.shape[1] if tb else b.shape[0]) == kdim
    if gate is not None:
        tm = gate[4].mm_tile
    else:
        tm = _pick(m, (1536, 768, 512, 256, 128))
    tn = _pick(n, (512, 256, 128))
    tk = _pick(kdim, (1024, 512, 256, 128))
    nk = kdim // tk
    dims = (((0 if ta else 1,), (1 if tb else 0,)), ((), ()))

    def body(*refs):
        if gate is not None:
            a_ref, b_ref, res_ref, gate_ref, o_ref, y_ref, acc_ref = refs
        else:
            a_ref, b_ref, o_ref, acc_ref = refs
        kk = pl.program_id(2)

        @pl.when(kk == 0)
        def _():
            acc_ref[...] = jnp.zeros_like(acc_ref)

        acc_ref[...] += lax.dot_general(a_ref[...].astype(BF16), b_ref[...].astype(BF16), dims,
                                        preferred_element_type=F32)

        @pl.when(kk == nk - 1)
        def _():
            acc = acc_ref[...]
            if gate is not None:
                y_ref[...] = acc.astype(y_ref.dtype)
                o_ref[...] = res_ref[...] + (gate[3] * gate_ref[...]) * acc
            else:
                o_ref[...] = acc.astype(o_ref.dtype)

    a_spec = pl.BlockSpec((tk, tm), lambda i, j, k: (k, i)) if ta else pl.BlockSpec((tm, tk), lambda i, j, k: (i, k))
    b_spec = pl.BlockSpec((tn, tk), lambda i, j, k: (j, k)) if tb else pl.BlockSpec((tk, tn), lambda i, j, k: (k, j))
    o_spec = pl.BlockSpec((tm, tn), lambda i, j, k: (i, j))
    in_specs, args = [a_spec, b_spec], [a, b]
    out_shape, out_specs = jax.ShapeDtypeStruct((m, n), out_dtype), o_spec
    if gate is not None:
        res, mod4, kmod, _, geo = gate
        in_specs += [o_spec, pl.BlockSpec((None, None, 1, tn), lambda i, j, k: (geo.seg(i, tm), kmod, 0, j))]
        args += [res, mod4]
        out_shape = (jax.ShapeDtypeStruct((m, n), F32), jax.ShapeDtypeStruct((m, n), BF16))
        out_specs = (o_spec, o_spec)
    return _call(
        body, hosted, name=name, grid=(m // tm, n // tn, nk), in_specs=in_specs, out_specs=out_specs,
        out_shape=out_shape, scratch_shapes=[pltpu.VMEM((tm, tn), F32)],
        compiler_params=_params("parallel", "parallel", "arbitrary"),
    )(*args)


def _tn_wide(lhs, rhs, name, into=None, s0=0, hosted=None):
    t, m = lhs.shape
    n = rhs.shape[1]
    tm = _pick(m, (1408, 1024, 512, 256, 128))
    while tm * n * 4 > 6.5 * 1024 * 1024 and tm % 256 == 0:
        tm //= 2
    tk = next(c for c in (1536, 768, 512, 256, 128, t)
              if t % c == 0 and c * (tm + n) * 4 + tm * n * 12 <= 36 * 1024 * 1024)

    def body(l_ref, r_ref, *rest):
        o_ref = rest[-1]
        kk = pl.program_id(1)
        part = lax.dot_general(l_ref[...], r_ref[...], _TN, preferred_element_type=F32)

        @pl.when(kk == 0)
        def _():
            o_ref[...] = part

        @pl.when(kk > 0)
        def _():
            o_ref[...] += part

    in_specs = [pl.BlockSpec((tk, tm), lambda i, k: (k, i)), pl.BlockSpec((tk, n), lambda i, k: (k, 0))]
    if into is None:
        return _call(
            body, hosted, name=name, grid=(m // tm, t // tk), in_specs=in_specs,
            out_specs=pl.BlockSpec((tm, n), lambda i, k: (i, 0)), out_shape=jax.ShapeDtypeStruct((m, n), F32),
            compiler_params=_params("parallel", "arbitrary"),
        )(lhs, rhs)
    return _call(
        body, hosted, name=name, grid=(m // tm, t // tk), in_specs=in_specs + [pl.BlockSpec(memory_space=pl.ANY)],
        out_specs=pl.BlockSpec((None, tm, n), lambda i, k: (s0, i, 0)),
        out_shape=jax.ShapeDtypeStruct(into.shape, into.dtype), input_output_aliases={2: 0},
        compiler_params=_params("parallel", "arbitrary"),
    )(lhs, rhs, into)


def _mod_spec(geo, tile, kmod, d):
    return pl.BlockSpec((None, None, 1, d), lambda i: (geo.seg(i, tile), kmod, 0, 0))


def _pre_fwd(h, g, mod4, k_shift, geo, name):
    t, d = h.shape
    tile = geo.mm_tile

    def body(h_ref, g_ref, sh_ref, sc_ref, o_ref):
        hv = h_ref[...]
        r = lax.rsqrt(jnp.mean(hv * hv, axis=-1, keepdims=True) + EPS)
        y = hv * r * g_ref[...]
        o_ref[...] = (y * (1.0 + sc_ref[...]) + sh_ref[...]).astype(o_ref.dtype)

    row = pl.BlockSpec((tile, d), lambda i: (i, 0))
    return pl.pallas_call(
        body, name=name, grid=(t // tile,),
        in_specs=[row, pl.BlockSpec((1, d), lambda i: (0, 0)), _mod_spec(geo, tile, k_shift, d),
                  _mod_spec(geo, tile, k_shift + 1, d)],
        out_specs=row, out_shape=jax.ShapeDtypeStruct((t, d), BF16), compiler_params=_params("parallel"),
    )(h, g, mod4, mod4)


def _pre_bwd(h, g, mod4, k_shift, dnx, dres, geo, name):
    t, d = h.shape
    tile = geo.mm_tile

    def body(h_ref, g_ref, sc_ref, dnx_ref, dres_ref, dh_ref, dg_ref, dsh_ref, dsc_ref):
        i = pl.program_id(0)
        hv, gv, dout = h_ref[...], g_ref[...], dnx_ref[...].astype(F32)
        r = lax.rsqrt(jnp.mean(hv * hv, axis=-1, keepdims=True) + EPS)
        xhat = hv * r
        dy = dout * (1.0 + sc_ref[...])
        u = dy * gv
        dh_ref[...] = r * (u - xhat * jnp.mean(u * xhat, axis=-1, keepdims=True)) + dres_ref[...]

        @pl.when(i == 0)
        def _():
            dg_ref[...] = jnp.zeros_like(dg_ref)

        @pl.when(geo.seg_start(i, tile))
        def _():
            dsh_ref[...] = jnp.zeros_like(dsh_ref)
            dsc_ref[...] = jnp.zeros_like(dsc_ref)

        dg_ref[...] += jnp.sum(dy * xhat, axis=0, keepdims=True)
        dsh_ref[...] += jnp.sum(dout, axis=0, keepdims=True)
        dsc_ref[...] += jnp.sum(dout * (xhat * gv), axis=0, keepdims=True)

    row = pl.BlockSpec((tile, d), lambda i: (i, 0))
    vec = pl.BlockSpec((1, d), lambda i: (0, 0))
    segv = pl.BlockSpec((None, 1, d), lambda i: (geo.seg(i, tile), 0, 0))
    return pl.pallas_call(
        body, name=name, grid=(t // tile,),
        in_specs=[row, vec, _mod_spec(geo, tile, k_shift + 1, d), row, row],
        out_specs=(row, vec, segv, segv),
        out_shape=(jax.ShapeDtypeStruct((t, d), F32), jax.ShapeDtypeStruct((1, d), F32),
                   jax.ShapeDtypeStruct((N_SEG, 1, d), F32), jax.ShapeDtypeStruct((N_SEG, 1, d), F32)),
        compiler_params=_params("arbitrary"),
    )(h, g, mod4, dnx, dres)


def _gate_bwd(dh, y, mod4, k_gate, coef, geo, name):
    t, d = dh.shape
    tile = geo.mm_tile

    def body(dh_ref, y_ref, gt_ref, dy_ref, dgt_ref):
        i = pl.program_id(0)
        dhv = dh_ref[...]
        dy_ref[...] = ((coef * gt_ref[...]) * dhv).astype(dy_ref.dtype)

        @pl.when(geo.seg_start(i, tile))
        def _():
            dgt_ref[...] = jnp.zeros_like(dgt_ref)

        dgt_ref[...] += coef * jnp.sum(dhv * y_ref[...].astype(F32), axis=0, keepdims=True)

    row = pl.BlockSpec((tile, d), lambda i: (i, 0))
    segv = pl.BlockSpec((None, 1, d), lambda i: (geo.seg(i, tile), 0, 0))
    return pl.pallas_call(
        body, name=name, grid=(t // tile,), in_specs=[row, row, _mod_spec(geo, tile, k_gate, d)],
        out_specs=(row, segv),
        out_shape=(jax.ShapeDtypeStruct((t, d), BF16), jax.ShapeDtypeStruct((N_SEG, 1, d), F32)),
        compiler_params=_params("arbitrary"),
    )(dh, y, mod4)


def _ff_tile(f):
    return _pick(f, (256, 128))


def _ffn_up(nx, w1t, w3t, s0, geo, name, hosted=None):
    t, d = nx.shape
    f = w1t.shape[1]
    tm, tn = geo.big_tile, _ff_tile(f)

    def body(x_ref, w1_ref, w3_ref, ga_ref, gb_ref, act_ref):
        xv = x_ref[...]
        a = lax.dot_general(xv, w1_ref[...], _NT, preferred_element_type=F32)
        bv = lax.dot_general(xv, w3_ref[...], _NT, preferred_element_type=F32)
        sg = jax.nn.sigmoid(a)
        silu = a * sg
        ga_ref[...] = (bv * (sg + silu * (1.0 - sg))).astype(ga_ref.dtype)
        gb_ref[...] = silu.astype(gb_ref.dtype)
        act_ref[...] = (silu * bv).astype(act_ref.dtype)

    w_spec = pl.BlockSpec((None, tn, d), lambda i, j: (s0, j, 0))
    o_spec = pl.BlockSpec((tm, tn), lambda i, j: (i, j))
    sds = jax.ShapeDtypeStruct((t, f), BF16)
    return _call(
        body, hosted, name=name, grid=(t // tm, f // tn),
        in_specs=[pl.BlockSpec((tm, d), lambda i, j: (i, 0)), w_spec, w_spec],
        out_specs=(o_spec,) * 3, out_shape=(sds,) * 3, compiler_params=_params("parallel", "parallel"),
    )(nx, w1t, w3t)


def _ffn_down(act, w2, s0, res, mod4, k_gate, geo, name, hosted=None):
    t, f = act.shape
    d = w2.shape[2]
    tm, tn = geo.mm_tile, _pick(d, (1024, 512, 256, 128))

    def body(a_ref, w_ref, res_ref, gate_ref, o_ref, y_ref):
        acc = lax.dot_general(a_ref[...], w_ref[...], _NN, preferred_element_type=F32)
        y_ref[...] = acc.astype(y_ref.dtype)
        o_ref[...] = res_ref[...] + (0.5 * gate_ref[...]) * acc

    o_spec = pl.BlockSpec((tm, tn), lambda i, j: (i, j))
    return _call(
        body, hosted, name=name, grid=(t // tm, d // tn),
        in_specs=[pl.BlockSpec((tm, f), lambda i, j: (i, 0)), pl.BlockSpec((None, f, tn), lambda i, j: (s0, 0, j)),
                  o_spec, pl.BlockSpec((None, None, 1, tn), lambda i, j: (geo.seg(i, tm), k_gate, 0, j))],
        out_specs=(o_spec, o_spec),
        out_shape=(jax.ShapeDtypeStruct((t, d), F32), jax.ShapeDtypeStruct((t, d), BF16)),
        compiler_params=_params("parallel", "parallel"),
    )(act, w2, res, mod4)


def _ffn_dact(dy, w2, ga, gb, s0, geo, name, hosted=None):
    t, d = dy.shape
    f = w2.shape[1]
    tm, tn = geo.big_tile, _ff_tile(f)

    def body(dy_ref, w_ref, ga_ref, gb_ref, da_ref, db_ref):
        dact = lax.dot_general(dy_ref[...], w_ref[...], _NT, preferred_element_type=F32)
        da_ref[...] = (dact * ga_ref[...].astype(F32)).astype(da_ref.dtype)
        db_ref[...] = (dact * gb_ref[...].astype(F32)).astype(db_ref.dtype)

    o_spec = pl.BlockSpec((tm, tn), lambda i, j: (i, j))
    sds = jax.ShapeDtypeStruct((t, f), BF16)
    return _call(
        body, hosted, name=name, grid=(t // tm, f // tn),
        in_specs=[pl.BlockSpec((tm, d), lambda i, j: (i, 0)), pl.BlockSpec((None, tn, d), lambda i, j: (s0, j, 0)),
                  o_spec, o_spec],
        out_specs=(o_spec, o_spec), out_shape=(sds, sds), compiler_params=_params("parallel", "parallel"),
    )(dy, w2, ga, gb)


def _ffn_dnx(da, db, w1t, w3t, s0, geo, name, hosted=None):
    t, f = da.shape
    d = w1t.shape[2]
    tm, tn = geo.mm_tile, _pick(d, (1024, 512, 256, 128))

    def body(da_ref, db_ref, w1_ref, w3_ref, o_ref):
        o_ref[...] = (lax.dot_general(da_ref[...], w1_ref[...], _NN, preferred_element_type=F32)
                      + lax.dot_general(db_ref[...], w3_ref[...], _NN, preferred_element_type=F32))

    x_spec = pl.BlockSpec((tm, f), lambda j, i: (i, 0))
    w_spec = pl.BlockSpec((None, f, tn), lambda j, i: (s0, 0, j))
    return _call(
        body, hosted, name=name, grid=(d // tn, t // tm), in_specs=[x_spec, x_spec, w_spec, w_spec],
        out_specs=pl.BlockSpec((tm, tn), lambda j, i: (i, j)), out_shape=jax.ShapeDtypeStruct((t, d), F32),
        compiler_params=_params("parallel", "parallel"),
    )(da, db, w1t, w3t)


def _with_host(fn, hosts, got, slot, *args, **kw):
    hosted = hosts.get(slot)
    if hosted is None:
        return fn(*args, **kw)
    out, got[slot] = fn(*args, hosted=hosted, **kw)
    return out


def _ffn_fwd(h, g, mod4, k0, w, s0, geo, tag, sub, hosts, got):
    nx = _pre_fwd(h, g, mod4, k0, geo, f"{tag}_pre")
    a, b, act = _with_host(_ffn_up, hosts, got, f"{sub}_up", nx, w["ffn_w1"], w["ffn_w3"], s0, geo, f"{tag}_up")
    h_out, y = _with_host(_ffn_down, hosts, got, f"{sub}_down", act, w["ffn_w2"], s0, h, mod4, k0 + 2, geo, f"{tag}_down")
    return h_out, (h, nx, a, b, act, y)


def _ffn_bwd(dh_out, saved, g, mod4, k0, w, s0, gbuf, geo, tag, sub, hosts, got):
    h, nx, a, b, act, y = saved
    dy, dgate = _gate_bwd(dh_out, y, mod4, k0 + 2, 0.5, geo, f"{tag}_dgate")
    da, db = _with_host(_ffn_dact, hosts, got, f"{sub}_dact", dy, w["ffn_w2"], a, b, s0, geo, f"{tag}_dact")
    gbuf["ffn_w2"] = _with_host(_tn_wide, hosts, got, f"{sub}_dw2", act, dy, f"{tag}_dw2", into=gbuf["ffn_w2"], s0=s0)
    dnx = _with_host(_ffn_dnx, hosts, got, f"{sub}_dnx", da, db, w["ffn_w1"], w["ffn_w3"], s0, geo, f"{tag}_dnx")
    gbuf["ffn_w1"] = _tn_wide(da, nx, f"{tag}_dw1", into=gbuf["ffn_w1"], s0=s0)
    gbuf["ffn_w3"] = _tn_wide(db, nx, f"{tag}_dw3", into=gbuf["ffn_w3"], s0=s0)
    dh, dg, dshift, dscale = _pre_bwd(h, g, mod4, k0, dnx, dh_out, geo, f"{tag}_dpre")
    return dh, dg, (dshift, dscale, dgate)


def _interleave(w, n_parts, tile):
    lead, cols = w.shape[:-1], w.shape[-1] // n_parts
    return w.reshape(*lead, n_parts, cols // tile, tile).swapaxes(-3, -2).reshape(*lead, n_parts * cols)


def _deinterleave(w, n_parts, tile):
    lead, cols = w.shape[:-1], w.shape[-1] // n_parts
    return w.reshape(*lead, cols // tile, n_parts, tile).swapaxes(-3, -2).reshape(*lead, n_parts * cols)


HALO = 16


def _conv_tile(c):
    return _pick(c, (256, 128))


def _conv_specs(geo, tc, t):
    tile = geo.tile
    per = tile // HALO
    last = t // HALO - 1
    cur = pl.BlockSpec((tile, 3 * tc), lambda j, i: (i, j))
    prev = pl.BlockSpec((HALO, 3 * tc), lambda j, i: (jnp.maximum(i * per - 1, 0), j))
    nxt = pl.BlockSpec((HALO, 3 * tc), lambda j, i: (jnp.minimum((i + 1) * per, last), j))
    return cur, prev, nxt


def _conv_edges(geo, i):
    tile = geo.tile
    row = i * tile
    lat = row < 2 * geo.n_lat
    first = jnp.where(lat, row % geo.n_lat == 0, (row - 2 * geo.n_lat) % geo.n_ctx == 0)
    end = row + tile
    last = jnp.where(lat, end % geo.n_lat == 0, (end - 2 * geo.n_lat) % geo.n_ctx == 0)
    return first, last


def _shift_rows(v, before, after):
    n = v.shape[0]
    rows = lax.broadcasted_iota(jnp.int32, v.shape, 0)
    down = jnp.where(rows == 0, before, pltpu.roll(v, 1, 0))
    up = jnp.where(rows == n - 1, after, pltpu.roll(v, n - 1, 0))
    return down, up


def _conv_fwd(proj, conv_w, geo, name, hosted=None):
    t, c3 = proj.shape
    c = c3 // 3
    tc, tile = _conv_tile(c), geo.tile

    def body(cur_ref, prev_ref, next_ref, w_ref, o_ref):
        first, last = _conv_edges(geo, pl.program_id(1))
        bv = cur_ref[:, :tc].astype(F32)
        p = cur_ref[:, tc:2 * tc].astype(F32) * cur_ref[:, 2 * tc:].astype(F32)
        p_before = prev_ref[HALO - 1:HALO, tc:2 * tc].astype(F32) * prev_ref[HALO - 1:HALO, 2 * tc:].astype(F32)
        p_after = next_ref[0:1, tc:2 * tc].astype(F32) * next_ref[0:1, 2 * tc:].astype(F32)
        p_before = jnp.where(first, 0.0, p_before)
        p_after = jnp.where(last, 0.0, p_after)
        pm1, pp1 = _shift_rows(p, p_before, p_after)
        w = w_ref[...]
        q = w[0:1] * pm1 + w[1:2] * p + w[2:3] * pp1
        o_ref[...] = (bv * q).astype(o_ref.dtype)

    cur, prev, nxt = _conv_specs(geo, tc, t)
    return _call(
        body, hosted, name=name, grid=(c // tc, t // tile),
        in_specs=[cur, prev, nxt, pl.BlockSpec((3, tc), lambda j, i: (0, j))],
        out_specs=pl.BlockSpec((tile, tc), lambda j, i: (i, j)), out_shape=jax.ShapeDtypeStruct((t, c), BF16),
        compiler_params=_params("parallel", "parallel"),
    )(proj, proj, proj, conv_w)


def _conv_bwd(proj, dyc, conv_w, geo, name, hosted=None):
    t, c3 = proj.shape
    c = c3 // 3
    tc, tile = _conv_tile(c), geo.tile

    def body(cur_ref, prev_ref, next_ref, d_ref, dprev_ref, dnext_ref, w_ref, o_ref, dw_ref):
        i = pl.program_id(1)
        first, last = _conv_edges(geo, i)
        bv = cur_ref[:, :tc].astype(F32)
        cv = cur_ref[:, tc:2 * tc].astype(F32)
        uv = cur_ref[:, 2 * tc:].astype(F32)
        p = cv * uv
        p_before = prev_ref[HALO - 1:HALO, tc:2 * tc].astype(F32) * prev_ref[HALO - 1:HALO, 2 * tc:].astype(F32)
        p_after = next_ref[0:1, tc:2 * tc].astype(F32) * next_ref[0:1, 2 * tc:].astype(F32)
        p_before = jnp.where(first, 0.0, p_before)
        p_after = jnp.where(last, 0.0, p_after)
        pm1, pp1 = _shift_rows(p, p_before, p_after)
        w = w_ref[...]
        q = w[0:1] * pm1 + w[1:2] * p + w[2:3] * pp1
        dy = d_ref[...].astype(F32)
        dq = dy * bv
        dq_before = dprev_ref[HALO - 1:HALO, :].astype(F32) * prev_ref[HALO - 1:HALO, :tc].astype(F32)
        dq_after = dnext_ref[0:1, :].astype(F32) * next_ref[0:1, :tc].astype(F32)
        dq_before = jnp.where(first, 0.0, dq_before)
        dq_after = jnp.where(last, 0.0, dq_after)
        dqm1, dqp1 = _shift_rows(dq, dq_before, dq_after)
        dp = w[0:1] * dqp1 + w[1:2] * dq + w[2:3] * dqm1
        o_ref[:, :tc] = (dy * q).astype(o_ref.dtype)
        o_ref[:, tc:2 * tc] = (dp * uv).astype(o_ref.dtype)
        o_ref[:, 2 * tc:] = (dp * cv).astype(o_ref.dtype)

        @pl.when(i == 0)
        def _():
            dw_ref[...] = jnp.zeros_like(dw_ref)

        dw_ref[0:1, :] += jnp.sum(dq * pm1, axis=0, keepdims=True)
        dw_ref[1:2, :] += jnp.sum(dq * p, axis=0, keepdims=True)
        dw_ref[2:3, :] += jnp.sum(dq * pp1, axis=0, keepdims=True)

    cur, prev, nxt = _conv_specs(geo, tc, t)
    per, lastb = tile // HALO, t // HALO - 1
    dcur = pl.BlockSpec((tile, tc), lambda j, i: (i, j))
    dprev = pl.BlockSpec((HALO, tc), lambda j, i: (jnp.maximum(i * per - 1, 0), j))
    dnext = pl.BlockSpec((HALO, tc), lambda j, i: (jnp.minimum((i + 1) * per, lastb), j))
    wspec = pl.BlockSpec((3, tc), lambda j, i: (0, j))
    return _call(
        body, hosted, name=name, grid=(c // tc, t // tile), in_specs=[cur, prev, nxt, dcur, dprev, dnext, wspec],
        out_specs=(cur, wspec), out_shape=(jax.ShapeDtypeStruct((t, c3), BF16), jax.ShapeDtypeStruct((3, c), F32)),
        compiler_params=_params("parallel", "arbitrary"),
    )(proj, proj, proj, dyc, dyc, dyc, conv_w)


def _sconv_fwd(h, g, mod4, w_in, conv_w, w_out, geo, tag, hosts, got):
    nx = _pre_fwd(h, g, mod4, 3, geo, f"{tag}_pre")
    proj = _with_host(_mm, hosts, got, "mix_a", nx, w_in, out_dtype=BF16, name=f"{tag}_in")
    yc = _with_host(_conv_fwd, hosts, got, "mix_b", proj, conv_w, geo, f"{tag}_conv")
    h_out, y = _mm(yc, w_out, name=f"{tag}_out", gate=(h, mod4, 5, 1.0, geo))
    return h_out, (h, nx, proj, yc, y)


def _sconv_bwd(dh_out, saved, g, mod4, w_in, conv_w, w_out, geo, tag, hosts, got):
    h, nx, proj, yc, y = saved
    dy, dgate = _gate_bwd(dh_out, y, mod4, 5, 1.0, geo, f"{tag}_dgate")
    dyc = _with_host(_mm, hosts, got, "mix_d", dy, w_out, tb=True, out_dtype=BF16, name=f"{tag}_dyc")
    dw_out = _tn_wide(yc, dy, f"{tag}_dwout")
    dproj, dconv = _with_host(_conv_bwd, hosts, got, "mix_c", proj, dyc, conv_w, geo, f"{tag}_dconv")
    dnx = _with_host(_mm, hosts, got, "mix_b", dproj, w_in, tb=True, name=f"{tag}_dnx")
    dw_in = _with_host(_tn_wide, hosts, got, "mix_a", nx, dproj, f"{tag}_dwin")
    dh, dg, dshift, dscale = _pre_bwd(h, g, mod4, 3, dnx, dh_out, geo, f"{tag}_dpre")
    return dh, dg, (dshift, dscale, dgate), dw_in, dconv, dw_out


def _rope_swap(v):
    nf = QK_ROPE // 4
    return v.reshape(v.shape[:-1] + (2, 2, nf)).swapaxes(-3, -2).reshape(v.shape)


def _rope_tables(geo):
    n = geo.n_lat
    nf = QK_ROPE // 4
    pos = np.arange(n)
    inv = ROPE_BASE ** (-np.arange(nf, dtype=np.float32) / nf)
    ang = np.concatenate([(pos // GRID_W)[:, None] * inv, (pos % GRID_W)[:, None] * inv], axis=1).astype(np.float32)
    cos, sin = np.cos(ang), np.sin(ang)
    zeros = np.zeros((n, LANE - QK_ROPE), np.float32)
    c_lat = np.concatenate([cos, cos, zeros], axis=1)
    s_lat = np.concatenate([-sin, sin, zeros], axis=1)
    c_ctx = np.concatenate([np.ones((2 * geo.n_ctx, QK_ROPE), np.float32), np.zeros((2 * geo.n_ctx, LANE - QK_ROPE), np.float32)], 1)
    s_ctx = np.zeros((2 * geo.n_ctx, LANE), np.float32)
    return (jnp.asarray(np.concatenate([c_lat, c_lat, c_ctx], 0)), jnp.asarray(np.concatenate([s_lat, s_lat, s_ctx], 0)))


def _swap_halves(v):
    lanes = lax.broadcasted_iota(jnp.int32, v.shape, 1)
    return jnp.where(lanes < QK_ROPE // 2, pltpu.roll(v, LANE - QK_ROPE // 2, 1), pltpu.roll(v, QK_ROPE // 2, 1))


def _latent_norm_fwd(down, g_qa, g_kva, geo, name):
    t, wd = down.shape
    ql, kl = g_qa.shape[1], g_kva.shape[1]
    tile = geo.mm_tile

    def body(d_ref, gq_ref, gk_ref, cq_ref, ckv_ref):
        for lo, n, g_ref, o_ref in ((0, ql, gq_ref, cq_ref), (ql, kl, gk_ref, ckv_ref)):
            x = d_ref[:, lo:lo + n]
            r = lax.rsqrt(jnp.mean(x * x, axis=-1, keepdims=True) + EPS)
            o_ref[...] = (x * r * g_ref[...]).astype(o_ref.dtype)

    return pl.pallas_call(
        body, name=name, grid=(t // tile,),
        in_specs=[pl.BlockSpec((tile, wd), lambda i: (i, 0)), pl.BlockSpec((1, ql), lambda i: (0, 0)),
                  pl.BlockSpec((1, kl), lambda i: (0, 0))],
        out_specs=(pl.BlockSpec((tile, ql), lambda i: (i, 0)), pl.BlockSpec((tile, kl), lambda i: (i, 0))),
        out_shape=(jax.ShapeDtypeStruct((t, ql), BF16), jax.ShapeDtypeStruct((t, kl), BF16)),
        compiler_params=_params("parallel"),
    )(down, g_qa, g_kva)


def _latent_norm_bwd(down, g_qa, g_kva, dcqn, dckvn, dkr, geo, name):
    t, wd = down.shape
    ql, kl = g_qa.shape[1], g_kva.shape[1]
    tile = geo.mm_tile

    def body(d_ref, gq_ref, gk_ref, dq_ref, dk_ref, dkr_ref, o_ref, dgq_ref, dgk_ref):
        i = pl.program_id(0)

        @pl.when(i == 0)
        def _():
            dgq_ref[...] = jnp.zeros_like(dgq_ref)
            dgk_ref[...] = jnp.zeros_like(dgk_ref)

        for lo, n, g_ref, dy_ref, dg_ref in ((0, ql, gq_ref, dq_ref, dgq_ref), (ql, kl, gk_ref, dk_ref, dgk_ref)):
            x = d_ref[:, lo:lo + n]
            dy = dy_ref[...].astype(F32)
            r = lax.rsqrt(jnp.mean(x * x, axis=-1, keepdims=True) + EPS)
            xhat = x * r
            u = dy * g_ref[...]
            o_ref[:, lo:lo + n] = (r * (u - xhat * jnp.mean(u * xhat, axis=-1, keepdims=True))).astype(o_ref.dtype)
            dg_ref[...] += jnp.sum(dy * xhat, axis=0, keepdims=True)
        o_ref[:, ql + kl:] = dkr_ref[...].astype(o_ref.dtype)

    def row(n):
        return pl.BlockSpec((tile, n), lambda i: (i, 0))

    def vec(n):
        return pl.BlockSpec((1, n), lambda i: (0, 0))

    return pl.pallas_call(
        body, name=name, grid=(t // tile,),
        in_specs=[row(wd), vec(ql), vec(kl), row(ql), row(kl), row(wd - ql - kl)],
        out_specs=(row(wd), vec(ql), vec(kl)),
        out_shape=(jax.ShapeDtypeStruct((t, wd), BF16), jax.ShapeDtypeStruct((1, ql), F32),
                   jax.ShapeDtypeStruct((1, kl), F32)),
        compiler_params=_params("arbitrary"),
    )(down, g_qa, g_kva, dcqn, dckvn, dkr)


def _qk_specs(geo, xr_col, shared_rope):
    tile = geo.mm_tile
    xn_spec = pl.BlockSpec((tile, HEADS * LANE), lambda i: (i, 0))
    if shared_rope:
        xr_spec = pl.BlockSpec((tile, LANE), lambda i: (i, xr_col))
    else:
        xr_spec = pl.BlockSpec((tile, HEADS * LANE), lambda i: (i, xr_col // HEADS))
    vec = pl.BlockSpec((1, LANE), lambda i: (0, 0))
    tab = pl.BlockSpec((tile, LANE), lambda i: (i, 0))
    return tile, xn_spec, xr_spec, vec, tab


def _qk_norm(xn, xr):
    ss = jnp.sum(xn * xn, axis=-1, keepdims=True) + jnp.sum(xr * xr, axis=-1, keepdims=True)
    return lax.rsqrt(ss * (1.0 / QK_HEAD) + EPS)


def _head_lanes(ref, hh, shared=False):
    return ref[...] if shared else ref[:, hh * LANE:(hh + 1) * LANE]


def _qk_fwd(xn_arr, xr_arr, xr_col, shared_rope, gn, gr, cos, sin, geo, name):
    t = xn_arr.shape[0]
    tile, xn_spec, xr_spec, vec, tab = _qk_specs(geo, xr_col, shared_rope)

    def body(xn_ref, xr_ref, gn_ref, gr_ref, c_ref, s_ref, o_ref):
        cv, sv, gnv, grv = c_ref[...], s_ref[...], gn_ref[...], gr_ref[...]
        for hh in range(HEADS):
            xn = _head_lanes(xn_ref, hh).astype(F32)
            xr = _head_lanes(xr_ref, hh, shared_rope).astype(F32)
            r = _qk_norm(xn, xr)
            yr = xr * r * grv
            o_ref[:, hh * HEAD_PAD:hh * HEAD_PAD + LANE] = (xn * r * gnv).astype(o_ref.dtype)
            o_ref[:, hh * HEAD_PAD + LANE:(hh + 1) * HEAD_PAD] = (yr * cv + _swap_halves(yr) * sv).astype(o_ref.dtype)

    return pl.pallas_call(
        body, name=name, grid=(t // tile,), in_specs=[xn_spec, xr_spec, vec, vec, tab, tab],
        out_specs=pl.BlockSpec((tile, HEADS * HEAD_PAD), lambda i: (i, 0)),
        out_shape=jax.ShapeDtypeStruct((t, HEADS * HEAD_PAD), BF16), compiler_params=_params("parallel"),
    )(xn_arr, xr_arr, gn, gr, cos, sin)


def _qk_bwd(xn_arr, xr_arr, xr_col, shared_rope, gn, gr, cos, sin, dout, geo, name):
    t = xn_arr.shape[0]
    tile, xn_spec, xr_spec, vec, tab = _qk_specs(geo, xr_col, shared_rope)
    half = HEADS * LANE
    if shared_rope:
        n_lat_tiles = dout[0].shape[0] // tile
        assert dout[0].shape[0] % tile == 0 and dout[1].shape[0] % tile == 0

    def body(*refs):
        if shared_rope:
            xn_ref, xr_ref, gn_ref, gr_ref, c_ref, s_ref, dl_ref, dc_ref, vl_ref, vc_ref, raw_ref, dxr_ref, dgn_ref, dgr_ref = refs
        else:
            xn_ref, xr_ref, gn_ref, gr_ref, c_ref, s_ref, d_ref, raw_ref, dgn_ref, dgr_ref = refs
        i = pl.program_id(0)
        cv, sv, gnv, grv = c_ref[...], s_ref[...], gn_ref[...], gr_ref[...]
        dgn = jnp.zeros((1, LANE), F32)
        dgr = jnp.zeros((1, LANE), F32)
        dxr_sum = jnp.zeros((tile, LANE), F32)
        if shared_rope:
            latent = i < n_lat_tiles
            raw_ref[:, half:] = jnp.where(latent, vl_ref[...], vc_ref[...])
        for hh in range(HEADS):
            xn = _head_lanes(xn_ref, hh).astype(F32)
            xr = _head_lanes(xr_ref, hh, shared_rope).astype(F32)
            r = _qk_norm(xn, xr)
            xhn, xhr = xn * r, xr * r
            lo = hh * HEAD_PAD
            if shared_rope:
                dhead = jnp.where(latent, dl_ref[:, lo:lo + HEAD_PAD], dc_ref[:, lo:lo + HEAD_PAD]).astype(F32)
            else:
                dhead = d_ref[:, lo:lo + HEAD_PAD].astype(F32)
            dyn, dro = dhead[:, :LANE], dhead[:, LANE:]
            dyr = dro * cv + _swap_halves(dro * sv)
            un, ur = dyn * gnv, dyr * grv
            mean = (jnp.sum(un * xhn, axis=-1, keepdims=True) + jnp.sum(ur * xhr, axis=-1, keepdims=True)) * (1.0 / QK_HEAD)
            raw_ref[:, hh * LANE:(hh + 1) * LANE] = (r * (un - xhn * mean)).astype(raw_ref.dtype)
            dxr = r * (ur - xhr * mean)
            if shared_rope:
                dxr_sum = dxr_sum + dxr
            else:
                raw_ref[:, half + hh * LANE:half + (hh + 1) * LANE] = dxr.astype(raw_ref.dtype)
            dgn = dgn + jnp.sum(dyn * xhn, axis=0, keepdims=True)
            dgr = dgr + jnp.sum(dyr * xhr, axis=0, keepdims=True)
        if shared_rope:
            dxr_ref[...] = dxr_sum

        @pl.when(i == 0)
        def _():
            dgn_ref[...] = jnp.zeros_like(dgn_ref)
            dgr_ref[...] = jnp.zeros_like(dgr_ref)

        dgn_ref[...] += dgn
        dgr_ref[...] += dgr

    raw_spec = pl.BlockSpec((tile, 2 * half), lambda i: (i, 0))
    raw_shape = jax.ShapeDtypeStruct((t, 2 * half), BF16)
    vec_shape = jax.ShapeDtypeStruct((1, LANE), F32)
    in_specs = [xn_spec, xr_spec, vec, vec, tab, tab]
    if shared_rope:
        def two(width):
            return [pl.BlockSpec((tile, width), lambda i: (jnp.minimum(i, n_lat_tiles - 1), 0)),
                    pl.BlockSpec((tile, width), lambda i: (jnp.maximum(i - n_lat_tiles, 0), 0))]

        return pl.pallas_call(
            body, name=name, grid=(t // tile,), in_specs=in_specs + two(HEADS * HEAD_PAD) + two(half),
            out_specs=(raw_spec, pl.BlockSpec((tile, LANE), lambda i: (i, 0)), vec, vec),
            out_shape=(raw_shape, jax.ShapeDtypeStruct((t, LANE), F32), vec_shape, vec_shape),
            compiler_params=_params("arbitrary"),
        )(xn_arr, xr_arr, gn, gr, cos, sin, *dout)
    return pl.pallas_call(
        body, name=name, grid=(t // tile,),
        in_specs=in_specs + [pl.BlockSpec((tile, HEADS * HEAD_PAD), lambda i: (i, 0))],
        out_specs=(raw_spec, vec, vec), out_shape=(raw_shape, vec_shape, vec_shape),
        compiler_params=_params("arbitrary"),
    )(xn_arr, xr_arr, gn, gr, cos, sin, dout)


def _attn_specs(geo):
    tq, nq = geo.n_ctx, geo.n_lat // geo.n_ctx

    def qrow(b, i):
        return jnp.where(i < nq, b * nq + i, 2 * nq + b)

    q_spec = pl.BlockSpec((tq, HEAD_PAD), lambda b, hh, i: (qrow(b, i), hh))
    kc_spec = pl.BlockSpec((geo.n_ctx, HEAD_PAD), lambda b, hh, i: (2 * nq + b, hh))
    kl_spec = pl.BlockSpec((geo.n_lat, HEAD_PAD), lambda b, hh, i: (b, hh))
    vc_spec = pl.BlockSpec((geo.n_ctx, V_HEAD), lambda b, hh, i: (2 * nq + b, HEADS + hh))
    vl_spec = pl.BlockSpec((geo.n_lat, V_HEAD), lambda b, hh, i: (b, HEADS + hh))
    o_spec = pl.BlockSpec((tq, V_HEAD), lambda b, hh, i: (qrow(b, i), hh))
    return tq, nq, q_spec, kc_spec, kl_spec, vc_spec, vl_spec, o_spec


def _attn_fwd(q, k, kv, with_ctx_q, geo, name, hosted=None):
    t = q.shape[0]
    tq, nq, q_spec, kc_spec, kl_spec, vc_spec, vl_spec, o_spec = _attn_specs(geo)

    def body(q_ref, kc_ref, kl_ref, vc_ref, vl_ref, o_ref):
        i = pl.program_id(2)
        qv = q_ref[...]
        s_c = _dot(qv, kc_ref[...], _NT) * QK_SCALE

        @pl.when(i < nq)
        def _():
            s_l = _dot(qv, kl_ref[...], _NT) * QK_SCALE
            m = jnp.maximum(jnp.max(s_c, axis=-1, keepdims=True), jnp.max(s_l, axis=-1, keepdims=True))
            p_c, p_l = jnp.exp(s_c - m), jnp.exp(s_l - m)
            den = jnp.sum(p_c, axis=-1, keepdims=True) + jnp.sum(p_l, axis=-1, keepdims=True)
            o = _dot(p_c, vc_ref[...], _NN) + _dot(p_l, vl_ref[...], _NN)
            o_ref[...] = (o / den).astype(o_ref.dtype)

        @pl.when(i == nq)
        def _():
            if with_ctx_q:
                m = jnp.max(s_c, axis=-1, keepdims=True)
                p_c = jnp.exp(s_c - m)
                o = _dot(p_c, vc_ref[...], _NN) / jnp.sum(p_c, axis=-1, keepdims=True)
                o_ref[...] = o.astype(o_ref.dtype)
            else:
                o_ref[...] = jnp.zeros_like(o_ref)

    return _call(
        body, hosted, name=name, grid=(2, HEADS, nq + 1), in_specs=[q_spec, kc_spec, kl_spec, vc_spec, vl_spec],
        out_specs=o_spec, out_shape=jax.ShapeDtypeStruct((t, HEADS * V_HEAD), BF16),
        compiler_params=_params("parallel", "parallel", "arbitrary"),
    )(q, k, k, kv, kv)


def _attn_bwd(q, k, kv, do, with_ctx_q, geo, name, hosted=None):
    t = q.shape[0]
    tq, nq, q_spec, kc_spec, kl_spec, vc_spec, vl_spec, o_spec = _attn_specs(geo)

    def body(q_ref, kc_ref, kl_ref, vc_ref, vl_ref, do_ref, dq_ref, dkl_ref, dkc_ref, dvl_ref, dvc_ref,
             akl_ref, akc_ref, avl_ref, avc_ref):
        i = pl.program_id(2)

        @pl.when(i == 0)
        def _():
            for ref in (akl_ref, akc_ref, avl_ref, avc_ref):
                ref[...] = jnp.zeros_like(ref)

        qv, dov = q_ref[...], do_ref[...]
        s_c = _dot(qv, kc_ref[...], _NT) * QK_SCALE
        dp_c = _dot(dov, vc_ref[...], _NT)

        def ctx_part(p_c, delta):
            ds_c = (p_c * (dp_c - delta) * QK_SCALE).astype(BF16)
            akc_ref[...] += _dot(ds_c, qv, _TN)
            avc_ref[...] += _dot(p_c, dov, _TN)
            return _dot(ds_c, kc_ref[...], _NN)

        @pl.when(i < nq)
        def _():
            s_l = _dot(qv, kl_ref[...], _NT) * QK_SCALE
            m = jnp.maximum(jnp.max(s_c, axis=-1, keepdims=True), jnp.max(s_l, axis=-1, keepdims=True))
            p_c, p_l = jnp.exp(s_c - m), jnp.exp(s_l - m)
            inv = 1.0 / (jnp.sum(p_c, axis=-1, keepdims=True) + jnp.sum(p_l, axis=-1, keepdims=True))
            p_c, p_l = p_c * inv, p_l * inv
            dp_l = _dot(dov, vl_ref[...], _NT)
            delta = jnp.sum(p_c * dp_c, axis=-1, keepdims=True) + jnp.sum(p_l * dp_l, axis=-1, keepdims=True)
            ds_l = (p_l * (dp_l - delta) * QK_SCALE).astype(BF16)
            akl_ref[...] += _dot(ds_l, qv, _TN)
            avl_ref[...] += _dot(p_l, dov, _TN)
            dq_ref[...] = (ctx_part(p_c, delta) + _dot(ds_l, kl_ref[...], _NN)).astype(dq_ref.dtype)

        @pl.when(i == nq)
        def _():
            if with_ctx_q:
                m = jnp.max(s_c, axis=-1, keepdims=True)
                p_c = jnp.exp(s_c - m)
                p_c = p_c * (1.0 / jnp.sum(p_c, axis=-1, keepdims=True))
                delta = jnp.sum(p_c * dp_c, axis=-1, keepdims=True)
                dq_ref[...] = ctx_part(p_c, delta).astype(dq_ref.dtype)
            else:
                dq_ref[...] = jnp.zeros_like(dq_ref)
            dkl_ref[...] = akl_ref[...].astype(dkl_ref.dtype)
            dkc_ref[...] = akc_ref[...].astype(dkc_ref.dtype)
            dvl_ref[...] = avl_ref[...].astype(dvl_ref.dtype)
            dvc_ref[...] = avc_ref[...].astype(dvc_ref.dtype)

    def acc_spec(rows, width):
        return pl.BlockSpec((rows, width), lambda b, hh, i: (b, hh))

    return _call(
        body, hosted, name=name, grid=(2, HEADS, nq + 1), in_specs=[q_spec, kc_spec, kl_spec, vc_spec, vl_spec, o_spec],
        out_specs=(q_spec, acc_spec(geo.n_lat, HEAD_PAD), acc_spec(geo.n_ctx, HEAD_PAD), acc_spec(geo.n_lat, V_HEAD),
                   acc_spec(geo.n_ctx, V_HEAD)),
        out_shape=(jax.ShapeDtypeStruct((t, HEADS * HEAD_PAD), BF16),
                   jax.ShapeDtypeStruct((2 * geo.n_lat, HEADS * HEAD_PAD), BF16),
                   jax.ShapeDtypeStruct((2 * geo.n_ctx, HEADS * HEAD_PAD), BF16),
                   jax.ShapeDtypeStruct((2 * geo.n_lat, HEADS * V_HEAD), BF16),
                   jax.ShapeDtypeStruct((2 * geo.n_ctx, HEADS * V_HEAD), BF16)),
        scratch_shapes=[pltpu.VMEM((geo.n_lat, HEAD_PAD), F32), pltpu.VMEM((geo.n_ctx, HEAD_PAD), F32),
                        pltpu.VMEM((geo.n_lat, V_HEAD), F32), pltpu.VMEM((geo.n_ctx, V_HEAD), F32)],
        compiler_params=_params("parallel", "parallel", "arbitrary"),
    )(q, k, k, kv, kv, do)


def _mla_fwd(h, g, mod4, w, with_ctx_q, tabs, geo, tag, hosts, got):
    cos, sin = tabs
    ql, kl = w["g_qa"].shape[1], w["g_kva"].shape[1]
    kr_col = (ql + kl) // LANE
    nx = _pre_fwd(h, g, mod4, 3, geo, f"{tag}_pre")
    down = _mm(nx, w["w_a"], name=f"{tag}_down")
    cqn, ckvn = _latent_norm_fwd(down, w["g_qa"], w["g_kva"], geo, f"{tag}_lnorm")
    qraw = _mm(cqn, w["w_uq"], out_dtype=BF16, name=f"{tag}_uq")
    kvraw = _mm(ckvn, w["w_ukv"], out_dtype=BF16, name=f"{tag}_ukv")
    q = _qk_fwd(qraw, qraw, HEADS, False, w["gq_n"], w["gq_r"], cos, sin, geo, f"{tag}_qnorm")
    k = _qk_fwd(kvraw, down, kr_col, True, w["gk_n"], w["gk_r"], cos, sin, geo, f"{tag}_knorm")
    o = _with_host(_attn_fwd, hosts, got, "mix_a", q, k, kvraw, with_ctx_q, geo, f"{tag}_attn")
    h_out, y = _mm(o, w["w_o"], name=f"{tag}_o", gate=(h, mod4, 5, 1.0, geo))
    return h_out, (h, nx, down, cqn, ckvn, qraw, kvraw, q, k, o, y)


def _mla_bwd(dh_out, saved, g, mod4, w, with_ctx_q, tabs, geo, tag, hosts, got):
    cos, sin = tabs
    h, nx, down, cqn, ckvn, qraw, kvraw, q, k, o, y = saved
    ql, kl = w["g_qa"].shape[1], w["g_kva"].shape[1]
    kr_col = (ql + kl) // LANE
    dy, dgate = _gate_bwd(dh_out, y, mod4, 5, 1.0, geo, f"{tag}_dgate")
    do = _mm(dy, w["w_o"], tb=True, out_dtype=BF16, name=f"{tag}_do")
    dw_o = _tn_wide(o, dy, f"{tag}_dwo")
    dq, dk_lat, dk_ctx, dv_lat, dv_ctx = _with_host(_attn_bwd, hosts, got, "mix_a", q, k, kvraw, do, with_ctx_q, geo,
                                                    f"{tag}_dattn")
    dqraw, dgq_n, dgq_r = _qk_bwd(qraw, qraw, HEADS, False, w["gq_n"], w["gq_r"], cos, sin, dq, geo, f"{tag}_dqnorm")
    dkvraw, dkr, dgk_n, dgk_r = _qk_bwd(kvraw, down, kr_col, True, w["gk_n"], w["gk_r"], cos, sin,
                                        (dk_lat, dk_ctx, dv_lat, dv_ctx), geo, f"{tag}_dknorm")
    dcqn = _mm(dqraw, w["w_uq"], tb=True, out_dtype=BF16, name=f"{tag}_dcqn")
    dw_uq = _tn_wide(cqn, dqraw, f"{tag}_dwuq")
    dckvn = _mm(dkvraw, w["w_ukv"], tb=True, out_dtype=BF16, name=f"{tag}_dckvn")
    dw_ukv = _tn_wide(ckvn, dkvraw, f"{tag}_dwukv")
    ddown, dg_qa, dg_kva = _latent_norm_bwd(down, w["g_qa"], w["g_kva"], dcqn, dckvn, dkr, geo, f"{tag}_dlnorm")
    dnx = _mm(ddown, w["w_a"], tb=True, name=f"{tag}_dnx")
    dw_a = _tn_wide(nx, ddown, f"{tag}_dwa")
    dh, dg, dshift, dscale = _pre_bwd(h, g, mod4, 3, dnx, dh_out, geo, f"{tag}_dpre")
    grads = dict(w_a=dw_a, g_qa=dg_qa, w_uq=dw_uq, g_kva=dg_kva, w_ukv=dw_ukv, gq_n=dgq_n, gq_r=dgq_r, gk_n=dgk_n,
                 gk_r=dgk_r, w_o=dw_o)
    return dh, dg, (dshift, dscale, dgate), grads


def _mla_prepare(w_a, g_qa, w_uq, g_kva, w_ukv, g_q, g_k, w_o):
    ql, kl = g_qa.shape[0], g_kva.shape[0]
    d = w_a.shape[0]
    w_a_pad = jnp.concatenate([w_a[:, :ql + kl], _rope_swap(w_a[:, ql + kl:]), jnp.zeros((d, LANE - QK_ROPE), w_a.dtype)], axis=1)
    uq = w_uq.reshape(ql, HEADS, QK_HEAD)
    uq_r = jnp.pad(_rope_swap(uq[:, :, QK_NOPE:]), ((0, 0), (0, 0), (0, LANE - QK_ROPE)))
    w_uq_pad = jnp.concatenate([uq[:, :, :QK_NOPE].reshape(ql, HEADS * LANE), uq_r.reshape(ql, HEADS * LANE)], axis=1)
    ukv = w_ukv.reshape(kl, HEADS, QK_NOPE + V_HEAD)
    w_ukv_p = jnp.concatenate([ukv[:, :, :QK_NOPE].reshape(kl, HEADS * LANE), ukv[:, :, QK_NOPE:].reshape(kl, HEADS * V_HEAD)], axis=1)

    def gains(gv):
        gv = gv.astype(F32)
        return gv[None, :QK_NOPE], jnp.pad(_rope_swap(gv[QK_NOPE:]), (0, LANE - QK_ROPE))[None]

    gq_n, gq_r = gains(g_q)
    gk_n, gk_r = gains(g_k)
    return dict(w_a=w_a_pad, g_qa=g_qa.astype(F32)[None], w_uq=w_uq_pad, g_kva=g_kva.astype(F32)[None], w_ukv=w_ukv_p,
                gq_n=gq_n, gq_r=gq_r, gk_n=gk_n, gk_r=gk_r, w_o=w_o)


def _mla_unprepare(gr):
    ql, kl = gr["g_qa"].shape[1], gr["g_kva"].shape[1]
    dw_a = jnp.concatenate([gr["w_a"][:, :ql + kl], _rope_swap(gr["w_a"][:, ql + kl:ql + kl + QK_ROPE])], axis=1)
    uqn = gr["w_uq"][:, :HEADS * LANE].reshape(ql, HEADS, LANE)
    uqr = _rope_swap(gr["w_uq"][:, HEADS * LANE:].reshape(ql, HEADS, LANE)[:, :, :QK_ROPE])
    dw_uq = jnp.concatenate([uqn, uqr], axis=2).reshape(ql, HEADS * QK_HEAD)
    ukn = gr["w_ukv"][:, :HEADS * LANE].reshape(kl, HEADS, LANE)
    ukv = gr["w_ukv"][:, HEADS * LANE:].reshape(kl, HEADS, V_HEAD)
    dw_ukv = jnp.concatenate([ukn, ukv], axis=2).reshape(kl, HEADS * (QK_NOPE + V_HEAD))

    def gains(gn, grr):
        return jnp.concatenate([gn[0], _rope_swap(grr[0, :QK_ROPE])])

    return dict(mla_w_a=dw_a, mla_g_qa=gr["g_qa"][0], mla_w_uq=dw_uq, mla_g_kva=gr["g_kva"][0], mla_w_ukv=dw_ukv,
                mla_g_q=gains(gr["gq_n"], gr["gq_r"]), mla_g_k=gains(gr["gk_n"], gr["gk_r"]), mla_w_o=gr["w_o"])


def _loss_head(h, target, geo, name):
    t, d = h.shape
    tile = geo.mm_tile
    n_lat_tiles = 2 * geo.n_lat // tile

    def body(h_ref, t_ref, dh_ref, loss_ref):
        i = pl.program_id(0)

        @pl.when(i == 0)
        def _():
            loss_ref[...] = jnp.zeros_like(loss_ref)

        @pl.when(i < n_lat_tiles)
        def _():
            e = h_ref[...] - t_ref[...]
            dh_ref[...] = e * (1.0 / d)
            part = jnp.sum(e * e, axis=0, keepdims=True) * (0.5 / d)
            loss_ref[...] += sum(part[:, j * LANE:(j + 1) * LANE] for j in range(d // LANE))

        @pl.when(i >= n_lat_tiles)
        def _():
            dh_ref[...] = jnp.zeros_like(dh_ref)

    row = pl.BlockSpec((tile, d), lambda i: (i, 0))
    tgt = pl.BlockSpec((tile, d), lambda i: (jnp.minimum(i, n_lat_tiles - 1), 0))
    dh, loss = pl.pallas_call(
        body, name=name, grid=(t // tile,), in_specs=[row, tgt],
        out_specs=(row, pl.BlockSpec((1, LANE), lambda i: (0, 0))),
        out_shape=(jax.ShapeDtypeStruct((t, d), F32), jax.ShapeDtypeStruct((1, LANE), F32)),
        compiler_params=_params("arbitrary"),
    )(h, target)
    return jnp.sum(loss), dh


def _adamw(w, g, m, v, name):
    shape = w.shape
    cols = shape[-1]
    rows = int(np.prod(shape[:-1])) if len(shape) > 1 else 1
    w2, g2, m2, v2 = (a.reshape(rows, cols) for a in (w, g, m, v))
    tr = _pick(rows, (512, 256, 128, 64, 32, 16, 8))
    c1 = 1.0 / (1.0 - ADAM_B1 ** ADAM_STEP)
    c2 = 1.0 / (1.0 - ADAM_B2 ** ADAM_STEP)

    def body(w_ref, g_ref, m_ref, v_ref, d_ref, mo_ref, vo_ref):
        gv = g_ref[...]
        mn = ADAM_B1 * m_ref[...] + (1.0 - ADAM_B1) * gv
        vn = ADAM_B2 * v_ref[...] + (1.0 - ADAM_B2) * (gv * gv)
        d_ref[...] = -ADAM_LR * ((mn * c1) / (jnp.sqrt(vn * c2) + ADAM_EPS) + ADAM_WD * w_ref[...])
        mo_ref[...] = mn
        vo_ref[...] = vn

    blk = pl.BlockSpec((tr, cols), lambda i: (i, 0))
    sds = jax.ShapeDtypeStruct((rows, cols), F32)
    d, mo, vo = pl.pallas_call(
        body, name=name, grid=(rows // tr,), in_specs=[blk] * 4, out_specs=(blk,) * 3, out_shape=(sds,) * 3,
        compiler_params=_params("parallel"),
    )(w2, g2, m2, v2)
    return d.reshape(shape), mo.reshape(shape), vo.reshape(shape)


SHARD_AXIS = {
    "w_mod": 2, "g_norm": 2, "ffn_w1": 3, "ffn_w3": 3, "ffn_w2": 2, "sc_w_in": 2, "sc_conv": 2, "sc_w_out": 1,
    "mla_w_a": 1, "mla_g_qa": 1, "mla_w_uq": 2, "mla_w_ukv": 2, "mla_w_o": 1,
}
HIDDEN_MAJOR = ("ffn_w1", "ffn_w3")


def _view(name, arr, swapped=False):
    form, swap, _ = EXCHANGE[name]
    if swap and not swapped:
        arr = jnp.swapaxes(arr, -1, -2)
    if form == "mid":
        arr = arr.reshape((-1,) + arr.shape[-2:])
        return jnp.pad(arr, ((0, 0), (0, 0), (0, -arr.shape[-1] % LANE)))
    arr = arr.reshape(-1, arr.shape[-1])
    return jnp.pad(arr, ((0, -arr.shape[0] % 16), (0, 0)))


def _unview(name, view, shape, keep_swapped=False):
    form, swap, _ = EXCHANGE[name]
    shape = shape[:-2] + (shape[-1], shape[-2]) if swap else shape
    if form == "mid":
        view = view[:, :, :shape[-1]]
    else:
        view = view[:int(np.prod(shape[:-1]))]
    arr = view.reshape(shape)
    return arr if (not swap or keep_swapped) else jnp.swapaxes(arr, -1, -2)


def _full_shape(name, local_shape):
    ax = SHARD_AXIS[name]
    return local_shape[:ax] + (N_DEV * local_shape[ax],) + local_shape[ax + 1:]


def _win(ref, form, n, j):
    start = j * n
    if not isinstance(start, int):
        start = pl.multiple_of(start, LANE if form == "last" else math.gcd(n, 16))
    if form == "mid":
        return ref.at[:, pl.ds(start, n), :]
    return ref.at[:, pl.ds(start, n)]


def _windows(view, count, of):
    return view.shape[:1] + (view.shape[1] * count // of,) + view.shape[2:]


def _gather_work(views, forms):
    na = len(views)

    def plan(x_refs, out_refs, sems):
        send_sems, recv_sems, local_sems = sems
        x, y, c = lax.axis_index("x"), lax.axis_index("y"), lax.axis_index("c")
        me, sibling = (x, y, c), (x, y, 1 - c)
        chips = [(1 - x, y), (x, 1 - y), (1 - x, 1 - y)]

        def copy(a, k, block, to, from_input):
            dst = _win(out_refs[a], forms[a], views[a].shape[1], 4 * block[0] + 2 * block[1] + block[2])
            return pltpu.make_async_remote_copy(
                src_ref=x_refs[a] if from_input else dst, dst_ref=dst, send_sem=send_sems.at[a, k],
                recv_sem=recv_sems.at[a, k], device_id=to, device_id_type=MESH)

        mine = [pltpu.make_async_copy(x_refs[a], _win(out_refs[a], forms[a], views[a].shape[1], 4 * x + 2 * y + c),
                                      local_sems.at[a]) for a in range(na)]
        first = []
        for a in range(na):
            first.append(copy(a, 0, me, sibling, True))
            first += [copy(a, 1 + j, me, (*chip, c), True) for j, chip in enumerate(chips)]
        return copy, mine, first, me, sibling, chips, c

    def start(x_refs, out_refs, sems):
        _, mine, first, *_ = plan(x_refs, out_refs, sems)
        for cp in mine + first:
            cp.start()

    def finish(x_refs, out_refs, sems):
        copy, mine, first, me, sibling, chips, c = plan(x_refs, out_refs, sems)
        passed = []
        for j, chip in enumerate(chips):
            for a in range(na):
                copy(a, 1 + j, (*chip, c), me, False).wait_recv()
                fwd = copy(a, 4 + j, (*chip, c), sibling, False)
                fwd.start()
                passed.append(fwd)
        for a in range(na):
            copy(a, 0, sibling, me, False).wait_recv()
            for j, chip in enumerate(chips):
                copy(a, 4 + j, (*chip, 1 - c), me, False).wait_recv()
        for cp in first + passed:
            cp.wait_send()
        for cp in mine:
            cp.wait()

    return Hosted(
        list(views), [jax.ShapeDtypeStruct(_windows(v, N_DEV, 1), v.dtype) for v in views],
        [pltpu.SemaphoreType.DMA((na, 7)), pltpu.SemaphoreType.DMA((na, 7)), pltpu.SemaphoreType.DMA((na,))], start, finish)


def _push_work(srcs, out_shapes, n_copies, make_copies):
    na = len(srcs)

    def start(s_refs, r_refs, sems):
        for cp in make_copies(s_refs, r_refs, sems[0], sems[1]):
            cp.start()

    def finish(s_refs, r_refs, sems):
        copies = make_copies(s_refs, r_refs, sems[0], sems[1])
        for cp in copies:
            cp.wait_recv()
        for cp in copies:
            cp.wait_send()

    return Hosted(list(srcs), out_shapes, [pltpu.SemaphoreType.DMA((na, n_copies)), pltpu.SemaphoreType.DMA((na, n_copies))],
                  start, finish)


def _sibling_work(fulls, forms):
    na = len(fulls)
    widths = [f.shape[1] // N_DEV for f in fulls]

    def make_copies(g_refs, r_refs, send_sems, recv_sems):
        x, y, c = lax.axis_index("x"), lax.axis_index("y"), lax.axis_index("c")
        return [
            pltpu.make_async_remote_copy(
                src_ref=_win(g_refs[a], forms[a], widths[a], 2 * chip + (1 - c)),
                dst_ref=_win(r_refs[a], forms[a], widths[a], chip), send_sem=send_sems.at[a, chip],
                recv_sem=recv_sems.at[a, chip], device_id=(x, y, 1 - c), device_id_type=MESH)
            for a in range(na) for chip in range(N_CHIP)
        ]

    return _push_work(fulls, [jax.ShapeDtypeStruct(_windows(f, N_CHIP, N_DEV), f.dtype) for f in fulls], N_CHIP, make_copies)


def _chip_work(parts, forms):
    na = len(parts)
    widths = [p.shape[1] // N_CHIP for p in parts]

    def make_copies(p_refs, r_refs, send_sems, recv_sems):
        x, y, c = lax.axis_index("x"), lax.axis_index("y"), lax.axis_index("c")
        chips = [(1 - x, y), (x, 1 - y), (1 - x, 1 - y)]
        return [
            pltpu.make_async_remote_copy(
                src_ref=_win(p_refs[a], forms[a], widths[a], 2 * px + py), dst_ref=_win(r_refs[a], forms[a], widths[a], j),
                send_sem=send_sems.at[a, j], recv_sem=recv_sems.at[a, j], device_id=(px, py, c), device_id_type=MESH)
            for a in range(na) for j, (px, py) in enumerate(chips)
        ]

    return _push_work(parts, [jax.ShapeDtypeStruct(_windows(p, 3, N_CHIP), p.dtype) for p in parts], 3, make_copies)


def _sum_tiles(view, form, n):
    if form == "mid":
        tr = n
        while tr * view.shape[2] * 4 > 2 * 1024 * 1024 and tr % 32 == 0:
            tr //= 2
        return 1, tr
    return _pick(view.shape[0], (512, 256, 128, 64, 32, 16)), n


def _window_spec(form, tl, tr, rest, window_of):
    if form == "mid":
        return lambda per: pl.BlockSpec((None, tr) + rest, lambda l, k, i, s: (l, window_of(k, s) * per + i, 0))
    return lambda per: pl.BlockSpec((tl, tr), lambda l, k, i, s: (l, window_of(k, s)))


def _chip_partials(g, recv, core, form, name):
    n = g.shape[1] // N_DEV
    tl, tr = _sum_tiles(g, form, n)
    per = n // tr
    rest = tuple(g.shape[2:])

    def body(core_ref, g_ref, r_ref, o_ref):
        o_ref[...] = (g_ref[...] + r_ref[...]).astype(o_ref.dtype)

    own = _window_spec(form, tl, tr, rest, lambda k, s: 2 * k + s[0])(per)
    by_chip = _window_spec(form, tl, tr, rest, lambda k, s: k)(per)
    return pl.pallas_call(
        body, name=name,
        grid_spec=pltpu.PrefetchScalarGridSpec(
            num_scalar_prefetch=1, grid=(g.shape[0] // tl, N_CHIP, per), in_specs=[own, by_chip], out_specs=by_chip),
        out_shape=jax.ShapeDtypeStruct(recv.shape, BF16), compiler_params=_params("parallel", "parallel", "parallel"),
    )(core, g, recv)


def _reduce_final(p, recv, chip, form, name):
    n = p.shape[1] // N_CHIP
    tl, tr = _sum_tiles(p, form, n)
    per = n // tr
    rest = tuple(p.shape[2:])

    def body(chip_ref, p_ref, ry_ref, rx_ref, rxy_ref, o_ref):
        own_pair = p_ref[...].astype(F32) + ry_ref[...].astype(F32)
        o_ref[...] = own_pair + (rx_ref[...].astype(F32) + rxy_ref[...].astype(F32))

    def rel(j):
        return _window_spec(form, tl, tr, rest, lambda k, s: j)(per)

    own = _window_spec(form, tl, tr, rest, lambda k, s: s[0])(per)
    return pl.pallas_call(
        body, name=name,
        grid_spec=pltpu.PrefetchScalarGridSpec(
            num_scalar_prefetch=1, grid=(p.shape[0] // tl, 1, per), in_specs=[own, rel(1), rel(0), rel(2)],
            out_specs=rel(0)),
        out_shape=jax.ShapeDtypeStruct(p.shape[:1] + (n,) + p.shape[2:], F32),
        compiler_params=_params("parallel", "parallel", "parallel"),
    )(chip, p, recv, recv, recv)


def _pack_replicated(arrays):
    pieces = []
    for a in arrays:
        flat = a.reshape(-1).astype(F32)
        pieces.append(jnp.pad(flat, (0, -flat.size % LANE)))
    total = sum(p.size for p in pieces)
    pieces.append(jnp.zeros((-total % (16 * LANE),), F32))
    return jnp.concatenate(pieces).reshape(-1, LANE)


def _unpack_replicated(buf, shapes):
    flat, out, off = buf.reshape(-1), [], 0
    for shape in shapes:
        size = int(np.prod(shape))
        out.append(flat[off:off + size].reshape(shape))
        off += size + (-size % LANE)
    return out


def _silu(v):
    return v * jax.nn.sigmoid(v)


FFN_NAMES = ("ffn_w1", "ffn_w3", "ffn_w2")
SC_NAMES = ("sc_w_in", "sc_conv", "sc_w_out")
MLA_SHARDED = ("mla_w_a", "mla_g_qa", "mla_w_uq", "mla_w_ukv", "mla_w_o")
MLA_NAMES = ("mla_w_a", "mla_g_qa", "mla_w_uq", "mla_g_kva", "mla_w_ukv", "mla_g_q", "mla_g_k", "mla_w_o")


def _local_step(src, x, c, ctx, target):
    bsz, n_lat, d = x.shape
    n_ctx = ctx.shape[1]
    assert bsz == 2
    geo = Geo(n_lat, n_ctx)
    depth = src.depth
    tc = _conv_tile(d)
    tabs = _rope_tables(geo)

    h = jnp.concatenate([x.reshape(2 * n_lat, d), ctx.reshape(2 * n_ctx, d)], axis=0)
    tgt = target.reshape(2 * n_lat, d)

    saved = []
    for i in range(depth):
        kind = i % 2
        wl, slots = src.weights(i), src.fwd_slots(i)
        gn = wl["g_norm"].astype(F32)
        mod4 = src.mod(i).reshape(N_SEG, N_MOD, 1, d)
        h, s1 = _ffn_fwd(h, gn[0:1], mod4, 0, wl, 0, geo, f"l{i}_f1", "f1", slots, slots)
        if kind == 0:
            mix = (_interleave(wl["sc_w_in"], 3, tc), wl["sc_conv"].astype(F32), wl["sc_w_out"])
            h, s2 = _sconv_fwd(h, gn[1:2], mod4, *mix, geo, f"l{i}_sc", slots, slots)
        else:
            mix = _mla_prepare(*[wl[name] for name in MLA_NAMES])
            h, s2 = _mla_fwd(h, gn[1:2], mod4, mix, i != depth - 1, tabs, geo, f"l{i}_mla", slots, slots)
        h, s3 = _ffn_fwd(h, gn[2:3], mod4, 6, wl, 1, geo, f"l{i}_f2", "f2", slots, slots)
        saved.append((wl, gn, mod4, mix, s1, s2, s3))

    loss, dh = _loss_head(h, tgt, geo, "loss_head")

    g_b_mod = [None] * depth
    for i in reversed(range(depth)):
        kind = i % 2
        wl, gn, mod4, mix, s1, s2, s3 = saved[i]
        slots = src.bwd_slots(i)
        gbuf = {name: lax.empty(wl[name].shape, F32) for name in ("ffn_w1", "ffn_w3", "ffn_w2")}
        dh, dg2, dm2 = _ffn_bwd(dh, s3, gn[2:3], mod4, 6, wl, 1, gbuf, geo, f"l{i}_f2", "f2", slots, slots)
        src.ffn2_grads(i, gbuf)
        if kind == 0:
            dh, dg1, dm1, dwin, dconv, dwout = _sconv_bwd(dh, s2, gn[1:2], mod4, *mix, geo, f"l{i}_sc", slots, slots)
            gl = dict(sc_w_in=_deinterleave(dwin, 3, tc), sc_conv=dconv, sc_w_out=dwout)
        else:
            dh, dg1, dm1, gm = _mla_bwd(dh, s2, gn[1:2], mod4, mix, i != depth - 1, tabs, geo, f"l{i}_mla", slots, slots)
            gl = _mla_unprepare(gm)
        dh, dg0, dm0 = _ffn_bwd(dh, s1, gn[0:1], mod4, 0, wl, 0, gbuf, geo, f"l{i}_f1", "f1", slots, slots)
        dmod = jnp.concatenate(list(dm0) + list(dm1) + list(dm2), axis=1).reshape(N_SEG, N_MOD * d)
        dmod8 = jnp.concatenate([dmod, jnp.zeros((8 - N_SEG, N_MOD * d), F32)], axis=0)
        g_b_mod[i] = jnp.sum(dmod, axis=0)
        gl.update(gbuf, g_norm=jnp.concatenate([dg0, dg1, dg2], axis=0))
        src.dmod(i, dmod8)
        src.grads(i, gl)

    grad_x = dh[:2 * n_lat].reshape(x.shape)
    return loss, grad_x, jnp.stack(g_b_mod)


class _Slots:
    def __init__(self, get, put):
        self.get, self._put = get, put

    def __setitem__(self, slot, outs):
        self._put(slot, outs)


FWD_PLAN = {
    0: {"f1_up": ("ffn_w1",), "f1_down": ("g_norm", "mix"), "mix_a": ("ffn_w3",), "mix_b": ("ffn_w2",)},
    1: {"f1_up": ("ffn_w1",), "mix_a": ("ffn_w3", "g_norm", "mix"), "f2_up": ("ffn_w2",)},
}
SIBLING_PLAN = {"f2_dact": ("ffn_w1", "g_norm", "mix"), "f2_dw2": ("ffn_w3", "ffn_w2")}
BWD_PLAN = {
    0: {"f2_dnx": ("ffn_w1",), "mix_b": ("ffn_w3",), "mix_a": ("ffn_w2",), "f1_dact": ("g_norm", "mix")},
    1: {"f2_dnx": ("ffn_w1",), "mix_a": ("ffn_w3", "ffn_w2"), "f1_dnx": ("g_norm", "mix")},
}
DMOD_SLOT = {0: "mix_c", 1: "f1_dact"}
MOD_ROWS = 32


class _Exchange:
    def __init__(self, w):
        self.w = w
        self.depth = w["w_mod"].shape[0]
        self.c_ctx = w["c_ctx"]
        self.me = 4 * lax.axis_index("x") + 2 * lax.axis_index("y") + lax.axis_index("c")
        self.core = lax.axis_index("c").astype(jnp.int32).reshape(1)
        self.chip = (2 * lax.axis_index("x") + lax.axis_index("y")).astype(jnp.int32).reshape(1)
        self.full, self.gviews, self.parts, self.reduced, self.rep, self.dmods = {}, {}, {}, {}, {}, {}
        self.ctx_pre = jnp.zeros_like(self.c_ctx)

    def _layer_of(self, name, i):
        return i // 2 if name.startswith(("sc_", "mla_")) else i

    def _mixer(self, i):
        return SC_NAMES if i % 2 == 0 else MLA_SHARDED

    def _expand(self, names, i):
        out = []
        for name in names:
            out += list(self._mixer(i)) if name == "mix" else [name]
        return out

    def _group(self, i):
        return ["g_norm", "ffn_w1", "ffn_w3", "ffn_w2"] + list(self._mixer(i))

    def _local(self, name, i):
        arr = self.w[name][self._layer_of(name, i)]
        return arr[:, None] if name == "mla_g_qa" else arr

    def _shapes(self, name, i):
        local = tuple(self._local(name, i).shape)
        ax = SHARD_AXIS[name] - 1
        return local, local[:ax] + (N_DEV * local[ax],) + local[ax + 1:]

    def _gather(self, names, i):
        views = [_view(n, self._local(n, i).astype(BF16 if EXCHANGE[n][2] else F32)) for n in names]
        return _gather_work(views, [EXCHANGE[n][0] for n in names])

    def _gathered(self, names, i, outs):
        for name, fv in zip(names, outs):
            arr = _unview(name, fv, self._shapes(name, i)[1], keep_swapped=name in HIDDEN_MAJOR)
            self.full[name, i] = arr[:, 0] if name == "mla_g_qa" else arr

    def prefetch(self, c):
        bsz, d = c.shape
        (conds,) = _run_hosted(_gather_work([jnp.pad(c, ((0, 8 - bsz), (0, 0)))[None]], ["mid"]), "gather_cond")
        conds = conds.reshape(N_DEV, 8, d)[:, :bsz]
        act = _silu(jnp.concatenate([conds, jnp.broadcast_to(self.c_ctx, (N_DEV, 1, d))], axis=1))
        self.s_rows = jnp.pad(act.reshape(N_DEV * N_SEG, d), ((0, MOD_ROWS - N_DEV * N_SEG), (0, 0)))
        cols = jnp.stack([_mm(self.s_rows, self.w["w_mod"][l], name=f"mod_cols_{l}") for l in range(self.depth)])
        names = self._group(0)
        work = self._gather(names, 0)
        both = _gather_work(work.inputs + [cols.reshape(self.depth * MOD_ROWS, -1)], self._forms(names) + ["last"])
        outs = _run_hosted(both, "gather_l0")
        self._gathered(names, 0, outs[:-1])
        mods = lax.dynamic_slice_in_dim(outs[-1].reshape(self.depth, MOD_ROWS, -1), N_SEG * self.me, N_SEG, axis=1)
        self.mods = mods + self.w["b_mod"][:, None, :]

    def mod(self, i):
        return self.mods[i]

    def weights(self, i):
        wl = {name: self.full[name, i] for name in self._group(i)}
        if i % 2 == 1:
            for name in ("mla_g_kva", "mla_g_q", "mla_g_k"):
                wl[name] = self.w[name][i // 2]
        return wl

    def fwd_slots(self, i):
        plan = FWD_PLAN[i % 2] if i + 1 < self.depth else {}
        names = {slot: self._expand(plan[slot], i + 1) for slot in plan}
        return _Slots(lambda slot: self._gather(names[slot], i + 1) if slot in names else None,
                      lambda slot, outs: self._gathered(names[slot], i + 1, outs))

    def ffn2_grads(self, i, gbuf):
        if i == 0:
            for name in FFN_NAMES:
                self.gviews[name + "#1", 0] = _view(name, gbuf[name][1:2], swapped=True)

    def grads(self, i, gl):
        for name in self._group(i):
            g = gl[name][:, None] if name == "mla_g_qa" else gl[name]
            if i == 0 and name in FFN_NAMES:
                self.gviews[name + "#0", 0] = _view(name, g[0:1], swapped=True)
            else:
                self.gviews[name, i] = _view(name, g, swapped=name in HIDDEN_MAJOR)
        for name in REPLICATED:
            if name in gl:
                self.rep[name, i // 2] = gl[name]

    def dmod(self, i, dmod8):
        self.dmods[i] = dmod8

    def _dmod_gather(self, i):
        return _gather_work([self.dmods[i][None]], ["mid"])

    def _dmod_gathered(self, i, outs):
        n = self.w["w_mod"].shape[2]
        rows = outs[0].reshape(N_DEV, 8, -1)[:, :N_SEG]
        mine = lax.dynamic_slice_in_dim(rows, n * self.me, n, axis=2)
        flat = jnp.pad(mine.reshape(N_DEV * N_SEG, n), ((0, MOD_ROWS - N_DEV * N_SEG), (0, 0)))
        self.reduced["w_mod", i] = _mm(self.s_rows, flat, ta=True, name=f"dwmod_{i}")
        ctx_rows = jnp.pad(jnp.sum(mine[:, N_SEG - 1], axis=0, keepdims=True), ((0, 7), (0, 0)))
        self.ctx_pre = self.ctx_pre + _mm(ctx_rows, self.w["w_mod"][i], tb=True, name=f"dcond_{i}")[0]

    def _forms(self, names):
        return [EXCHANGE[n.split("#")[0]][0] for n in names]

    def _partials(self, names, i, from_sibling):
        for name, recv in zip(names, from_sibling):
            self.parts[name, i] = _chip_partials(self.gviews[name, i], recv, self.core, self._forms([name])[0],
                                                 f"partial_{name.replace('#', '_')}_{i}")

    def _finals(self, names, i, from_chips):
        for name, recv in zip(names, from_chips):
            rv = _reduce_final(self.parts[name, i], recv, self.chip, self._forms([name])[0],
                               f"final_{name.replace('#', '_')}_{i}")
            base = name.split("#")[0]
            shape = self._shapes(base, i)[0]
            arr = _unview(base, rv, (1,) + shape[1:] if "#" in name else shape)
            self.reduced[name, i] = arr[:, 0] if name == "mla_g_qa" else arr

    def bwd_slots(self, i):
        if i + 1 >= self.depth:
            return _Slots(lambda slot: None, None)
        plan = BWD_PLAN[i % 2]
        chips = {slot: (self._expand(plan[slot], i + 1), i + 1) for slot in plan}
        sibling = {slot: (self._expand(SIBLING_PLAN[slot], i + 1), i + 1) for slot in SIBLING_PLAN}
        if i == 0:
            sibling["mix_d"] = ([name + "#1" for name in FFN_NAMES], 0)
            chips["f1_dw2"] = (["ffn_w1#1"], 0)
            chips["f1_dnx"] = (["ffn_w3#1", "ffn_w2#1"], 0)

        def get(slot):
            if slot in sibling:
                names, group = sibling[slot]
                return _sibling_work([self.gviews[n, group] for n in names], self._forms(names))
            if slot in chips:
                names, group = chips[slot]
                return _chip_work([self.parts[n, group] for n in names], self._forms(names))
            if slot == DMOD_SLOT[i % 2]:
                return self._dmod_gather(i + 1)
            return None

        def put(slot, outs):
            if slot in sibling:
                self._partials(*sibling[slot], outs)
            elif slot in chips:
                self._finals(*chips[slot], outs)
            else:
                self._dmod_gathered(i + 1, outs)

        return _Slots(get, put)

    def finish(self, rep_grads):
        group = [name + "#0" if name in FFN_NAMES else name for name in self._group(0)]
        self._dmod_gathered(0, _run_hosted(self._dmod_gather(0), "gather_dmod_l0"))
        rep_grads["c_ctx"] = self.ctx_pre
        for name in REPLICATED:
            if name not in rep_grads:
                rep_grads[name] = jnp.stack([self.rep[name, j] for j in range(self.w[name].shape[0])])
        rep = _pack_replicated([rep_grads[name] for name in REPLICATED])
        views = [self.gviews[n, 0] for n in group] + [jnp.tile(rep[None], (1, N_DEV, 1))]
        forms = self._forms(group) + ["mid"]
        from_sibling = _run_hosted(_sibling_work(views, forms), "reduce_sibling_l0")
        self._partials(group, 0, from_sibling[:-1])
        rep_part = _chip_partials(views[-1], from_sibling[-1], self.core, "mid", "partial_replicated")
        parts = [self.parts[n, 0] for n in group] + [rep_part]
        from_chips = _run_hosted(_chip_work(parts, forms), "reduce_chips_l0")
        self._finals(group, 0, from_chips[:-1])
        for name in FFN_NAMES:
            self.reduced[name, 0] = jnp.concatenate([self.reduced[name + "#0", 0], self.reduced[name + "#1", 0]], axis=0)
        rep_sum = _reduce_final(rep_part, from_chips[-1], self.chip, "mid", "final_replicated")
        out = dict(zip(REPLICATED, _unpack_replicated(rep_sum, [self.w[name].shape for name in REPLICATED])))
        sg = jax.nn.sigmoid(self.c_ctx)
        out["c_ctx"] = out["c_ctx"] * (sg * (1.0 + self.c_ctx * (1.0 - sg)))
        for name in EXCHANGE:
            layers = range(self.w[name].shape[0])
            step = 2 if name.startswith(("sc_", "mla_")) else 1
            first = 1 if name.startswith("mla_") else 0
            out[name] = jnp.stack([self.reduced[name, first + step * l] for l in layers])
        return out


def kernel(x, c, ctx, c_ctx, w_mod, b_mod, g_norm, ffn_w1, ffn_w3, ffn_w2, sc_w_in, sc_conv, sc_w_out, mla_w_a, mla_g_qa, mla_w_uq, mla_g_kva, mla_w_ukv, mla_g_q, mla_g_k, mla_w_o, loss_target, m_c_ctx, m_w_mod, m_b_mod, m_g_norm, m_ffn_w1, m_ffn_w3, m_ffn_w2, m_sc_w_in, m_sc_conv, m_sc_w_out, m_mla_w_a, m_mla_g_qa, m_mla_w_uq, m_mla_g_kva, m_mla_w_ukv, m_mla_g_q, m_mla_g_k, m_mla_w_o, v_c_ctx, v_w_mod, v_b_mod, v_g_norm, v_ffn_w1, v_ffn_w3, v_ffn_w2, v_sc_w_in, v_sc_conv, v_sc_w_out, v_mla_w_a, v_mla_g_qa, v_mla_w_uq, v_mla_g_kva, v_mla_w_ukv, v_mla_g_q, v_mla_g_k, v_mla_w_o):
    w = dict(c_ctx=c_ctx, w_mod=w_mod, b_mod=b_mod, g_norm=g_norm, ffn_w1=ffn_w1, ffn_w3=ffn_w3, ffn_w2=ffn_w2,
             sc_w_in=sc_w_in, sc_conv=sc_conv, sc_w_out=sc_w_out, mla_w_a=mla_w_a, mla_g_qa=mla_g_qa, mla_w_uq=mla_w_uq,
             mla_g_kva=mla_g_kva, mla_w_ukv=mla_w_ukv, mla_g_q=mla_g_q, mla_g_k=mla_g_k, mla_w_o=mla_w_o)
    m = dict(c_ctx=m_c_ctx, w_mod=m_w_mod, b_mod=m_b_mod, g_norm=m_g_norm, ffn_w1=m_ffn_w1, ffn_w3=m_ffn_w3,
             ffn_w2=m_ffn_w2, sc_w_in=m_sc_w_in, sc_conv=m_sc_conv, sc_w_out=m_sc_w_out, mla_w_a=m_mla_w_a,
             mla_g_qa=m_mla_g_qa, mla_w_uq=m_mla_w_uq, mla_g_kva=m_mla_g_kva, mla_w_ukv=m_mla_w_ukv, mla_g_q=m_mla_g_q,
             mla_g_k=m_mla_g_k, mla_w_o=m_mla_w_o)
    v = dict(c_ctx=v_c_ctx, w_mod=v_w_mod, b_mod=v_b_mod, g_norm=v_g_norm, ffn_w1=v_ffn_w1, ffn_w3=v_ffn_w3,
             ffn_w2=v_ffn_w2, sc_w_in=v_sc_w_in, sc_conv=v_sc_conv, sc_w_out=v_sc_w_out, mla_w_a=v_mla_w_a,
             mla_g_qa=v_mla_g_qa, mla_w_uq=v_mla_w_uq, mla_g_kva=v_mla_g_kva, mla_w_ukv=v_mla_w_ukv, mla_g_q=v_mla_g_q,
             mla_g_k=v_mla_g_k, mla_w_o=v_mla_w_o)
    exchange = _Exchange(w)
    exchange.prefetch(c)
    loss, grad_x, g_b_mod = _local_step(exchange, x, c, ctx, loss_target)
    loss = lax.psum(loss, ("x", "y", "c"))
    reduced = exchange.finish(dict(b_mod=g_b_mod))

    outs = [[], [], [], []]
    for name in WEIGHTS:
        delta, new_m, new_v = _adamw(w[name], reduced[name], m[name], v[name], f"adamw_{name}")
        for lst, val in zip(outs, (reduced[name], delta, new_m, new_v)):
            lst.append(val)
    return (loss, grad_x, *outs[0], *outs[1], *outs[2], *outs[3])
```

```python
import functools
import math

import jax
import jax.numpy as jnp
import numpy as np
from jax import lax
from jax.experimental import pallas as pl
from jax.experimental.pallas import tpu as pltpu

F32 = jnp.float32
BF16 = jnp.bfloat16

N_MOD = 9
HEADS = 8
QK_NOPE = 128
QK_ROPE = 64
QK_HEAD = QK_NOPE + QK_ROPE
V_HEAD = 128
GRID_W = 64
ROPE_BASE = 10000.0
QK_SCALE = QK_HEAD ** -0.5
EPS = 1e-6
ADAM_LR, ADAM_B1, ADAM_B2, ADAM_EPS, ADAM_WD, ADAM_STEP = 0.001, 0.9, 0.999, 1e-08, 0.01, 10

N_DEV = 8
N_CHIP = 4
N_SEG = 3
LANE = 128
HEAD_PAD = 2 * LANE
VMEM_LIMIT_BYTES = 48 * 1024 * 1024
MESH = pl.DeviceIdType.MESH

WEIGHTS = ["c_ctx", "w_mod", "b_mod", "g_norm", "ffn_w1", "ffn_w3", "ffn_w2", "sc_w_in", "sc_conv", "sc_w_out",
           "mla_w_a", "mla_g_qa", "mla_w_uq", "mla_g_kva", "mla_w_ukv", "mla_g_q", "mla_g_k", "mla_w_o"]
EXCHANGE = {
    "w_mod": ("last", False, True), "ffn_w1": ("mid", True, True), "ffn_w3": ("mid", True, True),
    "ffn_w2": ("mid", False, True), "sc_w_in": ("last", False, True), "sc_w_out": ("mid", False, True),
    "mla_w_a": ("mid", False, True), "mla_w_uq": ("mid", True, True), "mla_w_ukv": ("last", False, True),
    "mla_w_o": ("mid", False, True), "g_norm": ("last", False, False), "sc_conv": ("last", False, False),
    "mla_g_qa": ("mid", False, False),
}
REPLICATED = ["c_ctx", "b_mod", "mla_g_kva", "mla_g_q", "mla_g_k"]


def _pick(n, cands):
    for cand in cands:
        if n % cand == 0:
            return cand
    return n


def _params(*sem):
    return pltpu.CompilerParams(dimension_semantics=sem, vmem_limit_bytes=VMEM_LIMIT_BYTES)


def _hbm():
    return pl.BlockSpec(memory_space=pl.ANY)


class Hosted:
    def __init__(self, inputs, out_shapes, scratch, start, finish):
        self.inputs, self.out_shapes, self.scratch, self.start, self.finish = inputs, out_shapes, scratch, start, finish


def _call(body, hosted, **kw):
    if hosted is None:
        return pl.pallas_call(body, **kw)
    single = not isinstance(kw["out_shape"], (tuple, list))
    out_shape = [kw["out_shape"]] if single else list(kw["out_shape"])
    out_specs = [kw["out_specs"]] if single else list(kw["out_specs"])
    in_specs, scratch, grid = list(kw["in_specs"]), list(kw.get("scratch_shapes", ())), kw["grid"]
    n_in, n_out, n_scr = len(in_specs), len(out_shape), len(scratch)
    h_in, h_out = len(hosted.inputs), len(hosted.out_shapes)

    def wrapped(*refs):
        ins, hins = refs[:n_in], refs[n_in:n_in + h_in]
        o0 = n_in + h_in
        outs, houts = refs[o0:o0 + n_out], refs[o0 + n_out:o0 + n_out + h_out]
        s0 = o0 + n_out + h_out
        scr, hscr = refs[s0:s0 + n_scr], refs[s0 + n_scr:]
        first = functools.reduce(jnp.logical_and, [pl.program_id(a) == 0 for a in range(len(grid))])
        last = functools.reduce(jnp.logical_and, [pl.program_id(a) == g - 1 for a, g in enumerate(grid)])

        @pl.when(first)
        def _():
            hosted.start(hins, houts, hscr)

        body(*ins, *outs, *scr)

        @pl.when(last)
        def _():
            hosted.finish(hins, houts, hscr)

    call = pl.pallas_call(
        wrapped, name=kw["name"], grid=grid, in_specs=in_specs + [_hbm()] * h_in,
        out_specs=tuple(out_specs + [_hbm()] * h_out), out_shape=tuple(out_shape + list(hosted.out_shapes)),
        scratch_shapes=scratch + list(hosted.scratch), input_output_aliases=kw.get("input_output_aliases", {}),
        compiler_params=_params(*["arbitrary"] * len(grid)))

    def run(*args):
        res = call(*args, *hosted.inputs)
        comp = res[:n_out]
        return (comp[0] if single else tuple(comp)), list(res[n_out:])

    return run


def _run_hosted(hosted, name):
    def body(*refs):
        h_in, h_out = len(hosted.inputs), len(hosted.out_shapes)
        hins, houts, hscr = refs[:h_in], refs[h_in:h_in + h_out], refs[h_in + h_out:]
        hosted.start(hins, houts, hscr)
        hosted.finish(hins, houts, hscr)

    return list(pl.pallas_call(
        body, name=name, in_specs=[_hbm()] * len(hosted.inputs), out_specs=tuple([_hbm()] * len(hosted.out_shapes)),
        out_shape=tuple(hosted.out_shapes), scratch_shapes=list(hosted.scratch))(*hosted.inputs))


class Geo:
    def __init__(self, n_lat, n_ctx):
        self.n_lat, self.n_ctx = n_lat, n_ctx
        self.rows = 2 * n_lat + 2 * n_ctx
        self.tile = n_ctx
        assert n_lat % n_ctx == 0 and n_ctx % 16 == 0
        self.mm_tile = _pick(n_lat, (512, 256, 128)) if self.rows % _pick(n_lat, (512, 256, 128)) == 0 else n_ctx
        self.big_tile = _pick(self.rows, (1536, 768, 512, 256))

    def seg(self, i, tile):
        return jnp.minimum((i * tile) // self.n_lat, N_SEG - 1)

    def seg_start(self, i, tile):
        row = i * tile
        return jnp.logical_or(row % self.n_lat == 0, row == 2 * self.n_lat) & (row <= 2 * self.n_lat)


_NT = (((1,), (1,)), ((), ()))
_NN = (((1,), (0,)), ((), ()))
_TN = (((0,), (0,)), ((), ()))


def _dot(a, b, dims):
    return lax.dot_general(a.astype(BF16), b.astype(BF16), dims, preferred_element_type=F32)


def _mm(a, b, *, ta=False, tb=False, out_dtype=F32, name, gate=None, hosted=None):
    (kdim, m) = a.shape if ta else a.shape[::-1]
    n = b.shape[0] if tb else b.shape[1]
    assert (b.shape[1] if tb else b.shape[0]) == kdim
    if gate is not None:
        tm = gate[4].mm_tile
    else:
        tm = _pick(m, (1536, 768, 512, 256, 128))
    tn = _pick(n, (1024, 512, 256, 128))
    tk = _pick(kdim, (1024, 512, 256, 128))
    nk = kdim // tk
    dims = (((0 if ta else 1,), (1 if tb else 0,)), ((), ()))

    def body(*refs):
        if gate is not None:
            a_ref, b_ref, res_ref, gate_ref, o_ref, y_ref, acc_ref = refs
        else:
            a_ref, b_ref, o_ref, acc_ref = refs
        kk = pl.program_id(2)

        @pl.when(kk == 0)
        def _():
            acc_ref[...] = jnp.zeros_like(acc_ref)

        acc_ref[...] += lax.dot_general(a_ref[...].astype(BF16), b_ref[...].astype(BF16), dims,
                                        preferred_element_type=F32)

        @pl.when(kk == nk - 1)
        def _():
            acc = acc_ref[...]
            if gate is not None:
                y_ref[...] = acc.astype(y_ref.dtype)
                o_ref[...] = res_ref[...] + (gate[3] * gate_ref[...]) * acc
            else:
                o_ref[...] = acc.astype(o_ref.dtype)

    a_spec = pl.BlockSpec((tk, tm), lambda i, j, k: (k, i)) if ta else pl.BlockSpec((tm, tk), lambda i, j, k: (i, k))
    b_spec = pl.BlockSpec((tn, tk), lambda i, j, k: (j, k)) if tb else pl.BlockSpec((tk, tn), lambda i, j, k: (k, j))
    o_spec = pl.BlockSpec((tm, tn), lambda i, j, k: (i, j))
    in_specs, args = [a_spec, b_spec], [a, b]
    out_shape, out_specs = jax.ShapeDtypeStruct((m, n), out_dtype), o_spec
    if gate is not None:
        res, mod4, kmod, _, geo = gate
        in_specs += [o_spec, pl.BlockSpec((None, None, 1, tn), lambda i, j, k: (geo.seg(i, tm), kmod, 0, j))]
        args += [res, mod4]
        out_shape = (jax.ShapeDtypeStruct((m, n), F32), jax.ShapeDtypeStruct((m, n), BF16))
        out_specs = (o_spec, o_spec)
    return _call(
        body, hosted, name=name, grid=(m // tm, n // tn, nk), in_specs=in_specs, out_specs=out_specs,
        out_shape=out_shape, scratch_shapes=[pltpu.VMEM((tm, tn), F32)],
        compiler_params=_params("parallel", "parallel", "arbitrary"),
    )(*args)


def _tn_wide(lhs, rhs, name, into=None, s0=0, hosted=None):
    t, m = lhs.shape
    n = rhs.shape[1]
    tm = _pick(m, (1408, 1024, 512, 256, 128))
    while tm * n * 4 > 6.5 * 1024 * 1024 and tm % 256 == 0:
        tm //= 2
    tk = next(c for c in (1536, 768, 512, 256, 128, t)
              if t % c == 0 and c * (tm + n) * 4 + tm * n * 12 <= 36 * 1024 * 1024)

    def body(l_ref, r_ref, *rest):
        o_ref = rest[-1]
        kk = pl.program_id(1)
        part = lax.dot_general(l_ref[...], r_ref[...], _TN, preferred_element_type=F32)

        @pl.when(kk == 0)
        def _():
            o_ref[...] = part

        @pl.when(kk > 0)
        def _():
            o_ref[...] += part

    in_specs = [pl.BlockSpec((tk, tm), lambda i, k: (k, i)), pl.BlockSpec((tk, n), lambda i, k: (k, 0))]
    if into is None:
        return _call(
            body, hosted, name=name, grid=(m // tm, t // tk), in_specs=in_specs,
            out_specs=pl.BlockSpec((tm, n), lambda i, k: (i, 0)), out_shape=jax.ShapeDtypeStruct((m, n), F32),
            compiler_params=_params("parallel", "arbitrary"),
        )(lhs, rhs)
    return _call(
        body, hosted, name=name, grid=(m // tm, t // tk), in_specs=in_specs + [pl.BlockSpec(memory_space=pl.ANY)],
        out_specs=pl.BlockSpec((None, tm, n), lambda i, k: (s0, i, 0)),
        out_shape=jax.ShapeDtypeStruct(into.shape, into.dtype), input_output_aliases={2: 0},
        compiler_params=_params("parallel", "arbitrary"),
    )(lhs, rhs, into)


def _mod_spec(geo, tile, kmod, d):
    return pl.BlockSpec((None, None, 1, d), lambda i: (geo.seg(i, tile), kmod, 0, 0))


def _pre_fwd(h, g, mod4, k_shift, geo, name):
    t, d = h.shape
    tile = geo.mm_tile

    def body(h_ref, g_ref, sh_ref, sc_ref, o_ref):
        hv = h_ref[...]
        r = lax.rsqrt(jnp.mean(hv * hv, axis=-1, keepdims=True) + EPS)
        y = hv * r * g_ref[...]
        o_ref[...] = (y * (1.0 + sc_ref[...]) + sh_ref[...]).astype(o_ref.dtype)

    row = pl.BlockSpec((tile, d), lambda i: (i, 0))
    return pl.pallas_call(
        body, name=name, grid=(t // tile,),
        in_specs=[row, pl.BlockSpec((1, d), lambda i: (0, 0)), _mod_spec(geo, tile, k_shift, d),
                  _mod_spec(geo, tile, k_shift + 1, d)],
        out_specs=row, out_shape=jax.ShapeDtypeStruct((t, d), BF16), compiler_params=_params("parallel"),
    )(h, g, mod4, mod4)


def _pre_bwd(h, g, mod4, k_shift, dnx, dres, geo, name):
    t, d = h.shape
    tile = geo.mm_tile

    def body(h_ref, g_ref, sc_ref, dnx_ref, dres_ref, dh_ref, dg_ref, dsh_ref, dsc_ref):
        i = pl.program_id(0)
        hv, gv, dout = h_ref[...], g_ref[...], dnx_ref[...].astype(F32)
        r = lax.rsqrt(jnp.mean(hv * hv, axis=-1, keepdims=True) + EPS)
        xhat = hv * r
        dy = dout * (1.0 + sc_ref[...])
        u = dy * gv
        dh_ref[...] = r * (u - xhat * jnp.mean(u * xhat, axis=-1, keepdims=True)) + dres_ref[...]

        @pl.when(i == 0)
        def _():
            dg_ref[...] = jnp.zeros_like(dg_ref)

        @pl.when(geo.seg_start(i, tile))
        def _():
            dsh_ref[...] = jnp.zeros_like(dsh_ref)
            dsc_ref[...] = jnp.zeros_like(dsc_ref)

        dg_ref[...] += jnp.sum(dy * xhat, axis=0, keepdims=True)
        dsh_ref[...] += jnp.sum(dout, axis=0, keepdims=True)
        dsc_ref[...] += jnp.sum(dout * (xhat * gv), axis=0, keepdims=True)

    row = pl.BlockSpec((tile, d), lambda i: (i, 0))
    vec = pl.BlockSpec((1, d), lambda i: (0, 0))
    segv = pl.BlockSpec((None, 1, d), lambda i: (geo.seg(i, tile), 0, 0))
    return pl.pallas_call(
        body, name=name, grid=(t // tile,),
        in_specs=[row, vec, _mod_spec(geo, tile, k_shift + 1, d), row, row],
        out_specs=(row, vec, segv, segv),
        out_shape=(jax.ShapeDtypeStruct((t, d), F32), jax.ShapeDtypeStruct((1, d), F32),
                   jax.ShapeDtypeStruct((N_SEG, 1, d), F32), jax.ShapeDtypeStruct((N_SEG, 1, d), F32)),
        compiler_params=_params("arbitrary"),
    )(h, g, mod4, dnx, dres)


def _gate_bwd(dh, y, mod4, k_gate, coef, geo, name):
    t, d = dh.shape
    tile = geo.mm_tile

    def body(dh_ref, y_ref, gt_ref, dy_ref, dgt_ref):
        i = pl.program_id(0)
        dhv = dh_ref[...]
        dy_ref[...] = ((coef * gt_ref[...]) * dhv).astype(dy_ref.dtype)

        @pl.when(geo.seg_start(i, tile))
        def _():
            dgt_ref[...] = jnp.zeros_like(dgt_ref)

        dgt_ref[...] += coef * jnp.sum(dhv * y_ref[...].astype(F32), axis=0, keepdims=True)

    row = pl.BlockSpec((tile, d), lambda i: (i, 0))
    segv = pl.BlockSpec((None, 1, d), lambda i: (geo.seg(i, tile), 0, 0))
    return pl.pallas_call(
        body, name=name, grid=(t // tile,), in_specs=[row, row, _mod_spec(geo, tile, k_gate, d)],
        out_specs=(row, segv),
        out_shape=(jax.ShapeDtypeStruct((t, d), BF16), jax.ShapeDtypeStruct((N_SEG, 1, d), F32)),
        compiler_params=_params("arbitrary"),
    )(dh, y, mod4)


def _ff_tile(f):
    return _pick(f, (256, 128))


def _ffn_up(nx, w1t, w3t, s0, geo, name, hosted=None):
    t, d = nx.shape
    f = w1t.shape[1]
    tm, tn = geo.big_tile, _ff_tile(f)

    def body(x_ref, w1_ref, w3_ref, ga_ref, gb_ref, act_ref):
        xv = x_ref[...]
        a = lax.dot_general(xv, w1_ref[...], _NT, preferred_element_type=F32)
        bv = lax.dot_general(xv, w3_ref[...], _NT, preferred_element_type=F32)
        sg = jax.nn.sigmoid(a)
        silu = a * sg
        ga_ref[...] = (bv * (sg + silu * (1.0 - sg))).astype(ga_ref.dtype)
        gb_ref[...] = silu.astype(gb_ref.dtype)
        act_ref[...] = (silu * bv).astype(act_ref.dtype)

    w_spec = pl.BlockSpec((None, tn, d), lambda i, j: (s0, j, 0))
    o_spec = pl.BlockSpec((tm, tn), lambda i, j: (i, j))
    sds = jax.ShapeDtypeStruct((t, f), BF16)
    return _call(
        body, hosted, name=name, grid=(t // tm, f // tn),
        in_specs=[pl.BlockSpec((tm, d), lambda i, j: (i, 0)), w_spec, w_spec],
        out_specs=(o_spec,) * 3, out_shape=(sds,) * 3, compiler_params=_params("parallel", "parallel"),
    )(nx, w1t, w3t)


def _ffn_down(act, w2, s0, res, mod4, k_gate, geo, name, hosted=None):
    t, f = act.shape
    d = w2.shape[2]
    tm, tn = geo.mm_tile, _pick(d, (1024, 512, 256, 128))

    def body(a_ref, w_ref, res_ref, gate_ref, o_ref, y_ref):
        acc = lax.dot_general(a_ref[...], w_ref[...], _NN, preferred_element_type=F32)
        y_ref[...] = acc.astype(y_ref.dtype)
        o_ref[...] = res_ref[...] + (0.5 * gate_ref[...]) * acc

    o_spec = pl.BlockSpec((tm, tn), lambda i, j: (i, j))
    return _call(
        body, hosted, name=name, grid=(t // tm, d // tn),
        in_specs=[pl.BlockSpec((tm, f), lambda i, j: (i, 0)), pl.BlockSpec((None, f, tn), lambda i, j: (s0, 0, j)),
                  o_spec, pl.BlockSpec((None, None, 1, tn), lambda i, j: (geo.seg(i, tm), k_gate, 0, j))],
        out_specs=(o_spec, o_spec),
        out_shape=(jax.ShapeDtypeStruct((t, d), F32), jax.ShapeDtypeStruct((t, d), BF16)),
        compiler_params=_params("parallel", "parallel"),
    )(act, w2, res, mod4)


def _ffn_dact(dy, w2, ga, gb, s0, geo, name, hosted=None):
    t, d = dy.shape
    f = w2.shape[1]
    tm, tn = geo.big_tile, _ff_tile(f)

    def body(dy_ref, w_ref, ga_ref, gb_ref, da_ref, db_ref):
        dact = lax.dot_general(dy_ref[...], w_ref[...], _NT, preferred_element_type=F32)
        da_ref[...] = (dact * ga_ref[...].astype(F32)).astype(da_ref.dtype)
        db_ref[...] = (dact * gb_ref[...].astype(F32)).astype(db_ref.dtype)

    o_spec = pl.BlockSpec((tm, tn), lambda i, j: (i, j))
    sds = jax.ShapeDtypeStruct((t, f), BF16)
    return _call(
        body, hosted, name=name, grid=(t // tm, f // tn),
        in_specs=[pl.BlockSpec((tm, d), lambda i, j: (i, 0)), pl.BlockSpec((None, tn, d), lambda i, j: (s0, j, 0)),
                  o_spec, o_spec],
        out_specs=(o_spec, o_spec), out_shape=(sds, sds), compiler_params=_params("parallel", "parallel"),
    )(dy, w2, ga, gb)


def _ffn_dnx(da, db, w1t, w3t, s0, geo, name, hosted=None):
    t, f = da.shape
    d = w1t.shape[2]
    tm, tn = geo.mm_tile, _pick(d, (1024, 512, 256, 128))

    def body(da_ref, db_ref, w1_ref, w3_ref, o_ref):
        o_ref[...] = (lax.dot_general(da_ref[...], w1_ref[...], _NN, preferred_element_type=F32)
                      + lax.dot_general(db_ref[...], w3_ref[...], _NN, preferred_element_type=F32))

    x_spec = pl.BlockSpec((tm, f), lambda j, i: (i, 0))
    w_spec = pl.BlockSpec((None, f, tn), lambda j, i: (s0, 0, j))
    return _call(
        body, hosted, name=name, grid=(d // tn, t // tm), in_specs=[x_spec, x_spec, w_spec, w_spec],
        out_specs=pl.BlockSpec((tm, tn), lambda j, i: (i, j)), out_shape=jax.ShapeDtypeStruct((t, d), F32),
        compiler_params=_params("parallel", "parallel"),
    )(da, db, w1t, w3t)


def _with_host(fn, hosts, got, slot, *args, **kw):
    hosted = hosts.get(slot)
    if hosted is None:
        return fn(*args, **kw)
    out, got[slot] = fn(*args, hosted=hosted, **kw)
    return out


def _ffn_fwd(h, g, mod4, k0, w, s0, geo, tag, sub, hosts, got):
    nx = _pre_fwd(h, g, mod4, k0, geo, f"{tag}_pre")
    a, b, act = _with_host(_ffn_up, hosts, got, f"{sub}_up", nx, w["ffn_w1"], w["ffn_w3"], s0, geo, f"{tag}_up")
    h_out, y = _with_host(_ffn_down, hosts, got, f"{sub}_down", act, w["ffn_w2"], s0, h, mod4, k0 + 2, geo, f"{tag}_down")
    return h_out, (h, nx, a, b, act, y)


def _ffn_bwd(dh_out, saved, g, mod4, k0, w, s0, gbuf, geo, tag, sub, hosts, got):
    h, nx, a, b, act, y = saved
    dy, dgate = _gate_bwd(dh_out, y, mod4, k0 + 2, 0.5, geo, f"{tag}_dgate")
    da, db = _with_host(_ffn_dact, hosts, got, f"{sub}_dact", dy, w["ffn_w2"], a, b, s0, geo, f"{tag}_dact")
    gbuf["ffn_w2"] = _with_host(_tn_wide, hosts, got, f"{sub}_dw2", act, dy, f"{tag}_dw2", into=gbuf["ffn_w2"], s0=s0)
    dnx = _with_host(_ffn_dnx, hosts, got, f"{sub}_dnx", da, db, w["ffn_w1"], w["ffn_w3"], s0, geo, f"{tag}_dnx")
    gbuf["ffn_w1"] = _tn_wide(da, nx, f"{tag}_dw1", into=gbuf["ffn_w1"], s0=s0)
    gbuf["ffn_w3"] = _tn_wide(db, nx, f"{tag}_dw3", into=gbuf["ffn_w3"], s0=s0)
    dh, dg, dshift, dscale = _pre_bwd(h, g, mod4, k0, dnx, dh_out, geo, f"{tag}_dpre")
    return dh, dg, (dshift, dscale, dgate)


def _interleave(w, n_parts, tile):
    lead, cols = w.shape[:-1], w.shape[-1] // n_parts
    return w.reshape(*lead, n_parts, cols // tile, tile).swapaxes(-3, -2).reshape(*lead, n_parts * cols)


def _deinterleave(w, n_parts, tile):
    lead, cols = w.shape[:-1], w.shape[-1] // n_parts
    return w.reshape(*lead, cols // tile, n_parts, tile).swapaxes(-3, -2).reshape(*lead, n_parts * cols)


HALO = 16


def _conv_tile(c):
    return _pick(c, (256, 128))


def _conv_specs(geo, tc, t):
    tile = geo.tile
    per = tile // HALO
    last = t // HALO - 1
    cur = pl.BlockSpec((tile, 3 * tc), lambda j, i: (i, j))
    prev = pl.BlockSpec((HALO, 3 * tc), lambda j, i: (jnp.maximum(i * per - 1, 0), j))
    nxt = pl.BlockSpec((HALO, 3 * tc), lambda j, i: (jnp.minimum((i + 1) * per, last), j))
    return cur, prev, nxt


def _conv_edges(geo, i):
    tile = geo.tile
    row = i * tile
    lat = row < 2 * geo.n_lat
    first = jnp.where(lat, row % geo.n_lat == 0, (row - 2 * geo.n_lat) % geo.n_ctx == 0)
    end = row + tile
    last = jnp.where(lat, end % geo.n_lat == 0, (end - 2 * geo.n_lat) % geo.n_ctx == 0)
    return first, last


def _shift_rows(v, before, after):
    n = v.shape[0]
    rows = lax.broadcasted_iota(jnp.int32, v.shape, 0)
    down = jnp.where(rows == 0, before, pltpu.roll(v, 1, 0))
    up = jnp.where(rows == n - 1, after, pltpu.roll(v, n - 1, 0))
    return down, up


def _conv_fwd(proj, conv_w, geo, name, hosted=None):
    t, c3 = proj.shape
    c = c3 // 3
    tc, tile = _conv_tile(c), geo.tile

    def body(cur_ref, prev_ref, next_ref, w_ref, o_ref):
        first, last = _conv_edges(geo, pl.program_id(1))
        bv = cur_ref[:, :tc].astype(F32)
        p = cur_ref[:, tc:2 * tc].astype(F32) * cur_ref[:, 2 * tc:].astype(F32)
        p_before = prev_ref[HALO - 1:HALO, tc:2 * tc].astype(F32) * prev_ref[HALO - 1:HALO, 2 * tc:].astype(F32)
        p_after = next_ref[0:1, tc:2 * tc].astype(F32) * next_ref[0:1, 2 * tc:].astype(F32)
        p_before = jnp.where(first, 0.0, p_before)
        p_after = jnp.where(last, 0.0, p_after)
        pm1, pp1 = _shift_rows(p, p_before, p_after)
        w = w_ref[...]
        q = w[0:1] * pm1 + w[1:2] * p + w[2:3] * pp1
        o_ref[...] = (bv * q).astype(o_ref.dtype)

    cur, prev, nxt = _conv_specs(geo, tc, t)
    return _call(
        body, hosted, name=name, grid=(c // tc, t // tile),
        in_specs=[cur, prev, nxt, pl.BlockSpec((3, tc), lambda j, i: (0, j))],
        out_specs=pl.BlockSpec((tile, tc), lambda j, i: (i, j)), out_shape=jax.ShapeDtypeStruct((t, c), BF16),
        compiler_params=_params("parallel", "parallel"),
    )(proj, proj, proj, conv_w)


def _conv_bwd(proj, dyc, conv_w, geo, name, hosted=None):
    t, c3 = proj.shape
    c = c3 // 3
    tc, tile = _conv_tile(c), geo.tile

    def body(cur_ref, prev_ref, next_ref, d_ref, dprev_ref, dnext_ref, w_ref, o_ref, dw_ref):
        i = pl.program_id(1)
        first, last = _conv_edges(geo, i)
        bv = cur_ref[:, :tc].astype(F32)
        cv = cur_ref[:, tc:2 * tc].astype(F32)
        uv = cur_ref[:, 2 * tc:].astype(F32)
        p = cv * uv
        p_before = prev_ref[HALO - 1:HALO, tc:2 * tc].astype(F32) * prev_ref[HALO - 1:HALO, 2 * tc:].astype(F32)
        p_after = next_ref[0:1, tc:2 * tc].astype(F32) * next_ref[0:1, 2 * tc:].astype(F32)
        p_before = jnp.where(first, 0.0, p_before)
        p_after = jnp.where(last, 0.0, p_after)
        pm1, pp1 = _shift_rows(p, p_before, p_after)
        w = w_ref[...]
        q = w[0:1] * pm1 + w[1:2] * p + w[2:3] * pp1
        dy = d_ref[...].astype(F32)
        dq = dy * bv
        dq_before = dprev_ref[HALO - 1:HALO, :].astype(F32) * prev_ref[HALO - 1:HALO, :tc].astype(F32)
        dq_after = dnext_ref[0:1, :].astype(F32) * next_ref[0:1, :tc].astype(F32)
        dq_before = jnp.where(first, 0.0, dq_before)
        dq_after = jnp.where(last, 0.0, dq_after)
        dqm1, dqp1 = _shift_rows(dq, dq_before, dq_after)
        dp = w[0:1] * dqp1 + w[1:2] * dq + w[2:3] * dqm1
        o_ref[:, :tc] = (dy * q).astype(o_ref.dtype)
        o_ref[:, tc:2 * tc] = (dp * uv).astype(o_ref.dtype)
        o_ref[:, 2 * tc:] = (dp * cv).astype(o_ref.dtype)

        @pl.when(i == 0)
        def _():
            dw_ref[...] = jnp.zeros_like(dw_ref)

        dw_ref[0:1, :] += jnp.sum(dq * pm1, axis=0, keepdims=True)
        dw_ref[1:2, :] += jnp.sum(dq * p, axis=0, keepdims=True)
        dw_ref[2:3, :] += jnp.sum(dq * pp1, axis=0, keepdims=True)

    cur, prev, nxt = _conv_specs(geo, tc, t)
    per, lastb = tile // HALO, t // HALO - 1
    dcur = pl.BlockSpec((tile, tc), lambda j, i: (i, j))
    dprev = pl.BlockSpec((HALO, tc), lambda j, i: (jnp.maximum(i * per - 1, 0), j))
    dnext = pl.BlockSpec((HALO, tc), lambda j, i: (jnp.minimum((i + 1) * per, lastb), j))
    wspec = pl.BlockSpec((3, tc), lambda j, i: (0, j))
    return _call(
        body, hosted, name=name, grid=(c // tc, t // tile), in_specs=[cur, prev, nxt, dcur, dprev, dnext, wspec],
        out_specs=(cur, wspec), out_shape=(jax.ShapeDtypeStruct((t, c3), BF16), jax.ShapeDtypeStruct((3, c), F32)),
        compiler_params=_params("parallel", "arbitrary"),
    )(proj, proj, proj, dyc, dyc, dyc, conv_w)


def _sconv_fwd(h, g, mod4, w_in, conv_w, w_out, geo, tag, hosts, got):
    nx = _pre_fwd(h, g, mod4, 3, geo, f"{tag}_pre")
    proj = _with_host(_mm, hosts, got, "mix_a", nx, w_in, out_dtype=BF16, name=f"{tag}_in")
    yc = _with_host(_conv_fwd, hosts, got, "mix_b", proj, conv_w, geo, f"{tag}_conv")
    h_out, y = _mm(yc, w_out, name=f"{tag}_out", gate=(h, mod4, 5, 1.0, geo))
    return h_out, (h, nx, proj, yc, y)


def _sconv_bwd(dh_out, saved, g, mod4, w_in, conv_w, w_out, geo, tag, hosts, got):
    h, nx, proj, yc, y = saved
    dy, dgate = _gate_bwd(dh_out, y, mod4, 5, 1.0, geo, f"{tag}_dgate")
    dyc = _with_host(_mm, hosts, got, "mix_d", dy, w_out, tb=True, out_dtype=BF16, name=f"{tag}_dyc")
    dw_out = _tn_wide(yc, dy, f"{tag}_dwout")
    dproj, dconv = _with_host(_conv_bwd, hosts, got, "mix_c", proj, dyc, conv_w, geo, f"{tag}_dconv")
    dnx = _with_host(_mm, hosts, got, "mix_b", dproj, w_in, tb=True, name=f"{tag}_dnx")
    dw_in = _with_host(_tn_wide, hosts, got, "mix_a", nx, dproj, f"{tag}_dwin")
    dh, dg, dshift, dscale = _pre_bwd(h, g, mod4, 3, dnx, dh_out, geo, f"{tag}_dpre")
    return dh, dg, (dshift, dscale, dgate), dw_in, dconv, dw_out


def _rope_swap(v):
    nf = QK_ROPE // 4
    return v.reshape(v.shape[:-1] + (2, 2, nf)).swapaxes(-3, -2).reshape(v.shape)


def _rope_tables(geo):
    n = geo.n_lat
    nf = QK_ROPE // 4
    pos = np.arange(n)
    inv = ROPE_BASE ** (-np.arange(nf, dtype=np.float32) / nf)
    ang = np.concatenate([(pos // GRID_W)[:, None] * inv, (pos % GRID_W)[:, None] * inv], axis=1).astype(np.float32)
    cos, sin = np.cos(ang), np.sin(ang)
    zeros = np.zeros((n, LANE - QK_ROPE), np.float32)
    c_lat = np.concatenate([cos, cos, zeros], axis=1)
    s_lat = np.concatenate([-sin, sin, zeros], axis=1)
    c_ctx = np.concatenate([np.ones((2 * geo.n_ctx, QK_ROPE), np.float32), np.zeros((2 * geo.n_ctx, LANE - QK_ROPE), np.float32)], 1)
    s_ctx = np.zeros((2 * geo.n_ctx, LANE), np.float32)
    return (jnp.asarray(np.concatenate([c_lat, c_lat, c_ctx], 0)), jnp.asarray(np.concatenate([s_lat, s_lat, s_ctx], 0)))


def _swap_halves(v):
    lanes = lax.broadcasted_iota(jnp.int32, v.shape, 1)
    return jnp.where(lanes < QK_ROPE // 2, pltpu.roll(v, LANE - QK_ROPE // 2, 1), pltpu.roll(v, QK_ROPE // 2, 1))


def _latent_norm_fwd(down, g_qa, g_kva, geo, name):
    t, wd = down.shape
    ql, kl = g_qa.shape[1], g_kva.shape[1]
    tile = geo.mm_tile

    def body(d_ref, gq_ref, gk_ref, cq_ref, ckv_ref):
        for lo, n, g_ref, o_ref in ((0, ql, gq_ref, cq_ref), (ql, kl, gk_ref, ckv_ref)):
            x = d_ref[:, lo:lo + n]
            r = lax.rsqrt(jnp.mean(x * x, axis=-1, keepdims=True) + EPS)
            o_ref[...] = (x * r * g_ref[...]).astype(o_ref.dtype)

    return pl.pallas_call(
        body, name=name, grid=(t // tile,),
        in_specs=[pl.BlockSpec((tile, wd), lambda i: (i, 0)), pl.BlockSpec((1, ql), lambda i: (0, 0)),
                  pl.BlockSpec((1, kl), lambda i: (0, 0))],
        out_specs=(pl.BlockSpec((tile, ql), lambda i: (i, 0)), pl.BlockSpec((tile, kl), lambda i: (i, 0))),
        out_shape=(jax.ShapeDtypeStruct((t, ql), BF16), jax.ShapeDtypeStruct((t, kl), BF16)),
        compiler_params=_params("parallel"),
    )(down, g_qa, g_kva)


def _latent_norm_bwd(down, g_qa, g_kva, dcqn, dckvn, dkr, geo, name):
    t, wd = down.shape
    ql, kl = g_qa.shape[1], g_kva.shape[1]
    tile = geo.mm_tile

    def body(d_ref, gq_ref, gk_ref, dq_ref, dk_ref, dkr_ref, o_ref, dgq_ref, dgk_ref):
        i = pl.program_id(0)

        @pl.when(i == 0)
        def _():
            dgq_ref[...] = jnp.zeros_like(dgq_ref)
            dgk_ref[...] = jnp.zeros_like(dgk_ref)

        for lo, n, g_ref, dy_ref, dg_ref in ((0, ql, gq_ref, dq_ref, dgq_ref), (ql, kl, gk_ref, dk_ref, dgk_ref)):
            x = d_ref[:, lo:lo + n]
            dy = dy_ref[...].astype(F32)
            r = lax.rsqrt(jnp.mean(x * x, axis=-1, keepdims=True) + EPS)
            xhat = x * r
            u = dy * g_ref[...]
            o_ref[:, lo:lo + n] = (r * (u - xhat * jnp.mean(u * xhat, axis=-1, keepdims=True))).astype(o_ref.dtype)
            dg_ref[...] += jnp.sum(dy * xhat, axis=0, keepdims=True)
        o_ref[:, ql + kl:] = dkr_ref[...].astype(o_ref.dtype)

    def row(n):
        return pl.BlockSpec((tile, n), lambda i: (i, 0))

    def vec(n):
        return pl.BlockSpec((1, n), lambda i: (0, 0))

    return pl.pallas_call(
        body, name=name, grid=(t // tile,),
        in_specs=[row(wd), vec(ql), vec(kl), row(ql), row(kl), row(wd - ql - kl)],
        out_specs=(row(wd), vec(ql), vec(kl)),
        out_shape=(jax.ShapeDtypeStruct((t, wd), BF16), jax.ShapeDtypeStruct((1, ql), F32),
                   jax.ShapeDtypeStruct((1, kl), F32)),
        compiler_params=_params("arbitrary"),
    )(down, g_qa, g_kva, dcqn, dckvn, dkr)


def _qk_specs(geo, xr_col, shared_rope):
    tile = geo.mm_tile
    xn_spec = pl.BlockSpec((tile, HEADS * LANE), lambda i: (i, 0))
    if shared_rope:
        xr_spec = pl.BlockSpec((tile, LANE), lambda i: (i, xr_col))
    else:
        xr_spec = pl.BlockSpec((tile, HEADS * LANE), lambda i: (i, xr_col // HEADS))
    vec = pl.BlockSpec((1, LANE), lambda i: (0, 0))
    tab = pl.BlockSpec((tile, LANE), lambda i: (i, 0))
    return tile, xn_spec, xr_spec, vec, tab


def _qk_norm(xn, xr):
    ss = jnp.sum(xn * xn, axis=-1, keepdims=True) + jnp.sum(xr * xr, axis=-1, keepdims=True)
    return lax.rsqrt(ss * (1.0 / QK_HEAD) + EPS)


def _head_lanes(ref, hh, shared=False):
    return ref[...] if shared else ref[:, hh * LANE:(hh + 1) * LANE]


def _qk_fwd(xn_arr, xr_arr, xr_col, shared_rope, gn, gr, cos, sin, geo, name):
    t = xn_arr.shape[0]
    tile, xn_spec, xr_spec, vec, tab = _qk_specs(geo, xr_col, shared_rope)

    def body(xn_ref, xr_ref, gn_ref, gr_ref, c_ref, s_ref, o_ref):
        cv, sv, gnv, grv = c_ref[...], s_ref[...], gn_ref[...], gr_ref[...]
        for hh in range(HEADS):
            xn = _head_lanes(xn_ref, hh).astype(F32)
            xr = _head_lanes(xr_ref, hh, shared_rope).astype(F32)
            r = _qk_norm(xn, xr)
            yr = xr * r * grv
            o_ref[:, hh * HEAD_PAD:hh * HEAD_PAD + LANE] = (xn * r * gnv).astype(o_ref.dtype)
            o_ref[:, hh * HEAD_PAD + LANE:(hh + 1) * HEAD_PAD] = (yr * cv + _swap_halves(yr) * sv).astype(o_ref.dtype)

    return pl.pallas_call(
        body, name=name, grid=(t // tile,), in_specs=[xn_spec, xr_spec, vec, vec, tab, tab],
        out_specs=pl.BlockSpec((tile, HEADS * HEAD_PAD), lambda i: (i, 0)),
        out_shape=jax.ShapeDtypeStruct((t, HEADS * HEAD_PAD), BF16), compiler_params=_params("parallel"),
    )(xn_arr, xr_arr, gn, gr, cos, sin)


def _qk_bwd(xn_arr, xr_arr, xr_col, shared_rope, gn, gr, cos, sin, dout, geo, name):
    t = xn_arr.shape[0]
    tile, xn_spec, xr_spec, vec, tab = _qk_specs(geo, xr_col, shared_rope)
    half = HEADS * LANE
    if shared_rope:
        n_lat_tiles = dout[0].shape[0] // tile
        assert dout[0].shape[0] % tile == 0 and dout[1].shape[0] % tile == 0

    def body(*refs):
        if shared_rope:
            xn_ref, xr_ref, gn_ref, gr_ref, c_ref, s_ref, dl_ref, dc_ref, vl_ref, vc_ref, raw_ref, dxr_ref, dgn_ref, dgr_ref = refs
        else:
            xn_ref, xr_ref, gn_ref, gr_ref, c_ref, s_ref, d_ref, raw_ref, dgn_ref, dgr_ref = refs
        i = pl.program_id(0)
        cv, sv, gnv, grv = c_ref[...], s_ref[...], gn_ref[...], gr_ref[...]
        dgn = jnp.zeros((1, LANE), F32)
        dgr = jnp.zeros((1, LANE), F32)
        dxr_sum = jnp.zeros((tile, LANE), F32)
        if shared_rope:
            latent = i < n_lat_tiles
            raw_ref[:, half:] = jnp.where(latent, vl_ref[...], vc_ref[...])
        for hh in range(HEADS):
            xn = _head_lanes(xn_ref, hh).astype(F32)
            xr = _head_lanes(xr_ref, hh, shared_rope).astype(F32)
            r = _qk_norm(xn, xr)
            xhn, xhr = xn * r, xr * r
            lo = hh * HEAD_PAD
            if shared_rope:
                dhead = jnp.where(latent, dl_ref[:, lo:lo + HEAD_PAD], dc_ref[:, lo:lo + HEAD_PAD]).astype(F32)
            else:
                dhead = d_ref[:, lo:lo + HEAD_PAD].astype(F32)
            dyn, dro = dhead[:, :LANE], dhead[:, LANE:]
            dyr = dro * cv + _swap_halves(dro * sv)
            un, ur = dyn * gnv, dyr * grv
            mean = (jnp.sum(un * xhn, axis=-1, keepdims=True) + jnp.sum(ur * xhr, axis=-1, keepdims=True)) * (1.0 / QK_HEAD)
            raw_ref[:, hh * LANE:(hh + 1) * LANE] = (r * (un - xhn * mean)).astype(raw_ref.dtype)
            dxr = r * (ur - xhr * mean)
            if shared_rope:
                dxr_sum = dxr_sum + dxr
            else:
                raw_ref[:, half + hh * LANE:half + (hh + 1) * LANE] = dxr.astype(raw_ref.dtype)
            dgn = dgn + jnp.sum(dyn * xhn, axis=0, keepdims=True)
            dgr = dgr + jnp.sum(dyr * xhr, axis=0, keepdims=True)
        if shared_rope:
            dxr_ref[...] = dxr_sum

        @pl.when(i == 0)
        def _():
            dgn_ref[...] = jnp.zeros_like(dgn_ref)
            dgr_ref[...] = jnp.zeros_like(dgr_ref)

        dgn_ref[...] += dgn
        dgr_ref[...] += dgr

    raw_spec = pl.BlockSpec((tile, 2 * half), lambda i: (i, 0))
    raw_shape = jax.ShapeDtypeStruct((t, 2 * half), BF16)
    vec_shape = jax.ShapeDtypeStruct((1, LANE), F32)
    in_specs = [xn_spec, xr_spec, vec, vec, tab, tab]
    if shared_rope:
        def two(width):
            return [pl.BlockSpec((tile, width), lambda i: (jnp.minimum(i, n_lat_tiles - 1), 0)),
                    pl.BlockSpec((tile, width), lambda i: (jnp.maximum(i - n_lat_tiles, 0), 0))]

        return pl.pallas_call(
            body, name=name, grid=(t // tile,), in_specs=in_specs + two(HEADS * HEAD_PAD) + two(half),
            out_specs=(raw_spec, pl.BlockSpec((tile, LANE), lambda i: (i, 0)), vec, vec),
            out_shape=(raw_shape, jax.ShapeDtypeStruct((t, LANE), F32), vec_shape, vec_shape),
            compiler_params=_params("arbitrary"),
        )(xn_arr, xr_arr, gn, gr, cos, sin, *dout)
    return pl.pallas_call(
        body, name=name, grid=(t // tile,),
        in_specs=in_specs + [pl.BlockSpec((tile, HEADS * HEAD_PAD), lambda i: (i, 0))],
        out_specs=(raw_spec, vec, vec), out_shape=(raw_shape, vec_shape, vec_shape),
        compiler_params=_params("arbitrary"),
    )(xn_arr, xr_arr, gn, gr, cos, sin, dout)


def _attn_specs(geo):
    tq, nq = geo.n_ctx, geo.n_lat // geo.n_ctx

    def qrow(b, i):
        return jnp.where(i < nq, b * nq + i, 2 * nq + b)

    q_spec = pl.BlockSpec((tq, HEAD_PAD), lambda b, hh, i: (qrow(b, i), hh))
    kc_spec = pl.BlockSpec((geo.n_ctx, HEAD_PAD), lambda b, hh, i: (2 * nq + b, hh))
    kl_spec = pl.BlockSpec((geo.n_lat, HEAD_PAD), lambda b, hh, i: (b, hh))
    vc_spec = pl.BlockSpec((geo.n_ctx, V_HEAD), lambda b, hh, i: (2 * nq + b, HEADS + hh))
    vl_spec = pl.BlockSpec((geo.n_lat, V_HEAD), lambda b, hh, i: (b, HEADS + hh))
    o_spec = pl.BlockSpec((tq, V_HEAD), lambda b, hh, i: (qrow(b, i), hh))
    return tq, nq, q_spec, kc_spec, kl_spec, vc_spec, vl_spec, o_spec


def _attn_fwd(q, k, kv, with_ctx_q, geo, name, hosted=None):
    t = q.shape[0]
    tq, nq, q_spec, kc_spec, kl_spec, vc_spec, vl_spec, o_spec = _attn_specs(geo)

    def body(q_ref, kc_ref, kl_ref, vc_ref, vl_ref, o_ref):
        i = pl.program_id(2)
        qv = q_ref[...]
        s_c = _dot(qv, kc_ref[...], _NT) * QK_SCALE

        @pl.when(i < nq)
        def _():
            s_l = _dot(qv, kl_ref[...], _NT) * QK_SCALE
            m = jnp.maximum(jnp.max(s_c, axis=-1, keepdims=True), jnp.max(s_l, axis=-1, keepdims=True))
            p_c, p_l = jnp.exp(s_c - m), jnp.exp(s_l - m)
            den = jnp.sum(p_c, axis=-1, keepdims=True) + jnp.sum(p_l, axis=-1, keepdims=True)
            o = _dot(p_c, vc_ref[...], _NN) + _dot(p_l, vl_ref[...], _NN)
            o_ref[...] = (o / den).astype(o_ref.dtype)

        @pl.when(i == nq)
        def _():
            if with_ctx_q:
                m = jnp.max(s_c, axis=-1, keepdims=True)
                p_c = jnp.exp(s_c - m)
                o = _dot(p_c, vc_ref[...], _NN) / jnp.sum(p_c, axis=-1, keepdims=True)
                o_ref[...] = o.astype(o_ref.dtype)
            else:
                o_ref[...] = jnp.zeros_like(o_ref)

    return _call(
        body, hosted, name=name, grid=(2, HEADS, nq + 1), in_specs=[q_spec, kc_spec, kl_spec, vc_spec, vl_spec],
        out_specs=o_spec, out_shape=jax.ShapeDtypeStruct((t, HEADS * V_HEAD), BF16),
        compiler_params=_params("parallel", "parallel", "arbitrary"),
    )(q, k, k, kv, kv)


def _attn_bwd(q, k, kv, do, with_ctx_q, geo, name, hosted=None):
    t = q.shape[0]
    tq, nq, q_spec, kc_spec, kl_spec, vc_spec, vl_spec, o_spec = _attn_specs(geo)

    def body(q_ref, kc_ref, kl_ref, vc_ref, vl_ref, do_ref, dq_ref, dkl_ref, dkc_ref, dvl_ref, dvc_ref,
             akl_ref, akc_ref, avl_ref, avc_ref):
        i = pl.program_id(2)

        @pl.when(i == 0)
        def _():
            for ref in (akl_ref, akc_ref, avl_ref, avc_ref):
                ref[...] = jnp.zeros_like(ref)

        qv, dov = q_ref[...], do_ref[...]
        s_c = _dot(qv, kc_ref[...], _NT) * QK_SCALE
        dp_c = _dot(dov, vc_ref[...], _NT)

        def ctx_part(p_c, delta):
            ds_c = (p_c * (dp_c - delta) * QK_SCALE).astype(BF16)
            akc_ref[...] += _dot(ds_c, qv, _TN)
            avc_ref[...] += _dot(p_c, dov, _TN)
            return _dot(ds_c, kc_ref[...], _NN)

        @pl.when(i < nq)
        def _():
            s_l = _dot(qv, kl_ref[...], _NT) * QK_SCALE
            m = jnp.maximum(jnp.max(s_c, axis=-1, keepdims=True), jnp.max(s_l, axis=-1, keepdims=True))
            p_c, p_l = jnp.exp(s_c - m), jnp.exp(s_l - m)
            inv = 1.0 / (jnp.sum(p_c, axis=-1, keepdims=True) + jnp.sum(p_l, axis=-1, keepdims=True))
            p_c, p_l = p_c * inv, p_l * inv
            dp_l = _dot(dov, vl_ref[...], _NT)
            delta = jnp.sum(p_c * dp_c, axis=-1, keepdims=True) + jnp.sum(p_l * dp_l, axis=-1, keepdims=True)
            ds_l = (p_l * (dp_l - delta) * QK_SCALE).astype(BF16)
            akl_ref[...] += _dot(ds_l, qv, _TN)
            avl_ref[...] += _dot(p_l, dov, _TN)
            dq_ref[...] = (ctx_part(p_c, delta) + _dot(ds_l, kl_ref[...], _NN)).astype(dq_ref.dtype)

        @pl.when(i == nq)
        def _():
            if with_ctx_q:
                m = jnp.max(s_c, axis=-1, keepdims=True)
                p_c = jnp.exp(s_c - m)
                p_c = p_c * (1.0 / jnp.sum(p_c, axis=-1, keepdims=True))
                delta = jnp.sum(p_c * dp_c, axis=-1, keepdims=True)
                dq_ref[...] = ctx_part(p_c, delta).astype(dq_ref.dtype)
            else:
                dq_ref[...] = jnp.zeros_like(dq_ref)
            dkl_ref[...] = akl_ref[...].astype(dkl_ref.dtype)
            dkc_ref[...] = akc_ref[...].astype(dkc_ref.dtype)
            dvl_ref[...] = avl_ref[...].astype(dvl_ref.dtype)
            dvc_ref[...] = avc_ref[...].astype(dvc_ref.dtype)

    def acc_spec(rows, width):
        return pl.BlockSpec((rows, width), lambda b, hh, i: (b, hh))

    return _call(
        body, hosted, name=name, grid=(2, HEADS, nq + 1), in_specs=[q_spec, kc_spec, kl_spec, vc_spec, vl_spec, o_spec],
        out_specs=(q_spec, acc_spec(geo.n_lat, HEAD_PAD), acc_spec(geo.n_ctx, HEAD_PAD), acc_spec(geo.n_lat, V_HEAD),
                   acc_spec(geo.n_ctx, V_HEAD)),
        out_shape=(jax.ShapeDtypeStruct((t, HEADS * HEAD_PAD), BF16),
                   jax.ShapeDtypeStruct((2 * geo.n_lat, HEADS * HEAD_PAD), BF16),
                   jax.ShapeDtypeStruct((2 * geo.n_ctx, HEADS * HEAD_PAD), BF16),
                   jax.ShapeDtypeStruct((2 * geo.n_lat, HEADS * V_HEAD), BF16),
                   jax.ShapeDtypeStruct((2 * geo.n_ctx, HEADS * V_HEAD), BF16)),
        scratch_shapes=[pltpu.VMEM((geo.n_lat, HEAD_PAD), F32), pltpu.VMEM((geo.n_ctx, HEAD_PAD), F32),
                        pltpu.VMEM((geo.n_lat, V_HEAD), F32), pltpu.VMEM((geo.n_ctx, V_HEAD), F32)],
        compiler_params=_params("parallel", "parallel", "arbitrary"),
    )(q, k, k, kv, kv, do)


def _mla_fwd(h, g, mod4, w, with_ctx_q, tabs, geo, tag, hosts, got):
    cos, sin = tabs
    ql, kl = w["g_qa"].shape[1], w["g_kva"].shape[1]
    kr_col = (ql + kl) // LANE
    nx = _pre_fwd(h, g, mod4, 3, geo, f"{tag}_pre")
    down = _mm(nx, w["w_a"], name=f"{tag}_down")
    cqn, ckvn = _latent_norm_fwd(down, w["g_qa"], w["g_kva"], geo, f"{tag}_lnorm")
    qraw = _mm(cqn, w["w_uq"], out_dtype=BF16, name=f"{tag}_uq")
    kvraw = _mm(ckvn, w["w_ukv"], out_dtype=BF16, name=f"{tag}_ukv")
    q = _qk_fwd(qraw, qraw, HEADS, False, w["gq_n"], w["gq_r"], cos, sin, geo, f"{tag}_qnorm")
    k = _qk_fwd(kvraw, down, kr_col, True, w["gk_n"], w["gk_r"], cos, sin, geo, f"{tag}_knorm")
    o = _with_host(_attn_fwd, hosts, got, "mix_a", q, k, kvraw, with_ctx_q, geo, f"{tag}_attn")
    h_out, y = _mm(o, w["w_o"], name=f"{tag}_o", gate=(h, mod4, 5, 1.0, geo))
    return h_out, (h, nx, down, cqn, ckvn, qraw, kvraw, q, k, o, y)


def _mla_bwd(dh_out, saved, g, mod4, w, with_ctx_q, tabs, geo, tag, hosts, got):
    cos, sin = tabs
    h, nx, down, cqn, ckvn, qraw, kvraw, q, k, o, y = saved
    ql, kl = w["g_qa"].shape[1], w["g_kva"].shape[1]
    kr_col = (ql + kl) // LANE
    dy, dgate = _gate_bwd(dh_out, y, mod4, 5, 1.0, geo, f"{tag}_dgate")
    do = _mm(dy, w["w_o"], tb=True, out_dtype=BF16, name=f"{tag}_do")
    dw_o = _tn_wide(o, dy, f"{tag}_dwo")
    dq, dk_lat, dk_ctx, dv_lat, dv_ctx = _with_host(_attn_bwd, hosts, got, "mix_a", q, k, kvraw, do, with_ctx_q, geo,
                                                    f"{tag}_dattn")
    dqraw, dgq_n, dgq_r = _qk_bwd(qraw, qraw, HEADS, False, w["gq_n"], w["gq_r"], cos, sin, dq, geo, f"{tag}_dqnorm")
    dkvraw, dkr, dgk_n, dgk_r = _qk_bwd(kvraw, down, kr_col, True, w["gk_n"], w["gk_r"], cos, sin,
                                        (dk_lat, dk_ctx, dv_lat, dv_ctx), geo, f"{tag}_dknorm")
    dcqn = _mm(dqraw, w["w_uq"], tb=True, out_dtype=BF16, name=f"{tag}_dcqn")
    dw_uq = _tn_wide(cqn, dqraw, f"{tag}_dwuq")
    dckvn = _mm(dkvraw, w["w_ukv"], tb=True, out_dtype=BF16, name=f"{tag}_dckvn")
    dw_ukv = _tn_wide(ckvn, dkvraw, f"{tag}_dwukv")
    ddown, dg_qa, dg_kva = _latent_norm_bwd(down, w["g_qa"], w["g_kva"], dcqn, dckvn, dkr, geo, f"{tag}_dlnorm")
    dnx = _mm(ddown, w["w_a"], tb=True, name=f"{tag}_dnx")
    dw_a = _tn_wide(nx, ddown, f"{tag}_dwa")
    dh, dg, dshift, dscale = _pre_bwd(h, g, mod4, 3, dnx, dh_out, geo, f"{tag}_dpre")
    grads = dict(w_a=dw_a, g_qa=dg_qa, w_uq=dw_uq, g_kva=dg_kva, w_ukv=dw_ukv, gq_n=dgq_n, gq_r=dgq_r, gk_n=dgk_n,
                 gk_r=dgk_r, w_o=dw_o)
    return dh, dg, (dshift, dscale, dgate), grads


def _mla_prepare(w_a, g_qa, w_uq, g_kva, w_ukv, g_q, g_k, w_o):
    ql, kl = g_qa.shape[0], g_kva.shape[0]
    d = w_a.shape[0]
    w_a_pad = jnp.concatenate([w_a[:, :ql + kl], _rope_swap(w_a[:, ql + kl:]), jnp.zeros((d, LANE - QK_ROPE), w_a.dtype)], axis=1)
    uq = w_uq.reshape(ql, HEADS, QK_HEAD)
    uq_r = jnp.pad(_rope_swap(uq[:, :, QK_NOPE:]), ((0, 0), (0, 0), (0, LANE - QK_ROPE)))
    w_uq_pad = jnp.concatenate([uq[:, :, :QK_NOPE].reshape(ql, HEADS * LANE), uq_r.reshape(ql, HEADS * LANE)], axis=1)
    ukv = w_ukv.reshape(kl, HEADS, QK_NOPE + V_HEAD)
    w_ukv_p = jnp.concatenate([ukv[:, :, :QK_NOPE].reshape(kl, HEADS * LANE), ukv[:, :, QK_NOPE:].reshape(kl, HEADS * V_HEAD)], axis=1)

    def gains(gv):
        gv = gv.astype(F32)
        return gv[None, :QK_NOPE], jnp.pad(_rope_swap(gv[QK_NOPE:]), (0, LANE - QK_ROPE))[None]

    gq_n, gq_r = gains(g_q)
    gk_n, gk_r = gains(g_k)
    return dict(w_a=w_a_pad, g_qa=g_qa.astype(F32)[None], w_uq=w_uq_pad, g_kva=g_kva.astype(F32)[None], w_ukv=w_ukv_p,
                gq_n=gq_n, gq_r=gq_r, gk_n=gk_n, gk_r=gk_r, w_o=w_o)


def _mla_unprepare(gr):
    ql, kl = gr["g_qa"].shape[1], gr["g_kva"].shape[1]
    dw_a = jnp.concatenate([gr["w_a"][:, :ql + kl], _rope_swap(gr["w_a"][:, ql + kl:ql + kl + QK_ROPE])], axis=1)
    uqn = gr["w_uq"][:, :HEADS * LANE].reshape(ql, HEADS, LANE)
    uqr = _rope_swap(gr["w_uq"][:, HEADS * LANE:].reshape(ql, HEADS, LANE)[:, :, :QK_ROPE])
    dw_uq = jnp.concatenate([uqn, uqr], axis=2).reshape(ql, HEADS * QK_HEAD)
    ukn = gr["w_ukv"][:, :HEADS * LANE].reshape(kl, HEADS, LANE)
    ukv = gr["w_ukv"][:, HEADS * LANE:].reshape(kl, HEADS, V_HEAD)
    dw_ukv = jnp.concatenate([ukn, ukv], axis=2).reshape(kl, HEADS * (QK_NOPE + V_HEAD))

    def gains(gn, grr):
        return jnp.concatenate([gn[0], _rope_swap(grr[0, :QK_ROPE])])

    return dict(mla_w_a=dw_a, mla_g_qa=gr["g_qa"][0], mla_w_uq=dw_uq, mla_g_kva=gr["g_kva"][0], mla_w_ukv=dw_ukv,
                mla_g_q=gains(gr["gq_n"], gr["gq_r"]), mla_g_k=gains(gr["gk_n"], gr["gk_r"]), mla_w_o=gr["w_o"])


def _loss_head(h, target, geo, name):
    t, d = h.shape
    tile = geo.mm_tile
    n_lat_tiles = 2 * geo.n_lat // tile

    def body(h_ref, t_ref, dh_ref, loss_ref):
        i = pl.program_id(0)

        @pl.when(i == 0)
        def _():
            loss_ref[...] = jnp.zeros_like(loss_ref)

        @pl.when(i < n_lat_tiles)
        def _():
            e = h_ref[...] - t_ref[...]
            dh_ref[...] = e * (1.0 / d)
            part = jnp.sum(e * e, axis=0, keepdims=True) * (0.5 / d)
            loss_ref[...] += sum(part[:, j * LANE:(j + 1) * LANE] for j in range(d // LANE))

        @pl.when(i >= n_lat_tiles)
        def _():
            dh_ref[...] = jnp.zeros_like(dh_ref)

    row = pl.BlockSpec((tile, d), lambda i: (i, 0))
    tgt = pl.BlockSpec((tile, d), lambda i: (jnp.minimum(i, n_lat_tiles - 1), 0))
    dh, loss = pl.pallas_call(
        body, name=name, grid=(t // tile,), in_specs=[row, tgt],
        out_specs=(row, pl.BlockSpec((1, LANE), lambda i: (0, 0))),
        out_shape=(jax.ShapeDtypeStruct((t, d), F32), jax.ShapeDtypeStruct((1, LANE), F32)),
        compiler_params=_params("arbitrary"),
    )(h, target)
    return jnp.sum(loss), dh


def _adamw(w, g, m, v, name):
    shape = w.shape
    cols = shape[-1]
    rows = int(np.prod(shape[:-1])) if len(shape) > 1 else 1
    w2, g2, m2, v2 = (a.reshape(rows, cols) for a in (w, g, m, v))
    tr = _pick(rows, (512, 256, 128, 64, 32, 16, 8))
    c1 = 1.0 / (1.0 - ADAM_B1 ** ADAM_STEP)
    c2 = 1.0 / (1.0 - ADAM_B2 ** ADAM_STEP)

    def body(w_ref, g_ref, m_ref, v_ref, d_ref, mo_ref, vo_ref):
        gv = g_ref[...]
        mn = ADAM_B1 * m_ref[...] + (1.0 - ADAM_B1) * gv
        vn = ADAM_B2 * v_ref[...] + (1.0 - ADAM_B2) * (gv * gv)
        d_ref[...] = -ADAM_LR * ((mn * c1) / (jnp.sqrt(vn * c2) + ADAM_EPS) + ADAM_WD * w_ref[...])
        mo_ref[...] = mn
        vo_ref[...] = vn

    blk = pl.BlockSpec((tr, cols), lambda i: (i, 0))
    sds = jax.ShapeDtypeStruct((rows, cols), F32)
    d, mo, vo = pl.pallas_call(
        body, name=name, grid=(rows // tr,), in_specs=[blk] * 4, out_specs=(blk,) * 3, out_shape=(sds,) * 3,
        compiler_params=_params("parallel"),
    )(w2, g2, m2, v2)
    return d.reshape(shape), mo.reshape(shape), vo.reshape(shape)


SHARD_AXIS = {
    "w_mod": 2, "g_norm": 2, "ffn_w1": 3, "ffn_w3": 3, "ffn_w2": 2, "sc_w_in": 2, "sc_conv": 2, "sc_w_out": 1,
    "mla_w_a": 1, "mla_g_qa": 1, "mla_w_uq": 2, "mla_w_ukv": 2, "mla_w_o": 1,
}
HIDDEN_MAJOR = ("ffn_w1", "ffn_w3")


def _view(name, arr, swapped=False):
    form, swap, _ = EXCHANGE[name]
    if swap and not swapped:
        arr = jnp.swapaxes(arr, -1, -2)
    if form == "mid":
        arr = arr.reshape((-1,) + arr.shape[-2:])
        return jnp.pad(arr, ((0, 0), (0, 0), (0, -arr.shape[-1] % LANE)))
    arr = arr.reshape(-1, arr.shape[-1])
    return jnp.pad(arr, ((0, -arr.shape[0] % 16), (0, 0)))


def _unview(name, view, shape, keep_swapped=False):
    form, swap, _ = EXCHANGE[name]
    shape = shape[:-2] + (shape[-1], shape[-2]) if swap else shape
    if form == "mid":
        view = view[:, :, :shape[-1]]
    else:
        view = view[:int(np.prod(shape[:-1]))]
    arr = view.reshape(shape)
    return arr if (not swap or keep_swapped) else jnp.swapaxes(arr, -1, -2)


def _full_shape(name, local_shape):
    ax = SHARD_AXIS[name]
    return local_shape[:ax] + (N_DEV * local_shape[ax],) + local_shape[ax + 1:]


def _win(ref, form, n, j):
    start = j * n
    if not isinstance(start, int):
        start = pl.multiple_of(start, LANE if form == "last" else math.gcd(n, 16))
    if form == "mid":
        return ref.at[:, pl.ds(start, n), :]
    return ref.at[:, pl.ds(start, n)]


def _windows(view, count, of):
    return view.shape[:1] + (view.shape[1] * count // of,) + view.shape[2:]


def _gather_work(views, forms):
    na = len(views)

    def plan(x_refs, out_refs, sems):
        send_sems, recv_sems, local_sems = sems
        x, y, c = lax.axis_index("x"), lax.axis_index("y"), lax.axis_index("c")
        me, sibling = (x, y, c), (x, y, 1 - c)
        chips = [(1 - x, y), (x, 1 - y), (1 - x, 1 - y)]

        def copy(a, k, block, to, from_input):
            dst = _win(out_refs[a], forms[a], views[a].shape[1], 4 * block[0] + 2 * block[1] + block[2])
            return pltpu.make_async_remote_copy(
                src_ref=x_refs[a] if from_input else dst, dst_ref=dst, send_sem=send_sems.at[a, k],
                recv_sem=recv_sems.at[a, k], device_id=to, device_id_type=MESH)

        mine = [pltpu.make_async_copy(x_refs[a], _win(out_refs[a], forms[a], views[a].shape[1], 4 * x + 2 * y + c),
                                      local_sems.at[a]) for a in range(na)]
        first = []
        for a in range(na):
            first.append(copy(a, 0, me, sibling, True))
            first += [copy(a, 1 + j, me, (*chip, c), True) for j, chip in enumerate(chips)]
        return copy, mine, first, me, sibling, chips, c

    def start(x_refs, out_refs, sems):
        _, mine, first, *_ = plan(x_refs, out_refs, sems)
        for cp in mine + first:
            cp.start()

    def finish(x_refs, out_refs, sems):
        copy, mine, first, me, sibling, chips, c = plan(x_refs, out_refs, sems)
        passed = []
        for j, chip in enumerate(chips):
            for a in range(na):
                copy(a, 1 + j, (*chip, c), me, False).wait_recv()
                fwd = copy(a, 4 + j, (*chip, c), sibling, False)
                fwd.start()
                passed.append(fwd)
        for a in range(na):
            copy(a, 0, sibling, me, False).wait_recv()
            for j, chip in enumerate(chips):
                copy(a, 4 + j, (*chip, 1 - c), me, False).wait_recv()
        for cp in first + passed:
            cp.wait_send()
        for cp in mine:
            cp.wait()

    return Hosted(
        list(views), [jax.ShapeDtypeStruct(_windows(v, N_DEV, 1), v.dtype) for v in views],
        [pltpu.SemaphoreType.DMA((na, 7)), pltpu.SemaphoreType.DMA((na, 7)), pltpu.SemaphoreType.DMA((na,))], start, finish)


def _push_work(srcs, out_shapes, n_copies, make_copies):
    na = len(srcs)

    def start(s_refs, r_refs, sems):
        for cp in make_copies(s_refs, r_refs, sems[0], sems[1]):
            cp.start()

    def finish(s_refs, r_refs, sems):
        copies = make_copies(s_refs, r_refs, sems[0], sems[1])
        for cp in copies:
            cp.wait_recv()
        for cp in copies:
            cp.wait_send()

    return Hosted(list(srcs), out_shapes, [pltpu.SemaphoreType.DMA((na, n_copies)), pltpu.SemaphoreType.DMA((na, n_copies))],
                  start, finish)


def _sibling_work(fulls, forms):
    na = len(fulls)
    widths = [f.shape[1] // N_DEV for f in fulls]

    def make_copies(g_refs, r_refs, send_sems, recv_sems):
        x, y, c = lax.axis_index("x"), lax.axis_index("y"), lax.axis_index("c")
        return [
            pltpu.make_async_remote_copy(
                src_ref=_win(g_refs[a], forms[a], widths[a], 2 * chip + (1 - c)),
                dst_ref=_win(r_refs[a], forms[a], widths[a], chip), send_sem=send_sems.at[a, chip],
                recv_sem=recv_sems.at[a, chip], device_id=(x, y, 1 - c), device_id_type=MESH)
            for a in range(na) for chip in range(N_CHIP)
        ]

    return _push_work(fulls, [jax.ShapeDtypeStruct(_windows(f, N_CHIP, N_DEV), f.dtype) for f in fulls], N_CHIP, make_copies)


def _chip_work(parts, forms):
    na = len(parts)
    widths = [p.shape[1] // N_CHIP for p in parts]

    def make_copies(p_refs, r_refs, send_sems, recv_sems):
        x, y, c = lax.axis_index("x"), lax.axis_index("y"), lax.axis_index("c")
        chips = [(1 - x, y), (x, 1 - y), (1 - x, 1 - y)]
        return [
            pltpu.make_async_remote_copy(
                src_ref=_win(p_refs[a], forms[a], widths[a], 2 * px + py), dst_ref=_win(r_refs[a], forms[a], widths[a], j),
                send_sem=send_sems.at[a, j], recv_sem=recv_sems.at[a, j], device_id=(px, py, c), device_id_type=MESH)
            for a in range(na) for j, (px, py) in enumerate(chips)
        ]

    return _push_work(parts, [jax.ShapeDtypeStruct(_windows(p, 3, N_CHIP), p.dtype) for p in parts], 3, make_copies)


def _sum_tiles(view, form, n):
    if form == "mid":
        tr = n
        while tr * view.shape[2] * 4 > 2 * 1024 * 1024 and tr % 32 == 0:
            tr //= 2
        return 1, tr
    return _pick(view.shape[0], (512, 256, 128, 64, 32, 16)), n


def _window_spec(form, tl, tr, rest, window_of):
    if form == "mid":
        return lambda per: pl.BlockSpec((None, tr) + rest, lambda l, k, i, s: (l, window_of(k, s) * per + i, 0))
    return lambda per: pl.BlockSpec((tl, tr), lambda l, k, i, s: (l, window_of(k, s)))


def _chip_partials(g, recv, core, form, name):
    n = g.shape[1] // N_DEV
    tl, tr = _sum_tiles(g, form, n)
    per = n // tr
    rest = tuple(g.shape[2:])

    def body(core_ref, g_ref, r_ref, o_ref):
        o_ref[...] = (g_ref[...] + r_ref[...]).astype(o_ref.dtype)

    own = _window_spec(form, tl, tr, rest, lambda k, s: 2 * k + s[0])(per)
    by_chip = _window_spec(form, tl, tr, rest, lambda k, s: k)(per)
    return pl.pallas_call(
        body, name=name,
        grid_spec=pltpu.PrefetchScalarGridSpec(
            num_scalar_prefetch=1, grid=(g.shape[0] // tl, N_CHIP, per), in_specs=[own, by_chip], out_specs=by_chip),
        out_shape=jax.ShapeDtypeStruct(recv.shape, BF16), compiler_params=_params("parallel", "parallel", "parallel"),
    )(core, g, recv)


def _reduce_final(p, recv, chip, form, name):
    n = p.shape[1] // N_CHIP
    tl, tr = _sum_tiles(p, form, n)
    per = n // tr
    rest = tuple(p.shape[2:])

    def body(chip_ref, p_ref, ry_ref, rx_ref, rxy_ref, o_ref):
        own_pair = p_ref[...].astype(F32) + ry_ref[...].astype(F32)
        o_ref[...] = own_pair + (rx_ref[...].astype(F32) + rxy_ref[...].astype(F32))

    def rel(j):
        return _window_spec(form, tl, tr, rest, lambda k, s: j)(per)

    own = _window_spec(form, tl, tr, rest, lambda k, s: s[0])(per)
    return pl.pallas_call(
        body, name=name,
        grid_spec=pltpu.PrefetchScalarGridSpec(
            num_scalar_prefetch=1, grid=(p.shape[0] // tl, 1, per), in_specs=[own, rel(1), rel(0), rel(2)],
            out_specs=rel(0)),
        out_shape=jax.ShapeDtypeStruct(p.shape[:1] + (n,) + p.shape[2:], F32),
        compiler_params=_params("parallel", "parallel", "parallel"),
    )(chip, p, recv, recv, recv)


def _pack_replicated(arrays):
    pieces = []
    for a in arrays:
        flat = a.reshape(-1).astype(F32)
        pieces.append(jnp.pad(flat, (0, -flat.size % LANE)))
    total = sum(p.size for p in pieces)
    pieces.append(jnp.zeros((-total % (16 * LANE),), F32))
    return jnp.concatenate(pieces).reshape(-1, LANE)


def _unpack_replicated(buf, shapes):
    flat, out, off = buf.reshape(-1), [], 0
    for shape in shapes:
        size = int(np.prod(shape))
        out.append(flat[off:off + size].reshape(shape))
        off += size + (-size % LANE)
    return out


def _silu(v):
    return v * jax.nn.sigmoid(v)


FFN_NAMES = ("ffn_w1", "ffn_w3", "ffn_w2")
SC_NAMES = ("sc_w_in", "sc_conv", "sc_w_out")
MLA_SHARDED = ("mla_w_a", "mla_g_qa", "mla_w_uq", "mla_w_ukv", "mla_w_o")
MLA_NAMES = ("mla_w_a", "mla_g_qa", "mla_w_uq", "mla_g_kva", "mla_w_ukv", "mla_g_q", "mla_g_k", "mla_w_o")


def _local_step(src, x, c, ctx, target):
    bsz, n_lat, d = x.shape
    n_ctx = ctx.shape[1]
    assert bsz == 2
    geo = Geo(n_lat, n_ctx)
    depth = src.depth
    tc = _conv_tile(d)
    tabs = _rope_tables(geo)

    h = jnp.concatenate([x.reshape(2 * n_lat, d), ctx.reshape(2 * n_ctx, d)], axis=0)
    tgt = target.reshape(2 * n_lat, d)

    saved = []
    for i in range(depth):
        kind = i % 2
        wl, slots = src.weights(i), src.fwd_slots(i)
        gn = wl["g_norm"].astype(F32)
        mod4 = src.mod(i).reshape(N_SEG, N_MOD, 1, d)
        h, s1 = _ffn_fwd(h, gn[0:1], mod4, 0, wl, 0, geo, f"l{i}_f1", "f1", slots, slots)
        if kind == 0:
            mix = (_interleave(wl["sc_w_in"], 3, tc), wl["sc_conv"].astype(F32), wl["sc_w_out"])
            h, s2 = _sconv_fwd(h, gn[1:2], mod4, *mix, geo, f"l{i}_sc", slots, slots)
        else:
            mix = _mla_prepare(*[wl[name] for name in MLA_NAMES])
            h, s2 = _mla_fwd(h, gn[1:2], mod4, mix, i != depth - 1, tabs, geo, f"l{i}_mla", slots, slots)
        h, s3 = _ffn_fwd(h, gn[2:3], mod4, 6, wl, 1, geo, f"l{i}_f2", "f2", slots, slots)
        saved.append((wl, gn, mod4, mix, s1, s2, s3))

    loss, dh = _loss_head(h, tgt, geo, "loss_head")

    g_b_mod = [None] * depth
    for i in reversed(range(depth)):
        kind = i % 2
        wl, gn, mod4, mix, s1, s2, s3 = saved[i]
        slots = src.bwd_slots(i)
        gbuf = {name: lax.empty(wl[name].shape, F32) for name in ("ffn_w1", "ffn_w3", "ffn_w2")}
        dh, dg2, dm2 = _ffn_bwd(dh, s3, gn[2:3], mod4, 6, wl, 1, gbuf, geo, f"l{i}_f2", "f2", slots, slots)
        src.ffn2_grads(i, gbuf)
        if kind == 0:
            dh, dg1, dm1, dwin, dconv, dwout = _sconv_bwd(dh, s2, gn[1:2], mod4, *mix, geo, f"l{i}_sc", slots, slots)
            gl = dict(sc_w_in=_deinterleave(dwin, 3, tc), sc_conv=dconv, sc_w_out=dwout)
        else:
            dh, dg1, dm1, gm = _mla_bwd(dh, s2, gn[1:2], mod4, mix, i != depth - 1, tabs, geo, f"l{i}_mla", slots, slots)
            gl = _mla_unprepare(gm)
        dh, dg0, dm0 = _ffn_bwd(dh, s1, gn[0:1], mod4, 0, wl, 0, gbuf, geo, f"l{i}_f1", "f1", slots, slots)
        dmod = jnp.concatenate(list(dm0) + list(dm1) + list(dm2), axis=1).reshape(N_SEG, N_MOD * d)
        dmod8 = jnp.concatenate([dmod, jnp.zeros((8 - N_SEG, N_MOD * d), F32)], axis=0)
        g_b_mod[i] = jnp.sum(dmod, axis=0)
        gl.update(gbuf, g_norm=jnp.concatenate([dg0, dg1, dg2], axis=0))
        src.dmod(i, dmod8)
        src.grads(i, gl)

    grad_x = dh[:2 * n_lat].reshape(x.shape)
    return loss, grad_x, jnp.stack(g_b_mod)


class _Slots:
    def __init__(self, get, put):
        self.get, self._put = get, put

    def __setitem__(self, slot, outs):
        self._put(slot, outs)


FWD_PLAN = {
    0: {"f1_up": ("ffn_w1",), "f1_down": ("g_norm", "mix"), "mix_a": ("ffn_w3",), "mix_b": ("ffn_w2",)},
    1: {"f1_up": ("ffn_w1",), "mix_a": ("ffn_w3", "g_norm", "mix"), "f2_up": ("ffn_w2",)},
}
SIBLING_PLAN = {"f2_dact": ("ffn_w1", "g_norm", "mix"), "f2_dw2": ("ffn_w3", "ffn_w2")}
BWD_PLAN = {
    0: {"f2_dnx": ("ffn_w1",), "mix_b": ("ffn_w3",), "mix_a": ("ffn_w2",), "f1_dact": ("g_norm", "mix")},
    1: {"f2_dnx": ("ffn_w1",), "mix_a": ("ffn_w3", "ffn_w2"), "f1_dnx": ("g_norm", "mix")},
}
DMOD_SLOT = {0: "mix_c", 1: "f1_dact"}
MOD_ROWS = 32


class _Exchange:
    def __init__(self, w):
        self.w = w
        self.depth = w["w_mod"].shape[0]
        self.c_ctx = w["c_ctx"]
        self.me = 4 * lax.axis_index("x") + 2 * lax.axis_index("y") + lax.axis_index("c")
        self.core = lax.axis_index("c").astype(jnp.int32).reshape(1)
        self.chip = (2 * lax.axis_index("x") + lax.axis_index("y")).astype(jnp.int32).reshape(1)
        self.full, self.gviews, self.parts, self.reduced, self.rep, self.dmods = {}, {}, {}, {}, {}, {}
        self.ctx_pre = jnp.zeros_like(self.c_ctx)

    def _layer_of(self, name, i):
        return i // 2 if name.startswith(("sc_", "mla_")) else i

    def _mixer(self, i):
        return SC_NAMES if i % 2 == 0 else MLA_SHARDED

    def _expand(self, names, i):
        out = []
        for name in names:
            out += list(self._mixer(i)) if name == "mix" else [name]
        return out

    def _group(self, i):
        return ["g_norm", "ffn_w1", "ffn_w3", "ffn_w2"] + list(self._mixer(i))

    def _local(self, name, i):
        arr = self.w[name][self._layer_of(name, i)]
        return arr[:, None] if name == "mla_g_qa" else arr

    def _shapes(self, name, i):
        local = tuple(self._local(name, i).shape)
        ax = SHARD_AXIS[name] - 1
        return local, local[:ax] + (N_DEV * local[ax],) + local[ax + 1:]

    def _gather(self, names, i):
        views = [_view(n, self._local(n, i).astype(BF16 if EXCHANGE[n][2] else F32)) for n in names]
        return _gather_work(views, [EXCHANGE[n][0] for n in names])

    def _gathered(self, names, i, outs):
        for name, fv in zip(names, outs):
            arr = _unview(name, fv, self._shapes(name, i)[1], keep_swapped=name in HIDDEN_MAJOR)
            self.full[name, i] = arr[:, 0] if name == "mla_g_qa" else arr

    def prefetch(self, c):
        bsz, d = c.shape
        (conds,) = _run_hosted(_gather_work([jnp.pad(c, ((0, 8 - bsz), (0, 0)))[None]], ["mid"]), "gather_cond")
        conds = conds.reshape(N_DEV, 8, d)[:, :bsz]
        act = _silu(jnp.concatenate([conds, jnp.broadcast_to(self.c_ctx, (N_DEV, 1, d))], axis=1))
        self.s_rows = jnp.pad(act.reshape(N_DEV * N_SEG, d), ((0, MOD_ROWS - N_DEV * N_SEG), (0, 0)))
        cols = jnp.stack([_mm(self.s_rows, self.w["w_mod"][l], name=f"mod_cols_{l}") for l in range(self.depth)])
        names = self._group(0)
        work = self._gather(names, 0)
        both = _gather_work(work.inputs + [cols.reshape(self.depth * MOD_ROWS, -1)], self._forms(names) + ["last"])
        outs = _run_hosted(both, "gather_l0")
        self._gathered(names, 0, outs[:-1])
        mods = lax.dynamic_slice_in_dim(outs[-1].reshape(self.depth, MOD_ROWS, -1), N_SEG * self.me, N_SEG, axis=1)
        self.mods = mods + self.w["b_mod"][:, None, :]

    def mod(self, i):
        return self.mods[i]

    def weights(self, i):
        wl = {name: self.full[name, i] for name in self._group(i)}
        if i % 2 == 1:
            for name in ("mla_g_kva", "mla_g_q", "mla_g_k"):
                wl[name] = self.w[name][i // 2]
        return wl

    def fwd_slots(self, i):
        plan = FWD_PLAN[i % 2] if i + 1 < self.depth else {}
        names = {slot: self._expand(plan[slot], i + 1) for slot in plan}
        return _Slots(lambda slot: self._gather(names[slot], i + 1) if slot in names else None,
                      lambda slot, outs: self._gathered(names[slot], i + 1, outs))

    def ffn2_grads(self, i, gbuf):
        if i == 0:
            for name in FFN_NAMES:
                self.gviews[name + "#1", 0] = _view(name, gbuf[name][1:2], swapped=True)

    def grads(self, i, gl):
        for name in self._group(i):
            g = gl[name][:, None] if name == "mla_g_qa" else gl[name]
            if i == 0 and name in FFN_NAMES:
                self.gviews[name + "#0", 0] = _view(name, g[0:1], swapped=True)
            else:
                self.gviews[name, i] = _view(name, g, swapped=name in HIDDEN_MAJOR)
        for name in REPLICATED:
            if name in gl:
                self.rep[name, i // 2] = gl[name]

    def dmod(self, i, dmod8):
        self.dmods[i] = dmod8

    def _dmod_gather(self, i):
        return _gather_work([self.dmods[i][None]], ["mid"])

    def _dmod_gathered(self, i, outs):
        n = self.w["w_mod"].shape[2]
        rows = outs[0].reshape(N_DEV, 8, -1)[:, :N_SEG]
        mine = lax.dynamic_slice_in_dim(rows, n * self.me, n, axis=2)
        flat = jnp.pad(mine.reshape(N_DEV * N_SEG, n), ((0, MOD_ROWS - N_DEV * N_SEG), (0, 0)))
        self.reduced["w_mod", i] = _mm(self.s_rows, flat, ta=True, name=f"dwmod_{i}")
        ctx_rows = jnp.pad(jnp.sum(mine[:, N_SEG - 1], axis=0, keepdims=True), ((0, 7), (0, 0)))
        self.ctx_pre = self.ctx_pre + _mm(ctx_rows, self.w["w_mod"][i], tb=True, name=f"dcond_{i}")[0]

    def _forms(self, names):
        return [EXCHANGE[n.split("#")[0]][0] for n in names]

    def _partials(self, names, i, from_sibling):
        for name, recv in zip(names, from_sibling):
            self.parts[name, i] = _chip_partials(self.gviews[name, i], recv, self.core, self._forms([name])[0],
                                                 f"partial_{name.replace('#', '_')}_{i}")

    def _finals(self, names, i, from_chips):
        for name, recv in zip(names, from_chips):
            rv = _reduce_final(self.parts[name, i], recv, self.chip, self._forms([name])[0],
                               f"final_{name.replace('#', '_')}_{i}")
            base = name.split("#")[0]
            shape = self._shapes(base, i)[0]
            arr = _unview(base, rv, (1,) + shape[1:] if "#" in name else shape)
            self.reduced[name, i] = arr[:, 0] if name == "mla_g_qa" else arr

    def bwd_slots(self, i):
        if i + 1 >= self.depth:
            return _Slots(lambda slot: None, None)
        plan = BWD_PLAN[i % 2]
        chips = {slot: (self._expand(plan[slot], i + 1), i + 1) for slot in plan}
        sibling = {slot: (self._expand(SIBLING_PLAN[slot], i + 1), i + 1) for slot in SIBLING_PLAN}
        if i == 0:
            sibling["mix_d"] = ([name + "#1" for name in FFN_NAMES], 0)
            chips["f1_dw2"] = (["ffn_w1#1"], 0)
            chips["f1_dnx"] = (["ffn_w3#1", "ffn_w2#1"], 0)

        def get(slot):
            if slot in sibling:
                names, group = sibling[slot]
                return _sibling_work([self.gviews[n, group] for n in names], self._forms(names))
            if slot in chips:
                names, group = chips[slot]
                return _chip_work([self.parts[n, group] for n in names], self._forms(names))
            if slot == DMOD_SLOT[i % 2]:
                return self._dmod_gather(i + 1)
            return None

        def put(slot, outs):
            if slot in sibling:
                self._partials(*sibling[slot], outs)
            elif slot in chips:
                self._finals(*chips[slot], outs)
            else:
                self._dmod_gathered(i + 1, outs)

        return _Slots(get, put)

    def finish(self, rep_grads):
        group = [name + "#0" if name in FFN_NAMES else name for name in self._group(0)]
        self._dmod_gathered(0, _run_hosted(self._dmod_gather(0), "gather_dmod_l0"))
        rep_grads["c_ctx"] = self.ctx_pre
        for name in REPLICATED:
            if name not in rep_grads:
                rep_grads[name] = jnp.stack([self.rep[name, j] for j in range(self.w[name].shape[0])])
        rep = _pack_replicated([rep_grads[name] for name in REPLICATED])
        views = [self.gviews[n, 0] for n in group] + [jnp.tile(rep[None], (1, N_DEV, 1))]
        forms = self._forms(group) + ["mid"]
        from_sibling = _run_hosted(_sibling_work(views, forms), "reduce_sibling_l0")
        self._partials(group, 0, from_sibling[:-1])
        rep_part = _chip_partials(views[-1], from_sibling[-1], self.core, "mid", "partial_replicated")
        parts = [self.parts[n, 0] for n in group] + [rep_part]
        from_chips = _run_hosted(_chip_work(parts, forms), "reduce_chips_l0")
        self._finals(group, 0, from_chips[:-1])
        for name in FFN_NAMES:
            self.reduced[name, 0] = jnp.concatenate([self.reduced[name + "#0", 0], self.reduced[name + "#1", 0]], axis=0)
        rep_sum = _reduce_final(rep_part, from_chips[-1], self.chip, "mid", "final_replicated")
        out = dict(zip(REPLICATED, _unpack_replicated(rep_sum, [self.w[name].shape for name in REPLICATED])))
        sg = jax.nn.sigmoid(self.c_ctx)
        out["c_ctx"] = out["c_ctx"] * (sg * (1.0 + self.c_ctx * (1.0 - sg)))
        for name in EXCHANGE:
            layers = range(self.w[name].shape[0])
            step = 2 if name.startswith(("sc_", "mla_")) else 1
            first = 1 if name.startswith("mla_") else 0
            out[name] = jnp.stack([self.reduced[name, first + step * l] for l in layers])
        return out


def kernel(x, c, ctx, c_ctx, w_mod, b_mod, g_norm, ffn_w1, ffn_w3, ffn_w2, sc_w_in, sc_conv, sc_w_out, mla_w_a, mla_g_qa, mla_w_uq, mla_g_kva, mla_w_ukv, mla_g_q, mla_g_k, mla_w_o, loss_target, m_c_ctx, m_w_mod, m_b_mod, m_g_norm, m_ffn_w1, m_ffn_w3, m_ffn_w2, m_sc_w_in, m_sc_conv, m_sc_w_out, m_mla_w_a, m_mla_g_qa, m_mla_w_uq, m_mla_g_kva, m_mla_w_ukv, m_mla_g_q, m_mla_g_k, m_mla_w_o, v_c_ctx, v_w_mod, v_b_mod, v_g_norm, v_ffn_w1, v_ffn_w3, v_ffn_w2, v_sc_w_in, v_sc_conv, v_sc_w_out, v_mla_w_a, v_mla_g_qa, v_mla_w_uq, v_mla_g_kva, v_mla_w_ukv, v_mla_g_q, v_mla_g_k, v_mla_w_o):
    w = dict(c_ctx=c_ctx, w_mod=w_mod, b_mod=b_mod, g_norm=g_norm, ffn_w1=ffn_w1, ffn_w3=ffn_w3, ffn_w2=ffn_w2,
             sc_w_in=sc_w_in, sc_conv=sc_conv, sc_w_out=sc_w_out, mla_w_a=mla_w_a, mla_g_qa=mla_g_qa, mla_w_uq=mla_w_uq,
             mla_g_kva=mla_g_kva, mla_w_ukv=mla_w_ukv, mla_g_q=mla_g_q, mla_g_k=mla_g_k, mla_w_o=mla_w_o)
    m = dict(c_ctx=m_c_ctx, w_mod=m_w_mod, b_mod=m_b_mod, g_norm=m_g_norm, ffn_w1=m_ffn_w1, ffn_w3=m_ffn_w3,
             ffn_w2=m_ffn_w2, sc_w_in=m_sc_w_in, sc_conv=m_sc_conv, sc_w_out=m_sc_w_out, mla_w_a=m_mla_w_a,
             mla_g_qa=m_mla_g_qa, mla_w_uq=m_mla_w_uq, mla_g_kva=m_mla_g_kva, mla_w_ukv=m_mla_w_ukv, mla_g_q=m_mla_g_q,
             mla_g_k=m_mla_g_k, mla_w_o=m_mla_w_o)
    v = dict(c_ctx=v_c_ctx, w_mod=v_w_mod, b_mod=v_b_mod, g_norm=v_g_norm, ffn_w1=v_ffn_w1, ffn_w3=v_ffn_w3,
             ffn_w2=v_ffn_w2, sc_w_in=v_sc_w_in, sc_conv=v_sc_conv, sc_w_out=v_sc_w_out, mla_w_a=v_mla_w_a,
             mla_g_qa=v_mla_g_qa, mla_w_uq=v_mla_w_uq, mla_g_kva=v_mla_g_kva, mla_w_ukv=v_mla_w_ukv, mla_g_q=v_mla_g_q,
             mla_g_k=v_mla_g_k, mla_w_o=v_mla_w_o)
    exchange = _Exchange(w)
    exchange.prefetch(c)
    loss, grad_x, g_b_mod = _local_step(exchange, x, c, ctx, loss_target)
    loss = lax.psum(loss, ("x", "y", "c"))
    reduced = exchange.finish(dict(b_mod=g_b_mod))

    outs = [[], [], [], []]
    for name in WEIGHTS:
        delta, new_m, new_v = _adamw(w[name], reduced[name], m[name], v[name], f"adamw_{name}")
        for lst, val in zip(outs, (reduced[name], delta, new_m, new_v)):
            lst.append(val)
    return (loss, grad_x, *outs[0], *outs[1], *outs[2], *outs[3])
```

```python
import functools
import math

import jax
import jax.numpy as jnp
import numpy as np
from jax import lax
from jax.experimental import pallas as pl
from jax.experimental.pallas import tpu as pltpu

F32 = jnp.float32
BF16 = jnp.bfloat16

N_MOD = 9
HEADS = 8
QK_NOPE = 128
QK_ROPE = 64
QK_HEAD = QK_NOPE + QK_ROPE
V_HEAD = 128
GRID_W = 64
ROPE_BASE = 10000.0
QK_SCALE = QK_HEAD ** -0.5
EPS = 1e-6
ADAM_LR, ADAM_B1, ADAM_B2, ADAM_EPS, ADAM_WD, ADAM_STEP = 0.001, 0.9, 0.999, 1e-08, 0.01, 10

N_DEV = 8
N_CHIP = 4
N_SEG = 3
LANE = 128
HEAD_PAD = 2 * LANE
VMEM_LIMIT_BYTES = 48 * 1024 * 1024
MESH = pl.DeviceIdType.MESH

WEIGHTS = ["c_ctx", "w_mod", "b_mod", "g_norm", "ffn_w1", "ffn_w3", "ffn_w2", "sc_w_in", "sc_conv", "sc_w_out",
           "mla_w_a", "mla_g_qa", "mla_w_uq", "mla_g_kva", "mla_w_ukv", "mla_g_q", "mla_g_k", "mla_w_o"]
EXCHANGE = {
    "w_mod": ("last", False, True), "ffn_w1": ("mid", True, True), "ffn_w3": ("mid", True, True),
    "ffn_w2": ("mid", False, True), "sc_w_in": ("last", False, True), "sc_w_out": ("mid", False, True),
    "mla_w_a": ("mid", False, True), "mla_w_uq": ("mid", True, True), "mla_w_ukv": ("last", False, True),
    "mla_w_o": ("mid", False, True), "g_norm": ("last", False, False), "sc_conv": ("last", False, False),
    "mla_g_qa": ("mid", False, False),
}
REPLICATED = ["c_ctx", "b_mod", "mla_g_kva", "mla_g_q", "mla_g_k"]


def _pick(n, cands):
    for cand in cands:
        if n % cand == 0:
            return cand
    return n


def _params(*sem):
    return pltpu.CompilerParams(dimension_semantics=sem, vmem_limit_bytes=VMEM_LIMIT_BYTES)


def _hbm():
    return pl.BlockSpec(memory_space=pl.ANY)


class Hosted:
    def __init__(self, inputs, out_shapes, scratch, start, finish):
        self.inputs, self.out_shapes, self.scratch, self.start, self.finish = inputs, out_shapes, scratch, start, finish


def _call(body, hosted, **kw):
    if hosted is None:
        return pl.pallas_call(body, **kw)
    single = not isinstance(kw["out_shape"], (tuple, list))
    out_shape = [kw["out_shape"]] if single else list(kw["out_shape"])
    out_specs = [kw["out_specs"]] if single else list(kw["out_specs"])
    in_specs, scratch, grid = list(kw["in_specs"]), list(kw.get("scratch_shapes", ())), kw["grid"]
    n_in, n_out, n_scr = len(in_specs), len(out_shape), len(scratch)
    h_in, h_out = len(hosted.inputs), len(hosted.out_shapes)

    def wrapped(*refs):
        ins, hins = refs[:n_in], refs[n_in:n_in + h_in]
        o0 = n_in + h_in
        outs, houts = refs[o0:o0 + n_out], refs[o0 + n_out:o0 + n_out + h_out]
        s0 = o0 + n_out + h_out
        scr, hscr = refs[s0:s0 + n_scr], refs[s0 + n_scr:]
        first = functools.reduce(jnp.logical_and, [pl.program_id(a) == 0 for a in range(len(grid))])
        last = functools.reduce(jnp.logical_and, [pl.program_id(a) == g - 1 for a, g in enumerate(grid)])

        @pl.when(first)
        def _():
            hosted.start(hins, houts, hscr)

        body(*ins, *outs, *scr)

        @pl.when(last)
        def _():
            hosted.finish(hins, houts, hscr)

    call = pl.pallas_call(
        wrapped, name=kw["name"], grid=grid, in_specs=in_specs + [_hbm()] * h_in,
        out_specs=tuple(out_specs + [_hbm()] * h_out), out_shape=tuple(out_shape + list(hosted.out_shapes)),
        scratch_shapes=scratch + list(hosted.scratch), input_output_aliases=kw.get("input_output_aliases", {}),
        compiler_params=_params(*["arbitrary"] * len(grid)))

    def run(*args):
        res = call(*args, *hosted.inputs)
        comp = res[:n_out]
        return (comp[0] if single else tuple(comp)), list(res[n_out:])

    return run


def _run_hosted(hosted, name):
    def body(*refs):
        h_in, h_out = len(hosted.inputs), len(hosted.out_shapes)
        hins, houts, hscr = refs[:h_in], refs[h_in:h_in + h_out], refs[h_in + h_out:]
        hosted.start(hins, houts, hscr)
        hosted.finish(hins, houts, hscr)

    return list(pl.pallas_call(
        body, name=name, in_specs=[_hbm()] * len(hosted.inputs), out_specs=tuple([_hbm()] * len(hosted.out_shapes)),
        out_shape=tuple(hosted.out_shapes), scratch_shapes=list(hosted.scratch))(*hosted.inputs))


class Geo:
    def __init__(self, n_lat, n_ctx):
        self.n_lat, self.n_ctx = n_lat, n_ctx
        self.rows = 2 * n_lat + 2 * n_ctx
        self.tile = n_ctx
        assert n_lat % n_ctx == 0 and n_ctx % 16 == 0
        self.mm_tile = _pick(n_lat, (512, 256, 128)) if self.rows % _pick(n_lat, (512, 256, 128)) == 0 else n_ctx
        self.big_tile = _pick(self.rows, (1536, 768, 512, 256))

    def seg(self, i, tile):
        return jnp.minimum((i * tile) // self.n_lat, N_SEG - 1)

    def seg_start(self, i, tile):
        row = i * tile
        return jnp.logical_or(row % self.n_lat == 0, row == 2 * self.n_lat) & (row <= 2 * self.n_lat)


_NT = (((1,), (1,)), ((), ()))
_NN = (((1,), (0,)), ((), ()))
_TN = (((0,), (0,)), ((), ()))


def _dot(a, b, dims):
    return lax.dot_general(a.astype(BF16), b.astype(BF16), dims, preferred_element_type=F32)


def _mm(a, b, *, ta=False, tb=False, out_dtype=F32, name, gate=None, hosted=None):
    (kdim, m) = a.shape if ta else a.shape[::-1]
    n = b.shape[0] if tb else b.shape[1]
    assert (b.shape[1] if tb else b.shape[0]) == kdim
    if gate is not None:
        tm = gate[4].mm_tile
    else:
        tm = _pick(m, (1536, 768, 512, 256, 128))
    tn = _pick(n, (1024, 512, 256, 128))
    tk = _pick(kdim, (1024, 512, 256, 128))
    nk = kdim // tk
    dims = (((0 if ta else 1,), (1 if tb else 0,)), ((), ()))

    def body(*refs):
        if gate is not None:
            a_ref, b_ref, res_ref, gate_ref, o_ref, y_ref, acc_ref = refs
        else:
            a_ref, b_ref, o_ref, acc_ref = refs
        kk = pl.program_id(2)

        @pl.when(kk == 0)
        def _():
            acc_ref[...] = jnp.zeros_like(acc_ref)

        acc_ref[...] += lax.dot_general(a_ref[...].astype(BF16), b_ref[...].astype(BF16), dims,
                                        preferred_element_type=F32)

        @pl.when(kk == nk - 1)
        def _():
            acc = acc_ref[...]
            if gate is not None:
                y_ref[...] = acc.astype(y_ref.dtype)
                o_ref[...] = res_ref[...] + (gate[3] * gate_ref[...]) * acc
            else:
                o_ref[...] = acc.astype(o_ref.dtype)

    a_spec = pl.BlockSpec((tk, tm), lambda i, j, k: (k, i)) if ta else pl.BlockSpec((tm, tk), lambda i, j, k: (i, k))
    b_spec = pl.BlockSpec((tn, tk), lambda i, j, k: (j, k)) if tb else pl.BlockSpec((tk, tn), lambda i, j, k: (k, j))
    o_spec = pl.BlockSpec((tm, tn), lambda i, j, k: (i, j))
    in_specs, args = [a_spec, b_spec], [a, b]
    out_shape, out_specs = jax.ShapeDtypeStruct((m, n), out_dtype), o_spec
    if gate is not None:
        res, mod4, kmod, _, geo = gate
        in_specs += [o_spec, pl.BlockSpec((None, None, 1, tn), lambda i, j, k: (geo.seg(i, tm), kmod, 0, j))]
        args += [res, mod4]
        out_shape = (jax.ShapeDtypeStruct((m, n), F32), jax.ShapeDtypeStruct((m, n), BF16))
        out_specs = (o_spec, o_spec)
    return _call(
        body, hosted, name=name, grid=(m // tm, n // tn, nk), in_specs=in_specs, out_specs=out_specs,
        out_shape=out_shape, scratch_shapes=[pltpu.VMEM((tm, tn), F32)],
        compiler_params=_params("parallel", "parallel", "arbitrary"),
    )(*args)


def _tn_wide(lhs, rhs, name, into=None, s0=0, hosted=None):
    t, m = lhs.shape
    n = rhs.shape[1]
    tm = _pick(m, (1408, 1024, 512, 256, 128))
    while tm * n * 4 > 6.5 * 1024 * 1024 and tm % 256 == 0:
        tm //= 2
    tk = next(c for c in (1536, 768, 512, 256, 128, t)
              if t % c == 0 and c * (tm + n) * 4 + tm * n * 12 <= 36 * 1024 * 1024)

    def body(l_ref, r_ref, *rest):
        o_ref = rest[-1]
        kk = pl.program_id(1)
        part = lax.dot_general(l_ref[...], r_ref[...], _TN, preferred_element_type=F32)

        @pl.when(kk == 0)
        def _():
            o_ref[...] = part

        @pl.when(kk > 0)
        def _():
            o_ref[...] += part

    in_specs = [pl.BlockSpec((tk, tm), lambda i, k: (k, i)), pl.BlockSpec((tk, n), lambda i, k: (k, 0))]
    if into is None:
        return _call(
            body, hosted, name=name, grid=(m // tm, t // tk), in_specs=in_specs,
            out_specs=pl.BlockSpec((tm, n), lambda i, k: (i, 0)), out_shape=jax.ShapeDtypeStruct((m, n), F32),
            compiler_params=_params("parallel", "arbitrary"),
        )(lhs, rhs)
    return _call(
        body, hosted, name=name, grid=(m // tm, t // tk), in_specs=in_specs + [pl.BlockSpec(memory_space=pl.ANY)],
        out_specs=pl.BlockSpec((None, tm, n), lambda i, k: (s0, i, 0)),
        out_shape=jax.ShapeDtypeStruct(into.shape, into.dtype), input_output_aliases={2: 0},
        compiler_params=_params("parallel", "arbitrary"),
    )(lhs, rhs, into)


def _mod_spec(geo, tile, kmod, d):
    return pl.BlockSpec((None, None, 1, d), lambda i: (geo.seg(i, tile), kmod, 0, 0))


def _pre_fwd(h, g, mod4, k_shift, geo, name):
    t, d = h.shape
    tile = geo.mm_tile

    def body(h_ref, g_ref, sh_ref, sc_ref, o_ref):
        hv = h_ref[...]
        r = lax.rsqrt(jnp.mean(hv * hv, axis=-1, keepdims=True) + EPS)
        y = hv * r * g_ref[...]
        o_ref[...] = (y * (1.0 + sc_ref[...]) + sh_ref[...]).astype(o_ref.dtype)

    row = pl.BlockSpec((tile, d), lambda i: (i, 0))
    return pl.pallas_call(
        body, name=name, grid=(t // tile,),
        in_specs=[row, pl.BlockSpec((1, d), lambda i: (0, 0)), _mod_spec(geo, tile, k_shift, d),
                  _mod_spec(geo, tile, k_shift + 1, d)],
        out_specs=row, out_shape=jax.ShapeDtypeStruct((t, d), BF16), compiler_params=_params("parallel"),
    )(h, g, mod4, mod4)


def _pre_bwd(h, g, mod4, k_shift, dnx, dres, geo, name):
    t, d = h.shape
    tile = geo.mm_tile

    def body(h_ref, g_ref, sc_ref, dnx_ref, dres_ref, dh_ref, dg_ref, dsh_ref, dsc_ref):
        i = pl.program_id(0)
        hv, gv, dout = h_ref[...], g_ref[...], dnx_ref[...].astype(F32)
        r = lax.rsqrt(jnp.mean(hv * hv, axis=-1, keepdims=True) + EPS)
        xhat = hv * r
        dy = dout * (1.0 + sc_ref[...])
        u = dy * gv
        dh_ref[...] = r * (u - xhat * jnp.mean(u * xhat, axis=-1, keepdims=True)) + dres_ref[...]

        @pl.when(i == 0)
        def _():
            dg_ref[...] = jnp.zeros_like(dg_ref)

        @pl.when(geo.seg_start(i, tile))
        def _():
            dsh_ref[...] = jnp.zeros_like(dsh_ref)
            dsc_ref[...] = jnp.zeros_like(dsc_ref)

        dg_ref[...] += jnp.sum(dy * xhat, axis=0, keepdims=True)
        dsh_ref[...] += jnp.sum(dout, axis=0, keepdims=True)
        dsc_ref[...] += jnp.sum(dout * (xhat * gv), axis=0, keepdims=True)

    row = pl.BlockSpec((tile, d), lambda i: (i, 0))
    vec = pl.BlockSpec((1, d), lambda i: (0, 0))
    segv = pl.BlockSpec((None, 1, d), lambda i: (geo.seg(i, tile), 0, 0))
    return pl.pallas_call(
        body, name=name, grid=(t // tile,),
        in_specs=[row, vec, _mod_spec(geo, tile, k_shift + 1, d), row, row],
        out_specs=(row, vec, segv, segv),
        out_shape=(jax.ShapeDtypeStruct((t, d), F32), jax.ShapeDtypeStruct((1, d), F32),
                   jax.ShapeDtypeStruct((N_SEG, 1, d), F32), jax.ShapeDtypeStruct((N_SEG, 1, d), F32)),
        compiler_params=_params("arbitrary"),
    )(h, g, mod4, dnx, dres)


def _gate_bwd(dh, y, mod4, k_gate, coef, geo, name):
    t, d = dh.shape
    tile = geo.mm_tile

    def body(dh_ref, y_ref, gt_ref, dy_ref, dgt_ref):
        i = pl.program_id(0)
        dhv = dh_ref[...]
        dy_ref[...] = ((coef * gt_ref[...]) * dhv).astype(dy_ref.dtype)

        @pl.when(geo.seg_start(i, tile))
        def _():
            dgt_ref[...] = jnp.zeros_like(dgt_ref)

        dgt_ref[...] += coef * jnp.sum(dhv * y_ref[...].astype(F32), axis=0, keepdims=True)

    row = pl.BlockSpec((tile, d), lambda i: (i, 0))
    segv = pl.BlockSpec((None, 1, d), lambda i: (geo.seg(i, tile), 0, 0))
    return pl.pallas_call(
        body, name=name, grid=(t // tile,), in_specs=[row, row, _mod_spec(geo, tile, k_gate, d)],
        out_specs=(row, segv),
        out_shape=(jax.ShapeDtypeStruct((t, d), BF16), jax.ShapeDtypeStruct((N_SEG, 1, d), F32)),
        compiler_params=_params("arbitrary"),
    )(dh, y, mod4)


def _ff_tile(f):
    return _pick(f, (256, 128))


def _ffn_up(nx, w1t, w3t, s0, geo, name, hosted=None):
    t, d = nx.shape
    f = w1t.shape[1]
    tm, tn = geo.big_tile, _ff_tile(f)

    def body(x_ref, w1_ref, w3_ref, ga_ref, gb_ref, act_ref):
        xv = x_ref[...]
        a = lax.dot_general(xv, w1_ref[...], _NT, preferred_element_type=F32)
        bv = lax.dot_general(xv, w3_ref[...], _NT, preferred_element_type=F32)
        sg = jax.nn.sigmoid(a)
        silu = a * sg
        ga_ref[...] = (bv * (sg + silu * (1.0 - sg))).astype(ga_ref.dtype)
        gb_ref[...] = silu.astype(gb_ref.dtype)
        act_ref[...] = (silu * bv).astype(act_ref.dtype)

    w_spec = pl.BlockSpec((None, tn, d), lambda i, j: (s0, j, 0))
    o_spec = pl.BlockSpec((tm, tn), lambda i, j: (i, j))
    sds = jax.ShapeDtypeStruct((t, f), BF16)
    return _call(
        body, hosted, name=name, grid=(t // tm, f // tn),
        in_specs=[pl.BlockSpec((tm, d), lambda i, j: (i, 0)), w_spec, w_spec],
        out_specs=(o_spec,) * 3, out_shape=(sds,) * 3, compiler_params=_params("parallel", "parallel"),
    )(nx, w1t, w3t)


def _ffn_down(act, w2, s0, res, mod4, k_gate, geo, name, hosted=None):
    t, f = act.shape
    d = w2.shape[2]
    tm, tn = geo.mm_tile, _pick(d, (1024, 512, 256, 128))

    def body(a_ref, w_ref, res_ref, gate_ref, o_ref, y_ref):
        acc = lax.dot_general(a_ref[...], w_ref[...], _NN, preferred_element_type=F32)
        y_ref[...] = acc.astype(y_ref.dtype)
        o_ref[...] = res_ref[...] + (0.5 * gate_ref[...]) * acc

    o_spec = pl.BlockSpec((tm, tn), lambda i, j: (i, j))
    return _call(
        body, hosted, name=name, grid=(t // tm, d // tn),
        in_specs=[pl.BlockSpec((tm, f), lambda i, j: (i, 0)), pl.BlockSpec((None, f, tn), lambda i, j: (s0, 0, j)),
                  o_spec, pl.BlockSpec((None, None, 1, tn), lambda i, j: (geo.seg(i, tm), k_gate, 0, j))],
        out_specs=(o_spec, o_spec),
        out_shape=(jax.ShapeDtypeStruct((t, d), F32), jax.ShapeDtypeStruct((t, d), BF16)),
        compiler_params=_params("parallel", "parallel"),
    )(act, w2, res, mod4)


def _ffn_dact(dy, w2, ga, gb, s0, geo, name, hosted=None):
    t, d = dy.shape
    f = w2.shape[1]
    tm, tn = geo.big_tile, _ff_tile(f)

    def body(dy_ref, w_ref, ga_ref, gb_ref, da_ref, db_ref):
        dact = lax.dot_general(dy_ref[...], w_ref[...], _NT, preferred_element_type=F32)
        da_ref[...] = (dact * ga_ref[...].astype(F32)).astype(da_ref.dtype)
        db_ref[...] = (dact * gb_ref[...].astype(F32)).astype(db_ref.dtype)

    o_spec = pl.BlockSpec((tm, tn), lambda i, j: (i, j))
    sds = jax.ShapeDtypeStruct((t, f), BF16)
    return _call(
        body, hosted, name=name, grid=(t // tm, f // tn),
        in_specs=[pl.BlockSpec((tm, d), lambda i, j: (i, 0)), pl.BlockSpec((None, tn, d), lambda i, j: (s0, j, 0)),
                  o_spec, o_spec],
        out_specs=(o_spec, o_spec), out_shape=(sds, sds), compiler_params=_params("parallel", "parallel"),
    )(dy, w2, ga, gb)


def _ffn_dnx(da, db, w1t, w3t, s0, geo, name, hosted=None):
    t, f = da.shape
    d = w1t.shape[2]
    tm, tn = geo.mm_tile, _pick(d, (1024, 512, 256, 128))

    def body(da_ref, db_ref, w1_ref, w3_ref, o_ref):
        o_ref[...] = (lax.dot_general(da_ref[...], w1_ref[...], _NN, preferred_element_type=F32)
                      + lax.dot_general(db_ref[...], w3_ref[...], _NN, preferred_element_type=F32))

    x_spec = pl.BlockSpec((tm, f), lambda j, i: (i, 0))
    w_spec = pl.BlockSpec((None, f, tn), lambda j, i: (s0, 0, j))
    return _call(
        body, hosted, name=name, grid=(d // tn, t // tm), in_specs=[x_spec, x_spec, w_spec, w_spec],
        out_specs=pl.BlockSpec((tm, tn), lambda j, i: (i, j)), out_shape=jax.ShapeDtypeStruct((t, d), F32),
        compiler_params=_params("parallel", "parallel"),
    )(da, db, w1t, w3t)


def _with_host(fn, hosts, got, slot, *args, **kw):
    hosted = hosts.get(slot)
    if hosted is None:
        return fn(*args, **kw)
    out, got[slot] = fn(*args, hosted=hosted, **kw)
    return out


def _ffn_fwd(h, g, mod4, k0, w, s0, geo, tag, sub, hosts, got):
    nx = _pre_fwd(h, g, mod4, k0, geo, f"{tag}_pre")
    a, b, act = _with_host(_ffn_up, hosts, got, f"{sub}_up", nx, w["ffn_w1"], w["ffn_w3"], s0, geo, f"{tag}_up")
    h_out, y = _with_host(_ffn_down, hosts, got, f"{sub}_down", act, w["ffn_w2"], s0, h, mod4, k0 + 2, geo, f"{tag}_down")
    return h_out, (h, nx, a, b, act, y)


def _ffn_bwd(dh_out, saved, g, mod4, k0, w, s0, gbuf, geo, tag, sub, hosts, got):
    h, nx, a, b, act, y = saved
    dy, dgate = _gate_bwd(dh_out, y, mod4, k0 + 2, 0.5, geo, f"{tag}_dgate")
    da, db = _with_host(_ffn_dact, hosts, got, f"{sub}_dact", dy, w["ffn_w2"], a, b, s0, geo, f"{tag}_dact")
    gbuf["ffn_w2"] = _with_host(_tn_wide, hosts, got, f"{sub}_dw2", act, dy, f"{tag}_dw2", into=gbuf["ffn_w2"], s0=s0)
    dnx = _with_host(_ffn_dnx, hosts, got, f"{sub}_dnx", da, db, w["ffn_w1"], w["ffn_w3"], s0, geo, f"{tag}_dnx")
    gbuf["ffn_w1"] = _tn_wide(da, nx, f"{tag}_dw1", into=gbuf["ffn_w1"], s0=s0)
    gbuf["ffn_w3"] = _tn_wide(db, nx, f"{tag}_dw3", into=gbuf["ffn_w3"], s0=s0)
    dh, dg, dshift, dscale = _pre_bwd(h, g, mod4, k0, dnx, dh_out, geo, f"{tag}_dpre")
    return dh, dg, (dshift, dscale, dgate)


def _interleave(w, n_parts, tile):
    lead, cols = w.shape[:-1], w.shape[-1] // n_parts
    return w.reshape(*lead, n_parts, cols // tile, tile).swapaxes(-3, -2).reshape(*lead, n_parts * cols)


def _deinterleave(w, n_parts, tile):
    lead, cols = w.shape[:-1], w.shape[-1] // n_parts
    return w.reshape(*lead, cols // tile, n_parts, tile).swapaxes(-3, -2).reshape(*lead, n_parts * cols)


HALO = 16


def _conv_tile(c):
    return _pick(c, (512, 256, 128))


def _conv_specs(geo, tc, t):
    tile = geo.tile
    per = tile // HALO
    last = t // HALO - 1
    cur = pl.BlockSpec((tile, 3 * tc), lambda j, i: (i, j))
    prev = pl.BlockSpec((HALO, 3 * tc), lambda j, i: (jnp.maximum(i * per - 1, 0), j))
    nxt = pl.BlockSpec((HALO, 3 * tc), lambda j, i: (jnp.minimum((i + 1) * per, last), j))
    return cur, prev, nxt


def _conv_edges(geo, i):
    tile = geo.tile
    row = i * tile
    lat = row < 2 * geo.n_lat
    first = jnp.where(lat, row % geo.n_lat == 0, (row - 2 * geo.n_lat) % geo.n_ctx == 0)
    end = row + tile
    last = jnp.where(lat, end % geo.n_lat == 0, (end - 2 * geo.n_lat) % geo.n_ctx == 0)
    return first, last


def _shift_rows(v, before, after):
    n = v.shape[0]
    rows = lax.broadcasted_iota(jnp.int32, v.shape, 0)
    down = jnp.where(rows == 0, before, pltpu.roll(v, 1, 0))
    up = jnp.where(rows == n - 1, after, pltpu.roll(v, n - 1, 0))
    return down, up


def _conv_fwd(proj, conv_w, geo, name, hosted=None):
    t, c3 = proj.shape
    c = c3 // 3
    tc, tile = _conv_tile(c), geo.tile

    def body(cur_ref, prev_ref, next_ref, w_ref, o_ref):
        first, last = _conv_edges(geo, pl.program_id(1))
        bv = cur_ref[:, :tc].astype(F32)
        p = cur_ref[:, tc:2 * tc].astype(F32) * cur_ref[:, 2 * tc:].astype(F32)
        p_before = prev_ref[HALO - 1:HALO, tc:2 * tc].astype(F32) * prev_ref[HALO - 1:HALO, 2 * tc:].astype(F32)
        p_after = next_ref[0:1, tc:2 * tc].astype(F32) * next_ref[0:1, 2 * tc:].astype(F32)
        p_before = jnp.where(first, 0.0, p_before)
        p_after = jnp.where(last, 0.0, p_after)
        pm1, pp1 = _shift_rows(p, p_before, p_after)
        w = w_ref[...]
        q = w[0:1] * pm1 + w[1:2] * p + w[2:3] * pp1
        o_ref[...] = (bv * q).astype(o_ref.dtype)

    cur, prev, nxt = _conv_specs(geo, tc, t)
    return _call(
        body, hosted, name=name, grid=(c // tc, t // tile),
        in_specs=[cur, prev, nxt, pl.BlockSpec((3, tc), lambda j, i: (0, j))],
        out_specs=pl.BlockSpec((tile, tc), lambda j, i: (i, j)), out_shape=jax.ShapeDtypeStruct((t, c), BF16),
        compiler_params=_params("parallel", "parallel"),
    )(proj, proj, proj, conv_w)


def _conv_bwd(proj, dyc, conv_w, geo, name, hosted=None):
    t, c3 = proj.shape
    c = c3 // 3
    tc, tile = _conv_tile(c), geo.tile

    def body(cur_ref, prev_ref, next_ref, d_ref, dprev_ref, dnext_ref, w_ref, o_ref, dw_ref):
        i = pl.program_id(1)
        first, last = _conv_edges(geo, i)
        bv = cur_ref[:, :tc].astype(F32)
        cv = cur_ref[:, tc:2 * tc].astype(F32)
        uv = cur_ref[:, 2 * tc:].astype(F32)
        p = cv * uv
        p_before = prev_ref[HALO - 1:HALO, tc:2 * tc].astype(F32) * prev_ref[HALO - 1:HALO, 2 * tc:].astype(F32)
        p_after = next_ref[0:1, tc:2 * tc].astype(F32) * next_ref[0:1, 2 * tc:].astype(F32)
        p_before = jnp.where(first, 0.0, p_before)
        p_after = jnp.where(last, 0.0, p_after)
        pm1, pp1 = _shift_rows(p, p_before, p_after)
        w = w_ref[...]
        q = w[0:1] * pm1 + w[1:2] * p + w[2:3] * pp1
        dy = d_ref[...].astype(F32)
        dq = dy * bv
        dq_before = dprev_ref[HALO - 1:HALO, :].astype(F32) * prev_ref[HALO - 1:HALO, :tc].astype(F32)
        dq_after = dnext_ref[0:1, :].astype(F32) * next_ref[0:1, :tc].astype(F32)
        dq_before = jnp.where(first, 0.0, dq_before)
        dq_after = jnp.where(last, 0.0, dq_after)
        dqm1, dqp1 = _shift_rows(dq, dq_before, dq_after)
        dp = w[0:1] * dqp1 + w[1:2] * dq + w[2:3] * dqm1
        o_ref[:, :tc] = (dy * q).astype(o_ref.dtype)
        o_ref[:, tc:2 * tc] = (dp * uv).astype(o_ref.dtype)
        o_ref[:, 2 * tc:] = (dp * cv).astype(o_ref.dtype)

        @pl.when(i == 0)
        def _():
            dw_ref[...] = jnp.zeros_like(dw_ref)

        dw_ref[0:1, :] += jnp.sum(dq * pm1, axis=0, keepdims=True)
        dw_ref[1:2, :] += jnp.sum(dq * p, axis=0, keepdims=True)
        dw_ref[2:3, :] += jnp.sum(dq * pp1, axis=0, keepdims=True)

    cur, prev, nxt = _conv_specs(geo, tc, t)
    per, lastb = tile // HALO, t // HALO - 1
    dcur = pl.BlockSpec((tile, tc), lambda j, i: (i, j))
    dprev = pl.BlockSpec((HALO, tc), lambda j, i: (jnp.maximum(i * per - 1, 0), j))
    dnext = pl.BlockSpec((HALO, tc), lambda j, i: (jnp.minimum((i + 1) * per, lastb), j))
    wspec = pl.BlockSpec((3, tc), lambda j, i: (0, j))
    return _call(
        body, hosted, name=name, grid=(c // tc, t // tile), in_specs=[cur, prev, nxt, dcur, dprev, dnext, wspec],
        out_specs=(cur, wspec), out_shape=(jax.ShapeDtypeStruct((t, c3), BF16), jax.ShapeDtypeStruct((3, c), F32)),
        compiler_params=_params("parallel", "arbitrary"),
    )(proj, proj, proj, dyc, dyc, dyc, conv_w)


def _sconv_fwd(h, g, mod4, w_in, conv_w, w_out, geo, tag, hosts, got):
    nx = _pre_fwd(h, g, mod4, 3, geo, f"{tag}_pre")
    proj = _with_host(_mm, hosts, got, "mix_a", nx, w_in, out_dtype=BF16, name=f"{tag}_in")
    yc = _with_host(_conv_fwd, hosts, got, "mix_b", proj, conv_w, geo, f"{tag}_conv")
    h_out, y = _mm(yc, w_out, name=f"{tag}_out", gate=(h, mod4, 5, 1.0, geo))
    return h_out, (h, nx, proj, yc, y)


def _sconv_bwd(dh_out, saved, g, mod4, w_in, conv_w, w_out, geo, tag, hosts, got):
    h, nx, proj, yc, y = saved
    dy, dgate = _gate_bwd(dh_out, y, mod4, 5, 1.0, geo, f"{tag}_dgate")
    dyc = _with_host(_mm, hosts, got, "mix_d", dy, w_out, tb=True, out_dtype=BF16, name=f"{tag}_dyc")
    dw_out = _tn_wide(yc, dy, f"{tag}_dwout")
    dproj, dconv = _with_host(_conv_bwd, hosts, got, "mix_c", proj, dyc, conv_w, geo, f"{tag}_dconv")
    dnx = _with_host(_mm, hosts, got, "mix_b", dproj, w_in, tb=True, name=f"{tag}_dnx")
    dw_in = _with_host(_tn_wide, hosts, got, "mix_a", nx, dproj, f"{tag}_dwin")
    dh, dg, dshift, dscale = _pre_bwd(h, g, mod4, 3, dnx, dh_out, geo, f"{tag}_dpre")
    return dh, dg, (dshift, dscale, dgate), dw_in, dconv, dw_out


def _rope_swap(v):
    nf = QK_ROPE // 4
    return v.reshape(v.shape[:-1] + (2, 2, nf)).swapaxes(-3, -2).reshape(v.shape)


def _rope_tables(geo):
    n = geo.n_lat
    nf = QK_ROPE // 4
    pos = np.arange(n)
    inv = ROPE_BASE ** (-np.arange(nf, dtype=np.float32) / nf)
    ang = np.concatenate([(pos // GRID_W)[:, None] * inv, (pos % GRID_W)[:, None] * inv], axis=1).astype(np.float32)
    cos, sin = np.cos(ang), np.sin(ang)
    zeros = np.zeros((n, LANE - QK_ROPE), np.float32)
    c_lat = np.concatenate([cos, cos, zeros], axis=1)
    s_lat = np.concatenate([-sin, sin, zeros], axis=1)
    c_ctx = np.concatenate([np.ones((2 * geo.n_ctx, QK_ROPE), np.float32), np.zeros((2 * geo.n_ctx, LANE - QK_ROPE), np.float32)], 1)
    s_ctx = np.zeros((2 * geo.n_ctx, LANE), np.float32)
    return (jnp.asarray(np.concatenate([c_lat, c_lat, c_ctx], 0)), jnp.asarray(np.concatenate([s_lat, s_lat, s_ctx], 0)))


def _swap_halves(v):
    lanes = lax.broadcasted_iota(jnp.int32, v.shape, 1)
    return jnp.where(lanes < QK_ROPE // 2, pltpu.roll(v, LANE - QK_ROPE // 2, 1), pltpu.roll(v, QK_ROPE // 2, 1))


def _latent_norm_fwd(down, g_qa, g_kva, geo, name):
    t, wd = down.shape
    ql, kl = g_qa.shape[1], g_kva.shape[1]
    tile = geo.mm_tile

    def body(d_ref, gq_ref, gk_ref, cq_ref, ckv_ref):
        for lo, n, g_ref, o_ref in ((0, ql, gq_ref, cq_ref), (ql, kl, gk_ref, ckv_ref)):
            x = d_ref[:, lo:lo + n]
            r = lax.rsqrt(jnp.mean(x * x, axis=-1, keepdims=True) + EPS)
            o_ref[...] = (x * r * g_ref[...]).astype(o_ref.dtype)

    return pl.pallas_call(
        body, name=name, grid=(t // tile,),
        in_specs=[pl.BlockSpec((tile, wd), lambda i: (i, 0)), pl.BlockSpec((1, ql), lambda i: (0, 0)),
                  pl.BlockSpec((1, kl), lambda i: (0, 0))],
        out_specs=(pl.BlockSpec((tile, ql), lambda i: (i, 0)), pl.BlockSpec((tile, kl), lambda i: (i, 0))),
        out_shape=(jax.ShapeDtypeStruct((t, ql), BF16), jax.ShapeDtypeStruct((t, kl), BF16)),
        compiler_params=_params("parallel"),
    )(down, g_qa, g_kva)


def _latent_norm_bwd(down, g_qa, g_kva, dcqn, dckvn, dkr, geo, name):
    t, wd = down.shape
    ql, kl = g_qa.shape[1], g_kva.shape[1]
    tile = geo.mm_tile

    def body(d_ref, gq_ref, gk_ref, dq_ref, dk_ref, dkr_ref, o_ref, dgq_ref, dgk_ref):
        i = pl.program_id(0)

        @pl.when(i == 0)
        def _():
            dgq_ref[...] = jnp.zeros_like(dgq_ref)
            dgk_ref[...] = jnp.zeros_like(dgk_ref)

        for lo, n, g_ref, dy_ref, dg_ref in ((0, ql, gq_ref, dq_ref, dgq_ref), (ql, kl, gk_ref, dk_ref, dgk_ref)):
            x = d_ref[:, lo:lo + n]
            dy = dy_ref[...].astype(F32)
            r = lax.rsqrt(jnp.mean(x * x, axis=-1, keepdims=True) + EPS)
            xhat = x * r
            u = dy * g_ref[...]
            o_ref[:, lo:lo + n] = (r * (u - xhat * jnp.mean(u * xhat, axis=-1, keepdims=True))).astype(o_ref.dtype)
            dg_ref[...] += jnp.sum(dy * xhat, axis=0, keepdims=True)
        o_ref[:, ql + kl:] = dkr_ref[...].astype(o_ref.dtype)

    def row(n):
        return pl.BlockSpec((tile, n), lambda i: (i, 0))

    def vec(n):
        return pl.BlockSpec((1, n), lambda i: (0, 0))

    return pl.pallas_call(
        body, name=name, grid=(t // tile,),
        in_specs=[row(wd), vec(ql), vec(kl), row(ql), row(kl), row(wd - ql - kl)],
        out_specs=(row(wd), vec(ql), vec(kl)),
        out_shape=(jax.ShapeDtypeStruct((t, wd), BF16), jax.ShapeDtypeStruct((1, ql), F32),
                   jax.ShapeDtypeStruct((1, kl), F32)),
        compiler_params=_params("arbitrary"),
    )(down, g_qa, g_kva, dcqn, dckvn, dkr)


def _qk_specs(geo, xr_col, shared_rope):
    tile = geo.mm_tile
    xn_spec = pl.BlockSpec((tile, HEADS * LANE), lambda i: (i, 0))
    if shared_rope:
        xr_spec = pl.BlockSpec((tile, LANE), lambda i: (i, xr_col))
    else:
        xr_spec = pl.BlockSpec((tile, HEADS * LANE), lambda i: (i, xr_col // HEADS))
    vec = pl.BlockSpec((1, LANE), lambda i: (0, 0))
    tab = pl.BlockSpec((tile, LANE), lambda i: (i, 0))
    return tile, xn_spec, xr_spec, vec, tab


def _qk_norm(xn, xr):
    ss = jnp.sum(xn * xn, axis=-1, keepdims=True) + jnp.sum(xr * xr, axis=-1, keepdims=True)
    return lax.rsqrt(ss * (1.0 / QK_HEAD) + EPS)


def _head_lanes(ref, hh, shared=False):
    return ref[...] if shared else ref[:, hh * LANE:(hh + 1) * LANE]


def _qk_fwd(xn_arr, xr_arr, xr_col, shared_rope, gn, gr, cos, sin, geo, name):
    t = xn_arr.shape[0]
    tile, xn_spec, xr_spec, vec, tab = _qk_specs(geo, xr_col, shared_rope)

    def body(xn_ref, xr_ref, gn_ref, gr_ref, c_ref, s_ref, o_ref):
        cv, sv, gnv, grv = c_ref[...], s_ref[...], gn_ref[...], gr_ref[...]
        for hh in range(HEADS):
            xn = _head_lanes(xn_ref, hh).astype(F32)
            xr = _head_lanes(xr_ref, hh, shared_rope).astype(F32)
            r = _qk_norm(xn, xr)
            yr = xr * r * grv
            o_ref[:, hh * HEAD_PAD:hh * HEAD_PAD + LANE] = (xn * r * gnv).astype(o_ref.dtype)
            o_ref[:, hh * HEAD_PAD + LANE:(hh + 1) * HEAD_PAD] = (yr * cv + _swap_halves(yr) * sv).astype(o_ref.dtype)

    return pl.pallas_call(
        body, name=name, grid=(t // tile,), in_specs=[xn_spec, xr_spec, vec, vec, tab, tab],
        out_specs=pl.BlockSpec((tile, HEADS * HEAD_PAD), lambda i: (i, 0)),
        out_shape=jax.ShapeDtypeStruct((t, HEADS * HEAD_PAD), BF16), compiler_params=_params("parallel"),
    )(xn_arr, xr_arr, gn, gr, cos, sin)


def _qk_bwd(xn_arr, xr_arr, xr_col, shared_rope, gn, gr, cos, sin, dout, geo, name):
    t = xn_arr.shape[0]
    tile, xn_spec, xr_spec, vec, tab = _qk_specs(geo, xr_col, shared_rope)
    half = HEADS * LANE
    if shared_rope:
        n_lat_tiles = dout[0].shape[0] // tile
        assert dout[0].shape[0] % tile == 0 and dout[1].shape[0] % tile == 0

    def body(*refs):
        if shared_rope:
            xn_ref, xr_ref, gn_ref, gr_ref, c_ref, s_ref, dl_ref, dc_ref, vl_ref, vc_ref, raw_ref, dxr_ref, dgn_ref, dgr_ref = refs
        else:
            xn_ref, xr_ref, gn_ref, gr_ref, c_ref, s_ref, d_ref, raw_ref, dgn_ref, dgr_ref = refs
        i = pl.program_id(0)
        cv, sv, gnv, grv = c_ref[...], s_ref[...], gn_ref[...], gr_ref[...]
        dgn = jnp.zeros((1, LANE), F32)
        dgr = jnp.zeros((1, LANE), F32)
        dxr_sum = jnp.zeros((tile, LANE), F32)
        if shared_rope:
            latent = i < n_lat_tiles
            raw_ref[:, half:] = jnp.where(latent, vl_ref[...], vc_ref[...])
        for hh in range(HEADS):
            xn = _head_lanes(xn_ref, hh).astype(F32)
            xr = _head_lanes(xr_ref, hh, shared_rope).astype(F32)
            r = _qk_norm(xn, xr)
            xhn, xhr = xn * r, xr * r
            lo = hh * HEAD_PAD
            if shared_rope:
                dhead = jnp.where(latent, dl_ref[:, lo:lo + HEAD_PAD], dc_ref[:, lo:lo + HEAD_PAD]).astype(F32)
            else:
                dhead = d_ref[:, lo:lo + HEAD_PAD].astype(F32)
            dyn, dro = dhead[:, :LANE], dhead[:, LANE:]
            dyr = dro * cv + _swap_halves(dro * sv)
            un, ur = dyn * gnv, dyr * grv
            mean = (jnp.sum(un * xhn, axis=-1, keepdims=True) + jnp.sum(ur * xhr, axis=-1, keepdims=True)) * (1.0 / QK_HEAD)
            raw_ref[:, hh * LANE:(hh + 1) * LANE] = (r * (un - xhn * mean)).astype(raw_ref.dtype)
            dxr = r * (ur - xhr * mean)
            if shared_rope:
                dxr_sum = dxr_sum + dxr
            else:
                raw_ref[:, half + hh * LANE:half + (hh + 1) * LANE] = dxr.astype(raw_ref.dtype)
            dgn = dgn + jnp.sum(dyn * xhn, axis=0, keepdims=True)
            dgr = dgr + jnp.sum(dyr * xhr, axis=0, keepdims=True)
        if shared_rope:
            dxr_ref[...] = dxr_sum

        @pl.when(i == 0)
        def _():
            dgn_ref[...] = jnp.zeros_like(dgn_ref)
            dgr_ref[...] = jnp.zeros_like(dgr_ref)

        dgn_ref[...] += dgn
        dgr_ref[...] += dgr

    raw_spec = pl.BlockSpec((tile, 2 * half), lambda i: (i, 0))
    raw_shape = jax.ShapeDtypeStruct((t, 2 * half), BF16)
    vec_shape = jax.ShapeDtypeStruct((1, LANE), F32)
    in_specs = [xn_spec, xr_spec, vec, vec, tab, tab]
    if shared_rope:
        def two(width):
            return [pl.BlockSpec((tile, width), lambda i: (jnp.minimum(i, n_lat_tiles - 1), 0)),
                    pl.BlockSpec((tile, width), lambda i: (jnp.maximum(i - n_lat_tiles, 0), 0))]

        return pl.pallas_call(
            body, name=name, grid=(t // tile,), in_specs=in_specs + two(HEADS * HEAD_PAD) + two(half),
            out_specs=(raw_spec, pl.BlockSpec((tile, LANE), lambda i: (i, 0)), vec, vec),
            out_shape=(raw_shape, jax.ShapeDtypeStruct((t, LANE), F32), vec_shape, vec_shape),
            compiler_params=_params("arbitrary"),
        )(xn_arr, xr_arr, gn, gr, cos, sin, *dout)
    return pl.pallas_call(
        body, name=name, grid=(t // tile,),
        in_specs=in_specs + [pl.BlockSpec((tile, HEADS * HEAD_PAD), lambda i: (i, 0))],
        out_specs=(raw_spec, vec, vec), out_shape=(raw_shape, vec_shape, vec_shape),
        compiler_params=_params("arbitrary"),
    )(xn_arr, xr_arr, gn, gr, cos, sin, dout)


def _attn_specs(geo):
    tq, nq = geo.n_ctx, geo.n_lat // geo.n_ctx

    def qrow(b, i):
        return jnp.where(i < nq, b * nq + i, 2 * nq + b)

    q_spec = pl.BlockSpec((tq, HEAD_PAD), lambda b, hh, i: (qrow(b, i), hh))
    kc_spec = pl.BlockSpec((geo.n_ctx, HEAD_PAD), lambda b, hh, i: (2 * nq + b, hh))
    kl_spec = pl.BlockSpec((geo.n_lat, HEAD_PAD), lambda b, hh, i: (b, hh))
    vc_spec = pl.BlockSpec((geo.n_ctx, V_HEAD), lambda b, hh, i: (2 * nq + b, HEADS + hh))
    vl_spec = pl.BlockSpec((geo.n_lat, V_HEAD), lambda b, hh, i: (b, HEADS + hh))
    o_spec = pl.BlockSpec((tq, V_HEAD), lambda b, hh, i: (qrow(b, i), hh))
    return tq, nq, q_spec, kc_spec, kl_spec, vc_spec, vl_spec, o_spec


def _attn_fwd(q, k, kv, with_ctx_q, geo, name, hosted=None):
    t = q.shape[0]
    tq, nq, q_spec, kc_spec, kl_spec, vc_spec, vl_spec, o_spec = _attn_specs(geo)

    def body(q_ref, kc_ref, kl_ref, vc_ref, vl_ref, o_ref):
        i = pl.program_id(2)
        qv = q_ref[...]
        s_c = _dot(qv, kc_ref[...], _NT) * QK_SCALE

        @pl.when(i < nq)
        def _():
            s_l = _dot(qv, kl_ref[...], _NT) * QK_SCALE
            m = jnp.maximum(jnp.max(s_c, axis=-1, keepdims=True), jnp.max(s_l, axis=-1, keepdims=True))
            p_c, p_l = jnp.exp(s_c - m), jnp.exp(s_l - m)
            den = jnp.sum(p_c, axis=-1, keepdims=True) + jnp.sum(p_l, axis=-1, keepdims=True)
            o = _dot(p_c, vc_ref[...], _NN) + _dot(p_l, vl_ref[...], _NN)
            o_ref[...] = (o / den).astype(o_ref.dtype)

        @pl.when(i == nq)
        def _():
            if with_ctx_q:
                m = jnp.max(s_c, axis=-1, keepdims=True)
                p_c = jnp.exp(s_c - m)
                o = _dot(p_c, vc_ref[...], _NN) / jnp.sum(p_c, axis=-1, keepdims=True)
                o_ref[...] = o.astype(o_ref.dtype)
            else:
                o_ref[...] = jnp.zeros_like(o_ref)

    return _call(
        body, hosted, name=name, grid=(2, HEADS, nq + 1), in_specs=[q_spec, kc_spec, kl_spec, vc_spec, vl_spec],
        out_specs=o_spec, out_shape=jax.ShapeDtypeStruct((t, HEADS * V_HEAD), BF16),
        compiler_params=_params("parallel", "parallel", "arbitrary"),
    )(q, k, k, kv, kv)


def _attn_bwd(q, k, kv, do, with_ctx_q, geo, name, hosted=None):
    t = q.shape[0]
    tq, nq, q_spec, kc_spec, kl_spec, vc_spec, vl_spec, o_spec = _attn_specs(geo)

    def body(q_ref, kc_ref, kl_ref, vc_ref, vl_ref, do_ref, dq_ref, dkl_ref, dkc_ref, dvl_ref, dvc_ref,
             akl_ref, akc_ref, avl_ref, avc_ref):
        i = pl.program_id(2)

        @pl.when(i == 0)
        def _():
            for ref in (akl_ref, akc_ref, avl_ref, avc_ref):
                ref[...] = jnp.zeros_like(ref)

        qv, dov = q_ref[...], do_ref[...]
        s_c = _dot(qv, kc_ref[...], _NT) * QK_SCALE
        dp_c = _dot(dov, vc_ref[...], _NT)

        def ctx_part(p_c, delta):
            ds_c = (p_c * (dp_c - delta) * QK_SCALE).astype(BF16)
            akc_ref[...] += _dot(ds_c, qv, _TN)
            avc_ref[...] += _dot(p_c, dov, _TN)
            return _dot(ds_c, kc_ref[...], _NN)

        @pl.when(i < nq)
        def _():
            s_l = _dot(qv, kl_ref[...], _NT) * QK_SCALE
            m = jnp.maximum(jnp.max(s_c, axis=-1, keepdims=True), jnp.max(s_l, axis=-1, keepdims=True))
            p_c, p_l = jnp.exp(s_c - m), jnp.exp(s_l - m)
            inv = 1.0 / (jnp.sum(p_c, axis=-1, keepdims=True) + jnp.sum(p_l, axis=-1, keepdims=True))
            p_c, p_l = p_c * inv, p_l * inv
            dp_l = _dot(dov, vl_ref[...], _NT)
            delta = jnp.sum(p_c * dp_c, axis=-1, keepdims=True) + jnp.sum(p_l * dp_l, axis=-1, keepdims=True)
            ds_l = (p_l * (dp_l - delta) * QK_SCALE).astype(BF16)
            akl_ref[...] += _dot(ds_l, qv, _TN)
            avl_ref[...] += _dot(p_l, dov, _TN)
            dq_ref[...] = (ctx_part(p_c, delta) + _dot(ds_l, kl_ref[...], _NN)).astype(dq_ref.dtype)

        @pl.when(i == nq)
        def _():
            if with_ctx_q:
                m = jnp.max(s_c, axis=-1, keepdims=True)
                p_c = jnp.exp(s_c - m)
                p_c = p_c * (1.0 / jnp.sum(p_c, axis=-1, keepdims=True))
                delta = jnp.sum(p_c * dp_c, axis=-1, keepdims=True)
                dq_ref[...] = ctx_part(p_c, delta).astype(dq_ref.dtype)
            else:
                dq_ref[...] = jnp.zeros_like(dq_ref)
            dkl_ref[...] = akl_ref[...].astype(dkl_ref.dtype)
            dkc_ref[...] = akc_ref[...].astype(dkc_ref.dtype)
            dvl_ref[...] = avl_ref[...].astype(dvl_ref.dtype)
            dvc_ref[...] = avc_ref[...].astype(dvc_ref.dtype)

    def acc_spec(rows, width):
        return pl.BlockSpec((rows, width), lambda b, hh, i: (b, hh))

    return _call(
        body, hosted, name=name, grid=(2, HEADS, nq + 1), in_specs=[q_spec, kc_spec, kl_spec, vc_spec, vl_spec, o_spec],
        out_specs=(q_spec, acc_spec(geo.n_lat, HEAD_PAD), acc_spec(geo.n_ctx, HEAD_PAD), acc_spec(geo.n_lat, V_HEAD),
                   acc_spec(geo.n_ctx, V_HEAD)),
        out_shape=(jax.ShapeDtypeStruct((t, HEADS * HEAD_PAD), BF16),
                   jax.ShapeDtypeStruct((2 * geo.n_lat, HEADS * HEAD_PAD), BF16),
                   jax.ShapeDtypeStruct((2 * geo.n_ctx, HEADS * HEAD_PAD), BF16),
                   jax.ShapeDtypeStruct((2 * geo.n_lat, HEADS * V_HEAD), BF16),
                   jax.ShapeDtypeStruct((2 * geo.n_ctx, HEADS * V_HEAD), BF16)),
        scratch_shapes=[pltpu.VMEM((geo.n_lat, HEAD_PAD), F32), pltpu.VMEM((geo.n_ctx, HEAD_PAD), F32),
                        pltpu.VMEM((geo.n_lat, V_HEAD), F32), pltpu.VMEM((geo.n_ctx, V_HEAD), F32)],
        compiler_params=_params("parallel", "parallel", "arbitrary"),
    )(q, k, k, kv, kv, do)


def _mla_fwd(h, g, mod4, w, with_ctx_q, tabs, geo, tag, hosts, got):
    cos, sin = tabs
    ql, kl = w["g_qa"].shape[1], w["g_kva"].shape[1]
    kr_col = (ql + kl) // LANE
    nx = _pre_fwd(h, g, mod4, 3, geo, f"{tag}_pre")
    down = _mm(nx, w["w_a"], name=f"{tag}_down")
    cqn, ckvn = _latent_norm_fwd(down, w["g_qa"], w["g_kva"], geo, f"{tag}_lnorm")
    qraw = _mm(cqn, w["w_uq"], out_dtype=BF16, name=f"{tag}_uq")
    kvraw = _mm(ckvn, w["w_ukv"], out_dtype=BF16, name=f"{tag}_ukv")
    q = _qk_fwd(qraw, qraw, HEADS, False, w["gq_n"], w["gq_r"], cos, sin, geo, f"{tag}_qnorm")
    k = _qk_fwd(kvraw, down, kr_col, True, w["gk_n"], w["gk_r"], cos, sin, geo, f"{tag}_knorm")
    o = _with_host(_attn_fwd, hosts, got, "mix_a", q, k, kvraw, with_ctx_q, geo, f"{tag}_attn")
    h_out, y = _mm(o, w["w_o"], name=f"{tag}_o", gate=(h, mod4, 5, 1.0, geo))
    return h_out, (h, nx, down, cqn, ckvn, qraw, kvraw, q, k, o, y)


def _mla_bwd(dh_out, saved, g, mod4, w, with_ctx_q, tabs, geo, tag, hosts, got):
    cos, sin = tabs
    h, nx, down, cqn, ckvn, qraw, kvraw, q, k, o, y = saved
    ql, kl = w["g_qa"].shape[1], w["g_kva"].shape[1]
    kr_col = (ql + kl) // LANE
    dy, dgate = _gate_bwd(dh_out, y, mod4, 5, 1.0, geo, f"{tag}_dgate")
    do = _mm(dy, w["w_o"], tb=True, out_dtype=BF16, name=f"{tag}_do")
    dw_o = _tn_wide(o, dy, f"{tag}_dwo")
    dq, dk_lat, dk_ctx, dv_lat, dv_ctx = _with_host(_attn_bwd, hosts, got, "mix_a", q, k, kvraw, do, with_ctx_q, geo,
                                                    f"{tag}_dattn")
    dqraw, dgq_n, dgq_r = _qk_bwd(qraw, qraw, HEADS, False, w["gq_n"], w["gq_r"], cos, sin, dq, geo, f"{tag}_dqnorm")
    dkvraw, dkr, dgk_n, dgk_r = _qk_bwd(kvraw, down, kr_col, True, w["gk_n"], w["gk_r"], cos, sin,
                                        (dk_lat, dk_ctx, dv_lat, dv_ctx), geo, f"{tag}_dknorm")
    dcqn = _mm(dqraw, w["w_uq"], tb=True, out_dtype=BF16, name=f"{tag}_dcqn")
    dw_uq = _tn_wide(cqn, dqraw, f"{tag}_dwuq")
    dckvn = _mm(dkvraw, w["w_ukv"], tb=True, out_dtype=BF16, name=f"{tag}_dckvn")
    dw_ukv = _tn_wide(ckvn, dkvraw, f"{tag}_dwukv")
    ddown, dg_qa, dg_kva = _latent_norm_bwd(down, w["g_qa"], w["g_kva"], dcqn, dckvn, dkr, geo, f"{tag}_dlnorm")
    dnx = _mm(ddown, w["w_a"], tb=True, name=f"{tag}_dnx")
    dw_a = _tn_wide(nx, ddown, f"{tag}_dwa")
    dh, dg, dshift, dscale = _pre_bwd(h, g, mod4, 3, dnx, dh_out, geo, f"{tag}_dpre")
    grads = dict(w_a=dw_a, g_qa=dg_qa, w_uq=dw_uq, g_kva=dg_kva, w_ukv=dw_ukv, gq_n=dgq_n, gq_r=dgq_r, gk_n=dgk_n,
                 gk_r=dgk_r, w_o=dw_o)
    return dh, dg, (dshift, dscale, dgate), grads


def _mla_prepare(w_a, g_qa, w_uq, g_kva, w_ukv, g_q, g_k, w_o):
    ql, kl = g_qa.shape[0], g_kva.shape[0]
    d = w_a.shape[0]
    w_a_pad = jnp.concatenate([w_a[:, :ql + kl], _rope_swap(w_a[:, ql + kl:]), jnp.zeros((d, LANE - QK_ROPE), w_a.dtype)], axis=1)
    uq = w_uq.reshape(ql, HEADS, QK_HEAD)
    uq_r = jnp.pad(_rope_swap(uq[:, :, QK_NOPE:]), ((0, 0), (0, 0), (0, LANE - QK_ROPE)))
    w_uq_pad = jnp.concatenate([uq[:, :, :QK_NOPE].reshape(ql, HEADS * LANE), uq_r.reshape(ql, HEADS * LANE)], axis=1)
    ukv = w_ukv.reshape(kl, HEADS, QK_NOPE + V_HEAD)
    w_ukv_p = jnp.concatenate([ukv[:, :, :QK_NOPE].reshape(kl, HEADS * LANE), ukv[:, :, QK_NOPE:].reshape(kl, HEADS * V_HEAD)], axis=1)

    def gains(gv):
        gv = gv.astype(F32)
        return gv[None, :QK_NOPE], jnp.pad(_rope_swap(gv[QK_NOPE:]), (0, LANE - QK_ROPE))[None]

    gq_n, gq_r = gains(g_q)
    gk_n, gk_r = gains(g_k)
    return dict(w_a=w_a_pad, g_qa=g_qa.astype(F32)[None], w_uq=w_uq_pad, g_kva=g_kva.astype(F32)[None], w_ukv=w_ukv_p,
                gq_n=gq_n, gq_r=gq_r, gk_n=gk_n, gk_r=gk_r, w_o=w_o)


def _mla_unprepare(gr):
    ql, kl = gr["g_qa"].shape[1], gr["g_kva"].shape[1]
    dw_a = jnp.concatenate([gr["w_a"][:, :ql + kl], _rope_swap(gr["w_a"][:, ql + kl:ql + kl + QK_ROPE])], axis=1)
    uqn = gr["w_uq"][:, :HEADS * LANE].reshape(ql, HEADS, LANE)
    uqr = _rope_swap(gr["w_uq"][:, HEADS * LANE:].reshape(ql, HEADS, LANE)[:, :, :QK_ROPE])
    dw_uq = jnp.concatenate([uqn, uqr], axis=2).reshape(ql, HEADS * QK_HEAD)
    ukn = gr["w_ukv"][:, :HEADS * LANE].reshape(kl, HEADS, LANE)
    ukv = gr["w_ukv"][:, HEADS * LANE:].reshape(kl, HEADS, V_HEAD)
    dw_ukv = jnp.concatenate([ukn, ukv], axis=2).reshape(kl, HEADS * (QK_NOPE + V_HEAD))

    def gains(gn, grr):
        return jnp.concatenate([gn[0], _rope_swap(grr[0, :QK_ROPE])])

    return dict(mla_w_a=dw_a, mla_g_qa=gr["g_qa"][0], mla_w_uq=dw_uq, mla_g_kva=gr["g_kva"][0], mla_w_ukv=dw_ukv,
                mla_g_q=gains(gr["gq_n"], gr["gq_r"]), mla_g_k=gains(gr["gk_n"], gr["gk_r"]), mla_w_o=gr["w_o"])


def _loss_head(h, target, geo, name):
    t, d = h.shape
    tile = geo.mm_tile
    n_lat_tiles = 2 * geo.n_lat // tile

    def body(h_ref, t_ref, dh_ref, loss_ref):
        i = pl.program_id(0)

        @pl.when(i == 0)
        def _():
            loss_ref[...] = jnp.zeros_like(loss_ref)

        @pl.when(i < n_lat_tiles)
        def _():
            e = h_ref[...] - t_ref[...]
            dh_ref[...] = e * (1.0 / d)
            part = jnp.sum(e * e, axis=0, keepdims=True) * (0.5 / d)
            loss_ref[...] += sum(part[:, j * LANE:(j + 1) * LANE] for j in range(d // LANE))

        @pl.when(i >= n_lat_tiles)
        def _():
            dh_ref[...] = jnp.zeros_like(dh_ref)

    row = pl.BlockSpec((tile, d), lambda i: (i, 0))
    tgt = pl.BlockSpec((tile, d), lambda i: (jnp.minimum(i, n_lat_tiles - 1), 0))
    dh, loss = pl.pallas_call(
        body, name=name, grid=(t // tile,), in_specs=[row, tgt],
        out_specs=(row, pl.BlockSpec((1, LANE), lambda i: (0, 0))),
        out_shape=(jax.ShapeDtypeStruct((t, d), F32), jax.ShapeDtypeStruct((1, LANE), F32)),
        compiler_params=_params("arbitrary"),
    )(h, target)
    return jnp.sum(loss), dh


def _adamw(w, g, m, v, name):
    shape = w.shape
    cols = shape[-1]
    rows = int(np.prod(shape[:-1])) if len(shape) > 1 else 1
    w2, g2, m2, v2 = (a.reshape(rows, cols) for a in (w, g, m, v))
    tr = _pick(rows, (512, 256, 128, 64, 32, 16, 8))
    c1 = 1.0 / (1.0 - ADAM_B1 ** ADAM_STEP)
    c2 = 1.0 / (1.0 - ADAM_B2 ** ADAM_STEP)

    def body(w_ref, g_ref, m_ref, v_ref, d_ref, mo_ref, vo_ref):
        gv = g_ref[...]
        mn = ADAM_B1 * m_ref[...] + (1.0 - ADAM_B1) * gv
        vn = ADAM_B2 * v_ref[...] + (1.0 - ADAM_B2) * (gv * gv)
        d_ref[...] = -ADAM_LR * ((mn * c1) / (jnp.sqrt(vn * c2) + ADAM_EPS) + ADAM_WD * w_ref[...])
        mo_ref[...] = mn
        vo_ref[...] = vn

    blk = pl.BlockSpec((tr, cols), lambda i: (i, 0))
    sds = jax.ShapeDtypeStruct((rows, cols), F32)
    d, mo, vo = pl.pallas_call(
        body, name=name, grid=(rows // tr,), in_specs=[blk] * 4, out_specs=(blk,) * 3, out_shape=(sds,) * 3,
        compiler_params=_params("parallel"),
    )(w2, g2, m2, v2)
    return d.reshape(shape), mo.reshape(shape), vo.reshape(shape)


SHARD_AXIS = {
    "w_mod": 2, "g_norm": 2, "ffn_w1": 3, "ffn_w3": 3, "ffn_w2": 2, "sc_w_in": 2, "sc_conv": 2, "sc_w_out": 1,
    "mla_w_a": 1, "mla_g_qa": 1, "mla_w_uq": 2, "mla_w_ukv": 2, "mla_w_o": 1,
}
HIDDEN_MAJOR = ("ffn_w1", "ffn_w3")


def _view(name, arr, swapped=False):
    form, swap, _ = EXCHANGE[name]
    if swap and not swapped:
        arr = jnp.swapaxes(arr, -1, -2)
    if form == "mid":
        arr = arr.reshape((-1,) + arr.shape[-2:])
        return jnp.pad(arr, ((0, 0), (0, 0), (0, -arr.shape[-1] % LANE)))
    arr = arr.reshape(-1, arr.shape[-1])
    return jnp.pad(arr, ((0, -arr.shape[0] % 16), (0, 0)))


def _unview(name, view, shape, keep_swapped=False):
    form, swap, _ = EXCHANGE[name]
    shape = shape[:-2] + (shape[-1], shape[-2]) if swap else shape
    if form == "mid":
        view = view[:, :, :shape[-1]]
    else:
        view = view[:int(np.prod(shape[:-1]))]
    arr = view.reshape(shape)
    return arr if (not swap or keep_swapped) else jnp.swapaxes(arr, -1, -2)


def _full_shape(name, local_shape):
    ax = SHARD_AXIS[name]
    return local_shape[:ax] + (N_DEV * local_shape[ax],) + local_shape[ax + 1:]


def _win(ref, form, n, j):
    start = j * n
    if not isinstance(start, int):
        start = pl.multiple_of(start, LANE if form == "last" else math.gcd(n, 16))
    if form == "mid":
        return ref.at[:, pl.ds(start, n), :]
    return ref.at[:, pl.ds(start, n)]


def _windows(view, count, of):
    return view.shape[:1] + (view.shape[1] * count // of,) + view.shape[2:]


def _gather_work(views, forms):
    na = len(views)

    def plan(x_refs, out_refs, sems):
        send_sems, recv_sems, local_sems = sems
        x, y, c = lax.axis_index("x"), lax.axis_index("y"), lax.axis_index("c")
        me, sibling = (x, y, c), (x, y, 1 - c)
        chips = [(1 - x, y), (x, 1 - y), (1 - x, 1 - y)]

        def copy(a, k, block, to, from_input):
            dst = _win(out_refs[a], forms[a], views[a].shape[1], 4 * block[0] + 2 * block[1] + block[2])
            return pltpu.make_async_remote_copy(
                src_ref=x_refs[a] if from_input else dst, dst_ref=dst, send_sem=send_sems.at[a, k],
                recv_sem=recv_sems.at[a, k], device_id=to, device_id_type=MESH)

        mine = [pltpu.make_async_copy(x_refs[a], _win(out_refs[a], forms[a], views[a].shape[1], 4 * x + 2 * y + c),
                                      local_sems.at[a]) for a in range(na)]
        first = []
        for a in range(na):
            first.append(copy(a, 0, me, sibling, True))
            first += [copy(a, 1 + j, me, (*chip, c), True) for j, chip in enumerate(chips)]
        return copy, mine, first, me, sibling, chips, c

    def start(x_refs, out_refs, sems):
        _, mine, first, *_ = plan(x_refs, out_refs, sems)
        for cp in mine + first:
            cp.start()

    def finish(x_refs, out_refs, sems):
        copy, mine, first, me, sibling, chips, c = plan(x_refs, out_refs, sems)
        passed = []
        for j, chip in enumerate(chips):
            for a in range(na):
                copy(a, 1 + j, (*chip, c), me, False).wait_recv()
                fwd = copy(a, 4 + j, (*chip, c), sibling, False)
                fwd.start()
                passed.append(fwd)
        for a in range(na):
            copy(a, 0, sibling, me, False).wait_recv()
            for j, chip in enumerate(chips):
                copy(a, 4 + j, (*chip, 1 - c), me, False).wait_recv()
        for cp in first + passed:
            cp.wait_send()
        for cp in mine:
            cp.wait()

    return Hosted(
        list(views), [jax.ShapeDtypeStruct(_windows(v, N_DEV, 1), v.dtype) for v in views],
        [pltpu.SemaphoreType.DMA((na, 7)), pltpu.SemaphoreType.DMA((na, 7)), pltpu.SemaphoreType.DMA((na,))], start, finish)


def _push_work(srcs, out_shapes, n_copies, make_copies):
    na = len(srcs)

    def start(s_refs, r_refs, sems):
        for cp in make_copies(s_refs, r_refs, sems[0], sems[1]):
            cp.start()

    def finish(s_refs, r_refs, sems):
        copies = make_copies(s_refs, r_refs, sems[0], sems[1])
        for cp in copies:
            cp.wait_recv()
        for cp in copies:
            cp.wait_send()

    return Hosted(list(srcs), out_shapes, [pltpu.SemaphoreType.DMA((na, n_copies)), pltpu.SemaphoreType.DMA((na, n_copies))],
                  start, finish)


def _sibling_work(fulls, forms):
    na = len(fulls)
    widths = [f.shape[1] // N_DEV for f in fulls]

    def make_copies(g_refs, r_refs, send_sems, recv_sems):
        x, y, c = lax.axis_index("x"), lax.axis_index("y"), lax.axis_index("c")
        return [
            pltpu.make_async_remote_copy(
                src_ref=_win(g_refs[a], forms[a], widths[a], 2 * chip + (1 - c)),
                dst_ref=_win(r_refs[a], forms[a], widths[a], chip), send_sem=send_sems.at[a, chip],
                recv_sem=recv_sems.at[a, chip], device_id=(x, y, 1 - c), device_id_type=MESH)
            for a in range(na) for chip in range(N_CHIP)
        ]

    return _push_work(fulls, [jax.ShapeDtypeStruct(_windows(f, N_CHIP, N_DEV), f.dtype) for f in fulls], N_CHIP, make_copies)


def _chip_work(parts, forms):
    na = len(parts)
    widths = [p.shape[1] // N_CHIP for p in parts]

    def make_copies(p_refs, r_refs, send_sems, recv_sems):
        x, y, c = lax.axis_index("x"), lax.axis_index("y"), lax.axis_index("c")
        chips = [(1 - x, y), (x, 1 - y), (1 - x, 1 - y)]
        return [
            pltpu.make_async_remote_copy(
                src_ref=_win(p_refs[a], forms[a], widths[a], 2 * px + py), dst_ref=_win(r_refs[a], forms[a], widths[a], j),
                send_sem=send_sems.at[a, j], recv_sem=recv_sems.at[a, j], device_id=(px, py, c), device_id_type=MESH)
            for a in range(na) for j, (px, py) in enumerate(chips)
        ]

    return _push_work(parts, [jax.ShapeDtypeStruct(_windows(p, 3, N_CHIP), p.dtype) for p in parts], 3, make_copies)


def _sum_tiles(view, form, n):
    if form == "mid":
        tr = n
        while tr * view.shape[2] * 4 > 2 * 1024 * 1024 and tr % 32 == 0:
            tr //= 2
        return 1, tr
    return _pick(view.shape[0], (512, 256, 128, 64, 32, 16)), n


def _window_spec(form, tl, tr, rest, window_of):
    if form == "mid":
        return lambda per: pl.BlockSpec((None, tr) + rest, lambda l, k, i, s: (l, window_of(k, s) * per + i, 0))
    return lambda per: pl.BlockSpec((tl, tr), lambda l, k, i, s: (l, window_of(k, s)))


def _chip_partials(g, recv, core, form, name):
    n = g.shape[1] // N_DEV
    tl, tr = _sum_tiles(g, form, n)
    per = n // tr
    rest = tuple(g.shape[2:])

    def body(core_ref, g_ref, r_ref, o_ref):
        o_ref[...] = (g_ref[...] + r_ref[...]).astype(o_ref.dtype)

    own = _window_spec(form, tl, tr, rest, lambda k, s: 2 * k + s[0])(per)
    by_chip = _window_spec(form, tl, tr, rest, lambda k, s: k)(per)
    return pl.pallas_call(
        body, name=name,
        grid_spec=pltpu.PrefetchScalarGridSpec(
            num_scalar_prefetch=1, grid=(g.shape[0] // tl, N_CHIP, per), in_specs=[own, by_chip], out_specs=by_chip),
        out_shape=jax.ShapeDtypeStruct(recv.shape, BF16), compiler_params=_params("parallel", "parallel", "parallel"),
    )(core, g, recv)


def _reduce_final(p, recv, chip, form, name):
    n = p.shape[1] // N_CHIP
    tl, tr = _sum_tiles(p, form, n)
    per = n // tr
    rest = tuple(p.shape[2:])

    def body(chip_ref, p_ref, ry_ref, rx_ref, rxy_ref, o_ref):
        own_pair = p_ref[...].astype(F32) + ry_ref[...].astype(F32)
        o_ref[...] = own_pair + (rx_ref[...].astype(F32) + rxy_ref[...].astype(F32))

    def rel(j):
        return _window_spec(form, tl, tr, rest, lambda k, s: j)(per)

    own = _window_spec(form, tl, tr, rest, lambda k, s: s[0])(per)
    return pl.pallas_call(
        body, name=name,
        grid_spec=pltpu.PrefetchScalarGridSpec(
            num_scalar_prefetch=1, grid=(p.shape[0] // tl, 1, per), in_specs=[own, rel(1), rel(0), rel(2)],
            out_specs=rel(0)),
        out_shape=jax.ShapeDtypeStruct(p.shape[:1] + (n,) + p.shape[2:], F32),
        compiler_params=_params("parallel", "parallel", "parallel"),
    )(chip, p, recv, recv, recv)


def _pack_replicated(arrays):
    pieces = []
    for a in arrays:
        flat = a.reshape(-1).astype(F32)
        pieces.append(jnp.pad(flat, (0, -flat.size % LANE)))
    total = sum(p.size for p in pieces)
    pieces.append(jnp.zeros((-total % (16 * LANE),), F32))
    return jnp.concatenate(pieces).reshape(-1, LANE)


def _unpack_replicated(buf, shapes):
    flat, out, off = buf.reshape(-1), [], 0
    for shape in shapes:
        size = int(np.prod(shape))
        out.append(flat[off:off + size].reshape(shape))
        off += size + (-size % LANE)
    return out


def _silu(v):
    return v * jax.nn.sigmoid(v)


FFN_NAMES = ("ffn_w1", "ffn_w3", "ffn_w2")
SC_NAMES = ("sc_w_in", "sc_conv", "sc_w_out")
MLA_SHARDED = ("mla_w_a", "mla_g_qa", "mla_w_uq", "mla_w_ukv", "mla_w_o")
MLA_NAMES = ("mla_w_a", "mla_g_qa", "mla_w_uq", "mla_g_kva", "mla_w_ukv", "mla_g_q", "mla_g_k", "mla_w_o")


def _local_step(src, x, c, ctx, target):
    bsz, n_lat, d = x.shape
    n_ctx = ctx.shape[1]
    assert bsz == 2
    geo = Geo(n_lat, n_ctx)
    depth = src.depth
    tc = _conv_tile(d)
    tabs = _rope_tables(geo)

    h = jnp.concatenate([x.reshape(2 * n_lat, d), ctx.reshape(2 * n_ctx, d)], axis=0)
    tgt = target.reshape(2 * n_lat, d)

    saved = []
    for i in range(depth):
        kind = i % 2
        wl, slots = src.weights(i), src.fwd_slots(i)
        gn = wl["g_norm"].astype(F32)
        mod4 = src.mod(i).reshape(N_SEG, N_MOD, 1, d)
        h, s1 = _ffn_fwd(h, gn[0:1], mod4, 0, wl, 0, geo, f"l{i}_f1", "f1", slots, slots)
        if kind == 0:
            mix = (_interleave(wl["sc_w_in"], 3, tc), wl["sc_conv"].astype(F32), wl["sc_w_out"])
            h, s2 = _sconv_fwd(h, gn[1:2], mod4, *mix, geo, f"l{i}_sc", slots, slots)
        else:
            mix = _mla_prepare(*[wl[name] for name in MLA_NAMES])
            h, s2 = _mla_fwd(h, gn[1:2], mod4, mix, i != depth - 1, tabs, geo, f"l{i}_mla", slots, slots)
        h, s3 = _ffn_fwd(h, gn[2:3], mod4, 6, wl, 1, geo, f"l{i}_f2", "f2", slots, slots)
        saved.append((wl, gn, mod4, mix, s1, s2, s3))

    loss, dh = _loss_head(h, tgt, geo, "loss_head")

    g_b_mod = [None] * depth
    for i in reversed(range(depth)):
        kind = i % 2
        wl, gn, mod4, mix, s1, s2, s3 = saved[i]
        slots = src.bwd_slots(i)
        gbuf = {name: lax.empty(wl[name].shape, F32) for name in ("ffn_w1", "ffn_w3", "ffn_w2")}
        dh, dg2, dm2 = _ffn_bwd(dh, s3, gn[2:3], mod4, 6, wl, 1, gbuf, geo, f"l{i}_f2", "f2", slots, slots)
        src.ffn2_grads(i, gbuf)
        if kind == 0:
            dh, dg1, dm1, dwin, dconv, dwout = _sconv_bwd(dh, s2, gn[1:2], mod4, *mix, geo, f"l{i}_sc", slots, slots)
            gl = dict(sc_w_in=_deinterleave(dwin, 3, tc), sc_conv=dconv, sc_w_out=dwout)
        else:
            dh, dg1, dm1, gm = _mla_bwd(dh, s2, gn[1:2], mod4, mix, i != depth - 1, tabs, geo, f"l{i}_mla", slots, slots)
            gl = _mla_unprepare(gm)
        dh, dg0, dm0 = _ffn_bwd(dh, s1, gn[0:1], mod4, 0, wl, 0, gbuf, geo, f"l{i}_f1", "f1", slots, slots)
        dmod = jnp.concatenate(list(dm0) + list(dm1) + list(dm2), axis=1).reshape(N_SEG, N_MOD * d)
        dmod8 = jnp.concatenate([dmod, jnp.zeros((8 - N_SEG, N_MOD * d), F32)], axis=0)
        g_b_mod[i] = jnp.sum(dmod, axis=0)
        gl.update(gbuf, g_norm=jnp.concatenate([dg0, dg1, dg2], axis=0))
        src.dmod(i, dmod8)
        src.grads(i, gl)

    grad_x = dh[:2 * n_lat].reshape(x.shape)
    return loss, grad_x, jnp.stack(g_b_mod)


class _Slots:
    def __init__(self, get, put):
        self.get, self._put = get, put

    def __setitem__(self, slot, outs):
        self._put(slot, outs)


FWD_PLAN = {
    0: {"f1_up": ("ffn_w1",), "f1_down": ("g_norm", "mix"), "mix_a": ("ffn_w3",), "mix_b": ("ffn_w2",)},
    1: {"f1_up": ("ffn_w1",), "mix_a": ("ffn_w3", "g_norm", "mix"), "f2_up": ("ffn_w2",)},
}
SIBLING_PLAN = {"f2_dact": ("ffn_w1", "g_norm", "mix"), "f2_dw2": ("ffn_w3", "ffn_w2")}
BWD_PLAN = {
    0: {"f2_dnx": ("ffn_w1",), "mix_b": ("ffn_w3",), "mix_a": ("ffn_w2",), "f1_dact": ("g_norm", "mix")},
    1: {"f2_dnx": ("ffn_w1",), "mix_a": ("ffn_w3", "ffn_w2"), "f1_dnx": ("g_norm", "mix")},
}
DMOD_SLOT = {0: "mix_c", 1: "f1_dact"}
MOD_ROWS = 32


class _Exchange:
    def __init__(self, w):
        self.w = w
        self.depth = w["w_mod"].shape[0]
        self.c_ctx = w["c_ctx"]
        self.me = 4 * lax.axis_index("x") + 2 * lax.axis_index("y") + lax.axis_index("c")
        self.core = lax.axis_index("c").astype(jnp.int32).reshape(1)
        self.chip = (2 * lax.axis_index("x") + lax.axis_index("y")).astype(jnp.int32).reshape(1)
        self.full, self.gviews, self.parts, self.reduced, self.rep, self.dmods = {}, {}, {}, {}, {}, {}
        self.ctx_pre = jnp.zeros_like(self.c_ctx)

    def _layer_of(self, name, i):
        return i // 2 if name.startswith(("sc_", "mla_")) else i

    def _mixer(self, i):
        return SC_NAMES if i % 2 == 0 else MLA_SHARDED

    def _expand(self, names, i):
        out = []
        for name in names:
            out += list(self._mixer(i)) if name == "mix" else [name]
        return out

    def _group(self, i):
        return ["g_norm", "ffn_w1", "ffn_w3", "ffn_w2"] + list(self._mixer(i))

    def _local(self, name, i):
        arr = self.w[name][self._layer_of(name, i)]
        return arr[:, None] if name == "mla_g_qa" else arr

    def _shapes(self, name, i):
        local = tuple(self._local(name, i).shape)
        ax = SHARD_AXIS[name] - 1
        return local, local[:ax] + (N_DEV * local[ax],) + local[ax + 1:]

    def _gather(self, names, i):
        views = [_view(n, self._local(n, i).astype(BF16 if EXCHANGE[n][2] else F32)) for n in names]
        return _gather_work(views, [EXCHANGE[n][0] for n in names])

    def _gathered(self, names, i, outs):
        for name, fv in zip(names, outs):
            arr = _unview(name, fv, self._shapes(name, i)[1], keep_swapped=name in HIDDEN_MAJOR)
            self.full[name, i] = arr[:, 0] if name == "mla_g_qa" else arr

    def prefetch(self, c):
        bsz, d = c.shape
        (conds,) = _run_hosted(_gather_work([jnp.pad(c, ((0, 8 - bsz), (0, 0)))[None]], ["mid"]), "gather_cond")
        conds = conds.reshape(N_DEV, 8, d)[:, :bsz]
        act = _silu(jnp.concatenate([conds, jnp.broadcast_to(self.c_ctx, (N_DEV, 1, d))], axis=1))
        self.s_rows = jnp.pad(act.reshape(N_DEV * N_SEG, d), ((0, MOD_ROWS - N_DEV * N_SEG), (0, 0)))
        cols = jnp.stack([_mm(self.s_rows, self.w["w_mod"][l], name=f"mod_cols_{l}") for l in range(self.depth)])
        names = self._group(0)
        work = self._gather(names, 0)
        both = _gather_work(work.inputs + [cols.reshape(self.depth * MOD_ROWS, -1)], self._forms(names) + ["last"])
        outs = _run_hosted(both, "gather_l0")
        self._gathered(names, 0, outs[:-1])
        mods = lax.dynamic_slice_in_dim(outs[-1].reshape(self.depth, MOD_ROWS, -1), N_SEG * self.me, N_SEG, axis=1)
        self.mods = mods + self.w["b_mod"][:, None, :]

    def mod(self, i):
        return self.mods[i]

    def weights(self, i):
        wl = {name: self.full[name, i] for name in self._group(i)}
        if i % 2 == 1:
            for name in ("mla_g_kva", "mla_g_q", "mla_g_k"):
                wl[name] = self.w[name][i // 2]
        return wl

    def fwd_slots(self, i):
        plan = FWD_PLAN[i % 2] if i + 1 < self.depth else {}
        names = {slot: self._expand(plan[slot], i + 1) for slot in plan}
        return _Slots(lambda slot: self._gather(names[slot], i + 1) if slot in names else None,
                      lambda slot, outs: self._gathered(names[slot], i + 1, outs))

    def ffn2_grads(self, i, gbuf):
        if i == 0:
            for name in FFN_NAMES:
                self.gviews[name + "#1", 0] = _view(name, gbuf[name][1:2], swapped=True)

    def grads(self, i, gl):
        for name in self._group(i):
            g = gl[name][:, None] if name == "mla_g_qa" else gl[name]
            if i == 0 and name in FFN_NAMES:
                self.gviews[name + "#0", 0] = _view(name, g[0:1], swapped=True)
            else:
                self.gviews[name, i] = _view(name, g, swapped=name in HIDDEN_MAJOR)
        for name in REPLICATED:
            if name in gl:
                self.rep[name, i // 2] = gl[name]

    def dmod(self, i, dmod8):
        self.dmods[i] = dmod8

    def _dmod_gather(self, i):
        return _gather_work([self.dmods[i][None]], ["mid"])

    def _dmod_gathered(self, i, outs):
        n = self.w["w_mod"].shape[2]
        rows = outs[0].reshape(N_DEV, 8, -1)[:, :N_SEG]
        mine = lax.dynamic_slice_in_dim(rows, n * self.me, n, axis=2)
        flat = jnp.pad(mine.reshape(N_DEV * N_SEG, n), ((0, MOD_ROWS - N_DEV * N_SEG), (0, 0)))
        self.reduced["w_mod", i] = _mm(self.s_rows, flat, ta=True, name=f"dwmod_{i}")
        ctx_rows = jnp.pad(jnp.sum(mine[:, N_SEG - 1], axis=0, keepdims=True), ((0, 7), (0, 0)))
        self.ctx_pre = self.ctx_pre + _mm(ctx_rows, self.w["w_mod"][i], tb=True, name=f"dcond_{i}")[0]

    def _forms(self, names):
        return [EXCHANGE[n.split("#")[0]][0] for n in names]

    def _partials(self, names, i, from_sibling):
        for name, recv in zip(names, from_sibling):
            self.parts[name, i] = _chip_partials(self.gviews[name, i], recv, self.core, self._forms([name])[0],
                                                 f"partial_{name.replace('#', '_')}_{i}")

    def _finals(self, names, i, from_chips):
        for name, recv in zip(names, from_chips):
            rv = _reduce_final(self.parts[name, i], recv, self.chip, self._forms([name])[0],
                               f"final_{name.replace('#', '_')}_{i}")
            base = name.split("#")[0]
            shape = self._shapes(base, i)[0]
            arr = _unview(base, rv, (1,) + shape[1:] if "#" in name else shape)
            self.reduced[name, i] = arr[:, 0] if name == "mla_g_qa" else arr

    def bwd_slots(self, i):
        if i + 1 >= self.depth:
            return _Slots(lambda slot: None, None)
        plan = BWD_PLAN[i % 2]
        chips = {slot: (self._expand(plan[slot], i + 1), i + 1) for slot in plan}
        sibling = {slot: (self._expand(SIBLING_PLAN[slot], i + 1), i + 1) for slot in SIBLING_PLAN}
        if i == 0:
            sibling["mix_d"] = ([name + "#1" for name in FFN_NAMES], 0)
            chips["f1_dw2"] = (["ffn_w1#1"], 0)
            chips["f1_dnx"] = (["ffn_w3#1", "ffn_w2#1"], 0)

        def get(slot):
            if slot in sibling:
                names, group = sibling[slot]
                return _sibling_work([self.gviews[n, group] for n in names], self._forms(names))
            if slot in chips:
                names, group = chips[slot]
                return _chip_work([self.parts[n, group] for n in names], self._forms(names))
            if slot == DMOD_SLOT[i % 2]:
                return self._dmod_gather(i + 1)
            return None

        def put(slot, outs):
            if slot in sibling:
                self._partials(*sibling[slot], outs)
            elif slot in chips:
                self._finals(*chips[slot], outs)
            else:
                self._dmod_gathered(i + 1, outs)

        return _Slots(get, put)

    def finish(self, rep_grads):
        group = [name + "#0" if name in FFN_NAMES else name for name in self._group(0)]
        self._dmod_gathered(0, _run_hosted(self._dmod_gather(0), "gather_dmod_l0"))
        rep_grads["c_ctx"] = self.ctx_pre
        for name in REPLICATED:
            if name not in rep_grads:
                rep_grads[name] = jnp.stack([self.rep[name, j] for j in range(self.w[name].shape[0])])
        rep = _pack_replicated([rep_grads[name] for name in REPLICATED])
        views = [self.gviews[n, 0] for n in group] + [jnp.tile(rep[None], (1, N_DEV, 1))]
        forms = self._forms(group) + ["mid"]
        from_sibling = _run_hosted(_sibling_work(views, forms), "reduce_sibling_l0")
        self._partials(group, 0, from_sibling[:-1])
        rep_part = _chip_partials(views[-1], from_sibling[-1], self.core, "mid", "partial_replicated")
        parts = [self.parts[n, 0] for n in group] + [rep_part]
        from_chips = _run_hosted(_chip_work(parts, forms), "reduce_chips_l0")
        self._finals(group, 0, from_chips[:-1])
        for name in FFN_NAMES:
            self.reduced[name, 0] = jnp.concatenate([self.reduced[name + "#0", 0], self.reduced[name + "#1", 0]], axis=0)
        rep_sum = _reduce_final(rep_part, from_chips[-1], self.chip, "mid", "final_replicated")
        out = dict(zip(REPLICATED, _unpack_replicated(rep_sum, [self.w[name].shape for name in REPLICATED])))
        sg = jax.nn.sigmoid(self.c_ctx)
        out["c_ctx"] = out["c_ctx"] * (sg * (1.0 + self.c_ctx * (1.0 - sg)))
        for name in EXCHANGE:
            layers = range(self.w[name].shape[0])
            step = 2 if name.startswith(("sc_", "mla_")) else 1
            first = 1 if name.startswith("mla_") else 0
            out[name] = jnp.stack([self.reduced[name, first + step * l] for l in layers])
        return out


def kernel(x, c, ctx, c_ctx, w_mod, b_mod, g_norm, ffn_w1, ffn_w3, ffn_w2, sc_w_in, sc_conv, sc_w_out, mla_w_a, mla_g_qa, mla_w_uq, mla_g_kva, mla_w_ukv, mla_g_q, mla_g_k, mla_w_o, loss_target, m_c_ctx, m_w_mod, m_b_mod, m_g_norm, m_ffn_w1, m_ffn_w3, m_ffn_w2, m_sc_w_in, m_sc_conv, m_sc_w_out, m_mla_w_a, m_mla_g_qa, m_mla_w_uq, m_mla_g_kva, m_mla_w_ukv, m_mla_g_q, m_mla_g_k, m_mla_w_o, v_c_ctx, v_w_mod, v_b_mod, v_g_norm, v_ffn_w1, v_ffn_w3, v_ffn_w2, v_sc_w_in, v_sc_conv, v_sc_w_out, v_mla_w_a, v_mla_g_qa, v_mla_w_uq, v_mla_g_kva, v_mla_w_ukv, v_mla_g_q, v_mla_g_k, v_mla_w_o):
    w = dict(c_ctx=c_ctx, w_mod=w_mod, b_mod=b_mod, g_norm=g_norm, ffn_w1=ffn_w1, ffn_w3=ffn_w3, ffn_w2=ffn_w2,
             sc_w_in=sc_w_in, sc_conv=sc_conv, sc_w_out=sc_w_out, mla_w_a=mla_w_a, mla_g_qa=mla_g_qa, mla_w_uq=mla_w_uq,
             mla_g_kva=mla_g_kva, mla_w_ukv=mla_w_ukv, mla_g_q=mla_g_q, mla_g_k=mla_g_k, mla_w_o=mla_w_o)
    m = dict(c_ctx=m_c_ctx, w_mod=m_w_mod, b_mod=m_b_mod, g_norm=m_g_norm, ffn_w1=m_ffn_w1, ffn_w3=m_ffn_w3,
             ffn_w2=m_ffn_w2, sc_w_in=m_sc_w_in, sc_conv=m_sc_conv, sc_w_out=m_sc_w_out, mla_w_a=m_mla_w_a,
             mla_g_qa=m_mla_g_qa, mla_w_uq=m_mla_w_uq, mla_g_kva=m_mla_g_kva, mla_w_ukv=m_mla_w_ukv, mla_g_q=m_mla_g_q,
             mla_g_k=m_mla_g_k, mla_w_o=m_mla_w_o)
    v = dict(c_ctx=v_c_ctx, w_mod=v_w_mod, b_mod=v_b_mod, g_norm=v_g_norm, ffn_w1=v_ffn_w1, ffn_w3=v_ffn_w3,
             ffn_w2=v_ffn_w2, sc_w_in=v_sc_w_in, sc_conv=v_sc_conv, sc_w_out=v_sc_w_out, mla_w_a=v_mla_w_a,
             mla_g_qa=v_mla_g_qa, mla_w_uq=v_mla_w_uq, mla_g_kva=v_mla_g_kva, mla_w_ukv=v_mla_w_ukv, mla_g_q=v_mla_g_q,
             mla_g_k=v_mla_g_k, mla_w_o=v_mla_w_o)
    exchange = _Exchange(w)
    exchange.prefetch(c)
    loss, grad_x, g_b_mod = _local_step(exchange, x, c, ctx, loss_target)
    loss = lax.psum(loss, ("x", "y", "c"))
    reduced = exchange.finish(dict(b_mod=g_b_mod))

    outs = [[], [], [], []]
    for name in WEIGHTS:
        delta, new_m, new_v = _adamw(w[name], reduced[name], m[name], v[name], f"adamw_{name}")
        for lst, val in zip(outs, (reduced[name], delta, new_m, new_v)):
            lst.append(val)
    return (loss, grad_x, *outs[0], *outs[1], *outs[2], *outs[3])
```
